```python
import jax, jax.numpy as jnp
from jax import lax
import numpy as np

D_MODEL = 2048
BATCH = 8
SEQ = 2048
DEPTH = 2

N_MEM = 256
MIX_WIDTH = D_MODEL
GM_WIDTH = MIX_WIDTH // 2
GM_HEAD_DIM = 128
GM_HEADS = GM_WIDTH // GM_HEAD_DIM
GM_CHUNK = 128
DN_WIDTH = MIX_WIDTH - GM_WIDTH
DN_HEAD_DIM = 128
DN_HEADS = DN_WIDTH // DN_HEAD_DIM
DN_CHUNK = 64
CONV_WIDTH = 4
XA_HEADS = 4
XA_HEAD_DIM = D_MODEL // XA_HEADS
D_FF = 5632
NORM_EPS = 1e-6
IN_COLS = 2 * GM_WIDTH + 4 * DN_WIDTH + 2 * DN_HEADS

kernel_name = "hybrid_sgu_deltanet_macaron_block"


def rmsnorm(x, g):
    xf = x.astype(jnp.float32)
    y = xf * lax.rsqrt(jnp.mean(xf * xf, axis=-1, keepdims=True) + NORM_EPS)
    return (y * g.astype(jnp.float32)).astype(x.dtype)


def l2norm(x):
    xf = x.astype(jnp.float32)
    return xf * lax.rsqrt(jnp.sum(xf * xf, axis=-1, keepdims=True) + NORM_EPS)


def swiglu_ffn(h, w_gate_up, w_down):
    gate, up = jnp.split(h @ w_gate_up, 2, axis=-1)
    return (jax.nn.silu(gate) * up) @ w_down


def causal_short_conv(x, w):
    k_width = w.shape[0]
    t_len = x.shape[1]
    xp = jnp.pad(x, ((0, 0), (k_width - 1, 0), (0, 0)))
    out = xp[:, 0:t_len] * w[0]
    for i in range(1, k_width):
        out = out + xp[:, i:i + t_len] * w[i]
    return out


def chunk_spatial_gating(u, v, sm_w, sm_b, ln_g, ln_b):
    b_, t_len, _ = u.shape
    n_chunks = t_len // GM_CHUNK
    u = jax.nn.gelu(u, approximate=False)
    vf = jax.nn.gelu(v, approximate=False).astype(jnp.float32)
    mu = jnp.mean(vf, axis=-1, keepdims=True)
    var = jnp.mean(jnp.square(vf - mu), axis=-1, keepdims=True)
    vn = ((vf - mu) * lax.rsqrt(var + NORM_EPS) * ln_g + ln_b).astype(u.dtype)
    vc = vn.reshape(b_, n_chunks, GM_CHUNK, GM_HEADS, GM_HEAD_DIM)
    causal = jnp.tril(jnp.ones((GM_CHUNK, GM_CHUNK), dtype=bool))
    w_masked = jnp.where(causal, sm_w, 0)
    mixed = jnp.einsum('hts,bnshd->bnthd', w_masked, vc) + sm_b.T[None, None, :, :, None]
    uc = u.reshape(b_, n_chunks, GM_CHUNK, GM_HEADS, GM_HEAD_DIM)
    return (uc * mixed).reshape(b_, t_len, GM_WIDTH)


def gated_delta_rule(q, k, v, g, beta):
    b_, t_len, h_, dk = q.shape
    dv = v.shape[-1]
    n_chunks = t_len // DN_CHUNK
    c = DN_CHUNK

    def to_chunks(a):
        return a.reshape(b_, n_chunks, c, h_, -1).transpose(1, 0, 3, 2, 4)

    qc = to_chunks(l2norm(q) * (dk ** -0.5))
    kc = to_chunks(l2norm(k))
    vc = to_chunks(v.astype(jnp.float32))
    gc = g.astype(jnp.float32).reshape(b_, n_chunks, c, h_).transpose(1, 0, 3, 2)
    bc = beta.astype(jnp.float32).reshape(b_, n_chunks, c, h_).transpose(1, 0, 3, 2)
    gcum = jnp.cumsum(gc, axis=-1)

    causal = jnp.tril(jnp.ones((c, c), dtype=bool))
    strict = jnp.tril(jnp.ones((c, c), dtype=bool), k=-1)
    decay = jnp.exp(jnp.where(causal, gcum[..., :, None] - gcum[..., None, :], -jnp.inf))

    k_beta = kc * bc[..., None]
    kkt = jnp.einsum('nbhtd,nbhsd->nbhts', k_beta, kc) * decay
    a_mat = jnp.eye(c, dtype=jnp.float32) + jnp.where(strict, kkt, 0.0)
    rhs = jnp.concatenate([vc * bc[..., None], k_beta * jnp.exp(gcum)[..., None]], axis=-1)
    sol = lax.linalg.triangular_solve(a_mat, rhs, left_side=True, lower=True, unit_diagonal=True)
    u_c, w_c = sol[..., :dv], sol[..., dv:]
    qk_intra = jnp.where(causal, jnp.einsum('nbhtd,nbhsd->nbhts', qc, kc) * decay, 0.0)

    def chunk_step(state, inp):
        q_i, k_i, u_i, w_i, g_i, a_i = inp
        v_new = u_i - jnp.einsum('bhck,bhkv->bhcv', w_i, state)
        o_i = (jnp.einsum('bhck,bhkv->bhcv', q_i * jnp.exp(g_i)[..., None], state)
               + jnp.einsum('bhts,bhsv->bhtv', a_i, v_new))
        g_last = g_i[..., -1]
        k_dec = k_i * jnp.exp(g_last[..., None] - g_i)[..., None]
        state = state * jnp.exp(g_last)[..., None, None] + jnp.einsum('bhck,bhcv->bhkv', k_dec, v_new)
        return state, o_i

    s0 = jnp.zeros((b_, h_, dk, dv), jnp.float32)
    _, o = lax.scan(chunk_step, s0, (qc, kc, u_c, w_c, gcum, qk_intra))
    return o.transpose(1, 0, 3, 2, 4).reshape(b_, t_len, h_, dv)


def token_mix(h, w_in, conv_w, a_log, dt_bias, sm_w, sm_b, sm_ln_g, sm_ln_b, dn_norm_w, w_out):
    b_, t_len, _ = h.shape
    proj = h @ w_in
    splits = [GM_WIDTH, 2 * GM_WIDTH, 2 * GM_WIDTH + 3 * DN_WIDTH,
              2 * GM_WIDTH + 4 * DN_WIDTH, 2 * GM_WIDTH + 4 * DN_WIDTH + DN_HEADS]
    u_a, v_a, qkv, z, b_raw, a_raw = jnp.split(proj, splits, axis=-1)

    y_a = chunk_spatial_gating(u_a, v_a, sm_w, sm_b, sm_ln_g, sm_ln_b)

    qkv = jax.nn.silu(causal_short_conv(qkv, conv_w))
    q, k, v = jnp.split(qkv, 3, axis=-1)
    q = q.reshape(b_, t_len, DN_HEADS, DN_HEAD_DIM)
    k = k.reshape(b_, t_len, DN_HEADS, DN_HEAD_DIM)
    v = v.reshape(b_, t_len, DN_HEADS, DN_HEAD_DIM)
    beta = jax.nn.sigmoid(b_raw.astype(jnp.float32))
    g = -jnp.exp(a_log.astype(jnp.float32)) * jax.nn.softplus(a_raw.astype(jnp.float32) + dt_bias.astype(jnp.float32))
    o = gated_delta_rule(q, k, v, g, beta)
    zf = z.astype(jnp.float32).reshape(b_, t_len, DN_HEADS, DN_HEAD_DIM)
    o = rmsnorm(o, dn_norm_w) * jax.nn.silu(zf)
    y_b = o.reshape(b_, t_len, DN_WIDTH).astype(h.dtype)

    return jnp.concatenate([y_a, y_b], axis=-1) @ w_out


def cross_attend(h, mem_h, w_xq, w_xkv, w_xo):
    b_, t_len, _ = h.shape
    m_len = mem_h.shape[1]
    q = (h @ w_xq).reshape(b_, t_len, XA_HEADS, XA_HEAD_DIM)
    kv = (mem_h @ w_xkv).reshape(b_, m_len, 2, XA_HEADS, XA_HEAD_DIM)
    k, v = kv[:, :, 0], kv[:, :, 1]
    s = jnp.einsum('bthd,bmhd->bhtm', q, k).astype(jnp.float32) * (XA_HEAD_DIM ** -0.5)
    p = jax.nn.softmax(s, axis=-1).astype(v.dtype)
    o = jnp.einsum('bhtm,bmhd->bthd', p, v).reshape(b_, t_len, XA_HEADS * XA_HEAD_DIM)
    return o @ w_xo


def _fwd_setup_inputs(seed: int = 0) -> dict:
    key = jax.random.key(seed)
    ks = iter(jax.random.split(key, 64))
    L = DEPTH

    def dense(shape, fan_in):
        return jax.random.normal(next(ks), shape, jnp.float32) * (fan_in ** -0.5)

    def gain(shape):
        return 1.0 + 0.05 * jax.random.normal(next(ks), shape, jnp.float32)

    def small(shape):
        return 0.02 * jax.random.normal(next(ks), shape, jnp.float32)

    x = jax.random.normal(next(ks), (BATCH, SEQ, D_MODEL), jnp.float32)
    mem = jax.random.normal(next(ks), (BATCH, N_MEM, D_MODEL), jnp.float32)
    a_log = jnp.log(jax.random.uniform(next(ks), (L, DN_HEADS), jnp.float32, minval=1.0, maxval=16.0))
    dt = jnp.exp(jax.random.uniform(next(ks), (L, DN_HEADS), jnp.float32,
                                    minval=float(np.log(1e-3)), maxval=float(np.log(1e-1))))
    dt_bias = dt + jnp.log(-jnp.expm1(-dt))
    return {
        "x": x,
        "mem": mem,
        "ffn1_norm_pre": gain((L, D_MODEL)),
        "ffn1_w_gate_up": dense((L, D_MODEL, 2 * D_FF), D_MODEL),
        "ffn1_w_down": dense((L, D_FF, D_MODEL), D_FF),
        "ffn1_norm_post": gain((L, D_MODEL)),
        "mix_norm_pre": gain((L, D_MODEL)),
        "w_in": dense((L, D_MODEL, IN_COLS), D_MODEL),
        "conv_w": dense((L, CONV_WIDTH, 3 * DN_WIDTH), CONV_WIDTH),
        "a_log": a_log,
        "dt_bias": dt_bias,
        "sm_w": dense((L, GM_HEADS, GM_CHUNK, GM_CHUNK), GM_CHUNK),
        "sm_b": gain((L, GM_HEADS, GM_CHUNK)),
        "sm_ln_g": gain((L, GM_WIDTH)),
        "sm_ln_b": small((L, GM_WIDTH)),
        "dn_norm_w": gain((L, DN_HEAD_DIM)),
        "w_out": dense((L, MIX_WIDTH, D_MODEL), MIX_WIDTH),
        "mix_norm_post": gain((L, D_MODEL)),
        "xa_norm_pre": gain((L, D_MODEL)),
        "mem_norm": gain((L, D_MODEL)),
        "w_xq": dense((L, D_MODEL, D_MODEL), D_MODEL),
        "w_xkv": dense((L, D_MODEL, 2 * D_MODEL), D_MODEL),
        "w_xo": dense((L, D_MODEL, D_MODEL), D_MODEL),
        "xa_norm_post": gain((L, D_MODEL)),
        "ffn2_norm_pre": gain((L, D_MODEL)),
        "ffn2_w_gate_up": dense((L, D_MODEL, 2 * D_FF), D_MODEL),
        "ffn2_w_down": dense((L, D_FF, D_MODEL), D_FF),
        "ffn2_norm_post": gain((L, D_MODEL)),
    }


def _fwd_reference(x, mem, ffn1_norm_pre, ffn1_w_gate_up, ffn1_w_down, ffn1_norm_post,
              mix_norm_pre, w_in, conv_w, a_log, dt_bias, sm_w, sm_b, sm_ln_g, sm_ln_b,
              dn_norm_w, w_out, mix_norm_post, xa_norm_pre, mem_norm, w_xq, w_xkv, w_xo,
              xa_norm_post, ffn2_norm_pre, ffn2_w_gate_up, ffn2_w_down, ffn2_norm_post):
    for l in range(DEPTH):
        f = swiglu_ffn(rmsnorm(x, ffn1_norm_pre[l]), ffn1_w_gate_up[l], ffn1_w_down[l])
        x = x + 0.5 * rmsnorm(f, ffn1_norm_post[l])
        m = token_mix(rmsnorm(x, mix_norm_pre[l]), w_in[l], conv_w[l], a_log[l], dt_bias[l],
                      sm_w[l], sm_b[l], sm_ln_g[l], sm_ln_b[l], dn_norm_w[l], w_out[l])
        x = x + rmsnorm(m, mix_norm_post[l])
        c = cross_attend(rmsnorm(x, xa_norm_pre[l]), rmsnorm(mem, mem_norm[l]), w_xq[l], w_xkv[l], w_xo[l])
        x = x + rmsnorm(c, xa_norm_post[l])
        f = swiglu_ffn(rmsnorm(x, ffn2_norm_pre[l]), ffn2_w_gate_up[l], ffn2_w_down[l])
        x = x + 0.5 * rmsnorm(f, ffn2_norm_post[l])
    return x


import jax as _jax
import jax.numpy as _jnp

TWIN_FORMAT = 'train_step'
FWD_PARAMS = ['x', 'mem', 'ffn1_norm_pre', 'ffn1_w_gate_up', 'ffn1_w_down', 'ffn1_norm_post', 'mix_norm_pre', 'w_in', 'conv_w', 'a_log', 'dt_bias', 'sm_w', 'sm_b', 'sm_ln_g', 'sm_ln_b', 'dn_norm_w', 'w_out', 'mix_norm_post', 'xa_norm_pre', 'mem_norm', 'w_xq', 'w_xkv', 'w_xo', 'xa_norm_post', 'ffn2_norm_pre', 'ffn2_w_gate_up', 'ffn2_w_down', 'ffn2_norm_post']
TWIN_WEIGHTS = ['ffn1_norm_pre', 'ffn1_w_gate_up', 'ffn1_w_down', 'ffn1_norm_post', 'mix_norm_pre', 'w_in', 'conv_w', 'a_log', 'dt_bias', 'sm_w', 'sm_b', 'sm_ln_g', 'sm_ln_b', 'dn_norm_w', 'w_out', 'mix_norm_post', 'xa_norm_pre', 'mem_norm', 'w_xq', 'w_xkv', 'w_xo', 'xa_norm_post', 'ffn2_norm_pre', 'ffn2_w_gate_up', 'ffn2_w_down', 'ffn2_norm_post']
TWIN_DIFF_INPUT = 'x'
TWIN_INPUTS = ['x', 'mem', 'ffn1_norm_pre', 'ffn1_w_gate_up', 'ffn1_w_down', 'ffn1_norm_post', 'mix_norm_pre', 'w_in', 'conv_w', 'a_log', 'dt_bias', 'sm_w', 'sm_b', 'sm_ln_g', 'sm_ln_b', 'dn_norm_w', 'w_out', 'mix_norm_post', 'xa_norm_pre', 'mem_norm', 'w_xq', 'w_xkv', 'w_xo', 'xa_norm_post', 'ffn2_norm_pre', 'ffn2_w_gate_up', 'ffn2_w_down', 'ffn2_norm_post', 'loss_target', 'm_ffn1_norm_pre', 'm_ffn1_w_gate_up', 'm_ffn1_w_down', 'm_ffn1_norm_post', 'm_mix_norm_pre', 'm_w_in', 'm_conv_w', 'm_a_log', 'm_dt_bias', 'm_sm_w', 'm_sm_b', 'm_sm_ln_g', 'm_sm_ln_b', 'm_dn_norm_w', 'm_w_out', 'm_mix_norm_post', 'm_xa_norm_pre', 'm_mem_norm', 'm_w_xq', 'm_w_xkv', 'm_w_xo', 'm_xa_norm_post', 'm_ffn2_norm_pre', 'm_ffn2_w_gate_up', 'm_ffn2_w_down', 'm_ffn2_norm_post', 'v_ffn1_norm_pre', 'v_ffn1_w_gate_up', 'v_ffn1_w_down', 'v_ffn1_norm_post', 'v_mix_norm_pre', 'v_w_in', 'v_conv_w', 'v_a_log', 'v_dt_bias', 'v_sm_w', 'v_sm_b', 'v_sm_ln_g', 'v_sm_ln_b', 'v_dn_norm_w', 'v_w_out', 'v_mix_norm_post', 'v_xa_norm_pre', 'v_mem_norm', 'v_w_xq', 'v_w_xkv', 'v_w_xo', 'v_xa_norm_post', 'v_ffn2_norm_pre', 'v_ffn2_w_gate_up', 'v_ffn2_w_down', 'v_ffn2_norm_post']
TWIN_OUTPUTS = ['loss', 'grad_x', 'grad_ffn1_norm_pre', 'grad_ffn1_w_gate_up', 'grad_ffn1_w_down', 'grad_ffn1_norm_post', 'grad_mix_norm_pre', 'grad_w_in', 'grad_conv_w', 'grad_a_log', 'grad_dt_bias', 'grad_sm_w', 'grad_sm_b', 'grad_sm_ln_g', 'grad_sm_ln_b', 'grad_dn_norm_w', 'grad_w_out', 'grad_mix_norm_post', 'grad_xa_norm_pre', 'grad_mem_norm', 'grad_w_xq', 'grad_w_xkv', 'grad_w_xo', 'grad_xa_norm_post', 'grad_ffn2_norm_pre', 'grad_ffn2_w_gate_up', 'grad_ffn2_w_down', 'grad_ffn2_norm_post', 'delta_ffn1_norm_pre', 'delta_ffn1_w_gate_up', 'delta_ffn1_w_down', 'delta_ffn1_norm_post', 'delta_mix_norm_pre', 'delta_w_in', 'delta_conv_w', 'delta_a_log', 'delta_dt_bias', 'delta_sm_w', 'delta_sm_b', 'delta_sm_ln_g', 'delta_sm_ln_b', 'delta_dn_norm_w', 'delta_w_out', 'delta_mix_norm_post', 'delta_xa_norm_pre', 'delta_mem_norm', 'delta_w_xq', 'delta_w_xkv', 'delta_w_xo', 'delta_xa_norm_post', 'delta_ffn2_norm_pre', 'delta_ffn2_w_gate_up', 'delta_ffn2_w_down', 'delta_ffn2_norm_post', 'new_m_ffn1_norm_pre', 'new_m_ffn1_w_gate_up', 'new_m_ffn1_w_down', 'new_m_ffn1_norm_post', 'new_m_mix_norm_pre', 'new_m_w_in', 'new_m_conv_w', 'new_m_a_log', 'new_m_dt_bias', 'new_m_sm_w', 'new_m_sm_b', 'new_m_sm_ln_g', 'new_m_sm_ln_b', 'new_m_dn_norm_w', 'new_m_w_out', 'new_m_mix_norm_post', 'new_m_xa_norm_pre', 'new_m_mem_norm', 'new_m_w_xq', 'new_m_w_xkv', 'new_m_w_xo', 'new_m_xa_norm_post', 'new_m_ffn2_norm_pre', 'new_m_ffn2_w_gate_up', 'new_m_ffn2_w_down', 'new_m_ffn2_norm_post', 'new_v_ffn1_norm_pre', 'new_v_ffn1_w_gate_up', 'new_v_ffn1_w_down', 'new_v_ffn1_norm_post', 'new_v_mix_norm_pre', 'new_v_w_in', 'new_v_conv_w', 'new_v_a_log', 'new_v_dt_bias', 'new_v_sm_w', 'new_v_sm_b', 'new_v_sm_ln_g', 'new_v_sm_ln_b', 'new_v_dn_norm_w', 'new_v_w_out', 'new_v_mix_norm_post', 'new_v_xa_norm_pre', 'new_v_mem_norm', 'new_v_w_xq', 'new_v_w_xkv', 'new_v_w_xo', 'new_v_xa_norm_post', 'new_v_ffn2_norm_pre', 'new_v_ffn2_w_gate_up', 'new_v_ffn2_w_down', 'new_v_ffn2_norm_post']
TWIN_LEAF_KINDS = {'loss': 'loss', 'grad_x': 'grad_x', 'grad_ffn1_norm_pre': 'grad_w', 'grad_ffn1_w_gate_up': 'grad_w', 'grad_ffn1_w_down': 'grad_w', 'grad_ffn1_norm_post': 'grad_w', 'grad_mix_norm_pre': 'grad_w', 'grad_w_in': 'grad_w', 'grad_conv_w': 'grad_w', 'grad_a_log': 'grad_w', 'grad_dt_bias': 'grad_w', 'grad_sm_w': 'grad_w', 'grad_sm_b': 'grad_w', 'grad_sm_ln_g': 'grad_w', 'grad_sm_ln_b': 'grad_w', 'grad_dn_norm_w': 'grad_w', 'grad_w_out': 'grad_w', 'grad_mix_norm_post': 'grad_w', 'grad_xa_norm_pre': 'grad_w', 'grad_mem_norm': 'grad_w', 'grad_w_xq': 'grad_w', 'grad_w_xkv': 'grad_w', 'grad_w_xo': 'grad_w', 'grad_xa_norm_post': 'grad_w', 'grad_ffn2_norm_pre': 'grad_w', 'grad_ffn2_w_gate_up': 'grad_w', 'grad_ffn2_w_down': 'grad_w', 'grad_ffn2_norm_post': 'grad_w', 'delta_ffn1_norm_pre': 'delta_w', 'delta_ffn1_w_gate_up': 'delta_w', 'delta_ffn1_w_down': 'delta_w', 'delta_ffn1_norm_post': 'delta_w', 'delta_mix_norm_pre': 'delta_w', 'delta_w_in': 'delta_w', 'delta_conv_w': 'delta_w', 'delta_a_log': 'delta_w', 'delta_dt_bias': 'delta_w', 'delta_sm_w': 'delta_w', 'delta_sm_b': 'delta_w', 'delta_sm_ln_g': 'delta_w', 'delta_sm_ln_b': 'delta_w', 'delta_dn_norm_w': 'delta_w', 'delta_w_out': 'delta_w', 'delta_mix_norm_post': 'delta_w', 'delta_xa_norm_pre': 'delta_w', 'delta_mem_norm': 'delta_w', 'delta_w_xq': 'delta_w', 'delta_w_xkv': 'delta_w', 'delta_w_xo': 'delta_w', 'delta_xa_norm_post': 'delta_w', 'delta_ffn2_norm_pre': 'delta_w', 'delta_ffn2_w_gate_up': 'delta_w', 'delta_ffn2_w_down': 'delta_w', 'delta_ffn2_norm_post': 'delta_w', 'new_m_ffn1_norm_pre': 'new_m', 'new_m_ffn1_w_gate_up': 'new_m', 'new_m_ffn1_w_down': 'new_m', 'new_m_ffn1_norm_post': 'new_m', 'new_m_mix_norm_pre': 'new_m', 'new_m_w_in': 'new_m', 'new_m_conv_w': 'new_m', 'new_m_a_log': 'new_m', 'new_m_dt_bias': 'new_m', 'new_m_sm_w': 'new_m', 'new_m_sm_b': 'new_m', 'new_m_sm_ln_g': 'new_m', 'new_m_sm_ln_b': 'new_m', 'new_m_dn_norm_w': 'new_m', 'new_m_w_out': 'new_m', 'new_m_mix_norm_post': 'new_m', 'new_m_xa_norm_pre': 'new_m', 'new_m_mem_norm': 'new_m', 'new_m_w_xq': 'new_m', 'new_m_w_xkv': 'new_m', 'new_m_w_xo': 'new_m', 'new_m_xa_norm_post': 'new_m', 'new_m_ffn2_norm_pre': 'new_m', 'new_m_ffn2_w_gate_up': 'new_m', 'new_m_ffn2_w_down': 'new_m', 'new_m_ffn2_norm_post': 'new_m', 'new_v_ffn1_norm_pre': 'new_v', 'new_v_ffn1_w_gate_up': 'new_v', 'new_v_ffn1_w_down': 'new_v', 'new_v_ffn1_norm_post': 'new_v', 'new_v_mix_norm_pre': 'new_v', 'new_v_w_in': 'new_v', 'new_v_conv_w': 'new_v', 'new_v_a_log': 'new_v', 'new_v_dt_bias': 'new_v', 'new_v_sm_w': 'new_v', 'new_v_sm_b': 'new_v', 'new_v_sm_ln_g': 'new_v', 'new_v_sm_ln_b': 'new_v', 'new_v_dn_norm_w': 'new_v', 'new_v_w_out': 'new_v', 'new_v_mix_norm_post': 'new_v', 'new_v_xa_norm_pre': 'new_v', 'new_v_mem_norm': 'new_v', 'new_v_w_xq': 'new_v', 'new_v_w_xkv': 'new_v', 'new_v_w_xo': 'new_v', 'new_v_xa_norm_post': 'new_v', 'new_v_ffn2_norm_pre': 'new_v', 'new_v_ffn2_w_gate_up': 'new_v', 'new_v_ffn2_w_down': 'new_v', 'new_v_ffn2_norm_post': 'new_v'}


def _forward(args):
    return _fwd_reference(*[args[k] for k in FWD_PARAMS])


def _output_shape():
    out = _jax.eval_shape(lambda: _forward(_fwd_setup_inputs(0)))
    return out.shape, out.dtype

N_MICROBATCH = 1
ADAM_LR = 0.001
ADAM_B1 = 0.9
ADAM_B2 = 0.999
ADAM_EPS = 1e-08
ADAM_WD = 0.01
ADAM_STEP = 10
PER_EXAMPLE_BATCH_AXIS = {'x': 0, 'mem': 0, 'loss_target': 0}
SHARED_INPUTS = []
_WEIGHT_DTYPES = {'ffn1_norm_pre': _jnp.float32, 'ffn1_w_gate_up': _jnp.float32, 'ffn1_w_down': _jnp.float32, 'ffn1_norm_post': _jnp.float32, 'mix_norm_pre': _jnp.float32, 'w_in': _jnp.float32, 'conv_w': _jnp.float32, 'a_log': _jnp.float32, 'dt_bias': _jnp.float32, 'sm_w': _jnp.float32, 'sm_b': _jnp.float32, 'sm_ln_g': _jnp.float32, 'sm_ln_b': _jnp.float32, 'dn_norm_w': _jnp.float32, 'w_out': _jnp.float32, 'mix_norm_post': _jnp.float32, 'xa_norm_pre': _jnp.float32, 'mem_norm': _jnp.float32, 'w_xq': _jnp.float32, 'w_xkv': _jnp.float32, 'w_xo': _jnp.float32, 'xa_norm_post': _jnp.float32, 'ffn2_norm_pre': _jnp.float32, 'ffn2_w_gate_up': _jnp.float32, 'ffn2_w_down': _jnp.float32, 'ffn2_norm_post': _jnp.float32}
MOMENT_SCALE = {'ffn1_norm_pre': 3.843341e-01, 'ffn1_w_gate_up': 1.682957e-01, 'ffn1_w_down': 3.035947e-01, 'ffn1_norm_post': 1.938048e+00, 'mix_norm_pre': 7.174119e-01, 'w_in': 4.104892e-01, 'conv_w': 4.199642e-01, 'a_log': 1.141037e+00, 'dt_bias': 1.117082e+00, 'sm_w': 1.638298e-01, 'sm_b': 2.654553e-01, 'sm_ln_g': 1.643457e-01, 'sm_ln_b': 1.799876e-01, 'dn_norm_w': 2.610845e+00, 'w_out': 1.687108e+00, 'mix_norm_post': 8.305838e+00, 'xa_norm_pre': 6.643606e-01, 'mem_norm': 2.451731e+00, 'w_xq': 6.564080e-01, 'w_xkv': 1.653895e+00, 'w_xo': 2.258140e+00, 'xa_norm_post': 8.631944e+00, 'ffn2_norm_pre': 5.215232e-01, 'ffn2_w_gate_up': 2.241001e-01, 'ffn2_w_down': 4.337476e-01, 'ffn2_norm_post': 2.040706e+00}


def _to_microbatches(a, axis):
    t = _jnp.moveaxis(a, axis, 0)
    t = t.reshape((N_MICROBATCH, t.shape[0] // N_MICROBATCH) + t.shape[1:])
    return _jnp.moveaxis(t, 1, axis + 1)


def setup_inputs(seed: int = 0) -> dict:
    inp = _fwd_setup_inputs(seed)
    key = _jax.random.fold_in(_jax.random.key(seed), 7919)
    shape, _ = _output_shape()
    out = dict(inp)
    out["loss_target"] = _jax.random.normal(_jax.random.fold_in(key, 0), shape, _jnp.float32)
    for i, name in enumerate(TWIN_WEIGHTS):
        w = inp[name].astype(_jnp.float32)
        if MOMENT_SCALE is None:
            s = _jnp.sqrt(_jnp.mean(_jnp.square(w)) + 1e-30)
        else:
            s = MOMENT_SCALE[name]
        km, kv = _jax.random.split(_jax.random.fold_in(key, i + 1))
        out[name] = w
        out["m_" + name] = s * _jax.random.normal(km, w.shape, _jnp.float32)
        out["v_" + name] = (s * s) * _jax.random.uniform(kv, w.shape, _jnp.float32, 0.5, 1.5)
    if N_MICROBATCH > 1:
        for name, axis in PER_EXAMPLE_BATCH_AXIS.items():
            out[name] = _to_microbatches(out[name], axis)
    return {'x': out['x'], 'mem': out['mem'], 'ffn1_norm_pre': out['ffn1_norm_pre'], 'ffn1_w_gate_up': out['ffn1_w_gate_up'], 'ffn1_w_down': out['ffn1_w_down'], 'ffn1_norm_post': out['ffn1_norm_post'], 'mix_norm_pre': out['mix_norm_pre'], 'w_in': out['w_in'], 'conv_w': out['conv_w'], 'a_log': out['a_log'], 'dt_bias': out['dt_bias'], 'sm_w': out['sm_w'], 'sm_b': out['sm_b'], 'sm_ln_g': out['sm_ln_g'], 'sm_ln_b': out['sm_ln_b'], 'dn_norm_w': out['dn_norm_w'], 'w_out': out['w_out'], 'mix_norm_post': out['mix_norm_post'], 'xa_norm_pre': out['xa_norm_pre'], 'mem_norm': out['mem_norm'], 'w_xq': out['w_xq'], 'w_xkv': out['w_xkv'], 'w_xo': out['w_xo'], 'xa_norm_post': out['xa_norm_post'], 'ffn2_norm_pre': out['ffn2_norm_pre'], 'ffn2_w_gate_up': out['ffn2_w_gate_up'], 'ffn2_w_down': out['ffn2_w_down'], 'ffn2_norm_post': out['ffn2_norm_post'], 'loss_target': out['loss_target'], 'm_ffn1_norm_pre': out['m_ffn1_norm_pre'], 'm_ffn1_w_gate_up': out['m_ffn1_w_gate_up'], 'm_ffn1_w_down': out['m_ffn1_w_down'], 'm_ffn1_norm_post': out['m_ffn1_norm_post'], 'm_mix_norm_pre': out['m_mix_norm_pre'], 'm_w_in': out['m_w_in'], 'm_conv_w': out['m_conv_w'], 'm_a_log': out['m_a_log'], 'm_dt_bias': out['m_dt_bias'], 'm_sm_w': out['m_sm_w'], 'm_sm_b': out['m_sm_b'], 'm_sm_ln_g': out['m_sm_ln_g'], 'm_sm_ln_b': out['m_sm_ln_b'], 'm_dn_norm_w': out['m_dn_norm_w'], 'm_w_out': out['m_w_out'], 'm_mix_norm_post': out['m_mix_norm_post'], 'm_xa_norm_pre': out['m_xa_norm_pre'], 'm_mem_norm': out['m_mem_norm'], 'm_w_xq': out['m_w_xq'], 'm_w_xkv': out['m_w_xkv'], 'm_w_xo': out['m_w_xo'], 'm_xa_norm_post': out['m_xa_norm_post'], 'm_ffn2_norm_pre': out['m_ffn2_norm_pre'], 'm_ffn2_w_gate_up': out['m_ffn2_w_gate_up'], 'm_ffn2_w_down': out['m_ffn2_w_down'], 'm_ffn2_norm_post': out['m_ffn2_norm_post'], 'v_ffn1_norm_pre': out['v_ffn1_norm_pre'], 'v_ffn1_w_gate_up': out['v_ffn1_w_gate_up'], 'v_ffn1_w_down': out['v_ffn1_w_down'], 'v_ffn1_norm_post': out['v_ffn1_norm_post'], 'v_mix_norm_pre': out['v_mix_norm_pre'], 'v_w_in': out['v_w_in'], 'v_conv_w': out['v_conv_w'], 'v_a_log': out['v_a_log'], 'v_dt_bias': out['v_dt_bias'], 'v_sm_w': out['v_sm_w'], 'v_sm_b': out['v_sm_b'], 'v_sm_ln_g': out['v_sm_ln_g'], 'v_sm_ln_b': out['v_sm_ln_b'], 'v_dn_norm_w': out['v_dn_norm_w'], 'v_w_out': out['v_w_out'], 'v_mix_norm_post': out['v_mix_norm_post'], 'v_xa_norm_pre': out['v_xa_norm_pre'], 'v_mem_norm': out['v_mem_norm'], 'v_w_xq': out['v_w_xq'], 'v_w_xkv': out['v_w_xkv'], 'v_w_xo': out['v_w_xo'], 'v_xa_norm_post': out['v_xa_norm_post'], 'v_ffn2_norm_pre': out['v_ffn2_norm_pre'], 'v_ffn2_w_gate_up': out['v_ffn2_w_gate_up'], 'v_ffn2_w_down': out['v_ffn2_w_down'], 'v_ffn2_norm_post': out['v_ffn2_norm_post']}


def _loss(weights, diff, rest, loss_target):
    with _jax.named_scope("forward"):
        args = {**rest, TWIN_DIFF_INPUT: diff, **{k: w.astype(_WEIGHT_DTYPES[k]) for k, w in weights.items()}}
        y = _forward(args)
    with _jax.named_scope("loss_head"):
        err = _jnp.square(y.astype(_jnp.float32) - loss_target)
        return 0.5 * _jnp.sum(_jnp.mean(err, axis=-1)) if err.ndim else 0.5 * err


def _adamw(w, g, m, v):
    m = ADAM_B1 * m + (1.0 - ADAM_B1) * g
    v = ADAM_B2 * v + (1.0 - ADAM_B2) * _jnp.square(g)
    m_hat = m / (1.0 - ADAM_B1 ** ADAM_STEP)
    v_hat = v / (1.0 - ADAM_B2 ** ADAM_STEP)
    delta = -ADAM_LR * (m_hat / (_jnp.sqrt(v_hat) + ADAM_EPS) + ADAM_WD * w)
    return delta, m, v


def reference(x, mem, ffn1_norm_pre, ffn1_w_gate_up, ffn1_w_down, ffn1_norm_post, mix_norm_pre, w_in, conv_w, a_log, dt_bias, sm_w, sm_b, sm_ln_g, sm_ln_b, dn_norm_w, w_out, mix_norm_post, xa_norm_pre, mem_norm, w_xq, w_xkv, w_xo, xa_norm_post, ffn2_norm_pre, ffn2_w_gate_up, ffn2_w_down, ffn2_norm_post, loss_target, m_ffn1_norm_pre, m_ffn1_w_gate_up, m_ffn1_w_down, m_ffn1_norm_post, m_mix_norm_pre, m_w_in, m_conv_w, m_a_log, m_dt_bias, m_sm_w, m_sm_b, m_sm_ln_g, m_sm_ln_b, m_dn_norm_w, m_w_out, m_mix_norm_post, m_xa_norm_pre, m_mem_norm, m_w_xq, m_w_xkv, m_w_xo, m_xa_norm_post, m_ffn2_norm_pre, m_ffn2_w_gate_up, m_ffn2_w_down, m_ffn2_norm_post, v_ffn1_norm_pre, v_ffn1_w_gate_up, v_ffn1_w_down, v_ffn1_norm_post, v_mix_norm_pre, v_w_in, v_conv_w, v_a_log, v_dt_bias, v_sm_w, v_sm_b, v_sm_ln_g, v_sm_ln_b, v_dn_norm_w, v_w_out, v_mix_norm_post, v_xa_norm_pre, v_mem_norm, v_w_xq, v_w_xkv, v_w_xo, v_xa_norm_post, v_ffn2_norm_pre, v_ffn2_w_gate_up, v_ffn2_w_down, v_ffn2_norm_post):
    given = dict(x=x, mem=mem, ffn1_norm_pre=ffn1_norm_pre, ffn1_w_gate_up=ffn1_w_gate_up, ffn1_w_down=ffn1_w_down, ffn1_norm_post=ffn1_norm_post, mix_norm_pre=mix_norm_pre, w_in=w_in, conv_w=conv_w, a_log=a_log, dt_bias=dt_bias, sm_w=sm_w, sm_b=sm_b, sm_ln_g=sm_ln_g, sm_ln_b=sm_ln_b, dn_norm_w=dn_norm_w, w_out=w_out, mix_norm_post=mix_norm_post, xa_norm_pre=xa_norm_pre, mem_norm=mem_norm, w_xq=w_xq, w_xkv=w_xkv, w_xo=w_xo, xa_norm_post=xa_norm_post, ffn2_norm_pre=ffn2_norm_pre, ffn2_w_gate_up=ffn2_w_gate_up, ffn2_w_down=ffn2_w_down, ffn2_norm_post=ffn2_norm_post, loss_target=loss_target, m_ffn1_norm_pre=m_ffn1_norm_pre, m_ffn1_w_gate_up=m_ffn1_w_gate_up, m_ffn1_w_down=m_ffn1_w_down, m_ffn1_norm_post=m_ffn1_norm_post, m_mix_norm_pre=m_mix_norm_pre, m_w_in=m_w_in, m_conv_w=m_conv_w, m_a_log=m_a_log, m_dt_bias=m_dt_bias, m_sm_w=m_sm_w, m_sm_b=m_sm_b, m_sm_ln_g=m_sm_ln_g, m_sm_ln_b=m_sm_ln_b, m_dn_norm_w=m_dn_norm_w, m_w_out=m_w_out, m_mix_norm_post=m_mix_norm_post, m_xa_norm_pre=m_xa_norm_pre, m_mem_norm=m_mem_norm, m_w_xq=m_w_xq, m_w_xkv=m_w_xkv, m_w_xo=m_w_xo, m_xa_norm_post=m_xa_norm_post, m_ffn2_norm_pre=m_ffn2_norm_pre, m_ffn2_w_gate_up=m_ffn2_w_gate_up, m_ffn2_w_down=m_ffn2_w_down, m_ffn2_norm_post=m_ffn2_norm_post, v_ffn1_norm_pre=v_ffn1_norm_pre, v_ffn1_w_gate_up=v_ffn1_w_gate_up, v_ffn1_w_down=v_ffn1_w_down, v_ffn1_norm_post=v_ffn1_norm_post, v_mix_norm_pre=v_mix_norm_pre, v_w_in=v_w_in, v_conv_w=v_conv_w, v_a_log=v_a_log, v_dt_bias=v_dt_bias, v_sm_w=v_sm_w, v_sm_b=v_sm_b, v_sm_ln_g=v_sm_ln_g, v_sm_ln_b=v_sm_ln_b, v_dn_norm_w=v_dn_norm_w, v_w_out=v_w_out, v_mix_norm_post=v_mix_norm_post, v_xa_norm_pre=v_xa_norm_pre, v_mem_norm=v_mem_norm, v_w_xq=v_w_xq, v_w_xkv=v_w_xkv, v_w_xo=v_w_xo, v_xa_norm_post=v_xa_norm_post, v_ffn2_norm_pre=v_ffn2_norm_pre, v_ffn2_w_gate_up=v_ffn2_w_gate_up, v_ffn2_w_down=v_ffn2_w_down, v_ffn2_norm_post=v_ffn2_norm_post)
    weights = {n: given[n] for n in TWIN_WEIGHTS}
    shared = {n: given[n] for n in SHARED_INPUTS}
    per_example = {n: given[n] for n in ['x', 'mem']}
    grad_fn = _jax.value_and_grad(_loss, argnums=(0, 1))

    def one_microbatch(ex, loss_target):
        ex = dict(ex)
        diff = ex.pop(TWIN_DIFF_INPUT)
        return grad_fn(weights, diff, {**shared, **ex}, loss_target)

    if N_MICROBATCH == 1:
        loss, (grad_w, grad_x) = one_microbatch(per_example, given["loss_target"])
    else:
        def body(carry, xs):
            loss_sum, grad_sum = carry
            l_k, (gw_k, gx_k) = one_microbatch(xs[0], xs[1])
            with _jax.named_scope("update"):
                return (loss_sum + l_k, _jax.tree.map(_jnp.add, grad_sum, gw_k)), gx_k

        init = (_jnp.zeros((), _jnp.float32), _jax.tree.map(_jnp.zeros_like, weights))
        (loss, grad_w), grad_x = _jax.lax.scan(body, init, (per_example, given["loss_target"]))
    with _jax.named_scope("update"):
        delta_w, new_m, new_v = {}, {}, {}
        for n in TWIN_WEIGHTS:
            delta_w[n], new_m[n], new_v[n] = _adamw(weights[n], grad_w[n], given["m_" + n], given["v_" + n])
    return (loss, grad_x, *[grad_w[n] for n in TWIN_WEIGHTS], *[delta_w[n] for n in TWIN_WEIGHTS],
            *[new_m[n] for n in TWIN_WEIGHTS], *[new_v[n] for n in TWIN_WEIGHTS])
```

```python
import functools
import math

import jax
import jax.numpy as jnp
from jax import lax
from jax.experimental import pallas as pl
from jax.experimental.pallas import tpu as pltpu

F32, BF16 = jnp.float32, jnp.bfloat16
NORM_EPS = 1e-6
HEAD_DIM = 128
GM_CHUNK = 128
DN_CHUNK = 64
CONV_WIDTH = 4
XA_HEADS = 4
LANES = 128
N_CHIPS = 4
N_DEV = 8
VMEM_LIMIT = 56 * 1024 * 1024
MESH_IDS = pl.DeviceIdType.MESH
ADAM_LR, ADAM_B1, ADAM_B2, ADAM_EPS, ADAM_WD, ADAM_STEP = 0.001, 0.9, 0.999, 1e-08, 0.01, 10

WEIGHTS = ['ffn1_norm_pre', 'ffn1_w_gate_up', 'ffn1_w_down', 'ffn1_norm_post', 'mix_norm_pre', 'w_in', 'conv_w',
           'a_log', 'dt_bias', 'sm_w', 'sm_b', 'sm_ln_g', 'sm_ln_b', 'dn_norm_w', 'w_out', 'mix_norm_post',
           'xa_norm_pre', 'mem_norm', 'w_xq', 'w_xkv', 'w_xo', 'xa_norm_post', 'ffn2_norm_pre', 'ffn2_w_gate_up',
           'ffn2_w_down', 'ffn2_norm_post']
BIG = {'ffn1_w_gate_up': 'col', 'ffn1_w_down': 'row', 'w_in': 'col', 'w_out': 'row', 'w_xq': 'row',
       'w_xkv': 'col', 'w_xo': 'row', 'ffn2_w_gate_up': 'col', 'ffn2_w_down': 'row'}
BIG_NAMES = list(BIG)
SMALL_NAMES = [n for n in WEIGHTS if n not in BIG]


def _pcall(body, **kw):
    return pl.pallas_call(body, **kw)


def _params(sem=None):
    return pltpu.CompilerParams(dimension_semantics=sem, vmem_limit_bytes=VMEM_LIMIT)


def _divtile(dim, cap, align=LANES):
    if dim <= cap:
        return dim
    t = (cap // align) * align
    while t > align and dim % t:
        t -= align
    assert dim % t == 0, (dim, cap)
    return t


_DN = {'nn': (((1,), (0,)), ((), ())), 'nt': (((1,), (1,)), ((), ())), 'tn': (((0,), (0,)), ((), ()))}


def _mm(a, b, mode, out_dtype, name):
    if mode == 'nn':
        (m, k), (k2, n) = a.shape, b.shape
    elif mode == 'nt':
        (m, k), (n, k2) = a.shape, b.shape
    else:
        (k, m), (k2, n) = a.shape, b.shape
    assert k == k2, (a.shape, b.shape, mode)
    tm, tn, tk = _divtile(m, 1024), _divtile(n, 1024), _divtile(k, 512)
    nk = k // tk
    a_spec = {'nn': pl.BlockSpec((tm, tk), lambda i, j, kk: (i, kk)),
              'nt': pl.BlockSpec((tm, tk), lambda i, j, kk: (i, kk)),
              'tn': pl.BlockSpec((tk, tm), lambda i, j, kk: (kk, i))}[mode]
    b_spec = {'nn': pl.BlockSpec((tk, tn), lambda i, j, kk: (kk, j)),
              'nt': pl.BlockSpec((tn, tk), lambda i, j, kk: (j, kk)),
              'tn': pl.BlockSpec((tk, tn), lambda i, j, kk: (kk, j))}[mode]

    def body(a_ref, b_ref, o_ref, acc):
        kk = pl.program_id(2)

        @pl.when(kk == 0)
        def _():
            acc[...] = jnp.zeros_like(acc)

        acc[...] += lax.dot_general(a_ref[...].astype(BF16), b_ref[...].astype(BF16), _DN[mode],
                                    preferred_element_type=F32)

        @pl.when(kk == nk - 1)
        def _():
            o_ref[...] = acc[...].astype(o_ref.dtype)

    return _pcall(
        body, name=name, grid=(m // tm, n // tn, nk),
        in_specs=[a_spec, b_spec], out_specs=pl.BlockSpec((tm, tn), lambda i, j, kk: (i, j)),
        out_shape=jax.ShapeDtypeStruct((m, n), out_dtype),
        scratch_shapes=[pltpu.VMEM((tm, tn), F32)],
        compiler_params=_params(("parallel", "parallel", "arbitrary")),
    )(a, b)


def _rowcall(name, f, rows, params=(), row_outs=(), acc_outs=(), br=256, nrows=None):
    rows = [r if isinstance(r, tuple) else (r, 0) for r in rows]
    t = nrows if nrows is not None else rows[0][0].shape[0]
    br = min(br, t)
    assert t % br == 0, (name, t, br)
    n_in, n_ro = len(rows) + len(params), len(row_outs)

    def row_spec(arr, off):
        assert off % br == 0
        return pl.BlockSpec((br, arr.shape[1]), functools.partial(lambda i, o: (i + o, 0), o=off // br))

    def whole_spec(shape):
        return pl.BlockSpec(shape, functools.partial(lambda i, nd: (0,) * nd, nd=len(shape)))

    def body(*refs):
        res = f(*[r[...] for r in refs[:n_in]])
        outs = refs[n_in:]
        for o_ref, val in zip(outs[:n_ro], res[:n_ro]):
            o_ref[...] = val.astype(o_ref.dtype)
        if acc_outs:
            @pl.when(pl.program_id(0) == 0)
            def _():
                for o_ref in outs[n_ro:]:
                    o_ref[...] = jnp.zeros_like(o_ref)

            for o_ref, val in zip(outs[n_ro:], res[n_ro:]):
                o_ref[...] += val.astype(F32)

    out = _pcall(
        body, name=name, grid=(t // br,),
        in_specs=[row_spec(a, o) for a, o in rows] + [whole_spec(p.shape) for p in params],
        out_specs=[pl.BlockSpec((br, c), lambda i: (i, 0)) for c, _ in row_outs] + [whole_spec(s) for s in acc_outs],
        out_shape=[jax.ShapeDtypeStruct((t, c), dt) for c, dt in row_outs]
        + [jax.ShapeDtypeStruct(s, F32) for s in acc_outs],
        compiler_params=_params(("arbitrary",) if acc_outs else ("parallel",)),
    )(*[a for a, _ in rows], *params)
    return out


def _bdot_raw(a, b, dn):
    return lax.dot_general(a.astype(BF16), b.astype(BF16), _DN[dn], preferred_element_type=F32)


def _hdot_raw(a, b, dn):
    return lax.dot_general(a.astype(F32), b.astype(F32), _DN[dn], precision=lax.Precision.HIGHEST,
                           preferred_element_type=F32)


def _with_vjp(raw):
    @functools.partial(jax.custom_vjp, nondiff_argnums=(2,))
    def dot(a, b, dn):
        return raw(a, b, dn)

    def fwd(a, b, dn):
        return raw(a, b, dn), (a, b)

    def bwd(dn, res, ct):
        a, b = res
        if dn == 'nn':
            da, db = raw(ct, b, 'nt'), raw(a, ct, 'tn')
        elif dn == 'nt':
            da, db = raw(ct, b, 'nn'), raw(ct, a, 'tn')
        else:
            da, db = raw(b, ct, 'nt'), raw(a, ct, 'nn')
        return da.astype(a.dtype), db.astype(b.dtype)

    dot.defvjp(fwd, bwd)
    return dot


_bdot, _hdot = _with_vjp(_bdot_raw), _with_vjp(_hdot_raw)


def _rms(x, g):
    return x * lax.rsqrt(jnp.mean(x * x, axis=-1, keepdims=True) + NORM_EPS) * g


def _sigmoid(x):
    return 1.0 / (1.0 + jnp.exp(-x))


def _silu(x):
    return x * _sigmoid(x)


def _gelu(x):
    return 0.5 * x * (1.0 + lax.erf(x * (2.0 ** -0.5)))


def _norm_fwd(x, g, name):
    return _rowcall(name, lambda x_, g_: (_rms(x_, g_),), [x], [g], [(x.shape[1], BF16)])[0]


def _resnorm_fwd(x, f, g, alpha, name):
    return _rowcall(name, lambda x_, f_, g_: (x_ + alpha * _rms(f_, g_),), [x, f], [g], [(x.shape[1], F32)])[0]


def _resnorm_bwd(f, dx, g, alpha, name):
    def fn(f_, dx_, g_):
        _, vjp = jax.vjp(lambda a, b: alpha * _rms(a, b), f_, g_)
        return vjp(dx_)

    return _rowcall(name, fn, [f, dx], [g], [(f.shape[1], BF16)], [g.shape])


def _norm_bwd(x, dh, dx_out, g, name):
    def fn(x_, dh_, dxo_, g_):
        _, vjp = jax.vjp(_rms, x_, g_)
        dx, dg = vjp(dh_)
        return dxo_ + dx, dg

    return _rowcall(name, fn, [x, dh, dx_out], [g], [(x.shape[1], F32)], [g.shape])


def _norm_bwd_gain(x, dh, g, name):
    def fn(x_, dh_, g_):
        _, vjp = jax.vjp(lambda b: _rms(x_, b), g_)
        return vjp(dh_)

    return _rowcall(name, fn, [x, dh], [g], [], [g.shape])[0]


def _swiglu_fwd(gu, name):
    ff = gu.shape[1] // 2
    return _rowcall(name, lambda gu_: (_silu(gu_[:, :ff]) * gu_[:, ff:],), [gu], [], [(ff, BF16)], br=64)[0]


def _swiglu_bwd(gu, da, name):
    ff = gu.shape[1] // 2

    def fn(gu_, da_):
        gate, up = gu_[:, :ff], gu_[:, ff:]
        s = _sigmoid(gate)
        dgate = da_ * up * (s * (1.0 + gate * (1.0 - s)))
        dup = da_ * (gate * s)
        return (jnp.concatenate([dgate, dup], axis=1),)

    return _rowcall(name, fn, [gu, da], [], [(2 * ff, BF16)], br=64)[0]


def _sgu_norm(v, lg, lb):
    vf = _gelu(v)
    mu = jnp.mean(vf, axis=-1, keepdims=True)
    var = jnp.mean(jnp.square(vf - mu), axis=-1, keepdims=True)
    return (vf - mu) * lax.rsqrt(var + NORM_EPS) * lg + lb


def _sgu_head(dot, u_h, vn_h, w_h, b_h):
    r = lax.broadcasted_iota(jnp.int32, w_h.shape, 0)
    c = lax.broadcasted_iota(jnp.int32, w_h.shape, 1)
    mixed = dot(jnp.where(r >= c, w_h, 0.0), vn_h, 'nn') + b_h
    return _gelu(u_h) * mixed


def _heads(width):
    return [slice(h * HEAD_DIM, (h + 1) * HEAD_DIM) for h in range(width // HEAD_DIM)]


def _sgu_fwd(u, v, w, bcol, lg, lb, name):
    def fn(u_, v_, w_, b_, lg_, lb_):
        vn = _sgu_norm(v_, lg_, lb_)
        return (jnp.concatenate([_sgu_head(_bdot_raw, u_[:, s], vn[:, s], w_[h], b_[h])
                                 for h, s in enumerate(_heads(u_.shape[1]))], axis=1),)

    return _rowcall(name, fn, [u, v], [w, bcol, lg, lb], [(u.shape[1], BF16)], br=GM_CHUNK)[0]


def _sgu_bwd(u, v, w, bcol, lg, lb, dy, name):
    def fn(u_, v_, dy_, w_, b_, lg_, lb_):
        vn, vjp_norm = jax.vjp(_sgu_norm, v_, lg_, lb_)
        du, dvn, dw, db = [], [], [], []
        for h, s in enumerate(_heads(u_.shape[1])):
            _, vjp = jax.vjp(functools.partial(_sgu_head, _bdot), u_[:, s], vn[:, s], w_[h], b_[h])
            a, b, c, d = vjp(dy_[:, s])
            du.append(a), dvn.append(b), dw.append(c[None]), db.append(d[None])
        dv, dlg, dlb = vjp_norm(jnp.concatenate(dvn, axis=1))
        return (jnp.concatenate(du, axis=1), dv, jnp.concatenate(dw, axis=0), jnp.concatenate(db, axis=0), dlg, dlb)

    wd = u.shape[1]
    return _rowcall(name, fn, [u, v, dy], [w, bcol, lg, lb], [(wd, F32), (wd, F32)],
                    [w.shape, bcol.shape, lg.shape, lb.shape], br=GM_CHUNK)


def _shift_down(x, s):
    if s == 0:
        return x
    rows = lax.broadcasted_iota(jnp.int32, x.shape, 0)
    return jnp.where(rows >= s, pltpu.roll(x, s, 0), 0.0)


def _shift_up(x, s):
    if s == 0:
        return x
    t = x.shape[0]
    rows = lax.broadcasted_iota(jnp.int32, x.shape, 0)
    return jnp.where(rows < t - s, pltpu.roll(x, t - s, 0), 0.0)


def _conv_pre(x, taps):
    acc = None
    for i in range(CONV_WIDTH):
        term = _shift_down(x, CONV_WIDTH - 1 - i) * taps[i]
        acc = term if acc is None else acc + term
    return acc


def _taps(w_ref):
    return [w_ref[i:i + 1, :] for i in range(CONV_WIDTH)]


def _conv_fwd(x, w, name):
    t, ch = x.shape
    cb = _divtile(ch, 512)

    def body(x_ref, w_ref, o_ref):
        o_ref[...] = _silu(_conv_pre(x_ref[...], _taps(w_ref)))

    return _pcall(body, name=name, grid=(ch // cb,),
                  in_specs=[pl.BlockSpec((t, cb), lambda j: (0, j)), pl.BlockSpec((CONV_WIDTH, cb), lambda j: (0, j))],
                  out_specs=pl.BlockSpec((t, cb), lambda j: (0, j)),
                  out_shape=jax.ShapeDtypeStruct((t, ch), F32), compiler_params=_params(("parallel",)))(x, w)


def _conv_bwd(x, w, dout, name):
    t, ch = x.shape
    cb = _divtile(ch, 256)

    def body(x_ref, w_ref, do_ref, dx_ref, dw_ref):
        xv, taps = x_ref[...], _taps(w_ref)
        pre = _conv_pre(xv, taps)
        s = _sigmoid(pre)
        dpre = do_ref[...] * (s * (1.0 + pre * (1.0 - s)))
        dx = None
        for i in range(CONV_WIDTH):
            sh = CONV_WIDTH - 1 - i
            term = _shift_up(dpre, sh) * taps[i]
            dx = term if dx is None else dx + term
            dw_ref[i:i + 1, :] = jnp.sum(dpre * _shift_down(xv, sh), axis=0, keepdims=True)
        dx_ref[...] = dx

    return _pcall(body, name=name, grid=(ch // cb,),
                  in_specs=[pl.BlockSpec((t, cb), lambda j: (0, j)), pl.BlockSpec((CONV_WIDTH, cb), lambda j: (0, j)),
                            pl.BlockSpec((t, cb), lambda j: (0, j))],
                  out_specs=[pl.BlockSpec((t, cb), lambda j: (0, j)), pl.BlockSpec((CONV_WIDTH, cb), lambda j: (0, j))],
                  out_shape=[jax.ShapeDtypeStruct((t, ch), F32), jax.ShapeDtypeStruct((CONV_WIDTH, ch), F32)],
                  compiler_params=_params(("parallel",)))(x, w, dout)


def _gates(b_raw, a_raw, a_log, dt_bias):
    z = a_raw + dt_bias
    softplus = jnp.maximum(z, 0.0) + jnp.log(1.0 + jnp.exp(-jnp.abs(z)))
    return _sigmoid(b_raw), -jnp.exp(a_log) * softplus


def _gates_fwd(b_raw, a_raw, a_log, dt_bias, name):
    h = b_raw.shape[1]
    return _rowcall(name, _gates, [b_raw, a_raw], [a_log, dt_bias], [(h, F32), (h, F32)])


def _gates_bwd(b_raw, a_raw, a_log, dt_bias, dbeta, dg, name):
    def fn(b_, a_, dbeta_, dg_, al_, dt_):
        _, vjp = jax.vjp(_gates, b_, a_, al_, dt_)
        return vjp((dbeta_, dg_))

    h = b_raw.shape[1]
    return _rowcall(name, fn, [b_raw, a_raw, dbeta, dg], [a_log, dt_bias], [(h, F32), (h, F32)],
                    [a_log.shape, dt_bias.shape])


def _delta_chunk(bd, hd, state, q, k, v, gc, gr, bc):
    c, d = q.shape
    ri = lax.broadcasted_iota(jnp.int32, (c, c), 0)
    ci = lax.broadcasted_iota(jnp.int32, (c, c), 1)
    causal, strict = ri >= ci, ri > ci
    ltri, utri, eye = causal.astype(F32), (ri <= ci).astype(F32), (ri == ci).astype(F32)
    qn = q * lax.rsqrt(jnp.sum(q * q, axis=-1, keepdims=True) + NORM_EPS) * (d ** -0.5)
    kn = k * lax.rsqrt(jnp.sum(k * k, axis=-1, keepdims=True) + NORM_EPS)
    gb = jnp.broadcast_to(gc, (c, d))
    gcum = hd(ltri, gb, 'nn')
    gcum_t = hd(ltri, jnp.broadcast_to(gc, (c, c)), 'nn')
    gcum_s = hd(jnp.broadcast_to(gr, (c, c)), utri, 'nn')
    decay = jnp.where(causal, jnp.exp(jnp.where(causal, gcum_t - gcum_s, 0.0)), 0.0)
    bb = jnp.broadcast_to(bc, (c, d))
    k_beta = kn * bb
    a = jnp.where(strict, bd(k_beta, kn, 'nt') * decay, 0.0)
    inv, p = eye - a, hd(a, a, 'nn')
    n_fac = int(math.log2(c)) - 1
    for it in range(n_fac):
        inv = inv + hd(inv, p, 'nn')
        if it < n_fac - 1:
            p = hd(p, p, 'nn')
    u = hd(inv, v * bb, 'nn')
    w = hd(inv, k_beta * jnp.exp(gcum), 'nn')
    qk = jnp.where(causal, bd(qn, kn, 'nt') * decay, 0.0)
    v_new = u - bd(w, state, 'nn')
    o = bd(qn * jnp.exp(gcum), state, 'nn') + bd(qk, v_new, 'nn')
    g_last = jnp.sum(gb, axis=0, keepdims=True)
    k_dec = kn * jnp.exp(g_last - gcum)
    return state * jnp.exp(g_last) + bd(k_dec, v_new, 'tn'), o


def _delta_specs(nh, nc, rev):
    c, d = DN_CHUNK, HEAD_DIM
    ch = (lambda n: nc - 1 - n) if rev else (lambda n: n)
    qkv = [pl.BlockSpec((c, d), functools.partial(lambda h, n, o: (ch(n), o + h), o=o)) for o in (0, nh, 2 * nh)]
    col = pl.BlockSpec((None, c, 1), lambda h, n: (h, ch(n), 0))
    row = pl.BlockSpec((None, None, 1, c), lambda h, n: (h, ch(n), 0, 0))
    st = pl.BlockSpec((None, None, d, d), lambda h, n: (h, ch(n), 0, 0))
    tok = pl.BlockSpec((c, d), lambda h, n: (ch(n), h))
    return qkv, col, row, st, tok


def _delta_fwd(qkv, gcol, grow, bcol, name):
    t = qkv.shape[0]
    nh, nc, d = gcol.shape[0], t // DN_CHUNK, HEAD_DIM
    qkv_s, col, row, st, tok = _delta_specs(nh, nc, False)

    def body(q_ref, k_ref, v_ref, gc_ref, gr_ref, bc_ref, o_ref, st_ref, state):
        @pl.when(pl.program_id(1) == 0)
        def _():
            state[...] = jnp.zeros_like(state)

        s0 = state[...]
        st_ref[...] = s0
        s1, o = _delta_chunk(_bdot_raw, _hdot_raw, s0, q_ref[...], k_ref[...], v_ref[...], gc_ref[...], gr_ref[...],
                             bc_ref[...])
        o_ref[...] = o
        state[...] = s1

    return _pcall(body, name=name, grid=(nh, nc), in_specs=qkv_s + [col, row, col], out_specs=[tok, st],
                  out_shape=[jax.ShapeDtypeStruct((t, nh * d), F32), jax.ShapeDtypeStruct((nh, nc, d, d), F32)],
                  scratch_shapes=[pltpu.VMEM((d, d), F32)],
                  compiler_params=_params(("parallel", "arbitrary")))(qkv, qkv, qkv, gcol, grow, bcol)


def _delta_bwd(qkv, gcol, grow, bcol, states, do, name):
    t = qkv.shape[0]
    nh, nc, d = gcol.shape[0], t // DN_CHUNK, HEAD_DIM
    qkv_s, col, row, st, tok = _delta_specs(nh, nc, True)

    def body(q_ref, k_ref, v_ref, gc_ref, gr_ref, bc_ref, st_ref, do_ref,
             dq_ref, dk_ref, dv_ref, dgc_ref, dgr_ref, dbc_ref, dstate):
        @pl.when(pl.program_id(1) == 0)
        def _():
            dstate[...] = jnp.zeros_like(dstate)

        _, vjp = jax.vjp(functools.partial(_delta_chunk, _bdot, _hdot), st_ref[...], q_ref[...], k_ref[...],
                         v_ref[...], gc_ref[...], gr_ref[...], bc_ref[...])
        ds, dq, dk, dv, dgc, dgr, dbc = vjp((dstate[...], do_ref[...]))
        dq_ref[...], dk_ref[...], dv_ref[...] = dq, dk, dv
        dgc_ref[...], dgr_ref[...], dbc_ref[...] = dgc, dgr, dbc
        dstate[...] = ds

    tok_out = jax.ShapeDtypeStruct((t, nh * d), F32)
    return _pcall(body, name=name, grid=(nh, nc), in_specs=qkv_s + [col, row, col, st, tok],
                  out_specs=[tok, tok, tok, col, row, col],
                  out_shape=[tok_out, tok_out, tok_out, jax.ShapeDtypeStruct(gcol.shape, F32),
                             jax.ShapeDtypeStruct(grow.shape, F32), jax.ShapeDtypeStruct(bcol.shape, F32)],
                  scratch_shapes=[pltpu.VMEM((d, d), F32)],
                  compiler_params=_params(("parallel", "arbitrary")))(qkv, qkv, qkv, gcol, grow, bcol, states, do)


def _outgate_head(o_h, z_h, w):
    return _rms(o_h, w) * _silu(z_h)


def _outgate_fwd(o, z, w, name):
    def fn(o_, z_, w_):
        return (jnp.concatenate([_outgate_head(o_[:, s], z_[:, s], w_) for s in _heads(o_.shape[1])], axis=1),)

    return _rowcall(name, fn, [o, z], [w], [(o.shape[1], BF16)])[0]


def _outgate_bwd(o, z, w, dy, name):
    def fn(o_, z_, dy_, w_):
        do, dz, dw = [], [], None
        for s in _heads(o_.shape[1]):
            _, vjp = jax.vjp(_outgate_head, o_[:, s], z_[:, s], w_)
            a, b, c = vjp(dy_[:, s])
            do.append(a), dz.append(b)
            dw = c if dw is None else dw + c
        return jnp.concatenate(do, axis=1), jnp.concatenate(dz, axis=1), dw

    wd = o.shape[1]
    return _rowcall(name, fn, [o, z, dy], [w], [(wd, F32), (wd, F32)], [w.shape])


def _attn_head(dot, q_h, k_h, v_h):
    s = dot(q_h, k_h, 'nt') * (q_h.shape[1] ** -0.5)
    e = jnp.exp(s - lax.stop_gradient(jnp.max(s, axis=-1, keepdims=True)))
    p = e / jnp.sum(e, axis=-1, keepdims=True)
    return dot(p, v_h, 'nn')


def _xa_slices(width):
    hd = width // XA_HEADS
    return [slice(h * hd, (h + 1) * hd) for h in range(XA_HEADS)]


def _attn_fwd(q, kv, name):
    wd = q.shape[1]

    def fn(q_, kv_):
        k_, v_ = kv_[:, :wd], kv_[:, wd:]
        return (jnp.concatenate([_attn_head(_bdot_raw, q_[:, s], k_[:, s], v_[:, s]) for s in _xa_slices(wd)],
                                axis=1),)

    return _rowcall(name, fn, [q], [kv], [(wd, BF16)])[0]


def _attn_bwd(q, kv, do, name):
    wd = q.shape[1]

    def fn(q_, do_, kv_):
        k_, v_ = kv_[:, :wd].astype(F32), kv_[:, wd:].astype(F32)
        dq, dk, dv = [], [], []
        for s in _xa_slices(wd):
            _, vjp = jax.vjp(functools.partial(_attn_head, _bdot), q_[:, s].astype(F32), k_[:, s], v_[:, s])
            a, b, c = vjp(do_[:, s])
            dq.append(a), dk.append(b), dv.append(c)
        return jnp.concatenate(dq, axis=1), jnp.concatenate(dk + dv, axis=1)

    return _rowcall(name, fn, [q, do], [kv], [(wd, BF16)], [kv.shape])


def _loss_call(y, target, name):
    d = y.shape[1]

    def fn(y_, t_):
        err = y_ - t_
        part = 0.5 * jnp.sum(jnp.sum(err * err, axis=1, keepdims=True), axis=0, keepdims=True) / d
        return err / d, jnp.broadcast_to(part, (1, LANES))

    return _rowcall(name, fn, [y, target], [], [(d, F32)], [(1, LANES)])


def _adamw_call(w, g, m, v, name):
    def fn(w_, g_, m_, v_):
        m1 = ADAM_B1 * m_ + (1.0 - ADAM_B1) * g_
        v1 = ADAM_B2 * v_ + (1.0 - ADAM_B2) * jnp.square(g_)
        m_hat = m1 / (1.0 - ADAM_B1 ** ADAM_STEP)
        v_hat = v1 / (1.0 - ADAM_B2 ** ADAM_STEP)
        return -ADAM_LR * (m_hat / (jnp.sqrt(v_hat) + ADAM_EPS) + ADAM_WD * w_), m1, v1

    c = w.shape[1]
    return _rowcall(name, fn, [w, g, m, v], [], [(c, F32)] * 3, br=_divtile(w.shape[0], 128, 8))


def _add2_call(a, b, name):
    return _rowcall(name, lambda a_, b_: (a_.astype(F32) + b_.astype(F32),), [a, b], [], [(a.shape[1], BF16)],
                    br=_divtile(a.shape[0], 128, 16))[0]


def _sum4_call(buf, name):
    hr = buf.shape[0] // N_CHIPS

    def fn(a, b, c, d):
        return (((a.astype(F32) + b.astype(F32)) + c.astype(F32)) + d.astype(F32),)

    return _rowcall(name, fn, [(buf, j * hr) for j in range(N_CHIPS)], [], [(buf.shape[1], F32)],
                    br=_divtile(hr, 128, 16), nrows=hr)[0]


def _position():
    x, y, c = lax.axis_index("x"), lax.axis_index("y"), lax.axis_index("c")
    others = [(1 - x, y), (x, 1 - y), (1 - x, 1 - y)]
    return x, y, c, others


def _any_specs(n):
    return [pl.BlockSpec(memory_space=pl.ANY)] * n


def _ds(start, size):
    return pl.ds(pl.multiple_of(start, size), size)


def _gather_weights(shards, kinds, name):
    n = len(shards)

    def body(*refs):
        src, dst = refs[:n], refs[n:2 * n]
        send, recv, local = refs[2 * n:]
        x, y, c, others = _position()
        me, sib = 2 * x + y, (x, y, 1 - c)

        def region(w, chip, half):
            hr, cw = src[w].shape[0] // 2, src[w].shape[1]
            if kinds[w] == 'row':
                return dst[w].at[chip, _ds(half * hr, hr), :]
            return dst[w].at[_ds(half * hr, hr), _ds(chip * cw, cw)]

        def whole(w, chip):
            if kinds[w] == 'row':
                return dst[w].at[chip]
            return dst[w].at[:, _ds(chip * src[w].shape[1], src[w].shape[1])]

        def copy(w, sem, s, chip, half, to):
            return pltpu.make_async_remote_copy(src_ref=s, dst_ref=region(w, chip, half), send_sem=send.at[6 * w + sem],
                                                recv_sem=recv.at[6 * w + sem], device_id=to, device_id_type=MESH_IDS)

        mine = [pltpu.make_async_copy(src[w], whole(w, me), local.at[w]) for w in range(n)]
        first = [copy(w, k, src[w].at[_ds(c * (src[w].shape[0] // 2), src[w].shape[0] // 2), :], me, c, (px, py, c))
                 for w in range(n) for k, (px, py) in enumerate(others)]
        for cp in mine + first:
            cp.start()
        passed = []
        for w in range(n):
            for k, (px, py) in enumerate(others):
                theirs = region(w, 2 * px + py, c)
                copy(w, k, theirs, 2 * px + py, c, sib).wait_recv()
                passed.append(copy(w, 3 + k, theirs, 2 * px + py, c, sib))
                passed[-1].start()
        for w in range(n):
            for k, (px, py) in enumerate(others):
                theirs = region(w, 2 * px + py, 1 - c)
                copy(w, 3 + k, theirs, 2 * px + py, 1 - c, sib).wait_recv()
        for cp in first + passed:
            cp.wait_send()
        for cp in mine:
            cp.wait()

    def full_shape(s, kind):
        return (N_CHIPS, *s.shape) if kind == 'row' else (s.shape[0], N_CHIPS * s.shape[1])

    return _pcall(body, name=name, in_specs=_any_specs(n), out_specs=_any_specs(n),
                  out_shape=[jax.ShapeDtypeStruct(full_shape(s, kd), s.dtype) for s, kd in zip(shards, kinds)],
                  scratch_shapes=[pltpu.SemaphoreType.DMA((6 * n,)), pltpu.SemaphoreType.DMA((6 * n,)),
                                  pltpu.SemaphoreType.DMA((n,))],)(*shards)


def _pair_exchange(grads, kinds, name):
    n = len(grads)

    def body(*refs):
        src, mine_ref, theirs_ref = refs[:n], refs[n:2 * n], refs[2 * n:3 * n]
        send, recv, local = refs[3 * n:]
        x, y, c, _ = _position()
        sib = (x, y, 1 - c)

        def half(w, h):
            if kinds[w] == 'row':
                hr = src[w].shape[1] // 2
                return src[w].at[:, _ds(h * hr, hr), :]
            hr = src[w].shape[0] // 2
            return src[w].at[_ds(h * hr, hr), :]

        keep = [pltpu.make_async_copy(half(w, c), mine_ref[w], local.at[w]) for w in range(n)]
        give = [pltpu.make_async_remote_copy(src_ref=half(w, 1 - c), dst_ref=theirs_ref[w], send_sem=send.at[w],
                                             recv_sem=recv.at[w], device_id=sib, device_id_type=MESH_IDS)
                for w in range(n)]
        for cp in keep + give:
            cp.start()
        for cp in give:
            cp.wait()
        for cp in keep:
            cp.wait()

    def half_shape(g, kind):
        return (g.shape[0], g.shape[1] // 2, g.shape[2]) if kind == 'row' else (g.shape[0] // 2, g.shape[1])

    shapes = [jax.ShapeDtypeStruct(half_shape(g, kd), g.dtype) for g, kd in zip(grads, kinds)]
    out = _pcall(body, name=name, in_specs=_any_specs(n), out_specs=_any_specs(2 * n), out_shape=shapes + shapes,
                 scratch_shapes=[pltpu.SemaphoreType.DMA((n,)), pltpu.SemaphoreType.DMA((n,)),
                                 pltpu.SemaphoreType.DMA((n,))],)(*grads)
    return out[:n], out[n:]


def _chip_exchange(parts, kinds, name):
    n = len(parts)

    def dims(w):
        p = parts[w]
        return (p.shape[1], p.shape[2]) if kinds[w] == 'row' else (p.shape[0], p.shape[1] // N_CHIPS)

    def body(*refs):
        src, dst = refs[:n], refs[n:2 * n]
        send, recv, local = refs[2 * n:]
        x, y, c, others = _position()
        me = 2 * x + y

        def slot(w, chip):
            if kinds[w] == 'row':
                return src[w].at[chip]
            return src[w].at[:, _ds(chip * dims(w)[1], dims(w)[1])]

        keep = [pltpu.make_async_copy(slot(w, me), dst[w].at[me], local.at[w]) for w in range(n)]
        give = [pltpu.make_async_remote_copy(src_ref=slot(w, 2 * px + py), dst_ref=dst[w].at[me],
                                             send_sem=send.at[3 * w + k], recv_sem=recv.at[3 * w + k],
                                             device_id=(px, py, c), device_id_type=MESH_IDS)
                for w in range(n) for k, (px, py) in enumerate(others)]
        for cp in keep + give:
            cp.start()
        for w in range(n):
            for k, (px, py) in enumerate(others):
                pltpu.make_async_remote_copy(src_ref=dst[w].at[2 * px + py], dst_ref=dst[w].at[2 * px + py],
                                             send_sem=send.at[3 * w + k], recv_sem=recv.at[3 * w + k],
                                             device_id=(px, py, c), device_id_type=MESH_IDS).wait_recv()
        for cp in give:
            cp.wait_send()
        for cp in keep:
            cp.wait()

    return _pcall(body, name=name, in_specs=_any_specs(n), out_specs=_any_specs(n),
                  out_shape=[jax.ShapeDtypeStruct((N_CHIPS, *dims(w)), parts[w].dtype) for w in range(n)],
                  scratch_shapes=[pltpu.SemaphoreType.DMA((3 * n,)), pltpu.SemaphoreType.DMA((3 * n,)),
                                  pltpu.SemaphoreType.DMA((n,))],)(*parts)


def _pair_gather(halves, n_layers, name):
    n = len(halves)

    def body(*refs):
        src = [refs[w * n_layers:(w + 1) * n_layers] for w in range(n)]
        dst = refs[n * n_layers:n * n_layers + n]
        send, recv, local = refs[n * n_layers + n:]
        x, y, c, _ = _position()
        sib = (x, y, 1 - c)
        keep, give, take = [], [], []
        for w in range(n):
            hr = src[w][0].shape[0]
            for l in range(n_layers):
                i = w * n_layers + l
                keep.append(pltpu.make_async_copy(src[w][l], dst[w].at[l, _ds(c * hr, hr), :], local.at[i]))
                give.append(pltpu.make_async_remote_copy(
                    src_ref=src[w][l], dst_ref=dst[w].at[l, _ds(c * hr, hr), :], send_sem=send.at[i],
                    recv_sem=recv.at[i], device_id=sib, device_id_type=MESH_IDS))
                take.append(pltpu.make_async_remote_copy(
                    src_ref=src[w][l], dst_ref=dst[w].at[l, _ds((1 - c) * hr, hr), :], send_sem=send.at[i],
                    recv_sem=recv.at[i], device_id=sib, device_id_type=MESH_IDS))
        for cp in keep + give:
            cp.start()
        for cp in take:
            cp.wait_recv()
        for cp in give:
            cp.wait_send()
        for cp in keep:
            cp.wait()

    flat = [h for hs in halves for h in hs]
    m = len(flat)
    return _pcall(body, name=name, in_specs=_any_specs(m), out_specs=_any_specs(n),
                  out_shape=[jax.ShapeDtypeStruct((n_layers, 2 * hs[0].shape[0], hs[0].shape[1]), F32) for hs in halves],
                  scratch_shapes=[pltpu.SemaphoreType.DMA((m,)), pltpu.SemaphoreType.DMA((m,)),
                                  pltpu.SemaphoreType.DMA((m,))],)(*flat)


def _all_sum(buf, name):
    rows = buf.shape[0]
    flips = [(fx, fy, fc) for fx in (0, 1) for fy in (0, 1) for fc in (0, 1)][1:]

    def body(x_ref, o_ref, land, send, recv):
        x, y, c, _ = _position()
        me = 4 * x + 2 * y + c

        def peer(f):
            return (1 - x if f[0] else x, 1 - y if f[1] else y, 1 - c if f[2] else c)

        give = [pltpu.make_async_remote_copy(src_ref=x_ref, dst_ref=land.at[me], send_sem=send.at[k],
                                             recv_sem=recv.at[k], device_id=peer(f), device_id_type=MESH_IDS)
                for k, f in enumerate(flips)]
        for cp in give:
            cp.start()
        land[me] = x_ref[...]
        for k, f in enumerate(flips):
            px, py, pc = peer(f)
            slot = land.at[4 * px + 2 * py + pc]
            pltpu.make_async_remote_copy(src_ref=slot, dst_ref=slot, send_sem=send.at[k], recv_sem=recv.at[k],
                                         device_id=peer(f), device_id_type=MESH_IDS).wait_recv()
        for cp in give:
            cp.wait_send()
        total = land[0]
        for dev in range(1, N_DEV):
            total = total + land[dev]
        o_ref[...] = total

    return _pcall(body, name=name, in_specs=[pl.BlockSpec(memory_space=pltpu.VMEM)],
                  out_specs=pl.BlockSpec(memory_space=pltpu.VMEM), out_shape=jax.ShapeDtypeStruct(buf.shape, F32),
                  scratch_shapes=[pltpu.VMEM((N_DEV, rows, LANES), F32), pltpu.SemaphoreType.DMA((N_DEV - 1,)),
                                  pltpu.SemaphoreType.DMA((N_DEV - 1,))],
                  compiler_params=pltpu.CompilerParams(vmem_limit_bytes=VMEM_LIMIT))(buf)


def _pack(arrays):
    flat = jnp.concatenate([a.reshape(-1).astype(F32) for a in arrays])
    pad = (-flat.shape[0]) % (8 * LANES)
    return jnp.pad(flat, (0, pad)).reshape(-1, LANES)


def _unpack(buf, like):
    flat, out, off = buf.reshape(-1), [], 0
    for a in like:
        out.append(flat[off:off + a.size].reshape(a.shape))
        off += a.size
    return out


def _row2(v):
    return v.reshape(1, -1)


def _ffn_fwd(x, g_pre, w_gu, w_d, g_post, tag):
    h = _norm_fwd(x, g_pre, tag + "_pre")
    gu = _mm(h, w_gu, 'nn', F32, tag + "_gu")
    a = _swiglu_fwd(gu, tag + "_act")
    f = _mm(a, w_d, 'nn', F32, tag + "_down")
    return _resnorm_fwd(x, f, g_post, 0.5, tag + "_post"), (x, h, gu, a, f)


def _ffn_bwd(dx_out, saved, g_pre, w_gu, w_d, g_post, tag):
    x, h, gu, a, f = saved
    df, dg_post = _resnorm_bwd(f, dx_out, g_post, 0.5, tag + "_post_b")
    da = _mm(df, w_d, 'nt', F32, tag + "_down_bx")
    dw_d = _mm(a, df, 'tn', BF16, tag + "_down_bw")
    dgu = _swiglu_bwd(gu, da, tag + "_act_b")
    dh = _mm(dgu, w_gu, 'nt', F32, tag + "_gu_bx")
    dw_gu = _mm(h, dgu, 'tn', BF16, tag + "_gu_bw")
    dx, dg_pre = _norm_bwd(x, dh, dx_out, g_pre, tag + "_pre_b")
    return dx, dg_pre, dw_gu, dw_d, dg_post


def _split_cols(pp, cs, cp, widths):
    proj = jnp.concatenate([pp[:, j * cp:j * cp + cs] for j in range(N_CHIPS)], axis=1)
    out, off = [], 0
    for wd in widths:
        out.append(proj[:, off:off + wd])
        off += wd
    return out


def _join_cols(pieces, cs, cp):
    proj = jnp.concatenate(pieces, axis=1)
    t = proj.shape[0]
    cols = []
    for j in range(N_CHIPS):
        cols += [proj[:, j * cs:(j + 1) * cs], jnp.zeros((t, cp - cs), proj.dtype)]
    return jnp.concatenate(cols, axis=1)


def _mix_fwd(x, p, w_in, w_out, conv_w, cs, cp, tag):
    t, d = x.shape
    half = d // 2
    nh = half // HEAD_DIM
    h = _norm_fwd(x, p['mix_norm_pre'], tag + "_pre")
    pp = _mm(h, w_in, 'nn', F32, tag + "_in")
    u_a, v_a, qkv_raw, z, b_raw, a_raw = _split_cols(pp, cs, cp, [half, half, 3 * half, half, nh, nh])
    y_a = _sgu_fwd(u_a, v_a, p['sm_w'], p['sm_b'], p['sm_ln_g'], p['sm_ln_b'], tag + "_sgu")
    qkv = _conv_fwd(qkv_raw, conv_w, tag + "_conv")
    beta, g = _gates_fwd(b_raw, a_raw, p['a_log'], p['dt_bias'], tag + "_gates")
    gcol, bcol = g.T[:, :, None], beta.T[:, :, None]
    grow = g.T.reshape(nh, t // DN_CHUNK, 1, DN_CHUNK)
    o, states = _delta_fwd(qkv, gcol, grow, bcol, tag + "_delta")
    y_b = _outgate_fwd(o, z, p['dn_norm_w'], tag + "_gate")
    y = jnp.concatenate([y_a, y_b], axis=1)
    m = _mm(y, w_out, 'nn', F32, tag + "_out")
    x_out = _resnorm_fwd(x, m, p['mix_norm_post'], 1.0, tag + "_post")
    return x_out, (x, h, u_a, v_a, qkv_raw, z, b_raw, a_raw, qkv, gcol, grow, bcol, states, o, y, m)


def _mix_bwd(dx_out, saved, p, w_in, w_out, conv_w, cs, cp, tag):
    x, h, u_a, v_a, qkv_raw, z, b_raw, a_raw, qkv, gcol, grow, bcol, states, o, y, m = saved
    t, d = x.shape
    half = d // 2
    nh = half // HEAD_DIM
    gr = {}
    dm, gr['mix_norm_post'] = _resnorm_bwd(m, dx_out, p['mix_norm_post'], 1.0, tag + "_post_b")
    dy = _mm(dm, w_out, 'nt', F32, tag + "_out_bx")
    gr['w_out'] = _mm(y, dm, 'tn', BF16, tag + "_out_bw")
    dy_a, dy_b = dy[:, :half], dy[:, half:]
    do, dz, gr['dn_norm_w'] = _outgate_bwd(o, z, p['dn_norm_w'], dy_b, tag + "_gate_b")
    dq, dk, dv, dgcol, dgrow, dbcol = _delta_bwd(qkv, gcol, grow, bcol, states, do, tag + "_delta_b")
    dg = (dgcol[:, :, 0] + dgrow.reshape(nh, t)).T
    dbeta = dbcol[:, :, 0].T
    db_raw, da_raw, gr['a_log'], gr['dt_bias'] = _gates_bwd(b_raw, a_raw, p['a_log'], p['dt_bias'], dbeta, dg,
                                                             tag + "_gates_b")
    dqkv_raw, gr['conv_w'] = _conv_bwd(qkv_raw, conv_w, jnp.concatenate([dq, dk, dv], axis=1), tag + "_conv_b")
    du_a, dv_a, gr['sm_w'], gr['sm_b'], gr['sm_ln_g'], gr['sm_ln_b'] = _sgu_bwd(
        u_a, v_a, p['sm_w'], p['sm_b'], p['sm_ln_g'], p['sm_ln_b'], dy_a, tag + "_sgu_b")
    dpp = _join_cols([du_a, dv_a, dqkv_raw, dz, db_raw, da_raw], cs, cp).astype(BF16)
    dh = _mm(dpp, w_in, 'nt', F32, tag + "_in_bx")
    gr['w_in'] = _mm(h, dpp, 'tn', BF16, tag + "_in_bw")
    dx, gr['mix_norm_pre'] = _norm_bwd(x, dh, dx_out, p['mix_norm_pre'], tag + "_pre_b")
    return dx, gr


def _xa_fwd(x, mem, p, w_xq, w_xkv, w_xo, tag):
    h = _norm_fwd(x, p['xa_norm_pre'], tag + "_pre")
    mh = _norm_fwd(mem, p['mem_norm'], tag + "_mem")
    q = _mm(h, w_xq, 'nn', BF16, tag + "_q")
    kv = _mm(mh, w_xkv, 'nn', BF16, tag + "_kv")
    o = _attn_fwd(q, kv, tag + "_attn")
    c = _mm(o, w_xo, 'nn', F32, tag + "_o")
    return _resnorm_fwd(x, c, p['xa_norm_post'], 1.0, tag + "_post"), (x, h, mh, q, kv, o, c)


def _xa_bwd(dx_out, saved, mem, p, w_xq, w_xkv, w_xo, tag):
    x, h, mh, q, kv, o, c = saved
    gr = {}
    dc, gr['xa_norm_post'] = _resnorm_bwd(c, dx_out, p['xa_norm_post'], 1.0, tag + "_post_b")
    do = _mm(dc, w_xo, 'nt', F32, tag + "_o_bx")
    gr['w_xo'] = _mm(o, dc, 'tn', BF16, tag + "_o_bw")
    dq, dkv = _attn_bwd(q, kv, do, tag + "_attn_b")
    dkv = dkv.astype(BF16)
    dh = _mm(dq, w_xq, 'nt', F32, tag + "_q_bx")
    gr['w_xq'] = _mm(h, dq, 'tn', BF16, tag + "_q_bw")
    dmh = _mm(dkv, w_xkv, 'nt', F32, tag + "_kv_bx")
    gr['w_xkv'] = _mm(mh, dkv, 'tn', BF16, tag + "_kv_bw")
    gr['mem_norm'] = _norm_bwd_gain(mem, dmh, p['mem_norm'], tag + "_mem_b")
    dx, gr['xa_norm_pre'] = _norm_bwd(x, dh, dx_out, p['xa_norm_pre'], tag + "_pre_b")
    return dx, gr


def _step(inputs):
    x, mem, target = inputs['x'][0], inputs['mem'][0], inputs['loss_target'][0]
    n_layers = inputs['w_in'].shape[0]
    cx, cy, cc = lax.axis_index("x"), lax.axis_index("y"), lax.axis_index("c")
    chip = 2 * cx + cy
    cs = inputs['w_in'].shape[2]
    cp = -(-cs // LANES) * LANES
    kinds = [BIG[nm] for nm in BIG_NAMES]

    conv_local = inputs['conv_w']
    ccols = conv_local.shape[2]
    placed = lax.dynamic_update_slice(jnp.zeros((n_layers, CONV_WIDTH, N_CHIPS * ccols), F32),
                                      jnp.where(cc == 0, conv_local, 0.0), (0, 0, chip * ccols))
    conv_full = _unpack(_all_sum(_pack([placed]), "conv_gather"), [placed])[0]

    full = []
    for l in range(n_layers):
        shards = []
        for nm in BIG_NAMES:
            w = inputs[nm][l].astype(BF16)
            if nm == 'w_in':
                w = jnp.pad(w, ((0, 0), (0, cp - cs)))
            shards.append(w)
        got = _gather_weights(shards, kinds, "gather_weights")
        full.append({nm: (g.reshape(-1, g.shape[2]) if kd == 'row' else g)
                     for nm, g, kd in zip(BIG_NAMES, got, kinds)})

    def small(l):
        p = {nm: inputs[nm][l] for nm in SMALL_NAMES}
        for nm in SMALL_NAMES:
            if p[nm].ndim == 1:
                p[nm] = _row2(p[nm])
        p['sm_b'] = p['sm_b'][:, :, None]
        return p

    saved = []
    for l in range(n_layers):
        p, w, tag = small(l), full[l], ""
        x, s1 = _ffn_fwd(x, p['ffn1_norm_pre'], w['ffn1_w_gate_up'], w['ffn1_w_down'], p['ffn1_norm_post'], "ffn")
        x, s2 = _mix_fwd(x, p, w['w_in'], w['w_out'], conv_full[l], cs, cp, "mix")
        x, s3 = _xa_fwd(x, mem, p, w['w_xq'], w['w_xkv'], w['w_xo'], "xa")
        x, s4 = _ffn_fwd(x, p['ffn2_norm_pre'], w['ffn2_w_gate_up'], w['ffn2_w_down'], p['ffn2_norm_post'], "ffn")
        saved.append((s1, s2, s3, s4))

    dx, loss_part = _loss_call(x, target, "loss")

    grads = [None] * n_layers
    for l in reversed(range(n_layers)):
        p, w = small(l), full[l]
        s1, s2, s3, s4 = saved[l]
        gr = {}
        dx, gr['ffn2_norm_pre'], gr['ffn2_w_gate_up'], gr['ffn2_w_down'], gr['ffn2_norm_post'] = _ffn_bwd(
            dx, s4, p['ffn2_norm_pre'], w['ffn2_w_gate_up'], w['ffn2_w_down'], p['ffn2_norm_post'], "ffn")
        dx, g3 = _xa_bwd(dx, s3, mem, p, w['w_xq'], w['w_xkv'], w['w_xo'], "xa")
        dx, g2 = _mix_bwd(dx, s2, p, w['w_in'], w['w_out'], conv_full[l], cs, cp, "mix")
        dx, gr['ffn1_norm_pre'], gr['ffn1_w_gate_up'], gr['ffn1_w_down'], gr['ffn1_norm_post'] = _ffn_bwd(
            dx, s1, p['ffn1_norm_pre'], w['ffn1_w_gate_up'], w['ffn1_w_down'], p['ffn1_norm_post'], "ffn")
        gr.update(g3), gr.update(g2)
        grads[l] = gr

    halves = [[] for _ in BIG_NAMES]
    for l in range(n_layers):
        gs = []
        for nm, kd in zip(BIG_NAMES, kinds):
            g = grads[l][nm]
            gs.append(g.reshape(N_CHIPS, -1, g.shape[1]) if kd == 'row' else g)
        mine, theirs = _pair_exchange(gs, kinds, "pair_exchange")
        parts = []
        for a, b, kd in zip(mine, theirs, kinds):
            s = _add2_call(a.reshape(-1, a.shape[-1]), b.reshape(-1, b.shape[-1]), "pair_add")
            parts.append(s.reshape(a.shape))
        landed = _chip_exchange(parts, kinds, "chip_exchange")
        for i, buf in enumerate(landed):
            halves[i].append(_sum4_call(buf.reshape(-1, buf.shape[2]), "chip_sum"))
    big_grads = dict(zip(BIG_NAMES, _pair_gather(halves, n_layers, "pair_gather")))
    big_grads['w_in'] = big_grads['w_in'][:, :, :cs]

    small_grads = [jnp.stack([grads[l][nm].reshape(inputs[nm].shape[1:]) if nm != 'conv_w' else grads[l][nm]
                              for l in range(n_layers)]) for nm in SMALL_NAMES]
    summed = _unpack(_all_sum(_pack(small_grads + [loss_part]), "small_sum"), small_grads + [loss_part])
    loss = summed[-1][0, 0]
    small_g = dict(zip(SMALL_NAMES, summed[:-1]))
    small_g['conv_w'] = lax.dynamic_slice(small_g['conv_w'], (0, 0, chip * ccols), (n_layers, CONV_WIDTH, ccols))

    out = {}
    for nm in BIG_NAMES:
        g = big_grads[nm]
        two = lambda a: a.reshape(-1, a.shape[-1])
        delta, m1, v1 = _adamw_call(two(inputs[nm]), two(g), two(inputs['m_' + nm]), two(inputs['v_' + nm]), "adamw")
        out[nm] = (g, delta.reshape(g.shape), m1.reshape(g.shape), v1.reshape(g.shape))
    sw = [inputs[nm] for nm in SMALL_NAMES]
    packed = [_pack([small_g[nm] for nm in SMALL_NAMES])] + [_pack([inputs[pre + nm] for nm in SMALL_NAMES])
                                                            for pre in ('', 'm_', 'v_')]
    delta, m1, v1 = _adamw_call(packed[1], packed[0], packed[2], packed[3], "adamw_small")
    for nm, d_, m_, v_ in zip(SMALL_NAMES, _unpack(delta, sw), _unpack(m1, sw), _unpack(v1, sw)):
        out[nm] = (small_g[nm], d_, m_, v_)
    return (loss, dx[None], *[out[nm][0] for nm in WEIGHTS], *[out[nm][1] for nm in WEIGHTS],
            *[out[nm][2] for nm in WEIGHTS], *[out[nm][3] for nm in WEIGHTS])


def kernel(x, mem, ffn1_norm_pre, ffn1_w_gate_up, ffn1_w_down, ffn1_norm_post, mix_norm_pre, w_in, conv_w, a_log, dt_bias, sm_w, sm_b, sm_ln_g, sm_ln_b, dn_norm_w, w_out, mix_norm_post, xa_norm_pre, mem_norm, w_xq, w_xkv, w_xo, xa_norm_post, ffn2_norm_pre, ffn2_w_gate_up, ffn2_w_down, ffn2_norm_post, loss_target, m_ffn1_norm_pre, m_ffn1_w_gate_up, m_ffn1_w_down, m_ffn1_norm_post, m_mix_norm_pre, m_w_in, m_conv_w, m_a_log, m_dt_bias, m_sm_w, m_sm_b, m_sm_ln_g, m_sm_ln_b, m_dn_norm_w, m_w_out, m_mix_norm_post, m_xa_norm_pre, m_mem_norm, m_w_xq, m_w_xkv, m_w_xo, m_xa_norm_post, m_ffn2_norm_pre, m_ffn2_w_gate_up, m_ffn2_w_down, m_ffn2_norm_post, v_ffn1_norm_pre, v_ffn1_w_gate_up, v_ffn1_w_down, v_ffn1_norm_post, v_mix_norm_pre, v_w_in, v_conv_w, v_a_log, v_dt_bias, v_sm_w, v_sm_b, v_sm_ln_g, v_sm_ln_b, v_dn_norm_w, v_w_out, v_mix_norm_post, v_xa_norm_pre, v_mem_norm, v_w_xq, v_w_xkv, v_w_xo, v_xa_norm_post, v_ffn2_norm_pre, v_ffn2_w_gate_up, v_ffn2_w_down, v_ffn2_norm_post):
    vals = dict(locals())
    return _step(vals)
```

```python
import functools
import math

import jax
import jax.numpy as jnp
from jax import lax
from jax.experimental import pallas as pl
from jax.experimental.pallas import tpu as pltpu

F32, BF16 = jnp.float32, jnp.bfloat16
NORM_EPS = 1e-6
HEAD_DIM = 128
GM_CHUNK = 128
DN_CHUNK = 64
CONV_WIDTH = 4
XA_HEADS = 4
LANES = 128
N_CHIPS = 4
N_DEV = 8
VMEM_LIMIT = 56 * 1024 * 1024
MESH_IDS = pl.DeviceIdType.MESH
ADAM_LR, ADAM_B1, ADAM_B2, ADAM_EPS, ADAM_WD, ADAM_STEP = 0.001, 0.9, 0.999, 1e-08, 0.01, 10

WEIGHTS = ['ffn1_norm_pre', 'ffn1_w_gate_up', 'ffn1_w_down', 'ffn1_norm_post', 'mix_norm_pre', 'w_in', 'conv_w',
           'a_log', 'dt_bias', 'sm_w', 'sm_b', 'sm_ln_g', 'sm_ln_b', 'dn_norm_w', 'w_out', 'mix_norm_post',
           'xa_norm_pre', 'mem_norm', 'w_xq', 'w_xkv', 'w_xo', 'xa_norm_post', 'ffn2_norm_pre', 'ffn2_w_gate_up',
           'ffn2_w_down', 'ffn2_norm_post']
BIG = {'ffn1_w_gate_up': 'col', 'ffn1_w_down': 'row', 'w_in': 'col', 'w_out': 'row', 'w_xq': 'row',
       'w_xkv': 'col', 'w_xo': 'row', 'ffn2_w_gate_up': 'col', 'ffn2_w_down': 'row'}
BIG_NAMES = list(BIG)
SMALL_NAMES = [n for n in WEIGHTS if n not in BIG]


def _pcall(body, **kw):
    return pl.pallas_call(body, **kw)


def _params(sem=None):
    return pltpu.CompilerParams(dimension_semantics=sem, vmem_limit_bytes=VMEM_LIMIT)


def _divtile(dim, cap, align=LANES):
    if dim <= cap:
        return dim
    t = (cap // align) * align
    while t > align and dim % t:
        t -= align
    assert dim % t == 0, (dim, cap)
    return t


_DN = {'nn': (((1,), (0,)), ((), ())), 'nt': (((1,), (1,)), ((), ())), 'tn': (((0,), (0,)), ((), ()))}


def _mm(a, b, mode, out_dtype, name):
    if mode == 'nn':
        (m, k), (k2, n) = a.shape, b.shape
    elif mode == 'nt':
        (m, k), (n, k2) = a.shape, b.shape
    else:
        (k, m), (k2, n) = a.shape, b.shape
    assert k == k2, (a.shape, b.shape, mode)
    tm, tn, tk = _divtile(m, 1024), _divtile(n, 1024), _divtile(k, 512)
    nk = k // tk
    a_spec = {'nn': pl.BlockSpec((tm, tk), lambda i, j, kk: (i, kk)),
              'nt': pl.BlockSpec((tm, tk), lambda i, j, kk: (i, kk)),
              'tn': pl.BlockSpec((tk, tm), lambda i, j, kk: (kk, i))}[mode]
    b_spec = {'nn': pl.BlockSpec((tk, tn), lambda i, j, kk: (kk, j)),
              'nt': pl.BlockSpec((tn, tk), lambda i, j, kk: (j, kk)),
              'tn': pl.BlockSpec((tk, tn), lambda i, j, kk: (kk, j))}[mode]

    def body(a_ref, b_ref, o_ref, acc):
        kk = pl.program_id(2)

        @pl.when(kk == 0)
        def _():
            acc[...] = jnp.zeros_like(acc)

        acc[...] += lax.dot_general(a_ref[...].astype(BF16), b_ref[...].astype(BF16), _DN[mode],
                                    preferred_element_type=F32)

        @pl.when(kk == nk - 1)
        def _():
            o_ref[...] = acc[...].astype(o_ref.dtype)

    return _pcall(
        body, name=name, grid=(m // tm, n // tn, nk),
        in_specs=[a_spec, b_spec], out_specs=pl.BlockSpec((tm, tn), lambda i, j, kk: (i, j)),
        out_shape=jax.ShapeDtypeStruct((m, n), out_dtype),
        scratch_shapes=[pltpu.VMEM((tm, tn), F32)],
        compiler_params=_params(("parallel", "parallel", "arbitrary")),
    )(a, b)


def _rowcall(name, f, rows, params=(), row_outs=(), acc_outs=(), br=256, nrows=None):
    rows = [r if isinstance(r, tuple) else (r, 0) for r in rows]
    t = nrows if nrows is not None else rows[0][0].shape[0]
    br = min(br, t)
    assert t % br == 0, (name, t, br)
    n_in, n_ro = len(rows) + len(params), len(row_outs)

    def row_spec(arr, off):
        assert off % br == 0
        return pl.BlockSpec((br, arr.shape[1]), functools.partial(lambda i, o: (i + o, 0), o=off // br))

    def whole_spec(shape):
        return pl.BlockSpec(shape, functools.partial(lambda i, nd: (0,) * nd, nd=len(shape)))

    def body(*refs):
        res = f(*[r[...] for r in refs[:n_in]])
        outs = refs[n_in:]
        for o_ref, val in zip(outs[:n_ro], res[:n_ro]):
            o_ref[...] = val.astype(o_ref.dtype)
        if acc_outs:
            @pl.when(pl.program_id(0) == 0)
            def _():
                for o_ref in outs[n_ro:]:
                    o_ref[...] = jnp.zeros_like(o_ref)

            for o_ref, val in zip(outs[n_ro:], res[n_ro:]):
                o_ref[...] += val.astype(F32)

    out = _pcall(
        body, name=name, grid=(t // br,),
        in_specs=[row_spec(a, o) for a, o in rows] + [whole_spec(p.shape) for p in params],
        out_specs=[pl.BlockSpec((br, c), lambda i: (i, 0)) for c, _ in row_outs] + [whole_spec(s) for s in acc_outs],
        out_shape=[jax.ShapeDtypeStruct((t, c), dt) for c, dt in row_outs]
        + [jax.ShapeDtypeStruct(s, F32) for s in acc_outs],
        compiler_params=_params(("arbitrary",) if acc_outs else ("parallel",)),
    )(*[a for a, _ in rows], *params)
    return out


def _bdot_raw(a, b, dn):
    return lax.dot_general(a.astype(BF16), b.astype(BF16), _DN[dn], preferred_element_type=F32)


def _hdot_raw(a, b, dn):
    a_hi, b_hi = a.astype(BF16), b.astype(BF16)
    a_lo, b_lo = (a - a_hi.astype(F32)).astype(BF16), (b - b_hi.astype(F32)).astype(BF16)

    def dot(p, q):
        return lax.dot_general(p, q, _DN[dn], preferred_element_type=F32)

    return dot(a_hi, b_hi) + (dot(a_hi, b_lo) + dot(a_lo, b_hi))


def _with_vjp(raw):
    @functools.partial(jax.custom_vjp, nondiff_argnums=(2,))
    def dot(a, b, dn):
        return raw(a, b, dn)

    def fwd(a, b, dn):
        return raw(a, b, dn), (a, b)

    def bwd(dn, res, ct):
        a, b = res
        if dn == 'nn':
            da, db = raw(ct, b, 'nt'), raw(a, ct, 'tn')
        elif dn == 'nt':
            da, db = raw(ct, b, 'nn'), raw(ct, a, 'tn')
        else:
            da, db = raw(b, ct, 'nt'), raw(a, ct, 'nn')
        return da.astype(a.dtype), db.astype(b.dtype)

    dot.defvjp(fwd, bwd)
    return dot


_bdot, _hdot = _with_vjp(_bdot_raw), _with_vjp(_hdot_raw)


def _rms(x, g):
    return x * lax.rsqrt(jnp.mean(x * x, axis=-1, keepdims=True) + NORM_EPS) * g


def _sigmoid(x):
    return 1.0 / (1.0 + jnp.exp(-x))


def _silu(x):
    return x * _sigmoid(x)


def _gelu(x):
    return 0.5 * x * (1.0 + lax.erf(x * (2.0 ** -0.5)))


def _norm_fwd(x, g, name):
    return _rowcall(name, lambda x_, g_: (_rms(x_, g_),), [x], [g], [(x.shape[1], BF16)])[0]


def _resnorm_fwd(x, f, g, alpha, name):
    return _rowcall(name, lambda x_, f_, g_: (x_ + alpha * _rms(f_, g_),), [x, f], [g], [(x.shape[1], F32)])[0]


def _resnorm_bwd(f, dx, g, alpha, name):
    def fn(f_, dx_, g_):
        _, vjp = jax.vjp(lambda a, b: alpha * _rms(a, b), f_, g_)
        return vjp(dx_)

    return _rowcall(name, fn, [f, dx], [g], [(f.shape[1], BF16)], [g.shape])


def _norm_bwd(x, dh, dx_out, g, name):
    def fn(x_, dh_, dxo_, g_):
        _, vjp = jax.vjp(_rms, x_, g_)
        dx, dg = vjp(dh_)
        return dxo_ + dx, dg

    return _rowcall(name, fn, [x, dh, dx_out], [g], [(x.shape[1], F32)], [g.shape])


def _norm_bwd_gain(x, dh, g, name):
    def fn(x_, dh_, g_):
        _, vjp = jax.vjp(lambda b: _rms(x_, b), g_)
        return vjp(dh_)

    return _rowcall(name, fn, [x, dh], [g], [], [g.shape])[0]


def _swiglu_fwd(gu, name):
    ff = gu.shape[1] // 2
    return _rowcall(name, lambda gu_: (_silu(gu_[:, :ff]) * gu_[:, ff:],), [gu], [], [(ff, BF16)], br=64)[0]


def _swiglu_bwd(gu, da, name):
    ff = gu.shape[1] // 2

    def fn(gu_, da_):
        gate, up = gu_[:, :ff], gu_[:, ff:]
        s = _sigmoid(gate)
        dgate = da_ * up * (s * (1.0 + gate * (1.0 - s)))
        dup = da_ * (gate * s)
        return (jnp.concatenate([dgate, dup], axis=1),)

    return _rowcall(name, fn, [gu, da], [], [(2 * ff, BF16)], br=64)[0]


def _sgu_norm(v, lg, lb):
    vf = _gelu(v)
    mu = jnp.mean(vf, axis=-1, keepdims=True)
    var = jnp.mean(jnp.square(vf - mu), axis=-1, keepdims=True)
    return (vf - mu) * lax.rsqrt(var + NORM_EPS) * lg + lb


def _sgu_head(dot, u_h, vn_h, w_h, b_h):
    r = lax.broadcasted_iota(jnp.int32, w_h.shape, 0)
    c = lax.broadcasted_iota(jnp.int32, w_h.shape, 1)
    mixed = dot(jnp.where(r >= c, w_h, 0.0), vn_h, 'nn') + b_h
    return _gelu(u_h) * mixed


def _heads(width):
    return [slice(h * HEAD_DIM, (h + 1) * HEAD_DIM) for h in range(width // HEAD_DIM)]


def _sgu_fwd(u, v, w, bcol, lg, lb, name):
    def fn(u_, v_, w_, b_, lg_, lb_):
        vn = _sgu_norm(v_, lg_, lb_)
        return (jnp.concatenate([_sgu_head(_bdot_raw, u_[:, s], vn[:, s], w_[h], b_[h])
                                 for h, s in enumerate(_heads(u_.shape[1]))], axis=1),)

    return _rowcall(name, fn, [u, v], [w, bcol, lg, lb], [(u.shape[1], BF16)], br=GM_CHUNK)[0]


def _sgu_bwd(u, v, w, bcol, lg, lb, dy, name):
    def fn(u_, v_, dy_, w_, b_, lg_, lb_):
        vn, vjp_norm = jax.vjp(_sgu_norm, v_, lg_, lb_)
        du, dvn, dw, db = [], [], [], []
        for h, s in enumerate(_heads(u_.shape[1])):
            _, vjp = jax.vjp(functools.partial(_sgu_head, _bdot), u_[:, s], vn[:, s], w_[h], b_[h])
            a, b, c, d = vjp(dy_[:, s])
            du.append(a), dvn.append(b), dw.append(c[None]), db.append(d[None])
        dv, dlg, dlb = vjp_norm(jnp.concatenate(dvn, axis=1))
        return (jnp.concatenate(du, axis=1), dv, jnp.concatenate(dw, axis=0), jnp.concatenate(db, axis=0), dlg, dlb)

    wd = u.shape[1]
    return _rowcall(name, fn, [u, v, dy], [w, bcol, lg, lb], [(wd, F32), (wd, F32)],
                    [w.shape, bcol.shape, lg.shape, lb.shape], br=GM_CHUNK)


def _shift_down(x, s):
    if s == 0:
        return x
    rows = lax.broadcasted_iota(jnp.int32, x.shape, 0)
    return jnp.where(rows >= s, pltpu.roll(x, s, 0), 0.0)


def _shift_up(x, s):
    if s == 0:
        return x
    t = x.shape[0]
    rows = lax.broadcasted_iota(jnp.int32, x.shape, 0)
    return jnp.where(rows < t - s, pltpu.roll(x, t - s, 0), 0.0)


def _conv_pre(x, taps):
    acc = None
    for i in range(CONV_WIDTH):
        term = _shift_down(x, CONV_WIDTH - 1 - i) * taps[i]
        acc = term if acc is None else acc + term
    return acc


def _taps(w_ref):
    return [w_ref[i:i + 1, :] for i in range(CONV_WIDTH)]


def _conv_fwd(x, w, name):
    t, ch = x.shape
    cb = _divtile(ch, 512)

    def body(x_ref, w_ref, o_ref):
        o_ref[...] = _silu(_conv_pre(x_ref[...], _taps(w_ref)))

    return _pcall(body, name=name, grid=(ch // cb,),
                  in_specs=[pl.BlockSpec((t, cb), lambda j: (0, j)), pl.BlockSpec((CONV_WIDTH, cb), lambda j: (0, j))],
                  out_specs=pl.BlockSpec((t, cb), lambda j: (0, j)),
                  out_shape=jax.ShapeDtypeStruct((t, ch), F32), compiler_params=_params(("parallel",)))(x, w)


def _conv_bwd(x, w, dout, name):
    t, ch = x.shape
    cb = _divtile(ch, 256)

    def body(x_ref, w_ref, do_ref, dx_ref, dw_ref):
        xv, taps = x_ref[...], _taps(w_ref)
        pre = _conv_pre(xv, taps)
        s = _sigmoid(pre)
        dpre = do_ref[...] * (s * (1.0 + pre * (1.0 - s)))
        dx = None
        for i in range(CONV_WIDTH):
            sh = CONV_WIDTH - 1 - i
            term = _shift_up(dpre, sh) * taps[i]
            dx = term if dx is None else dx + term
            dw_ref[i:i + 1, :] = jnp.sum(dpre * _shift_down(xv, sh), axis=0, keepdims=True)
        dx_ref[...] = dx

    return _pcall(body, name=name, grid=(ch // cb,),
                  in_specs=[pl.BlockSpec((t, cb), lambda j: (0, j)), pl.BlockSpec((CONV_WIDTH, cb), lambda j: (0, j)),
                            pl.BlockSpec((t, cb), lambda j: (0, j))],
                  out_specs=[pl.BlockSpec((t, cb), lambda j: (0, j)), pl.BlockSpec((CONV_WIDTH, cb), lambda j: (0, j))],
                  out_shape=[jax.ShapeDtypeStruct((t, ch), F32), jax.ShapeDtypeStruct((CONV_WIDTH, ch), F32)],
                  compiler_params=_params(("parallel",)))(x, w, dout)


GATE_BLOCK = 256


def _gates(dot, b_raw, a_raw, a_log, dt_bias):
    blk = b_raw.shape[1]
    z = a_raw + dt_bias
    softplus = jnp.maximum(z, 0.0) + jnp.log(1.0 + jnp.exp(-jnp.abs(z)))
    g = -jnp.exp(a_log) * softplus
    ri = lax.broadcasted_iota(jnp.int32, (blk, blk), 0)
    ci = lax.broadcasted_iota(jnp.int32, (blk, blk), 1)
    upto = ((ri <= ci) & (ri // DN_CHUNK == ci // DN_CHUNK)).astype(F32)
    return _sigmoid(b_raw), dot(g, upto, 'nn')


def _gate_blocks(t):
    blk = min(GATE_BLOCK, t)
    return [slice(i, i + blk) for i in range(0, t, blk)]


def _whole_call(name, f, ins, out_shapes):
    n_in = len(ins)

    def body(*refs):
        for o_ref, val in zip(refs[n_in:], f(*[r[...] for r in refs[:n_in]])):
            o_ref[...] = val

    vmem = pl.BlockSpec(memory_space=pltpu.VMEM)
    return _pcall(body, name=name, in_specs=[vmem] * n_in, out_specs=[vmem] * len(out_shapes),
                  out_shape=[jax.ShapeDtypeStruct(s, F32) for s in out_shapes],
                  compiler_params=pltpu.CompilerParams(vmem_limit_bytes=VMEM_LIMIT))(*ins)


def _gates_fwd(b_raw, a_raw, a_log, dt_bias, name):
    def fn(b_, a_, al_, dt_):
        parts = [_gates(_hdot_raw, b_[:, s], a_[:, s], al_, dt_) for s in _gate_blocks(b_.shape[1])]
        return [jnp.concatenate(p, axis=1) for p in zip(*parts)]

    return _whole_call(name, fn, [b_raw, a_raw, a_log, dt_bias], [b_raw.shape, b_raw.shape])


def _gates_bwd(b_raw, a_raw, a_log, dt_bias, dbeta, dgcum, name):
    def fn(b_, a_, al_, dt_, dbeta_, dgcum_):
        db, da, dal, ddt = [], [], None, None
        for s in _gate_blocks(b_.shape[1]):
            _, vjp = jax.vjp(functools.partial(_gates, _hdot), b_[:, s], a_[:, s], al_, dt_)
            p, q, r, u = vjp((dbeta_[:, s], dgcum_[:, s]))
            db.append(p), da.append(q)
            dal, ddt = (r, u) if dal is None else (dal + r, ddt + u)
        return jnp.concatenate(db, axis=1), jnp.concatenate(da, axis=1), dal, ddt

    return _whole_call(name, fn, [b_raw, a_raw, a_log, dt_bias, dbeta, dgcum],
                       [b_raw.shape, a_raw.shape, a_log.shape, dt_bias.shape])


def _unit_lower_inverse_raw(a):
    c = a.shape[0]
    eye = (lax.broadcasted_iota(jnp.int32, (c, c), 0) == lax.broadcasted_iota(jnp.int32, (c, c), 1)).astype(F32)
    inv, p = eye - a, _hdot_raw(a, a, 'nn')
    n_fac = int(math.log2(c)) - 1
    for it in range(n_fac):
        inv = inv + _hdot_raw(inv, p, 'nn')
        if it < n_fac - 1:
            p = _hdot_raw(p, p, 'nn')
    return inv


@jax.custom_vjp
def _unit_lower_inverse(a):
    return _unit_lower_inverse_raw(a)


def _unit_lower_inverse_fwd(a):
    inv = _unit_lower_inverse_raw(a)
    return inv, inv


def _unit_lower_inverse_bwd(inv, ct):
    return (-_hdot_raw(_hdot_raw(inv, ct, 'tn'), inv, 'nt'),)


_unit_lower_inverse.defvjp(_unit_lower_inverse_fwd, _unit_lower_inverse_bwd)


def _halves_raw(x):
    return x[:, :x.shape[1] // 2], x[:, x.shape[1] // 2:]


@jax.custom_vjp
def _halves(x):
    return _halves_raw(x)


_halves.defvjp(lambda x: (_halves_raw(x), None), lambda _, ct: (jnp.concatenate(ct, axis=1),))

_DELTA_OPS = (_bdot_raw, _hdot_raw, _unit_lower_inverse_raw, _halves_raw)
_DELTA_OPS_VJP = (_bdot, _hdot, _unit_lower_inverse, _halves)


def _delta_chunk(ops, state, q, k, v, gc, gr, bc):
    bd, hd, inverse, halves = ops
    c, d = q.shape
    ri = lax.broadcasted_iota(jnp.int32, (c, c), 0)
    ci = lax.broadcasted_iota(jnp.int32, (c, c), 1)
    causal, strict = ri >= ci, ri > ci
    qn = q * lax.rsqrt(jnp.sum(q * q, axis=-1, keepdims=True) + NORM_EPS) * (d ** -0.5)
    kn = k * lax.rsqrt(jnp.sum(k * k, axis=-1, keepdims=True) + NORM_EPS)
    gcum = jnp.broadcast_to(gc, (c, d))
    diff = jnp.broadcast_to(gc, (c, c)) - jnp.broadcast_to(gr, (c, c))
    decay = jnp.where(causal, jnp.exp(jnp.where(causal, diff, 0.0)), 0.0)
    bb = jnp.broadcast_to(bc, (c, d))
    k_beta = kn * bb
    inv = inverse(jnp.where(strict, bd(k_beta, kn, 'nt') * decay, 0.0))
    uw = hd(inv, jnp.concatenate([v * bb, k_beta * jnp.exp(gcum)], axis=1), 'nn')
    u, w = halves(uw)
    qk =jnp.where(causal, bd(qn, kn, 'nt') * decay, 0.0)
    v_new = u - bd(w, state, 'nn')
    o = bd(qn * jnp.exp(gcum), state, 'nn') + bd(qk, v_new, 'nn')
    last = lax.broadcasted_iota(jnp.int32, (c, d), 0) == c - 1
    g_last = jnp.sum(jnp.where(last, gcum, 0.0), axis=0, keepdims=True)
    k_dec = kn * jnp.exp(g_last - gcum)
    return state * jnp.exp(g_last) + bd(k_dec, v_new, 'tn'), o


DN_HEADS_PER_STEP = 4


def _delta_specs(nh, nc, rev):
    c, d = DN_CHUNK, HEAD_DIM
    hb = min(DN_HEADS_PER_STEP, nh)
    ng = nh // hb
    ch = (lambda n: nc - 1 - n) if rev else (lambda n: n)
    qkv = [pl.BlockSpec((c, hb * d), functools.partial(lambda h, n, o: (ch(n), o * ng + h), o=o)) for o in range(3)]
    col = pl.BlockSpec((hb, c, 1), lambda h, n: (h, ch(n), 0))
    row = pl.BlockSpec((hb, None, 1, c), lambda h, n: (h, ch(n), 0, 0))
    st = pl.BlockSpec((hb, None, d, d), lambda h, n: (h, ch(n), 0, 0))
    tok = pl.BlockSpec((c, hb * d), lambda h, n: (ch(n), h))
    return hb, ng, qkv, col, row, st, tok


def _delta_fwd(qkv, gcol, grow, bcol, name):
    t = qkv.shape[0]
    nh, nc, d = gcol.shape[0], t // DN_CHUNK, HEAD_DIM
    hb, ng, qkv_s, col, row, st, tok = _delta_specs(nh, nc, False)

    def body(q_ref, k_ref, v_ref, gc_ref, gr_ref, bc_ref, o_ref, st_ref, state):
        @pl.when(pl.program_id(1) == 0)
        def _():
            state[...] = jnp.zeros_like(state)

        for j, s in enumerate(_heads(hb * d)):
            s0 = state[j]
            st_ref[j] = s0
            s1, o = _delta_chunk(_DELTA_OPS, s0, q_ref[:, s], k_ref[:, s], v_ref[:, s], gc_ref[j], gr_ref[j], bc_ref[j])
            o_ref[:, s] = o
            state[j] = s1

    return _pcall(body, name=name, grid=(ng, nc), in_specs=qkv_s + [col, row, col], out_specs=[tok, st],
                  out_shape=[jax.ShapeDtypeStruct((t, nh * d), F32), jax.ShapeDtypeStruct((nh, nc, d, d), F32)],
                  scratch_shapes=[pltpu.VMEM((hb, d, d), F32)],
                  compiler_params=_params(("parallel", "arbitrary")))(qkv, qkv, qkv, gcol, grow, bcol)


def _delta_bwd(qkv, gcol, grow, bcol, states, do, name):
    t = qkv.shape[0]
    nh, nc, d = gcol.shape[0], t // DN_CHUNK, HEAD_DIM
    hb, ng, qkv_s, col, row, st, tok = _delta_specs(nh, nc, True)

    def body(q_ref, k_ref, v_ref, gc_ref, gr_ref, bc_ref, st_ref, do_ref,
             dq_ref, dk_ref, dv_ref, dgc_ref, dgr_ref, dbc_ref, dstate):
        @pl.when(pl.program_id(1) == 0)
        def _():
            dstate[...] = jnp.zeros_like(dstate)

        for j, s in enumerate(_heads(hb * d)):
            _, vjp = jax.vjp(functools.partial(_delta_chunk, _DELTA_OPS_VJP), st_ref[j], q_ref[:, s], k_ref[:, s],
                             v_ref[:, s], gc_ref[j], gr_ref[j], bc_ref[j])
            ds, dq, dk, dv, dgc, dgr, dbc = vjp((dstate[j], do_ref[:, s]))
            dq_ref[:, s], dk_ref[:, s], dv_ref[:, s] = dq, dk, dv
            dgc_ref[j], dgr_ref[j], dbc_ref[j] = dgc, dgr, dbc
            dstate[j] = ds

    tok_out = jax.ShapeDtypeStruct((t, nh * d), F32)
    return _pcall(body, name=name, grid=(ng, nc), in_specs=qkv_s + [col, row, col, st, tok],
                  out_specs=[tok, tok, tok, col, row, col],
                  out_shape=[tok_out, tok_out, tok_out, jax.ShapeDtypeStruct(gcol.shape, F32),
                             jax.ShapeDtypeStruct(grow.shape, F32), jax.ShapeDtypeStruct(bcol.shape, F32)],
                  scratch_shapes=[pltpu.VMEM((hb, d, d), F32)],
                  compiler_params=_params(("parallel", "arbitrary")))(qkv, qkv, qkv, gcol, grow, bcol, states, do)


def _outgate_head(o_h, z_h, w):
    return _rms(o_h, w) * _silu(z_h)


def _outgate_fwd(o, z, w, name):
    def fn(o_, z_, w_):
        return (jnp.concatenate([_outgate_head(o_[:, s], z_[:, s], w_) for s in _heads(o_.shape[1])], axis=1),)

    return _rowcall(name, fn, [o, z], [w], [(o.shape[1], BF16)])[0]


def _outgate_bwd(o, z, w, dy, name):
    def fn(o_, z_, dy_, w_):
        do, dz, dw = [], [], None
        for s in _heads(o_.shape[1]):
            _, vjp = jax.vjp(_outgate_head, o_[:, s], z_[:, s], w_)
            a, b, c = vjp(dy_[:, s])
            do.append(a), dz.append(b)
            dw = c if dw is None else dw + c
        return jnp.concatenate(do, axis=1), jnp.concatenate(dz, axis=1), dw

    wd = o.shape[1]
    return _rowcall(name, fn, [o, z, dy], [w], [(wd, F32), (wd, F32)], [w.shape])


def _attn_head(dot, q_h, k_h, v_h):
    s = dot(q_h, k_h, 'nt') * (q_h.shape[1] ** -0.5)
    e = jnp.exp(s - lax.stop_gradient(jnp.max(s, axis=-1, keepdims=True)))
    p = e / jnp.sum(e, axis=-1, keepdims=True)
    return dot(p, v_h, 'nn')


def _xa_slices(width):
    hd = width // XA_HEADS
    return [slice(h * hd, (h + 1) * hd) for h in range(XA_HEADS)]


def _attn_fwd(q, kv, name):
    wd = q.shape[1]

    def fn(q_, kv_):
        k_, v_ = kv_[:, :wd], kv_[:, wd:]
        return (jnp.concatenate([_attn_head(_bdot_raw, q_[:, s], k_[:, s], v_[:, s]) for s in _xa_slices(wd)],
                                axis=1),)

    return _rowcall(name, fn, [q], [kv], [(wd, BF16)])[0]


def _attn_bwd(q, kv, do, name):
    wd = q.shape[1]

    def fn(q_, do_, kv_):
        k_, v_ = kv_[:, :wd].astype(F32), kv_[:, wd:].astype(F32)
        dq, dk, dv = [], [], []
        for s in _xa_slices(wd):
            _, vjp = jax.vjp(functools.partial(_attn_head, _bdot), q_[:, s].astype(F32), k_[:, s], v_[:, s])
            a, b, c = vjp(do_[:, s])
            dq.append(a), dk.append(b), dv.append(c)
        return jnp.concatenate(dq, axis=1), jnp.concatenate(dk + dv, axis=1)

    return _rowcall(name, fn, [q, do], [kv], [(wd, BF16)], [kv.shape])


def _loss_call(y, target, name):
    d = y.shape[1]

    def fn(y_, t_):
        err = y_ - t_
        part = 0.5 * jnp.sum(jnp.sum(err * err, axis=1, keepdims=True), axis=0, keepdims=True) / d
        return err / d, jnp.broadcast_to(part, (1, LANES))

    return _rowcall(name, fn, [y, target], [], [(d, F32)], [(1, LANES)])


def _adamw_call(w, g, m, v, name):
    def fn(w_, g_, m_, v_):
        m1 = ADAM_B1 * m_ + (1.0 - ADAM_B1) * g_
        v1 = ADAM_B2 * v_ + (1.0 - ADAM_B2) * jnp.square(g_)
        m_hat = m1 / (1.0 - ADAM_B1 ** ADAM_STEP)
        v_hat = v1 / (1.0 - ADAM_B2 ** ADAM_STEP)
        return -ADAM_LR * (m_hat / (jnp.sqrt(v_hat) + ADAM_EPS) + ADAM_WD * w_), m1, v1

    c = w.shape[1]
    return _rowcall(name, fn, [w, g, m, v], [], [(c, F32)] * 3, br=_divtile(w.shape[0], 128, 8))


def _pair_add(grad, theirs, core, name):
    s, hr, cols = theirs.shape
    br = _divtile(hr, 128, 16)
    nb = hr // br

    def body(c_ref, g_ref, t_ref, o_ref):
        o_ref[...] = (g_ref[...].astype(F32) + t_ref[...].astype(F32)).astype(o_ref.dtype)

    spec = pl.BlockSpec((None, br, cols), lambda j, i, c_ref: (j, i, 0))
    return _pcall(body, name=name, out_shape=jax.ShapeDtypeStruct(theirs.shape, BF16),
                  grid_spec=pltpu.PrefetchScalarGridSpec(
                      num_scalar_prefetch=1, grid=(s, nb),
                      in_specs=[pl.BlockSpec((None, br, cols), lambda j, i, c_ref: (j, c_ref[0] * nb + i, 0)), spec],
                      out_specs=spec),
                  compiler_params=_params(("parallel", "parallel")))(core, grad, theirs)


def _chip_sum(part, landed, kind, chip, name):
    _, hr, cw = landed.shape
    br = _divtile(hr, 128, 16)

    def body(c_ref, p_ref, a_ref, b_ref, d_ref, o_ref):
        o_ref[...] = ((p_ref[...].astype(F32) + a_ref[...].astype(F32)) + b_ref[...].astype(F32)) + d_ref[...].astype(F32)

    if kind == 'row':
        own = pl.BlockSpec((None, br, cw), lambda i, c_ref: (c_ref[0], i, 0))
    else:
        own = pl.BlockSpec((br, cw), lambda i, c_ref: (i, c_ref[0]))
    got = [pl.BlockSpec((None, br, cw), functools.partial(lambda i, c_ref, k: (k, i, 0), k=k)) for k in range(3)]
    return _pcall(body, name=name, out_shape=jax.ShapeDtypeStruct((hr, cw), F32),
                  grid_spec=pltpu.PrefetchScalarGridSpec(
                      num_scalar_prefetch=1, grid=(hr // br,), in_specs=[own] + got,
                      out_specs=pl.BlockSpec((br, cw), lambda i, c_ref: (i, 0))),
                  compiler_params=_params(("parallel",)))(chip, part, landed, landed, landed)


def _position():
    x, y, c = lax.axis_index("x"), lax.axis_index("y"), lax.axis_index("c")
    others = [(1 - x, y), (x, 1 - y), (1 - x, 1 - y)]
    return x, y, c, others


def _any_specs(n):
    return [pl.BlockSpec(memory_space=pl.ANY)] * n


def _ds(start, size):
    return pl.ds(pl.multiple_of(start, size), size)


def _gather_weights(shards, kinds, name):
    n = len(shards)

    def body(*refs):
        src, dst = refs[:n], refs[n:2 * n]
        send, recv = refs[2 * n:]
        x, y, c, others = _position()
        me, sib = 2 * x + y, (x, y, 1 - c)

        def region(w, chip, half):
            hr, cw = src[w].shape[0] // 2, src[w].shape[1]
            if kinds[w] == 'row':
                return dst[w].at[chip, _ds(half * hr, hr), :]
            return dst[w].at[_ds(half * hr, hr), _ds(chip * cw, cw)]

        def copy(w, sem, s, chip, half, to):
            return pltpu.make_async_remote_copy(src_ref=s, dst_ref=region(w, chip, half), send_sem=send.at[6 * w + sem],
                                                recv_sem=recv.at[6 * w + sem], device_id=to, device_id_type=MESH_IDS)

        first = [copy(w, k, src[w].at[_ds(c * (src[w].shape[0] // 2), src[w].shape[0] // 2), :], me, c, (px, py, c))
                 for w in range(n) for k, (px, py) in enumerate(others)]
        for cp in first:
            cp.start()
        passed = []
        for w in range(n):
            for k, (px, py) in enumerate(others):
                theirs = region(w, 2 * px + py, c)
                copy(w, k, theirs, 2 * px + py, c, sib).wait_recv()
                passed.append(copy(w, 3 + k, theirs, 2 * px + py, c, sib))
                passed[-1].start()
        for w in range(n):
            for k, (px, py) in enumerate(others):
                theirs = region(w, 2 * px + py, 1 - c)
                copy(w, 3 + k, theirs, 2 * px + py, 1 - c, sib).wait_recv()
        for cp in first + passed:
            cp.wait_send()

    def full_shape(s, kind):
        return (N_CHIPS, *s.shape) if kind == 'row' else (s.shape[0], N_CHIPS * s.shape[1])

    got = _pcall(body, name=name, in_specs=_any_specs(n), out_specs=_any_specs(n),
                 out_shape=[jax.ShapeDtypeStruct(full_shape(s, kd), s.dtype) for s, kd in zip(shards, kinds)],
                 scratch_shapes=[pltpu.SemaphoreType.DMA((6 * n,)), pltpu.SemaphoreType.DMA((6 * n,))])(*shards)
    chip = 2 * lax.axis_index("x") + lax.axis_index("y")
    return [lax.dynamic_update_slice(g, s[None], (chip, 0, 0)) if kd == 'row'
            else lax.dynamic_update_slice(g, s, (0, chip * s.shape[1])) for g, s, kd in zip(got, shards, kinds)]


def _pair_exchange(grads, kinds, name):
    n = len(grads)

    def body(*refs):
        src, theirs_ref = refs[:n], refs[n:2 * n]
        send, recv = refs[2 * n:]
        x, y, c, _ = _position()
        sib = (x, y, 1 - c)

        def half(w, h):
            if kinds[w] == 'row':
                hr = src[w].shape[1] // 2
                return src[w].at[:, _ds(h * hr, hr), :]
            hr = src[w].shape[0] // 2
            return src[w].at[_ds(h * hr, hr), :]

        give = [pltpu.make_async_remote_copy(src_ref=half(w, 1 - c), dst_ref=theirs_ref[w], send_sem=send.at[w],
                                             recv_sem=recv.at[w], device_id=sib, device_id_type=MESH_IDS)
                for w in range(n)]
        for cp in give:
            cp.start()
        for cp in give:
            cp.wait()

    def half_shape(g, kind):
        return (g.shape[0], g.shape[1] // 2, g.shape[2]) if kind == 'row' else (g.shape[0] // 2, g.shape[1])

    return _pcall(body, name=name, in_specs=_any_specs(n), out_specs=_any_specs(n),
                  out_shape=[jax.ShapeDtypeStruct(half_shape(g, kd), g.dtype) for g, kd in zip(grads, kinds)],
                  scratch_shapes=[pltpu.SemaphoreType.DMA((n,)), pltpu.SemaphoreType.DMA((n,))])(*grads)


def _chip_exchange(parts, kinds, name):
    n = len(parts)

    def dims(w):
        p = parts[w]
        return (p.shape[1], p.shape[2]) if kinds[w] == 'row' else (p.shape[0], p.shape[1] // N_CHIPS)

    def body(*refs):
        src, dst = refs[:n], refs[n:2 * n]
        send, recv = refs[2 * n:]
        x, y, c, others = _position()

        def slot(w, chip):
            if kinds[w] == 'row':
                return src[w].at[chip]
            return src[w].at[:, _ds(chip * dims(w)[1], dims(w)[1])]

        give = [pltpu.make_async_remote_copy(src_ref=slot(w, 2 * px + py), dst_ref=dst[w].at[k],
                                             send_sem=send.at[3 * w + k], recv_sem=recv.at[3 * w + k],
                                             device_id=(px, py, c), device_id_type=MESH_IDS)
                for w in range(n) for k, (px, py) in enumerate(others)]
        for cp in give:
            cp.start()
        for cp in give:
            cp.wait()

    return _pcall(body, name=name, in_specs=_any_specs(n), out_specs=_any_specs(n),
                  out_shape=[jax.ShapeDtypeStruct((3, *dims(w)), parts[w].dtype) for w in range(n)],
                  scratch_shapes=[pltpu.SemaphoreType.DMA((3 * n,)), pltpu.SemaphoreType.DMA((3 * n,))])(*parts)


def _pair_gather(halves, n_layers, name):
    n = len(halves)

    def body(*refs):
        src = [refs[w * n_layers:(w + 1) * n_layers] for w in range(n)]
        dst = refs[n * n_layers:n * n_layers + n]
        send, recv = refs[n * n_layers + n:]
        x, y, c, _ = _position()
        sib = (x, y, 1 - c)
        give, take = [], []
        for w in range(n):
            hr = src[w][0].shape[0]
            for l in range(n_layers):
                i = w * n_layers + l
                give.append(pltpu.make_async_remote_copy(
                    src_ref=src[w][l], dst_ref=dst[w].at[l, _ds(c * hr, hr), :], send_sem=send.at[i],
                    recv_sem=recv.at[i], device_id=sib, device_id_type=MESH_IDS))
                take.append(pltpu.make_async_remote_copy(
                    src_ref=src[w][l], dst_ref=dst[w].at[l, _ds((1 - c) * hr, hr), :], send_sem=send.at[i],
                    recv_sem=recv.at[i], device_id=sib, device_id_type=MESH_IDS))
        for cp in give:
            cp.start()
        for cp in take:
            cp.wait_recv()
        for cp in give:
            cp.wait_send()

    flat = [h for hs in halves for h in hs]
    m = len(flat)
    got = _pcall(body, name=name, in_specs=_any_specs(m), out_specs=_any_specs(n),
                 out_shape=[jax.ShapeDtypeStruct((n_layers, 2 * hs[0].shape[0], hs[0].shape[1]), F32) for hs in halves],
                 scratch_shapes=[pltpu.SemaphoreType.DMA((m,)), pltpu.SemaphoreType.DMA((m,))])(*flat)
    core = lax.axis_index("c")
    out = []
    for g, hs in zip(got, halves):
        for l, h in enumerate(hs):
            g = lax.dynamic_update_slice(g, h[None], (l, core * h.shape[0], 0))
        out.append(g)
    return out


def _all_sum(buf, name):
    rows = buf.shape[0]
    flips = [(fx, fy, fc) for fx in (0, 1) for fy in (0, 1) for fc in (0, 1)][1:]

    def body(x_ref, o_ref, land, send, recv):
        x, y, c, _ = _position()
        me = 4 * x + 2 * y + c

        def peer(f):
            return (1 - x if f[0] else x, 1 - y if f[1] else y, 1 - c if f[2] else c)

        give = [pltpu.make_async_remote_copy(src_ref=x_ref, dst_ref=land.at[me], send_sem=send.at[k],
                                             recv_sem=recv.at[k], device_id=peer(f), device_id_type=MESH_IDS)
                for k, f in enumerate(flips)]
        for cp in give:
            cp.start()
        land[me] = x_ref[...]
        for k, f in enumerate(flips):
            px, py, pc = peer(f)
            slot = land.at[4 * px + 2 * py + pc]
            pltpu.make_async_remote_copy(src_ref=slot, dst_ref=slot, send_sem=send.at[k], recv_sem=recv.at[k],
                                         device_id=peer(f), device_id_type=MESH_IDS).wait_recv()
        for cp in give:
            cp.wait_send()
        total = land[0]
        for dev in range(1, N_DEV):
            total = total + land[dev]
        o_ref[...] = total

    return _pcall(body, name=name, in_specs=[pl.BlockSpec(memory_space=pltpu.VMEM)],
                  out_specs=pl.BlockSpec(memory_space=pltpu.VMEM), out_shape=jax.ShapeDtypeStruct(buf.shape, F32),
                  scratch_shapes=[pltpu.VMEM((N_DEV, rows, LANES), F32), pltpu.SemaphoreType.DMA((N_DEV - 1,)),
                                  pltpu.SemaphoreType.DMA((N_DEV - 1,))],
                  compiler_params=pltpu.CompilerParams(vmem_limit_bytes=VMEM_LIMIT))(buf)


def _pack(arrays):
    flat = jnp.concatenate([a.reshape(-1).astype(F32) for a in arrays])
    pad = (-flat.shape[0]) % (8 * LANES)
    return jnp.pad(flat, (0, pad)).reshape(-1, LANES)


def _unpack(buf, like):
    flat, out, off = buf.reshape(-1), [], 0
    for a in like:
        out.append(flat[off:off + a.size].reshape(a.shape))
        off += a.size
    return out


def _row2(v):
    return v.reshape(1, -1)


def _ffn_fwd(x, g_pre, w_gu, w_d, g_post, tag):
    h = _norm_fwd(x, g_pre, tag + "_pre")
    gu = _mm(h, w_gu, 'nn', F32, tag + "_gu")
    a = _swiglu_fwd(gu, tag + "_act")
    f = _mm(a, w_d, 'nn', F32, tag + "_down")
    return _resnorm_fwd(x, f, g_post, 0.5, tag + "_post"), (x, h, gu, a, f)


def _ffn_bwd(dx_out, saved, g_pre, w_gu, w_d, g_post, tag):
    x, h, gu, a, f = saved
    df, dg_post = _resnorm_bwd(f, dx_out, g_post, 0.5, tag + "_post_b")
    da = _mm(df, w_d, 'nt', F32, tag + "_down_bx")
    dw_d = _mm(a, df, 'tn', BF16, tag + "_down_bw")
    dgu = _swiglu_bwd(gu, da, tag + "_act_b")
    dh = _mm(dgu, w_gu, 'nt', F32, tag + "_gu_bx")
    dw_gu = _mm(h, dgu, 'tn', BF16, tag + "_gu_bw")
    dx, dg_pre = _norm_bwd(x, dh, dx_out, g_pre, tag + "_pre_b")
    return dx, dg_pre, dw_gu, dw_d, dg_post


def _split_cols(pp, cs, cp, widths):
    proj = jnp.concatenate([pp[:, j * cp:j * cp + cs] for j in range(N_CHIPS)], axis=1)
    out, off = [], 0
    for wd in widths:
        out.append(proj[:, off:off + wd])
        off += wd
    return out


def _join_cols(pieces, cs, cp):
    proj = jnp.concatenate(pieces, axis=1)
    t = proj.shape[0]
    cols = []
    for j in range(N_CHIPS):
        cols += [proj[:, j * cs:(j + 1) * cs], jnp.zeros((t, cp - cs), proj.dtype)]
    return jnp.concatenate(cols, axis=1)


def _mix_fwd(x, p, w_in, w_out, conv_w, cs, cp, tag):
    t, d = x.shape
    half = d // 2
    nh = half // HEAD_DIM
    h = _norm_fwd(x, p['mix_norm_pre'], tag + "_pre")
    pp = _mm(h, w_in, 'nn', F32, tag + "_in")
    u_a, v_a, qkv_raw, z, b_raw, a_raw = _split_cols(pp, cs, cp, [half, half, 3 * half, half, nh, nh])
    y_a = _sgu_fwd(u_a, v_a, p['sm_w'], p['sm_b'], p['sm_ln_g'], p['sm_ln_b'], tag + "_sgu")
    qkv = _conv_fwd(qkv_raw, conv_w, tag + "_conv")
    b_raw, a_raw = b_raw.T, a_raw.T
    beta, gcum = _gates_fwd(b_raw, a_raw, p['a_log'].T, p['dt_bias'].T, tag + "_gates")
    gcol, bcol = gcum[:, :, None], beta[:, :, None]
    grow = gcum.reshape(nh, t // DN_CHUNK, 1, DN_CHUNK)
    o, states = _delta_fwd(qkv, gcol, grow, bcol, tag + "_delta")
    y_b = _outgate_fwd(o, z, p['dn_norm_w'], tag + "_gate")
    y = jnp.concatenate([y_a, y_b], axis=1)
    m = _mm(y, w_out, 'nn', F32, tag + "_out")
    x_out = _resnorm_fwd(x, m, p['mix_norm_post'], 1.0, tag + "_post")
    return x_out, (x, h, u_a, v_a, qkv_raw, z, b_raw, a_raw, qkv, gcol, grow, bcol, states, o, y, m)


def _mix_bwd(dx_out, saved, p, w_in, w_out, conv_w, cs, cp, tag):
    x, h, u_a, v_a, qkv_raw, z, b_raw, a_raw, qkv, gcol, grow, bcol, states, o, y, m = saved
    t, d = x.shape
    half = d // 2
    nh = half // HEAD_DIM
    gr = {}
    dm, gr['mix_norm_post'] = _resnorm_bwd(m, dx_out, p['mix_norm_post'], 1.0, tag + "_post_b")
    dy = _mm(dm, w_out, 'nt', F32, tag + "_out_bx")
    gr['w_out'] = _mm(y, dm, 'tn', BF16, tag + "_out_bw")
    dy_a, dy_b = dy[:, :half], dy[:, half:]
    do, dz, gr['dn_norm_w'] = _outgate_bwd(o, z, p['dn_norm_w'], dy_b, tag + "_gate_b")
    dq, dk, dv, dgcol, dgrow, dbcol = _delta_bwd(qkv, gcol, grow, bcol, states, do, tag + "_delta_b")
    dgcum = dgcol[:, :, 0] + dgrow.reshape(nh, t)
    db_raw, da_raw, gr['a_log'], gr['dt_bias'] = _gates_bwd(b_raw, a_raw, p['a_log'].T, p['dt_bias'].T,
                                                             dbcol[:, :, 0], dgcum, tag + "_gates_b")
    db_raw, da_raw = db_raw.T, da_raw.T
    dqkv_raw, gr['conv_w'] = _conv_bwd(qkv_raw, conv_w, jnp.concatenate([dq, dk, dv], axis=1), tag + "_conv_b")
    du_a, dv_a, gr['sm_w'], gr['sm_b'], gr['sm_ln_g'], gr['sm_ln_b'] = _sgu_bwd(
        u_a, v_a, p['sm_w'], p['sm_b'], p['sm_ln_g'], p['sm_ln_b'], dy_a, tag + "_sgu_b")
    dpp = _join_cols([du_a, dv_a, dqkv_raw, dz, db_raw, da_raw], cs, cp).astype(BF16)
    dh = _mm(dpp, w_in, 'nt', F32, tag + "_in_bx")
    gr['w_in'] = _mm(h, dpp, 'tn', BF16, tag + "_in_bw")
    dx, gr['mix_norm_pre'] = _norm_bwd(x, dh, dx_out, p['mix_norm_pre'], tag + "_pre_b")
    return dx, gr


def _xa_fwd(x, mem, p, w_xq, w_xkv, w_xo, tag):
    h = _norm_fwd(x, p['xa_norm_pre'], tag + "_pre")
    mh = _norm_fwd(mem, p['mem_norm'], tag + "_mem")
    q = _mm(h, w_xq, 'nn', BF16, tag + "_q")
    kv = _mm(mh, w_xkv, 'nn', BF16, tag + "_kv")
    o = _attn_fwd(q, kv, tag + "_attn")
    c = _mm(o, w_xo, 'nn', F32, tag + "_o")
    return _resnorm_fwd(x, c, p['xa_norm_post'], 1.0, tag + "_post"), (x, h, mh, q, kv, o, c)


def _xa_bwd(dx_out, saved, mem, p, w_xq, w_xkv, w_xo, tag):
    x, h, mh, q, kv, o, c = saved
    gr = {}
    dc, gr['xa_norm_post'] = _resnorm_bwd(c, dx_out, p['xa_norm_post'], 1.0, tag + "_post_b")
    do = _mm(dc, w_xo, 'nt', F32, tag + "_o_bx")
    gr['w_xo'] = _mm(o, dc, 'tn', BF16, tag + "_o_bw")
    dq, dkv = _attn_bwd(q, kv, do, tag + "_attn_b")
    dkv = dkv.astype(BF16)
    dh = _mm(dq, w_xq, 'nt', F32, tag + "_q_bx")
    gr['w_xq'] = _mm(h, dq, 'tn', BF16, tag + "_q_bw")
    dmh = _mm(dkv, w_xkv, 'nt', F32, tag + "_kv_bx")
    gr['w_xkv'] = _mm(mh, dkv, 'tn', BF16, tag + "_kv_bw")
    gr['mem_norm'] = _norm_bwd_gain(mem, dmh, p['mem_norm'], tag + "_mem_b")
    dx, gr['xa_norm_pre'] = _norm_bwd(x, dh, dx_out, p['xa_norm_pre'], tag + "_pre_b")
    return dx, gr


def _step(inputs):
    x, mem, target = inputs['x'][0], inputs['mem'][0], inputs['loss_target'][0]
    n_layers = inputs['w_in'].shape[0]
    cx, cy, cc = lax.axis_index("x"), lax.axis_index("y"), lax.axis_index("c")
    chip = 2 * cx + cy
    cs = inputs['w_in'].shape[2]
    cp = -(-cs // LANES) * LANES
    kinds = [BIG[nm] for nm in BIG_NAMES]

    conv_local = inputs['conv_w']
    ccols = conv_local.shape[2]
    placed = lax.dynamic_update_slice(jnp.zeros((n_layers, CONV_WIDTH, N_CHIPS * ccols), F32),
                                      jnp.where(cc == 0, conv_local, 0.0), (0, 0, chip * ccols))
    conv_full = _unpack(_all_sum(_pack([placed]), "conv_gather"), [placed])[0]

    full = []
    for l in range(n_layers):
        shards = []
        for nm in BIG_NAMES:
            w = inputs[nm][l].astype(BF16)
            if nm == 'w_in':
                w = jnp.pad(w, ((0, 0), (0, cp - cs)))
            shards.append(w)
        got = _gather_weights(shards, kinds, "gather_weights")
        full.append({nm: (g.reshape(-1, g.shape[2]) if kd == 'row' else g)
                     for nm, g, kd in zip(BIG_NAMES, got, kinds)})

    def small(l):
        p = {nm: inputs[nm][l] for nm in SMALL_NAMES}
        for nm in SMALL_NAMES:
            if p[nm].ndim == 1:
                p[nm] = _row2(p[nm])
        p['sm_b'] = p['sm_b'][:, :, None]
        return p

    saved = []
    for l in range(n_layers):
        p, w, tag = small(l), full[l], ""
        x, s1 = _ffn_fwd(x, p['ffn1_norm_pre'], w['ffn1_w_gate_up'], w['ffn1_w_down'], p['ffn1_norm_post'], "ffn")
        x, s2 = _mix_fwd(x, p, w['w_in'], w['w_out'], conv_full[l], cs, cp, "mix")
        x, s3 = _xa_fwd(x, mem, p, w['w_xq'], w['w_xkv'], w['w_xo'], "xa")
        x, s4 = _ffn_fwd(x, p['ffn2_norm_pre'], w['ffn2_w_gate_up'], w['ffn2_w_down'], p['ffn2_norm_post'], "ffn")
        saved.append((s1, s2, s3, s4))

    dx, loss_part = _loss_call(x, target, "loss")

    grads = [None] * n_layers
    for l in reversed(range(n_layers)):
        p, w = small(l), full[l]
        s1, s2, s3, s4 = saved[l]
        gr = {}
        dx, gr['ffn2_norm_pre'], gr['ffn2_w_gate_up'], gr['ffn2_w_down'], gr['ffn2_norm_post'] = _ffn_bwd(
            dx, s4, p['ffn2_norm_pre'], w['ffn2_w_gate_up'], w['ffn2_w_down'], p['ffn2_norm_post'], "ffn")
        dx, g3 = _xa_bwd(dx, s3, mem, p, w['w_xq'], w['w_xkv'], w['w_xo'], "xa")
        dx, g2 = _mix_bwd(dx, s2, p, w['w_in'], w['w_out'], conv_full[l], cs, cp, "mix")
        dx, gr['ffn1_norm_pre'], gr['ffn1_w_gate_up'], gr['ffn1_w_down'], gr['ffn1_norm_post'] = _ffn_bwd(
            dx, s1, p['ffn1_norm_pre'], w['ffn1_w_gate_up'], w['ffn1_w_down'], p['ffn1_norm_post'], "ffn")
        gr.update(g3), gr.update(g2)
        grads[l] = gr

    halves = [[] for _ in BIG_NAMES]
    core_idx, chip_idx = cc.astype(jnp.int32).reshape(1), chip.astype(jnp.int32).reshape(1)
    for l in range(n_layers):
        gs = []
        for nm, kd in zip(BIG_NAMES, kinds):
            g = grads[l][nm]
            gs.append(g.reshape(N_CHIPS, -1, g.shape[1]) if kd == 'row' else g)
        theirs = _pair_exchange(gs, kinds, "pair_exchange")
        parts = []
        for g, t, kd in zip(gs, theirs, kinds):
            if kd == 'row':
                parts.append(_pair_add(g, t, core_idx, "pair_add"))
            else:
                parts.append(_pair_add(g[None], t[None], core_idx, "pair_add")[0])
        landed = _chip_exchange(parts, kinds, "chip_exchange")
        for i, (part, buf, kd) in enumerate(zip(parts, landed, kinds)):
            halves[i].append(_chip_sum(part, buf, kd, chip_idx, "chip_sum"))
    big_grads = dict(zip(BIG_NAMES, _pair_gather(halves, n_layers, "pair_gather")))
    big_grads['w_in'] = big_grads['w_in'][:, :, :cs]

    small_grads = [jnp.stack([grads[l][nm].reshape(inputs[nm].shape[1:]) if nm != 'conv_w' else grads[l][nm]
                              for l in range(n_layers)]) for nm in SMALL_NAMES]
    summed = _unpack(_all_sum(_pack(small_grads + [loss_part]), "small_sum"), small_grads + [loss_part])
    loss = summed[-1][0, 0]
    small_g = dict(zip(SMALL_NAMES, summed[:-1]))
    small_g['conv_w'] = lax.dynamic_slice(small_g['conv_w'], (0, 0, chip * ccols), (n_layers, CONV_WIDTH, ccols))

    out = {}
    for nm in BIG_NAMES:
        g = big_grads[nm]
        two = lambda a: a.reshape(-1, a.shape[-1])
        delta, m1, v1 = _adamw_call(two(inputs[nm]), two(g), two(inputs['m_' + nm]), two(inputs['v_' + nm]), "adamw")
        out[nm] = (g, delta.reshape(g.shape), m1.reshape(g.shape), v1.reshape(g.shape))
    sw = [inputs[nm] for nm in SMALL_NAMES]
    packed = [_pack([small_g[nm] for nm in SMALL_NAMES])] + [_pack([inputs[pre + nm] for nm in SMALL_NAMES])
                                                            for pre in ('', 'm_', 'v_')]
    delta, m1, v1 = _adamw_call(packed[1], packed[0], packed[2], packed[3], "adamw_small")
    for nm, d_, m_, v_ in zip(SMALL_NAMES, _unpack(delta, sw), _unpack(m1, sw), _unpack(v1, sw)):
        out[nm] = (small_g[nm], d_, m_, v_)
    return (loss, dx[None], *[out[nm][0] for nm in WEIGHTS], *[out[nm][1] for nm in WEIGHTS],
            *[out[nm][2] for nm in WEIGHTS], *[out[nm][3] for nm in WEIGHTS])


def kernel(x, mem, ffn1_norm_pre, ffn1_w_gate_up, ffn1_w_down, ffn1_norm_post, mix_norm_pre, w_in, conv_w, a_log, dt_bias, sm_w, sm_b, sm_ln_g, sm_ln_b, dn_norm_w, w_out, mix_norm_post, xa_norm_pre, mem_norm, w_xq, w_xkv, w_xo, xa_norm_post, ffn2_norm_pre, ffn2_w_gate_up, ffn2_w_down, ffn2_norm_post, loss_target, m_ffn1_norm_pre, m_ffn1_w_gate_up, m_ffn1_w_down, m_ffn1_norm_post, m_mix_norm_pre, m_w_in, m_conv_w, m_a_log, m_dt_bias, m_sm_w, m_sm_b, m_sm_ln_g, m_sm_ln_b, m_dn_norm_w, m_w_out, m_mix_norm_post, m_xa_norm_pre, m_mem_norm, m_w_xq, m_w_xkv, m_w_xo, m_xa_norm_post, m_ffn2_norm_pre, m_ffn2_w_gate_up, m_ffn2_w_down, m_ffn2_norm_post, v_ffn1_norm_pre, v_ffn1_w_gate_up, v_ffn1_w_down, v_ffn1_norm_post, v_mix_norm_pre, v_w_in, v_conv_w, v_a_log, v_dt_bias, v_sm_w, v_sm_b, v_sm_ln_g, v_sm_ln_b, v_dn_norm_w, v_w_out, v_mix_norm_post, v_xa_norm_pre, v_mem_norm, v_w_xq, v_w_xkv, v_w_xo, v_xa_norm_post, v_ffn2_norm_pre, v_ffn2_w_gate_up, v_ffn2_w_down, v_ffn2_norm_post):
    vals = dict(locals())
    return _step(vals)
```

```python
import functools
import math

import jax
import jax.numpy as jnp
from jax import lax
from jax.experimental import pallas as pl
from jax.experimental.pallas import tpu as pltpu

F32, BF16 = jnp.float32, jnp.bfloat16
NORM_EPS = 1e-6
HEAD_DIM = 128
GM_CHUNK = 128
DN_CHUNK = 64
CONV_WIDTH = 4
XA_HEADS = 4
LANES = 128
N_CHIPS = 4
N_DEV = 8
VMEM_LIMIT = 56 * 1024 * 1024
MESH_IDS = pl.DeviceIdType.MESH
ADAM_LR, ADAM_B1, ADAM_B2, ADAM_EPS, ADAM_WD, ADAM_STEP = 0.001, 0.9, 0.999, 1e-08, 0.01, 10

WEIGHTS = ['ffn1_norm_pre', 'ffn1_w_gate_up', 'ffn1_w_down', 'ffn1_norm_post', 'mix_norm_pre', 'w_in', 'conv_w',
           'a_log', 'dt_bias', 'sm_w', 'sm_b', 'sm_ln_g', 'sm_ln_b', 'dn_norm_w', 'w_out', 'mix_norm_post',
           'xa_norm_pre', 'mem_norm', 'w_xq', 'w_xkv', 'w_xo', 'xa_norm_post', 'ffn2_norm_pre', 'ffn2_w_gate_up',
           'ffn2_w_down', 'ffn2_norm_post']
BIG = {'ffn1_w_gate_up': 'col', 'ffn1_w_down': 'row', 'w_in': 'col', 'w_out': 'row', 'w_xq': 'row',
       'w_xkv': 'col', 'w_xo': 'row', 'ffn2_w_gate_up': 'col', 'ffn2_w_down': 'row'}
BIG_NAMES = list(BIG)
GROUPS = (('ffn1', ('ffn1_w_gate_up', 'ffn1_w_down')), ('mix', ('w_in', 'w_out')),
          ('xa', ('w_xq', 'w_xkv', 'w_xo')), ('ffn2', ('ffn2_w_gate_up', 'ffn2_w_down')))
SMALL_NAMES = [n for n in WEIGHTS if n not in BIG]


def _pcall(body, **kw):
    return pl.pallas_call(body, **kw)


def _params(sem=None):
    return pltpu.CompilerParams(dimension_semantics=sem, vmem_limit_bytes=VMEM_LIMIT)


def _divtile(dim, cap, align=LANES):
    if dim <= cap:
        return dim
    t = (cap // align) * align
    while t > align and dim % t:
        t -= align
    assert dim % t == 0, (dim, cap)
    return t


_DN = {'nn': (((1,), (0,)), ((), ())), 'nt': (((1,), (1,)), ((), ())), 'tn': (((0,), (0,)), ((), ()))}


def _mm(a, b, mode, out_dtype, name):
    if mode == 'nn':
        (m, k), (k2, n) = a.shape, b.shape
    elif mode == 'nt':
        (m, k), (n, k2) = a.shape, b.shape
    else:
        (k, m), (k2, n) = a.shape, b.shape
    assert k == k2, (a.shape, b.shape, mode)
    tm, tn, tk = _divtile(m, 1024), _divtile(n, 1024), _divtile(k, 512)
    nk = k // tk
    a_spec = {'nn': pl.BlockSpec((tm, tk), lambda i, j, kk: (i, kk)),
              'nt': pl.BlockSpec((tm, tk), lambda i, j, kk: (i, kk)),
              'tn': pl.BlockSpec((tk, tm), lambda i, j, kk: (kk, i))}[mode]
    b_spec = {'nn': pl.BlockSpec((tk, tn), lambda i, j, kk: (kk, j)),
              'nt': pl.BlockSpec((tn, tk), lambda i, j, kk: (j, kk)),
              'tn': pl.BlockSpec((tk, tn), lambda i, j, kk: (kk, j))}[mode]

    def body(a_ref, b_ref, o_ref, acc):
        kk = pl.program_id(2)

        @pl.when(kk == 0)
        def _():
            acc[...] = jnp.zeros_like(acc)

        acc[...] += lax.dot_general(a_ref[...].astype(BF16), b_ref[...].astype(BF16), _DN[mode],
                                    preferred_element_type=F32)

        @pl.when(kk == nk - 1)
        def _():
            o_ref[...] = acc[...].astype(o_ref.dtype)

    return _pcall(
        body, name=name, grid=(m // tm, n // tn, nk),
        in_specs=[a_spec, b_spec], out_specs=pl.BlockSpec((tm, tn), lambda i, j, kk: (i, j)),
        out_shape=jax.ShapeDtypeStruct((m, n), out_dtype),
        scratch_shapes=[pltpu.VMEM((tm, tn), F32)],
        compiler_params=_params(("parallel", "parallel", "arbitrary")),
    )(a, b)


def _rowcall(name, f, rows, params=(), row_outs=(), acc_outs=(), br=256, nrows=None, after=None):
    rows = [r if isinstance(r, tuple) else (r, 0) for r in rows]
    t = nrows if nrows is not None else rows[0][0].shape[0]
    br = min(br, t)
    assert t % br == 0, (name, t, br)
    n_in, n_ro = len(rows) + len(params), len(row_outs)
    extra = [] if after is None else [after]

    def row_spec(arr, off):
        assert off % br == 0
        return pl.BlockSpec((br, arr.shape[1]), functools.partial(lambda i, o: (i + o, 0), o=off // br))

    def whole_spec(shape):
        return pl.BlockSpec(shape, functools.partial(lambda i, nd: (0,) * nd, nd=len(shape)))

    def body(*refs):
        res = f(*[r[...] for r in refs[:n_in]])
        outs = refs[n_in + len(extra):]
        for o_ref, val in zip(outs[:n_ro], res[:n_ro]):
            o_ref[...] = val.astype(o_ref.dtype)
        if acc_outs:
            @pl.when(pl.program_id(0) == 0)
            def _():
                for o_ref in outs[n_ro:]:
                    o_ref[...] = jnp.zeros_like(o_ref)

            for o_ref, val in zip(outs[n_ro:], res[n_ro:]):
                o_ref[...] += val.astype(F32)

    out = _pcall(
        body, name=name, grid=(t // br,),
        in_specs=[row_spec(a, o) for a, o in rows] + [whole_spec(p.shape) for p in params]
        + [pl.BlockSpec(memory_space=pl.ANY)] * len(extra),
        out_specs=[pl.BlockSpec((br, c), lambda i: (i, 0)) for c, _ in row_outs] + [whole_spec(s) for s in acc_outs],
        out_shape=[jax.ShapeDtypeStruct((t, c), dt) for c, dt in row_outs]
        + [jax.ShapeDtypeStruct(s, F32) for s in acc_outs],
        compiler_params=_params(("arbitrary",) if acc_outs else ("parallel",)),
    )(*[a for a, _ in rows], *params, *extra)
    return out


def _bdot_raw(a, b, dn):
    return lax.dot_general(a.astype(BF16), b.astype(BF16), _DN[dn], preferred_element_type=F32)


def _hdot_raw(a, b, dn):
    a_hi, b_hi = a.astype(BF16), b.astype(BF16)
    a_lo, b_lo = (a - a_hi.astype(F32)).astype(BF16), (b - b_hi.astype(F32)).astype(BF16)

    def dot(p, q):
        return lax.dot_general(p, q, _DN[dn], preferred_element_type=F32)

    return dot(a_hi, b_hi) + (dot(a_hi, b_lo) + dot(a_lo, b_hi))


def _with_vjp(raw):
    @functools.partial(jax.custom_vjp, nondiff_argnums=(2,))
    def dot(a, b, dn):
        return raw(a, b, dn)

    def fwd(a, b, dn):
        return raw(a, b, dn), (a, b)

    def bwd(dn, res, ct):
        a, b = res
        if dn == 'nn':
            da, db = raw(ct, b, 'nt'), raw(a, ct, 'tn')
        elif dn == 'nt':
            da, db = raw(ct, b, 'nn'), raw(ct, a, 'tn')
        else:
            da, db = raw(b, ct, 'nt'), raw(a, ct, 'nn')
        return da.astype(a.dtype), db.astype(b.dtype)

    dot.defvjp(fwd, bwd)
    return dot


_bdot, _hdot = _with_vjp(_bdot_raw), _with_vjp(_hdot_raw)


def _rms(x, g):
    return x * lax.rsqrt(jnp.mean(x * x, axis=-1, keepdims=True) + NORM_EPS) * g


def _sigmoid(x):
    return 1.0 / (1.0 + jnp.exp(-x))


def _silu(x):
    return x * _sigmoid(x)


def _gelu(x):
    return 0.5 * x * (1.0 + lax.erf(x * (2.0 ** -0.5)))


def _norm_fwd(x, g, name, after=None):
    return _rowcall(name, lambda x_, g_: (_rms(x_, g_),), [x], [g], [(x.shape[1], BF16)], after=after)[0]


def _resnorm_fwd(x, f, g, alpha, name):
    return _rowcall(name, lambda x_, f_, g_: (x_ + alpha * _rms(f_, g_),), [x, f], [g], [(x.shape[1], F32)])[0]


def _resnorm_bwd(f, dx, g, alpha, name, after=None):
    def fn(f_, dx_, g_):
        _, vjp = jax.vjp(lambda a, b: alpha * _rms(a, b), f_, g_)
        return vjp(dx_)

    return _rowcall(name, fn, [f, dx], [g], [(f.shape[1], BF16)], [g.shape], after=after)


def _norm_bwd(x, dh, dx_out, g, name):
    def fn(x_, dh_, dxo_, g_):
        _, vjp = jax.vjp(_rms, x_, g_)
        dx, dg = vjp(dh_)
        return dxo_ + dx, dg

    return _rowcall(name, fn, [x, dh, dx_out], [g], [(x.shape[1], F32)], [g.shape])


def _norm_bwd_gain(x, dh, g, name):
    def fn(x_, dh_, g_):
        _, vjp = jax.vjp(lambda b: _rms(x_, b), g_)
        return vjp(dh_)

    return _rowcall(name, fn, [x, dh], [g], [], [g.shape])[0]


def _swiglu_fwd(gu, name):
    ff = gu.shape[1] // 2
    return _rowcall(name, lambda gu_: (_silu(gu_[:, :ff]) * gu_[:, ff:],), [gu], [], [(ff, BF16)], br=64)[0]


def _swiglu_bwd(gu, da, name):
    ff = gu.shape[1] // 2

    def fn(gu_, da_):
        gate, up = gu_[:, :ff], gu_[:, ff:]
        s = _sigmoid(gate)
        dgate = da_ * up * (s * (1.0 + gate * (1.0 - s)))
        dup = da_ * (gate * s)
        return (jnp.concatenate([dgate, dup], axis=1),)

    return _rowcall(name, fn, [gu, da], [], [(2 * ff, BF16)], br=64)[0]


def _sgu_norm(v, lg, lb):
    vf = _gelu(v)
    mu = jnp.mean(vf, axis=-1, keepdims=True)
    var = jnp.mean(jnp.square(vf - mu), axis=-1, keepdims=True)
    return (vf - mu) * lax.rsqrt(var + NORM_EPS) * lg + lb


def _sgu_head(dot, u_h, vn_h, w_h, b_h):
    r = lax.broadcasted_iota(jnp.int32, w_h.shape, 0)
    c = lax.broadcasted_iota(jnp.int32, w_h.shape, 1)
    mixed = dot(jnp.where(r >= c, w_h, 0.0), vn_h, 'nn') + b_h
    return _gelu(u_h) * mixed


def _heads(width):
    return [slice(h * HEAD_DIM, (h + 1) * HEAD_DIM) for h in range(width // HEAD_DIM)]


def _sgu_fwd(u, v, w, bcol, lg, lb, name):
    def fn(u_, v_, w_, b_, lg_, lb_):
        vn = _sgu_norm(v_, lg_, lb_)
        return (jnp.concatenate([_sgu_head(_bdot_raw, u_[:, s], vn[:, s], w_[h], b_[h])
                                 for h, s in enumerate(_heads(u_.shape[1]))], axis=1),)

    return _rowcall(name, fn, [u, v], [w, bcol, lg, lb], [(u.shape[1], BF16)], br=GM_CHUNK)[0]


def _sgu_bwd(u, v, w, bcol, lg, lb, dy, name):
    def fn(u_, v_, dy_, w_, b_, lg_, lb_):
        vn, vjp_norm = jax.vjp(_sgu_norm, v_, lg_, lb_)
        du, dvn, dw, db = [], [], [], []
        for h, s in enumerate(_heads(u_.shape[1])):
            _, vjp = jax.vjp(functools.partial(_sgu_head, _bdot), u_[:, s], vn[:, s], w_[h], b_[h])
            a, b, c, d = vjp(dy_[:, s])
            du.append(a), dvn.append(b), dw.append(c[None]), db.append(d[None])
        dv, dlg, dlb = vjp_norm(jnp.concatenate(dvn, axis=1))
        return (jnp.concatenate(du, axis=1), dv, jnp.concatenate(dw, axis=0), jnp.concatenate(db, axis=0), dlg, dlb)

    wd = u.shape[1]
    return _rowcall(name, fn, [u, v, dy], [w, bcol, lg, lb], [(wd, F32), (wd, F32)],
                    [w.shape, bcol.shape, lg.shape, lb.shape], br=GM_CHUNK)


def _shift_down(x, s):
    if s == 0:
        return x
    rows = lax.broadcasted_iota(jnp.int32, x.shape, 0)
    return jnp.where(rows >= s, pltpu.roll(x, s, 0), 0.0)


def _shift_up(x, s):
    if s == 0:
        return x
    t = x.shape[0]
    rows = lax.broadcasted_iota(jnp.int32, x.shape, 0)
    return jnp.where(rows < t - s, pltpu.roll(x, t - s, 0), 0.0)


def _conv_pre(x, taps):
    acc = None
    for i in range(CONV_WIDTH):
        term = _shift_down(x, CONV_WIDTH - 1 - i) * taps[i]
        acc = term if acc is None else acc + term
    return acc


def _taps(w_ref):
    return [w_ref[i:i + 1, :] for i in range(CONV_WIDTH)]


def _conv_fwd(x, w, name):
    t, ch = x.shape
    cb = _divtile(ch, 512)

    def body(x_ref, w_ref, o_ref):
        o_ref[...] = _silu(_conv_pre(x_ref[...], _taps(w_ref)))

    return _pcall(body, name=name, grid=(ch // cb,),
                  in_specs=[pl.BlockSpec((t, cb), lambda j: (0, j)), pl.BlockSpec((CONV_WIDTH, cb), lambda j: (0, j))],
                  out_specs=pl.BlockSpec((t, cb), lambda j: (0, j)),
                  out_shape=jax.ShapeDtypeStruct((t, ch), F32), compiler_params=_params(("parallel",)))(x, w)


def _conv_bwd(x, w, dout, name):
    t, ch = x.shape
    cb = _divtile(ch, 256)

    def body(x_ref, w_ref, do_ref, dx_ref, dw_ref):
        xv, taps = x_ref[...], _taps(w_ref)
        pre = _conv_pre(xv, taps)
        s = _sigmoid(pre)
        dpre = do_ref[...] * (s * (1.0 + pre * (1.0 - s)))
        dx = None
        for i in range(CONV_WIDTH):
            sh = CONV_WIDTH - 1 - i
            term = _shift_up(dpre, sh) * taps[i]
            dx = term if dx is None else dx + term
            dw_ref[i:i + 1, :] = jnp.sum(dpre * _shift_down(xv, sh), axis=0, keepdims=True)
        dx_ref[...] = dx

    return _pcall(body, name=name, grid=(ch // cb,),
                  in_specs=[pl.BlockSpec((t, cb), lambda j: (0, j)), pl.BlockSpec((CONV_WIDTH, cb), lambda j: (0, j)),
                            pl.BlockSpec((t, cb), lambda j: (0, j))],
                  out_specs=[pl.BlockSpec((t, cb), lambda j: (0, j)), pl.BlockSpec((CONV_WIDTH, cb), lambda j: (0, j))],
                  out_shape=[jax.ShapeDtypeStruct((t, ch), F32), jax.ShapeDtypeStruct((CONV_WIDTH, ch), F32)],
                  compiler_params=_params(("parallel",)))(x, w, dout)


GATE_BLOCK = 256


def _gates(dot, b_raw, a_raw, a_log, dt_bias):
    blk = b_raw.shape[1]
    z = a_raw + dt_bias
    softplus = jnp.maximum(z, 0.0) + jnp.log(1.0 + jnp.exp(-jnp.abs(z)))
    g = -jnp.exp(a_log) * softplus
    ri = lax.broadcasted_iota(jnp.int32, (blk, blk), 0)
    ci = lax.broadcasted_iota(jnp.int32, (blk, blk), 1)
    upto = ((ri <= ci) & (ri // DN_CHUNK == ci // DN_CHUNK)).astype(F32)
    return _sigmoid(b_raw), dot(g, upto, 'nn')


def _gate_blocks(t):
    blk = min(GATE_BLOCK, t)
    return [slice(i, i + blk) for i in range(0, t, blk)]


def _whole_call(name, f, ins, out_shapes):
    n_in = len(ins)

    def body(*refs):
        for o_ref, val in zip(refs[n_in:], f(*[r[...] for r in refs[:n_in]])):
            o_ref[...] = val

    vmem = pl.BlockSpec(memory_space=pltpu.VMEM)
    return _pcall(body, name=name, in_specs=[vmem] * n_in, out_specs=[vmem] * len(out_shapes),
                  out_shape=[jax.ShapeDtypeStruct(s, F32) for s in out_shapes],
                  compiler_params=pltpu.CompilerParams(vmem_limit_bytes=VMEM_LIMIT))(*ins)


def _gates_fwd(b_raw, a_raw, a_log, dt_bias, name):
    def fn(b_, a_, al_, dt_):
        parts = [_gates(_hdot_raw, b_[:, s], a_[:, s], al_, dt_) for s in _gate_blocks(b_.shape[1])]
        return [jnp.concatenate(p, axis=1) for p in zip(*parts)]

    return _whole_call(name, fn, [b_raw, a_raw, a_log, dt_bias], [b_raw.shape, b_raw.shape])


def _gates_bwd(b_raw, a_raw, a_log, dt_bias, dbeta, dgcum, name):
    def fn(b_, a_, al_, dt_, dbeta_, dgcum_):
        db, da, dal, ddt = [], [], None, None
        for s in _gate_blocks(b_.shape[1]):
            _, vjp = jax.vjp(functools.partial(_gates, _hdot), b_[:, s], a_[:, s], al_, dt_)
            p, q, r, u = vjp((dbeta_[:, s], dgcum_[:, s]))
            db.append(p), da.append(q)
            dal, ddt = (r, u) if dal is None else (dal + r, ddt + u)
        return jnp.concatenate(db, axis=1), jnp.concatenate(da, axis=1), dal, ddt

    return _whole_call(name, fn, [b_raw, a_raw, a_log, dt_bias, dbeta, dgcum],
                       [b_raw.shape, a_raw.shape, a_log.shape, dt_bias.shape])


def _unit_lower_inverse_raw(a):
    c = a.shape[0]
    eye = (lax.broadcasted_iota(jnp.int32, (c, c), 0) == lax.broadcasted_iota(jnp.int32, (c, c), 1)).astype(F32)
    inv, p = eye - a, _hdot_raw(a, a, 'nn')
    n_fac = int(math.log2(c)) - 1
    for it in range(n_fac):
        inv = inv + _hdot_raw(inv, p, 'nn')
        if it < n_fac - 1:
            p = _hdot_raw(p, p, 'nn')
    return inv


@jax.custom_vjp
def _unit_lower_inverse(a):
    return _unit_lower_inverse_raw(a)


def _unit_lower_inverse_fwd(a):
    inv = _unit_lower_inverse_raw(a)
    return inv, inv


def _unit_lower_inverse_bwd(inv, ct):
    return (-_hdot_raw(_hdot_raw(inv, ct, 'tn'), inv, 'nt'),)


_unit_lower_inverse.defvjp(_unit_lower_inverse_fwd, _unit_lower_inverse_bwd)


def _halves_raw(x):
    return x[:, :x.shape[1] // 2], x[:, x.shape[1] // 2:]


@jax.custom_vjp
def _halves(x):
    return _halves_raw(x)


_halves.defvjp(lambda x: (_halves_raw(x), None), lambda _, ct: (jnp.concatenate(ct, axis=1),))

_DELTA_OPS = (_bdot_raw, _hdot_raw, _unit_lower_inverse_raw, _halves_raw)
_DELTA_OPS_VJP = (_bdot, _hdot, _unit_lower_inverse, _halves)


def _delta_chunk(ops, state, q, k, v, gc, gr, bc):
    bd, hd, inverse, halves = ops
    c, d = q.shape
    ri = lax.broadcasted_iota(jnp.int32, (c, c), 0)
    ci = lax.broadcasted_iota(jnp.int32, (c, c), 1)
    causal, strict = ri >= ci, ri > ci
    qn = q * lax.rsqrt(jnp.sum(q * q, axis=-1, keepdims=True) + NORM_EPS) * (d ** -0.5)
    kn = k * lax.rsqrt(jnp.sum(k * k, axis=-1, keepdims=True) + NORM_EPS)
    gcum = jnp.broadcast_to(gc, (c, d))
    diff = jnp.broadcast_to(gc, (c, c)) - jnp.broadcast_to(gr, (c, c))
    decay = jnp.where(causal, jnp.exp(jnp.where(causal, diff, 0.0)), 0.0)
    bb = jnp.broadcast_to(bc, (c, d))
    k_beta = kn * bb
    inv = inverse(jnp.where(strict, bd(k_beta, kn, 'nt') * decay, 0.0))
    uw = hd(inv, jnp.concatenate([v * bb, k_beta * jnp.exp(gcum)], axis=1), 'nn')
    u, w = halves(uw)
    qk =jnp.where(causal, bd(qn, kn, 'nt') * decay, 0.0)
    v_new = u - bd(w, state, 'nn')
    o = bd(qn * jnp.exp(gcum), state, 'nn') + bd(qk, v_new, 'nn')
    last = lax.broadcasted_iota(jnp.int32, (c, d), 0) == c - 1
    g_last = jnp.sum(jnp.where(last, gcum, 0.0), axis=0, keepdims=True)
    k_dec = kn * jnp.exp(g_last - gcum)
    return state * jnp.exp(g_last) + bd(k_dec, v_new, 'tn'), o


DN_HEADS_PER_STEP = 4


def _delta_specs(nh, nc, rev):
    c, d = DN_CHUNK, HEAD_DIM
    hb = min(DN_HEADS_PER_STEP, nh)
    ng = nh // hb
    ch = (lambda n: nc - 1 - n) if rev else (lambda n: n)
    qkv = [pl.BlockSpec((c, hb * d), functools.partial(lambda h, n, o: (ch(n), o * ng + h), o=o)) for o in range(3)]
    col = pl.BlockSpec((hb, c, 1), lambda h, n: (h, ch(n), 0))
    row = pl.BlockSpec((hb, None, 1, c), lambda h, n: (h, ch(n), 0, 0))
    st = pl.BlockSpec((hb, None, d, d), lambda h, n: (h, ch(n), 0, 0))
    tok = pl.BlockSpec((c, hb * d), lambda h, n: (ch(n), h))
    return hb, ng, qkv, col, row, st, tok


def _delta_fwd(qkv, gcol, grow, bcol, name):
    t = qkv.shape[0]
    nh, nc, d = gcol.shape[0], t // DN_CHUNK, HEAD_DIM
    hb, ng, qkv_s, col, row, st, tok = _delta_specs(nh, nc, False)

    def body(q_ref, k_ref, v_ref, gc_ref, gr_ref, bc_ref, o_ref, st_ref, state):
        @pl.when(pl.program_id(1) == 0)
        def _():
            state[...] = jnp.zeros_like(state)

        for j, s in enumerate(_heads(hb * d)):
            s0 = state[j]
            st_ref[j] = s0
            s1, o = _delta_chunk(_DELTA_OPS, s0, q_ref[:, s], k_ref[:, s], v_ref[:, s], gc_ref[j], gr_ref[j], bc_ref[j])
            o_ref[:, s] = o
            state[j] = s1

    return _pcall(body, name=name, grid=(ng, nc), in_specs=qkv_s + [col, row, col], out_specs=[tok, st],
                  out_shape=[jax.ShapeDtypeStruct((t, nh * d), F32), jax.ShapeDtypeStruct((nh, nc, d, d), F32)],
                  scratch_shapes=[pltpu.VMEM((hb, d, d), F32)],
                  compiler_params=_params(("parallel", "arbitrary")))(qkv, qkv, qkv, gcol, grow, bcol)


def _delta_bwd(qkv, gcol, grow, bcol, states, do, name):
    t = qkv.shape[0]
    nh, nc, d = gcol.shape[0], t // DN_CHUNK, HEAD_DIM
    hb, ng, qkv_s, col, row, st, tok = _delta_specs(nh, nc, True)

    def body(q_ref, k_ref, v_ref, gc_ref, gr_ref, bc_ref, st_ref, do_ref,
             dq_ref, dk_ref, dv_ref, dgc_ref, dgr_ref, dbc_ref, dstate):
        @pl.when(pl.program_id(1) == 0)
        def _():
            dstate[...] = jnp.zeros_like(dstate)

        for j, s in enumerate(_heads(hb * d)):
            _, vjp = jax.vjp(functools.partial(_delta_chunk, _DELTA_OPS_VJP), st_ref[j], q_ref[:, s], k_ref[:, s],
                             v_ref[:, s], gc_ref[j], gr_ref[j], bc_ref[j])
            ds, dq, dk, dv, dgc, dgr, dbc = vjp((dstate[j], do_ref[:, s]))
            dq_ref[:, s], dk_ref[:, s], dv_ref[:, s] = dq, dk, dv
            dgc_ref[j], dgr_ref[j], dbc_ref[j] = dgc, dgr, dbc
            dstate[j] = ds

    tok_out = jax.ShapeDtypeStruct((t, nh * d), F32)
    return _pcall(body, name=name, grid=(ng, nc), in_specs=qkv_s + [col, row, col, st, tok],
                  out_specs=[tok, tok, tok, col, row, col],
                  out_shape=[tok_out, tok_out, tok_out, jax.ShapeDtypeStruct(gcol.shape, F32),
                             jax.ShapeDtypeStruct(grow.shape, F32), jax.ShapeDtypeStruct(bcol.shape, F32)],
                  scratch_shapes=[pltpu.VMEM((hb, d, d), F32)],
                  compiler_params=_params(("parallel", "arbitrary")))(qkv, qkv, qkv, gcol, grow, bcol, states, do)


def _outgate_head(o_h, z_h, w):
    return _rms(o_h, w) * _silu(z_h)


def _outgate_fwd(o, z, w, name):
    def fn(o_, z_, w_):
        return (jnp.concatenate([_outgate_head(o_[:, s], z_[:, s], w_) for s in _heads(o_.shape[1])], axis=1),)

    return _rowcall(name, fn, [o, z], [w], [(o.shape[1], BF16)])[0]


def _outgate_bwd(o, z, w, dy, name):
    def fn(o_, z_, dy_, w_):
        do, dz, dw = [], [], None
        for s in _heads(o_.shape[1]):
            _, vjp = jax.vjp(_outgate_head, o_[:, s], z_[:, s], w_)
            a, b, c = vjp(dy_[:, s])
            do.append(a), dz.append(b)
            dw = c if dw is None else dw + c
        return jnp.concatenate(do, axis=1), jnp.concatenate(dz, axis=1), dw

    wd = o.shape[1]
    return _rowcall(name, fn, [o, z, dy], [w], [(wd, F32), (wd, F32)], [w.shape])


def _attn_head(dot, q_h, k_h, v_h):
    s = dot(q_h, k_h, 'nt') * (q_h.shape[1] ** -0.5)
    e = jnp.exp(s - lax.stop_gradient(jnp.max(s, axis=-1, keepdims=True)))
    p = e / jnp.sum(e, axis=-1, keepdims=True)
    return dot(p, v_h, 'nn')


def _xa_slices(width):
    hd = width // XA_HEADS
    return [slice(h * hd, (h + 1) * hd) for h in range(XA_HEADS)]


def _attn_fwd(q, kv, name):
    wd = q.shape[1]

    def fn(q_, kv_):
        k_, v_ = kv_[:, :wd], kv_[:, wd:]
        return (jnp.concatenate([_attn_head(_bdot_raw, q_[:, s], k_[:, s], v_[:, s]) for s in _xa_slices(wd)],
                                axis=1),)

    return _rowcall(name, fn, [q], [kv], [(wd, BF16)])[0]


def _attn_bwd(q, kv, do, name):
    wd = q.shape[1]

    def fn(q_, do_, kv_):
        k_, v_ = kv_[:, :wd].astype(F32), kv_[:, wd:].astype(F32)
        dq, dk, dv = [], [], []
        for s in _xa_slices(wd):
            _, vjp = jax.vjp(functools.partial(_attn_head, _bdot), q_[:, s].astype(F32), k_[:, s], v_[:, s])
            a, b, c = vjp(do_[:, s])
            dq.append(a), dk.append(b), dv.append(c)
        return jnp.concatenate(dq, axis=1), jnp.concatenate(dk + dv, axis=1)

    return _rowcall(name, fn, [q, do], [kv], [(wd, BF16)], [kv.shape])


def _loss_call(y, target, name):
    d = y.shape[1]

    def fn(y_, t_):
        err = y_ - t_
        part = 0.5 * jnp.sum(jnp.sum(err * err, axis=1, keepdims=True), axis=0, keepdims=True) / d
        return err / d, jnp.broadcast_to(part, (1, LANES))

    return _rowcall(name, fn, [y, target], [], [(d, F32)], [(1, LANES)])


def _adamw_call(w, g, m, v, name):
    def fn(w_, g_, m_, v_):
        m1 = ADAM_B1 * m_ + (1.0 - ADAM_B1) * g_
        v1 = ADAM_B2 * v_ + (1.0 - ADAM_B2) * jnp.square(g_)
        m_hat = m1 / (1.0 - ADAM_B1 ** ADAM_STEP)
        v_hat = v1 / (1.0 - ADAM_B2 ** ADAM_STEP)
        return -ADAM_LR * (m_hat / (jnp.sqrt(v_hat) + ADAM_EPS) + ADAM_WD * w_), m1, v1

    c = w.shape[1]
    return _rowcall(name, fn, [w, g, m, v], [], [(c, F32)] * 3, br=_divtile(w.shape[0], 128, 8))


def _pair_add(grad, theirs, core, name):
    s, hr, cols = theirs.shape
    br = _divtile(hr, 128, 16)
    nb = hr // br

    def body(c_ref, g_ref, t_ref, o_ref):
        o_ref[...] = (g_ref[...].astype(F32) + t_ref[...].astype(F32)).astype(o_ref.dtype)

    spec = pl.BlockSpec((None, br, cols), lambda j, i, c_ref: (j, i, 0))
    return _pcall(body, name=name, out_shape=jax.ShapeDtypeStruct(theirs.shape, BF16),
                  grid_spec=pltpu.PrefetchScalarGridSpec(
                      num_scalar_prefetch=1, grid=(s, nb),
                      in_specs=[pl.BlockSpec((None, br, cols), lambda j, i, c_ref: (j, c_ref[0] * nb + i, 0)), spec],
                      out_specs=spec),
                  compiler_params=_params(("parallel", "parallel")))(core, grad, theirs)


def _chip_sum(part, landed, kind, where, grad, layer, name):
    _, hr, cw = landed.shape
    br = _divtile(hr, 128, 16)
    nb = hr // br

    def body(w_ref, p_ref, a_ref, b_ref, d_ref, g_ref, o_ref):
        o_ref[...] = ((p_ref[...].astype(F32) + a_ref[...].astype(F32)) + b_ref[...].astype(F32)) + d_ref[...].astype(F32)

    if kind == 'row':
        own = pl.BlockSpec((None, br, cw), lambda i, w_ref: (w_ref[0], i, 0))
    else:
        own = pl.BlockSpec((br, cw), lambda i, w_ref: (i, w_ref[0]))
    got = [pl.BlockSpec((None, br, cw), functools.partial(lambda i, w_ref, k: (k, i, 0), k=k)) for k in range(3)]
    return _pcall(body, name=name, out_shape=jax.ShapeDtypeStruct(grad.shape, F32),
                  grid_spec=pltpu.PrefetchScalarGridSpec(
                      num_scalar_prefetch=1, grid=(nb,), in_specs=[own] + got + [ANY_SPEC],
                      out_specs=pl.BlockSpec((None, br, cw), lambda i, w_ref: (layer, w_ref[1] * nb + i, 0))),
                  input_output_aliases={5: 0},
                  compiler_params=_params(("parallel",)))(where, part, landed, landed, landed, grad)


def _position():
    x, y, c = lax.axis_index("x"), lax.axis_index("y"), lax.axis_index("c")
    others = [(1 - x, y), (x, 1 - y), (1 - x, 1 - y)]
    return x, y, c, others


def _any_specs(n):
    return [pl.BlockSpec(memory_space=pl.ANY)] * n


def _ds(start, size):
    return pl.ds(pl.multiple_of(start, size), size)


HBM_SPEC = pl.BlockSpec(memory_space=pltpu.HBM)
SEM_SPEC = pl.BlockSpec(memory_space=pltpu.SEMAPHORE)
ANY_SPEC = pl.BlockSpec(memory_space=pl.ANY)
DATAFLOW = pltpu.SideEffectType.DATAFLOW_SIDE_EFFECTING


def _hbm(a):
    return pltpu.with_memory_space_constraint(a, pltpu.HBM)


def _full_shape(shard_shape, kind):
    r, cw = shard_shape
    return (N_CHIPS, r, cw) if kind == 'row' else (r, N_CHIPS * cw)


def _block_dims(full, kind):
    return (full.shape[1], full.shape[2]) if kind == 'row' else (full.shape[0], full.shape[1] // N_CHIPS)


def _region(full, kind, chip, half):
    r, cw = _block_dims(full, kind)
    if kind == 'row':
        return full.at[chip, _ds(half * (r // 2), r // 2), :]
    return full.at[_ds(half * (r // 2), r // 2), _ds(chip * cw, cw)]


def _place_block(full, shard, kind, chip, name):
    r, cw = shard.shape
    br = _divtile(r, 256, 16)

    def body(c_ref, s_ref, full_ref, o_ref):
        o_ref[...] = s_ref[...].astype(o_ref.dtype)

    if kind == 'row':
        out_spec = pl.BlockSpec((None, br, cw), lambda i, c_ref: (c_ref[0], i, 0))
    else:
        out_spec = pl.BlockSpec((br, cw), lambda i, c_ref: (i, c_ref[0]))
    return _pcall(body, name=name, out_shape=jax.ShapeDtypeStruct(full.shape, full.dtype),
                  grid_spec=pltpu.PrefetchScalarGridSpec(
                      num_scalar_prefetch=1, grid=(r // br,),
                      in_specs=[pl.BlockSpec((br, cw), lambda i, c_ref: (i, 0)), ANY_SPEC], out_specs=out_spec),
                  input_output_aliases={2: 0}, compiler_params=_params(("parallel",)))(chip, shard, full)


def _split_copy_start(bufs, lands, plan, after, name):
    nb, nl = len(bufs), len(lands)
    extra = [] if after is None else [after]

    def body(*refs):
        ins = refs[:nb + nl]
        send, recv = refs[nb + nl + len(extra)], refs[nb + nl + len(extra) + 1]
        for cp in plan(ins[:nb], ins[nb:], send, recv):
            cp.start()
        refs[-1][...] = jnp.zeros_like(refs[-1])

    n_copies = plan.n_copies
    out = _pcall(
        body, name=name,
        out_shape=(pltpu.SemaphoreType.DMA((n_copies,)), pltpu.SemaphoreType.DMA((n_copies,)),
                   *[pltpu.HBM(a.shape, a.dtype) for a in (*bufs, *lands)], jax.ShapeDtypeStruct((8, LANES), F32)),
        in_specs=[HBM_SPEC] * (nb + nl) + [ANY_SPEC] * len(extra),
        out_specs=(SEM_SPEC, SEM_SPEC, *[HBM_SPEC] * (nb + nl), pl.BlockSpec(memory_space=pltpu.VMEM)),
        input_output_aliases={i: 2 + i for i in range(nb + nl)},
        compiler_params=pltpu.CompilerParams(has_side_effects=DATAFLOW),
    )(*[_hbm(a) for a in (*bufs, *lands)], *extra)
    return out[0], out[1], list(out[2:2 + nb]), list(out[2 + nb:2 + nb + nl]), out[-1]


def _split_copy_wait(send, recv, bufs, lands, plan, after, name):
    nb, nl = len(bufs), len(lands)
    extra = [] if after is None else [after]

    def body(*refs):
        ins = refs[:nb + nl]
        send_ref, recv_ref = refs[nb + nl], refs[nb + nl + 1]
        for cp in plan(ins[:nb], ins[nb:], send_ref, recv_ref):
            cp.wait_send()
            cp.wait_recv()

    out = _pcall(
        body, name=name, out_shape=tuple(pltpu.HBM(a.shape, a.dtype) for a in (*bufs, *lands)),
        in_specs=[HBM_SPEC] * (nb + nl) + [SEM_SPEC, SEM_SPEC] + [ANY_SPEC] * len(extra),
        out_specs=tuple([HBM_SPEC] * (nb + nl)), input_output_aliases={i: i for i in range(nb + nl)},
        compiler_params=pltpu.CompilerParams(has_side_effects=DATAFLOW),
    )(*bufs, *lands, send, recv, *extra)
    return list(out[:nb]), list(out[nb:])


def _gather_plan(kinds):
    def plan(bufs, lands, send, recv):
        x, y, c, others = _position()
        me = 2 * x + y
        return [pltpu.make_async_remote_copy(
            src_ref=_region(lands[w], kinds[w], me, c), dst_ref=_region(lands[w], kinds[w], me, c),
            send_sem=send.at[3 * w + k], recv_sem=recv.at[3 * w + k], device_id=(px, py, c), device_id_type=MESH_IDS)
            for w in range(len(lands)) for k, (px, py) in enumerate(others)]

    def wait_plan(bufs, lands, send, recv):
        x, y, c, others = _position()
        me = 2 * x + y
        return [pltpu.make_async_remote_copy(
            src_ref=_region(lands[w], kinds[w], me, c), dst_ref=_region(lands[w], kinds[w], 2 * px + py, c),
            send_sem=send.at[3 * w + k], recv_sem=recv.at[3 * w + k], device_id=(px, py, c), device_id_type=MESH_IDS)
            for w in range(len(lands)) for k, (px, py) in enumerate(others)]

    plan.n_copies = wait_plan.n_copies = 3 * len(kinds)
    return plan, wait_plan


def _gather_forward(fulls, kinds, name):
    n = len(fulls)

    def body(*refs):
        dst = refs[n:2 * n]
        send, recv = refs[2 * n:]
        x, y, c, others = _position()
        sib = (x, y, 1 - c)

        def copy(w, k, chip, half):
            reg = _region(dst[w], kinds[w], chip, half)
            return pltpu.make_async_remote_copy(src_ref=reg, dst_ref=reg, send_sem=send.at[3 * w + k],
                                                recv_sem=recv.at[3 * w + k], device_id=sib, device_id_type=MESH_IDS)

        give = [copy(w, k, 2 * px + py, c) for w in range(n) for k, (px, py) in enumerate(others)]
        for cp in give:
            cp.start()
        for w in range(n):
            for k, (px, py) in enumerate(others):
                copy(w, k, 2 * px + py, 1 - c).wait_recv()
        for cp in give:
            cp.wait_send()

    return _pcall(body, name=name, in_specs=_any_specs(n), out_specs=_any_specs(n),
                  out_shape=[jax.ShapeDtypeStruct(f.shape, f.dtype) for f in fulls],
                  input_output_aliases={i: i for i in range(n)},
                  scratch_shapes=[pltpu.SemaphoreType.DMA((3 * n,)), pltpu.SemaphoreType.DMA((3 * n,))])(*fulls)


def _pair_exchange(grads, kinds, name):
    n = len(grads)

    def body(*refs):
        src, theirs_ref = refs[:n], refs[n:2 * n]
        send, recv = refs[2 * n:]
        x, y, c, _ = _position()
        sib = (x, y, 1 - c)

        def half(w, h):
            if kinds[w] == 'row':
                hr = src[w].shape[1] // 2
                return src[w].at[:, _ds(h * hr, hr), :]
            hr = src[w].shape[0] // 2
            return src[w].at[_ds(h * hr, hr), :]

        give = [pltpu.make_async_remote_copy(src_ref=half(w, 1 - c), dst_ref=theirs_ref[w], send_sem=send.at[w],
                                             recv_sem=recv.at[w], device_id=sib, device_id_type=MESH_IDS)
                for w in range(n)]
        for cp in give:
            cp.start()
        for cp in give:
            cp.wait()

    def half_shape(g, kind):
        return (g.shape[0], g.shape[1] // 2, g.shape[2]) if kind == 'row' else (g.shape[0] // 2, g.shape[1])

    return _pcall(body, name=name, in_specs=_any_specs(n), out_specs=_any_specs(n),
                  out_shape=[jax.ShapeDtypeStruct(half_shape(g, kd), g.dtype) for g, kd in zip(grads, kinds)],
                  scratch_shapes=[pltpu.SemaphoreType.DMA((n,)), pltpu.SemaphoreType.DMA((n,))])(*grads)


def _part_dims(part, kind):
    return (part.shape[1], part.shape[2]) if kind == 'row' else (part.shape[0], part.shape[1] // N_CHIPS)


def _chip_plan(kinds):
    def plan(bufs, lands, send, recv):
        x, y, c, others = _position()

        def slot(w, chip):
            if kinds[w] == 'row':
                return bufs[w].at[chip]
            cw = _part_dims(bufs[w], kinds[w])[1]
            return bufs[w].at[:, _ds(chip * cw, cw)]

        return [pltpu.make_async_remote_copy(src_ref=slot(w, 2 * px + py), dst_ref=lands[w].at[k],
                                             send_sem=send.at[3 * w + k], recv_sem=recv.at[3 * w + k],
                                             device_id=(px, py, c), device_id_type=MESH_IDS)
                for w in range(len(bufs)) for k, (px, py) in enumerate(others)]

    plan.n_copies = 3 * len(kinds)
    return plan


def _pair_gather(grads, name):
    n = len(grads)
    n_layers = grads[0].shape[0]

    def body(*refs):
        dst = refs[n:2 * n]
        send, recv = refs[2 * n:]
        x, y, c, _ = _position()
        sib = (x, y, 1 - c)

        def copy(w, l, half):
            hr = dst[w].shape[1] // 2
            rows = dst[w].at[l, _ds(half * hr, hr), :]
            return pltpu.make_async_remote_copy(src_ref=rows, dst_ref=rows, send_sem=send.at[w * n_layers + l],
                                                recv_sem=recv.at[w * n_layers + l], device_id=sib,
                                                device_id_type=MESH_IDS)

        give = [copy(w, l, c) for w in range(n) for l in range(n_layers)]
        for cp in give:
            cp.start()
        for w in range(n):
            for l in range(n_layers):
                copy(w, l, 1 - c).wait_recv()
        for cp in give:
            cp.wait_send()

    m = n * n_layers
    return _pcall(body, name=name, in_specs=_any_specs(n), out_specs=_any_specs(n),
                  out_shape=[jax.ShapeDtypeStruct(g.shape, g.dtype) for g in grads],
                  input_output_aliases={i: i for i in range(n)},
                  scratch_shapes=[pltpu.SemaphoreType.DMA((m,)), pltpu.SemaphoreType.DMA((m,))])(*grads)


def _all_sum(buf, name):
    rows = buf.shape[0]
    flips = [(fx, fy, fc) for fx in (0, 1) for fy in (0, 1) for fc in (0, 1)][1:]

    def body(x_ref, o_ref, land, send, recv):
        x, y, c, _ = _position()
        me = 4 * x + 2 * y + c

        def peer(f):
            return (1 - x if f[0] else x, 1 - y if f[1] else y, 1 - c if f[2] else c)

        give = [pltpu.make_async_remote_copy(src_ref=x_ref, dst_ref=land.at[me], send_sem=send.at[k],
                                             recv_sem=recv.at[k], device_id=peer(f), device_id_type=MESH_IDS)
                for k, f in enumerate(flips)]
        for cp in give:
            cp.start()
        land[me] = x_ref[...]
        for k, f in enumerate(flips):
            px, py, pc = peer(f)
            slot = land.at[4 * px + 2 * py + pc]
            pltpu.make_async_remote_copy(src_ref=slot, dst_ref=slot, send_sem=send.at[k], recv_sem=recv.at[k],
                                         device_id=peer(f), device_id_type=MESH_IDS).wait_recv()
        for cp in give:
            cp.wait_send()
        total = land[0]
        for dev in range(1, N_DEV):
            total = total + land[dev]
        o_ref[...] = total

    return _pcall(body, name=name, in_specs=[pl.BlockSpec(memory_space=pltpu.VMEM)],
                  out_specs=pl.BlockSpec(memory_space=pltpu.VMEM), out_shape=jax.ShapeDtypeStruct(buf.shape, F32),
                  scratch_shapes=[pltpu.VMEM((N_DEV, rows, LANES), F32), pltpu.SemaphoreType.DMA((N_DEV - 1,)),
                                  pltpu.SemaphoreType.DMA((N_DEV - 1,))],
                  compiler_params=pltpu.CompilerParams(vmem_limit_bytes=VMEM_LIMIT))(buf)


def _pack(arrays):
    flat = jnp.concatenate([a.reshape(-1).astype(F32) for a in arrays])
    pad = (-flat.shape[0]) % (8 * LANES)
    return jnp.pad(flat, (0, pad)).reshape(-1, LANES)


def _unpack(buf, like):
    flat, out, off = buf.reshape(-1), [], 0
    for a in like:
        out.append(flat[off:off + a.size].reshape(a.shape))
        off += a.size
    return out


def _row2(v):
    return v.reshape(1, -1)


def _ffn_fwd(x, g_pre, w_gu, w_d, g_post, tag, after=None):
    h = _norm_fwd(x, g_pre, tag + "_pre", after)
    gu = _mm(h, w_gu, 'nn', F32, tag + "_gu")
    a = _swiglu_fwd(gu, tag + "_act")
    f = _mm(a, w_d, 'nn', F32, tag + "_down")
    return _resnorm_fwd(x, f, g_post, 0.5, tag + "_post"), (x, h, gu, a, f)


def _ffn_bwd(dx_out, saved, g_pre, w_gu, w_d, g_post, tag, after=None):
    x, h, gu, a, f = saved
    df, dg_post = _resnorm_bwd(f, dx_out, g_post, 0.5, tag + "_post_b", after)
    da = _mm(df, w_d, 'nt', F32, tag + "_down_bx")
    dw_d = _mm(a, df, 'tn', BF16, tag + "_down_bw")
    dgu = _swiglu_bwd(gu, da, tag + "_act_b")
    dh = _mm(dgu, w_gu, 'nt', F32, tag + "_gu_bx")
    dw_gu = _mm(h, dgu, 'tn', BF16, tag + "_gu_bw")
    dx, dg_pre = _norm_bwd(x, dh, dx_out, g_pre, tag + "_pre_b")
    return dx, dg_pre, dw_gu, dw_d, dg_post


def _split_cols(pp, cs, cp, widths):
    proj = jnp.concatenate([pp[:, j * cp:j * cp + cs] for j in range(N_CHIPS)], axis=1)
    out, off = [], 0
    for wd in widths:
        out.append(proj[:, off:off + wd])
        off += wd
    return out


def _join_cols(pieces, cs, cp):
    proj = jnp.concatenate(pieces, axis=1)
    t = proj.shape[0]
    cols = []
    for j in range(N_CHIPS):
        cols += [proj[:, j * cs:(j + 1) * cs], jnp.zeros((t, cp - cs), proj.dtype)]
    return jnp.concatenate(cols, axis=1)


def _mix_fwd(x, p, w_in, w_out, conv_w, cs, cp, tag, after=None):
    t, d = x.shape
    half = d // 2
    nh = half // HEAD_DIM
    h = _norm_fwd(x, p['mix_norm_pre'], tag + "_pre", after)
    pp = _mm(h, w_in, 'nn', F32, tag + "_in")
    u_a, v_a, qkv_raw, z, b_raw, a_raw = _split_cols(pp, cs, cp, [half, half, 3 * half, half, nh, nh])
    y_a = _sgu_fwd(u_a, v_a, p['sm_w'], p['sm_b'], p['sm_ln_g'], p['sm_ln_b'], tag + "_sgu")
    qkv = _conv_fwd(qkv_raw, conv_w, tag + "_conv")
    b_raw, a_raw = b_raw.T, a_raw.T
    beta, gcum = _gates_fwd(b_raw, a_raw, p['a_log'].T, p['dt_bias'].T, tag + "_gates")
    gcol, bcol = gcum[:, :, None], beta[:, :, None]
    grow = gcum.reshape(nh, t // DN_CHUNK, 1, DN_CHUNK)
    o, states = _delta_fwd(qkv, gcol, grow, bcol, tag + "_delta")
    y_b = _outgate_fwd(o, z, p['dn_norm_w'], tag + "_gate")
    y = jnp.concatenate([y_a, y_b], axis=1)
    m = _mm(y, w_out, 'nn', F32, tag + "_out")
    x_out = _resnorm_fwd(x, m, p['mix_norm_post'], 1.0, tag + "_post")
    return x_out, (x, h, u_a, v_a, qkv_raw, z, b_raw, a_raw, qkv, gcol, grow, bcol, states, o, y, m)


def _mix_bwd(dx_out, saved, p, w_in, w_out, conv_w, cs, cp, tag, after=None):
    x, h, u_a, v_a, qkv_raw, z, b_raw, a_raw, qkv, gcol, grow, bcol, states, o, y, m = saved
    t, d = x.shape
    half = d // 2
    nh = half // HEAD_DIM
    gr = {}
    dm, gr['mix_norm_post'] = _resnorm_bwd(m, dx_out, p['mix_norm_post'], 1.0, tag + "_post_b", after)
    dy = _mm(dm, w_out, 'nt', F32, tag + "_out_bx")
    gr['w_out'] = _mm(y, dm, 'tn', BF16, tag + "_out_bw")
    dy_a, dy_b = dy[:, :half], dy[:, half:]
    do, dz, gr['dn_norm_w'] = _outgate_bwd(o, z, p['dn_norm_w'], dy_b, tag + "_gate_b")
    dq, dk, dv, dgcol, dgrow, dbcol = _delta_bwd(qkv, gcol, grow, bcol, states, do, tag + "_delta_b")
    dgcum = dgcol[:, :, 0] + dgrow.reshape(nh, t)
    db_raw, da_raw, gr['a_log'], gr['dt_bias'] = _gates_bwd(b_raw, a_raw, p['a_log'].T, p['dt_bias'].T,
                                                             dbcol[:, :, 0], dgcum, tag + "_gates_b")
    db_raw, da_raw = db_raw.T, da_raw.T
    dqkv_raw, gr['conv_w'] = _conv_bwd(qkv_raw, conv_w, jnp.concatenate([dq, dk, dv], axis=1), tag + "_conv_b")
    du_a, dv_a, gr['sm_w'], gr['sm_b'], gr['sm_ln_g'], gr['sm_ln_b'] = _sgu_bwd(
        u_a, v_a, p['sm_w'], p['sm_b'], p['sm_ln_g'], p['sm_ln_b'], dy_a, tag + "_sgu_b")
    dpp = _join_cols([du_a, dv_a, dqkv_raw, dz, db_raw, da_raw], cs, cp).astype(BF16)
    dh = _mm(dpp, w_in, 'nt', F32, tag + "_in_bx")
    gr['w_in'] = _mm(h, dpp, 'tn', BF16, tag + "_in_bw")
    dx, gr['mix_norm_pre'] = _norm_bwd(x, dh, dx_out, p['mix_norm_pre'], tag + "_pre_b")
    return dx, gr


def _xa_fwd(x, mem, p, w_xq, w_xkv, w_xo, tag, after=None):
    h = _norm_fwd(x, p['xa_norm_pre'], tag + "_pre", after)
    mh = _norm_fwd(mem, p['mem_norm'], tag + "_mem")
    q = _mm(h, w_xq, 'nn', BF16, tag + "_q")
    kv = _mm(mh, w_xkv, 'nn', BF16, tag + "_kv")
    o = _attn_fwd(q, kv, tag + "_attn")
    c = _mm(o, w_xo, 'nn', F32, tag + "_o")
    return _resnorm_fwd(x, c, p['xa_norm_post'], 1.0, tag + "_post"), (x, h, mh, q, kv, o, c)


def _xa_bwd(dx_out, saved, mem, p, w_xq, w_xkv, w_xo, tag, after=None):
    x, h, mh, q, kv, o, c = saved
    gr = {}
    dc, gr['xa_norm_post'] = _resnorm_bwd(c, dx_out, p['xa_norm_post'], 1.0, tag + "_post_b", after)
    do = _mm(dc, w_xo, 'nt', F32, tag + "_o_bx")
    gr['w_xo'] = _mm(o, dc, 'tn', BF16, tag + "_o_bw")
    dq, dkv = _attn_bwd(q, kv, do, tag + "_attn_b")
    dkv = dkv.astype(BF16)
    dh = _mm(dq, w_xq, 'nt', F32, tag + "_q_bx")
    gr['w_xq'] = _mm(h, dq, 'tn', BF16, tag + "_q_bw")
    dmh = _mm(dkv, w_xkv, 'nt', F32, tag + "_kv_bx")
    gr['w_xkv'] = _mm(mh, dkv, 'tn', BF16, tag + "_kv_bw")
    gr['mem_norm'] = _norm_bwd_gain(mem, dmh, p['mem_norm'], tag + "_mem_b")
    dx, gr['xa_norm_pre'] = _norm_bwd(x, dh, dx_out, p['xa_norm_pre'], tag + "_pre_b")
    return dx, gr


def _step(inputs):
    x, mem, target = inputs['x'][0], inputs['mem'][0], inputs['loss_target'][0]
    n_layers = inputs['w_in'].shape[0]
    cx, cy, cc = lax.axis_index("x"), lax.axis_index("y"), lax.axis_index("c")
    chip = 2 * cx + cy
    cs = inputs['w_in'].shape[2]
    cp = -(-cs // LANES) * LANES
    kinds = [BIG[nm] for nm in BIG_NAMES]

    conv_local = inputs['conv_w']
    ccols = conv_local.shape[2]
    placed = lax.dynamic_update_slice(jnp.zeros((n_layers, CONV_WIDTH, N_CHIPS * ccols), F32),
                                      jnp.where(cc == 0, conv_local, 0.0), (0, 0, chip * ccols))
    conv_full = _unpack(_all_sum(_pack([placed]), "conv_gather"), [placed])[0]

    core_idx, chip_idx = cc.astype(jnp.int32).reshape(1), chip.astype(jnp.int32).reshape(1)
    where_idx = jnp.stack([chip, cc]).astype(jnp.int32)
    order = [(l, gi) for l in range(n_layers) for gi in range(len(GROUPS))]
    n_groups = len(order)

    def small(l):
        p = {nm: inputs[nm][l] for nm in SMALL_NAMES}
        for nm in SMALL_NAMES:
            if p[nm].ndim == 1:
                p[nm] = _row2(p[nm])
        p['sm_b'] = p['sm_b'][:, :, None]
        return p

    def gather_start(k, after):
        l, gi = order[k]
        names = GROUPS[gi][1]
        kds = [BIG[nm] for nm in names]
        lands = []
        for nm, kd in zip(names, kds):
            s = inputs[nm][l]
            if nm == 'w_in':
                s = jnp.pad(s, ((0, 0), (0, cp - cs)))
            lands.append(_place_block(lax.empty(_full_shape(s.shape, kd), BF16), s, kd, chip_idx, "place_block"))
        plan, wait_plan = _gather_plan(kds)
        send, recv, _, lands, token = _split_copy_start([], lands, plan, after, f"gather_start_{k}")
        return dict(send=send, recv=recv, lands=lands, token=token, kinds=kds, names=names, wait_plan=wait_plan)

    def gather_finish(k, st, after):
        _, lands = _split_copy_wait(st['send'], st['recv'], [], st['lands'], st['wait_plan'], after, f"gather_wait_{k}")
        lands = _gather_forward(lands, st['kinds'], "gather_forward_" + GROUPS[order[k][1]][0])
        return {nm: (g.reshape(-1, g.shape[2]) if kd == 'row' else g)
                for nm, g, kd in zip(st['names'], lands, st['kinds'])}

    def group_fwd(gi, x_, p, w, l, token):
        if gi == 0:
            return _ffn_fwd(x_, p['ffn1_norm_pre'], w['ffn1_w_gate_up'], w['ffn1_w_down'], p['ffn1_norm_post'], "ffn", token)
        if gi == 1:
            return _mix_fwd(x_, p, w['w_in'], w['w_out'], conv_full[l], cs, cp, "mix", token)
        if gi == 2:
            return _xa_fwd(x_, mem, p, w['w_xq'], w['w_xkv'], w['w_xo'], "xa", token)
        return _ffn_fwd(x_, p['ffn2_norm_pre'], w['ffn2_w_gate_up'], w['ffn2_w_down'], p['ffn2_norm_post'], "ffn", token)

    def group_bwd(gi, dx_, s, p, w, l, token):
        if gi in (0, 3):
            pre = 'ffn1' if gi == 0 else 'ffn2'
            dx_, g_pre, g_gu, g_d, g_post = _ffn_bwd(dx_, s, p[pre + '_norm_pre'], w[pre + '_w_gate_up'],
                                                     w[pre + '_w_down'], p[pre + '_norm_post'], "ffn", token)
            return dx_, {pre + '_norm_pre': g_pre, pre + '_w_gate_up': g_gu, pre + '_w_down': g_d,
                         pre + '_norm_post': g_post}
        if gi == 1:
            return _mix_bwd(dx_, s, p, w['w_in'], w['w_out'], conv_full[l], cs, cp, "mix", token)
        return _xa_bwd(dx_, s, mem, p, w['w_xq'], w['w_xkv'], w['w_xo'], "xa", token)

    started = {0: gather_start(0, None)}
    started[1] = gather_start(1, started[0]['token'])
    weights, saved = [], []
    for k, (l, gi) in enumerate(order):
        w = gather_finish(k, started[k], x if k > 0 else started[1]['token'])
        token = None
        if k + 2 < n_groups:
            started[k + 2] = gather_start(k + 2, next(iter(w.values())))
            token = started[k + 2]['token']
        weights.append(w)
        x, s = group_fwd(gi, x, small(l), w, l, token)
        saved.append(s)

    dx, loss_part = _loss_call(x, target, "loss")

    grads = [dict() for _ in range(n_layers)]
    big_grads = {nm: None for nm in BIG_NAMES}

    def reduce_finish(pd, after):
        parts, lands = _split_copy_wait(pd['send'], pd['recv'], pd['parts'], pd['lands'], pd['plan'], after,
                                        f"chip_wait_{pd['k']}")
        for nm, kd, part, landed in zip(pd['names'], pd['kinds'], parts, lands):
            if big_grads[nm] is None:
                big_grads[nm] = lax.empty((n_layers, 2 * landed.shape[1], landed.shape[2]), F32)
            big_grads[nm] = _chip_sum(part, landed, kd, where_idx, big_grads[nm], pd['l'], "chip_sum")

    pending = None
    for k in reversed(range(n_groups)):
        l, gi = order[k]
        dx, gr = group_bwd(gi, dx, saved[k], small(l), weights[k], l, pending['token'] if pending else None)
        if pending:
            reduce_finish(pending, dx)
        names = GROUPS[gi][1]
        kds = [BIG[nm] for nm in names]
        parts = []
        gs = [gr[nm].reshape(N_CHIPS, -1, gr[nm].shape[1]) if kd == 'row' else gr[nm] for nm, kd in zip(names, kds)]
        theirs = _pair_exchange(gs, kds, "pair_exchange_" + GROUPS[gi][0])
        for g, t, kd in zip(gs, theirs, kds):
            parts.append(_pair_add(g, t, core_idx, "pair_add") if kd == 'row'
                         else _pair_add(g[None], t[None], core_idx, "pair_add")[0])
        lands = [lax.empty((3, *_part_dims(part, kd)), BF16) for part, kd in zip(parts, kds)]
        plan = _chip_plan(kds)
        send, recv, parts, lands, token = _split_copy_start(parts, lands, plan, None, f"chip_start_{k}")
        pending = dict(send=send, recv=recv, parts=parts, lands=lands, token=token, plan=plan, names=names, kinds=kds,
                       k=k, l=l)
        grads[l].update({nm: g for nm, g in gr.items() if nm not in BIG})
    reduce_finish(pending, None)
    big_grads = dict(zip(BIG_NAMES, _pair_gather([big_grads[nm] for nm in BIG_NAMES], "pair_gather")))
    big_grads['w_in'] = big_grads['w_in'][:, :, :cs]

    small_grads = [jnp.stack([grads[l][nm].reshape(inputs[nm].shape[1:]) if nm != 'conv_w' else grads[l][nm]
                              for l in range(n_layers)]) for nm in SMALL_NAMES]
    summed = _unpack(_all_sum(_pack(small_grads + [loss_part]), "small_sum"), small_grads + [loss_part])
    loss = summed[-1][0, 0]
    small_g = dict(zip(SMALL_NAMES, summed[:-1]))
    small_g['conv_w'] = lax.dynamic_slice(small_g['conv_w'], (0, 0, chip * ccols), (n_layers, CONV_WIDTH, ccols))

    out = {}
    for nm in BIG_NAMES:
        g = big_grads[nm]
        two = lambda a: a.reshape(-1, a.shape[-1])
        delta, m1, v1 = _adamw_call(two(inputs[nm]), two(g), two(inputs['m_' + nm]), two(inputs['v_' + nm]), "adamw")
        out[nm] = (g, delta.reshape(g.shape), m1.reshape(g.shape), v1.reshape(g.shape))
    sw = [inputs[nm] for nm in SMALL_NAMES]
    packed = [_pack([small_g[nm] for nm in SMALL_NAMES])] + [_pack([inputs[pre + nm] for nm in SMALL_NAMES])
                                                            for pre in ('', 'm_', 'v_')]
    delta, m1, v1 = _adamw_call(packed[1], packed[0], packed[2], packed[3], "adamw_small")
    for nm, d_, m_, v_ in zip(SMALL_NAMES, _unpack(delta, sw), _unpack(m1, sw), _unpack(v1, sw)):
        out[nm] = (small_g[nm], d_, m_, v_)
    return (loss, dx[None], *[out[nm][0] for nm in WEIGHTS], *[out[nm][1] for nm in WEIGHTS],
            *[out[nm][2] for nm in WEIGHTS], *[out[nm][3] for nm in WEIGHTS])


def kernel(x, mem, ffn1_norm_pre, ffn1_w_gate_up, ffn1_w_down, ffn1_norm_post, mix_norm_pre, w_in, conv_w, a_log, dt_bias, sm_w, sm_b, sm_ln_g, sm_ln_b, dn_norm_w, w_out, mix_norm_post, xa_norm_pre, mem_norm, w_xq, w_xkv, w_xo, xa_norm_post, ffn2_norm_pre, ffn2_w_gate_up, ffn2_w_down, ffn2_norm_post, loss_target, m_ffn1_norm_pre, m_ffn1_w_gate_up, m_ffn1_w_down, m_ffn1_norm_post, m_mix_norm_pre, m_w_in, m_conv_w, m_a_log, m_dt_bias, m_sm_w, m_sm_b, m_sm_ln_g, m_sm_ln_b, m_dn_norm_w, m_w_out, m_mix_norm_post, m_xa_norm_pre, m_mem_norm, m_w_xq, m_w_xkv, m_w_xo, m_xa_norm_post, m_ffn2_norm_pre, m_ffn2_w_gate_up, m_ffn2_w_down, m_ffn2_norm_post, v_ffn1_norm_pre, v_ffn1_w_gate_up, v_ffn1_w_down, v_ffn1_norm_post, v_mix_norm_pre, v_w_in, v_conv_w, v_a_log, v_dt_bias, v_sm_w, v_sm_b, v_sm_ln_g, v_sm_ln_b, v_dn_norm_w, v_w_out, v_mix_norm_post, v_xa_norm_pre, v_mem_norm, v_w_xq, v_w_xkv, v_w_xo, v_xa_norm_post, v_ffn2_norm_pre, v_ffn2_w_gate_up, v_ffn2_w_down, v_ffn2_norm_post):
    vals = dict(locals())
    return _step(vals)
```

```python
import functools
import math

import jax
import jax.numpy as jnp
from jax import lax
from jax.experimental import pallas as pl
from jax.experimental.pallas import tpu as pltpu

F32, BF16 = jnp.float32, jnp.bfloat16
NORM_EPS = 1e-6
HEAD_DIM = 128
GM_CHUNK = 128
DN_CHUNK = 64
CONV_WIDTH = 4
XA_HEADS = 4
LANES = 128
N_CHIPS = 4
N_DEV = 8
VMEM_LIMIT = 56 * 1024 * 1024
MESH_IDS = pl.DeviceIdType.MESH
ADAM_LR, ADAM_B1, ADAM_B2, ADAM_EPS, ADAM_WD, ADAM_STEP = 0.001, 0.9, 0.999, 1e-08, 0.01, 10

WEIGHTS = ['ffn1_norm_pre', 'ffn1_w_gate_up', 'ffn1_w_down', 'ffn1_norm_post', 'mix_norm_pre', 'w_in', 'conv_w',
           'a_log', 'dt_bias', 'sm_w', 'sm_b', 'sm_ln_g', 'sm_ln_b', 'dn_norm_w', 'w_out', 'mix_norm_post',
           'xa_norm_pre', 'mem_norm', 'w_xq', 'w_xkv', 'w_xo', 'xa_norm_post', 'ffn2_norm_pre', 'ffn2_w_gate_up',
           'ffn2_w_down', 'ffn2_norm_post']
BIG = {'ffn1_w_gate_up': 'col', 'ffn1_w_down': 'row', 'w_in': 'col', 'w_out': 'row', 'w_xq': 'row',
       'w_xkv': 'col', 'w_xo': 'row', 'ffn2_w_gate_up': 'col', 'ffn2_w_down': 'row'}
BIG_NAMES = list(BIG)
GROUPS = (('ffn1', ('ffn1_w_gate_up', 'ffn1_w_down')), ('mix', ('w_in', 'w_out')),
          ('xa', ('w_xq', 'w_xkv', 'w_xo')), ('ffn2', ('ffn2_w_gate_up', 'ffn2_w_down')))
SMALL_NAMES = [n for n in WEIGHTS if n not in BIG]


def _pcall(body, **kw):
    return pl.pallas_call(body, **kw)


def _params(sem=None):
    return pltpu.CompilerParams(dimension_semantics=sem, vmem_limit_bytes=VMEM_LIMIT)


def _divtile(dim, cap, align=LANES):
    if dim <= cap:
        return dim
    t = (cap // align) * align
    while t > align and dim % t:
        t -= align
    assert dim % t == 0, (dim, cap)
    return t


_DN = {'nn': (((1,), (0,)), ((), ())), 'nt': (((1,), (1,)), ((), ())), 'tn': (((0,), (0,)), ((), ()))}


def _mm(a, b, mode, out_dtype, name):
    if mode == 'nn':
        (m, k), (k2, n) = a.shape, b.shape
    elif mode == 'nt':
        (m, k), (n, k2) = a.shape, b.shape
    else:
        (k, m), (k2, n) = a.shape, b.shape
    assert k == k2, (a.shape, b.shape, mode)
    tm, tn, tk = _divtile(m, 1024), _divtile(n, 1024), _divtile(k, 2048)
    nk = k // tk
    a_spec = {'nn': pl.BlockSpec((tm, tk), lambda i, j, kk: (i, kk)),
              'nt': pl.BlockSpec((tm, tk), lambda i, j, kk: (i, kk)),
              'tn': pl.BlockSpec((tk, tm), lambda i, j, kk: (kk, i))}[mode]
    b_spec = {'nn': pl.BlockSpec((tk, tn), lambda i, j, kk: (kk, j)),
              'nt': pl.BlockSpec((tn, tk), lambda i, j, kk: (j, kk)),
              'tn': pl.BlockSpec((tk, tn), lambda i, j, kk: (kk, j))}[mode]

    def dot(a_ref, b_ref):
        return lax.dot_general(a_ref[...].astype(BF16), b_ref[...].astype(BF16), _DN[mode], preferred_element_type=F32)

    def body_once(a_ref, b_ref, o_ref):
        o_ref[...] = dot(a_ref, b_ref).astype(o_ref.dtype)

    def body_steps(a_ref, b_ref, o_ref, acc):
        kk = pl.program_id(2)

        @pl.when(kk == 0)
        def _():
            acc[...] = jnp.zeros_like(acc)

        acc[...] += dot(a_ref, b_ref)

        @pl.when(kk == nk - 1)
        def _():
            o_ref[...] = acc[...].astype(o_ref.dtype)

    return _pcall(
        body_once if nk == 1 else body_steps, name=name, grid=(m // tm, n // tn, nk),
        in_specs=[a_spec, b_spec], out_specs=pl.BlockSpec((tm, tn), lambda i, j, kk: (i, j)),
        out_shape=jax.ShapeDtypeStruct((m, n), out_dtype),
        scratch_shapes=[] if nk == 1 else [pltpu.VMEM((tm, tn), F32)],
        compiler_params=_params(("parallel", "parallel", "arbitrary")),
    )(a, b)


def _rowcall(name, f, rows, params=(), row_outs=(), acc_outs=(), br=256, nrows=None, after=None):
    rows = [r if isinstance(r, tuple) else (r, 0) for r in rows]
    t = nrows if nrows is not None else rows[0][0].shape[0]
    br = min(br, t)
    assert t % br == 0, (name, t, br)
    n_in, n_ro = len(rows) + len(params), len(row_outs)
    extra = [] if after is None else [after]

    def row_spec(arr, off):
        assert off % br == 0
        return pl.BlockSpec((br, arr.shape[1]), functools.partial(lambda i, o: (i + o, 0), o=off // br))

    def whole_spec(shape):
        return pl.BlockSpec(shape, functools.partial(lambda i, nd: (0,) * nd, nd=len(shape)))

    def body(*refs):
        res = f(*[r[...] for r in refs[:n_in]])
        outs = refs[n_in + len(extra):]
        for o_ref, val in zip(outs[:n_ro], res[:n_ro]):
            o_ref[...] = val.astype(o_ref.dtype)
        if acc_outs:
            @pl.when(pl.program_id(0) == 0)
            def _():
                for o_ref in outs[n_ro:]:
                    o_ref[...] = jnp.zeros_like(o_ref)

            for o_ref, val in zip(outs[n_ro:], res[n_ro:]):
                o_ref[...] += val.astype(F32)

    out = _pcall(
        body, name=name, grid=(t // br,),
        in_specs=[row_spec(a, o) for a, o in rows] + [whole_spec(p.shape) for p in params]
        + [pl.BlockSpec(memory_space=pl.ANY)] * len(extra),
        out_specs=[pl.BlockSpec((br, c), lambda i: (i, 0)) for c, _ in row_outs] + [whole_spec(s) for s in acc_outs],
        out_shape=[jax.ShapeDtypeStruct((t, c), dt) for c, dt in row_outs]
        + [jax.ShapeDtypeStruct(s, F32) for s in acc_outs],
        compiler_params=_params(("arbitrary",) if acc_outs else ("parallel",)),
    )(*[a for a, _ in rows], *params, *extra)
    return out


_DN_BATCH = {'nn': (((2,), (1,)), ((0,), (0,))), 'nt': (((2,), (2,)), ((0,), (0,))), 'tn': (((1,), (1,)), ((0,), (0,)))}


def _dims(a, dn):
    return _DN_BATCH[dn] if a.ndim == 3 else _DN[dn]


def _bdot_raw(a, b, dn):
    return lax.dot_general(a.astype(BF16), b.astype(BF16), _dims(a, dn), preferred_element_type=F32)


def _hdot_raw(a, b, dn):
    a_hi, b_hi = a.astype(BF16), b.astype(BF16)
    a_lo, b_lo = (a - a_hi.astype(F32)).astype(BF16), (b - b_hi.astype(F32)).astype(BF16)

    def dot(p, q):
        return lax.dot_general(p, q, _dims(a, dn), preferred_element_type=F32)

    return dot(a_hi, b_hi) + (dot(a_hi, b_lo) + dot(a_lo, b_hi))


def _with_vjp(raw):
    @functools.partial(jax.custom_vjp, nondiff_argnums=(2,))
    def dot(a, b, dn):
        return raw(a, b, dn)

    def fwd(a, b, dn):
        return raw(a, b, dn), (a, b)

    def bwd(dn, res, ct):
        a, b = res
        if dn == 'nn':
            da, db = raw(ct, b, 'nt'), raw(a, ct, 'tn')
        elif dn == 'nt':
            da, db = raw(ct, b, 'nn'), raw(ct, a, 'tn')
        else:
            da, db = raw(b, ct, 'nt'), raw(a, ct, 'nn')
        return da.astype(a.dtype), db.astype(b.dtype)

    dot.defvjp(fwd, bwd)
    return dot


_bdot, _hdot = _with_vjp(_bdot_raw), _with_vjp(_hdot_raw)


def _rms(x, g):
    return x * lax.rsqrt(jnp.mean(x * x, axis=-1, keepdims=True) + NORM_EPS) * g


def _sigmoid(x):
    return 1.0 / (1.0 + jnp.exp(-x))


def _silu(x):
    return x * _sigmoid(x)


def _gelu(x):
    return 0.5 * x * (1.0 + lax.erf(x * (2.0 ** -0.5)))


def _norm_fwd(x, g, name, after=None):
    return _rowcall(name, lambda x_, g_: (_rms(x_, g_),), [x], [g], [(x.shape[1], BF16)], after=after)[0]


def _resnorm_fwd(x, f, g, alpha, name):
    return _rowcall(name, lambda x_, f_, g_: (x_ + alpha * _rms(f_, g_),), [x, f], [g], [(x.shape[1], F32)])[0]


def _resnorm_bwd(f, dx, g, alpha, name, after=None):
    def fn(f_, dx_, g_):
        _, vjp = jax.vjp(lambda a, b: alpha * _rms(a, b), f_, g_)
        return vjp(dx_)

    return _rowcall(name, fn, [f, dx], [g], [(f.shape[1], BF16)], [g.shape], after=after)


def _norm_bwd(x, dh, dx_out, g, name):
    def fn(x_, dh_, dxo_, g_):
        _, vjp = jax.vjp(_rms, x_, g_)
        dx, dg = vjp(dh_)
        return dxo_ + dx, dg

    return _rowcall(name, fn, [x, dh, dx_out], [g], [(x.shape[1], F32)], [g.shape])


def _norm_bwd_gain(x, dh, g, name):
    def fn(x_, dh_, g_):
        _, vjp = jax.vjp(lambda b: _rms(x_, b), g_)
        return vjp(dh_)

    return _rowcall(name, fn, [x, dh], [g], [], [g.shape])[0]


def _swiglu_fwd(gu, name):
    ff = gu.shape[1] // 2
    return _rowcall(name, lambda gu_: (_silu(gu_[:, :ff]) * gu_[:, ff:],), [gu], [], [(ff, BF16)], br=64)[0]


def _swiglu_bwd(gu, da, name):
    ff = gu.shape[1] // 2

    def fn(gu_, da_):
        gate, up = gu_[:, :ff], gu_[:, ff:]
        s = _sigmoid(gate)
        dgate = da_ * up * (s * (1.0 + gate * (1.0 - s)))
        dup = da_ * (gate * s)
        return (jnp.concatenate([dgate, dup], axis=1),)

    return _rowcall(name, fn, [gu, da], [], [(2 * ff, BF16)], br=64)[0]


def _sgu_norm(v, lg, lb):
    vf = _gelu(v)
    mu = jnp.mean(vf, axis=-1, keepdims=True)
    var = jnp.mean(jnp.square(vf - mu), axis=-1, keepdims=True)
    return (vf - mu) * lax.rsqrt(var + NORM_EPS) * lg + lb


def _sgu_head(dot, u_h, vn_h, w_h, b_h):
    r = lax.broadcasted_iota(jnp.int32, w_h.shape, 0)
    c = lax.broadcasted_iota(jnp.int32, w_h.shape, 1)
    mixed = dot(jnp.where(r >= c, w_h, 0.0), vn_h, 'nn') + b_h
    return _gelu(u_h) * mixed


def _heads(width):
    return [slice(h * HEAD_DIM, (h + 1) * HEAD_DIM) for h in range(width // HEAD_DIM)]


def _sgu_fwd(u, v, w, bcol, lg, lb, name):
    def fn(u_, v_, w_, b_, lg_, lb_):
        vn = _sgu_norm(v_, lg_, lb_)
        return (jnp.concatenate([_sgu_head(_bdot_raw, u_[:, s], vn[:, s], w_[h], b_[h])
                                 for h, s in enumerate(_heads(u_.shape[1]))], axis=1),)

    return _rowcall(name, fn, [u, v], [w, bcol, lg, lb], [(u.shape[1], BF16)], br=GM_CHUNK)[0]


def _sgu_bwd(u, v, w, bcol, lg, lb, dy, name):
    def fn(u_, v_, dy_, w_, b_, lg_, lb_):
        vn, vjp_norm = jax.vjp(_sgu_norm, v_, lg_, lb_)
        du, dvn, dw, db = [], [], [], []
        for h, s in enumerate(_heads(u_.shape[1])):
            _, vjp = jax.vjp(functools.partial(_sgu_head, _bdot), u_[:, s], vn[:, s], w_[h], b_[h])
            a, b, c, d = vjp(dy_[:, s])
            du.append(a), dvn.append(b), dw.append(c[None]), db.append(d[None])
        dv, dlg, dlb = vjp_norm(jnp.concatenate(dvn, axis=1))
        return (jnp.concatenate(du, axis=1), dv, jnp.concatenate(dw, axis=0), jnp.concatenate(db, axis=0), dlg, dlb)

    wd = u.shape[1]
    return _rowcall(name, fn, [u, v, dy], [w, bcol, lg, lb], [(wd, F32), (wd, F32)],
                    [w.shape, bcol.shape, lg.shape, lb.shape], br=GM_CHUNK)


def _shift_down(x, s):
    if s == 0:
        return x
    rows = lax.broadcasted_iota(jnp.int32, x.shape, 0)
    return jnp.where(rows >= s, pltpu.roll(x, s, 0), 0.0)


def _shift_up(x, s):
    if s == 0:
        return x
    t = x.shape[0]
    rows = lax.broadcasted_iota(jnp.int32, x.shape, 0)
    return jnp.where(rows < t - s, pltpu.roll(x, t - s, 0), 0.0)


def _conv_pre(x, taps):
    acc = None
    for i in range(CONV_WIDTH):
        term = _shift_down(x, CONV_WIDTH - 1 - i) * taps[i]
        acc = term if acc is None else acc + term
    return acc


def _taps(w_ref):
    return [w_ref[i:i + 1, :] for i in range(CONV_WIDTH)]


def _conv_fwd(x, w, name):
    t, ch = x.shape
    cb = _divtile(ch, 512)

    def body(x_ref, w_ref, o_ref):
        o_ref[...] = _silu(_conv_pre(x_ref[...], _taps(w_ref)))

    return _pcall(body, name=name, grid=(ch // cb,),
                  in_specs=[pl.BlockSpec((t, cb), lambda j: (0, j)), pl.BlockSpec((CONV_WIDTH, cb), lambda j: (0, j))],
                  out_specs=pl.BlockSpec((t, cb), lambda j: (0, j)),
                  out_shape=jax.ShapeDtypeStruct((t, ch), F32), compiler_params=_params(("parallel",)))(x, w)


def _conv_bwd(x, w, dout, col_off, name):
    t, ch = dout.shape
    cb = _divtile(ch, 256)
    assert col_off % cb == 0
    off = col_off // cb

    def body(x_ref, w_ref, do_ref, dx_ref, dw_ref):
        xv, taps = x_ref[...], _taps(w_ref)
        pre = _conv_pre(xv, taps)
        s = _sigmoid(pre)
        dpre = do_ref[...] * (s * (1.0 + pre * (1.0 - s)))
        dx = None
        for i in range(CONV_WIDTH):
            sh = CONV_WIDTH - 1 - i
            term = _shift_up(dpre, sh) * taps[i]
            dx = term if dx is None else dx + term
            dw_ref[i:i + 1, :] = jnp.sum(dpre * _shift_down(xv, sh), axis=0, keepdims=True)
        dx_ref[...] = dx

    return _pcall(body, name=name, grid=(ch // cb,),
                  in_specs=[pl.BlockSpec((t, cb), lambda j: (0, j + off)),
                            pl.BlockSpec((CONV_WIDTH, cb), lambda j: (0, j + off)),
                            pl.BlockSpec((t, cb), lambda j: (0, j))],
                  out_specs=[pl.BlockSpec((t, cb), lambda j: (0, j)), pl.BlockSpec((CONV_WIDTH, cb), lambda j: (0, j))],
                  out_shape=[jax.ShapeDtypeStruct((t, ch), F32), jax.ShapeDtypeStruct((CONV_WIDTH, ch), F32)],
                  compiler_params=_params(("parallel",)))(x, w, dout)


GATE_BLOCK = 256


def _gates(dot, b_raw, a_raw, a_log, dt_bias):
    blk = b_raw.shape[1]
    z = a_raw + dt_bias
    softplus = jnp.maximum(z, 0.0) + jnp.log(1.0 + jnp.exp(-jnp.abs(z)))
    g = -jnp.exp(a_log) * softplus
    ri = lax.broadcasted_iota(jnp.int32, (blk, blk), 0)
    ci = lax.broadcasted_iota(jnp.int32, (blk, blk), 1)
    upto = ((ri <= ci) & (ri // DN_CHUNK == ci // DN_CHUNK)).astype(F32)
    return _sigmoid(b_raw), dot(g, upto, 'nn')


def _gate_blocks(t):
    blk = min(GATE_BLOCK, t)
    return [slice(i, i + blk) for i in range(0, t, blk)]


def _whole_call(name, f, ins, out_shapes):
    n_in = len(ins)

    def body(*refs):
        for o_ref, val in zip(refs[n_in:], f(*[r[...] for r in refs[:n_in]])):
            o_ref[...] = val

    vmem = pl.BlockSpec(memory_space=pltpu.VMEM)
    return _pcall(body, name=name, in_specs=[vmem] * n_in, out_specs=[vmem] * len(out_shapes),
                  out_shape=[jax.ShapeDtypeStruct(s, F32) for s in out_shapes],
                  compiler_params=pltpu.CompilerParams(vmem_limit_bytes=VMEM_LIMIT))(*ins)


def _gates_fwd(b_raw, a_raw, a_log, dt_bias, name):
    def fn(b_, a_, al_, dt_):
        parts = [_gates(_hdot_raw, b_[:, s], a_[:, s], al_, dt_) for s in _gate_blocks(b_.shape[1])]
        return [jnp.concatenate(p, axis=1) for p in zip(*parts)]

    return _whole_call(name, fn, [b_raw, a_raw, a_log, dt_bias], [b_raw.shape, b_raw.shape])


def _gates_bwd(b_raw, a_raw, a_log, dt_bias, dbeta, dgcum, name):
    def fn(b_, a_, al_, dt_, dbeta_, dgcum_):
        db, da, dal, ddt = [], [], None, None
        for s in _gate_blocks(b_.shape[1]):
            _, vjp = jax.vjp(functools.partial(_gates, _hdot), b_[:, s], a_[:, s], al_, dt_)
            p, q, r, u = vjp((dbeta_[:, s], dgcum_[:, s]))
            db.append(p), da.append(q)
            dal, ddt = (r, u) if dal is None else (dal + r, ddt + u)
        return jnp.concatenate(db, axis=1), jnp.concatenate(da, axis=1), dal, ddt

    return _whole_call(name, fn, [b_raw, a_raw, a_log, dt_bias, dbeta, dgcum],
                       [b_raw.shape, a_raw.shape, a_log.shape, dt_bias.shape])


def _unit_lower_inverse_raw(a):
    c = a.shape[-1]
    eye = (lax.broadcasted_iota(jnp.int32, (c, c), 0) == lax.broadcasted_iota(jnp.int32, (c, c), 1)).astype(F32)
    inv, p = eye - a, _hdot_raw(a, a, 'nn')
    n_fac = int(math.log2(c)) - 1
    for it in range(n_fac):
        inv = inv + _hdot_raw(inv, p, 'nn')
        if it < n_fac - 1:
            p = _hdot_raw(p, p, 'nn')
    return inv


@jax.custom_vjp
def _unit_lower_inverse(a):
    return _unit_lower_inverse_raw(a)


def _unit_lower_inverse_fwd(a):
    inv = _unit_lower_inverse_raw(a)
    return inv, inv


def _unit_lower_inverse_bwd(inv, ct):
    return (-_hdot_raw(_hdot_raw(inv, ct, 'tn'), inv, 'nt'),)


_unit_lower_inverse.defvjp(_unit_lower_inverse_fwd, _unit_lower_inverse_bwd)


def _halves_raw(x):
    return x[..., :x.shape[-1] // 2], x[..., x.shape[-1] // 2:]


@jax.custom_vjp
def _halves(x):
    return _halves_raw(x)


_halves.defvjp(lambda x: (_halves_raw(x), None), lambda _, ct: (jnp.concatenate(ct, axis=-1),))

_DELTA_OPS = (_bdot_raw, _hdot_raw, _unit_lower_inverse_raw, _halves_raw)
_DELTA_OPS_VJP = (_bdot, _hdot, _unit_lower_inverse, _halves)


def _delta_chunk(ops, state, q, k, v, gc, gr, bc):
    bd, hd, inverse, halves = ops
    c, d = q.shape[-2:]
    ri = lax.broadcasted_iota(jnp.int32, (c, c), 0)
    ci = lax.broadcasted_iota(jnp.int32, (c, c), 1)
    causal, strict = ri >= ci, ri > ci
    qn = q * lax.rsqrt(jnp.sum(q * q, axis=-1, keepdims=True) + NORM_EPS) * (d ** -0.5)
    kn = k * lax.rsqrt(jnp.sum(k * k, axis=-1, keepdims=True) + NORM_EPS)
    gcum = jnp.broadcast_to(gc, q.shape)
    decay = jnp.where(causal, jnp.exp(jnp.where(causal, gc - gr, 0.0)), 0.0)
    bb = jnp.broadcast_to(bc, q.shape)
    k_beta = kn * bb
    inv = inverse(jnp.where(strict, bd(k_beta, kn, 'nt') * decay, 0.0))
    uw = hd(inv, jnp.concatenate([v * bb, k_beta * jnp.exp(gcum)], axis=-1), 'nn')
    u, w = halves(uw)
    qk = jnp.where(causal, bd(qn, kn, 'nt') * decay, 0.0)
    v_new = u - bd(w, state, 'nn')
    o = bd(qn * jnp.exp(gcum), state, 'nn') + bd(qk, v_new, 'nn')
    last = lax.broadcasted_iota(jnp.int32, (c, d), 0) == c - 1
    g_last = jnp.sum(jnp.where(last, gcum, 0.0), axis=-2, keepdims=True)
    k_dec = kn * jnp.exp(g_last - gcum)
    return state * jnp.exp(g_last) + bd(k_dec, v_new, 'tn'), o


DN_HEADS_PER_STEP = 8


def _by_head(ref):
    return jnp.stack([ref[:, s] for s in _heads(ref.shape[1])])


def _delta_specs(nh, nc, rev):
    c, d = DN_CHUNK, HEAD_DIM
    hb = min(DN_HEADS_PER_STEP, nh)
    ng = nh // hb
    ch = (lambda n: nc - 1 - n) if rev else (lambda n: n)
    qkv = [pl.BlockSpec((c, hb * d), functools.partial(lambda h, n, o: (ch(n), o * ng + h), o=o)) for o in range(3)]
    col = pl.BlockSpec((hb, c, 1), lambda h, n: (h, ch(n), 0))
    row = pl.BlockSpec((hb, None, 1, c), lambda h, n: (h, ch(n), 0, 0))
    st = pl.BlockSpec((hb, None, d, d), lambda h, n: (h, ch(n), 0, 0))
    tok = pl.BlockSpec((c, hb * d), lambda h, n: (ch(n), h))
    return hb, ng, qkv, col, row, st, tok


def _delta_fwd(qkv, gcol, grow, bcol, name):
    t = qkv.shape[0]
    nh, nc, d = gcol.shape[0], t // DN_CHUNK, HEAD_DIM
    hb, ng, qkv_s, col, row, st, tok = _delta_specs(nh, nc, False)

    def body(q_ref, k_ref, v_ref, gc_ref, gr_ref, bc_ref, o_ref, st_ref, state):
        @pl.when(pl.program_id(1) == 0)
        def _():
            state[...] = jnp.zeros_like(state)

        s0 = state[...]
        st_ref[...] = s0
        s1, o = _delta_chunk(_DELTA_OPS, s0, _by_head(q_ref), _by_head(k_ref), _by_head(v_ref), gc_ref[...],
                             gr_ref[...], bc_ref[...])
        for j, s in enumerate(_heads(hb * d)):
            o_ref[:, s] = o[j]
        state[...] = s1

    return _pcall(body, name=name, grid=(ng, nc), in_specs=qkv_s + [col, row, col], out_specs=[tok, st],
                  out_shape=[jax.ShapeDtypeStruct((t, nh * d), F32), jax.ShapeDtypeStruct((nh, nc, d, d), F32)],
                  scratch_shapes=[pltpu.VMEM((hb, d, d), F32)],
                  compiler_params=_params(("parallel", "arbitrary")))(qkv, qkv, qkv, gcol, grow, bcol)


def _delta_bwd(qkv, gcol, grow, bcol, states, do, name):
    t = qkv.shape[0]
    nh, nc, d = gcol.shape[0], t // DN_CHUNK, HEAD_DIM
    hb, ng, qkv_s, col, row, st, tok = _delta_specs(nh, nc, True)

    def body(q_ref, k_ref, v_ref, gc_ref, gr_ref, bc_ref, st_ref, do_ref,
             dq_ref, dk_ref, dv_ref, dgc_ref, dgr_ref, dbc_ref, dstate):
        @pl.when(pl.program_id(1) == 0)
        def _():
            dstate[...] = jnp.zeros_like(dstate)

        _, vjp = jax.vjp(functools.partial(_delta_chunk, _DELTA_OPS_VJP), st_ref[...], _by_head(q_ref), _by_head(k_ref),
                         _by_head(v_ref), gc_ref[...], gr_ref[...], bc_ref[...])
        ds, dq, dk, dv, dgc, dgr, dbc = vjp((dstate[...], _by_head(do_ref)))
        for j, s in enumerate(_heads(hb * d)):
            dq_ref[:, s], dk_ref[:, s], dv_ref[:, s] = dq[j], dk[j], dv[j]
        dgc_ref[...], dgr_ref[...], dbc_ref[...] = dgc, dgr, dbc
        dstate[...] = ds

    tok_out = jax.ShapeDtypeStruct((t, nh * d), F32)
    return _pcall(body, name=name, grid=(ng, nc), in_specs=qkv_s + [col, row, col, st, tok],
                  out_specs=[tok, tok, tok, col, row, col],
                  out_shape=[tok_out, tok_out, tok_out, jax.ShapeDtypeStruct(gcol.shape, F32),
                             jax.ShapeDtypeStruct(grow.shape, F32), jax.ShapeDtypeStruct(bcol.shape, F32)],
                  scratch_shapes=[pltpu.VMEM((hb, d, d), F32)],
                  compiler_params=_params(("parallel", "arbitrary")))(qkv, qkv, qkv, gcol, grow, bcol, states, do)


def _outgate_head(o_h, z_h, w):
    return _rms(o_h, w) * _silu(z_h)


def _outgate_fwd(o, z, w, name):
    def fn(o_, z_, w_):
        return (jnp.concatenate([_outgate_head(o_[:, s], z_[:, s], w_) for s in _heads(o_.shape[1])], axis=1),)

    return _rowcall(name, fn, [o, z], [w], [(o.shape[1], BF16)])[0]


def _outgate_bwd(o, z, w, dy, name):
    def fn(o_, z_, dy_, w_):
        do, dz, dw = [], [], None
        for s in _heads(o_.shape[1]):
            _, vjp = jax.vjp(_outgate_head, o_[:, s], z_[:, s], w_)
            a, b, c = vjp(dy_[:, s])
            do.append(a), dz.append(b)
            dw = c if dw is None else dw + c
        return jnp.concatenate(do, axis=1), jnp.concatenate(dz, axis=1), dw

    wd = o.shape[1]
    return _rowcall(name, fn, [o, z, dy], [w], [(wd, F32), (wd, F32)], [w.shape])


def _attn_head(dot, q_h, k_h, v_h):
    s = dot(q_h, k_h, 'nt') * (q_h.shape[1] ** -0.5)
    e = jnp.exp(s - lax.stop_gradient(jnp.max(s, axis=-1, keepdims=True)))
    p = e / jnp.sum(e, axis=-1, keepdims=True)
    return dot(p, v_h, 'nn')


def _xa_slices(width):
    hd = width // XA_HEADS
    return [slice(h * hd, (h + 1) * hd) for h in range(XA_HEADS)]


def _attn_fwd(q, kv, name):
    wd = q.shape[1]

    def fn(q_, kv_):
        k_, v_ = kv_[:, :wd], kv_[:, wd:]
        return (jnp.concatenate([_attn_head(_bdot_raw, q_[:, s], k_[:, s], v_[:, s]) for s in _xa_slices(wd)],
                                axis=1),)

    return _rowcall(name, fn, [q], [kv], [(wd, BF16)])[0]


def _attn_bwd(q, kv, do, name):
    wd = q.shape[1]

    def fn(q_, do_, kv_):
        k_, v_ = kv_[:, :wd].astype(F32), kv_[:, wd:].astype(F32)
        dq, dk, dv = [], [], []
        for s in _xa_slices(wd):
            _, vjp = jax.vjp(functools.partial(_attn_head, _bdot), q_[:, s].astype(F32), k_[:, s], v_[:, s])
            a, b, c = vjp(do_[:, s])
            dq.append(a), dk.append(b), dv.append(c)
        return jnp.concatenate(dq, axis=1), jnp.concatenate(dk + dv, axis=1)

    return _rowcall(name, fn, [q, do], [kv], [(wd, BF16)], [kv.shape])


def _loss_call(y, target, name):
    d = y.shape[1]

    def fn(y_, t_):
        err = y_ - t_
        part = 0.5 * jnp.sum(jnp.sum(err * err, axis=1, keepdims=True), axis=0, keepdims=True) / d
        return err / d, jnp.broadcast_to(part, (1, LANES))

    return _rowcall(name, fn, [y, target], [], [(d, F32)], [(1, LANES)])


def _adamw_call(w, g, m, v, name):
    def fn(w_, g_, m_, v_):
        m1 = ADAM_B1 * m_ + (1.0 - ADAM_B1) * g_
        v1 = ADAM_B2 * v_ + (1.0 - ADAM_B2) * jnp.square(g_)
        m_hat = m1 / (1.0 - ADAM_B1 ** ADAM_STEP)
        v_hat = v1 / (1.0 - ADAM_B2 ** ADAM_STEP)
        return -ADAM_LR * (m_hat / (jnp.sqrt(v_hat) + ADAM_EPS) + ADAM_WD * w_), m1, v1

    c = w.shape[1]
    return _rowcall(name, fn, [w, g, m, v], [], [(c, F32)] * 3, br=_divtile(w.shape[0], 128, 8))


STREAM_BLOCK_BYTES = 2 * 1024 * 1024


def _rows_for(cols, itemsize):
    return max(16, STREAM_BLOCK_BYTES // (cols * itemsize) // 16 * 16)


def _pair_add(grad, theirs, core, name):
    s, hr, cols = theirs.shape
    br = _divtile(hr, _rows_for(cols, 2), 16)
    nb = hr // br

    def body(c_ref, g_ref, t_ref, o_ref):
        o_ref[...] = (g_ref[...].astype(F32) + t_ref[...].astype(F32)).astype(o_ref.dtype)

    spec = pl.BlockSpec((None, br, cols), lambda j, i, c_ref: (j, i, 0))
    return _pcall(body, name=name, out_shape=jax.ShapeDtypeStruct(theirs.shape, BF16),
                  grid_spec=pltpu.PrefetchScalarGridSpec(
                      num_scalar_prefetch=1, grid=(s, nb),
                      in_specs=[pl.BlockSpec((None, br, cols), lambda j, i, c_ref: (j, c_ref[0] * nb + i, 0)), spec],
                      out_specs=spec),
                  compiler_params=_params(("parallel", "parallel")))(core, grad, theirs)


def _chip_sum(part, landed, kind, where, grad, layer, name):
    _, hr, cw = landed.shape
    br = _divtile(hr, _rows_for(cw, 2), 16)
    nb = hr // br

    def body(w_ref, p_ref, a_ref, b_ref, d_ref, g_ref, o_ref):
        o_ref[...] = ((p_ref[...].astype(F32) + a_ref[...].astype(F32)) + b_ref[...].astype(F32)) + d_ref[...].astype(F32)

    if kind == 'row':
        own = pl.BlockSpec((None, br, cw), lambda i, w_ref: (w_ref[0], i, 0))
    else:
        own = pl.BlockSpec((br, cw), lambda i, w_ref: (i, w_ref[0]))
    got = [pl.BlockSpec((None, br, cw), functools.partial(lambda i, w_ref, k: (k, i, 0), k=k)) for k in range(3)]
    return _pcall(body, name=name, out_shape=jax.ShapeDtypeStruct(grad.shape, F32),
                  grid_spec=pltpu.PrefetchScalarGridSpec(
                      num_scalar_prefetch=1, grid=(nb,), in_specs=[own] + got + [ANY_SPEC],
                      out_specs=pl.BlockSpec((None, br, cw), lambda i, w_ref: (layer, w_ref[1] * nb + i, 0))),
                  input_output_aliases={5: 0},
                  compiler_params=_params(("parallel",)))(where, part, landed, landed, landed, grad)


def _position():
    x, y, c = lax.axis_index("x"), lax.axis_index("y"), lax.axis_index("c")
    others = [(1 - x, y), (x, 1 - y), (1 - x, 1 - y)]
    return x, y, c, others


def _any_specs(n):
    return [pl.BlockSpec(memory_space=pl.ANY)] * n


def _ds(start, size):
    return pl.ds(pl.multiple_of(start, size), size)


HBM_SPEC = pl.BlockSpec(memory_space=pltpu.HBM)
SEM_SPEC = pl.BlockSpec(memory_space=pltpu.SEMAPHORE)
ANY_SPEC = pl.BlockSpec(memory_space=pl.ANY)
DATAFLOW = pltpu.SideEffectType.DATAFLOW_SIDE_EFFECTING


def _hbm(a):
    return pltpu.with_memory_space_constraint(a, pltpu.HBM)


def _full_shape(shard_shape, kind):
    r, cw = shard_shape
    return (N_CHIPS, r, cw) if kind == 'row' else (r, N_CHIPS * cw)


def _block_dims(full, kind):
    return (full.shape[1], full.shape[2]) if kind == 'row' else (full.shape[0], full.shape[1] // N_CHIPS)


def _region(full, kind, chip, half):
    r, cw = _block_dims(full, kind)
    if kind == 'row':
        return full.at[chip, _ds(half * (r // 2), r // 2), :]
    return full.at[_ds(half * (r // 2), r // 2), _ds(chip * cw, cw)]


def _place_block(full, shard, kind, chip, name):
    r, cw = shard.shape
    br = _divtile(r, 256, 16)

    def body(c_ref, s_ref, full_ref, o_ref):
        o_ref[...] = s_ref[...].astype(o_ref.dtype)

    if kind == 'row':
        out_spec = pl.BlockSpec((None, br, cw), lambda i, c_ref: (c_ref[0], i, 0))
    else:
        out_spec = pl.BlockSpec((br, cw), lambda i, c_ref: (i, c_ref[0]))
    return _pcall(body, name=name, out_shape=jax.ShapeDtypeStruct(full.shape, full.dtype),
                  grid_spec=pltpu.PrefetchScalarGridSpec(
                      num_scalar_prefetch=1, grid=(r // br,),
                      in_specs=[pl.BlockSpec((br, cw), lambda i, c_ref: (i, 0)), ANY_SPEC], out_specs=out_spec),
                  input_output_aliases={2: 0}, compiler_params=_params(("parallel",)))(chip, shard, full)


def _split_copy_start(bufs, lands, plan, after, name):
    nb, nl = len(bufs), len(lands)
    extra = [] if after is None else [after]

    def body(*refs):
        ins = refs[:nb + nl]
        send, recv = refs[nb + nl + len(extra)], refs[nb + nl + len(extra) + 1]
        for cp in plan(ins[:nb], ins[nb:], send, recv):
            cp.start()
        refs[-1][...] = jnp.zeros_like(refs[-1])

    n_copies = plan.n_copies
    out = _pcall(
        body, name=name,
        out_shape=(pltpu.SemaphoreType.DMA((n_copies,)), pltpu.SemaphoreType.DMA((n_copies,)),
                   *[pltpu.HBM(a.shape, a.dtype) for a in (*bufs, *lands)], jax.ShapeDtypeStruct((8, LANES), F32)),
        in_specs=[HBM_SPEC] * (nb + nl) + [ANY_SPEC] * len(extra),
        out_specs=(SEM_SPEC, SEM_SPEC, *[HBM_SPEC] * (nb + nl), pl.BlockSpec(memory_space=pltpu.VMEM)),
        input_output_aliases={i: 2 + i for i in range(nb + nl)},
        compiler_params=pltpu.CompilerParams(has_side_effects=DATAFLOW),
    )(*[_hbm(a) for a in (*bufs, *lands)], *extra)
    return out[0], out[1], list(out[2:2 + nb]), list(out[2 + nb:2 + nb + nl]), out[-1]


def _split_copy_wait(send, recv, bufs, lands, plan, after, name):
    nb, nl = len(bufs), len(lands)
    extra = [] if after is None else [after]

    def body(*refs):
        ins = refs[:nb + nl]
        send_ref, recv_ref = refs[nb + nl], refs[nb + nl + 1]
        for cp in plan(ins[:nb], ins[nb:], send_ref, recv_ref):
            cp.wait_send()
            cp.wait_recv()

    out = _pcall(
        body, name=name, out_shape=tuple(pltpu.HBM(a.shape, a.dtype) for a in (*bufs, *lands)),
        in_specs=[HBM_SPEC] * (nb + nl) + [SEM_SPEC, SEM_SPEC] + [ANY_SPEC] * len(extra),
        out_specs=tuple([HBM_SPEC] * (nb + nl)), input_output_aliases={i: i for i in range(nb + nl)},
        compiler_params=pltpu.CompilerParams(has_side_effects=DATAFLOW),
    )(*bufs, *lands, send, recv, *extra)
    return list(out[:nb]), list(out[nb:])


def _gather_plan(kinds):
    def plan(bufs, lands, send, recv):
        x, y, c, others = _position()
        me = 2 * x + y
        return [pltpu.make_async_remote_copy(
            src_ref=_region(lands[w], kinds[w], me, c), dst_ref=_region(lands[w], kinds[w], me, c),
            send_sem=send.at[3 * w + k], recv_sem=recv.at[3 * w + k], device_id=(px, py, c), device_id_type=MESH_IDS)
            for w in range(len(lands)) for k, (px, py) in enumerate(others)]

    def wait_plan(bufs, lands, send, recv):
        x, y, c, others = _position()
        me = 2 * x + y
        return [pltpu.make_async_remote_copy(
            src_ref=_region(lands[w], kinds[w], me, c), dst_ref=_region(lands[w], kinds[w], 2 * px + py, c),
            send_sem=send.at[3 * w + k], recv_sem=recv.at[3 * w + k], device_id=(px, py, c), device_id_type=MESH_IDS)
            for w in range(len(lands)) for k, (px, py) in enumerate(others)]

    plan.n_copies = wait_plan.n_copies = 3 * len(kinds)
    return plan, wait_plan


def _gather_forward(fulls, kinds, name):
    n = len(fulls)

    def body(*refs):
        dst = refs[n:2 * n]
        send, recv = refs[2 * n:]
        x, y, c, others = _position()
        sib = (x, y, 1 - c)

        def copy(w, k, chip, half):
            reg = _region(dst[w], kinds[w], chip, half)
            return pltpu.make_async_remote_copy(src_ref=reg, dst_ref=reg, send_sem=send.at[3 * w + k],
                                                recv_sem=recv.at[3 * w + k], device_id=sib, device_id_type=MESH_IDS)

        give = [copy(w, k, 2 * px + py, c) for w in range(n) for k, (px, py) in enumerate(others)]
        for cp in give:
            cp.start()
        for w in range(n):
            for k, (px, py) in enumerate(others):
                copy(w, k, 2 * px + py, 1 - c).wait_recv()
        for cp in give:
            cp.wait_send()

    return _pcall(body, name=name, in_specs=_any_specs(n), out_specs=_any_specs(n),
                  out_shape=[jax.ShapeDtypeStruct(f.shape, f.dtype) for f in fulls],
                  input_output_aliases={i: i for i in range(n)},
                  scratch_shapes=[pltpu.SemaphoreType.DMA((3 * n,)), pltpu.SemaphoreType.DMA((3 * n,))])(*fulls)


def _pair_exchange(grads, kinds, name):
    n = len(grads)

    def body(*refs):
        src, theirs_ref = refs[:n], refs[n:2 * n]
        send, recv = refs[2 * n:]
        x, y, c, _ = _position()
        sib = (x, y, 1 - c)

        def half(w, h):
            if kinds[w] == 'row':
                hr = src[w].shape[1] // 2
                return src[w].at[:, _ds(h * hr, hr), :]
            hr = src[w].shape[0] // 2
            return src[w].at[_ds(h * hr, hr), :]

        give = [pltpu.make_async_remote_copy(src_ref=half(w, 1 - c), dst_ref=theirs_ref[w], send_sem=send.at[w],
                                             recv_sem=recv.at[w], device_id=sib, device_id_type=MESH_IDS)
                for w in range(n)]
        for cp in give:
            cp.start()
        for cp in give:
            cp.wait()

    def half_shape(g, kind):
        return (g.shape[0], g.shape[1] // 2, g.shape[2]) if kind == 'row' else (g.shape[0] // 2, g.shape[1])

    return _pcall(body, name=name, in_specs=_any_specs(n), out_specs=_any_specs(n),
                  out_shape=[jax.ShapeDtypeStruct(half_shape(g, kd), g.dtype) for g, kd in zip(grads, kinds)],
                  scratch_shapes=[pltpu.SemaphoreType.DMA((n,)), pltpu.SemaphoreType.DMA((n,))])(*grads)


def _part_dims(part, kind):
    return (part.shape[1], part.shape[2]) if kind == 'row' else (part.shape[0], part.shape[1] // N_CHIPS)


def _chip_plan(kinds):
    def plan(bufs, lands, send, recv):
        x, y, c, others = _position()

        def slot(w, chip):
            if kinds[w] == 'row':
                return bufs[w].at[chip]
            cw = _part_dims(bufs[w], kinds[w])[1]
            return bufs[w].at[:, _ds(chip * cw, cw)]

        return [pltpu.make_async_remote_copy(src_ref=slot(w, 2 * px + py), dst_ref=lands[w].at[k],
                                             send_sem=send.at[3 * w + k], recv_sem=recv.at[3 * w + k],
                                             device_id=(px, py, c), device_id_type=MESH_IDS)
                for w in range(len(bufs)) for k, (px, py) in enumerate(others)]

    plan.n_copies = 3 * len(kinds)
    return plan


def _pair_gather(grads, name):
    n = len(grads)
    n_layers = grads[0].shape[0]

    def body(*refs):
        dst = refs[n:2 * n]
        send, recv = refs[2 * n:]
        x, y, c, _ = _position()
        sib = (x, y, 1 - c)

        def copy(w, l, half):
            hr = dst[w].shape[1] // 2
            rows = dst[w].at[l, _ds(half * hr, hr), :]
            return pltpu.make_async_remote_copy(src_ref=rows, dst_ref=rows, send_sem=send.at[w * n_layers + l],
                                                recv_sem=recv.at[w * n_layers + l], device_id=sib,
                                                device_id_type=MESH_IDS)

        give = [copy(w, l, c) for w in range(n) for l in range(n_layers)]
        for cp in give:
            cp.start()
        for w in range(n):
            for l in range(n_layers):
                copy(w, l, 1 - c).wait_recv()
        for cp in give:
            cp.wait_send()

    m = n * n_layers
    return _pcall(body, name=name, in_specs=_any_specs(n), out_specs=_any_specs(n),
                  out_shape=[jax.ShapeDtypeStruct(g.shape, g.dtype) for g in grads],
                  input_output_aliases={i: i for i in range(n)},
                  scratch_shapes=[pltpu.SemaphoreType.DMA((m,)), pltpu.SemaphoreType.DMA((m,))])(*grads)


def _all_sum(buf, name):
    rows = buf.shape[0]
    flips = [(fx, fy, fc) for fx in (0, 1) for fy in (0, 1) for fc in (0, 1)][1:]

    def body(x_ref, o_ref, land, send, recv):
        x, y, c, _ = _position()
        me = 4 * x + 2 * y + c

        def peer(f):
            return (1 - x if f[0] else x, 1 - y if f[1] else y, 1 - c if f[2] else c)

        give = [pltpu.make_async_remote_copy(src_ref=x_ref, dst_ref=land.at[me], send_sem=send.at[k],
                                             recv_sem=recv.at[k], device_id=peer(f), device_id_type=MESH_IDS)
                for k, f in enumerate(flips)]
        for cp in give:
            cp.start()
        land[me] = x_ref[...]
        for k, f in enumerate(flips):
            px, py, pc = peer(f)
            slot = land.at[4 * px + 2 * py + pc]
            pltpu.make_async_remote_copy(src_ref=slot, dst_ref=slot, send_sem=send.at[k], recv_sem=recv.at[k],
                                         device_id=peer(f), device_id_type=MESH_IDS).wait_recv()
        for cp in give:
            cp.wait_send()
        total = land[0]
        for dev in range(1, N_DEV):
            total = total + land[dev]
        o_ref[...] = total

    return _pcall(body, name=name, in_specs=[pl.BlockSpec(memory_space=pltpu.VMEM)],
                  out_specs=pl.BlockSpec(memory_space=pltpu.VMEM), out_shape=jax.ShapeDtypeStruct(buf.shape, F32),
                  scratch_shapes=[pltpu.VMEM((N_DEV, rows, LANES), F32), pltpu.SemaphoreType.DMA((N_DEV - 1,)),
                                  pltpu.SemaphoreType.DMA((N_DEV - 1,))],
                  compiler_params=pltpu.CompilerParams(vmem_limit_bytes=VMEM_LIMIT))(buf)


def _pack(arrays):
    flat = jnp.concatenate([a.reshape(-1).astype(F32) for a in arrays])
    pad = (-flat.shape[0]) % (8 * LANES)
    return jnp.pad(flat, (0, pad)).reshape(-1, LANES)


def _unpack(buf, like):
    flat, out, off = buf.reshape(-1), [], 0
    for a in like:
        out.append(flat[off:off + a.size].reshape(a.shape))
        off += a.size
    return out


def _row2(v):
    return v.reshape(1, -1)


def _ffn_fwd(x, g_pre, w_gu, w_d, g_post, tag, after=None):
    h = _norm_fwd(x, g_pre, tag + "_pre", after)
    gu = _mm(h, w_gu, 'nn', F32, tag + "_gu")
    a = _swiglu_fwd(gu, tag + "_act")
    f = _mm(a, w_d, 'nn', F32, tag + "_down")
    return _resnorm_fwd(x, f, g_post, 0.5, tag + "_post"), (x, h, gu, a, f)


def _ffn_bwd(dx_out, saved, g_pre, w_gu, w_d, g_post, tag, after=None):
    x, h, gu, a, f = saved
    df, dg_post = _resnorm_bwd(f, dx_out, g_post, 0.5, tag + "_post_b", after)
    da = _mm(df, w_d, 'nt', F32, tag + "_down_bx")
    dw_d = _mm(a, df, 'tn', BF16, tag + "_down_bw")
    dgu = _swiglu_bwd(gu, da, tag + "_act_b")
    dh = _mm(dgu, w_gu, 'nt', F32, tag + "_gu_bx")
    dw_gu = _mm(h, dgu, 'tn', BF16, tag + "_gu_bw")
    dx, dg_pre = _norm_bwd(x, dh, dx_out, g_pre, tag + "_pre_b")
    return dx, dg_pre, dw_gu, dw_d, dg_post


def _split_cols(pp, cs, cp, widths):
    proj = jnp.concatenate([pp[:, j * cp:j * cp + cs] for j in range(N_CHIPS)], axis=1)
    out, off = [], 0
    for wd in widths:
        out.append(proj[:, off:off + wd])
        off += wd
    return out


def _join_cols(pieces, cs, cp):
    proj = jnp.concatenate(pieces, axis=1)
    t = proj.shape[0]
    cols = []
    for j in range(N_CHIPS):
        cols += [proj[:, j * cs:(j + 1) * cs], jnp.zeros((t, cp - cs), proj.dtype)]
    return jnp.concatenate(cols, axis=1)


def _mix_fwd(x, p, w_in, w_out, conv_w, cs, cp, tag, after=None):
    t, d = x.shape
    half = d // 2
    nh = half // HEAD_DIM
    h = _norm_fwd(x, p['mix_norm_pre'], tag + "_pre", after)
    pp = _mm(h, w_in, 'nn', F32, tag + "_in")
    u_a, v_a, qkv_raw, z, b_raw, a_raw = _split_cols(pp, cs, cp, [half, half, 3 * half, half, nh, nh])
    y_a = _sgu_fwd(u_a, v_a, p['sm_w'], p['sm_b'], p['sm_ln_g'], p['sm_ln_b'], tag + "_sgu")
    qkv = _conv_fwd(qkv_raw, conv_w, tag + "_conv")
    b_raw, a_raw = b_raw.T, a_raw.T
    beta, gcum = _gates_fwd(b_raw, a_raw, p['a_log'].T, p['dt_bias'].T, tag + "_gates")
    gcol, bcol = gcum[:, :, None], beta[:, :, None]
    grow = gcum.reshape(nh, t // DN_CHUNK, 1, DN_CHUNK)
    o, states = _delta_fwd(qkv, gcol, grow, bcol, tag + "_delta")
    y_b = _outgate_fwd(o, z, p['dn_norm_w'], tag + "_gate")
    y = jnp.concatenate([y_a, y_b], axis=1)
    m = _mm(y, w_out, 'nn', F32, tag + "_out")
    x_out = _resnorm_fwd(x, m, p['mix_norm_post'], 1.0, tag + "_post")
    return x_out, (x, h, u_a, v_a, qkv_raw, z, b_raw, a_raw, qkv, gcol, grow, bcol, states, o, y, m)


def _mix_bwd(dx_out, saved, p, w_in, w_out, conv_w, cs, cp, tag, after=None):
    x, h, u_a, v_a, qkv_raw, z, b_raw, a_raw, qkv, gcol, grow, bcol, states, o, y, m = saved
    t, d = x.shape
    half = d // 2
    nh = half // HEAD_DIM
    gr = {}
    dm, gr['mix_norm_post'] = _resnorm_bwd(m, dx_out, p['mix_norm_post'], 1.0, tag + "_post_b", after)
    dy = _mm(dm, w_out, 'nt', F32, tag + "_out_bx")
    gr['w_out'] = _mm(y, dm, 'tn', BF16, tag + "_out_bw")
    dy_a, dy_b = dy[:, :half], dy[:, half:]
    do, dz, gr['dn_norm_w'] = _outgate_bwd(o, z, p['dn_norm_w'], dy_b, tag + "_gate_b")
    dq, dk, dv, dgcol, dgrow, dbcol = _delta_bwd(qkv, gcol, grow, bcol, states, do, tag + "_delta_b")
    dgcum = dgcol[:, :, 0] + dgrow.reshape(nh, t)
    db_raw, da_raw, gr['a_log'], gr['dt_bias'] = _gates_bwd(b_raw, a_raw, p['a_log'].T, p['dt_bias'].T,
                                                             dbcol[:, :, 0], dgcum, tag + "_gates_b")
    db_raw, da_raw = db_raw.T, da_raw.T
    thirds = [_conv_bwd(qkv_raw, conv_w, dpart, i * half, tag + "_conv_b") for i, dpart in enumerate((dq, dk, dv))]
    gr['conv_w'] = jnp.concatenate([dw for _, dw in thirds], axis=1)
    du_a, dv_a, gr['sm_w'], gr['sm_b'], gr['sm_ln_g'], gr['sm_ln_b'] = _sgu_bwd(
        u_a, v_a, p['sm_w'], p['sm_b'], p['sm_ln_g'], p['sm_ln_b'], dy_a, tag + "_sgu_b")
    dpp = _join_cols([du_a, dv_a, *[dx for dx, _ in thirds], dz, db_raw, da_raw], cs, cp).astype(BF16)
    dh = _mm(dpp, w_in, 'nt', F32, tag + "_in_bx")
    gr['w_in'] = _mm(h, dpp, 'tn', BF16, tag + "_in_bw")
    dx, gr['mix_norm_pre'] = _norm_bwd(x, dh, dx_out, p['mix_norm_pre'], tag + "_pre_b")
    return dx, gr


def _xa_fwd(x, mem, p, w_xq, w_xkv, w_xo, tag, after=None):
    h = _norm_fwd(x, p['xa_norm_pre'], tag + "_pre", after)
    mh = _norm_fwd(mem, p['mem_norm'], tag + "_mem")
    q = _mm(h, w_xq, 'nn', BF16, tag + "_q")
    kv = _mm(mh, w_xkv, 'nn', BF16, tag + "_kv")
    o = _attn_fwd(q, kv, tag + "_attn")
    c = _mm(o, w_xo, 'nn', F32, tag + "_o")
    return _resnorm_fwd(x, c, p['xa_norm_post'], 1.0, tag + "_post"), (x, h, mh, q, kv, o, c)


def _xa_bwd(dx_out, saved, mem, p, w_xq, w_xkv, w_xo, tag, after=None):
    x, h, mh, q, kv, o, c = saved
    gr = {}
    dc, gr['xa_norm_post'] = _resnorm_bwd(c, dx_out, p['xa_norm_post'], 1.0, tag + "_post_b", after)
    do = _mm(dc, w_xo, 'nt', F32, tag + "_o_bx")
    gr['w_xo'] = _mm(o, dc, 'tn', BF16, tag + "_o_bw")
    dq, dkv = _attn_bwd(q, kv, do, tag + "_attn_b")
    dkv = dkv.astype(BF16)
    dh = _mm(dq, w_xq, 'nt', F32, tag + "_q_bx")
    gr['w_xq'] = _mm(h, dq, 'tn', BF16, tag + "_q_bw")
    dmh = _mm(dkv, w_xkv, 'nt', F32, tag + "_kv_bx")
    gr['w_xkv'] = _mm(mh, dkv, 'tn', BF16, tag + "_kv_bw")
    gr['mem_norm'] = _norm_bwd_gain(mem, dmh, p['mem_norm'], tag + "_mem_b")
    dx, gr['xa_norm_pre'] = _norm_bwd(x, dh, dx_out, p['xa_norm_pre'], tag + "_pre_b")
    return dx, gr


def _step(inputs):
    x, mem, target = inputs['x'][0], inputs['mem'][0], inputs['loss_target'][0]
    n_layers = inputs['w_in'].shape[0]
    cx, cy, cc = lax.axis_index("x"), lax.axis_index("y"), lax.axis_index("c")
    chip = 2 * cx + cy
    cs = inputs['w_in'].shape[2]
    cp = -(-cs // LANES) * LANES
    kinds = [BIG[nm] for nm in BIG_NAMES]

    conv_local = inputs['conv_w']
    ccols = conv_local.shape[2]
    placed = lax.dynamic_update_slice(jnp.zeros((n_layers, CONV_WIDTH, N_CHIPS * ccols), F32),
                                      jnp.where(cc == 0, conv_local, 0.0), (0, 0, chip * ccols))
    conv_full = _unpack(_all_sum(_pack([placed]), "conv_gather"), [placed])[0]

    core_idx, chip_idx = cc.astype(jnp.int32).reshape(1), chip.astype(jnp.int32).reshape(1)
    where_idx = jnp.stack([chip, cc]).astype(jnp.int32)
    order = [(l, gi) for l in range(n_layers) for gi in range(len(GROUPS))]
    n_groups = len(order)

    def small(l):
        p = {nm: inputs[nm][l] for nm in SMALL_NAMES}
        for nm in SMALL_NAMES:
            if p[nm].ndim == 1:
                p[nm] = _row2(p[nm])
        p['sm_b'] = p['sm_b'][:, :, None]
        return p

    def gather_start(k, after):
        l, gi = order[k]
        names = GROUPS[gi][1]
        kds = [BIG[nm] for nm in names]
        lands = []
        for nm, kd in zip(names, kds):
            s = inputs[nm][l]
            if nm == 'w_in':
                s = jnp.pad(s, ((0, 0), (0, cp - cs)))
            lands.append(_place_block(lax.empty(_full_shape(s.shape, kd), BF16), s, kd, chip_idx, "place_block"))
        plan, wait_plan = _gather_plan(kds)
        send, recv, _, lands, token = _split_copy_start([], lands, plan, after, f"gather_start_{k}")
        return dict(send=send, recv=recv, lands=lands, token=token, kinds=kds, names=names, wait_plan=wait_plan)

    def gather_finish(k, st, after):
        _, lands = _split_copy_wait(st['send'], st['recv'], [], st['lands'], st['wait_plan'], after, f"gather_wait_{k}")
        lands = _gather_forward(lands, st['kinds'], "gather_forward_" + GROUPS[order[k][1]][0])
        return {nm: (g.reshape(-1, g.shape[2]) if kd == 'row' else g)
                for nm, g, kd in zip(st['names'], lands, st['kinds'])}

    def group_fwd(gi, x_, p, w, l, token):
        if gi == 0:
            return _ffn_fwd(x_, p['ffn1_norm_pre'], w['ffn1_w_gate_up'], w['ffn1_w_down'], p['ffn1_norm_post'], "ffn", token)
        if gi == 1:
            return _mix_fwd(x_, p, w['w_in'], w['w_out'], conv_full[l], cs, cp, "mix", token)
        if gi == 2:
            return _xa_fwd(x_, mem, p, w['w_xq'], w['w_xkv'], w['w_xo'], "xa", token)
        return _ffn_fwd(x_, p['ffn2_norm_pre'], w['ffn2_w_gate_up'], w['ffn2_w_down'], p['ffn2_norm_post'], "ffn", token)

    def group_bwd(gi, dx_, s, p, w, l, token):
        if gi in (0, 3):
            pre = 'ffn1' if gi == 0 else 'ffn2'
            dx_, g_pre, g_gu, g_d, g_post = _ffn_bwd(dx_, s, p[pre + '_norm_pre'], w[pre + '_w_gate_up'],
                                                     w[pre + '_w_down'], p[pre + '_norm_post'], "ffn", token)
            return dx_, {pre + '_norm_pre': g_pre, pre + '_w_gate_up': g_gu, pre + '_w_down': g_d,
                         pre + '_norm_post': g_post}
        if gi == 1:
            return _mix_bwd(dx_, s, p, w['w_in'], w['w_out'], conv_full[l], cs, cp, "mix", token)
        return _xa_bwd(dx_, s, mem, p, w['w_xq'], w['w_xkv'], w['w_xo'], "xa", token)

    started = {0: gather_start(0, None)}
    started[1] = gather_start(1, started[0]['token'])
    weights, saved = [], []
    for k, (l, gi) in enumerate(order):
        w = gather_finish(k, started[k], x if k > 0 else started[1]['token'])
        token = None
        if k + 2 < n_groups:
            started[k + 2] = gather_start(k + 2, next(iter(w.values())))
            token = started[k + 2]['token']
        weights.append(w)
        x, s = group_fwd(gi, x, small(l), w, l, token)
        saved.append(s)

    dx, loss_part = _loss_call(x, target, "loss")

    grads = [dict() for _ in range(n_layers)]
    big_grads = {nm: None for nm in BIG_NAMES}

    def reduce_finish(pd, after):
        parts, lands = _split_copy_wait(pd['send'], pd['recv'], pd['parts'], pd['lands'], pd['plan'], after,
                                        f"chip_wait_{pd['k']}")
        for nm, kd, part, landed in zip(pd['names'], pd['kinds'], parts, lands):
            if big_grads[nm] is None:
                big_grads[nm] = lax.empty((n_layers, 2 * landed.shape[1], landed.shape[2]), F32)
            big_grads[nm] = _chip_sum(part, landed, kd, where_idx, big_grads[nm], pd['l'], "chip_sum")

    pending = None
    for k in reversed(range(n_groups)):
        l, gi = order[k]
        dx, gr = group_bwd(gi, dx, saved[k], small(l), weights[k], l, pending['token'] if pending else None)
        if pending:
            reduce_finish(pending, dx)
        names = GROUPS[gi][1]
        kds = [BIG[nm] for nm in names]
        parts = []
        gs = [gr[nm].reshape(N_CHIPS, -1, gr[nm].shape[1]) if kd == 'row' else gr[nm] for nm, kd in zip(names, kds)]
        theirs = _pair_exchange(gs, kds, "pair_exchange_" + GROUPS[gi][0])
        for g, t, kd in zip(gs, theirs, kds):
            parts.append(_pair_add(g, t, core_idx, "pair_add") if kd == 'row'
                         else _pair_add(g[None], t[None], core_idx, "pair_add")[0])
        lands = [lax.empty((3, *_part_dims(part, kd)), BF16) for part, kd in zip(parts, kds)]
        plan = _chip_plan(kds)
        send, recv, parts, lands, token = _split_copy_start(parts, lands, plan, None, f"chip_start_{k}")
        pending = dict(send=send, recv=recv, parts=parts, lands=lands, token=token, plan=plan, names=names, kinds=kds,
                       k=k, l=l)
        grads[l].update({nm: g for nm, g in gr.items() if nm not in BIG})
    reduce_finish(pending, None)
    big_grads = dict(zip(BIG_NAMES, _pair_gather([big_grads[nm] for nm in BIG_NAMES], "pair_gather")))
    big_grads['w_in'] = big_grads['w_in'][:, :, :cs]

    small_grads = [jnp.stack([grads[l][nm].reshape(inputs[nm].shape[1:]) if nm != 'conv_w' else grads[l][nm]
                              for l in range(n_layers)]) for nm in SMALL_NAMES]
    summed = _unpack(_all_sum(_pack(small_grads + [loss_part]), "small_sum"), small_grads + [loss_part])
    loss = summed[-1][0, 0]
    small_g = dict(zip(SMALL_NAMES, summed[:-1]))
    small_g['conv_w'] = lax.dynamic_slice(small_g['conv_w'], (0, 0, chip * ccols), (n_layers, CONV_WIDTH, ccols))

    out = {}
    for nm in BIG_NAMES:
        g = big_grads[nm]
        two = lambda a: a.reshape(-1, a.shape[-1])
        delta, m1, v1 = _adamw_call(two(inputs[nm]), two(g), two(inputs['m_' + nm]), two(inputs['v_' + nm]), "adamw")
        out[nm] = (g, delta.reshape(g.shape), m1.reshape(g.shape), v1.reshape(g.shape))
    sw = [inputs[nm] for nm in SMALL_NAMES]
    packed = [_pack([small_g[nm] for nm in SMALL_NAMES])] + [_pack([inputs[pre + nm] for nm in SMALL_NAMES])
                                                            for pre in ('', 'm_', 'v_')]
    delta, m1, v1 = _adamw_call(packed[1], packed[0], packed[2], packed[3], "adamw_small")
    for nm, d_, m_, v_ in zip(SMALL_NAMES, _unpack(delta, sw), _unpack(m1, sw), _unpack(v1, sw)):
        out[nm] = (small_g[nm], d_, m_, v_)
    return (loss, dx[None], *[out[nm][0] for nm in WEIGHTS], *[out[nm][1] for nm in WEIGHTS],
            *[out[nm][2] for nm in WEIGHTS], *[out[nm][3] for nm in WEIGHTS])


def kernel(x, mem, ffn1_norm_pre, ffn1_w_gate_up, ffn1_w_down, ffn1_norm_post, mix_norm_pre, w_in, conv_w, a_log, dt_bias, sm_w, sm_b, sm_ln_g, sm_ln_b, dn_norm_w, w_out, mix_norm_post, xa_norm_pre, mem_norm, w_xq, w_xkv, w_xo, xa_norm_post, ffn2_norm_pre, ffn2_w_gate_up, ffn2_w_down, ffn2_norm_post, loss_target, m_ffn1_norm_pre, m_ffn1_w_gate_up, m_ffn1_w_down, m_ffn1_norm_post, m_mix_norm_pre, m_w_in, m_conv_w, m_a_log, m_dt_bias, m_sm_w, m_sm_b, m_sm_ln_g, m_sm_ln_b, m_dn_norm_w, m_w_out, m_mix_norm_post, m_xa_norm_pre, m_mem_norm, m_w_xq, m_w_xkv, m_w_xo, m_xa_norm_post, m_ffn2_norm_pre, m_ffn2_w_gate_up, m_ffn2_w_down, m_ffn2_norm_post, v_ffn1_norm_pre, v_ffn1_w_gate_up, v_ffn1_w_down, v_ffn1_norm_post, v_mix_norm_pre, v_w_in, v_conv_w, v_a_log, v_dt_bias, v_sm_w, v_sm_b, v_sm_ln_g, v_sm_ln_b, v_dn_norm_w, v_w_out, v_mix_norm_post, v_xa_norm_pre, v_mem_norm, v_w_xq, v_w_xkv, v_w_xo, v_xa_norm_post, v_ffn2_norm_pre, v_ffn2_w_gate_up, v_ffn2_w_down, v_ffn2_norm_post):
    vals = dict(locals())
    return _step(vals)
```

```python
import functools
import math

import jax
import jax.numpy as jnp
from jax import lax
from jax.experimental import pallas as pl
from jax.experimental.pallas import tpu as pltpu

F32, BF16 = jnp.float32, jnp.bfloat16
NORM_EPS = 1e-6
HEAD_DIM = 128
GM_CHUNK = 128
DN_CHUNK = 64
CONV_WIDTH = 4
XA_HEADS = 4
LANES = 128
N_CHIPS = 4
N_DEV = 8
VMEM_LIMIT = 56 * 1024 * 1024
MESH_IDS = pl.DeviceIdType.MESH
ADAM_LR, ADAM_B1, ADAM_B2, ADAM_EPS, ADAM_WD, ADAM_STEP = 0.001, 0.9, 0.999, 1e-08, 0.01, 10

WEIGHTS = ['ffn1_norm_pre', 'ffn1_w_gate_up', 'ffn1_w_down', 'ffn1_norm_post', 'mix_norm_pre', 'w_in', 'conv_w',
           'a_log', 'dt_bias', 'sm_w', 'sm_b', 'sm_ln_g', 'sm_ln_b', 'dn_norm_w', 'w_out', 'mix_norm_post',
           'xa_norm_pre', 'mem_norm', 'w_xq', 'w_xkv', 'w_xo', 'xa_norm_post', 'ffn2_norm_pre', 'ffn2_w_gate_up',
           'ffn2_w_down', 'ffn2_norm_post']
BIG = {'ffn1_w_gate_up': 'col', 'ffn1_w_down': 'row', 'w_in': 'col', 'w_out': 'row', 'w_xq': 'row',
       'w_xkv': 'col', 'w_xo': 'row', 'ffn2_w_gate_up': 'col', 'ffn2_w_down': 'row'}
BIG_NAMES = list(BIG)
GROUPS = (('ffn1', ('ffn1_w_gate_up', 'ffn1_w_down')), ('mix', ('w_in', 'w_out')),
          ('xa', ('w_xq', 'w_xkv', 'w_xo')), ('ffn2', ('ffn2_w_gate_up', 'ffn2_w_down')))
SMALL_NAMES = [n for n in WEIGHTS if n not in BIG]


def _pcall(body, **kw):
    return pl.pallas_call(body, **kw)


def _params(sem=None):
    return pltpu.CompilerParams(dimension_semantics=sem, vmem_limit_bytes=VMEM_LIMIT)


def _as_list(after):
    if after is None:
        return []
    return list(after) if isinstance(after, (list, tuple)) else [after]


def _divtile(dim, cap, align=LANES):
    if dim <= cap:
        return dim
    t = (cap // align) * align
    while t > align and dim % t:
        t -= align
    assert dim % t == 0, (dim, cap)
    return t


_DN = {'nn': (((1,), (0,)), ((), ())), 'nt': (((1,), (1,)), ((), ())), 'tn': (((0,), (0,)), ((), ()))}


def _mm(a, b, mode, out_dtype, name):
    if mode == 'nn':
        (m, k), (k2, n) = a.shape, b.shape
    elif mode == 'nt':
        (m, k), (n, k2) = a.shape, b.shape
    else:
        (k, m), (k2, n) = a.shape, b.shape
    assert k == k2, (a.shape, b.shape, mode)
    tm, tn, tk = _divtile(m, 1024), _divtile(n, 1024), _divtile(k, 2048)
    nk = k // tk
    a_spec = {'nn': pl.BlockSpec((tm, tk), lambda i, j, kk: (i, kk)),
              'nt': pl.BlockSpec((tm, tk), lambda i, j, kk: (i, kk)),
              'tn': pl.BlockSpec((tk, tm), lambda i, j, kk: (kk, i))}[mode]
    b_spec = {'nn': pl.BlockSpec((tk, tn), lambda i, j, kk: (kk, j)),
              'nt': pl.BlockSpec((tn, tk), lambda i, j, kk: (j, kk)),
              'tn': pl.BlockSpec((tk, tn), lambda i, j, kk: (kk, j))}[mode]

    def dot(a_ref, b_ref):
        return lax.dot_general(a_ref[...].astype(BF16), b_ref[...].astype(BF16), _DN[mode], preferred_element_type=F32)

    def body_once(a_ref, b_ref, o_ref):
        o_ref[...] = dot(a_ref, b_ref).astype(o_ref.dtype)

    def body_steps(a_ref, b_ref, o_ref, acc):
        kk = pl.program_id(2)

        @pl.when(kk == 0)
        def _():
            acc[...] = jnp.zeros_like(acc)

        acc[...] += dot(a_ref, b_ref)

        @pl.when(kk == nk - 1)
        def _():
            o_ref[...] = acc[...].astype(o_ref.dtype)

    return _pcall(
        body_once if nk == 1 else body_steps, name=name, grid=(m // tm, n // tn, nk),
        in_specs=[a_spec, b_spec], out_specs=pl.BlockSpec((tm, tn), lambda i, j, kk: (i, j)),
        out_shape=jax.ShapeDtypeStruct((m, n), out_dtype),
        scratch_shapes=[] if nk == 1 else [pltpu.VMEM((tm, tn), F32)],
        compiler_params=_params(("parallel", "parallel", "arbitrary")),
    )(a, b)


def _rowcall(name, f, rows, params=(), row_outs=(), acc_outs=(), br=256, nrows=None, after=None):
    rows = [r if isinstance(r, tuple) else (r, 0) for r in rows]
    t = nrows if nrows is not None else rows[0][0].shape[0]
    br = min(br, t)
    assert t % br == 0, (name, t, br)
    n_in, n_ro = len(rows) + len(params), len(row_outs)
    extra = _as_list(after)

    def row_spec(arr, off):
        assert off % br == 0
        return pl.BlockSpec((br, arr.shape[1]), functools.partial(lambda i, o: (i + o, 0), o=off // br))

    def whole_spec(shape):
        return pl.BlockSpec(shape, functools.partial(lambda i, nd: (0,) * nd, nd=len(shape)))

    def body(*refs):
        res = f(*[r[...] for r in refs[:n_in]])
        outs = refs[n_in + len(extra):]
        for o_ref, val in zip(outs[:n_ro], res[:n_ro]):
            o_ref[...] = val.astype(o_ref.dtype)
        if acc_outs:
            @pl.when(pl.program_id(0) == 0)
            def _():
                for o_ref in outs[n_ro:]:
                    o_ref[...] = jnp.zeros_like(o_ref)

            for o_ref, val in zip(outs[n_ro:], res[n_ro:]):
                o_ref[...] += val.astype(F32)

    out = _pcall(
        body, name=name, grid=(t // br,),
        in_specs=[row_spec(a, o) for a, o in rows] + [whole_spec(p.shape) for p in params]
        + [pl.BlockSpec(memory_space=pl.ANY)] * len(extra),
        out_specs=[pl.BlockSpec((br, c), lambda i: (i, 0)) for c, _ in row_outs] + [whole_spec(s) for s in acc_outs],
        out_shape=[jax.ShapeDtypeStruct((t, c), dt) for c, dt in row_outs]
        + [jax.ShapeDtypeStruct(s, F32) for s in acc_outs],
        compiler_params=_params(("arbitrary",) if acc_outs else ("parallel",)),
    )(*[a for a, _ in rows], *params, *extra)
    return out


_DN_BATCH = {'nn': (((2,), (1,)), ((0,), (0,))), 'nt': (((2,), (2,)), ((0,), (0,))), 'tn': (((1,), (1,)), ((0,), (0,)))}


def _dims(a, dn):
    return _DN_BATCH[dn] if a.ndim == 3 else _DN[dn]


def _bdot_raw(a, b, dn):
    return lax.dot_general(a.astype(BF16), b.astype(BF16), _dims(a, dn), preferred_element_type=F32)


def _hdot_raw(a, b, dn):
    a_hi, b_hi = a.astype(BF16), b.astype(BF16)
    a_lo, b_lo = (a - a_hi.astype(F32)).astype(BF16), (b - b_hi.astype(F32)).astype(BF16)

    def dot(p, q):
        return lax.dot_general(p, q, _dims(a, dn), preferred_element_type=F32)

    return dot(a_hi, b_hi) + (dot(a_hi, b_lo) + dot(a_lo, b_hi))


def _with_vjp(raw):
    @functools.partial(jax.custom_vjp, nondiff_argnums=(2,))
    def dot(a, b, dn):
        return raw(a, b, dn)

    def fwd(a, b, dn):
        return raw(a, b, dn), (a, b)

    def bwd(dn, res, ct):
        a, b = res
        if dn == 'nn':
            da, db = raw(ct, b, 'nt'), raw(a, ct, 'tn')
        elif dn == 'nt':
            da, db = raw(ct, b, 'nn'), raw(ct, a, 'tn')
        else:
            da, db = raw(b, ct, 'nt'), raw(a, ct, 'nn')
        return da.astype(a.dtype), db.astype(b.dtype)

    dot.defvjp(fwd, bwd)
    return dot


_bdot, _hdot = _with_vjp(_bdot_raw), _with_vjp(_hdot_raw)


def _rms(x, g):
    return x * lax.rsqrt(jnp.mean(x * x, axis=-1, keepdims=True) + NORM_EPS) * g


def _sigmoid(x):
    return 1.0 / (1.0 + jnp.exp(-x))


def _silu(x):
    return x * _sigmoid(x)


def _gelu(x):
    return 0.5 * x * (1.0 + lax.erf(x * (2.0 ** -0.5)))


def _norm_fwd(x, g, name, after=None):
    return _rowcall(name, lambda x_, g_: (_rms(x_, g_),), [x], [g], [(x.shape[1], BF16)], after=after)[0]


def _resnorm_fwd(x, f, g, alpha, name):
    return _rowcall(name, lambda x_, f_, g_: (x_ + alpha * _rms(f_, g_),), [x, f], [g], [(x.shape[1], F32)])[0]


def _resnorm_bwd(f, dx, g, alpha, name, after=None):
    def fn(f_, dx_, g_):
        _, vjp = jax.vjp(lambda a, b: alpha * _rms(a, b), f_, g_)
        return vjp(dx_)

    return _rowcall(name, fn, [f, dx], [g], [(f.shape[1], BF16)], [g.shape], after=after)


def _norm_bwd(x, dh, dx_out, g, name):
    def fn(x_, dh_, dxo_, g_):
        _, vjp = jax.vjp(_rms, x_, g_)
        dx, dg = vjp(dh_)
        return dxo_ + dx, dg

    return _rowcall(name, fn, [x, dh, dx_out], [g], [(x.shape[1], F32)], [g.shape])


def _norm_bwd_gain(x, dh, g, name):
    def fn(x_, dh_, g_):
        _, vjp = jax.vjp(lambda b: _rms(x_, b), g_)
        return vjp(dh_)

    return _rowcall(name, fn, [x, dh], [g], [], [g.shape])[0]


def _swiglu_fwd(gu, name):
    ff = gu.shape[1] // 2
    return _rowcall(name, lambda gu_: (_silu(gu_[:, :ff]) * gu_[:, ff:],), [gu], [], [(ff, BF16)], br=64)[0]


def _swiglu_bwd(gu, da, name):
    ff = gu.shape[1] // 2

    def fn(gu_, da_):
        gate, up = gu_[:, :ff], gu_[:, ff:]
        s = _sigmoid(gate)
        dgate = da_ * up * (s * (1.0 + gate * (1.0 - s)))
        dup = da_ * (gate * s)
        return (jnp.concatenate([dgate, dup], axis=1),)

    return _rowcall(name, fn, [gu, da], [], [(2 * ff, BF16)], br=64)[0]


def _sgu_norm(v, lg, lb):
    vf = _gelu(v)
    mu = jnp.mean(vf, axis=-1, keepdims=True)
    var = jnp.mean(jnp.square(vf - mu), axis=-1, keepdims=True)
    return (vf - mu) * lax.rsqrt(var + NORM_EPS) * lg + lb


def _sgu_head(dot, u_h, vn_h, w_h, b_h):
    r = lax.broadcasted_iota(jnp.int32, w_h.shape, 0)
    c = lax.broadcasted_iota(jnp.int32, w_h.shape, 1)
    mixed = dot(jnp.where(r >= c, w_h, 0.0), vn_h, 'nn') + b_h
    return _gelu(u_h) * mixed


def _heads(width):
    return [slice(h * HEAD_DIM, (h + 1) * HEAD_DIM) for h in range(width // HEAD_DIM)]


def _sgu_fwd(u, v, w, bcol, lg, lb, name):
    def fn(u_, v_, w_, b_, lg_, lb_):
        vn = _sgu_norm(v_, lg_, lb_)
        return (jnp.concatenate([_sgu_head(_bdot_raw, u_[:, s], vn[:, s], w_[h], b_[h])
                                 for h, s in enumerate(_heads(u_.shape[1]))], axis=1),)

    return _rowcall(name, fn, [u, v], [w, bcol, lg, lb], [(u.shape[1], BF16)], br=GM_CHUNK)[0]


def _sgu_bwd(u, v, w, bcol, lg, lb, dy, name):
    def fn(u_, v_, dy_, w_, b_, lg_, lb_):
        vn, vjp_norm = jax.vjp(_sgu_norm, v_, lg_, lb_)
        du, dvn, dw, db = [], [], [], []
        for h, s in enumerate(_heads(u_.shape[1])):
            _, vjp = jax.vjp(functools.partial(_sgu_head, _bdot), u_[:, s], vn[:, s], w_[h], b_[h])
            a, b, c, d = vjp(dy_[:, s])
            du.append(a), dvn.append(b), dw.append(c[None]), db.append(d[None])
        dv, dlg, dlb = vjp_norm(jnp.concatenate(dvn, axis=1))
        return (jnp.concatenate(du, axis=1), dv, jnp.concatenate(dw, axis=0), jnp.concatenate(db, axis=0), dlg, dlb)

    wd = u.shape[1]
    return _rowcall(name, fn, [u, v, dy], [w, bcol, lg, lb], [(wd, F32), (wd, F32)],
                    [w.shape, bcol.shape, lg.shape, lb.shape], br=GM_CHUNK)


def _shift_down(x, s):
    if s == 0:
        return x
    rows = lax.broadcasted_iota(jnp.int32, x.shape, 0)
    return jnp.where(rows >= s, pltpu.roll(x, s, 0), 0.0)


def _shift_up(x, s):
    if s == 0:
        return x
    t = x.shape[0]
    rows = lax.broadcasted_iota(jnp.int32, x.shape, 0)
    return jnp.where(rows < t - s, pltpu.roll(x, t - s, 0), 0.0)


def _conv_pre(x, taps):
    acc = None
    for i in range(CONV_WIDTH):
        term = _shift_down(x, CONV_WIDTH - 1 - i) * taps[i]
        acc = term if acc is None else acc + term
    return acc


def _taps(w_ref):
    return [w_ref[i:i + 1, :] for i in range(CONV_WIDTH)]


def _conv_fwd(x, w, name):
    t, ch = x.shape
    cb = _divtile(ch, 512)

    def body(x_ref, w_ref, o_ref):
        o_ref[...] = _silu(_conv_pre(x_ref[...], _taps(w_ref)))

    return _pcall(body, name=name, grid=(ch // cb,),
                  in_specs=[pl.BlockSpec((t, cb), lambda j: (0, j)), pl.BlockSpec((CONV_WIDTH, cb), lambda j: (0, j))],
                  out_specs=pl.BlockSpec((t, cb), lambda j: (0, j)),
                  out_shape=jax.ShapeDtypeStruct((t, ch), F32), compiler_params=_params(("parallel",)))(x, w)


def _conv_bwd(x, w, dout, col_off, name):
    t, ch = dout.shape
    cb = _divtile(ch, 256)
    assert col_off % cb == 0
    off = col_off // cb

    def body(x_ref, w_ref, do_ref, dx_ref, dw_ref):
        xv, taps = x_ref[...], _taps(w_ref)
        pre = _conv_pre(xv, taps)
        s = _sigmoid(pre)
        dpre = do_ref[...] * (s * (1.0 + pre * (1.0 - s)))
        dx = None
        for i in range(CONV_WIDTH):
            sh = CONV_WIDTH - 1 - i
            term = _shift_up(dpre, sh) * taps[i]
            dx = term if dx is None else dx + term
            dw_ref[i:i + 1, :] = jnp.sum(dpre * _shift_down(xv, sh), axis=0, keepdims=True)
        dx_ref[...] = dx

    return _pcall(body, name=name, grid=(ch // cb,),
                  in_specs=[pl.BlockSpec((t, cb), lambda j: (0, j + off)),
                            pl.BlockSpec((CONV_WIDTH, cb), lambda j: (0, j + off)),
                            pl.BlockSpec((t, cb), lambda j: (0, j))],
                  out_specs=[pl.BlockSpec((t, cb), lambda j: (0, j)), pl.BlockSpec((CONV_WIDTH, cb), lambda j: (0, j))],
                  out_shape=[jax.ShapeDtypeStruct((t, ch), F32), jax.ShapeDtypeStruct((CONV_WIDTH, ch), F32)],
                  compiler_params=_params(("parallel",)))(x, w, dout)


GATE_BLOCK = 256


def _gates(dot, b_raw, a_raw, a_log, dt_bias):
    blk = b_raw.shape[1]
    z = a_raw + dt_bias
    softplus = jnp.maximum(z, 0.0) + jnp.log(1.0 + jnp.exp(-jnp.abs(z)))
    g = -jnp.exp(a_log) * softplus
    ri = lax.broadcasted_iota(jnp.int32, (blk, blk), 0)
    ci = lax.broadcasted_iota(jnp.int32, (blk, blk), 1)
    upto = ((ri <= ci) & (ri // DN_CHUNK == ci // DN_CHUNK)).astype(F32)
    return _sigmoid(b_raw), dot(g, upto, 'nn')


def _gate_blocks(t):
    blk = min(GATE_BLOCK, t)
    return [slice(i, i + blk) for i in range(0, t, blk)]


def _whole_call(name, f, ins, out_shapes):
    n_in = len(ins)

    def body(*refs):
        for o_ref, val in zip(refs[n_in:], f(*[r[...] for r in refs[:n_in]])):
            o_ref[...] = val

    vmem = pl.BlockSpec(memory_space=pltpu.VMEM)
    return _pcall(body, name=name, in_specs=[vmem] * n_in, out_specs=[vmem] * len(out_shapes),
                  out_shape=[jax.ShapeDtypeStruct(s, F32) for s in out_shapes],
                  compiler_params=pltpu.CompilerParams(vmem_limit_bytes=VMEM_LIMIT))(*ins)


def _gates_fwd(b_raw, a_raw, a_log, dt_bias, name):
    def fn(b_, a_, al_, dt_):
        parts = [_gates(_hdot_raw, b_[:, s], a_[:, s], al_, dt_) for s in _gate_blocks(b_.shape[1])]
        return [jnp.concatenate(p, axis=1) for p in zip(*parts)]

    return _whole_call(name, fn, [b_raw, a_raw, a_log, dt_bias], [b_raw.shape, b_raw.shape])


def _gates_bwd(b_raw, a_raw, a_log, dt_bias, dbeta, dgcum, name):
    def fn(b_, a_, al_, dt_, dbeta_, dgcum_):
        db, da, dal, ddt = [], [], None, None
        for s in _gate_blocks(b_.shape[1]):
            _, vjp = jax.vjp(functools.partial(_gates, _hdot), b_[:, s], a_[:, s], al_, dt_)
            p, q, r, u = vjp((dbeta_[:, s], dgcum_[:, s]))
            db.append(p), da.append(q)
            dal, ddt = (r, u) if dal is None else (dal + r, ddt + u)
        return jnp.concatenate(db, axis=1), jnp.concatenate(da, axis=1), dal, ddt

    return _whole_call(name, fn, [b_raw, a_raw, a_log, dt_bias, dbeta, dgcum],
                       [b_raw.shape, a_raw.shape, a_log.shape, dt_bias.shape])


def _unit_lower_inverse_raw(a):
    c = a.shape[-1]
    eye = (lax.broadcasted_iota(jnp.int32, (c, c), 0) == lax.broadcasted_iota(jnp.int32, (c, c), 1)).astype(F32)
    inv, p = eye - a, _hdot_raw(a, a, 'nn')
    n_fac = int(math.log2(c)) - 1
    for it in range(n_fac):
        inv = inv + _hdot_raw(inv, p, 'nn')
        if it < n_fac - 1:
            p = _hdot_raw(p, p, 'nn')
    return inv


@jax.custom_vjp
def _unit_lower_inverse(a):
    return _unit_lower_inverse_raw(a)


def _unit_lower_inverse_fwd(a):
    inv = _unit_lower_inverse_raw(a)
    return inv, inv


def _unit_lower_inverse_bwd(inv, ct):
    return (-_hdot_raw(_hdot_raw(inv, ct, 'tn'), inv, 'nt'),)


_unit_lower_inverse.defvjp(_unit_lower_inverse_fwd, _unit_lower_inverse_bwd)


def _halves_raw(x):
    return x[..., :x.shape[-1] // 2], x[..., x.shape[-1] // 2:]


@jax.custom_vjp
def _halves(x):
    return _halves_raw(x)


_halves.defvjp(lambda x: (_halves_raw(x), None), lambda _, ct: (jnp.concatenate(ct, axis=-1),))

_DELTA_OPS = (_bdot_raw, _hdot_raw, _unit_lower_inverse_raw, _halves_raw)
_DELTA_OPS_VJP = (_bdot, _hdot, _unit_lower_inverse, _halves)


def _delta_chunk(ops, state, q, k, v, gc, gr, bc):
    bd, hd, inverse, halves = ops
    c, d = q.shape[-2:]
    ri = lax.broadcasted_iota(jnp.int32, (c, c), 0)
    ci = lax.broadcasted_iota(jnp.int32, (c, c), 1)
    causal, strict = ri >= ci, ri > ci
    qn = q * lax.rsqrt(jnp.sum(q * q, axis=-1, keepdims=True) + NORM_EPS) * (d ** -0.5)
    kn = k * lax.rsqrt(jnp.sum(k * k, axis=-1, keepdims=True) + NORM_EPS)
    gcum = jnp.broadcast_to(gc, q.shape)
    decay = jnp.where(causal, jnp.exp(jnp.where(causal, gc - gr, 0.0)), 0.0)
    bb = jnp.broadcast_to(bc, q.shape)
    k_beta = kn * bb
    inv = inverse(jnp.where(strict, bd(k_beta, kn, 'nt') * decay, 0.0))
    uw = hd(inv, jnp.concatenate([v * bb, k_beta * jnp.exp(gcum)], axis=-1), 'nn')
    u, w = halves(uw)
    qk = jnp.where(causal, bd(qn, kn, 'nt') * decay, 0.0)
    v_new = u - bd(w, state, 'nn')
    o = bd(qn * jnp.exp(gcum), state, 'nn') + bd(qk, v_new, 'nn')
    last = lax.broadcasted_iota(jnp.int32, (c, d), 0) == c - 1
    g_last = jnp.sum(jnp.where(last, gcum, 0.0), axis=-2, keepdims=True)
    k_dec = kn * jnp.exp(g_last - gcum)
    return state * jnp.exp(g_last) + bd(k_dec, v_new, 'tn'), o


DN_HEADS_PER_STEP = 8


def _by_head(ref):
    return jnp.stack([ref[:, s] for s in _heads(ref.shape[1])])


def _delta_specs(nh, nc, rev):
    c, d = DN_CHUNK, HEAD_DIM
    hb = min(DN_HEADS_PER_STEP, nh)
    ng = nh // hb
    ch = (lambda n: nc - 1 - n) if rev else (lambda n: n)
    qkv = [pl.BlockSpec((c, hb * d), functools.partial(lambda h, n, o: (ch(n), o * ng + h), o=o)) for o in range(3)]
    col = pl.BlockSpec((hb, c, 1), lambda h, n: (h, ch(n), 0))
    row = pl.BlockSpec((hb, None, 1, c), lambda h, n: (h, ch(n), 0, 0))
    st = pl.BlockSpec((hb, None, d, d), lambda h, n: (h, ch(n), 0, 0))
    tok = pl.BlockSpec((c, hb * d), lambda h, n: (ch(n), h))
    return hb, ng, qkv, col, row, st, tok


def _delta_fwd(qkv, gcol, grow, bcol, name):
    t = qkv.shape[0]
    nh, nc, d = gcol.shape[0], t // DN_CHUNK, HEAD_DIM
    hb, ng, qkv_s, col, row, st, tok = _delta_specs(nh, nc, False)

    def body(q_ref, k_ref, v_ref, gc_ref, gr_ref, bc_ref, o_ref, st_ref, state):
        @pl.when(pl.program_id(1) == 0)
        def _():
            state[...] = jnp.zeros_like(state)

        s0 = state[...]
        st_ref[...] = s0
        s1, o = _delta_chunk(_DELTA_OPS, s0, _by_head(q_ref), _by_head(k_ref), _by_head(v_ref), gc_ref[...],
                             gr_ref[...], bc_ref[...])
        for j, s in enumerate(_heads(hb * d)):
            o_ref[:, s] = o[j]
        state[...] = s1

    return _pcall(body, name=name, grid=(ng, nc), in_specs=qkv_s + [col, row, col], out_specs=[tok, st],
                  out_shape=[jax.ShapeDtypeStruct((t, nh * d), F32), jax.ShapeDtypeStruct((nh, nc, d, d), F32)],
                  scratch_shapes=[pltpu.VMEM((hb, d, d), F32)],
                  compiler_params=_params(("parallel", "arbitrary")))(qkv, qkv, qkv, gcol, grow, bcol)


def _delta_bwd(qkv, gcol, grow, bcol, states, do, name):
    t = qkv.shape[0]
    nh, nc, d = gcol.shape[0], t // DN_CHUNK, HEAD_DIM
    hb, ng, qkv_s, col, row, st, tok = _delta_specs(nh, nc, True)

    def body(q_ref, k_ref, v_ref, gc_ref, gr_ref, bc_ref, st_ref, do_ref,
             dq_ref, dk_ref, dv_ref, dgc_ref, dgr_ref, dbc_ref, dstate):
        @pl.when(pl.program_id(1) == 0)
        def _():
            dstate[...] = jnp.zeros_like(dstate)

        _, vjp = jax.vjp(functools.partial(_delta_chunk, _DELTA_OPS_VJP), st_ref[...], _by_head(q_ref), _by_head(k_ref),
                         _by_head(v_ref), gc_ref[...], gr_ref[...], bc_ref[...])
        ds, dq, dk, dv, dgc, dgr, dbc = vjp((dstate[...], _by_head(do_ref)))
        for j, s in enumerate(_heads(hb * d)):
            dq_ref[:, s], dk_ref[:, s], dv_ref[:, s] = dq[j], dk[j], dv[j]
        dgc_ref[...], dgr_ref[...], dbc_ref[...] = dgc, dgr, dbc
        dstate[...] = ds

    tok_out = jax.ShapeDtypeStruct((t, nh * d), F32)
    return _pcall(body, name=name, grid=(ng, nc), in_specs=qkv_s + [col, row, col, st, tok],
                  out_specs=[tok, tok, tok, col, row, col],
                  out_shape=[tok_out, tok_out, tok_out, jax.ShapeDtypeStruct(gcol.shape, F32),
                             jax.ShapeDtypeStruct(grow.shape, F32), jax.ShapeDtypeStruct(bcol.shape, F32)],
                  scratch_shapes=[pltpu.VMEM((hb, d, d), F32)],
                  compiler_params=_params(("parallel", "arbitrary")))(qkv, qkv, qkv, gcol, grow, bcol, states, do)


def _outgate_head(o_h, z_h, w):
    return _rms(o_h, w) * _silu(z_h)


def _outgate_fwd(o, z, w, name):
    def fn(o_, z_, w_):
        return (jnp.concatenate([_outgate_head(o_[:, s], z_[:, s], w_) for s in _heads(o_.shape[1])], axis=1),)

    return _rowcall(name, fn, [o, z], [w], [(o.shape[1], BF16)])[0]


def _outgate_bwd(o, z, w, dy, name):
    def fn(o_, z_, dy_, w_):
        do, dz, dw = [], [], None
        for s in _heads(o_.shape[1]):
            _, vjp = jax.vjp(_outgate_head, o_[:, s], z_[:, s], w_)
            a, b, c = vjp(dy_[:, s])
            do.append(a), dz.append(b)
            dw = c if dw is None else dw + c
        return jnp.concatenate(do, axis=1), jnp.concatenate(dz, axis=1), dw

    wd = o.shape[1]
    return _rowcall(name, fn, [o, z, dy], [w], [(wd, F32), (wd, F32)], [w.shape])


def _attn_head(dot, q_h, k_h, v_h):
    s = dot(q_h, k_h, 'nt') * (q_h.shape[1] ** -0.5)
    e = jnp.exp(s - lax.stop_gradient(jnp.max(s, axis=-1, keepdims=True)))
    p = e / jnp.sum(e, axis=-1, keepdims=True)
    return dot(p, v_h, 'nn')


def _xa_slices(width):
    hd = width // XA_HEADS
    return [slice(h * hd, (h + 1) * hd) for h in range(XA_HEADS)]


def _attn_fwd(q, kv, name):
    wd = q.shape[1]

    def fn(q_, kv_):
        k_, v_ = kv_[:, :wd], kv_[:, wd:]
        return (jnp.concatenate([_attn_head(_bdot_raw, q_[:, s], k_[:, s], v_[:, s]) for s in _xa_slices(wd)],
                                axis=1),)

    return _rowcall(name, fn, [q], [kv], [(wd, BF16)])[0]


def _attn_bwd(q, kv, do, name):
    wd = q.shape[1]

    def fn(q_, do_, kv_):
        k_, v_ = kv_[:, :wd].astype(F32), kv_[:, wd:].astype(F32)
        dq, dk, dv = [], [], []
        for s in _xa_slices(wd):
            _, vjp = jax.vjp(functools.partial(_attn_head, _bdot), q_[:, s].astype(F32), k_[:, s], v_[:, s])
            a, b, c = vjp(do_[:, s])
            dq.append(a), dk.append(b), dv.append(c)
        return jnp.concatenate(dq, axis=1), jnp.concatenate(dk + dv, axis=1)

    return _rowcall(name, fn, [q, do], [kv], [(wd, BF16)], [kv.shape])


def _loss_call(y, target, name):
    d = y.shape[1]

    def fn(y_, t_):
        err = y_ - t_
        part = 0.5 * jnp.sum(jnp.sum(err * err, axis=1, keepdims=True), axis=0, keepdims=True) / d
        return err / d, jnp.broadcast_to(part, (1, LANES))

    return _rowcall(name, fn, [y, target], [], [(d, F32)], [(1, LANES)])


def _adamw_call(w, g, m, v, name):
    def fn(w_, g_, m_, v_):
        m1 = ADAM_B1 * m_ + (1.0 - ADAM_B1) * g_
        v1 = ADAM_B2 * v_ + (1.0 - ADAM_B2) * jnp.square(g_)
        m_hat = m1 / (1.0 - ADAM_B1 ** ADAM_STEP)
        v_hat = v1 / (1.0 - ADAM_B2 ** ADAM_STEP)
        return -ADAM_LR * (m_hat / (jnp.sqrt(v_hat) + ADAM_EPS) + ADAM_WD * w_), m1, v1

    c = w.shape[1]
    return _rowcall(name, fn, [w, g, m, v], [], [(c, F32)] * 3, br=_divtile(w.shape[0], 128, 8))


STREAM_BLOCK_BYTES = 2 * 1024 * 1024


def _rows_for(cols, itemsize):
    return max(16, STREAM_BLOCK_BYTES // (cols * itemsize) // 16 * 16)


def _pair_add(grad, theirs, core, name):
    s, hr, cols = theirs.shape
    br = _divtile(hr, _rows_for(cols, 2), 16)
    nb = hr // br

    def body(c_ref, g_ref, t_ref, o_ref):
        o_ref[...] = (g_ref[...].astype(F32) + t_ref[...].astype(F32)).astype(o_ref.dtype)

    spec = pl.BlockSpec((None, br, cols), lambda j, i, c_ref: (j, i, 0))
    return _pcall(body, name=name, out_shape=jax.ShapeDtypeStruct(theirs.shape, BF16),
                  grid_spec=pltpu.PrefetchScalarGridSpec(
                      num_scalar_prefetch=1, grid=(s, nb),
                      in_specs=[pl.BlockSpec((None, br, cols), lambda j, i, c_ref: (j, c_ref[0] * nb + i, 0)), spec],
                      out_specs=spec),
                  compiler_params=_params(("parallel", "parallel")))(core, grad, theirs)


def _chip_sum(part, landed, kind, where, grad, layer, name):
    _, hr, cw = landed.shape
    br = _divtile(hr, _rows_for(cw, 2), 16)
    nb = hr // br

    def body(w_ref, p_ref, a_ref, b_ref, d_ref, g_ref, o_ref):
        o_ref[...] = ((p_ref[...].astype(F32) + a_ref[...].astype(F32)) + b_ref[...].astype(F32)) + d_ref[...].astype(F32)

    if kind == 'row':
        own = pl.BlockSpec((None, br, cw), lambda i, w_ref: (w_ref[0], i, 0))
    else:
        own = pl.BlockSpec((br, cw), lambda i, w_ref: (i, w_ref[0]))
    got = [pl.BlockSpec((None, br, cw), functools.partial(lambda i, w_ref, k: (k, i, 0), k=k)) for k in range(3)]
    return _pcall(body, name=name, out_shape=jax.ShapeDtypeStruct(grad.shape, F32),
                  grid_spec=pltpu.PrefetchScalarGridSpec(
                      num_scalar_prefetch=1, grid=(nb,), in_specs=[own] + got + [ANY_SPEC],
                      out_specs=pl.BlockSpec((None, br, cw), lambda i, w_ref: (layer, w_ref[1] * nb + i, 0))),
                  input_output_aliases={5: 0},
                  compiler_params=_params(("parallel",)))(where, part, landed, landed, landed, grad)


def _position():
    x, y, c = lax.axis_index("x"), lax.axis_index("y"), lax.axis_index("c")
    others = [(1 - x, y), (x, 1 - y), (1 - x, 1 - y)]
    return x, y, c, others


def _any_specs(n):
    return [pl.BlockSpec(memory_space=pl.ANY)] * n


def _ds(start, size):
    return pl.ds(pl.multiple_of(start, size), size)


HBM_SPEC = pl.BlockSpec(memory_space=pltpu.HBM)
SEM_SPEC = pl.BlockSpec(memory_space=pltpu.SEMAPHORE)
ANY_SPEC = pl.BlockSpec(memory_space=pl.ANY)
DATAFLOW = pltpu.SideEffectType.DATAFLOW_SIDE_EFFECTING


def _hbm(a):
    return pltpu.with_memory_space_constraint(a, pltpu.HBM)


def _full_shape(shard_shape, kind):
    r, cw = shard_shape
    return (N_CHIPS, r, cw) if kind == 'row' else (r, N_CHIPS * cw)


def _block_dims(full, kind):
    return (full.shape[1], full.shape[2]) if kind == 'row' else (full.shape[0], full.shape[1] // N_CHIPS)


def _region(full, kind, chip, half):
    r, cw = _block_dims(full, kind)
    if kind == 'row':
        return full.at[chip, _ds(half * (r // 2), r // 2), :]
    return full.at[_ds(half * (r // 2), r // 2), _ds(chip * cw, cw)]


def _place_block(full, shards, layer, kind, chip, name):
    _, r, cs = shards.shape
    cw = _block_dims(full, kind)[1]
    br = _divtile(r, 256, 16)

    def body(c_ref, s_ref, full_ref, o_ref):
        if cs == cw:
            o_ref[...] = s_ref[...].astype(o_ref.dtype)
        else:
            o_ref[:, :cs] = s_ref[...].astype(o_ref.dtype)
            o_ref[:, cs:] = jnp.zeros((br, cw - cs), o_ref.dtype)

    if kind == 'row':
        out_spec = pl.BlockSpec((None, br, cw), lambda i, c_ref: (c_ref[0], i, 0))
    else:
        out_spec = pl.BlockSpec((br, cw), lambda i, c_ref: (i, c_ref[0]))
    return _pcall(body, name=name, out_shape=jax.ShapeDtypeStruct(full.shape, full.dtype),
                  grid_spec=pltpu.PrefetchScalarGridSpec(
                      num_scalar_prefetch=1, grid=(r // br,),
                      in_specs=[pl.BlockSpec((None, br, cs), lambda i, c_ref: (layer, i, 0)), ANY_SPEC],
                      out_specs=out_spec),
                  input_output_aliases={2: 0}, compiler_params=_params(("parallel",)))(chip, shards, full)


def _split_copy_start(bufs, lands, plan, after, name):
    nb, nl = len(bufs), len(lands)
    extra = _as_list(after)

    def body(*refs):
        ins = refs[:nb + nl]
        send, recv = refs[nb + nl + len(extra)], refs[nb + nl + len(extra) + 1]
        for cp in plan(ins[:nb], ins[nb:], send, recv):
            cp.start()
        refs[-1][...] = jnp.zeros_like(refs[-1])

    n_copies = plan.n_copies
    out = _pcall(
        body, name=name,
        out_shape=(pltpu.SemaphoreType.DMA((n_copies,)), pltpu.SemaphoreType.DMA((n_copies,)),
                   *[pltpu.HBM(a.shape, a.dtype) for a in (*bufs, *lands)], jax.ShapeDtypeStruct((8, LANES), F32)),
        in_specs=[HBM_SPEC] * (nb + nl) + [ANY_SPEC] * len(extra),
        out_specs=(SEM_SPEC, SEM_SPEC, *[HBM_SPEC] * (nb + nl), pl.BlockSpec(memory_space=pltpu.VMEM)),
        input_output_aliases={i: 2 + i for i in range(nb + nl)},
        compiler_params=pltpu.CompilerParams(has_side_effects=DATAFLOW),
    )(*[_hbm(a) for a in (*bufs, *lands)], *extra)
    return out[0], out[1], list(out[2:2 + nb]), list(out[2 + nb:2 + nb + nl]), out[-1]


def _split_copy_wait(send, recv, bufs, lands, plan, after, name):
    nb, nl = len(bufs), len(lands)
    extra = _as_list(after)

    def body(*refs):
        ins = refs[:nb + nl]
        send_ref, recv_ref = refs[nb + nl], refs[nb + nl + 1]
        for cp in plan(ins[:nb], ins[nb:], send_ref, recv_ref):
            cp.wait_send()
            cp.wait_recv()

    out = _pcall(
        body, name=name, out_shape=tuple(pltpu.HBM(a.shape, a.dtype) for a in (*bufs, *lands)),
        in_specs=[HBM_SPEC] * (nb + nl) + [SEM_SPEC, SEM_SPEC] + [ANY_SPEC] * len(extra),
        out_specs=tuple([HBM_SPEC] * (nb + nl)), input_output_aliases={i: i for i in range(nb + nl)},
        compiler_params=pltpu.CompilerParams(has_side_effects=DATAFLOW),
    )(*bufs, *lands, send, recv, *extra)
    return list(out[:nb]), list(out[nb:])


def _gather_plan(kinds):
    def plan(bufs, lands, send, recv):
        x, y, c, others = _position()
        me = 2 * x + y
        return [pltpu.make_async_remote_copy(
            src_ref=_region(lands[w], kinds[w], me, c), dst_ref=_region(lands[w], kinds[w], me, c),
            send_sem=send.at[3 * w + k], recv_sem=recv.at[3 * w + k], device_id=(px, py, c), device_id_type=MESH_IDS)
            for w in range(len(lands)) for k, (px, py) in enumerate(others)]

    def wait_plan(bufs, lands, send, recv):
        x, y, c, others = _position()
        me = 2 * x + y
        return [pltpu.make_async_remote_copy(
            src_ref=_region(lands[w], kinds[w], me, c), dst_ref=_region(lands[w], kinds[w], 2 * px + py, c),
            send_sem=send.at[3 * w + k], recv_sem=recv.at[3 * w + k], device_id=(px, py, c), device_id_type=MESH_IDS)
            for w in range(len(lands)) for k, (px, py) in enumerate(others)]

    plan.n_copies = wait_plan.n_copies = 3 * len(kinds)
    return plan, wait_plan


def _gather_forward(fulls, kinds, name):
    n = len(fulls)

    def body(*refs):
        dst = refs[n:2 * n]
        send, recv = refs[2 * n:]
        x, y, c, others = _position()
        sib = (x, y, 1 - c)

        def copy(w, k, chip, half):
            reg = _region(dst[w], kinds[w], chip, half)
            return pltpu.make_async_remote_copy(src_ref=reg, dst_ref=reg, send_sem=send.at[3 * w + k],
                                                recv_sem=recv.at[3 * w + k], device_id=sib, device_id_type=MESH_IDS)

        give = [copy(w, k, 2 * px + py, c) for w in range(n) for k, (px, py) in enumerate(others)]
        for cp in give:
            cp.start()
        for w in range(n):
            for k, (px, py) in enumerate(others):
                copy(w, k, 2 * px + py, 1 - c).wait_recv()
        for cp in give:
            cp.wait_send()

    return _pcall(body, name=name, in_specs=_any_specs(n), out_specs=_any_specs(n),
                  out_shape=[jax.ShapeDtypeStruct(f.shape, f.dtype) for f in fulls],
                  input_output_aliases={i: i for i in range(n)},
                  scratch_shapes=[pltpu.SemaphoreType.DMA((3 * n,)), pltpu.SemaphoreType.DMA((3 * n,))])(*fulls)


def _pair_exchange(grads, kinds, name):
    n = len(grads)

    def body(*refs):
        src, theirs_ref = refs[:n], refs[n:2 * n]
        send, recv = refs[2 * n:]
        x, y, c, _ = _position()
        sib = (x, y, 1 - c)

        def half(w, h):
            if kinds[w] == 'row':
                hr = src[w].shape[1] // 2
                return src[w].at[:, _ds(h * hr, hr), :]
            hr = src[w].shape[0] // 2
            return src[w].at[_ds(h * hr, hr), :]

        give = [pltpu.make_async_remote_copy(src_ref=half(w, 1 - c), dst_ref=theirs_ref[w], send_sem=send.at[w],
                                             recv_sem=recv.at[w], device_id=sib, device_id_type=MESH_IDS)
                for w in range(n)]
        for cp in give:
            cp.start()
        for cp in give:
            cp.wait()

    def half_shape(g, kind):
        return (g.shape[0], g.shape[1] // 2, g.shape[2]) if kind == 'row' else (g.shape[0] // 2, g.shape[1])

    return _pcall(body, name=name, in_specs=_any_specs(n), out_specs=_any_specs(n),
                  out_shape=[jax.ShapeDtypeStruct(half_shape(g, kd), g.dtype) for g, kd in zip(grads, kinds)],
                  scratch_shapes=[pltpu.SemaphoreType.DMA((n,)), pltpu.SemaphoreType.DMA((n,))])(*grads)


def _part_dims(part, kind):
    return (part.shape[1], part.shape[2]) if kind == 'row' else (part.shape[0], part.shape[1] // N_CHIPS)


def _chip_plan(kinds):
    def plan(bufs, lands, send, recv):
        x, y, c, others = _position()

        def slot(w, chip):
            if kinds[w] == 'row':
                return bufs[w].at[chip]
            cw = _part_dims(bufs[w], kinds[w])[1]
            return bufs[w].at[:, _ds(chip * cw, cw)]

        return [pltpu.make_async_remote_copy(src_ref=slot(w, 2 * px + py), dst_ref=lands[w].at[k],
                                             send_sem=send.at[3 * w + k], recv_sem=recv.at[3 * w + k],
                                             device_id=(px, py, c), device_id_type=MESH_IDS)
                for w in range(len(bufs)) for k, (px, py) in enumerate(others)]

    plan.n_copies = 3 * len(kinds)
    return plan


def _pair_gather(grads, name):
    n = len(grads)
    n_layers = grads[0].shape[0]

    def body(*refs):
        dst = refs[n:2 * n]
        send, recv = refs[2 * n:]
        x, y, c, _ = _position()
        sib = (x, y, 1 - c)

        def copy(w, l, half):
            hr = dst[w].shape[1] // 2
            rows = dst[w].at[l, _ds(half * hr, hr), :]
            return pltpu.make_async_remote_copy(src_ref=rows, dst_ref=rows, send_sem=send.at[w * n_layers + l],
                                                recv_sem=recv.at[w * n_layers + l], device_id=sib,
                                                device_id_type=MESH_IDS)

        give = [copy(w, l, c) for w in range(n) for l in range(n_layers)]
        for cp in give:
            cp.start()
        for w in range(n):
            for l in range(n_layers):
                copy(w, l, 1 - c).wait_recv()
        for cp in give:
            cp.wait_send()

    m = n * n_layers
    return _pcall(body, name=name, in_specs=_any_specs(n), out_specs=_any_specs(n),
                  out_shape=[jax.ShapeDtypeStruct(g.shape, g.dtype) for g in grads],
                  input_output_aliases={i: i for i in range(n)},
                  scratch_shapes=[pltpu.SemaphoreType.DMA((m,)), pltpu.SemaphoreType.DMA((m,))])(*grads)


def _all_sum(buf, name):
    rows = buf.shape[0]
    flips = [(fx, fy, fc) for fx in (0, 1) for fy in (0, 1) for fc in (0, 1)][1:]

    def body(x_ref, o_ref, land, send, recv):
        x, y, c, _ = _position()
        me = 4 * x + 2 * y + c

        def peer(f):
            return (1 - x if f[0] else x, 1 - y if f[1] else y, 1 - c if f[2] else c)

        give = [pltpu.make_async_remote_copy(src_ref=x_ref, dst_ref=land.at[me], send_sem=send.at[k],
                                             recv_sem=recv.at[k], device_id=peer(f), device_id_type=MESH_IDS)
                for k, f in enumerate(flips)]
        for cp in give:
            cp.start()
        land[me] = x_ref[...]
        for k, f in enumerate(flips):
            px, py, pc = peer(f)
            slot = land.at[4 * px + 2 * py + pc]
            pltpu.make_async_remote_copy(src_ref=slot, dst_ref=slot, send_sem=send.at[k], recv_sem=recv.at[k],
                                         device_id=peer(f), device_id_type=MESH_IDS).wait_recv()
        for cp in give:
            cp.wait_send()
        total = land[0]
        for dev in range(1, N_DEV):
            total = total + land[dev]
        o_ref[...] = total

    return _pcall(body, name=name, in_specs=[pl.BlockSpec(memory_space=pltpu.VMEM)],
                  out_specs=pl.BlockSpec(memory_space=pltpu.VMEM), out_shape=jax.ShapeDtypeStruct(buf.shape, F32),
                  scratch_shapes=[pltpu.VMEM((N_DEV, rows, LANES), F32), pltpu.SemaphoreType.DMA((N_DEV - 1,)),
                                  pltpu.SemaphoreType.DMA((N_DEV - 1,))],
                  compiler_params=pltpu.CompilerParams(vmem_limit_bytes=VMEM_LIMIT))(buf)


def _pack(arrays):
    flat = jnp.concatenate([a.reshape(-1).astype(F32) for a in arrays])
    pad = (-flat.shape[0]) % (8 * LANES)
    return jnp.pad(flat, (0, pad)).reshape(-1, LANES)


def _unpack(buf, like):
    flat, out, off = buf.reshape(-1), [], 0
    for a in like:
        out.append(flat[off:off + a.size].reshape(a.shape))
        off += a.size
    return out


def _row2(v):
    return v.reshape(1, -1)


def _ffn_fwd(x, g_pre, w_gu, w_d, g_post, tag, after=None):
    h = _norm_fwd(x, g_pre, tag + "_pre", after)
    gu = _mm(h, w_gu, 'nn', F32, tag + "_gu")
    a = _swiglu_fwd(gu, tag + "_act")
    f = _mm(a, w_d, 'nn', F32, tag + "_down")
    return _resnorm_fwd(x, f, g_post, 0.5, tag + "_post"), (x, h, gu, a, f)


def _ffn_bwd(dx_out, saved, g_pre, w_gu, w_d, g_post, tag, after=None):
    x, h, gu, a, f = saved
    df, dg_post = _resnorm_bwd(f, dx_out, g_post, 0.5, tag + "_post_b", after)
    da = _mm(df, w_d, 'nt', F32, tag + "_down_bx")
    dw_d = _mm(a, df, 'tn', BF16, tag + "_down_bw")
    dgu = _swiglu_bwd(gu, da, tag + "_act_b")
    dh = _mm(dgu, w_gu, 'nt', F32, tag + "_gu_bx")
    dw_gu = _mm(h, dgu, 'tn', BF16, tag + "_gu_bw")
    dx, dg_pre = _norm_bwd(x, dh, dx_out, g_pre, tag + "_pre_b")
    return dx, dg_pre, dw_gu, dw_d, dg_post


def _split_cols(pp, cs, cp, widths):
    proj = jnp.concatenate([pp[:, j * cp:j * cp + cs] for j in range(N_CHIPS)], axis=1)
    out, off = [], 0
    for wd in widths:
        out.append(proj[:, off:off + wd])
        off += wd
    return out


def _join_cols(pieces, cs, cp):
    proj = jnp.concatenate(pieces, axis=1)
    t = proj.shape[0]
    cols = []
    for j in range(N_CHIPS):
        cols += [proj[:, j * cs:(j + 1) * cs], jnp.zeros((t, cp - cs), proj.dtype)]
    return jnp.concatenate(cols, axis=1)


def _mix_fwd(x, p, w_in, w_out, conv_w, cs, cp, tag, after=None):
    t, d = x.shape
    half = d // 2
    nh = half // HEAD_DIM
    h = _norm_fwd(x, p['mix_norm_pre'], tag + "_pre", after)
    pp = _mm(h, w_in, 'nn', F32, tag + "_in")
    u_a, v_a, qkv_raw, z, b_raw, a_raw = _split_cols(pp, cs, cp, [half, half, 3 * half, half, nh, nh])
    y_a = _sgu_fwd(u_a, v_a, p['sm_w'], p['sm_b'], p['sm_ln_g'], p['sm_ln_b'], tag + "_sgu")
    qkv = _conv_fwd(qkv_raw, conv_w, tag + "_conv")
    b_raw, a_raw = b_raw.T, a_raw.T
    beta, gcum = _gates_fwd(b_raw, a_raw, p['a_log'].T, p['dt_bias'].T, tag + "_gates")
    gcol, bcol = gcum[:, :, None], beta[:, :, None]
    grow = gcum.reshape(nh, t // DN_CHUNK, 1, DN_CHUNK)
    o, states = _delta_fwd(qkv, gcol, grow, bcol, tag + "_delta")
    y_b = _outgate_fwd(o, z, p['dn_norm_w'], tag + "_gate")
    y = jnp.concatenate([y_a, y_b], axis=1)
    m = _mm(y, w_out, 'nn', F32, tag + "_out")
    x_out = _resnorm_fwd(x, m, p['mix_norm_post'], 1.0, tag + "_post")
    return x_out, (x, h, u_a, v_a, qkv_raw, z, b_raw, a_raw, qkv, gcol, grow, bcol, states, o, y, m)


def _mix_bwd(dx_out, saved, p, w_in, w_out, conv_w, cs, cp, tag, after=None):
    x, h, u_a, v_a, qkv_raw, z, b_raw, a_raw, qkv, gcol, grow, bcol, states, o, y, m = saved
    t, d = x.shape
    half = d // 2
    nh = half // HEAD_DIM
    gr = {}
    dm, gr['mix_norm_post'] = _resnorm_bwd(m, dx_out, p['mix_norm_post'], 1.0, tag + "_post_b", after)
    dy = _mm(dm, w_out, 'nt', F32, tag + "_out_bx")
    gr['w_out'] = _mm(y, dm, 'tn', BF16, tag + "_out_bw")
    dy_a, dy_b = dy[:, :half], dy[:, half:]
    do, dz, gr['dn_norm_w'] = _outgate_bwd(o, z, p['dn_norm_w'], dy_b, tag + "_gate_b")
    dq, dk, dv, dgcol, dgrow, dbcol = _delta_bwd(qkv, gcol, grow, bcol, states, do, tag + "_delta_b")
    dgcum = dgcol[:, :, 0] + dgrow.reshape(nh, t)
    db_raw, da_raw, gr['a_log'], gr['dt_bias'] = _gates_bwd(b_raw, a_raw, p['a_log'].T, p['dt_bias'].T,
                                                             dbcol[:, :, 0], dgcum, tag + "_gates_b")
    db_raw, da_raw = db_raw.T, da_raw.T
    thirds = [_conv_bwd(qkv_raw, conv_w, dpart, i * half, tag + "_conv_b") for i, dpart in enumerate((dq, dk, dv))]
    gr['conv_w'] = jnp.concatenate([dw for _, dw in thirds], axis=1)
    du_a, dv_a, gr['sm_w'], gr['sm_b'], gr['sm_ln_g'], gr['sm_ln_b'] = _sgu_bwd(
        u_a, v_a, p['sm_w'], p['sm_b'], p['sm_ln_g'], p['sm_ln_b'], dy_a, tag + "_sgu_b")
    dpp = _join_cols([du_a, dv_a, *[dx for dx, _ in thirds], dz, db_raw, da_raw], cs, cp).astype(BF16)
    dh = _mm(dpp, w_in, 'nt', F32, tag + "_in_bx")
    gr['w_in'] = _mm(h, dpp, 'tn', BF16, tag + "_in_bw")
    dx, gr['mix_norm_pre'] = _norm_bwd(x, dh, dx_out, p['mix_norm_pre'], tag + "_pre_b")
    return dx, gr


def _xa_fwd(x, mem, p, w_xq, w_xkv, w_xo, tag, after=None):
    h = _norm_fwd(x, p['xa_norm_pre'], tag + "_pre", after)
    mh = _norm_fwd(mem, p['mem_norm'], tag + "_mem")
    q = _mm(h, w_xq, 'nn', BF16, tag + "_q")
    kv = _mm(mh, w_xkv, 'nn', BF16, tag + "_kv")
    o = _attn_fwd(q, kv, tag + "_attn")
    c = _mm(o, w_xo, 'nn', F32, tag + "_o")
    return _resnorm_fwd(x, c, p['xa_norm_post'], 1.0, tag + "_post"), (x, h, mh, q, kv, o, c)


def _xa_bwd(dx_out, saved, mem, p, w_xq, w_xkv, w_xo, tag, after=None):
    x, h, mh, q, kv, o, c = saved
    gr = {}
    dc, gr['xa_norm_post'] = _resnorm_bwd(c, dx_out, p['xa_norm_post'], 1.0, tag + "_post_b", after)
    do = _mm(dc, w_xo, 'nt', F32, tag + "_o_bx")
    gr['w_xo'] = _mm(o, dc, 'tn', BF16, tag + "_o_bw")
    dq, dkv = _attn_bwd(q, kv, do, tag + "_attn_b")
    dkv = dkv.astype(BF16)
    dh = _mm(dq, w_xq, 'nt', F32, tag + "_q_bx")
    gr['w_xq'] = _mm(h, dq, 'tn', BF16, tag + "_q_bw")
    dmh = _mm(dkv, w_xkv, 'nt', F32, tag + "_kv_bx")
    gr['w_xkv'] = _mm(mh, dkv, 'tn', BF16, tag + "_kv_bw")
    gr['mem_norm'] = _norm_bwd_gain(mem, dmh, p['mem_norm'], tag + "_mem_b")
    dx, gr['xa_norm_pre'] = _norm_bwd(x, dh, dx_out, p['xa_norm_pre'], tag + "_pre_b")
    return dx, gr


def _step(inputs):
    x, mem, target = inputs['x'][0], inputs['mem'][0], inputs['loss_target'][0]
    n_layers = inputs['w_in'].shape[0]
    cx, cy, cc = lax.axis_index("x"), lax.axis_index("y"), lax.axis_index("c")
    chip = 2 * cx + cy
    cs = inputs['w_in'].shape[2]
    cp = -(-cs // LANES) * LANES
    kinds = [BIG[nm] for nm in BIG_NAMES]

    conv_local = inputs['conv_w']
    ccols = conv_local.shape[2]

    core_idx, chip_idx = cc.astype(jnp.int32).reshape(1), chip.astype(jnp.int32).reshape(1)
    where_idx = jnp.stack([chip, cc]).astype(jnp.int32)
    order = [(l, gi) for l in range(n_layers) for gi in range(len(GROUPS))]
    n_groups = len(order)

    def small(l):
        p = {nm: inputs[nm][l] for nm in SMALL_NAMES}
        for nm in SMALL_NAMES:
            if p[nm].ndim == 1:
                p[nm] = _row2(p[nm])
        p['sm_b'] = p['sm_b'][:, :, None]
        return p

    def place(k):
        l, gi = order[k]
        lands = []
        for nm in GROUPS[gi][1]:
            _, r, c_in = inputs[nm].shape
            shape = _full_shape((r, cp if nm == 'w_in' else c_in), BIG[nm])
            lands.append(_place_block(lax.empty(shape, BF16), inputs[nm], l, BIG[nm], chip_idx, "place_block"))
        return lands

    placed = {}

    def gather_start(k, after):
        names = GROUPS[order[k][1]][1]
        kds = [BIG[nm] for nm in names]
        plan, wait_plan = _gather_plan(kds)
        if k not in placed:
            placed[k] = place(k)
        send, recv, _, lands, token = _split_copy_start([], placed[k], plan, after, f"gather_start_{k}")
        return dict(send=send, recv=recv, lands=lands, token=token, kinds=kds, names=names, wait_plan=wait_plan)

    def gather_finish(k, st, after):
        _, lands = _split_copy_wait(st['send'], st['recv'], [], st['lands'], st['wait_plan'], after, f"gather_wait_{k}")
        lands = _gather_forward(lands, st['kinds'], "gather_forward_" + GROUPS[order[k][1]][0])
        return {nm: (g.reshape(-1, g.shape[2]) if kd == 'row' else g)
                for nm, g, kd in zip(st['names'], lands, st['kinds'])}

    def group_fwd(gi, x_, p, w, l, token):
        if gi == 0:
            return _ffn_fwd(x_, p['ffn1_norm_pre'], w['ffn1_w_gate_up'], w['ffn1_w_down'], p['ffn1_norm_post'], "ffn", token)
        if gi == 1:
            return _mix_fwd(x_, p, w['w_in'], w['w_out'], conv_full[l], cs, cp, "mix", token)
        if gi == 2:
            return _xa_fwd(x_, mem, p, w['w_xq'], w['w_xkv'], w['w_xo'], "xa", token)
        return _ffn_fwd(x_, p['ffn2_norm_pre'], w['ffn2_w_gate_up'], w['ffn2_w_down'], p['ffn2_norm_post'], "ffn", token)

    def group_bwd(gi, dx_, s, p, w, l, token):
        if gi in (0, 3):
            pre = 'ffn1' if gi == 0 else 'ffn2'
            dx_, g_pre, g_gu, g_d, g_post = _ffn_bwd(dx_, s, p[pre + '_norm_pre'], w[pre + '_w_gate_up'],
                                                     w[pre + '_w_down'], p[pre + '_norm_post'], "ffn", token)
            return dx_, {pre + '_norm_pre': g_pre, pre + '_w_gate_up': g_gu, pre + '_w_down': g_d,
                         pre + '_norm_post': g_post}
        if gi == 1:
            return _mix_bwd(dx_, s, p, w['w_in'], w['w_out'], conv_full[l], cs, cp, "mix", token)
        return _xa_bwd(dx_, s, mem, p, w['w_xq'], w['w_xkv'], w['w_xo'], "xa", token)

    started = {0: gather_start(0, None)}
    started[1] = gather_start(1, started[0]['token'])
    weights, saved = [], []
    conv_placed = lax.dynamic_update_slice(jnp.zeros((n_layers, CONV_WIDTH, N_CHIPS * ccols), F32),
                                           jnp.where(cc == 0, conv_local, 0.0) + started[1]['token'][0, 0],
                                           (0, 0, chip * ccols))
    conv_full = _unpack(_all_sum(_pack([conv_placed]), "conv_gather"), [conv_placed])[0]
    for k in range(2, n_groups):
        placed[k] = place(k)
    head_work = [started[1]['token'], conv_full] + [land for k in range(2, n_groups) for land in placed[k]]
    for k, (l, gi) in enumerate(order):
        w = gather_finish(k, started[k], x if k > 0 else head_work)
        token = None
        if k + 2 < n_groups:
            started[k + 2] = gather_start(k + 2, next(iter(w.values())))
            token = started[k + 2]['token']
        weights.append(w)
        x, s = group_fwd(gi, x, small(l), w, l, token)
        saved.append(s)

    dx, loss_part = _loss_call(x, target, "loss")

    grads = [dict() for _ in range(n_layers)]
    big_grads = {nm: None for nm in BIG_NAMES}

    def reduce_finish(pd, after):
        parts, lands = _split_copy_wait(pd['send'], pd['recv'], pd['parts'], pd['lands'], pd['plan'], after,
                                        f"chip_wait_{pd['k']}")
        for nm, kd, part, landed in zip(pd['names'], pd['kinds'], parts, lands):
            if big_grads[nm] is None:
                big_grads[nm] = lax.empty((n_layers, 2 * landed.shape[1], landed.shape[2]), F32)
            big_grads[nm] = _chip_sum(part, landed, kd, where_idx, big_grads[nm], pd['l'], "chip_sum")

    pending = None
    for k in reversed(range(n_groups)):
        l, gi = order[k]
        dx, gr = group_bwd(gi, dx, saved[k], small(l), weights[k], l, pending['token'] if pending else None)
        if pending:
            reduce_finish(pending, dx)
        names = GROUPS[gi][1]
        kds = [BIG[nm] for nm in names]
        parts = []
        gs = [gr[nm].reshape(N_CHIPS, -1, gr[nm].shape[1]) if kd == 'row' else gr[nm] for nm, kd in zip(names, kds)]
        theirs = _pair_exchange(gs, kds, "pair_exchange_" + GROUPS[gi][0])
        for g, t, kd in zip(gs, theirs, kds):
            parts.append(_pair_add(g, t, core_idx, "pair_add") if kd == 'row'
                         else _pair_add(g[None], t[None], core_idx, "pair_add")[0])
        lands = [lax.empty((3, *_part_dims(part, kd)), BF16) for part, kd in zip(parts, kds)]
        plan = _chip_plan(kds)
        send, recv, parts, lands, token = _split_copy_start(parts, lands, plan, None, f"chip_start_{k}")
        pending = dict(send=send, recv=recv, parts=parts, lands=lands, token=token, plan=plan, names=names, kinds=kds,
                       k=k, l=l)
        grads[l].update({nm: g for nm, g in gr.items() if nm not in BIG})
    out = {}

    def two(a):
        return a.reshape(-1, a.shape[-1])

    def adamw(nm, g):
        delta, m1, v1 = _adamw_call(two(inputs[nm]), two(g), two(inputs['m_' + nm]), two(inputs['v_' + nm]), "adamw")
        out[nm] = (g, delta.reshape(g.shape), m1.reshape(g.shape), v1.reshape(g.shape))
        return delta

    def finish_weights(gi):
        names = GROUPS[gi][1]
        got = _pair_gather([big_grads[nm] for nm in names], "pair_gather_" + GROUPS[gi][0])
        return [adamw(nm, g[:, :, :cs] if nm == 'w_in' else g) for nm, g in zip(names, got)]

    done = [d for gi in range(len(GROUPS) - 1, 0, -1) for d in finish_weights(gi)]

    small_grads = [jnp.stack([grads[l][nm].reshape(inputs[nm].shape[1:]) if nm != 'conv_w' else grads[l][nm]
                              for l in range(n_layers)]) for nm in SMALL_NAMES]
    summed = _unpack(_all_sum(_pack(small_grads + [loss_part]), "small_sum"), small_grads + [loss_part])
    loss = summed[-1][0, 0]
    small_g = dict(zip(SMALL_NAMES, summed[:-1]))
    small_g['conv_w'] = lax.dynamic_slice(small_g['conv_w'], (0, 0, chip * ccols), (n_layers, CONV_WIDTH, ccols))
    done += [adamw(nm, small_g[nm]) for nm in SMALL_NAMES]

    reduce_finish(pending, done)
    finish_weights(0)
    return (loss, dx[None], *[out[nm][0] for nm in WEIGHTS], *[out[nm][1] for nm in WEIGHTS],
            *[out[nm][2] for nm in WEIGHTS], *[out[nm][3] for nm in WEIGHTS])


def kernel(x, mem, ffn1_norm_pre, ffn1_w_gate_up, ffn1_w_down, ffn1_norm_post, mix_norm_pre, w_in, conv_w, a_log, dt_bias, sm_w, sm_b, sm_ln_g, sm_ln_b, dn_norm_w, w_out, mix_norm_post, xa_norm_pre, mem_norm, w_xq, w_xkv, w_xo, xa_norm_post, ffn2_norm_pre, ffn2_w_gate_up, ffn2_w_down, ffn2_norm_post, loss_target, m_ffn1_norm_pre, m_ffn1_w_gate_up, m_ffn1_w_down, m_ffn1_norm_post, m_mix_norm_pre, m_w_in, m_conv_w, m_a_log, m_dt_bias, m_sm_w, m_sm_b, m_sm_ln_g, m_sm_ln_b, m_dn_norm_w, m_w_out, m_mix_norm_post, m_xa_norm_pre, m_mem_norm, m_w_xq, m_w_xkv, m_w_xo, m_xa_norm_post, m_ffn2_norm_pre, m_ffn2_w_gate_up, m_ffn2_w_down, m_ffn2_norm_post, v_ffn1_norm_pre, v_ffn1_w_gate_up, v_ffn1_w_down, v_ffn1_norm_post, v_mix_norm_pre, v_w_in, v_conv_w, v_a_log, v_dt_bias, v_sm_w, v_sm_b, v_sm_ln_g, v_sm_ln_b, v_dn_norm_w, v_w_out, v_mix_norm_post, v_xa_norm_pre, v_mem_norm, v_w_xq, v_w_xkv, v_w_xo, v_xa_norm_post, v_ffn2_norm_pre, v_ffn2_w_gate_up, v_ffn2_w_down, v_ffn2_norm_post):
    vals = dict(locals())
    return _step(vals)
```

```python
import functools
import math

import jax
import jax.numpy as jnp
from jax import lax
from jax.experimental import pallas as pl
from jax.experimental.pallas import tpu as pltpu

F32, BF16 = jnp.float32, jnp.bfloat16
NORM_EPS = 1e-6
HEAD_DIM = 128
GM_CHUNK = 128
DN_CHUNK = 64
CONV_WIDTH = 4
XA_HEADS = 4
LANES = 128
N_CHIPS = 4
N_DEV = 8
VMEM_LIMIT = 56 * 1024 * 1024
MESH_IDS = pl.DeviceIdType.MESH
ADAM_LR, ADAM_B1, ADAM_B2, ADAM_EPS, ADAM_WD, ADAM_STEP = 0.001, 0.9, 0.999, 1e-08, 0.01, 10

WEIGHTS = ['ffn1_norm_pre', 'ffn1_w_gate_up', 'ffn1_w_down', 'ffn1_norm_post', 'mix_norm_pre', 'w_in', 'conv_w',
           'a_log', 'dt_bias', 'sm_w', 'sm_b', 'sm_ln_g', 'sm_ln_b', 'dn_norm_w', 'w_out', 'mix_norm_post',
           'xa_norm_pre', 'mem_norm', 'w_xq', 'w_xkv', 'w_xo', 'xa_norm_post', 'ffn2_norm_pre', 'ffn2_w_gate_up',
           'ffn2_w_down', 'ffn2_norm_post']
BIG = {'ffn1_w_gate_up': 'col', 'ffn1_w_down': 'row', 'w_in': 'col', 'w_out': 'row', 'w_xq': 'row',
       'w_xkv': 'col', 'w_xo': 'row', 'ffn2_w_gate_up': 'col', 'ffn2_w_down': 'row'}
BIG_NAMES = list(BIG)
GROUPS = (('ffn1', ('ffn1_w_gate_up', 'ffn1_w_down')), ('mix', ('w_in', 'w_out')),
          ('xa', ('w_xq', 'w_xkv', 'w_xo')), ('ffn2', ('ffn2_w_gate_up', 'ffn2_w_down')))
SMALL_NAMES = [n for n in WEIGHTS if n not in BIG]


def _pcall(body, **kw):
    return pl.pallas_call(body, **kw)


def _params(sem=None):
    return pltpu.CompilerParams(dimension_semantics=sem, vmem_limit_bytes=VMEM_LIMIT)


def _as_list(after):
    if after is None:
        return []
    return list(after) if isinstance(after, (list, tuple)) else [after]


def _divtile(dim, cap, align=LANES):
    if dim <= cap:
        return dim
    t = (cap // align) * align
    while t > align and dim % t:
        t -= align
    assert dim % t == 0, (dim, cap)
    return t


_DN = {'nn': (((1,), (0,)), ((), ())), 'nt': (((1,), (1,)), ((), ())), 'tn': (((0,), (0,)), ((), ()))}


def _mm(a, b, mode, out_dtype, name):
    if mode == 'nn':
        (m, k), (k2, n) = a.shape, b.shape
    elif mode == 'nt':
        (m, k), (n, k2) = a.shape, b.shape
    else:
        (k, m), (k2, n) = a.shape, b.shape
    assert k == k2, (a.shape, b.shape, mode)
    tm, tn, tk = _divtile(m, 1024), _divtile(n, 1024), _divtile(k, 2048)
    nk = k // tk
    a_spec = {'nn': pl.BlockSpec((tm, tk), lambda i, j, kk: (i, kk)),
              'nt': pl.BlockSpec((tm, tk), lambda i, j, kk: (i, kk)),
              'tn': pl.BlockSpec((tk, tm), lambda i, j, kk: (kk, i))}[mode]
    b_spec = {'nn': pl.BlockSpec((tk, tn), lambda i, j, kk: (kk, j)),
              'nt': pl.BlockSpec((tn, tk), lambda i, j, kk: (j, kk)),
              'tn': pl.BlockSpec((tk, tn), lambda i, j, kk: (kk, j))}[mode]

    def dot(a_ref, b_ref):
        return lax.dot_general(a_ref[...].astype(BF16), b_ref[...].astype(BF16), _DN[mode], preferred_element_type=F32)

    def body_once(a_ref, b_ref, o_ref):
        o_ref[...] = dot(a_ref, b_ref).astype(o_ref.dtype)

    def body_steps(a_ref, b_ref, o_ref, acc):
        kk = pl.program_id(2)

        @pl.when(kk == 0)
        def _():
            acc[...] = jnp.zeros_like(acc)

        acc[...] += dot(a_ref, b_ref)

        @pl.when(kk == nk - 1)
        def _():
            o_ref[...] = acc[...].astype(o_ref.dtype)

    return _pcall(
        body_once if nk == 1 else body_steps, name=name, grid=(m // tm, n // tn, nk),
        in_specs=[a_spec, b_spec], out_specs=pl.BlockSpec((tm, tn), lambda i, j, kk: (i, j)),
        out_shape=jax.ShapeDtypeStruct((m, n), out_dtype),
        scratch_shapes=[] if nk == 1 else [pltpu.VMEM((tm, tn), F32)],
        compiler_params=_params(("parallel", "parallel", "arbitrary")),
    )(a, b)


def _rowcall(name, f, rows, params=(), row_outs=(), acc_outs=(), br=256, nrows=None, after=None):
    rows = [r if isinstance(r, tuple) else (r, 0) for r in rows]
    t = nrows if nrows is not None else rows[0][0].shape[0]
    br = min(br, t)
    assert t % br == 0, (name, t, br)
    n_in, n_ro = len(rows) + len(params), len(row_outs)
    extra = _as_list(after)

    def row_spec(arr, off):
        assert off % br == 0
        return pl.BlockSpec((br, arr.shape[1]), functools.partial(lambda i, o: (i + o, 0), o=off // br))

    def whole_spec(shape):
        return pl.BlockSpec(shape, functools.partial(lambda i, nd: (0,) * nd, nd=len(shape)))

    def body(*refs):
        res = f(*[r[...] for r in refs[:n_in]])
        outs = refs[n_in + len(extra):]
        for o_ref, val in zip(outs[:n_ro], res[:n_ro]):
            o_ref[...] = val.astype(o_ref.dtype)
        if acc_outs:
            @pl.when(pl.program_id(0) == 0)
            def _():
                for o_ref in outs[n_ro:]:
                    o_ref[...] = jnp.zeros_like(o_ref)

            for o_ref, val in zip(outs[n_ro:], res[n_ro:]):
                o_ref[...] += val.astype(F32)

    out = _pcall(
        body, name=name, grid=(t // br,),
        in_specs=[row_spec(a, o) for a, o in rows] + [whole_spec(p.shape) for p in params]
        + [pl.BlockSpec(memory_space=pl.ANY)] * len(extra),
        out_specs=[pl.BlockSpec((br, c), lambda i: (i, 0)) for c, _ in row_outs] + [whole_spec(s) for s in acc_outs],
        out_shape=[jax.ShapeDtypeStruct((t, c), dt) for c, dt in row_outs]
        + [jax.ShapeDtypeStruct(s, F32) for s in acc_outs],
        compiler_params=_params(("arbitrary",) if acc_outs else ("parallel",)),
    )(*[a for a, _ in rows], *params, *extra)
    return out


_DN_BATCH = {'nn': (((2,), (1,)), ((0,), (0,))), 'nt': (((2,), (2,)), ((0,), (0,))), 'tn': (((1,), (1,)), ((0,), (0,)))}


def _dims(a, dn):
    return _DN_BATCH[dn] if a.ndim == 3 else _DN[dn]


def _bdot_raw(a, b, dn):
    return lax.dot_general(a.astype(BF16), b.astype(BF16), _dims(a, dn), preferred_element_type=F32)


def _hdot_raw(a, b, dn):
    a_hi, b_hi = a.astype(BF16), b.astype(BF16)
    a_lo, b_lo = (a - a_hi.astype(F32)).astype(BF16), (b - b_hi.astype(F32)).astype(BF16)

    def dot(p, q):
        return lax.dot_general(p, q, _dims(a, dn), preferred_element_type=F32)

    return dot(a_hi, b_hi) + (dot(a_hi, b_lo) + dot(a_lo, b_hi))


def _with_vjp(raw):
    @functools.partial(jax.custom_vjp, nondiff_argnums=(2,))
    def dot(a, b, dn):
        return raw(a, b, dn)

    def fwd(a, b, dn):
        return raw(a, b, dn), (a, b)

    def bwd(dn, res, ct):
        a, b = res
        if dn == 'nn':
            da, db = raw(ct, b, 'nt'), raw(a, ct, 'tn')
        elif dn == 'nt':
            da, db = raw(ct, b, 'nn'), raw(ct, a, 'tn')
        else:
            da, db = raw(b, ct, 'nt'), raw(a, ct, 'nn')
        return da.astype(a.dtype), db.astype(b.dtype)

    dot.defvjp(fwd, bwd)
    return dot


_bdot, _hdot = _with_vjp(_bdot_raw), _with_vjp(_hdot_raw)


def _rms(x, g):
    return x * lax.rsqrt(jnp.mean(x * x, axis=-1, keepdims=True) + NORM_EPS) * g


def _sigmoid(x):
    return 1.0 / (1.0 + jnp.exp(-x))


def _silu(x):
    return x * _sigmoid(x)


def _gelu(x):
    return 0.5 * x * (1.0 + lax.erf(x * (2.0 ** -0.5)))


def _norm_fwd(x, g, name, after=None):
    return _rowcall(name, lambda x_, g_: (_rms(x_, g_),), [x], [g], [(x.shape[1], BF16)], after=after)[0]


def _resnorm_fwd(x, f, g, alpha, name):
    return _rowcall(name, lambda x_, f_, g_: (x_ + alpha * _rms(f_, g_),), [x, f], [g], [(x.shape[1], F32)])[0]


def _resnorm_bwd(f, dx, g, alpha, name, after=None):
    def fn(f_, dx_, g_):
        _, vjp = jax.vjp(lambda a, b: alpha * _rms(a, b), f_, g_)
        return vjp(dx_)

    return _rowcall(name, fn, [f, dx], [g], [(f.shape[1], BF16)], [g.shape], after=after)


def _norm_bwd(x, dh, dx_out, g, name, after=None):
    def fn(x_, dh_, dxo_, g_):
        _, vjp = jax.vjp(_rms, x_, g_)
        dx, dg = vjp(dh_)
        return dxo_ + dx, dg

    return _rowcall(name, fn, [x, dh, dx_out], [g], [(x.shape[1], F32)], [g.shape], after=after)


def _norm_bwd_gain(x, dh, g, name):
    def fn(x_, dh_, g_):
        _, vjp = jax.vjp(lambda b: _rms(x_, b), g_)
        return vjp(dh_)

    return _rowcall(name, fn, [x, dh], [g], [], [g.shape])[0]


def _swiglu_fwd(gu, name):
    ff = gu.shape[1] // 2
    return _rowcall(name, lambda gu_: (_silu(gu_[:, :ff]) * gu_[:, ff:],), [gu], [], [(ff, BF16)], br=64)[0]


def _swiglu_bwd(gu, da, name):
    ff = gu.shape[1] // 2

    def fn(gu_, da_):
        gate, up = gu_[:, :ff], gu_[:, ff:]
        s = _sigmoid(gate)
        dgate = da_ * up * (s * (1.0 + gate * (1.0 - s)))
        dup = da_ * (gate * s)
        return (jnp.concatenate([dgate, dup], axis=1),)

    return _rowcall(name, fn, [gu, da], [], [(2 * ff, BF16)], br=64)[0]


def _sgu_norm(v, lg, lb):
    vf = _gelu(v)
    mu = jnp.mean(vf, axis=-1, keepdims=True)
    var = jnp.mean(jnp.square(vf - mu), axis=-1, keepdims=True)
    return (vf - mu) * lax.rsqrt(var + NORM_EPS) * lg + lb


def _sgu_head(dot, u_h, vn_h, w_h, b_h):
    r = lax.broadcasted_iota(jnp.int32, w_h.shape, 0)
    c = lax.broadcasted_iota(jnp.int32, w_h.shape, 1)
    mixed = dot(jnp.where(r >= c, w_h, 0.0), vn_h, 'nn') + b_h
    return _gelu(u_h) * mixed


def _heads(width):
    return [slice(h * HEAD_DIM, (h + 1) * HEAD_DIM) for h in range(width // HEAD_DIM)]


def _sgu_fwd(u, v, w, bcol, lg, lb, name):
    def fn(u_, v_, w_, b_, lg_, lb_):
        vn = _sgu_norm(v_, lg_, lb_)
        return (jnp.concatenate([_sgu_head(_bdot_raw, u_[:, s], vn[:, s], w_[h], b_[h])
                                 for h, s in enumerate(_heads(u_.shape[1]))], axis=1),)

    return _rowcall(name, fn, [u, v], [w, bcol, lg, lb], [(u.shape[1], BF16)], br=GM_CHUNK)[0]


def _sgu_bwd(u, v, w, bcol, lg, lb, dy, name):
    def fn(u_, v_, dy_, w_, b_, lg_, lb_):
        vn, vjp_norm = jax.vjp(_sgu_norm, v_, lg_, lb_)
        du, dvn, dw, db = [], [], [], []
        for h, s in enumerate(_heads(u_.shape[1])):
            _, vjp = jax.vjp(functools.partial(_sgu_head, _bdot), u_[:, s], vn[:, s], w_[h], b_[h])
            a, b, c, d = vjp(dy_[:, s])
            du.append(a), dvn.append(b), dw.append(c[None]), db.append(d[None])
        dv, dlg, dlb = vjp_norm(jnp.concatenate(dvn, axis=1))
        return (jnp.concatenate(du, axis=1), dv, jnp.concatenate(dw, axis=0), jnp.concatenate(db, axis=0), dlg, dlb)

    wd = u.shape[1]
    return _rowcall(name, fn, [u, v, dy], [w, bcol, lg, lb], [(wd, F32), (wd, F32)],
                    [w.shape, bcol.shape, lg.shape, lb.shape], br=GM_CHUNK)


def _shift_down(x, s):
    if s == 0:
        return x
    rows = lax.broadcasted_iota(jnp.int32, x.shape, 0)
    return jnp.where(rows >= s, pltpu.roll(x, s, 0), 0.0)


def _shift_up(x, s):
    if s == 0:
        return x
    t = x.shape[0]
    rows = lax.broadcasted_iota(jnp.int32, x.shape, 0)
    return jnp.where(rows < t - s, pltpu.roll(x, t - s, 0), 0.0)


def _conv_pre(x, taps):
    acc = None
    for i in range(CONV_WIDTH):
        term = _shift_down(x, CONV_WIDTH - 1 - i) * taps[i]
        acc = term if acc is None else acc + term
    return acc


def _taps(w_ref):
    return [w_ref[i:i + 1, :] for i in range(CONV_WIDTH)]


def _conv_fwd(x, w, name):
    t, ch = x.shape
    cb = _divtile(ch, 512)

    def body(x_ref, w_ref, o_ref):
        o_ref[...] = _silu(_conv_pre(x_ref[...], _taps(w_ref)))

    return _pcall(body, name=name, grid=(ch // cb,),
                  in_specs=[pl.BlockSpec((t, cb), lambda j: (0, j)), pl.BlockSpec((CONV_WIDTH, cb), lambda j: (0, j))],
                  out_specs=pl.BlockSpec((t, cb), lambda j: (0, j)),
                  out_shape=jax.ShapeDtypeStruct((t, ch), F32), compiler_params=_params(("parallel",)))(x, w)


def _conv_bwd(x, w, dout, col_off, name):
    t, ch = dout.shape
    cb = _divtile(ch, 256)
    assert col_off % cb == 0
    off = col_off // cb

    def body(x_ref, w_ref, do_ref, dx_ref, dw_ref):
        xv, taps = x_ref[...], _taps(w_ref)
        pre = _conv_pre(xv, taps)
        s = _sigmoid(pre)
        dpre = do_ref[...] * (s * (1.0 + pre * (1.0 - s)))
        dx = None
        for i in range(CONV_WIDTH):
            sh = CONV_WIDTH - 1 - i
            term = _shift_up(dpre, sh) * taps[i]
            dx = term if dx is None else dx + term
            dw_ref[i:i + 1, :] = jnp.sum(dpre * _shift_down(xv, sh), axis=0, keepdims=True)
        dx_ref[...] = dx

    return _pcall(body, name=name, grid=(ch // cb,),
                  in_specs=[pl.BlockSpec((t, cb), lambda j: (0, j + off)),
                            pl.BlockSpec((CONV_WIDTH, cb), lambda j: (0, j + off)),
                            pl.BlockSpec((t, cb), lambda j: (0, j))],
                  out_specs=[pl.BlockSpec((t, cb), lambda j: (0, j)), pl.BlockSpec((CONV_WIDTH, cb), lambda j: (0, j))],
                  out_shape=[jax.ShapeDtypeStruct((t, ch), F32), jax.ShapeDtypeStruct((CONV_WIDTH, ch), F32)],
                  compiler_params=_params(("parallel",)))(x, w, dout)


GATE_BLOCK = 256


def _gates(dot, b_raw, a_raw, a_log, dt_bias):
    blk = b_raw.shape[1]
    z = a_raw + dt_bias
    softplus = jnp.maximum(z, 0.0) + jnp.log(1.0 + jnp.exp(-jnp.abs(z)))
    g = -jnp.exp(a_log) * softplus
    ri = lax.broadcasted_iota(jnp.int32, (blk, blk), 0)
    ci = lax.broadcasted_iota(jnp.int32, (blk, blk), 1)
    upto = ((ri <= ci) & (ri // DN_CHUNK == ci // DN_CHUNK)).astype(F32)
    return _sigmoid(b_raw), dot(g, upto, 'nn')


def _gate_blocks(t):
    blk = min(GATE_BLOCK, t)
    return [slice(i, i + blk) for i in range(0, t, blk)]


def _whole_call(name, f, ins, out_shapes):
    n_in = len(ins)

    def body(*refs):
        for o_ref, val in zip(refs[n_in:], f(*[r[...] for r in refs[:n_in]])):
            o_ref[...] = val

    vmem = pl.BlockSpec(memory_space=pltpu.VMEM)
    return _pcall(body, name=name, in_specs=[vmem] * n_in, out_specs=[vmem] * len(out_shapes),
                  out_shape=[jax.ShapeDtypeStruct(s, F32) for s in out_shapes],
                  compiler_params=pltpu.CompilerParams(vmem_limit_bytes=VMEM_LIMIT))(*ins)


def _gates_fwd(b_raw, a_raw, a_log, dt_bias, name):
    def fn(b_, a_, al_, dt_):
        parts = [_gates(_hdot_raw, b_[:, s], a_[:, s], al_, dt_) for s in _gate_blocks(b_.shape[1])]
        return [jnp.concatenate(p, axis=1) for p in zip(*parts)]

    return _whole_call(name, fn, [b_raw, a_raw, a_log, dt_bias], [b_raw.shape, b_raw.shape])


def _gates_bwd(b_raw, a_raw, a_log, dt_bias, dbeta, dgcum, name):
    def fn(b_, a_, al_, dt_, dbeta_, dgcum_):
        db, da, dal, ddt = [], [], None, None
        for s in _gate_blocks(b_.shape[1]):
            _, vjp = jax.vjp(functools.partial(_gates, _hdot), b_[:, s], a_[:, s], al_, dt_)
            p, q, r, u = vjp((dbeta_[:, s], dgcum_[:, s]))
            db.append(p), da.append(q)
            dal, ddt = (r, u) if dal is None else (dal + r, ddt + u)
        return jnp.concatenate(db, axis=1), jnp.concatenate(da, axis=1), dal, ddt

    return _whole_call(name, fn, [b_raw, a_raw, a_log, dt_bias, dbeta, dgcum],
                       [b_raw.shape, a_raw.shape, a_log.shape, dt_bias.shape])


def _unit_lower_inverse_raw(a):
    c = a.shape[-1]
    eye = (lax.broadcasted_iota(jnp.int32, (c, c), 0) == lax.broadcasted_iota(jnp.int32, (c, c), 1)).astype(F32)
    inv, p = eye - a, _hdot_raw(a, a, 'nn')
    n_fac = int(math.log2(c)) - 1
    for it in range(n_fac):
        inv = inv + _hdot_raw(inv, p, 'nn')
        if it < n_fac - 1:
            p = _hdot_raw(p, p, 'nn')
    return inv


@jax.custom_vjp
def _unit_lower_inverse(a):
    return _unit_lower_inverse_raw(a)


def _unit_lower_inverse_fwd(a):
    inv = _unit_lower_inverse_raw(a)
    return inv, inv


def _unit_lower_inverse_bwd(inv, ct):
    return (-_hdot_raw(_hdot_raw(inv, ct, 'tn'), inv, 'nt'),)


_unit_lower_inverse.defvjp(_unit_lower_inverse_fwd, _unit_lower_inverse_bwd)


def _halves_raw(x):
    return x[..., :x.shape[-1] // 2], x[..., x.shape[-1] // 2:]


@jax.custom_vjp
def _halves(x):
    return _halves_raw(x)


_halves.defvjp(lambda x: (_halves_raw(x), None), lambda _, ct: (jnp.concatenate(ct, axis=-1),))

_DELTA_OPS = (_bdot_raw, _hdot_raw, _unit_lower_inverse_raw, _halves_raw)
_DELTA_OPS_VJP = (_bdot, _hdot, _unit_lower_inverse, _halves)


def _delta_chunk(ops, state, q, k, v, gc, gr, bc):
    bd, hd, inverse, halves = ops
    c, d = q.shape[-2:]
    ri = lax.broadcasted_iota(jnp.int32, (c, c), 0)
    ci = lax.broadcasted_iota(jnp.int32, (c, c), 1)
    causal, strict = ri >= ci, ri > ci
    qn = q * lax.rsqrt(jnp.sum(q * q, axis=-1, keepdims=True) + NORM_EPS) * (d ** -0.5)
    kn = k * lax.rsqrt(jnp.sum(k * k, axis=-1, keepdims=True) + NORM_EPS)
    gcum = jnp.broadcast_to(gc, q.shape)
    decay = jnp.where(causal, jnp.exp(jnp.where(causal, gc - gr, 0.0)), 0.0)
    bb = jnp.broadcast_to(bc, q.shape)
    k_beta = kn * bb
    inv = inverse(jnp.where(strict, bd(k_beta, kn, 'nt') * decay, 0.0))
    uw = hd(inv, jnp.concatenate([v * bb, k_beta * jnp.exp(gcum)], axis=-1), 'nn')
    u, w = halves(uw)
    qk = jnp.where(causal, bd(qn, kn, 'nt') * decay, 0.0)
    v_new = u - bd(w, state, 'nn')
    o = bd(qn * jnp.exp(gcum), state, 'nn') + bd(qk, v_new, 'nn')
    last = lax.broadcasted_iota(jnp.int32, (c, d), 0) == c - 1
    g_last = jnp.sum(jnp.where(last, gcum, 0.0), axis=-2, keepdims=True)
    k_dec = kn * jnp.exp(g_last - gcum)
    return state * jnp.exp(g_last) + bd(k_dec, v_new, 'tn'), o


DN_HEADS_PER_STEP = 8


def _by_head(ref):
    return jnp.stack([ref[:, s] for s in _heads(ref.shape[1])])


def _delta_specs(nh, nc, rev):
    c, d = DN_CHUNK, HEAD_DIM
    hb = min(DN_HEADS_PER_STEP, nh)
    ng = nh // hb
    ch = (lambda n: nc - 1 - n) if rev else (lambda n: n)
    qkv = [pl.BlockSpec((c, hb * d), functools.partial(lambda h, n, o: (ch(n), o * ng + h), o=o)) for o in range(3)]
    col = pl.BlockSpec((hb, c, 1), lambda h, n: (h, ch(n), 0))
    row = pl.BlockSpec((hb, None, 1, c), lambda h, n: (h, ch(n), 0, 0))
    st = pl.BlockSpec((hb, None, d, d), lambda h, n: (h, ch(n), 0, 0))
    tok = pl.BlockSpec((c, hb * d), lambda h, n: (ch(n), h))
    return hb, ng, qkv, col, row, st, tok


def _delta_fwd(qkv, gcol, grow, bcol, name):
    t = qkv.shape[0]
    nh, nc, d = gcol.shape[0], t // DN_CHUNK, HEAD_DIM
    hb, ng, qkv_s, col, row, st, tok = _delta_specs(nh, nc, False)

    def body(q_ref, k_ref, v_ref, gc_ref, gr_ref, bc_ref, o_ref, st_ref, state):
        @pl.when(pl.program_id(1) == 0)
        def _():
            state[...] = jnp.zeros_like(state)

        s0 = state[...]
        st_ref[...] = s0
        s1, o = _delta_chunk(_DELTA_OPS, s0, _by_head(q_ref), _by_head(k_ref), _by_head(v_ref), gc_ref[...],
                             gr_ref[...], bc_ref[...])
        for j, s in enumerate(_heads(hb * d)):
            o_ref[:, s] = o[j]
        state[...] = s1

    return _pcall(body, name=name, grid=(ng, nc), in_specs=qkv_s + [col, row, col], out_specs=[tok, st],
                  out_shape=[jax.ShapeDtypeStruct((t, nh * d), F32), jax.ShapeDtypeStruct((nh, nc, d, d), F32)],
                  scratch_shapes=[pltpu.VMEM((hb, d, d), F32)],
                  compiler_params=_params(("parallel", "arbitrary")))(qkv, qkv, qkv, gcol, grow, bcol)


def _delta_bwd(qkv, gcol, grow, bcol, states, do, name):
    t = qkv.shape[0]
    nh, nc, d = gcol.shape[0], t // DN_CHUNK, HEAD_DIM
    hb, ng, qkv_s, col, row, st, tok = _delta_specs(nh, nc, True)

    def body(q_ref, k_ref, v_ref, gc_ref, gr_ref, bc_ref, st_ref, do_ref,
             dq_ref, dk_ref, dv_ref, dgc_ref, dgr_ref, dbc_ref, dstate):
        @pl.when(pl.program_id(1) == 0)
        def _():
            dstate[...] = jnp.zeros_like(dstate)

        _, vjp = jax.vjp(functools.partial(_delta_chunk, _DELTA_OPS_VJP), st_ref[...], _by_head(q_ref), _by_head(k_ref),
                         _by_head(v_ref), gc_ref[...], gr_ref[...], bc_ref[...])
        ds, dq, dk, dv, dgc, dgr, dbc = vjp((dstate[...], _by_head(do_ref)))
        for j, s in enumerate(_heads(hb * d)):
            dq_ref[:, s], dk_ref[:, s], dv_ref[:, s] = dq[j], dk[j], dv[j]
        dgc_ref[...], dgr_ref[...], dbc_ref[...] = dgc, dgr, dbc
        dstate[...] = ds

    tok_out = jax.ShapeDtypeStruct((t, nh * d), F32)
    return _pcall(body, name=name, grid=(ng, nc), in_specs=qkv_s + [col, row, col, st, tok],
                  out_specs=[tok, tok, tok, col, row, col],
                  out_shape=[tok_out, tok_out, tok_out, jax.ShapeDtypeStruct(gcol.shape, F32),
                             jax.ShapeDtypeStruct(grow.shape, F32), jax.ShapeDtypeStruct(bcol.shape, F32)],
                  scratch_shapes=[pltpu.VMEM((hb, d, d), F32)],
                  compiler_params=_params(("parallel", "arbitrary")))(qkv, qkv, qkv, gcol, grow, bcol, states, do)


def _outgate_head(o_h, z_h, w):
    return _rms(o_h, w) * _silu(z_h)


def _outgate_fwd(o, z, w, name):
    def fn(o_, z_, w_):
        return (jnp.concatenate([_outgate_head(o_[:, s], z_[:, s], w_) for s in _heads(o_.shape[1])], axis=1),)

    return _rowcall(name, fn, [o, z], [w], [(o.shape[1], BF16)])[0]


def _outgate_bwd(o, z, w, dy, name):
    def fn(o_, z_, dy_, w_):
        do, dz, dw = [], [], None
        for s in _heads(o_.shape[1]):
            _, vjp = jax.vjp(_outgate_head, o_[:, s], z_[:, s], w_)
            a, b, c = vjp(dy_[:, s])
            do.append(a), dz.append(b)
            dw = c if dw is None else dw + c
        return jnp.concatenate(do, axis=1), jnp.concatenate(dz, axis=1), dw

    wd = o.shape[1]
    return _rowcall(name, fn, [o, z, dy], [w], [(wd, F32), (wd, F32)], [w.shape])


def _attn_head(dot, q_h, k_h, v_h):
    s = dot(q_h, k_h, 'nt') * (q_h.shape[1] ** -0.5)
    e = jnp.exp(s - lax.stop_gradient(jnp.max(s, axis=-1, keepdims=True)))
    p = e / jnp.sum(e, axis=-1, keepdims=True)
    return dot(p, v_h, 'nn')


def _xa_slices(width):
    hd = width // XA_HEADS
    return [slice(h * hd, (h + 1) * hd) for h in range(XA_HEADS)]


def _attn_fwd(q, kv, name):
    wd = q.shape[1]

    def fn(q_, kv_):
        k_, v_ = kv_[:, :wd], kv_[:, wd:]
        return (jnp.concatenate([_attn_head(_bdot_raw, q_[:, s], k_[:, s], v_[:, s]) for s in _xa_slices(wd)],
                                axis=1),)

    return _rowcall(name, fn, [q], [kv], [(wd, BF16)])[0]


def _attn_bwd(q, kv, do, name):
    wd = q.shape[1]

    def fn(q_, do_, kv_):
        k_, v_ = kv_[:, :wd].astype(F32), kv_[:, wd:].astype(F32)
        dq, dk, dv = [], [], []
        for s in _xa_slices(wd):
            _, vjp = jax.vjp(functools.partial(_attn_head, _bdot), q_[:, s].astype(F32), k_[:, s], v_[:, s])
            a, b, c = vjp(do_[:, s])
            dq.append(a), dk.append(b), dv.append(c)
        return jnp.concatenate(dq, axis=1), jnp.concatenate(dk + dv, axis=1)

    return _rowcall(name, fn, [q, do], [kv], [(wd, BF16)], [kv.shape])


def _loss_call(y, target, name):
    d = y.shape[1]

    def fn(y_, t_):
        err = y_ - t_
        part = 0.5 * jnp.sum(jnp.sum(err * err, axis=1, keepdims=True), axis=0, keepdims=True) / d
        return err / d, jnp.broadcast_to(part, (1, LANES))

    return _rowcall(name, fn, [y, target], [], [(d, F32)], [(1, LANES)])


def _adamw_call(w, g, m, v, name):
    def fn(w_, g_, m_, v_):
        m1 = ADAM_B1 * m_ + (1.0 - ADAM_B1) * g_
        v1 = ADAM_B2 * v_ + (1.0 - ADAM_B2) * jnp.square(g_)
        m_hat = m1 / (1.0 - ADAM_B1 ** ADAM_STEP)
        v_hat = v1 / (1.0 - ADAM_B2 ** ADAM_STEP)
        return -ADAM_LR * (m_hat / (jnp.sqrt(v_hat) + ADAM_EPS) + ADAM_WD * w_), m1, v1

    c = w.shape[1]
    return _rowcall(name, fn, [w, g, m, v], [], [(c, F32)] * 3, br=_divtile(w.shape[0], 128, 8))


STREAM_BLOCK_BYTES = 2 * 1024 * 1024


def _rows_for(cols, itemsize):
    return max(16, STREAM_BLOCK_BYTES // (cols * itemsize) // 16 * 16)


def _pair_add(grad, theirs, core, name):
    s, hr, cols = theirs.shape
    br = _divtile(hr, _rows_for(cols, 2), 16)
    nb = hr // br

    def body(c_ref, g_ref, t_ref, o_ref):
        o_ref[...] = (g_ref[...].astype(F32) + t_ref[...].astype(F32)).astype(o_ref.dtype)

    spec = pl.BlockSpec((None, br, cols), lambda j, i, c_ref: (j, i, 0))
    return _pcall(body, name=name, out_shape=jax.ShapeDtypeStruct(theirs.shape, BF16),
                  grid_spec=pltpu.PrefetchScalarGridSpec(
                      num_scalar_prefetch=1, grid=(s, nb),
                      in_specs=[pl.BlockSpec((None, br, cols), lambda j, i, c_ref: (j, c_ref[0] * nb + i, 0)), spec],
                      out_specs=spec),
                  compiler_params=_params(("parallel", "parallel")))(core, grad, theirs)


def _chip_sum(part, landed, kind, where, grad, layer, name):
    _, hr, cw = landed.shape
    br = _divtile(hr, _rows_for(cw, 2), 16)
    nb = hr // br

    def body(w_ref, p_ref, a_ref, b_ref, d_ref, g_ref, o_ref):
        o_ref[...] = ((p_ref[...].astype(F32) + a_ref[...].astype(F32)) + b_ref[...].astype(F32)) + d_ref[...].astype(F32)

    if kind == 'row':
        own = pl.BlockSpec((None, br, cw), lambda i, w_ref: (w_ref[0], i, 0))
    else:
        own = pl.BlockSpec((br, cw), lambda i, w_ref: (i, w_ref[0]))
    got = [pl.BlockSpec((None, br, cw), functools.partial(lambda i, w_ref, k: (k, i, 0), k=k)) for k in range(3)]
    return _pcall(body, name=name, out_shape=jax.ShapeDtypeStruct(grad.shape, F32),
                  grid_spec=pltpu.PrefetchScalarGridSpec(
                      num_scalar_prefetch=1, grid=(nb,), in_specs=[own] + got + [ANY_SPEC],
                      out_specs=pl.BlockSpec((None, br, cw), lambda i, w_ref: (layer, w_ref[1] * nb + i, 0))),
                  input_output_aliases={5: 0},
                  compiler_params=_params(("parallel",)))(where, part, landed, landed, landed, grad)


def _position():
    x, y, c = lax.axis_index("x"), lax.axis_index("y"), lax.axis_index("c")
    others = [(1 - x, y), (x, 1 - y), (1 - x, 1 - y)]
    return x, y, c, others


def _any_specs(n):
    return [pl.BlockSpec(memory_space=pl.ANY)] * n


def _ds(start, size):
    return pl.ds(pl.multiple_of(start, size), size)


HBM_SPEC = pl.BlockSpec(memory_space=pltpu.HBM)
SEM_SPEC = pl.BlockSpec(memory_space=pltpu.SEMAPHORE)
ANY_SPEC = pl.BlockSpec(memory_space=pl.ANY)
DATAFLOW = pltpu.SideEffectType.DATAFLOW_SIDE_EFFECTING


def _hbm(a):
    return pltpu.with_memory_space_constraint(a, pltpu.HBM)


def _full_shape(shard_shape, kind):
    r, cw = shard_shape
    return (N_CHIPS, r, cw) if kind == 'row' else (r, N_CHIPS * cw)


def _block_dims(full, kind):
    return (full.shape[1], full.shape[2]) if kind == 'row' else (full.shape[0], full.shape[1] // N_CHIPS)


def _region(full, kind, chip, half):
    if kind == 'all':
        return full.at[chip]
    r, cw = _block_dims(full, kind)
    if kind == 'row':
        return full.at[chip, _ds(half * (r // 2), r // 2), :]
    return full.at[_ds(half * (r // 2), r // 2), _ds(chip * cw, cw)]


def _place_block(full, shards, layer, kind, chip, name):
    _, r, cs = shards.shape
    cw = _block_dims(full, kind)[1]
    br = _divtile(r, 256, 16)

    def body(c_ref, s_ref, full_ref, o_ref):
        if cs == cw:
            o_ref[...] = s_ref[...].astype(o_ref.dtype)
        else:
            o_ref[:, :cs] = s_ref[...].astype(o_ref.dtype)
            o_ref[:, cs:] = jnp.zeros((br, cw - cs), o_ref.dtype)

    if kind == 'row':
        out_spec = pl.BlockSpec((None, br, cw), lambda i, c_ref: (c_ref[0], i, 0))
    else:
        out_spec = pl.BlockSpec((br, cw), lambda i, c_ref: (i, c_ref[0]))
    return _pcall(body, name=name, out_shape=jax.ShapeDtypeStruct(full.shape, full.dtype),
                  grid_spec=pltpu.PrefetchScalarGridSpec(
                      num_scalar_prefetch=1, grid=(r // br,),
                      in_specs=[pl.BlockSpec((None, br, cs), lambda i, c_ref: (layer, i, 0)), ANY_SPEC],
                      out_specs=out_spec),
                  input_output_aliases={2: 0}, compiler_params=_params(("parallel",)))(chip, shards, full)


def _split_copy_start(bufs, lands, plan, after, name):
    nb, nl = len(bufs), len(lands)
    extra = _as_list(after)

    def body(*refs):
        ins = refs[:nb + nl]
        send, recv = refs[nb + nl + len(extra)], refs[nb + nl + len(extra) + 1]
        for cp in plan(ins[:nb], ins[nb:], send, recv):
            cp.start()
        refs[-1][...] = jnp.zeros_like(refs[-1])

    n_copies = plan.n_copies
    out = _pcall(
        body, name=name,
        out_shape=(pltpu.SemaphoreType.DMA((n_copies,)), pltpu.SemaphoreType.DMA((n_copies,)),
                   *[pltpu.HBM(a.shape, a.dtype) for a in (*bufs, *lands)], jax.ShapeDtypeStruct((8, LANES), F32)),
        in_specs=[HBM_SPEC] * (nb + nl) + [ANY_SPEC] * len(extra),
        out_specs=(SEM_SPEC, SEM_SPEC, *[HBM_SPEC] * (nb + nl), pl.BlockSpec(memory_space=pltpu.VMEM)),
        input_output_aliases={i: 2 + i for i in range(nb + nl)},
        compiler_params=pltpu.CompilerParams(has_side_effects=DATAFLOW),
    )(*[_hbm(a) for a in (*bufs, *lands)], *extra)
    return out[0], out[1], list(out[2:2 + nb]), list(out[2 + nb:2 + nb + nl]), out[-1]


def _split_copy_wait(send, recv, bufs, lands, plan, after, name):
    nb, nl = len(bufs), len(lands)
    extra = _as_list(after)

    def body(*refs):
        ins = refs[:nb + nl]
        send_ref, recv_ref = refs[nb + nl], refs[nb + nl + 1]
        for cp in plan(ins[:nb], ins[nb:], send_ref, recv_ref):
            cp.wait_send()
            cp.wait_recv()

    out = _pcall(
        body, name=name, out_shape=tuple(pltpu.HBM(a.shape, a.dtype) for a in (*bufs, *lands)),
        in_specs=[HBM_SPEC] * (nb + nl) + [SEM_SPEC, SEM_SPEC] + [ANY_SPEC] * len(extra),
        out_specs=tuple([HBM_SPEC] * (nb + nl)), input_output_aliases={i: i for i in range(nb + nl)},
        compiler_params=pltpu.CompilerParams(has_side_effects=DATAFLOW),
    )(*bufs, *lands, send, recv, *extra)
    return list(out[:nb]), list(out[nb:])


def _gather_plan(kinds):
    def plan(bufs, lands, send, recv):
        x, y, c, others = _position()
        me = 2 * x + y
        return [pltpu.make_async_remote_copy(
            src_ref=_region(lands[w], kinds[w], me, c), dst_ref=_region(lands[w], kinds[w], me, c),
            send_sem=send.at[3 * w + k], recv_sem=recv.at[3 * w + k], device_id=(px, py, c), device_id_type=MESH_IDS)
            for w in range(len(lands)) for k, (px, py) in enumerate(others)]

    def wait_plan(bufs, lands, send, recv):
        x, y, c, others = _position()
        me = 2 * x + y
        return [pltpu.make_async_remote_copy(
            src_ref=_region(lands[w], kinds[w], me, c), dst_ref=_region(lands[w], kinds[w], 2 * px + py, c),
            send_sem=send.at[3 * w + k], recv_sem=recv.at[3 * w + k], device_id=(px, py, c), device_id_type=MESH_IDS)
            for w in range(len(lands)) for k, (px, py) in enumerate(others)]

    plan.n_copies = wait_plan.n_copies = 3 * len(kinds)
    return plan, wait_plan


def _gather_forward(fulls, kinds, name):
    n = len(fulls)

    def body(*refs):
        dst = refs[n:2 * n]
        send, recv = refs[2 * n:]
        x, y, c, others = _position()
        sib = (x, y, 1 - c)

        def copy(w, k, chip, half):
            reg = _region(dst[w], kinds[w], chip, half)
            return pltpu.make_async_remote_copy(src_ref=reg, dst_ref=reg, send_sem=send.at[3 * w + k],
                                                recv_sem=recv.at[3 * w + k], device_id=sib, device_id_type=MESH_IDS)

        give = [copy(w, k, 2 * px + py, c) for w in range(n) for k, (px, py) in enumerate(others)]
        for cp in give:
            cp.start()
        for w in range(n):
            for k, (px, py) in enumerate(others):
                copy(w, k, 2 * px + py, 1 - c).wait_recv()
        for cp in give:
            cp.wait_send()

    return _pcall(body, name=name, in_specs=_any_specs(n), out_specs=_any_specs(n),
                  out_shape=[jax.ShapeDtypeStruct(f.shape, f.dtype) for f in fulls],
                  input_output_aliases={i: i for i in range(n)},
                  scratch_shapes=[pltpu.SemaphoreType.DMA((3 * n,)), pltpu.SemaphoreType.DMA((3 * n,))])(*fulls)


def _grad_half(ref, kind, half):
    if kind == 'row':
        hr = ref.shape[1] // 2
        return ref.at[:, _ds(half * hr, hr), :]
    hr = ref.shape[0] // 2
    return ref.at[_ds(half * hr, hr), :]


def _half_shape(g, kind):
    return (g.shape[0], g.shape[1] // 2, g.shape[2]) if kind == 'row' else (g.shape[0] // 2, g.shape[1])


def _pair_plan(kinds):
    def plan(bufs, lands, send, recv):
        x, y, c, _ = _position()
        return [pltpu.make_async_remote_copy(src_ref=_grad_half(bufs[w], kinds[w], 1 - c), dst_ref=lands[w],
                                             send_sem=send.at[w], recv_sem=recv.at[w], device_id=(x, y, 1 - c),
                                             device_id_type=MESH_IDS) for w in range(len(bufs))]

    plan.n_copies = len(kinds)
    return plan


def _pair_exchange(grads, kinds, name):
    n = len(grads)

    def body(*refs):
        src, theirs_ref = refs[:n], refs[n:2 * n]
        send, recv = refs[2 * n:]
        x, y, c, _ = _position()
        sib = (x, y, 1 - c)

        def half(w, h):
            if kinds[w] == 'row':
                hr = src[w].shape[1] // 2
                return src[w].at[:, _ds(h * hr, hr), :]
            hr = src[w].shape[0] // 2
            return src[w].at[_ds(h * hr, hr), :]

        give = [pltpu.make_async_remote_copy(src_ref=half(w, 1 - c), dst_ref=theirs_ref[w], send_sem=send.at[w],
                                             recv_sem=recv.at[w], device_id=sib, device_id_type=MESH_IDS)
                for w in range(n)]
        for cp in give:
            cp.start()
        for cp in give:
            cp.wait()

    def half_shape(g, kind):
        return (g.shape[0], g.shape[1] // 2, g.shape[2]) if kind == 'row' else (g.shape[0] // 2, g.shape[1])

    return _pcall(body, name=name, in_specs=_any_specs(n), out_specs=_any_specs(n),
                  out_shape=[jax.ShapeDtypeStruct(half_shape(g, kd), g.dtype) for g, kd in zip(grads, kinds)],
                  scratch_shapes=[pltpu.SemaphoreType.DMA((n,)), pltpu.SemaphoreType.DMA((n,))])(*grads)


def _part_dims(part, kind):
    return (part.shape[1], part.shape[2]) if kind == 'row' else (part.shape[0], part.shape[1] // N_CHIPS)


def _chip_plan(kinds):
    def plan(bufs, lands, send, recv):
        x, y, c, others = _position()

        def slot(w, chip):
            if kinds[w] == 'row':
                return bufs[w].at[chip]
            cw = _part_dims(bufs[w], kinds[w])[1]
            return bufs[w].at[:, _ds(chip * cw, cw)]

        return [pltpu.make_async_remote_copy(src_ref=slot(w, 2 * px + py), dst_ref=lands[w].at[k],
                                             send_sem=send.at[3 * w + k], recv_sem=recv.at[3 * w + k],
                                             device_id=(px, py, c), device_id_type=MESH_IDS)
                for w in range(len(bufs)) for k, (px, py) in enumerate(others)]

    plan.n_copies = 3 * len(kinds)
    return plan


def _pair_gather(grads, name, after=None):
    n = len(grads)
    n_layers = grads[0].shape[0]
    extra = _as_list(after)

    def body(*refs):
        dst = refs[n + len(extra):2 * n + len(extra)]
        send, recv = refs[2 * n + len(extra):]
        x, y, c, _ = _position()
        sib = (x, y, 1 - c)

        def copy(w, l, half):
            hr = dst[w].shape[1] // 2
            rows = dst[w].at[l, _ds(half * hr, hr), :]
            return pltpu.make_async_remote_copy(src_ref=rows, dst_ref=rows, send_sem=send.at[w * n_layers + l],
                                                recv_sem=recv.at[w * n_layers + l], device_id=sib,
                                                device_id_type=MESH_IDS)

        give = [copy(w, l, c) for w in range(n) for l in range(n_layers)]
        for cp in give:
            cp.start()
        for w in range(n):
            for l in range(n_layers):
                copy(w, l, 1 - c).wait_recv()
        for cp in give:
            cp.wait_send()

    m = n * n_layers
    return _pcall(body, name=name, in_specs=_any_specs(n + len(extra)), out_specs=_any_specs(n),
                  out_shape=[jax.ShapeDtypeStruct(g.shape, g.dtype) for g in grads],
                  input_output_aliases={i: i for i in range(n)},
                  scratch_shapes=[pltpu.SemaphoreType.DMA((m,)), pltpu.SemaphoreType.DMA((m,))])(*grads, *extra)


def _all_sum(buf, name):
    rows = buf.shape[0]
    flips = [(fx, fy, fc) for fx in (0, 1) for fy in (0, 1) for fc in (0, 1)][1:]

    def body(x_ref, o_ref, land, send, recv):
        x, y, c, _ = _position()
        me = 4 * x + 2 * y + c

        def peer(f):
            return (1 - x if f[0] else x, 1 - y if f[1] else y, 1 - c if f[2] else c)

        give = [pltpu.make_async_remote_copy(src_ref=x_ref, dst_ref=land.at[me], send_sem=send.at[k],
                                             recv_sem=recv.at[k], device_id=peer(f), device_id_type=MESH_IDS)
                for k, f in enumerate(flips)]
        for cp in give:
            cp.start()
        land[me] = x_ref[...]
        for k, f in enumerate(flips):
            px, py, pc = peer(f)
            slot = land.at[4 * px + 2 * py + pc]
            pltpu.make_async_remote_copy(src_ref=slot, dst_ref=slot, send_sem=send.at[k], recv_sem=recv.at[k],
                                         device_id=peer(f), device_id_type=MESH_IDS).wait_recv()
        for cp in give:
            cp.wait_send()
        total = land[0]
        for dev in range(1, N_DEV):
            total = total + land[dev]
        o_ref[...] = total

    return _pcall(body, name=name, in_specs=[pl.BlockSpec(memory_space=pltpu.VMEM)],
                  out_specs=pl.BlockSpec(memory_space=pltpu.VMEM), out_shape=jax.ShapeDtypeStruct(buf.shape, F32),
                  scratch_shapes=[pltpu.VMEM((N_DEV, rows, LANES), F32), pltpu.SemaphoreType.DMA((N_DEV - 1,)),
                                  pltpu.SemaphoreType.DMA((N_DEV - 1,))],
                  compiler_params=pltpu.CompilerParams(vmem_limit_bytes=VMEM_LIMIT))(buf)


def _pack(arrays):
    flat = jnp.concatenate([a.reshape(-1).astype(F32) for a in arrays])
    pad = (-flat.shape[0]) % (8 * LANES)
    return jnp.pad(flat, (0, pad)).reshape(-1, LANES)


def _unpack(buf, like):
    flat, out, off = buf.reshape(-1), [], 0
    for a in like:
        out.append(flat[off:off + a.size].reshape(a.shape))
        off += a.size
    return out


def _row2(v):
    return v.reshape(1, -1)


def _ffn_fwd(x, g_pre, w_gu, w_d, g_post, tag, after=None):
    h = _norm_fwd(x, g_pre, tag + "_pre", after)
    gu = _mm(h, w_gu, 'nn', F32, tag + "_gu")
    a = _swiglu_fwd(gu, tag + "_act")
    f = _mm(a, w_d, 'nn', F32, tag + "_down")
    return _resnorm_fwd(x, f, g_post, 0.5, tag + "_post"), (x, h, gu, a, f)


def _ffn_bwd_weights(dx_out, saved, w_d, g_post, tag, after=None):
    x, h, gu, a, f = saved
    df, dg_post = _resnorm_bwd(f, dx_out, g_post, 0.5, tag + "_post_b", after)
    da = _mm(df, w_d, 'nt', F32, tag + "_down_bx")
    dw_d = _mm(a, df, 'tn', BF16, tag + "_down_bw")
    dgu = _swiglu_bwd(gu, da, tag + "_act_b")
    dw_gu = _mm(h, dgu, 'tn', BF16, tag + "_gu_bw")
    return dgu, dw_gu, dw_d, dg_post


def _ffn_bwd_input(dx_out, saved, dgu, g_pre, w_gu, tag, after=None):
    x = saved[0]
    dh = _mm(dgu, w_gu, 'nt', F32, tag + "_gu_bx")
    return _norm_bwd(x, dh, dx_out, g_pre, tag + "_pre_b", after)


def _ffn_bwd(dx_out, saved, g_pre, w_gu, w_d, g_post, tag, after=None):
    dgu, dw_gu, dw_d, dg_post = _ffn_bwd_weights(dx_out, saved, w_d, g_post, tag, after)
    dx, dg_pre = _ffn_bwd_input(dx_out, saved, dgu, g_pre, w_gu, tag)
    return dx, dg_pre, dw_gu, dw_d, dg_post


def _split_cols(pp, cs, cp, widths):
    proj = jnp.concatenate([pp[:, j * cp:j * cp + cs] for j in range(N_CHIPS)], axis=1)
    out, off = [], 0
    for wd in widths:
        out.append(proj[:, off:off + wd])
        off += wd
    return out


def _join_cols(pieces, cs, cp):
    proj = jnp.concatenate(pieces, axis=1)
    t = proj.shape[0]
    cols = []
    for j in range(N_CHIPS):
        cols += [proj[:, j * cs:(j + 1) * cs], jnp.zeros((t, cp - cs), proj.dtype)]
    return jnp.concatenate(cols, axis=1)


def _mix_fwd(x, p, w_in, w_out, conv_w, cs, cp, tag, after=None):
    t, d = x.shape
    half = d // 2
    nh = half // HEAD_DIM
    h = _norm_fwd(x, p['mix_norm_pre'], tag + "_pre", after)
    pp = _mm(h, w_in, 'nn', F32, tag + "_in")
    u_a, v_a, qkv_raw, z, b_raw, a_raw = _split_cols(pp, cs, cp, [half, half, 3 * half, half, nh, nh])
    y_a = _sgu_fwd(u_a, v_a, p['sm_w'], p['sm_b'], p['sm_ln_g'], p['sm_ln_b'], tag + "_sgu")
    qkv = _conv_fwd(qkv_raw, conv_w, tag + "_conv")
    b_raw, a_raw = b_raw.T, a_raw.T
    beta, gcum = _gates_fwd(b_raw, a_raw, p['a_log'].T, p['dt_bias'].T, tag + "_gates")
    gcol, bcol = gcum[:, :, None], beta[:, :, None]
    grow = gcum.reshape(nh, t // DN_CHUNK, 1, DN_CHUNK)
    o, states = _delta_fwd(qkv, gcol, grow, bcol, tag + "_delta")
    y_b = _outgate_fwd(o, z, p['dn_norm_w'], tag + "_gate")
    y = jnp.concatenate([y_a, y_b], axis=1)
    m = _mm(y, w_out, 'nn', F32, tag + "_out")
    x_out = _resnorm_fwd(x, m, p['mix_norm_post'], 1.0, tag + "_post")
    return x_out, (x, h, u_a, v_a, qkv_raw, z, b_raw, a_raw, qkv, gcol, grow, bcol, states, o, y, m)


def _mix_bwd(dx_out, saved, p, w_in, w_out, conv_w, cs, cp, tag, after=None):
    x, h, u_a, v_a, qkv_raw, z, b_raw, a_raw, qkv, gcol, grow, bcol, states, o, y, m = saved
    t, d = x.shape
    half = d // 2
    nh = half // HEAD_DIM
    gr = {}
    dm, gr['mix_norm_post'] = _resnorm_bwd(m, dx_out, p['mix_norm_post'], 1.0, tag + "_post_b", after)
    dy = _mm(dm, w_out, 'nt', F32, tag + "_out_bx")
    gr['w_out'] = _mm(y, dm, 'tn', BF16, tag + "_out_bw")
    dy_a, dy_b = dy[:, :half], dy[:, half:]
    do, dz, gr['dn_norm_w'] = _outgate_bwd(o, z, p['dn_norm_w'], dy_b, tag + "_gate_b")
    dq, dk, dv, dgcol, dgrow, dbcol = _delta_bwd(qkv, gcol, grow, bcol, states, do, tag + "_delta_b")
    dgcum = dgcol[:, :, 0] + dgrow.reshape(nh, t)
    db_raw, da_raw, gr['a_log'], gr['dt_bias'] = _gates_bwd(b_raw, a_raw, p['a_log'].T, p['dt_bias'].T,
                                                             dbcol[:, :, 0], dgcum, tag + "_gates_b")
    db_raw, da_raw = db_raw.T, da_raw.T
    thirds = [_conv_bwd(qkv_raw, conv_w, dpart, i * half, tag + "_conv_b") for i, dpart in enumerate((dq, dk, dv))]
    gr['conv_w'] = jnp.concatenate([dw for _, dw in thirds], axis=1)
    du_a, dv_a, gr['sm_w'], gr['sm_b'], gr['sm_ln_g'], gr['sm_ln_b'] = _sgu_bwd(
        u_a, v_a, p['sm_w'], p['sm_b'], p['sm_ln_g'], p['sm_ln_b'], dy_a, tag + "_sgu_b")
    dpp = _join_cols([du_a, dv_a, *[dx for dx, _ in thirds], dz, db_raw, da_raw], cs, cp).astype(BF16)
    dh = _mm(dpp, w_in, 'nt', F32, tag + "_in_bx")
    gr['w_in'] = _mm(h, dpp, 'tn', BF16, tag + "_in_bw")
    dx, gr['mix_norm_pre'] = _norm_bwd(x, dh, dx_out, p['mix_norm_pre'], tag + "_pre_b")
    return dx, gr


def _xa_fwd(x, mem, p, w_xq, w_xkv, w_xo, tag, after=None):
    h = _norm_fwd(x, p['xa_norm_pre'], tag + "_pre", after)
    mh = _norm_fwd(mem, p['mem_norm'], tag + "_mem")
    q = _mm(h, w_xq, 'nn', BF16, tag + "_q")
    kv = _mm(mh, w_xkv, 'nn', BF16, tag + "_kv")
    o = _attn_fwd(q, kv, tag + "_attn")
    c = _mm(o, w_xo, 'nn', F32, tag + "_o")
    return _resnorm_fwd(x, c, p['xa_norm_post'], 1.0, tag + "_post"), (x, h, mh, q, kv, o, c)


def _xa_bwd(dx_out, saved, mem, p, w_xq, w_xkv, w_xo, tag, after=None):
    x, h, mh, q, kv, o, c = saved
    gr = {}
    dc, gr['xa_norm_post'] = _resnorm_bwd(c, dx_out, p['xa_norm_post'], 1.0, tag + "_post_b", after)
    do = _mm(dc, w_xo, 'nt', F32, tag + "_o_bx")
    gr['w_xo'] = _mm(o, dc, 'tn', BF16, tag + "_o_bw")
    dq, dkv = _attn_bwd(q, kv, do, tag + "_attn_b")
    dkv = dkv.astype(BF16)
    dh = _mm(dq, w_xq, 'nt', F32, tag + "_q_bx")
    gr['w_xq'] = _mm(h, dq, 'tn', BF16, tag + "_q_bw")
    dmh = _mm(dkv, w_xkv, 'nt', F32, tag + "_kv_bx")
    gr['w_xkv'] = _mm(mh, dkv, 'tn', BF16, tag + "_kv_bw")
    gr['mem_norm'] = _norm_bwd_gain(mem, dmh, p['mem_norm'], tag + "_mem_b")
    dx, gr['xa_norm_pre'] = _norm_bwd(x, dh, dx_out, p['xa_norm_pre'], tag + "_pre_b")
    return dx, gr


def _step(inputs):
    x, mem, target = inputs['x'][0], inputs['mem'][0], inputs['loss_target'][0]
    n_layers = inputs['w_in'].shape[0]
    cx, cy, cc = lax.axis_index("x"), lax.axis_index("y"), lax.axis_index("c")
    chip = 2 * cx + cy
    cs = inputs['w_in'].shape[2]
    cp = -(-cs // LANES) * LANES
    kinds = [BIG[nm] for nm in BIG_NAMES]

    conv_local = inputs['conv_w']
    ccols = conv_local.shape[2]

    core_idx, chip_idx = cc.astype(jnp.int32).reshape(1), chip.astype(jnp.int32).reshape(1)
    where_idx = jnp.stack([chip, cc]).astype(jnp.int32)
    order = [(l, gi) for l in range(n_layers) for gi in range(len(GROUPS))]
    n_groups = len(order)

    def small(l):
        p = {nm: inputs[nm][l] for nm in SMALL_NAMES}
        for nm in SMALL_NAMES:
            if p[nm].ndim == 1:
                p[nm] = _row2(p[nm])
        p['sm_b'] = p['sm_b'][:, :, None]
        return p

    def place(k):
        l, gi = order[k]
        lands = []
        for nm in GROUPS[gi][1]:
            _, r, c_in = inputs[nm].shape
            shape = _full_shape((r, cp if nm == 'w_in' else c_in), BIG[nm])
            lands.append(_place_block(lax.empty(shape, BF16), inputs[nm], l, BIG[nm], chip_idx, "place_block"))
        return lands

    placed = {}
    conv_group = 1

    def gather_start(k, after):
        names = list(GROUPS[order[k][1]][1])
        kds = [BIG[nm] for nm in names]
        if k not in placed:
            placed[k] = place(k)
        lands = list(placed[k])
        if k == conv_group:
            mine = conv_local.reshape(1, n_layers * CONV_WIDTH, ccols)
            lands.append(lax.dynamic_update_slice(jnp.zeros((N_CHIPS, *mine.shape[1:]), F32), mine, (chip, 0, 0)))
            names.append('conv_w'), kds.append('all')
        plan, wait_plan = _gather_plan(kds)
        send, recv, _, lands, token = _split_copy_start([], lands, plan, after, f"gather_start_{k}")
        return dict(send=send, recv=recv, lands=lands, token=token, kinds=kds, names=names, wait_plan=wait_plan)

    def gather_finish(k, st, after):
        _, lands = _split_copy_wait(st['send'], st['recv'], [], st['lands'], st['wait_plan'], after, f"gather_wait_{k}")
        got = dict(zip(st['names'], lands))
        halves = [nm for nm, kd in zip(st['names'], st['kinds']) if kd != 'all']
        passed = _gather_forward([got[nm] for nm in halves], [BIG[nm] for nm in halves],
                                 "gather_forward_" + GROUPS[order[k][1]][0])
        w = {nm: (g.reshape(-1, g.shape[2]) if BIG[nm] == 'row' else g) for nm, g in zip(halves, passed)}
        if 'conv_w' in got:
            w['conv_w'] = got['conv_w'].reshape(N_CHIPS, n_layers, CONV_WIDTH, ccols).transpose(1, 2, 0, 3).reshape(
                n_layers, CONV_WIDTH, N_CHIPS * ccols)
        return w

    def group_fwd(gi, x_, p, w, l, token):
        if gi == 0:
            return _ffn_fwd(x_, p['ffn1_norm_pre'], w['ffn1_w_gate_up'], w['ffn1_w_down'], p['ffn1_norm_post'], "ffn", token)
        if gi == 1:
            return _mix_fwd(x_, p, w['w_in'], w['w_out'], conv_full[l], cs, cp, "mix", token)
        if gi == 2:
            return _xa_fwd(x_, mem, p, w['w_xq'], w['w_xkv'], w['w_xo'], "xa", token)
        return _ffn_fwd(x_, p['ffn2_norm_pre'], w['ffn2_w_gate_up'], w['ffn2_w_down'], p['ffn2_norm_post'], "ffn", token)

    def group_bwd(gi, dx_, s, p, w, l, token):
        if gi in (0, 3):
            pre = 'ffn1' if gi == 0 else 'ffn2'
            dx_, g_pre, g_gu, g_d, g_post = _ffn_bwd(dx_, s, p[pre + '_norm_pre'], w[pre + '_w_gate_up'],
                                                     w[pre + '_w_down'], p[pre + '_norm_post'], "ffn", token)
            return dx_, {pre + '_norm_pre': g_pre, pre + '_w_gate_up': g_gu, pre + '_w_down': g_d,
                         pre + '_norm_post': g_post}
        if gi == 1:
            return _mix_bwd(dx_, s, p, w['w_in'], w['w_out'], conv_full[l], cs, cp, "mix", token)
        return _xa_bwd(dx_, s, mem, p, w['w_xq'], w['w_xkv'], w['w_xo'], "xa", token)

    started = {0: gather_start(0, None)}
    started[1] = gather_start(1, started[0]['token'])
    weights, saved = [], []
    for k in range(2, n_groups):
        placed[k] = place(k)
    head_work = [started[1]['token']] + [land for k in range(2, n_groups) for land in placed[k]]
    conv_full = None
    for k, (l, gi) in enumerate(order):
        w = gather_finish(k, started[k], x if k > 0 else head_work)
        if k == conv_group:
            conv_full = w.pop('conv_w')
        token = None
        if k + 2 < n_groups:
            started[k + 2] = gather_start(k + 2, next(iter(w.values())))
            token = started[k + 2]['token']
        weights.append(w)
        x, s = group_fwd(gi, x, small(l), w, l, token)
        saved.append(s)

    dx, loss_part = _loss_call(x, target, "loss")

    grads = [dict() for _ in range(n_layers)]
    big_grads = {nm: None for nm in BIG_NAMES}

    def reduce_finish(pd, after):
        parts, lands = _split_copy_wait(pd['send'], pd['recv'], pd['parts'], pd['lands'], pd['plan'], after,
                                        f"chip_wait_{pd['k']}")
        for nm, kd, part, landed in zip(pd['names'], pd['kinds'], parts, lands):
            if big_grads[nm] is None:
                big_grads[nm] = lax.empty((n_layers, 2 * landed.shape[1], landed.shape[2]), F32)
            big_grads[nm] = _chip_sum(part, landed, kd, where_idx, big_grads[nm], pd['l'], "chip_sum")

    def by_chip(gr, names, kds):
        return [gr[nm].reshape(N_CHIPS, -1, gr[nm].shape[1]) if kd == 'row' else gr[nm] for nm, kd in zip(names, kds)]

    def chip_start(k, names, kds, gs, theirs):
        parts = [_pair_add(g, t, core_idx, "pair_add") if kd == 'row'
                 else _pair_add(g[None], t[None], core_idx, "pair_add")[0] for g, t, kd in zip(gs, theirs, kds)]
        lands = [lax.empty((3, *_part_dims(part, kd)), BF16) for part, kd in zip(parts, kds)]
        plan = _chip_plan(kds)
        send, recv, parts, lands, token = _split_copy_start(parts, lands, plan, None, f"chip_start_{k}")
        return dict(send=send, recv=recv, parts=parts, lands=lands, token=token, plan=plan, names=names, kinds=kds,
                    k=k, l=order[k][0])

    pair_pending, chip_queue = None, []
    for k in reversed(range(n_groups)):
        l, gi = order[k]
        names = GROUPS[gi][1]
        kds = [BIG[nm] for nm in names]
        p, w = small(l), weights[k]
        tokens = [pd['token'] for pd in ([pair_pending] if pair_pending else []) + chip_queue] or None
        if k > 0:
            dx_next, gr = group_bwd(gi, dx, saved[k], p, w, l, tokens)
        else:
            dgu, g_gu, g_d, g_post = _ffn_bwd_weights(dx, saved[k], w['ffn1_w_down'], p['ffn1_norm_post'], "ffn", tokens)
            gr = {'ffn1_w_gate_up': g_gu, 'ffn1_w_down': g_d, 'ffn1_norm_post': g_post}
            dx_next = g_gu
        for pd in chip_queue:
            reduce_finish(pd, dx_next)
        chip_queue = []
        if pair_pending:
            pd = pair_pending
            gs_prev, theirs = _split_copy_wait(pd['send'], pd['recv'], pd['gs'], pd['lands'], pd['plan'], dx_next,
                                               f"pair_wait_{pd['k']}")
            chip_queue.append(chip_start(pd['k'], pd['names'], pd['kinds'], gs_prev, theirs))
            pair_pending = None
        gs = by_chip(gr, names, kds)
        if k > 1:
            plan = _pair_plan(kds)
            lands = [lax.empty(_half_shape(g, kd), BF16) for g, kd in zip(gs, kds)]
            send, recv, gs, lands, token = _split_copy_start(gs, lands, plan, None, f"pair_start_{k}")
            pair_pending = dict(send=send, recv=recv, gs=gs, lands=lands, token=token, plan=plan, names=names,
                                kinds=kds, k=k)
        else:
            started_now = chip_start(k, names, kds, gs, _pair_exchange(gs, kds, "pair_exchange_" + GROUPS[gi][0]))
            if k > 0:
                chip_queue.append(started_now)
        if k > 0:
            dx = dx_next
        else:
            pending = started_now
            dx, gr['ffn1_norm_pre'] = _ffn_bwd_input(dx, saved[k], dgu, p['ffn1_norm_pre'], w['ffn1_w_gate_up'], "ffn",
                                                     pending['token'])
        grads[l].update({nm: g for nm, g in gr.items() if nm not in BIG})
    for pd in chip_queue:
        reduce_finish(pd, dx)
    out = {}

    def two(a):
        return a.reshape(-1, a.shape[-1])

    def adamw(nm, g):
        delta, m1, v1 = _adamw_call(two(inputs[nm]), two(g), two(inputs['m_' + nm]), two(inputs['v_' + nm]), "adamw")
        out[nm] = (g, delta.reshape(g.shape), m1.reshape(g.shape), v1.reshape(g.shape))
        return delta

    def finish_weights(gi, after=None):
        names = GROUPS[gi][1]
        got = _pair_gather([big_grads[nm] for nm in names], "pair_gather_" + GROUPS[gi][0], after)
        return [adamw(nm, g[:, :, :cs] if nm == 'w_in' else g) for nm, g in zip(names, got)]

    done = [d for gi in range(len(GROUPS) - 1, 0, -1) for d in finish_weights(gi, pending['token'])]

    small_grads = [jnp.stack([grads[l][nm].reshape(inputs[nm].shape[1:]) if nm != 'conv_w' else grads[l][nm]
                              for l in range(n_layers)]) for nm in SMALL_NAMES]
    summed = _unpack(_all_sum(_pack(small_grads + [loss_part]), "small_sum"), small_grads + [loss_part])
    loss = summed[-1][0, 0]
    small_g = dict(zip(SMALL_NAMES, summed[:-1]))
    small_g['conv_w'] = lax.dynamic_slice(small_g['conv_w'], (0, 0, chip * ccols), (n_layers, CONV_WIDTH, ccols))
    done += [adamw(nm, small_g[nm]) for nm in SMALL_NAMES]

    reduce_finish(pending, done)
    finish_weights(0)
    return (loss, dx[None], *[out[nm][0] for nm in WEIGHTS], *[out[nm][1] for nm in WEIGHTS],
            *[out[nm][2] for nm in WEIGHTS], *[out[nm][3] for nm in WEIGHTS])


def kernel(x, mem, ffn1_norm_pre, ffn1_w_gate_up, ffn1_w_down, ffn1_norm_post, mix_norm_pre, w_in, conv_w, a_log, dt_bias, sm_w, sm_b, sm_ln_g, sm_ln_b, dn_norm_w, w_out, mix_norm_post, xa_norm_pre, mem_norm, w_xq, w_xkv, w_xo, xa_norm_post, ffn2_norm_pre, ffn2_w_gate_up, ffn2_w_down, ffn2_norm_post, loss_target, m_ffn1_norm_pre, m_ffn1_w_gate_up, m_ffn1_w_down, m_ffn1_norm_post, m_mix_norm_pre, m_w_in, m_conv_w, m_a_log, m_dt_bias, m_sm_w, m_sm_b, m_sm_ln_g, m_sm_ln_b, m_dn_norm_w, m_w_out, m_mix_norm_post, m_xa_norm_pre, m_mem_norm, m_w_xq, m_w_xkv, m_w_xo, m_xa_norm_post, m_ffn2_norm_pre, m_ffn2_w_gate_up, m_ffn2_w_down, m_ffn2_norm_post, v_ffn1_norm_pre, v_ffn1_w_gate_up, v_ffn1_w_down, v_ffn1_norm_post, v_mix_norm_pre, v_w_in, v_conv_w, v_a_log, v_dt_bias, v_sm_w, v_sm_b, v_sm_ln_g, v_sm_ln_b, v_dn_norm_w, v_w_out, v_mix_norm_post, v_xa_norm_pre, v_mem_norm, v_w_xq, v_w_xkv, v_w_xo, v_xa_norm_post, v_ffn2_norm_pre, v_ffn2_w_gate_up, v_ffn2_w_down, v_ffn2_norm_post):
    vals = dict(locals())
    return _step(vals)
```

```python
import functools
import math

import jax
import jax.numpy as jnp
from jax import lax
from jax.experimental import pallas as pl
from jax.experimental.pallas import tpu as pltpu

F32, BF16 = jnp.float32, jnp.bfloat16
NORM_EPS = 1e-6
HEAD_DIM = 128
GM_CHUNK = 128
DN_CHUNK = 64
CONV_WIDTH = 4
XA_HEADS = 4
LANES = 128
N_CHIPS = 4
N_DEV = 8
VMEM_LIMIT = 56 * 1024 * 1024
MESH_IDS = pl.DeviceIdType.MESH
ADAM_LR, ADAM_B1, ADAM_B2, ADAM_EPS, ADAM_WD, ADAM_STEP = 0.001, 0.9, 0.999, 1e-08, 0.01, 10

WEIGHTS = ['ffn1_norm_pre', 'ffn1_w_gate_up', 'ffn1_w_down', 'ffn1_norm_post', 'mix_norm_pre', 'w_in', 'conv_w',
           'a_log', 'dt_bias', 'sm_w', 'sm_b', 'sm_ln_g', 'sm_ln_b', 'dn_norm_w', 'w_out', 'mix_norm_post',
           'xa_norm_pre', 'mem_norm', 'w_xq', 'w_xkv', 'w_xo', 'xa_norm_post', 'ffn2_norm_pre', 'ffn2_w_gate_up',
           'ffn2_w_down', 'ffn2_norm_post']
BIG = {'ffn1_w_gate_up': 'col', 'ffn1_w_down': 'row', 'w_in': 'col', 'w_out': 'row', 'w_xq': 'row',
       'w_xkv': 'col', 'w_xo': 'row', 'ffn2_w_gate_up': 'col', 'ffn2_w_down': 'row'}
BIG_NAMES = list(BIG)
GROUPS = (('ffn1', ('ffn1_w_gate_up', 'ffn1_w_down')), ('mix', ('w_in', 'w_out')),
          ('xa', ('w_xq', 'w_xkv', 'w_xo')), ('ffn2', ('ffn2_w_gate_up', 'ffn2_w_down')))
SMALL_NAMES = [n for n in WEIGHTS if n not in BIG]


def _pcall(body, **kw):
    return pl.pallas_call(body, **kw)


def _params(sem=None):
    return pltpu.CompilerParams(dimension_semantics=sem, vmem_limit_bytes=VMEM_LIMIT)


def _as_list(after):
    if after is None:
        return []
    return list(after) if isinstance(after, (list, tuple)) else [after]


def _divtile(dim, cap, align=LANES):
    if dim <= cap:
        return dim
    t = (cap // align) * align
    while t > align and dim % t:
        t -= align
    assert dim % t == 0, (dim, cap)
    return t


_DN = {'nn': (((1,), (0,)), ((), ())), 'nt': (((1,), (1,)), ((), ())), 'tn': (((0,), (0,)), ((), ()))}


def _mm(a, b, mode, out_dtype, name):
    if mode == 'nn':
        (m, k), (k2, n) = a.shape, b.shape
    elif mode == 'nt':
        (m, k), (n, k2) = a.shape, b.shape
    else:
        (k, m), (k2, n) = a.shape, b.shape
    assert k == k2, (a.shape, b.shape, mode)
    tm, tn, tk = _divtile(m, 1024), _divtile(n, 1024), _divtile(k, 2048)
    nk = k // tk
    a_spec = {'nn': pl.BlockSpec((tm, tk), lambda i, j, kk: (i, kk)),
              'nt': pl.BlockSpec((tm, tk), lambda i, j, kk: (i, kk)),
              'tn': pl.BlockSpec((tk, tm), lambda i, j, kk: (kk, i))}[mode]
    b_spec = {'nn': pl.BlockSpec((tk, tn), lambda i, j, kk: (kk, j)),
              'nt': pl.BlockSpec((tn, tk), lambda i, j, kk: (j, kk)),
              'tn': pl.BlockSpec((tk, tn), lambda i, j, kk: (kk, j))}[mode]

    def dot(a_ref, b_ref):
        return lax.dot_general(a_ref[...].astype(BF16), b_ref[...].astype(BF16), _DN[mode], preferred_element_type=F32)

    def body_once(a_ref, b_ref, o_ref):
        o_ref[...] = dot(a_ref, b_ref).astype(o_ref.dtype)

    def body_steps(a_ref, b_ref, o_ref, acc):
        kk = pl.program_id(2)

        @pl.when(kk == 0)
        def _():
            acc[...] = jnp.zeros_like(acc)

        acc[...] += dot(a_ref, b_ref)

        @pl.when(kk == nk - 1)
        def _():
            o_ref[...] = acc[...].astype(o_ref.dtype)

    return _pcall(
        body_once if nk == 1 else body_steps, name=name, grid=(m // tm, n // tn, nk),
        in_specs=[a_spec, b_spec], out_specs=pl.BlockSpec((tm, tn), lambda i, j, kk: (i, j)),
        out_shape=jax.ShapeDtypeStruct((m, n), out_dtype),
        scratch_shapes=[] if nk == 1 else [pltpu.VMEM((tm, tn), F32)],
        compiler_params=_params(("parallel", "parallel", "arbitrary")),
    )(a, b)


def _rowcall(name, f, rows, params=(), row_outs=(), acc_outs=(), br=256, nrows=None, after=None):
    rows = [r if isinstance(r, tuple) else (r, 0) for r in rows]
    t = nrows if nrows is not None else rows[0][0].shape[0]
    br = min(br, t)
    assert t % br == 0, (name, t, br)
    n_in, n_ro = len(rows) + len(params), len(row_outs)
    extra = _as_list(after)

    def row_spec(arr, off):
        assert off % br == 0
        return pl.BlockSpec((br, arr.shape[1]), functools.partial(lambda i, o: (i + o, 0), o=off // br))

    def whole_spec(shape):
        return pl.BlockSpec(shape, functools.partial(lambda i, nd: (0,) * nd, nd=len(shape)))

    def body(*refs):
        res = f(*[r[...] for r in refs[:n_in]])
        outs = refs[n_in + len(extra):]
        for o_ref, val in zip(outs[:n_ro], res[:n_ro]):
            o_ref[...] = val.astype(o_ref.dtype)
        if acc_outs:
            @pl.when(pl.program_id(0) == 0)
            def _():
                for o_ref in outs[n_ro:]:
                    o_ref[...] = jnp.zeros_like(o_ref)

            for o_ref, val in zip(outs[n_ro:], res[n_ro:]):
                o_ref[...] += val.astype(F32)

    out = _pcall(
        body, name=name, grid=(t // br,),
        in_specs=[row_spec(a, o) for a, o in rows] + [whole_spec(p.shape) for p in params]
        + [pl.BlockSpec(memory_space=pl.ANY)] * len(extra),
        out_specs=[pl.BlockSpec((br, c), lambda i: (i, 0)) for c, _ in row_outs] + [whole_spec(s) for s in acc_outs],
        out_shape=[jax.ShapeDtypeStruct((t, c), dt) for c, dt in row_outs]
        + [jax.ShapeDtypeStruct(s, F32) for s in acc_outs],
        compiler_params=_params(("arbitrary",) if acc_outs else ("parallel",)),
    )(*[a for a, _ in rows], *params, *extra)
    return out


_DN_BATCH = {'nn': (((2,), (1,)), ((0,), (0,))), 'nt': (((2,), (2,)), ((0,), (0,))), 'tn': (((1,), (1,)), ((0,), (0,)))}


def _dims(a, dn):
    return _DN_BATCH[dn] if a.ndim == 3 else _DN[dn]


def _bdot_raw(a, b, dn):
    return lax.dot_general(a.astype(BF16), b.astype(BF16), _dims(a, dn), preferred_element_type=F32)


def _hdot_raw(a, b, dn):
    a_hi, b_hi = a.astype(BF16), b.astype(BF16)
    a_lo, b_lo = (a - a_hi.astype(F32)).astype(BF16), (b - b_hi.astype(F32)).astype(BF16)

    def dot(p, q):
        return lax.dot_general(p, q, _dims(a, dn), preferred_element_type=F32)

    return dot(a_hi, b_hi) + (dot(a_hi, b_lo) + dot(a_lo, b_hi))


def _with_vjp(raw):
    @functools.partial(jax.custom_vjp, nondiff_argnums=(2,))
    def dot(a, b, dn):
        return raw(a, b, dn)

    def fwd(a, b, dn):
        return raw(a, b, dn), (a, b)

    def bwd(dn, res, ct):
        a, b = res
        if dn == 'nn':
            da, db = raw(ct, b, 'nt'), raw(a, ct, 'tn')
        elif dn == 'nt':
            da, db = raw(ct, b, 'nn'), raw(ct, a, 'tn')
        else:
            da, db = raw(b, ct, 'nt'), raw(a, ct, 'nn')
        return da.astype(a.dtype), db.astype(b.dtype)

    dot.defvjp(fwd, bwd)
    return dot


_bdot, _hdot = _with_vjp(_bdot_raw), _with_vjp(_hdot_raw)


def _rms(x, g):
    return x * lax.rsqrt(jnp.mean(x * x, axis=-1, keepdims=True) + NORM_EPS) * g


def _sigmoid(x):
    return 1.0 / (1.0 + jnp.exp(-x))


def _silu(x):
    return x * _sigmoid(x)


def _gelu(x):
    return 0.5 * x * (1.0 + lax.erf(x * (2.0 ** -0.5)))


def _norm_fwd(x, g, name, after=None):
    return _rowcall(name, lambda x_, g_: (_rms(x_, g_),), [x], [g], [(x.shape[1], BF16)], after=after)[0]


def _resnorm_fwd(x, f, g, alpha, name):
    return _rowcall(name, lambda x_, f_, g_: (x_ + alpha * _rms(f_, g_),), [x, f], [g], [(x.shape[1], F32)])[0]


def _resnorm_bwd(f, dx, g, alpha, name, after=None):
    def fn(f_, dx_, g_):
        _, vjp = jax.vjp(lambda a, b: alpha * _rms(a, b), f_, g_)
        return vjp(dx_)

    return _rowcall(name, fn, [f, dx], [g], [(f.shape[1], BF16)], [g.shape], after=after)


def _norm_bwd(x, dh, dx_out, g, name, after=None):
    def fn(x_, dh_, dxo_, g_):
        _, vjp = jax.vjp(_rms, x_, g_)
        dx, dg = vjp(dh_)
        return dxo_ + dx, dg

    return _rowcall(name, fn, [x, dh, dx_out], [g], [(x.shape[1], F32)], [g.shape], after=after)


def _norm_bwd_gain(x, dh, g, name):
    def fn(x_, dh_, g_):
        _, vjp = jax.vjp(lambda b: _rms(x_, b), g_)
        return vjp(dh_)

    return _rowcall(name, fn, [x, dh], [g], [], [g.shape])[0]


def _swiglu_fwd(gu, name):
    ff = gu.shape[1] // 2
    return _rowcall(name, lambda gu_: (_silu(gu_[:, :ff]) * gu_[:, ff:],), [gu], [], [(ff, BF16)], br=64)[0]


def _swiglu_bwd(gu, da, name):
    ff = gu.shape[1] // 2

    def fn(gu_, da_):
        gate, up = gu_[:, :ff], gu_[:, ff:]
        s = _sigmoid(gate)
        dgate = da_ * up * (s * (1.0 + gate * (1.0 - s)))
        dup = da_ * (gate * s)
        return (jnp.concatenate([dgate, dup], axis=1),)

    return _rowcall(name, fn, [gu, da], [], [(2 * ff, BF16)], br=64)[0]


def _sgu_norm(v, lg, lb):
    vf = _gelu(v)
    mu = jnp.mean(vf, axis=-1, keepdims=True)
    var = jnp.mean(jnp.square(vf - mu), axis=-1, keepdims=True)
    return (vf - mu) * lax.rsqrt(var + NORM_EPS) * lg + lb


def _sgu_head(dot, u_h, vn_h, w_h, b_h):
    r = lax.broadcasted_iota(jnp.int32, w_h.shape, 0)
    c = lax.broadcasted_iota(jnp.int32, w_h.shape, 1)
    mixed = dot(jnp.where(r >= c, w_h, 0.0), vn_h, 'nn') + b_h
    return _gelu(u_h) * mixed


def _heads(width):
    return [slice(h * HEAD_DIM, (h + 1) * HEAD_DIM) for h in range(width // HEAD_DIM)]


def _sgu_fwd(u, v, w, bcol, lg, lb, name):
    def fn(u_, v_, w_, b_, lg_, lb_):
        vn = _sgu_norm(v_, lg_, lb_)
        return (jnp.concatenate([_sgu_head(_bdot_raw, u_[:, s], vn[:, s], w_[h], b_[h])
                                 for h, s in enumerate(_heads(u_.shape[1]))], axis=1),)

    return _rowcall(name, fn, [u, v], [w, bcol, lg, lb], [(u.shape[1], BF16)], br=GM_CHUNK)[0]


def _sgu_bwd(u, v, w, bcol, lg, lb, dy, name):
    def fn(u_, v_, dy_, w_, b_, lg_, lb_):
        vn, vjp_norm = jax.vjp(_sgu_norm, v_, lg_, lb_)
        du, dvn, dw, db = [], [], [], []
        for h, s in enumerate(_heads(u_.shape[1])):
            _, vjp = jax.vjp(functools.partial(_sgu_head, _bdot), u_[:, s], vn[:, s], w_[h], b_[h])
            a, b, c, d = vjp(dy_[:, s])
            du.append(a), dvn.append(b), dw.append(c[None]), db.append(d[None])
        dv, dlg, dlb = vjp_norm(jnp.concatenate(dvn, axis=1))
        return (jnp.concatenate(du, axis=1), dv, jnp.concatenate(dw, axis=0), jnp.concatenate(db, axis=0), dlg, dlb)

    wd = u.shape[1]
    return _rowcall(name, fn, [u, v, dy], [w, bcol, lg, lb], [(wd, F32), (wd, F32)],
                    [w.shape, bcol.shape, lg.shape, lb.shape], br=GM_CHUNK)


def _shift_down(x, s):
    if s == 0:
        return x
    rows = lax.broadcasted_iota(jnp.int32, x.shape, 0)
    return jnp.where(rows >= s, pltpu.roll(x, s, 0), 0.0)


def _shift_up(x, s):
    if s == 0:
        return x
    t = x.shape[0]
    rows = lax.broadcasted_iota(jnp.int32, x.shape, 0)
    return jnp.where(rows < t - s, pltpu.roll(x, t - s, 0), 0.0)


def _conv_pre(x, taps):
    acc = None
    for i in range(CONV_WIDTH):
        term = _shift_down(x, CONV_WIDTH - 1 - i) * taps[i]
        acc = term if acc is None else acc + term
    return acc


def _taps(w_ref):
    return [w_ref[i:i + 1, :] for i in range(CONV_WIDTH)]


def _conv_fwd(x, w, name):
    t, ch = x.shape
    cb = _divtile(ch, 512)

    def body(x_ref, w_ref, o_ref):
        o_ref[...] = _silu(_conv_pre(x_ref[...], _taps(w_ref)))

    return _pcall(body, name=name, grid=(ch // cb,),
                  in_specs=[pl.BlockSpec((t, cb), lambda j: (0, j)), pl.BlockSpec((CONV_WIDTH, cb), lambda j: (0, j))],
                  out_specs=pl.BlockSpec((t, cb), lambda j: (0, j)),
                  out_shape=jax.ShapeDtypeStruct((t, ch), F32), compiler_params=_params(("parallel",)))(x, w)


def _conv_bwd(x, w, dout, col_off, name):
    t, ch = dout.shape
    cb = _divtile(ch, 256)
    assert col_off % cb == 0
    off = col_off // cb

    def body(x_ref, w_ref, do_ref, dx_ref, dw_ref):
        xv, taps = x_ref[...], _taps(w_ref)
        pre = _conv_pre(xv, taps)
        s = _sigmoid(pre)
        dpre = do_ref[...] * (s * (1.0 + pre * (1.0 - s)))
        dx = None
        for i in range(CONV_WIDTH):
            sh = CONV_WIDTH - 1 - i
            term = _shift_up(dpre, sh) * taps[i]
            dx = term if dx is None else dx + term
            dw_ref[i:i + 1, :] = jnp.sum(dpre * _shift_down(xv, sh), axis=0, keepdims=True)
        dx_ref[...] = dx

    return _pcall(body, name=name, grid=(ch // cb,),
                  in_specs=[pl.BlockSpec((t, cb), lambda j: (0, j + off)),
                            pl.BlockSpec((CONV_WIDTH, cb), lambda j: (0, j + off)),
                            pl.BlockSpec((t, cb), lambda j: (0, j))],
                  out_specs=[pl.BlockSpec((t, cb), lambda j: (0, j)), pl.BlockSpec((CONV_WIDTH, cb), lambda j: (0, j))],
                  out_shape=[jax.ShapeDtypeStruct((t, ch), F32), jax.ShapeDtypeStruct((CONV_WIDTH, ch), F32)],
                  compiler_params=_params(("parallel",)))(x, w, dout)


GATE_BLOCK = 256


def _gates(dot, b_raw, a_raw, a_log, dt_bias):
    blk = b_raw.shape[1]
    z = a_raw + dt_bias
    softplus = jnp.maximum(z, 0.0) + jnp.log(1.0 + jnp.exp(-jnp.abs(z)))
    g = -jnp.exp(a_log) * softplus
    ri = lax.broadcasted_iota(jnp.int32, (blk, blk), 0)
    ci = lax.broadcasted_iota(jnp.int32, (blk, blk), 1)
    upto = ((ri <= ci) & (ri // DN_CHUNK == ci // DN_CHUNK)).astype(F32)
    return _sigmoid(b_raw), dot(g, upto, 'nn')


def _gate_blocks(t):
    blk = min(GATE_BLOCK, t)
    return [slice(i, i + blk) for i in range(0, t, blk)]


def _whole_call(name, f, ins, out_shapes):
    n_in = len(ins)

    def body(*refs):
        for o_ref, val in zip(refs[n_in:], f(*[r[...] for r in refs[:n_in]])):
            o_ref[...] = val

    vmem = pl.BlockSpec(memory_space=pltpu.VMEM)
    return _pcall(body, name=name, in_specs=[vmem] * n_in, out_specs=[vmem] * len(out_shapes),
                  out_shape=[jax.ShapeDtypeStruct(s, F32) for s in out_shapes],
                  compiler_params=pltpu.CompilerParams(vmem_limit_bytes=VMEM_LIMIT))(*ins)


def _gates_fwd(b_raw, a_raw, a_log, dt_bias, name):
    def fn(b_, a_, al_, dt_):
        parts = [_gates(_hdot_raw, b_[:, s], a_[:, s], al_, dt_) for s in _gate_blocks(b_.shape[1])]
        return [jnp.concatenate(p, axis=1) for p in zip(*parts)]

    return _whole_call(name, fn, [b_raw, a_raw, a_log, dt_bias], [b_raw.shape, b_raw.shape])


def _gates_bwd(b_raw, a_raw, a_log, dt_bias, dbeta, dgcum, name):
    def fn(b_, a_, al_, dt_, dbeta_, dgcum_):
        db, da, dal, ddt = [], [], None, None
        for s in _gate_blocks(b_.shape[1]):
            _, vjp = jax.vjp(functools.partial(_gates, _hdot), b_[:, s], a_[:, s], al_, dt_)
            p, q, r, u = vjp((dbeta_[:, s], dgcum_[:, s]))
            db.append(p), da.append(q)
            dal, ddt = (r, u) if dal is None else (dal + r, ddt + u)
        return jnp.concatenate(db, axis=1), jnp.concatenate(da, axis=1), dal, ddt

    return _whole_call(name, fn, [b_raw, a_raw, a_log, dt_bias, dbeta, dgcum],
                       [b_raw.shape, a_raw.shape, a_log.shape, dt_bias.shape])


def _unit_lower_inverse_raw(a):
    c = a.shape[-1]
    eye = (lax.broadcasted_iota(jnp.int32, (c, c), 0) == lax.broadcasted_iota(jnp.int32, (c, c), 1)).astype(F32)
    inv, p = eye - a, _hdot_raw(a, a, 'nn')
    n_fac = int(math.log2(c)) - 1
    for it in range(n_fac):
        inv = inv + _hdot_raw(inv, p, 'nn')
        if it < n_fac - 1:
            p = _hdot_raw(p, p, 'nn')
    return inv


@jax.custom_vjp
def _unit_lower_inverse(a):
    return _unit_lower_inverse_raw(a)


def _unit_lower_inverse_fwd(a):
    inv = _unit_lower_inverse_raw(a)
    return inv, inv


def _unit_lower_inverse_bwd(inv, ct):
    return (-_hdot_raw(_hdot_raw(inv, ct, 'tn'), inv, 'nt'),)


_unit_lower_inverse.defvjp(_unit_lower_inverse_fwd, _unit_lower_inverse_bwd)


def _halves_raw(x):
    return x[..., :x.shape[-1] // 2], x[..., x.shape[-1] // 2:]


@jax.custom_vjp
def _halves(x):
    return _halves_raw(x)


_halves.defvjp(lambda x: (_halves_raw(x), None), lambda _, ct: (jnp.concatenate(ct, axis=-1),))

_DELTA_OPS = (_bdot_raw, _hdot_raw, _unit_lower_inverse_raw, _halves_raw)
_DELTA_OPS_VJP = (_bdot, _hdot, _unit_lower_inverse, _halves)


def _delta_chunk(ops, state, q, k, v, gc, gr, bc):
    bd, hd, inverse, halves = ops
    c, d = q.shape[-2:]
    ri = lax.broadcasted_iota(jnp.int32, (c, c), 0)
    ci = lax.broadcasted_iota(jnp.int32, (c, c), 1)
    causal, strict = ri >= ci, ri > ci
    qn = q * lax.rsqrt(jnp.sum(q * q, axis=-1, keepdims=True) + NORM_EPS) * (d ** -0.5)
    kn = k * lax.rsqrt(jnp.sum(k * k, axis=-1, keepdims=True) + NORM_EPS)
    gcum = jnp.broadcast_to(gc, q.shape)
    decay = jnp.where(causal, jnp.exp(jnp.where(causal, gc - gr, 0.0)), 0.0)
    bb = jnp.broadcast_to(bc, q.shape)
    k_beta = kn * bb
    inv = inverse(jnp.where(strict, bd(k_beta, kn, 'nt') * decay, 0.0))
    uw = hd(inv, jnp.concatenate([v * bb, k_beta * jnp.exp(gcum)], axis=-1), 'nn')
    u, w = halves(uw)
    qk = jnp.where(causal, bd(qn, kn, 'nt') * decay, 0.0)
    v_new = u - bd(w, state, 'nn')
    o = bd(qn * jnp.exp(gcum), state, 'nn') + bd(qk, v_new, 'nn')
    last = lax.broadcasted_iota(jnp.int32, (c, d), 0) == c - 1
    g_last = jnp.sum(jnp.where(last, gcum, 0.0), axis=-2, keepdims=True)
    k_dec = kn * jnp.exp(g_last - gcum)
    return state * jnp.exp(g_last) + bd(k_dec, v_new, 'tn'), o


DN_HEADS_PER_STEP = 8


def _by_head(ref):
    return jnp.stack([ref[:, s] for s in _heads(ref.shape[1])])


def _delta_specs(nh, nc, rev):
    c, d = DN_CHUNK, HEAD_DIM
    hb = min(DN_HEADS_PER_STEP, nh)
    ng = nh // hb
    ch = (lambda n: nc - 1 - n) if rev else (lambda n: n)
    qkv = [pl.BlockSpec((c, hb * d), functools.partial(lambda h, n, o: (ch(n), o * ng + h), o=o)) for o in range(3)]
    col = pl.BlockSpec((hb, c, 1), lambda h, n: (h, ch(n), 0))
    row = pl.BlockSpec((hb, None, 1, c), lambda h, n: (h, ch(n), 0, 0))
    st = pl.BlockSpec((hb, None, d, d), lambda h, n: (h, ch(n), 0, 0))
    tok = pl.BlockSpec((c, hb * d), lambda h, n: (ch(n), h))
    return hb, ng, qkv, col, row, st, tok


def _delta_fwd(qkv, gcol, grow, bcol, name):
    t = qkv.shape[0]
    nh, nc, d = gcol.shape[0], t // DN_CHUNK, HEAD_DIM
    hb, ng, qkv_s, col, row, st, tok = _delta_specs(nh, nc, False)

    def body(q_ref, k_ref, v_ref, gc_ref, gr_ref, bc_ref, o_ref, st_ref, state):
        @pl.when(pl.program_id(1) == 0)
        def _():
            state[...] = jnp.zeros_like(state)

        s0 = state[...]
        st_ref[...] = s0
        s1, o = _delta_chunk(_DELTA_OPS, s0, _by_head(q_ref), _by_head(k_ref), _by_head(v_ref), gc_ref[...],
                             gr_ref[...], bc_ref[...])
        for j, s in enumerate(_heads(hb * d)):
            o_ref[:, s] = o[j]
        state[...] = s1

    return _pcall(body, name=name, grid=(ng, nc), in_specs=qkv_s + [col, row, col], out_specs=[tok, st],
                  out_shape=[jax.ShapeDtypeStruct((t, nh * d), F32), jax.ShapeDtypeStruct((nh, nc, d, d), F32)],
                  scratch_shapes=[pltpu.VMEM((hb, d, d), F32)],
                  compiler_params=_params(("parallel", "arbitrary")))(qkv, qkv, qkv, gcol, grow, bcol)


def _delta_bwd(qkv, gcol, grow, bcol, states, do, name):
    t = qkv.shape[0]
    nh, nc, d = gcol.shape[0], t // DN_CHUNK, HEAD_DIM
    hb, ng, qkv_s, col, row, st, tok = _delta_specs(nh, nc, True)

    def body(q_ref, k_ref, v_ref, gc_ref, gr_ref, bc_ref, st_ref, do_ref,
             dq_ref, dk_ref, dv_ref, dgc_ref, dgr_ref, dbc_ref, dstate):
        @pl.when(pl.program_id(1) == 0)
        def _():
            dstate[...] = jnp.zeros_like(dstate)

        _, vjp = jax.vjp(functools.partial(_delta_chunk, _DELTA_OPS_VJP), st_ref[...], _by_head(q_ref), _by_head(k_ref),
                         _by_head(v_ref), gc_ref[...], gr_ref[...], bc_ref[...])
        ds, dq, dk, dv, dgc, dgr, dbc = vjp((dstate[...], _by_head(do_ref)))
        for j, s in enumerate(_heads(hb * d)):
            dq_ref[:, s], dk_ref[:, s], dv_ref[:, s] = dq[j], dk[j], dv[j]
        dgc_ref[...], dgr_ref[...], dbc_ref[...] = dgc, dgr, dbc
        dstate[...] = ds

    tok_out = jax.ShapeDtypeStruct((t, nh * d), F32)
    return _pcall(body, name=name, grid=(ng, nc), in_specs=qkv_s + [col, row, col, st, tok],
                  out_specs=[tok, tok, tok, col, row, col],
                  out_shape=[tok_out, tok_out, tok_out, jax.ShapeDtypeStruct(gcol.shape, F32),
                             jax.ShapeDtypeStruct(grow.shape, F32), jax.ShapeDtypeStruct(bcol.shape, F32)],
                  scratch_shapes=[pltpu.VMEM((hb, d, d), F32)],
                  compiler_params=_params(("parallel", "arbitrary")))(qkv, qkv, qkv, gcol, grow, bcol, states, do)


def _outgate_head(o_h, z_h, w):
    return _rms(o_h, w) * _silu(z_h)


def _outgate_fwd(o, z, w, name):
    def fn(o_, z_, w_):
        return (jnp.concatenate([_outgate_head(o_[:, s], z_[:, s], w_) for s in _heads(o_.shape[1])], axis=1),)

    return _rowcall(name, fn, [o, z], [w], [(o.shape[1], BF16)])[0]


def _outgate_bwd(o, z, w, dy, name):
    def fn(o_, z_, dy_, w_):
        do, dz, dw = [], [], None
        for s in _heads(o_.shape[1]):
            _, vjp = jax.vjp(_outgate_head, o_[:, s], z_[:, s], w_)
            a, b, c = vjp(dy_[:, s])
            do.append(a), dz.append(b)
            dw = c if dw is None else dw + c
        return jnp.concatenate(do, axis=1), jnp.concatenate(dz, axis=1), dw

    wd = o.shape[1]
    return _rowcall(name, fn, [o, z, dy], [w], [(wd, F32), (wd, F32)], [w.shape])


def _attn_head(dot, q_h, k_h, v_h):
    s = dot(q_h, k_h, 'nt') * (q_h.shape[1] ** -0.5)
    e = jnp.exp(s - lax.stop_gradient(jnp.max(s, axis=-1, keepdims=True)))
    p = e / jnp.sum(e, axis=-1, keepdims=True)
    return dot(p, v_h, 'nn')


def _xa_slices(width):
    hd = width // XA_HEADS
    return [slice(h * hd, (h + 1) * hd) for h in range(XA_HEADS)]


def _attn_fwd(q, kv, name):
    wd = q.shape[1]

    def fn(q_, kv_):
        k_, v_ = kv_[:, :wd], kv_[:, wd:]
        return (jnp.concatenate([_attn_head(_bdot_raw, q_[:, s], k_[:, s], v_[:, s]) for s in _xa_slices(wd)],
                                axis=1),)

    return _rowcall(name, fn, [q], [kv], [(wd, BF16)])[0]


def _attn_bwd(q, kv, do, name):
    wd = q.shape[1]

    def fn(q_, do_, kv_):
        k_, v_ = kv_[:, :wd].astype(F32), kv_[:, wd:].astype(F32)
        dq, dk, dv = [], [], []
        for s in _xa_slices(wd):
            _, vjp = jax.vjp(functools.partial(_attn_head, _bdot), q_[:, s].astype(F32), k_[:, s], v_[:, s])
            a, b, c = vjp(do_[:, s])
            dq.append(a), dk.append(b), dv.append(c)
        return jnp.concatenate(dq, axis=1), jnp.concatenate(dk + dv, axis=1)

    return _rowcall(name, fn, [q, do], [kv], [(wd, BF16)], [kv.shape])


def _loss_call(y, target, name):
    d = y.shape[1]

    def fn(y_, t_):
        err = y_ - t_
        part = 0.5 * jnp.sum(jnp.sum(err * err, axis=1, keepdims=True), axis=0, keepdims=True) / d
        return err / d, jnp.broadcast_to(part, (1, LANES))

    return _rowcall(name, fn, [y, target], [], [(d, F32)], [(1, LANES)])


def _adamw_call(w, g, m, v, name):
    c = w.shape[1]

    def fn(w_, g_, m_, v_):
        g_ = g_[:, :c]
        m1 = ADAM_B1 * m_ + (1.0 - ADAM_B1) * g_
        v1 = ADAM_B2 * v_ + (1.0 - ADAM_B2) * jnp.square(g_)
        m_hat = m1 / (1.0 - ADAM_B1 ** ADAM_STEP)
        v_hat = v1 / (1.0 - ADAM_B2 ** ADAM_STEP)
        return g_, -ADAM_LR * (m_hat / (jnp.sqrt(v_hat) + ADAM_EPS) + ADAM_WD * w_), m1, v1

    return _rowcall(name, fn, [w, g, m, v], [], [(c, F32)] * 4, br=_divtile(w.shape[0], 128, 8))


STREAM_BLOCK_BYTES = 4 * 1024 * 1024


def _rows_for(cols, itemsize):
    return max(16, STREAM_BLOCK_BYTES // (cols * itemsize) // 16 * 16)


def _pair_add(grad, theirs, core, name):
    s, hr, cols = theirs.shape
    br = _divtile(hr, _rows_for(cols, 2), 16)
    nb = hr // br

    def body(c_ref, g_ref, t_ref, o_ref):
        o_ref[...] = (g_ref[...].astype(F32) + t_ref[...].astype(F32)).astype(o_ref.dtype)

    spec = pl.BlockSpec((None, br, cols), lambda j, i, c_ref: (j, i, 0))
    return _pcall(body, name=name, out_shape=jax.ShapeDtypeStruct(theirs.shape, BF16),
                  grid_spec=pltpu.PrefetchScalarGridSpec(
                      num_scalar_prefetch=1, grid=(s, nb),
                      in_specs=[pl.BlockSpec((None, br, cols), lambda j, i, c_ref: (j, c_ref[0] * nb + i, 0)), spec],
                      out_specs=spec),
                  compiler_params=_params(("parallel", "parallel")))(core, grad, theirs)


def _chip_sum(part, landed, kind, where, grad, layer, name):
    _, hr, cw = landed.shape
    br = _divtile(hr, _rows_for(cw, 4), 16)
    nb = hr // br

    def body(w_ref, p_ref, a_ref, b_ref, d_ref, g_ref, o_ref):
        o_ref[...] = ((p_ref[...].astype(F32) + a_ref[...].astype(F32)) + b_ref[...].astype(F32)) + d_ref[...].astype(F32)

    if kind == 'row':
        own = pl.BlockSpec((None, br, cw), lambda i, w_ref: (w_ref[0], i, 0))
    else:
        own = pl.BlockSpec((br, cw), lambda i, w_ref: (i, w_ref[0]))
    got = [pl.BlockSpec((None, br, cw), functools.partial(lambda i, w_ref, k: (k, i, 0), k=k)) for k in range(3)]
    return _pcall(body, name=name, out_shape=jax.ShapeDtypeStruct(grad.shape, F32),
                  grid_spec=pltpu.PrefetchScalarGridSpec(
                      num_scalar_prefetch=1, grid=(nb,), in_specs=[own] + got + [ANY_SPEC],
                      out_specs=pl.BlockSpec((None, br, cw), lambda i, w_ref: (layer, w_ref[1] * nb + i, 0))),
                  input_output_aliases={5: 0},
                  compiler_params=_params(("parallel",)))(where, part, landed, landed, landed, grad)


def _position():
    x, y, c = lax.axis_index("x"), lax.axis_index("y"), lax.axis_index("c")
    others = [(1 - x, y), (x, 1 - y), (1 - x, 1 - y)]
    return x, y, c, others


def _any_specs(n):
    return [pl.BlockSpec(memory_space=pl.ANY)] * n


def _ds(start, size):
    return pl.ds(pl.multiple_of(start, size), size)


HBM_SPEC = pl.BlockSpec(memory_space=pltpu.HBM)
SEM_SPEC = pl.BlockSpec(memory_space=pltpu.SEMAPHORE)
ANY_SPEC = pl.BlockSpec(memory_space=pl.ANY)
DATAFLOW = pltpu.SideEffectType.DATAFLOW_SIDE_EFFECTING


def _hbm(a):
    return pltpu.with_memory_space_constraint(a, pltpu.HBM)


def _full_shape(shard_shape, kind):
    r, cw = shard_shape
    return (N_CHIPS, r, cw) if kind == 'row' else (r, N_CHIPS * cw)


def _block_dims(full, kind):
    return (full.shape[1], full.shape[2]) if kind == 'row' else (full.shape[0], full.shape[1] // N_CHIPS)


def _region(full, kind, chip, half):
    if kind == 'all':
        return full.at[chip]
    r, cw = _block_dims(full, kind)
    if kind == 'row':
        return full.at[chip, _ds(half * (r // 2), r // 2), :]
    return full.at[_ds(half * (r // 2), r // 2), _ds(chip * cw, cw)]


def _place_block(full, shards, layer, kind, chip, name, after=None):
    _, r, cs = shards.shape
    cw = _block_dims(full, kind)[1]
    br = _divtile(r, 256, 16)

    def body(c_ref, s_ref, *rest):
        o_ref = rest[-1]
        if cs == cw:
            o_ref[...] = s_ref[...].astype(o_ref.dtype)
        else:
            o_ref[:, :cs] = s_ref[...].astype(o_ref.dtype)
            o_ref[:, cs:] = jnp.zeros((br, cw - cs), o_ref.dtype)

    if kind == 'row':
        out_spec = pl.BlockSpec((None, br, cw), lambda i, c_ref: (c_ref[0], i, 0))
    else:
        out_spec = pl.BlockSpec((br, cw), lambda i, c_ref: (i, c_ref[0]))
    extra = _as_list(after)
    return _pcall(body, name=name, out_shape=jax.ShapeDtypeStruct(full.shape, full.dtype),
                  grid_spec=pltpu.PrefetchScalarGridSpec(
                      num_scalar_prefetch=1, grid=(r // br,),
                      in_specs=[pl.BlockSpec((None, br, cs), lambda i, c_ref: (layer, i, 0)), ANY_SPEC]
                      + [ANY_SPEC] * len(extra), out_specs=out_spec),
                  input_output_aliases={2: 0}, compiler_params=_params(("parallel",)))(chip, shards, full, *extra)


def _split_copy_start(bufs, lands, plan, after, name):
    nb, nl = len(bufs), len(lands)
    extra = _as_list(after)

    def body(*refs):
        ins = refs[:nb + nl]
        send, recv = refs[nb + nl + len(extra)], refs[nb + nl + len(extra) + 1]
        for cp in plan(ins[:nb], ins[nb:], send, recv):
            cp.start()
        refs[-1][...] = jnp.zeros_like(refs[-1])

    n_copies = plan.n_copies
    out = _pcall(
        body, name=name,
        out_shape=(pltpu.SemaphoreType.DMA((n_copies,)), pltpu.SemaphoreType.DMA((n_copies,)),
                   *[pltpu.HBM(a.shape, a.dtype) for a in (*bufs, *lands)], jax.ShapeDtypeStruct((8, LANES), F32)),
        in_specs=[HBM_SPEC] * (nb + nl) + [ANY_SPEC] * len(extra),
        out_specs=(SEM_SPEC, SEM_SPEC, *[HBM_SPEC] * (nb + nl), pl.BlockSpec(memory_space=pltpu.VMEM)),
        input_output_aliases={i: 2 + i for i in range(nb + nl)},
        compiler_params=pltpu.CompilerParams(has_side_effects=DATAFLOW),
    )(*[_hbm(a) for a in (*bufs, *lands)], *extra)
    return out[0], out[1], list(out[2:2 + nb]), list(out[2 + nb:2 + nb + nl]), out[-1]


def _split_copy_wait(send, recv, bufs, lands, plan, after, name):
    nb, nl = len(bufs), len(lands)
    extra = _as_list(after)

    def body(*refs):
        ins = refs[:nb + nl]
        send_ref, recv_ref = refs[nb + nl], refs[nb + nl + 1]
        for cp in plan(ins[:nb], ins[nb:], send_ref, recv_ref):
            cp.wait_send()
            cp.wait_recv()

    out = _pcall(
        body, name=name, out_shape=tuple(pltpu.HBM(a.shape, a.dtype) for a in (*bufs, *lands)),
        in_specs=[HBM_SPEC] * (nb + nl) + [SEM_SPEC, SEM_SPEC] + [ANY_SPEC] * len(extra),
        out_specs=tuple([HBM_SPEC] * (nb + nl)), input_output_aliases={i: i for i in range(nb + nl)},
        compiler_params=pltpu.CompilerParams(has_side_effects=DATAFLOW),
    )(*bufs, *lands, send, recv, *extra)
    return list(out[:nb]), list(out[nb:])


def _gather_plan(kinds):
    def plan(bufs, lands, send, recv):
        x, y, c, others = _position()
        me = 2 * x + y
        return [pltpu.make_async_remote_copy(
            src_ref=_region(lands[w], kinds[w], me, c), dst_ref=_region(lands[w], kinds[w], me, c),
            send_sem=send.at[3 * w + k], recv_sem=recv.at[3 * w + k], device_id=(px, py, c), device_id_type=MESH_IDS)
            for w in range(len(lands)) for k, (px, py) in enumerate(others)]

    def wait_plan(bufs, lands, send, recv):
        x, y, c, others = _position()
        me = 2 * x + y
        return [pltpu.make_async_remote_copy(
            src_ref=_region(lands[w], kinds[w], me, c), dst_ref=_region(lands[w], kinds[w], 2 * px + py, c),
            send_sem=send.at[3 * w + k], recv_sem=recv.at[3 * w + k], device_id=(px, py, c), device_id_type=MESH_IDS)
            for w in range(len(lands)) for k, (px, py) in enumerate(others)]

    plan.n_copies = wait_plan.n_copies = 3 * len(kinds)
    return plan, wait_plan


def _gather_forward(fulls, kinds, name):
    n = len(fulls)

    def body(*refs):
        dst = refs[n:2 * n]
        send, recv = refs[2 * n:]
        x, y, c, others = _position()
        sib = (x, y, 1 - c)

        def copy(w, k, chip, half):
            reg = _region(dst[w], kinds[w], chip, half)
            return pltpu.make_async_remote_copy(src_ref=reg, dst_ref=reg, send_sem=send.at[3 * w + k],
                                                recv_sem=recv.at[3 * w + k], device_id=sib, device_id_type=MESH_IDS)

        give = [copy(w, k, 2 * px + py, c) for w in range(n) for k, (px, py) in enumerate(others)]
        for cp in give:
            cp.start()
        for w in range(n):
            for k, (px, py) in enumerate(others):
                copy(w, k, 2 * px + py, 1 - c).wait_recv()
        for cp in give:
            cp.wait_send()

    return _pcall(body, name=name, in_specs=_any_specs(n), out_specs=_any_specs(n),
                  out_shape=[jax.ShapeDtypeStruct(f.shape, f.dtype) for f in fulls],
                  input_output_aliases={i: i for i in range(n)},
                  scratch_shapes=[pltpu.SemaphoreType.DMA((3 * n,)), pltpu.SemaphoreType.DMA((3 * n,))])(*fulls)


def _grad_half(ref, kind, half):
    if kind == 'row':
        hr = ref.shape[1] // 2
        return ref.at[:, _ds(half * hr, hr), :]
    hr = ref.shape[0] // 2
    return ref.at[_ds(half * hr, hr), :]


def _half_shape(g, kind):
    return (g.shape[0], g.shape[1] // 2, g.shape[2]) if kind == 'row' else (g.shape[0] // 2, g.shape[1])


def _pair_plan(kinds):
    def plan(bufs, lands, send, recv):
        x, y, c, _ = _position()
        return [pltpu.make_async_remote_copy(src_ref=_grad_half(bufs[w], kinds[w], 1 - c), dst_ref=lands[w],
                                             send_sem=send.at[w], recv_sem=recv.at[w], device_id=(x, y, 1 - c),
                                             device_id_type=MESH_IDS) for w in range(len(bufs))]

    plan.n_copies = len(kinds)
    return plan


def _pair_exchange(grads, kinds, name):
    n = len(grads)

    def body(*refs):
        src, theirs_ref = refs[:n], refs[n:2 * n]
        send, recv = refs[2 * n:]
        x, y, c, _ = _position()
        sib = (x, y, 1 - c)

        def half(w, h):
            if kinds[w] == 'row':
                hr = src[w].shape[1] // 2
                return src[w].at[:, _ds(h * hr, hr), :]
            hr = src[w].shape[0] // 2
            return src[w].at[_ds(h * hr, hr), :]

        give = [pltpu.make_async_remote_copy(src_ref=half(w, 1 - c), dst_ref=theirs_ref[w], send_sem=send.at[w],
                                             recv_sem=recv.at[w], device_id=sib, device_id_type=MESH_IDS)
                for w in range(n)]
        for cp in give:
            cp.start()
        for cp in give:
            cp.wait()

    def half_shape(g, kind):
        return (g.shape[0], g.shape[1] // 2, g.shape[2]) if kind == 'row' else (g.shape[0] // 2, g.shape[1])

    return _pcall(body, name=name, in_specs=_any_specs(n), out_specs=_any_specs(n),
                  out_shape=[jax.ShapeDtypeStruct(half_shape(g, kd), g.dtype) for g, kd in zip(grads, kinds)],
                  scratch_shapes=[pltpu.SemaphoreType.DMA((n,)), pltpu.SemaphoreType.DMA((n,))])(*grads)


def _part_dims(part, kind):
    return (part.shape[1], part.shape[2]) if kind == 'row' else (part.shape[0], part.shape[1] // N_CHIPS)


def _chip_plan(kinds):
    def plan(bufs, lands, send, recv):
        x, y, c, others = _position()

        def slot(w, chip):
            if kinds[w] == 'row':
                return bufs[w].at[chip]
            cw = _part_dims(bufs[w], kinds[w])[1]
            return bufs[w].at[:, _ds(chip * cw, cw)]

        return [pltpu.make_async_remote_copy(src_ref=slot(w, 2 * px + py), dst_ref=lands[w].at[k],
                                             send_sem=send.at[3 * w + k], recv_sem=recv.at[3 * w + k],
                                             device_id=(px, py, c), device_id_type=MESH_IDS)
                for w in range(len(bufs)) for k, (px, py) in enumerate(others)]

    plan.n_copies = 3 * len(kinds)
    return plan


def _pair_gather_plan(n, n_layers):
    def copies(lands, send, recv, take):
        x, y, c, _ = _position()
        out = []
        for w in range(n):
            hr = lands[w].shape[1] // 2
            for l in range(n_layers):
                mine = lands[w].at[l, _ds(c * hr, hr), :]
                theirs = lands[w].at[l, _ds((1 - c) * hr, hr), :]
                out.append(pltpu.make_async_remote_copy(
                    src_ref=mine, dst_ref=theirs if take else mine, send_sem=send.at[w * n_layers + l],
                    recv_sem=recv.at[w * n_layers + l], device_id=(x, y, 1 - c), device_id_type=MESH_IDS))
        return out

    def plan(bufs, lands, send, recv):
        return copies(lands, send, recv, False)

    def wait_plan(bufs, lands, send, recv):
        return copies(lands, send, recv, True)

    plan.n_copies = wait_plan.n_copies = n * n_layers
    return plan, wait_plan


def _pair_gather(grads, name, after=None):
    n = len(grads)
    n_layers = grads[0].shape[0]
    extra = _as_list(after)

    def body(*refs):
        dst = refs[n + len(extra):2 * n + len(extra)]
        send, recv = refs[2 * n + len(extra):]
        x, y, c, _ = _position()
        sib = (x, y, 1 - c)

        def copy(w, l, half):
            hr = dst[w].shape[1] // 2
            rows = dst[w].at[l, _ds(half * hr, hr), :]
            return pltpu.make_async_remote_copy(src_ref=rows, dst_ref=rows, send_sem=send.at[w * n_layers + l],
                                                recv_sem=recv.at[w * n_layers + l], device_id=sib,
                                                device_id_type=MESH_IDS)

        give = [copy(w, l, c) for w in range(n) for l in range(n_layers)]
        for cp in give:
            cp.start()
        for w in range(n):
            for l in range(n_layers):
                copy(w, l, 1 - c).wait_recv()
        for cp in give:
            cp.wait_send()

    m = n * n_layers
    return _pcall(body, name=name, in_specs=_any_specs(n + len(extra)), out_specs=_any_specs(n),
                  out_shape=[jax.ShapeDtypeStruct(g.shape, g.dtype) for g in grads],
                  input_output_aliases={i: i for i in range(n)},
                  scratch_shapes=[pltpu.SemaphoreType.DMA((m,)), pltpu.SemaphoreType.DMA((m,))])(*grads, *extra)


def _all_sum(buf, name):
    rows = buf.shape[0]
    flips = [(fx, fy, fc) for fx in (0, 1) for fy in (0, 1) for fc in (0, 1)][1:]

    def body(x_ref, o_ref, land, send, recv):
        x, y, c, _ = _position()
        me = 4 * x + 2 * y + c

        def peer(f):
            return (1 - x if f[0] else x, 1 - y if f[1] else y, 1 - c if f[2] else c)

        give = [pltpu.make_async_remote_copy(src_ref=x_ref, dst_ref=land.at[me], send_sem=send.at[k],
                                             recv_sem=recv.at[k], device_id=peer(f), device_id_type=MESH_IDS)
                for k, f in enumerate(flips)]
        for cp in give:
            cp.start()
        land[me] = x_ref[...]
        for k, f in enumerate(flips):
            px, py, pc = peer(f)
            slot = land.at[4 * px + 2 * py + pc]
            pltpu.make_async_remote_copy(src_ref=slot, dst_ref=slot, send_sem=send.at[k], recv_sem=recv.at[k],
                                         device_id=peer(f), device_id_type=MESH_IDS).wait_recv()
        for cp in give:
            cp.wait_send()
        total = land[0]
        for dev in range(1, N_DEV):
            total = total + land[dev]
        o_ref[...] = total

    return _pcall(body, name=name, in_specs=[pl.BlockSpec(memory_space=pltpu.VMEM)],
                  out_specs=pl.BlockSpec(memory_space=pltpu.VMEM), out_shape=jax.ShapeDtypeStruct(buf.shape, F32),
                  scratch_shapes=[pltpu.VMEM((N_DEV, rows, LANES), F32), pltpu.SemaphoreType.DMA((N_DEV - 1,)),
                                  pltpu.SemaphoreType.DMA((N_DEV - 1,))],
                  compiler_params=pltpu.CompilerParams(vmem_limit_bytes=VMEM_LIMIT))(buf)


def _pack(arrays):
    flat = jnp.concatenate([a.reshape(-1).astype(F32) for a in arrays])
    pad = (-flat.shape[0]) % (8 * LANES)
    return jnp.pad(flat, (0, pad)).reshape(-1, LANES)


def _unpack(buf, like):
    flat, out, off = buf.reshape(-1), [], 0
    for a in like:
        out.append(flat[off:off + a.size].reshape(a.shape))
        off += a.size
    return out


def _row2(v):
    return v.reshape(1, -1)


def _ffn_fwd(x, g_pre, w_gu, w_d, g_post, tag, after=None):
    h = _norm_fwd(x, g_pre, tag + "_pre", after)
    gu = _mm(h, w_gu, 'nn', F32, tag + "_gu")
    a = _swiglu_fwd(gu, tag + "_act")
    f = _mm(a, w_d, 'nn', F32, tag + "_down")
    return _resnorm_fwd(x, f, g_post, 0.5, tag + "_post"), (x, h, gu, a, f)


def _ffn_bwd_weights(dx_out, saved, w_d, g_post, tag, after=None):
    x, h, gu, a, f = saved
    df, dg_post = _resnorm_bwd(f, dx_out, g_post, 0.5, tag + "_post_b", after)
    da = _mm(df, w_d, 'nt', F32, tag + "_down_bx")
    dw_d = _mm(a, df, 'tn', BF16, tag + "_down_bw")
    dgu = _swiglu_bwd(gu, da, tag + "_act_b")
    dw_gu = _mm(h, dgu, 'tn', BF16, tag + "_gu_bw")
    return dgu, dw_gu, dw_d, dg_post


def _ffn_bwd_input(dx_out, saved, dgu, g_pre, w_gu, tag, after=None):
    x = saved[0]
    dh = _mm(dgu, w_gu, 'nt', F32, tag + "_gu_bx")
    return _norm_bwd(x, dh, dx_out, g_pre, tag + "_pre_b", after)


def _ffn_bwd(dx_out, saved, g_pre, w_gu, w_d, g_post, tag, after=None):
    dgu, dw_gu, dw_d, dg_post = _ffn_bwd_weights(dx_out, saved, w_d, g_post, tag, after)
    dx, dg_pre = _ffn_bwd_input(dx_out, saved, dgu, g_pre, w_gu, tag)
    return dx, dg_pre, dw_gu, dw_d, dg_post


def _split_cols(pp, cs, cp, widths):
    proj = jnp.concatenate([pp[:, j * cp:j * cp + cs] for j in range(N_CHIPS)], axis=1)
    out, off = [], 0
    for wd in widths:
        out.append(proj[:, off:off + wd])
        off += wd
    return out


def _join_cols(pieces, cs, cp):
    proj = jnp.concatenate(pieces, axis=1)
    t = proj.shape[0]
    cols = []
    for j in range(N_CHIPS):
        cols += [proj[:, j * cs:(j + 1) * cs], jnp.zeros((t, cp - cs), proj.dtype)]
    return jnp.concatenate(cols, axis=1)


def _mix_fwd(x, p, w_in, w_out, conv_w, cs, cp, tag, after=None):
    t, d = x.shape
    half = d // 2
    nh = half // HEAD_DIM
    h = _norm_fwd(x, p['mix_norm_pre'], tag + "_pre", after)
    pp = _mm(h, w_in, 'nn', F32, tag + "_in")
    u_a, v_a, qkv_raw, z, b_raw, a_raw = _split_cols(pp, cs, cp, [half, half, 3 * half, half, nh, nh])
    y_a = _sgu_fwd(u_a, v_a, p['sm_w'], p['sm_b'], p['sm_ln_g'], p['sm_ln_b'], tag + "_sgu")
    qkv = _conv_fwd(qkv_raw, conv_w, tag + "_conv")
    b_raw, a_raw = b_raw.T, a_raw.T
    beta, gcum = _gates_fwd(b_raw, a_raw, p['a_log'].T, p['dt_bias'].T, tag + "_gates")
    gcol, bcol = gcum[:, :, None], beta[:, :, None]
    grow = gcum.reshape(nh, t // DN_CHUNK, 1, DN_CHUNK)
    o, states = _delta_fwd(qkv, gcol, grow, bcol, tag + "_delta")
    y_b = _outgate_fwd(o, z, p['dn_norm_w'], tag + "_gate")
    y = jnp.concatenate([y_a, y_b], axis=1)
    m = _mm(y, w_out, 'nn', F32, tag + "_out")
    x_out = _resnorm_fwd(x, m, p['mix_norm_post'], 1.0, tag + "_post")
    return x_out, (x, h, u_a, v_a, qkv_raw, z, b_raw, a_raw, qkv, gcol, grow, bcol, states, o, y, m)


def _mix_bwd(dx_out, saved, p, w_in, w_out, conv_w, cs, cp, tag, after=None):
    x, h, u_a, v_a, qkv_raw, z, b_raw, a_raw, qkv, gcol, grow, bcol, states, o, y, m = saved
    t, d = x.shape
    half = d // 2
    nh = half // HEAD_DIM
    gr = {}
    dm, gr['mix_norm_post'] = _resnorm_bwd(m, dx_out, p['mix_norm_post'], 1.0, tag + "_post_b", after)
    dy = _mm(dm, w_out, 'nt', F32, tag + "_out_bx")
    gr['w_out'] = _mm(y, dm, 'tn', BF16, tag + "_out_bw")
    dy_a, dy_b = dy[:, :half], dy[:, half:]
    do, dz, gr['dn_norm_w'] = _outgate_bwd(o, z, p['dn_norm_w'], dy_b, tag + "_gate_b")
    dq, dk, dv, dgcol, dgrow, dbcol = _delta_bwd(qkv, gcol, grow, bcol, states, do, tag + "_delta_b")
    dgcum = dgcol[:, :, 0] + dgrow.reshape(nh, t)
    db_raw, da_raw, gr['a_log'], gr['dt_bias'] = _gates_bwd(b_raw, a_raw, p['a_log'].T, p['dt_bias'].T,
                                                             dbcol[:, :, 0], dgcum, tag + "_gates_b")
    db_raw, da_raw = db_raw.T, da_raw.T
    thirds = [_conv_bwd(qkv_raw, conv_w, dpart, i * half, tag + "_conv_b") for i, dpart in enumerate((dq, dk, dv))]
    gr['conv_w'] = jnp.concatenate([dw for _, dw in thirds], axis=1)
    du_a, dv_a, gr['sm_w'], gr['sm_b'], gr['sm_ln_g'], gr['sm_ln_b'] = _sgu_bwd(
        u_a, v_a, p['sm_w'], p['sm_b'], p['sm_ln_g'], p['sm_ln_b'], dy_a, tag + "_sgu_b")
    dpp = _join_cols([du_a, dv_a, *[dx for dx, _ in thirds], dz, db_raw, da_raw], cs, cp).astype(BF16)
    dh = _mm(dpp, w_in, 'nt', F32, tag + "_in_bx")
    gr['w_in'] = _mm(h, dpp, 'tn', BF16, tag + "_in_bw")
    dx, gr['mix_norm_pre'] = _norm_bwd(x, dh, dx_out, p['mix_norm_pre'], tag + "_pre_b")
    return dx, gr


def _xa_fwd(x, mem, p, w_xq, w_xkv, w_xo, tag, after=None):
    h = _norm_fwd(x, p['xa_norm_pre'], tag + "_pre", after)
    mh = _norm_fwd(mem, p['mem_norm'], tag + "_mem")
    q = _mm(h, w_xq, 'nn', BF16, tag + "_q")
    kv = _mm(mh, w_xkv, 'nn', BF16, tag + "_kv")
    o = _attn_fwd(q, kv, tag + "_attn")
    c = _mm(o, w_xo, 'nn', F32, tag + "_o")
    return _resnorm_fwd(x, c, p['xa_norm_post'], 1.0, tag + "_post"), (x, h, mh, q, kv, o, c)


def _xa_bwd(dx_out, saved, mem, p, w_xq, w_xkv, w_xo, tag, after=None):
    x, h, mh, q, kv, o, c = saved
    gr = {}
    dc, gr['xa_norm_post'] = _resnorm_bwd(c, dx_out, p['xa_norm_post'], 1.0, tag + "_post_b", after)
    do = _mm(dc, w_xo, 'nt', F32, tag + "_o_bx")
    gr['w_xo'] = _mm(o, dc, 'tn', BF16, tag + "_o_bw")
    dq, dkv = _attn_bwd(q, kv, do, tag + "_attn_b")
    dkv = dkv.astype(BF16)
    dh = _mm(dq, w_xq, 'nt', F32, tag + "_q_bx")
    gr['w_xq'] = _mm(h, dq, 'tn', BF16, tag + "_q_bw")
    dmh = _mm(dkv, w_xkv, 'nt', F32, tag + "_kv_bx")
    gr['w_xkv'] = _mm(mh, dkv, 'tn', BF16, tag + "_kv_bw")
    gr['mem_norm'] = _norm_bwd_gain(mem, dmh, p['mem_norm'], tag + "_mem_b")
    dx, gr['xa_norm_pre'] = _norm_bwd(x, dh, dx_out, p['xa_norm_pre'], tag + "_pre_b")
    return dx, gr


def _step(inputs):
    x, mem, target = inputs['x'][0], inputs['mem'][0], inputs['loss_target'][0]
    n_layers = inputs['w_in'].shape[0]
    cx, cy, cc = lax.axis_index("x"), lax.axis_index("y"), lax.axis_index("c")
    chip = 2 * cx + cy
    cs = inputs['w_in'].shape[2]
    cp = -(-cs // LANES) * LANES
    kinds = [BIG[nm] for nm in BIG_NAMES]

    conv_local = inputs['conv_w']
    ccols = conv_local.shape[2]

    core_idx, chip_idx = cc.astype(jnp.int32).reshape(1), chip.astype(jnp.int32).reshape(1)
    where_idx = jnp.stack([chip, cc]).astype(jnp.int32)
    order = [(l, gi) for l in range(n_layers) for gi in range(len(GROUPS))]
    n_groups = len(order)

    def small(l):
        p = {nm: inputs[nm][l] for nm in SMALL_NAMES}
        for nm in SMALL_NAMES:
            if p[nm].ndim == 1:
                p[nm] = _row2(p[nm])
        p['sm_b'] = p['sm_b'][:, :, None]
        return p

    def place(k, after=None):
        l, gi = order[k]
        lands = []
        for nm in GROUPS[gi][1]:
            _, r, c_in = inputs[nm].shape
            shape = _full_shape((r, cp if nm == 'w_in' else c_in), BIG[nm])
            lands.append(_place_block(lax.empty(shape, BF16), inputs[nm], l, BIG[nm], chip_idx, "place_block", after))
        return lands

    placed = {}
    conv_group = 1

    def gather_start(k, after):
        names = list(GROUPS[order[k][1]][1])
        kds = [BIG[nm] for nm in names]
        if k not in placed:
            placed[k] = place(k)
        lands = list(placed[k])
        if k == conv_group:
            mine = conv_local.reshape(1, n_layers * CONV_WIDTH, ccols)
            lands.append(lax.dynamic_update_slice(jnp.zeros((N_CHIPS, *mine.shape[1:]), F32), mine, (chip, 0, 0)))
            names.append('conv_w'), kds.append('all')
        plan, wait_plan = _gather_plan(kds)
        send, recv, _, lands, token = _split_copy_start([], lands, plan, after, f"gather_start_{k}")
        return dict(send=send, recv=recv, lands=lands, token=token, kinds=kds, names=names, wait_plan=wait_plan)

    def gather_finish(k, st, after):
        _, lands = _split_copy_wait(st['send'], st['recv'], [], st['lands'], st['wait_plan'], after, f"gather_wait_{k}")
        got = dict(zip(st['names'], lands))
        halves = [nm for nm, kd in zip(st['names'], st['kinds']) if kd != 'all']
        passed = _gather_forward([got[nm] for nm in halves], [BIG[nm] for nm in halves],
                                 "gather_forward_" + GROUPS[order[k][1]][0])
        w = {nm: (g.reshape(-1, g.shape[2]) if BIG[nm] == 'row' else g) for nm, g in zip(halves, passed)}
        if 'conv_w' in got:
            w['conv_w'] = got['conv_w'].reshape(N_CHIPS, n_layers, CONV_WIDTH, ccols).transpose(1, 2, 0, 3).reshape(
                n_layers, CONV_WIDTH, N_CHIPS * ccols)
        return w

    def group_fwd(gi, x_, p, w, l, token):
        if gi == 0:
            return _ffn_fwd(x_, p['ffn1_norm_pre'], w['ffn1_w_gate_up'], w['ffn1_w_down'], p['ffn1_norm_post'], "ffn", token)
        if gi == 1:
            return _mix_fwd(x_, p, w['w_in'], w['w_out'], conv_full[l], cs, cp, "mix", token)
        if gi == 2:
            return _xa_fwd(x_, mem, p, w['w_xq'], w['w_xkv'], w['w_xo'], "xa", token)
        return _ffn_fwd(x_, p['ffn2_norm_pre'], w['ffn2_w_gate_up'], w['ffn2_w_down'], p['ffn2_norm_post'], "ffn", token)

    def group_bwd(gi, dx_, s, p, w, l, token):
        if gi in (0, 3):
            pre = 'ffn1' if gi == 0 else 'ffn2'
            dx_, g_pre, g_gu, g_d, g_post = _ffn_bwd(dx_, s, p[pre + '_norm_pre'], w[pre + '_w_gate_up'],
                                                     w[pre + '_w_down'], p[pre + '_norm_post'], "ffn", token)
            return dx_, {pre + '_norm_pre': g_pre, pre + '_w_gate_up': g_gu, pre + '_w_down': g_d,
                         pre + '_norm_post': g_post}
        if gi == 1:
            return _mix_bwd(dx_, s, p, w['w_in'], w['w_out'], conv_full[l], cs, cp, "mix", token)
        return _xa_bwd(dx_, s, mem, p, w['w_xq'], w['w_xkv'], w['w_xo'], "xa", token)

    started = {0: gather_start(0, None)}
    started[1] = gather_start(1, started[0]['token'])
    weights, saved = [], []
    for k in range(2, n_groups):
        placed[k] = place(k, started[1]['token'])
    head_work = [started[1]['token']] + [land for k in range(2, n_groups) for land in placed[k]]
    conv_full = None
    for k, (l, gi) in enumerate(order):
        w = gather_finish(k, started[k], x if k > 0 else head_work)
        if k == conv_group:
            conv_full = w.pop('conv_w')
        token = None
        if k + 2 < n_groups:
            started[k + 2] = gather_start(k + 2, next(iter(w.values())))
            token = started[k + 2]['token']
        weights.append(w)
        x, s = group_fwd(gi, x, small(l), w, l, token)
        saved.append(s)

    dx, loss_part = _loss_call(x, target, "loss")

    grads = [dict() for _ in range(n_layers)]
    big_grads = {nm: None for nm in BIG_NAMES}
    pair_gathers = {}

    def reduce_finish(pd, after):
        parts, lands = _split_copy_wait(pd['send'], pd['recv'], pd['parts'], pd['lands'], pd['plan'], after,
                                        f"chip_wait_{pd['k']}")
        for nm, kd, part, landed in zip(pd['names'], pd['kinds'], parts, lands):
            if big_grads[nm] is None:
                big_grads[nm] = lax.empty((n_layers, 2 * landed.shape[1], landed.shape[2]), F32)
            big_grads[nm] = _chip_sum(part, landed, kd, where_idx, big_grads[nm], pd['l'], "chip_sum")
        if pd['l'] == 0 and pd['k'] > 0:
            names = pd['names']
            plan, wait_plan = _pair_gather_plan(len(names), n_layers)
            send, recv, _, got, _ = _split_copy_start([], [big_grads[nm] for nm in names], plan, None,
                                                      f"pair_gather_start_{pd['k']}")
            pair_gathers[pd['k']] = dict(send=send, recv=recv, lands=got, wait_plan=wait_plan)

    def by_chip(gr, names, kds):
        return [gr[nm].reshape(N_CHIPS, -1, gr[nm].shape[1]) if kd == 'row' else gr[nm] for nm, kd in zip(names, kds)]

    def chip_start(k, names, kds, gs, theirs):
        parts = [_pair_add(g, t, core_idx, "pair_add") if kd == 'row'
                 else _pair_add(g[None], t[None], core_idx, "pair_add")[0] for g, t, kd in zip(gs, theirs, kds)]
        lands = [lax.empty((3, *_part_dims(part, kd)), BF16) for part, kd in zip(parts, kds)]
        plan = _chip_plan(kds)
        send, recv, parts, lands, token = _split_copy_start(parts, lands, plan, None, f"chip_start_{k}")
        return dict(send=send, recv=recv, parts=parts, lands=lands, token=token, plan=plan, names=names, kinds=kds,
                    k=k, l=order[k][0])

    pair_pending, chip_queue = None, []
    for k in reversed(range(n_groups)):
        l, gi = order[k]
        names = GROUPS[gi][1]
        kds = [BIG[nm] for nm in names]
        p, w = small(l), weights[k]
        tokens = [pd['token'] for pd in ([pair_pending] if pair_pending else []) + chip_queue] or None
        if k > 0:
            dx_next, gr = group_bwd(gi, dx, saved[k], p, w, l, tokens)
        else:
            dgu, g_gu, g_d, g_post = _ffn_bwd_weights(dx, saved[k], w['ffn1_w_down'], p['ffn1_norm_post'], "ffn", tokens)
            gr = {'ffn1_w_gate_up': g_gu, 'ffn1_w_down': g_d, 'ffn1_norm_post': g_post}
            dx_next = g_gu
        for pd in chip_queue:
            reduce_finish(pd, dx_next)
        chip_queue = []
        if pair_pending:
            pd = pair_pending
            gs_prev, theirs = _split_copy_wait(pd['send'], pd['recv'], pd['gs'], pd['lands'], pd['plan'], dx_next,
                                               f"pair_wait_{pd['k']}")
            chip_queue.append(chip_start(pd['k'], pd['names'], pd['kinds'], gs_prev, theirs))
            pair_pending = None
        gs = by_chip(gr, names, kds)
        if k > 1:
            plan = _pair_plan(kds)
            lands = [lax.empty(_half_shape(g, kd), BF16) for g, kd in zip(gs, kds)]
            send, recv, gs, lands, token = _split_copy_start(gs, lands, plan, None, f"pair_start_{k}")
            pair_pending = dict(send=send, recv=recv, gs=gs, lands=lands, token=token, plan=plan, names=names,
                                kinds=kds, k=k)
        else:
            started_now = chip_start(k, names, kds, gs, _pair_exchange(gs, kds, "pair_exchange_" + GROUPS[gi][0]))
            if k > 0:
                chip_queue.append(started_now)
        if k > 0:
            dx = dx_next
        else:
            pending = started_now
            dx, gr['ffn1_norm_pre'] = _ffn_bwd_input(dx, saved[k], dgu, p['ffn1_norm_pre'], w['ffn1_w_gate_up'], "ffn",
                                                     pending['token'])
        grads[l].update({nm: g for nm, g in gr.items() if nm not in BIG})
    for pd in chip_queue:
        reduce_finish(pd, dx)
    out = {}

    def two(a):
        return a.reshape(-1, a.shape[-1])

    def adamw(nm, g):
        res = _adamw_call(two(inputs[nm]), two(g), two(inputs['m_' + nm]), two(inputs['v_' + nm]), "adamw")
        out[nm] = tuple(a.reshape(inputs[nm].shape) for a in res)
        return res[1]

    def finish_weights(gi, after=None):
        names = GROUPS[gi][1]
        if gi in pair_gathers:
            st = pair_gathers[gi]
            _, got = _split_copy_wait(st['send'], st['recv'], [], st['lands'], st['wait_plan'], after,
                                      f"pair_gather_wait_{gi}")
        else:
            got = _pair_gather([big_grads[nm] for nm in names], "pair_gather_" + GROUPS[gi][0], after)
        return [adamw(nm, g) for nm, g in zip(names, got)]

    done = [d for gi in range(len(GROUPS) - 1, 0, -1) for d in finish_weights(gi, pending['token'])]

    small_grads = [jnp.stack([grads[l][nm].reshape(inputs[nm].shape[1:]) if nm != 'conv_w' else grads[l][nm]
                              for l in range(n_layers)]) for nm in SMALL_NAMES]
    summed = _unpack(_all_sum(_pack(small_grads + [loss_part]), "small_sum"), small_grads + [loss_part])
    loss = summed[-1][0, 0]
    small_g = dict(zip(SMALL_NAMES, summed[:-1]))
    small_g['conv_w'] = lax.dynamic_slice(small_g['conv_w'], (0, 0, chip * ccols), (n_layers, CONV_WIDTH, ccols))
    done += [adamw(nm, small_g[nm]) for nm in SMALL_NAMES]

    reduce_finish(pending, done)
    finish_weights(0)
    return (loss, dx[None], *[out[nm][0] for nm in WEIGHTS], *[out[nm][1] for nm in WEIGHTS],
            *[out[nm][2] for nm in WEIGHTS], *[out[nm][3] for nm in WEIGHTS])


def kernel(x, mem, ffn1_norm_pre, ffn1_w_gate_up, ffn1_w_down, ffn1_norm_post, mix_norm_pre, w_in, conv_w, a_log, dt_bias, sm_w, sm_b, sm_ln_g, sm_ln_b, dn_norm_w, w_out, mix_norm_post, xa_norm_pre, mem_norm, w_xq, w_xkv, w_xo, xa_norm_post, ffn2_norm_pre, ffn2_w_gate_up, ffn2_w_down, ffn2_norm_post, loss_target, m_ffn1_norm_pre, m_ffn1_w_gate_up, m_ffn1_w_down, m_ffn1_norm_post, m_mix_norm_pre, m_w_in, m_conv_w, m_a_log, m_dt_bias, m_sm_w, m_sm_b, m_sm_ln_g, m_sm_ln_b, m_dn_norm_w, m_w_out, m_mix_norm_post, m_xa_norm_pre, m_mem_norm, m_w_xq, m_w_xkv, m_w_xo, m_xa_norm_post, m_ffn2_norm_pre, m_ffn2_w_gate_up, m_ffn2_w_down, m_ffn2_norm_post, v_ffn1_norm_pre, v_ffn1_w_gate_up, v_ffn1_w_down, v_ffn1_norm_post, v_mix_norm_pre, v_w_in, v_conv_w, v_a_log, v_dt_bias, v_sm_w, v_sm_b, v_sm_ln_g, v_sm_ln_b, v_dn_norm_w, v_w_out, v_mix_norm_post, v_xa_norm_pre, v_mem_norm, v_w_xq, v_w_xkv, v_w_xo, v_xa_norm_post, v_ffn2_norm_pre, v_ffn2_w_gate_up, v_ffn2_w_down, v_ffn2_norm_post):
    vals = dict(locals())
    return _step(vals)
```

```python
import functools
import math

import jax
import jax.numpy as jnp
from jax import lax
from jax.experimental import pallas as pl
from jax.experimental.pallas import tpu as pltpu

F32, BF16 = jnp.float32, jnp.bfloat16
NORM_EPS = 1e-6
HEAD_DIM = 128
GM_CHUNK = 128
DN_CHUNK = 64
CONV_WIDTH = 4
XA_HEADS = 4
LANES = 128
MXU_WIDTH = 256
N_CHIPS = 4
N_DEV = 8
VMEM_LIMIT = 56 * 1024 * 1024
MESH_IDS = pl.DeviceIdType.MESH
ADAM_LR, ADAM_B1, ADAM_B2, ADAM_EPS, ADAM_WD, ADAM_STEP = 0.001, 0.9, 0.999, 1e-08, 0.01, 10

WEIGHTS = ['ffn1_norm_pre', 'ffn1_w_gate_up', 'ffn1_w_down', 'ffn1_norm_post', 'mix_norm_pre', 'w_in', 'conv_w',
           'a_log', 'dt_bias', 'sm_w', 'sm_b', 'sm_ln_g', 'sm_ln_b', 'dn_norm_w', 'w_out', 'mix_norm_post',
           'xa_norm_pre', 'mem_norm', 'w_xq', 'w_xkv', 'w_xo', 'xa_norm_post', 'ffn2_norm_pre', 'ffn2_w_gate_up',
           'ffn2_w_down', 'ffn2_norm_post']
BIG = {'ffn1_w_gate_up': 'col', 'ffn1_w_down': 'row', 'w_in': 'col', 'w_out': 'row', 'w_xq': 'row',
       'w_xkv': 'col', 'w_xo': 'row', 'ffn2_w_gate_up': 'col', 'ffn2_w_down': 'row'}
BIG_NAMES = list(BIG)
GROUPS = (('ffn1', ('ffn1_w_gate_up', 'ffn1_w_down')), ('mix', ('w_in', 'w_out')),
          ('xa', ('w_xq', 'w_xkv', 'w_xo')), ('ffn2', ('ffn2_w_gate_up', 'ffn2_w_down')))
SMALL_NAMES = [n for n in WEIGHTS if n not in BIG]


def _pcall(body, **kw):
    return pl.pallas_call(body, **kw)


def _params(sem=None):
    return pltpu.CompilerParams(dimension_semantics=sem, vmem_limit_bytes=VMEM_LIMIT)


def _as_list(after):
    if after is None:
        return []
    return list(after) if isinstance(after, (list, tuple)) else [after]


def _divtile(dim, cap, align=LANES):
    if dim <= cap:
        return dim
    t = (cap // align) * align
    while t > align and dim % t:
        t -= align
    assert dim % t == 0, (dim, cap)
    return t


_DN = {'nn': (((1,), (0,)), ((), ())), 'nt': (((1,), (1,)), ((), ())), 'tn': (((0,), (0,)), ((), ()))}


MM_VMEM_BUDGET = 44 * 1024 * 1024


def _mm_tiles(m, n, k, out_itemsize, acc=False):
    def tile(dim, cap):
        return _divtile(dim, cap, MXU_WIDTH) if dim % MXU_WIDTH == 0 else _divtile(dim, cap)

    tm = tile(m, 1024)
    for cap_n, cap_k in ((2048, 2048), (1024, 2048), (2048, 1024), (1024, 1024), (1024, 512), (512, 512)):
        tn, tk = tile(n, cap_n), _divtile(k, cap_k)
        need = 2 * 2 * (tm * tk + tk * tn) + 2 * tm * tn * out_itemsize + (tm * tn * 4 if tk < k or acc else 0)
        if need <= MM_VMEM_BUDGET:
            break
    return tm, tn, tk


def _mm(a, b, mode, out_dtype, name, halves=None):
    if halves == 'a':
        assert mode == 'nt'
        a_shape, b_shape = (a.shape[1], 2 * a.shape[2]), b.shape
    elif halves == 'b':
        assert mode == 'tn'
        a_shape, b_shape = a.shape, (b.shape[1], 2 * b.shape[2])
    else:
        a_shape, b_shape = a.shape, b.shape
    if mode == 'nn':
        (m, k), (k2, n) = a_shape, b_shape
    elif mode == 'nt':
        (m, k), (n, k2) = a_shape, b_shape
    else:
        (k, m), (k2, n) = a_shape, b_shape
    assert k == k2, (a.shape, b.shape, mode)
    if halves == 'a':
        tm, tn, tk = _mm_tiles(m, n, k // 2, jnp.dtype(out_dtype).itemsize, acc=True)
    elif halves == 'b':
        tm, tn, tk = _mm_tiles(m, n // 2, k, jnp.dtype(out_dtype).itemsize)
    else:
        tm, tn, tk = _mm_tiles(m, n, k, jnp.dtype(out_dtype).itemsize)
    nk = k // tk
    a_spec = {'nn': pl.BlockSpec((tm, tk), lambda i, j, kk: (i, kk)),
              'nt': pl.BlockSpec((tm, tk), lambda i, j, kk: (i, kk)),
              'tn': pl.BlockSpec((tk, tm), lambda i, j, kk: (kk, i))}[mode]
    b_spec = {'nn': pl.BlockSpec((tk, tn), lambda i, j, kk: (kk, j)),
              'nt': pl.BlockSpec((tn, tk), lambda i, j, kk: (j, kk)),
              'tn': pl.BlockSpec((tk, tn), lambda i, j, kk: (kk, j))}[mode]
    if halves == 'a':
        per = nk // 2
        a_spec = pl.BlockSpec((None, tm, tk), lambda i, j, kk: (kk // per, i, kk % per))
    elif halves == 'b':
        per = (n // tn) // 2
        b_spec = pl.BlockSpec((None, tk, tn), lambda i, j, kk: (j // per, kk, j % per))

    def dot(a_ref, b_ref):
        return lax.dot_general(a_ref[...].astype(BF16), b_ref[...].astype(BF16), _DN[mode], preferred_element_type=F32)

    def body_once(a_ref, b_ref, o_ref):
        o_ref[...] = dot(a_ref, b_ref).astype(o_ref.dtype)

    def body_steps(a_ref, b_ref, o_ref, acc):
        kk = pl.program_id(2)

        @pl.when(kk == 0)
        def _():
            acc[...] = jnp.zeros_like(acc)

        acc[...] += dot(a_ref, b_ref)

        @pl.when(kk == nk - 1)
        def _():
            o_ref[...] = acc[...].astype(o_ref.dtype)

    return _pcall(
        body_once if nk == 1 else body_steps, name=name, grid=(m // tm, n // tn, nk),
        in_specs=[a_spec, b_spec], out_specs=pl.BlockSpec((tm, tn), lambda i, j, kk: (i, j)),
        out_shape=jax.ShapeDtypeStruct((m, n), out_dtype),
        scratch_shapes=[] if nk == 1 else [pltpu.VMEM((tm, tn), F32)],
        compiler_params=_params(("parallel", "parallel", "arbitrary")),
    )(a, b)


def _rowcall(name, f, rows, params=(), row_outs=(), acc_outs=(), br=256, nrows=None, after=None):
    rows = [r if isinstance(r, tuple) else (r, 0) for r in rows]
    t = nrows if nrows is not None else rows[0][0].shape[0]
    br = min(br, t)
    assert t % br == 0, (name, t, br)
    n_in, n_ro = len(rows) + len(params), len(row_outs)
    extra = _as_list(after)

    def row_spec(arr, off):
        assert off % br == 0
        return pl.BlockSpec((br, arr.shape[1]), functools.partial(lambda i, o: (i + o, 0), o=off // br))

    def whole_spec(shape):
        return pl.BlockSpec(shape, functools.partial(lambda i, nd: (0,) * nd, nd=len(shape)))

    def body(*refs):
        res = f(*[r[...] for r in refs[:n_in]])
        outs = refs[n_in + len(extra):]
        for o_ref, val in zip(outs[:n_ro], res[:n_ro]):
            o_ref[...] = val.astype(o_ref.dtype)
        if acc_outs:
            @pl.when(pl.program_id(0) == 0)
            def _():
                for o_ref in outs[n_ro:]:
                    o_ref[...] = jnp.zeros_like(o_ref)

            for o_ref, val in zip(outs[n_ro:], res[n_ro:]):
                o_ref[...] += val.astype(F32)

    out = _pcall(
        body, name=name, grid=(t // br,),
        in_specs=[row_spec(a, o) for a, o in rows] + [whole_spec(p.shape) for p in params]
        + [pl.BlockSpec(memory_space=pl.ANY)] * len(extra),
        out_specs=[pl.BlockSpec((br, c), lambda i: (i, 0)) for c, _ in row_outs] + [whole_spec(s) for s in acc_outs],
        out_shape=[jax.ShapeDtypeStruct((t, c), dt) for c, dt in row_outs]
        + [jax.ShapeDtypeStruct(s, F32) for s in acc_outs],
        compiler_params=_params(("arbitrary",) if acc_outs else ("parallel",)),
    )(*[a for a, _ in rows], *params, *extra)
    return out


_DN_BATCH = {'nn': (((2,), (1,)), ((0,), (0,))), 'nt': (((2,), (2,)), ((0,), (0,))), 'tn': (((1,), (1,)), ((0,), (0,)))}


def _dims(a, dn):
    return _DN_BATCH[dn] if a.ndim == 3 else _DN[dn]


def _bdot_raw(a, b, dn):
    return lax.dot_general(a.astype(BF16), b.astype(BF16), _dims(a, dn), preferred_element_type=F32)


def _hdot_raw(a, b, dn):
    a_hi, b_hi = a.astype(BF16), b.astype(BF16)
    a_lo, b_lo = (a - a_hi.astype(F32)).astype(BF16), (b - b_hi.astype(F32)).astype(BF16)

    def dot(p, q):
        return lax.dot_general(p, q, _dims(a, dn), preferred_element_type=F32)

    return dot(a_hi, b_hi) + (dot(a_hi, b_lo) + dot(a_lo, b_hi))


def _with_vjp(raw):
    @functools.partial(jax.custom_vjp, nondiff_argnums=(2,))
    def dot(a, b, dn):
        return raw(a, b, dn)

    def fwd(a, b, dn):
        return raw(a, b, dn), (a, b)

    def bwd(dn, res, ct):
        a, b = res
        if dn == 'nn':
            da, db = raw(ct, b, 'nt'), raw(a, ct, 'tn')
        elif dn == 'nt':
            da, db = raw(ct, b, 'nn'), raw(ct, a, 'tn')
        else:
            da, db = raw(b, ct, 'nt'), raw(a, ct, 'nn')
        return da.astype(a.dtype), db.astype(b.dtype)

    dot.defvjp(fwd, bwd)
    return dot


_bdot, _hdot = _with_vjp(_bdot_raw), _with_vjp(_hdot_raw)


def _rms(x, g):
    return x * lax.rsqrt(jnp.mean(x * x, axis=-1, keepdims=True) + NORM_EPS) * g


def _sigmoid(x):
    return 1.0 / (1.0 + jnp.exp(-x))


def _silu(x):
    return x * _sigmoid(x)


def _gelu(x):
    return 0.5 * x * (1.0 + lax.erf(x * (2.0 ** -0.5)))


def _norm_fwd(x, g, name, after=None):
    return _rowcall(name, lambda x_, g_: (_rms(x_, g_),), [x], [g], [(x.shape[1], BF16)], after=after)[0]


def _resnorm_fwd(x, f, g, alpha, name):
    return _rowcall(name, lambda x_, f_, g_: (x_ + alpha * _rms(f_, g_),), [x, f], [g], [(x.shape[1], F32)])[0]


def _resnorm_bwd(f, dx, g, alpha, name, after=None):
    def fn(f_, dx_, g_):
        _, vjp = jax.vjp(lambda a, b: alpha * _rms(a, b), f_, g_)
        return vjp(dx_)

    return _rowcall(name, fn, [f, dx], [g], [(f.shape[1], BF16)], [g.shape], after=after)


def _norm_bwd(x, dh, dx_out, g, name, after=None):
    def fn(x_, dh_, dxo_, g_):
        _, vjp = jax.vjp(_rms, x_, g_)
        dx, dg = vjp(dh_)
        return dxo_ + dx, dg

    return _rowcall(name, fn, [x, dh, dx_out], [g], [(x.shape[1], F32)], [g.shape], after=after)


def _norm_bwd_gain(x, dh, g, name):
    def fn(x_, dh_, g_):
        _, vjp = jax.vjp(lambda b: _rms(x_, b), g_)
        return vjp(dh_)

    return _rowcall(name, fn, [x, dh], [g], [], [g.shape])[0]


def _ffn_up(h, w_gu, name):
    t, d = h.shape
    ff = w_gu.shape[1] // 2
    tm, tn = _divtile(t, 1024), _divtile(ff, 512, MXU_WIDTH)
    nj = ff // tn

    def body(h_ref, wg_ref, wu_ref, gu_ref, a_ref):
        hv = h_ref[...]
        gate = jnp.dot(hv, wg_ref[...], preferred_element_type=F32)
        up = jnp.dot(hv, wu_ref[...], preferred_element_type=F32)
        gu_ref[0], gu_ref[1] = gate, up
        a_ref[...] = (_silu(gate) * up).astype(a_ref.dtype)

    return _pcall(body, name=name, grid=(t // tm, nj),
                  in_specs=[pl.BlockSpec((tm, d), lambda i, j: (i, 0)), pl.BlockSpec((d, tn), lambda i, j: (0, j)),
                            pl.BlockSpec((d, tn), lambda i, j: (0, j + nj))],
                  out_specs=[pl.BlockSpec((2, tm, tn), lambda i, j: (0, i, j)), pl.BlockSpec((tm, tn), lambda i, j: (i, j))],
                  out_shape=[jax.ShapeDtypeStruct((2, t, ff), F32), jax.ShapeDtypeStruct((t, ff), BF16)],
                  compiler_params=_params(("parallel", "parallel")))(h, w_gu, w_gu)


def _ffn_down_bwd(df, w_d, gu, name):
    t, d = df.shape
    ff = w_d.shape[0]
    tm, tn = _divtile(t, 1024), _divtile(ff, 512, MXU_WIDTH)

    def body(df_ref, w_ref, gu_ref, o_ref):
        da = lax.dot_general(df_ref[...], w_ref[...], _DN['nt'], preferred_element_type=F32)
        gate, up = gu_ref[0], gu_ref[1]
        s = _sigmoid(gate)
        o_ref[0] = (da * up * (s * (1.0 + gate * (1.0 - s)))).astype(o_ref.dtype)
        o_ref[1] = (da * (gate * s)).astype(o_ref.dtype)

    blk = pl.BlockSpec((2, tm, tn), lambda i, j: (0, i, j))
    return _pcall(body, name=name, grid=(t // tm, ff // tn),
                  in_specs=[pl.BlockSpec((tm, d), lambda i, j: (i, 0)), pl.BlockSpec((tn, d), lambda i, j: (j, 0)), blk],
                  out_specs=blk, out_shape=jax.ShapeDtypeStruct((2, t, ff), BF16),
                  compiler_params=_params(("parallel", "parallel")))(df, w_d, gu)


def _sgu_norm(v, lg, lb):
    vf = _gelu(v)
    mu = jnp.mean(vf, axis=-1, keepdims=True)
    var = jnp.mean(jnp.square(vf - mu), axis=-1, keepdims=True)
    return (vf - mu) * lax.rsqrt(var + NORM_EPS) * lg + lb


def _sgu_head(dot, u_h, vn_h, w_h, b_h):
    r = lax.broadcasted_iota(jnp.int32, w_h.shape, 0)
    c = lax.broadcasted_iota(jnp.int32, w_h.shape, 1)
    mixed = dot(jnp.where(r >= c, w_h, 0.0), vn_h, 'nn') + b_h
    return _gelu(u_h) * mixed


def _heads(width):
    return [slice(h * HEAD_DIM, (h + 1) * HEAD_DIM) for h in range(width // HEAD_DIM)]


def _sgu_fwd(u, v, w, bcol, lg, lb, name):
    def fn(u_, v_, w_, b_, lg_, lb_):
        vn = _sgu_norm(v_, lg_, lb_)
        return (jnp.concatenate([_sgu_head(_bdot_raw, u_[:, s], vn[:, s], w_[h], b_[h])
                                 for h, s in enumerate(_heads(u_.shape[1]))], axis=1),)

    return _rowcall(name, fn, [u, v], [w, bcol, lg, lb], [(u.shape[1], BF16)], br=GM_CHUNK)[0]


def _sgu_bwd(u, v, w, bcol, lg, lb, dy, name):
    def fn(u_, v_, dy_, w_, b_, lg_, lb_):
        vn, vjp_norm = jax.vjp(_sgu_norm, v_, lg_, lb_)
        du, dvn, dw, db = [], [], [], []
        for h, s in enumerate(_heads(u_.shape[1])):
            _, vjp = jax.vjp(functools.partial(_sgu_head, _bdot), u_[:, s], vn[:, s], w_[h], b_[h])
            a, b, c, d = vjp(dy_[:, s])
            du.append(a), dvn.append(b), dw.append(c[None]), db.append(d[None])
        dv, dlg, dlb = vjp_norm(jnp.concatenate(dvn, axis=1))
        return (jnp.concatenate(du, axis=1), dv, jnp.concatenate(dw, axis=0), jnp.concatenate(db, axis=0), dlg, dlb)

    wd = u.shape[1]
    return _rowcall(name, fn, [u, v, dy], [w, bcol, lg, lb], [(wd, F32), (wd, F32)],
                    [w.shape, bcol.shape, lg.shape, lb.shape], br=GM_CHUNK)


def _shift_down(x, s):
    if s == 0:
        return x
    rows = lax.broadcasted_iota(jnp.int32, x.shape, 0)
    return jnp.where(rows >= s, pltpu.roll(x, s, 0), 0.0)


def _shift_up(x, s):
    if s == 0:
        return x
    t = x.shape[0]
    rows = lax.broadcasted_iota(jnp.int32, x.shape, 0)
    return jnp.where(rows < t - s, pltpu.roll(x, t - s, 0), 0.0)


def _conv_pre(x, taps):
    acc = None
    for i in range(CONV_WIDTH):
        term = _shift_down(x, CONV_WIDTH - 1 - i) * taps[i]
        acc = term if acc is None else acc + term
    return acc


def _taps(w_ref):
    return [w_ref[i:i + 1, :] for i in range(CONV_WIDTH)]


def _conv_fwd(x, w, name):
    t, ch = x.shape
    cb = _divtile(ch, 512)

    def body(x_ref, w_ref, o_ref):
        o_ref[...] = _silu(_conv_pre(x_ref[...], _taps(w_ref)))

    return _pcall(body, name=name, grid=(ch // cb,),
                  in_specs=[pl.BlockSpec((t, cb), lambda j: (0, j)), pl.BlockSpec((CONV_WIDTH, cb), lambda j: (0, j))],
                  out_specs=pl.BlockSpec((t, cb), lambda j: (0, j)),
                  out_shape=jax.ShapeDtypeStruct((t, ch), F32), compiler_params=_params(("parallel",)))(x, w)


def _conv_bwd(x, w, dout, col_off, name):
    t, ch = dout.shape
    cb = _divtile(ch, 256)
    assert col_off % cb == 0
    off = col_off // cb

    def body(x_ref, w_ref, do_ref, dx_ref, dw_ref):
        xv, taps = x_ref[...], _taps(w_ref)
        pre = _conv_pre(xv, taps)
        s = _sigmoid(pre)
        dpre = do_ref[...] * (s * (1.0 + pre * (1.0 - s)))
        dx = None
        for i in range(CONV_WIDTH):
            sh = CONV_WIDTH - 1 - i
            term = _shift_up(dpre, sh) * taps[i]
            dx = term if dx is None else dx + term
            dw_ref[i:i + 1, :] = jnp.sum(dpre * _shift_down(xv, sh), axis=0, keepdims=True)
        dx_ref[...] = dx

    return _pcall(body, name=name, grid=(ch // cb,),
                  in_specs=[pl.BlockSpec((t, cb), lambda j: (0, j + off)),
                            pl.BlockSpec((CONV_WIDTH, cb), lambda j: (0, j + off)),
                            pl.BlockSpec((t, cb), lambda j: (0, j))],
                  out_specs=[pl.BlockSpec((t, cb), lambda j: (0, j)), pl.BlockSpec((CONV_WIDTH, cb), lambda j: (0, j))],
                  out_shape=[jax.ShapeDtypeStruct((t, ch), F32), jax.ShapeDtypeStruct((CONV_WIDTH, ch), F32)],
                  compiler_params=_params(("parallel",)))(x, w, dout)


GATE_BLOCK = 256


def _gates(dot, b_raw, a_raw, a_log, dt_bias):
    blk = b_raw.shape[1]
    z = a_raw + dt_bias
    softplus = jnp.maximum(z, 0.0) + jnp.log(1.0 + jnp.exp(-jnp.abs(z)))
    g = -jnp.exp(a_log) * softplus
    ri = lax.broadcasted_iota(jnp.int32, (blk, blk), 0)
    ci = lax.broadcasted_iota(jnp.int32, (blk, blk), 1)
    upto = ((ri <= ci) & (ri // DN_CHUNK == ci // DN_CHUNK)).astype(F32)
    return _sigmoid(b_raw), dot(g, upto, 'nn')


def _gate_blocks(t):
    blk = min(GATE_BLOCK, t)
    return [slice(i, i + blk) for i in range(0, t, blk)]


def _whole_call(name, f, ins, out_shapes):
    n_in = len(ins)

    def body(*refs):
        for o_ref, val in zip(refs[n_in:], f(*[r[...] for r in refs[:n_in]])):
            o_ref[...] = val

    vmem = pl.BlockSpec(memory_space=pltpu.VMEM)
    return _pcall(body, name=name, in_specs=[vmem] * n_in, out_specs=[vmem] * len(out_shapes),
                  out_shape=[jax.ShapeDtypeStruct(s, F32) for s in out_shapes],
                  compiler_params=pltpu.CompilerParams(vmem_limit_bytes=VMEM_LIMIT))(*ins)


def _gates_fwd(b_raw, a_raw, a_log, dt_bias, name):
    def fn(b_, a_, al_, dt_):
        parts = [_gates(_hdot_raw, b_[:, s], a_[:, s], al_, dt_) for s in _gate_blocks(b_.shape[1])]
        return [jnp.concatenate(p, axis=1) for p in zip(*parts)]

    return _whole_call(name, fn, [b_raw, a_raw, a_log, dt_bias], [b_raw.shape, b_raw.shape])


def _gates_bwd(b_raw, a_raw, a_log, dt_bias, dbeta, dgcum, name):
    def fn(b_, a_, al_, dt_, dbeta_, dgcum_):
        db, da, dal, ddt = [], [], None, None
        for s in _gate_blocks(b_.shape[1]):
            _, vjp = jax.vjp(functools.partial(_gates, _hdot), b_[:, s], a_[:, s], al_, dt_)
            p, q, r, u = vjp((dbeta_[:, s], dgcum_[:, s]))
            db.append(p), da.append(q)
            dal, ddt = (r, u) if dal is None else (dal + r, ddt + u)
        return jnp.concatenate(db, axis=1), jnp.concatenate(da, axis=1), dal, ddt

    return _whole_call(name, fn, [b_raw, a_raw, a_log, dt_bias, dbeta, dgcum],
                       [b_raw.shape, a_raw.shape, a_log.shape, dt_bias.shape])


def _unit_lower_inverse_raw(a):
    c = a.shape[-1]
    eye = (lax.broadcasted_iota(jnp.int32, (c, c), 0) == lax.broadcasted_iota(jnp.int32, (c, c), 1)).astype(F32)
    inv, p = eye - a, _hdot_raw(a, a, 'nn')
    n_fac = int(math.log2(c)) - 1
    for it in range(n_fac):
        inv = inv + _hdot_raw(inv, p, 'nn')
        if it < n_fac - 1:
            p = _hdot_raw(p, p, 'nn')
    return inv


@jax.custom_vjp
def _unit_lower_inverse(a):
    return _unit_lower_inverse_raw(a)


def _unit_lower_inverse_fwd(a):
    inv = _unit_lower_inverse_raw(a)
    return inv, inv


def _unit_lower_inverse_bwd(inv, ct):
    return (-_hdot_raw(_hdot_raw(inv, ct, 'tn'), inv, 'nt'),)


_unit_lower_inverse.defvjp(_unit_lower_inverse_fwd, _unit_lower_inverse_bwd)


def _halves_raw(x):
    return x[..., :x.shape[-1] // 2], x[..., x.shape[-1] // 2:]


@jax.custom_vjp
def _halves(x):
    return _halves_raw(x)


_halves.defvjp(lambda x: (_halves_raw(x), None), lambda _, ct: (jnp.concatenate(ct, axis=-1),))

_DELTA_OPS = (_bdot_raw, _hdot_raw, _unit_lower_inverse_raw, _halves_raw)
_DELTA_OPS_VJP = (_bdot, _hdot, _unit_lower_inverse, _halves)


def _delta_chunk(ops, state, q, k, v, gc, gr, bc):
    bd, hd, inverse, halves = ops
    c, d = q.shape[-2:]
    ri = lax.broadcasted_iota(jnp.int32, (c, c), 0)
    ci = lax.broadcasted_iota(jnp.int32, (c, c), 1)
    causal, strict = ri >= ci, ri > ci
    qn = q * lax.rsqrt(jnp.sum(q * q, axis=-1, keepdims=True) + NORM_EPS) * (d ** -0.5)
    kn = k * lax.rsqrt(jnp.sum(k * k, axis=-1, keepdims=True) + NORM_EPS)
    gcum = jnp.broadcast_to(gc, q.shape)
    decay = jnp.where(causal, jnp.exp(jnp.where(causal, gc - gr, 0.0)), 0.0)
    bb = jnp.broadcast_to(bc, q.shape)
    k_beta = kn * bb
    inv = inverse(jnp.where(strict, bd(k_beta, kn, 'nt') * decay, 0.0))
    uw = hd(inv, jnp.concatenate([v * bb, k_beta * jnp.exp(gcum)], axis=-1), 'nn')
    u, w = halves(uw)
    qk = jnp.where(causal, bd(qn, kn, 'nt') * decay, 0.0)
    v_new = u - bd(w, state, 'nn')
    o = bd(qn * jnp.exp(gcum), state, 'nn') + bd(qk, v_new, 'nn')
    last = lax.broadcasted_iota(jnp.int32, (c, d), 0) == c - 1
    g_last = jnp.sum(jnp.where(last, gcum, 0.0), axis=-2, keepdims=True)
    k_dec = kn * jnp.exp(g_last - gcum)
    return state * jnp.exp(g_last) + bd(k_dec, v_new, 'tn'), o


DN_HEADS_PER_STEP = 8


def _by_head(ref):
    return jnp.stack([ref[:, s] for s in _heads(ref.shape[1])])


def _delta_specs(nh, nc, rev):
    c, d = DN_CHUNK, HEAD_DIM
    hb = min(DN_HEADS_PER_STEP, nh)
    ng = nh // hb
    ch = (lambda n: nc - 1 - n) if rev else (lambda n: n)
    qkv = [pl.BlockSpec((c, hb * d), functools.partial(lambda h, n, o: (ch(n), o * ng + h), o=o)) for o in range(3)]
    col = pl.BlockSpec((hb, c, 1), lambda h, n: (h, ch(n), 0))
    row = pl.BlockSpec((hb, None, 1, c), lambda h, n: (h, ch(n), 0, 0))
    st = pl.BlockSpec((hb, None, d, d), lambda h, n: (h, ch(n), 0, 0))
    tok = pl.BlockSpec((c, hb * d), lambda h, n: (ch(n), h))
    return hb, ng, qkv, col, row, st, tok


def _delta_fwd(qkv, gcol, grow, bcol, name):
    t = qkv.shape[0]
    nh, nc, d = gcol.shape[0], t // DN_CHUNK, HEAD_DIM
    hb, ng, qkv_s, col, row, st, tok = _delta_specs(nh, nc, False)

    def body(q_ref, k_ref, v_ref, gc_ref, gr_ref, bc_ref, o_ref, st_ref, state):
        @pl.when(pl.program_id(1) == 0)
        def _():
            state[...] = jnp.zeros_like(state)

        s0 = state[...]
        st_ref[...] = s0
        s1, o = _delta_chunk(_DELTA_OPS, s0, _by_head(q_ref), _by_head(k_ref), _by_head(v_ref), gc_ref[...],
                             gr_ref[...], bc_ref[...])
        for j, s in enumerate(_heads(hb * d)):
            o_ref[:, s] = o[j]
        state[...] = s1

    return _pcall(body, name=name, grid=(ng, nc), in_specs=qkv_s + [col, row, col], out_specs=[tok, st],
                  out_shape=[jax.ShapeDtypeStruct((t, nh * d), F32), jax.ShapeDtypeStruct((nh, nc, d, d), F32)],
                  scratch_shapes=[pltpu.VMEM((hb, d, d), F32)],
                  compiler_params=_params(("parallel", "arbitrary")))(qkv, qkv, qkv, gcol, grow, bcol)


def _delta_bwd(qkv, gcol, grow, bcol, states, do, name):
    t = qkv.shape[0]
    nh, nc, d = gcol.shape[0], t // DN_CHUNK, HEAD_DIM
    hb, ng, qkv_s, col, row, st, tok = _delta_specs(nh, nc, True)

    def body(q_ref, k_ref, v_ref, gc_ref, gr_ref, bc_ref, st_ref, do_ref,
             dq_ref, dk_ref, dv_ref, dgc_ref, dgr_ref, dbc_ref, dstate):
        @pl.when(pl.program_id(1) == 0)
        def _():
            dstate[...] = jnp.zeros_like(dstate)

        _, vjp = jax.vjp(functools.partial(_delta_chunk, _DELTA_OPS_VJP), st_ref[...], _by_head(q_ref), _by_head(k_ref),
                         _by_head(v_ref), gc_ref[...], gr_ref[...], bc_ref[...])
        ds, dq, dk, dv, dgc, dgr, dbc = vjp((dstate[...], _by_head(do_ref)))
        for j, s in enumerate(_heads(hb * d)):
            dq_ref[:, s], dk_ref[:, s], dv_ref[:, s] = dq[j], dk[j], dv[j]
        dgc_ref[...], dgr_ref[...], dbc_ref[...] = dgc, dgr, dbc
        dstate[...] = ds

    tok_out = jax.ShapeDtypeStruct((t, nh * d), F32)
    return _pcall(body, name=name, grid=(ng, nc), in_specs=qkv_s + [col, row, col, st, tok],
                  out_specs=[tok, tok, tok, col, row, col],
                  out_shape=[tok_out, tok_out, tok_out, jax.ShapeDtypeStruct(gcol.shape, F32),
                             jax.ShapeDtypeStruct(grow.shape, F32), jax.ShapeDtypeStruct(bcol.shape, F32)],
                  scratch_shapes=[pltpu.VMEM((hb, d, d), F32)],
                  compiler_params=_params(("parallel", "arbitrary")))(qkv, qkv, qkv, gcol, grow, bcol, states, do)


def _outgate_head(o_h, z_h, w):
    return _rms(o_h, w) * _silu(z_h)


def _outgate_fwd(o, z, w, name):
    def fn(o_, z_, w_):
        return (jnp.concatenate([_outgate_head(o_[:, s], z_[:, s], w_) for s in _heads(o_.shape[1])], axis=1),)

    return _rowcall(name, fn, [o, z], [w], [(o.shape[1], BF16)])[0]


def _outgate_bwd(o, z, w, dy, name):
    def fn(o_, z_, dy_, w_):
        do, dz, dw = [], [], None
        for s in _heads(o_.shape[1]):
            _, vjp = jax.vjp(_outgate_head, o_[:, s], z_[:, s], w_)
            a, b, c = vjp(dy_[:, s])
            do.append(a), dz.append(b)
            dw = c if dw is None else dw + c
        return jnp.concatenate(do, axis=1), jnp.concatenate(dz, axis=1), dw

    wd = o.shape[1]
    return _rowcall(name, fn, [o, z, dy], [w], [(wd, F32), (wd, F32)], [w.shape])


def _attn_head(dot, q_h, k_h, v_h):
    s = dot(q_h, k_h, 'nt') * (q_h.shape[1] ** -0.5)
    e = jnp.exp(s - lax.stop_gradient(jnp.max(s, axis=-1, keepdims=True)))
    p = e / jnp.sum(e, axis=-1, keepdims=True)
    return dot(p, v_h, 'nn')


def _xa_slices(width):
    hd = width // XA_HEADS
    return [slice(h * hd, (h + 1) * hd) for h in range(XA_HEADS)]


def _attn_fwd(q, kv, name):
    wd = q.shape[1]

    def fn(q_, kv_):
        k_, v_ = kv_[:, :wd], kv_[:, wd:]
        return (jnp.concatenate([_attn_head(_bdot_raw, q_[:, s], k_[:, s], v_[:, s]) for s in _xa_slices(wd)],
                                axis=1),)

    return _rowcall(name, fn, [q], [kv], [(wd, BF16)])[0]


def _attn_bwd(q, kv, do, name):
    wd = q.shape[1]

    def fn(q_, do_, kv_):
        k_, v_ = kv_[:, :wd].astype(F32), kv_[:, wd:].astype(F32)
        dq, dk, dv = [], [], []
        for s in _xa_slices(wd):
            _, vjp = jax.vjp(functools.partial(_attn_head, _bdot), q_[:, s].astype(F32), k_[:, s], v_[:, s])
            a, b, c = vjp(do_[:, s])
            dq.append(a), dk.append(b), dv.append(c)
        return jnp.concatenate(dq, axis=1), jnp.concatenate(dk + dv, axis=1)

    return _rowcall(name, fn, [q, do], [kv], [(wd, BF16)], [kv.shape])


def _loss_call(y, target, name):
    d = y.shape[1]

    def fn(y_, t_):
        err = y_ - t_
        part = 0.5 * jnp.sum(jnp.sum(err * err, axis=1, keepdims=True), axis=0, keepdims=True) / d
        return err / d, jnp.broadcast_to(part, (1, LANES))

    return _rowcall(name, fn, [y, target], [], [(d, F32)], [(1, LANES)])


def _adamw_call(w, g, m, v, name):
    c = w.shape[1]

    def fn(w_, g_, m_, v_):
        g_ = g_[:, :c]
        m1 = ADAM_B1 * m_ + (1.0 - ADAM_B1) * g_
        v1 = ADAM_B2 * v_ + (1.0 - ADAM_B2) * jnp.square(g_)
        m_hat = m1 / (1.0 - ADAM_B1 ** ADAM_STEP)
        v_hat = v1 / (1.0 - ADAM_B2 ** ADAM_STEP)
        return g_, -ADAM_LR * (m_hat / (jnp.sqrt(v_hat) + ADAM_EPS) + ADAM_WD * w_), m1, v1

    return _rowcall(name, fn, [w, g, m, v], [], [(c, F32)] * 4, br=_divtile(w.shape[0], 128, 8))


STREAM_BLOCK_BYTES = 4 * 1024 * 1024


def _rows_for(cols, itemsize):
    return max(16, STREAM_BLOCK_BYTES // (cols * itemsize) // 16 * 16)


def _pair_add(grad, theirs, core, name):
    s, hr, cols = theirs.shape
    br = _divtile(hr, _rows_for(cols, 2), 16)
    nb = hr // br

    def body(c_ref, g_ref, t_ref, o_ref):
        o_ref[...] = (g_ref[...].astype(F32) + t_ref[...].astype(F32)).astype(o_ref.dtype)

    spec = pl.BlockSpec((None, br, cols), lambda j, i, c_ref: (j, i, 0))
    return _pcall(body, name=name, out_shape=jax.ShapeDtypeStruct(theirs.shape, BF16),
                  grid_spec=pltpu.PrefetchScalarGridSpec(
                      num_scalar_prefetch=1, grid=(s, nb),
                      in_specs=[pl.BlockSpec((None, br, cols), lambda j, i, c_ref: (j, c_ref[0] * nb + i, 0)), spec],
                      out_specs=spec),
                  compiler_params=_params(("parallel", "parallel")))(core, grad, theirs)


def _chip_sum(part, landed, kind, where, grad, layer, name):
    _, hr, cw = landed.shape
    br = _divtile(hr, _rows_for(cw, 4), 16)
    nb = hr // br

    def body(w_ref, p_ref, a_ref, b_ref, d_ref, g_ref, o_ref):
        o_ref[...] = ((p_ref[...].astype(F32) + a_ref[...].astype(F32)) + b_ref[...].astype(F32)) + d_ref[...].astype(F32)

    if kind == 'row':
        own = pl.BlockSpec((None, br, cw), lambda i, w_ref: (w_ref[0], i, 0))
    else:
        own = pl.BlockSpec((br, cw), lambda i, w_ref: (i, w_ref[0]))
    got = [pl.BlockSpec((None, br, cw), functools.partial(lambda i, w_ref, k: (k, i, 0), k=k)) for k in range(3)]
    return _pcall(body, name=name, out_shape=jax.ShapeDtypeStruct(grad.shape, F32),
                  grid_spec=pltpu.PrefetchScalarGridSpec(
                      num_scalar_prefetch=1, grid=(nb,), in_specs=[own] + got + [ANY_SPEC],
                      out_specs=pl.BlockSpec((None, br, cw), lambda i, w_ref: (layer, w_ref[1] * nb + i, 0))),
                  input_output_aliases={5: 0},
                  compiler_params=_params(("parallel",)))(where, part, landed, landed, landed, grad)


def _position():
    x, y, c = lax.axis_index("x"), lax.axis_index("y"), lax.axis_index("c")
    others = [(1 - x, y), (x, 1 - y), (1 - x, 1 - y)]
    return x, y, c, others


def _any_specs(n):
    return [pl.BlockSpec(memory_space=pl.ANY)] * n


def _ds(start, size):
    return pl.ds(pl.multiple_of(start, size), size)


HBM_SPEC = pl.BlockSpec(memory_space=pltpu.HBM)
SEM_SPEC = pl.BlockSpec(memory_space=pltpu.SEMAPHORE)
ANY_SPEC = pl.BlockSpec(memory_space=pl.ANY)
DATAFLOW = pltpu.SideEffectType.DATAFLOW_SIDE_EFFECTING


def _hbm(a):
    return pltpu.with_memory_space_constraint(a, pltpu.HBM)


def _full_shape(shard_shape, kind):
    r, cw = shard_shape
    return (N_CHIPS, r, cw) if kind == 'row' else (r, N_CHIPS * cw)


def _block_dims(full, kind):
    return (full.shape[1], full.shape[2]) if kind == 'row' else (full.shape[0], full.shape[1] // N_CHIPS)


def _region(full, kind, chip, half):
    if kind == 'all':
        return full.at[chip]
    r, cw = _block_dims(full, kind)
    if kind == 'row':
        return full.at[chip, _ds(half * (r // 2), r // 2), :]
    return full.at[_ds(half * (r // 2), r // 2), _ds(chip * cw, cw)]


def _place_block(full, shards, layer, kind, chip, name, after=None):
    _, r, cs = shards.shape
    cw = _block_dims(full, kind)[1]
    br = _divtile(r, 256, 16)

    def body(c_ref, s_ref, *rest):
        o_ref = rest[-1]
        if cs == cw:
            o_ref[...] = s_ref[...].astype(o_ref.dtype)
        else:
            o_ref[:, :cs] = s_ref[...].astype(o_ref.dtype)
            o_ref[:, cs:] = jnp.zeros((br, cw - cs), o_ref.dtype)

    if kind == 'row':
        out_spec = pl.BlockSpec((None, br, cw), lambda i, c_ref: (c_ref[0], i, 0))
    else:
        out_spec = pl.BlockSpec((br, cw), lambda i, c_ref: (i, c_ref[0]))
    extra = _as_list(after)
    return _pcall(body, name=name, out_shape=jax.ShapeDtypeStruct(full.shape, full.dtype),
                  grid_spec=pltpu.PrefetchScalarGridSpec(
                      num_scalar_prefetch=1, grid=(r // br,),
                      in_specs=[pl.BlockSpec((None, br, cs), lambda i, c_ref: (layer, i, 0)), ANY_SPEC]
                      + [ANY_SPEC] * len(extra), out_specs=out_spec),
                  input_output_aliases={2: 0}, compiler_params=_params(("parallel",)))(chip, shards, full, *extra)


def _split_copy_start(bufs, lands, plan, after, name):
    nb, nl = len(bufs), len(lands)
    extra = _as_list(after)

    def body(*refs):
        ins = refs[:nb + nl]
        send, recv = refs[nb + nl + len(extra)], refs[nb + nl + len(extra) + 1]
        for cp in plan(ins[:nb], ins[nb:], send, recv):
            cp.start()
        refs[-1][...] = jnp.zeros_like(refs[-1])

    n_copies = plan.n_copies
    out = _pcall(
        body, name=name,
        out_shape=(pltpu.SemaphoreType.DMA((n_copies,)), pltpu.SemaphoreType.DMA((n_copies,)),
                   *[pltpu.HBM(a.shape, a.dtype) for a in (*bufs, *lands)], jax.ShapeDtypeStruct((8, LANES), F32)),
        in_specs=[HBM_SPEC] * (nb + nl) + [ANY_SPEC] * len(extra),
        out_specs=(SEM_SPEC, SEM_SPEC, *[HBM_SPEC] * (nb + nl), pl.BlockSpec(memory_space=pltpu.VMEM)),
        input_output_aliases={i: 2 + i for i in range(nb + nl)},
        compiler_params=pltpu.CompilerParams(has_side_effects=DATAFLOW),
    )(*[_hbm(a) for a in (*bufs, *lands)], *extra)
    return out[0], out[1], list(out[2:2 + nb]), list(out[2 + nb:2 + nb + nl]), out[-1]


def _split_copy_wait(send, recv, bufs, lands, plan, after, name):
    nb, nl = len(bufs), len(lands)
    extra = _as_list(after)

    def body(*refs):
        ins = refs[:nb + nl]
        send_ref, recv_ref = refs[nb + nl], refs[nb + nl + 1]
        for cp in plan(ins[:nb], ins[nb:], send_ref, recv_ref):
            cp.wait_send()
            cp.wait_recv()

    out = _pcall(
        body, name=name, out_shape=tuple(pltpu.HBM(a.shape, a.dtype) for a in (*bufs, *lands)),
        in_specs=[HBM_SPEC] * (nb + nl) + [SEM_SPEC, SEM_SPEC] + [ANY_SPEC] * len(extra),
        out_specs=tuple([HBM_SPEC] * (nb + nl)), input_output_aliases={i: i for i in range(nb + nl)},
        compiler_params=pltpu.CompilerParams(has_side_effects=DATAFLOW),
    )(*bufs, *lands, send, recv, *extra)
    return list(out[:nb]), list(out[nb:])


def _gather_plan(kinds):
    def plan(bufs, lands, send, recv):
        x, y, c, others = _position()
        me = 2 * x + y
        return [pltpu.make_async_remote_copy(
            src_ref=_region(lands[w], kinds[w], me, c), dst_ref=_region(lands[w], kinds[w], me, c),
            send_sem=send.at[3 * w + k], recv_sem=recv.at[3 * w + k], device_id=(px, py, c), device_id_type=MESH_IDS)
            for w in range(len(lands)) for k, (px, py) in enumerate(others)]

    def wait_plan(bufs, lands, send, recv):
        x, y, c, others = _position()
        me = 2 * x + y
        return [pltpu.make_async_remote_copy(
            src_ref=_region(lands[w], kinds[w], me, c), dst_ref=_region(lands[w], kinds[w], 2 * px + py, c),
            send_sem=send.at[3 * w + k], recv_sem=recv.at[3 * w + k], device_id=(px, py, c), device_id_type=MESH_IDS)
            for w in range(len(lands)) for k, (px, py) in enumerate(others)]

    plan.n_copies = wait_plan.n_copies = 3 * len(kinds)
    return plan, wait_plan


def _gather_forward(fulls, kinds, name):
    n = len(fulls)

    def body(*refs):
        dst = refs[n:2 * n]
        send, recv = refs[2 * n:]
        x, y, c, others = _position()
        sib = (x, y, 1 - c)

        def copy(w, k, chip, half):
            reg = _region(dst[w], kinds[w], chip, half)
            return pltpu.make_async_remote_copy(src_ref=reg, dst_ref=reg, send_sem=send.at[3 * w + k],
                                                recv_sem=recv.at[3 * w + k], device_id=sib, device_id_type=MESH_IDS)

        give = [copy(w, k, 2 * px + py, c) for w in range(n) for k, (px, py) in enumerate(others)]
        for cp in give:
            cp.start()
        for w in range(n):
            for k, (px, py) in enumerate(others):
                copy(w, k, 2 * px + py, 1 - c).wait_recv()
        for cp in give:
            cp.wait_send()

    return _pcall(body, name=name, in_specs=_any_specs(n), out_specs=_any_specs(n),
                  out_shape=[jax.ShapeDtypeStruct(f.shape, f.dtype) for f in fulls],
                  input_output_aliases={i: i for i in range(n)},
                  scratch_shapes=[pltpu.SemaphoreType.DMA((3 * n,)), pltpu.SemaphoreType.DMA((3 * n,))])(*fulls)


def _grad_half(ref, kind, half):
    if kind == 'row':
        hr = ref.shape[1] // 2
        return ref.at[:, _ds(half * hr, hr), :]
    hr = ref.shape[0] // 2
    return ref.at[_ds(half * hr, hr), :]


def _half_shape(g, kind):
    return (g.shape[0], g.shape[1] // 2, g.shape[2]) if kind == 'row' else (g.shape[0] // 2, g.shape[1])


def _pair_plan(kinds):
    def plan(bufs, lands, send, recv):
        x, y, c, _ = _position()
        return [pltpu.make_async_remote_copy(src_ref=_grad_half(bufs[w], kinds[w], 1 - c), dst_ref=lands[w],
                                             send_sem=send.at[w], recv_sem=recv.at[w], device_id=(x, y, 1 - c),
                                             device_id_type=MESH_IDS) for w in range(len(bufs))]

    plan.n_copies = len(kinds)
    return plan


def _pair_exchange(grads, kinds, name):
    n = len(grads)

    def body(*refs):
        src, theirs_ref = refs[:n], refs[n:2 * n]
        send, recv = refs[2 * n:]
        x, y, c, _ = _position()
        sib = (x, y, 1 - c)

        def half(w, h):
            if kinds[w] == 'row':
                hr = src[w].shape[1] // 2
                return src[w].at[:, _ds(h * hr, hr), :]
            hr = src[w].shape[0] // 2
            return src[w].at[_ds(h * hr, hr), :]

        give = [pltpu.make_async_remote_copy(src_ref=half(w, 1 - c), dst_ref=theirs_ref[w], send_sem=send.at[w],
                                             recv_sem=recv.at[w], device_id=sib, device_id_type=MESH_IDS)
                for w in range(n)]
        for cp in give:
            cp.start()
        for cp in give:
            cp.wait()

    def half_shape(g, kind):
        return (g.shape[0], g.shape[1] // 2, g.shape[2]) if kind == 'row' else (g.shape[0] // 2, g.shape[1])

    return _pcall(body, name=name, in_specs=_any_specs(n), out_specs=_any_specs(n),
                  out_shape=[jax.ShapeDtypeStruct(half_shape(g, kd), g.dtype) for g, kd in zip(grads, kinds)],
                  scratch_shapes=[pltpu.SemaphoreType.DMA((n,)), pltpu.SemaphoreType.DMA((n,))])(*grads)


def _part_dims(part, kind):
    return (part.shape[1], part.shape[2]) if kind == 'row' else (part.shape[0], part.shape[1] // N_CHIPS)


def _chip_plan(kinds):
    def plan(bufs, lands, send, recv):
        x, y, c, others = _position()

        def slot(w, chip):
            if kinds[w] == 'row':
                return bufs[w].at[chip]
            cw = _part_dims(bufs[w], kinds[w])[1]
            return bufs[w].at[:, _ds(chip * cw, cw)]

        return [pltpu.make_async_remote_copy(src_ref=slot(w, 2 * px + py), dst_ref=lands[w].at[k],
                                             send_sem=send.at[3 * w + k], recv_sem=recv.at[3 * w + k],
                                             device_id=(px, py, c), device_id_type=MESH_IDS)
                for w in range(len(bufs)) for k, (px, py) in enumerate(others)]

    plan.n_copies = 3 * len(kinds)
    return plan


def _pair_gather_plan(n, n_layers):
    def copies(lands, send, recv, take):
        x, y, c, _ = _position()
        out = []
        for w in range(n):
            hr = lands[w].shape[1] // 2
            for l in range(n_layers):
                mine = lands[w].at[l, _ds(c * hr, hr), :]
                theirs = lands[w].at[l, _ds((1 - c) * hr, hr), :]
                out.append(pltpu.make_async_remote_copy(
                    src_ref=mine, dst_ref=theirs if take else mine, send_sem=send.at[w * n_layers + l],
                    recv_sem=recv.at[w * n_layers + l], device_id=(x, y, 1 - c), device_id_type=MESH_IDS))
        return out

    def plan(bufs, lands, send, recv):
        return copies(lands, send, recv, False)

    def wait_plan(bufs, lands, send, recv):
        return copies(lands, send, recv, True)

    plan.n_copies = wait_plan.n_copies = n * n_layers
    return plan, wait_plan


def _pair_gather(grads, name, after=None):
    n = len(grads)
    n_layers = grads[0].shape[0]
    extra = _as_list(after)

    def body(*refs):
        dst = refs[n + len(extra):2 * n + len(extra)]
        send, recv = refs[2 * n + len(extra):]
        x, y, c, _ = _position()
        sib = (x, y, 1 - c)

        def copy(w, l, half):
            hr = dst[w].shape[1] // 2
            rows = dst[w].at[l, _ds(half * hr, hr), :]
            return pltpu.make_async_remote_copy(src_ref=rows, dst_ref=rows, send_sem=send.at[w * n_layers + l],
                                                recv_sem=recv.at[w * n_layers + l], device_id=sib,
                                                device_id_type=MESH_IDS)

        give = [copy(w, l, c) for w in range(n) for l in range(n_layers)]
        for cp in give:
            cp.start()
        for w in range(n):
            for l in range(n_layers):
                copy(w, l, 1 - c).wait_recv()
        for cp in give:
            cp.wait_send()

    m = n * n_layers
    return _pcall(body, name=name, in_specs=_any_specs(n + len(extra)), out_specs=_any_specs(n),
                  out_shape=[jax.ShapeDtypeStruct(g.shape, g.dtype) for g in grads],
                  input_output_aliases={i: i for i in range(n)},
                  scratch_shapes=[pltpu.SemaphoreType.DMA((m,)), pltpu.SemaphoreType.DMA((m,))])(*grads, *extra)


def _all_sum(buf, name):
    rows = buf.shape[0]
    flips = [(fx, fy, fc) for fx in (0, 1) for fy in (0, 1) for fc in (0, 1)][1:]

    def body(x_ref, o_ref, land, send, recv):
        x, y, c, _ = _position()
        me = 4 * x + 2 * y + c

        def peer(f):
            return (1 - x if f[0] else x, 1 - y if f[1] else y, 1 - c if f[2] else c)

        give = [pltpu.make_async_remote_copy(src_ref=x_ref, dst_ref=land.at[me], send_sem=send.at[k],
                                             recv_sem=recv.at[k], device_id=peer(f), device_id_type=MESH_IDS)
                for k, f in enumerate(flips)]
        for cp in give:
            cp.start()
        land[me] = x_ref[...]
        for k, f in enumerate(flips):
            px, py, pc = peer(f)
            slot = land.at[4 * px + 2 * py + pc]
            pltpu.make_async_remote_copy(src_ref=slot, dst_ref=slot, send_sem=send.at[k], recv_sem=recv.at[k],
                                         device_id=peer(f), device_id_type=MESH_IDS).wait_recv()
        for cp in give:
            cp.wait_send()
        total = land[0]
        for dev in range(1, N_DEV):
            total = total + land[dev]
        o_ref[...] = total

    return _pcall(body, name=name, in_specs=[pl.BlockSpec(memory_space=pltpu.VMEM)],
                  out_specs=pl.BlockSpec(memory_space=pltpu.VMEM), out_shape=jax.ShapeDtypeStruct(buf.shape, F32),
                  scratch_shapes=[pltpu.VMEM((N_DEV, rows, LANES), F32), pltpu.SemaphoreType.DMA((N_DEV - 1,)),
                                  pltpu.SemaphoreType.DMA((N_DEV - 1,))],
                  compiler_params=pltpu.CompilerParams(vmem_limit_bytes=VMEM_LIMIT))(buf)


def _pack(arrays):
    flat = jnp.concatenate([a.reshape(-1).astype(F32) for a in arrays])
    pad = (-flat.shape[0]) % (8 * LANES)
    return jnp.pad(flat, (0, pad)).reshape(-1, LANES)


def _unpack(buf, like):
    flat, out, off = buf.reshape(-1), [], 0
    for a in like:
        out.append(flat[off:off + a.size].reshape(a.shape))
        off += a.size
    return out


def _row2(v):
    return v.reshape(1, -1)


def _ffn_fwd(x, g_pre, w_gu, w_d, g_post, tag, after=None):
    h = _norm_fwd(x, g_pre, tag + "_pre", after)
    gu, a = _ffn_up(h, w_gu, tag + "_gu")
    f = _mm(a, w_d, 'nn', F32, tag + "_down")
    return _resnorm_fwd(x, f, g_post, 0.5, tag + "_post"), (x, h, gu, a, f)


def _ffn_bwd_weights(dx_out, saved, w_d, g_post, tag, after=None):
    x, h, gu, a, f = saved
    df, dg_post = _resnorm_bwd(f, dx_out, g_post, 0.5, tag + "_post_b", after)
    dw_d = _mm(a, df, 'tn', BF16, tag + "_down_bw")
    dgu = _ffn_down_bwd(df, w_d, gu, tag + "_down_bx")
    dw_gu = _mm(h, dgu, 'tn', BF16, tag + "_gu_bw", halves='b')
    return dgu, dw_gu, dw_d, dg_post


def _ffn_bwd_input(dx_out, saved, dgu, g_pre, w_gu, tag, after=None):
    x = saved[0]
    dh = _mm(dgu, w_gu, 'nt', F32, tag + "_gu_bx", halves='a')
    return _norm_bwd(x, dh, dx_out, g_pre, tag + "_pre_b", after)


def _ffn_bwd(dx_out, saved, g_pre, w_gu, w_d, g_post, tag, after=None):
    dgu, dw_gu, dw_d, dg_post = _ffn_bwd_weights(dx_out, saved, w_d, g_post, tag, after)
    dx, dg_pre = _ffn_bwd_input(dx_out, saved, dgu, g_pre, w_gu, tag)
    return dx, dg_pre, dw_gu, dw_d, dg_post


def _split_cols(pp, cs, cp, widths):
    proj = jnp.concatenate([pp[:, j * cp:j * cp + cs] for j in range(N_CHIPS)], axis=1)
    out, off = [], 0
    for wd in widths:
        out.append(proj[:, off:off + wd])
        off += wd
    return out


def _join_cols(pieces, cs, cp):
    proj = jnp.concatenate(pieces, axis=1)
    t = proj.shape[0]
    cols = []
    for j in range(N_CHIPS):
        cols += [proj[:, j * cs:(j + 1) * cs], jnp.zeros((t, cp - cs), proj.dtype)]
    return jnp.concatenate(cols, axis=1)


def _mix_fwd(x, p, w_in, w_out, conv_w, cs, cp, tag, after=None):
    t, d = x.shape
    half = d // 2
    nh = half // HEAD_DIM
    h = _norm_fwd(x, p['mix_norm_pre'], tag + "_pre", after)
    pp = _mm(h, w_in, 'nn', F32, tag + "_in")
    u_a, v_a, qkv_raw, z, b_raw, a_raw = _split_cols(pp, cs, cp, [half, half, 3 * half, half, nh, nh])
    y_a = _sgu_fwd(u_a, v_a, p['sm_w'], p['sm_b'], p['sm_ln_g'], p['sm_ln_b'], tag + "_sgu")
    qkv = _conv_fwd(qkv_raw, conv_w, tag + "_conv")
    b_raw, a_raw = b_raw.T, a_raw.T
    beta, gcum = _gates_fwd(b_raw, a_raw, p['a_log'].T, p['dt_bias'].T, tag + "_gates")
    gcol, bcol = gcum[:, :, None], beta[:, :, None]
    grow = gcum.reshape(nh, t // DN_CHUNK, 1, DN_CHUNK)
    o, states = _delta_fwd(qkv, gcol, grow, bcol, tag + "_delta")
    y_b = _outgate_fwd(o, z, p['dn_norm_w'], tag + "_gate")
    y = jnp.concatenate([y_a, y_b], axis=1)
    m = _mm(y, w_out, 'nn', F32, tag + "_out")
    x_out = _resnorm_fwd(x, m, p['mix_norm_post'], 1.0, tag + "_post")
    return x_out, (x, h, u_a, v_a, qkv_raw, z, b_raw, a_raw, qkv, gcol, grow, bcol, states, o, y, m)


def _mix_bwd(dx_out, saved, p, w_in, w_out, conv_w, cs, cp, tag, after=None):
    x, h, u_a, v_a, qkv_raw, z, b_raw, a_raw, qkv, gcol, grow, bcol, states, o, y, m = saved
    t, d = x.shape
    half = d // 2
    nh = half // HEAD_DIM
    gr = {}
    dm, gr['mix_norm_post'] = _resnorm_bwd(m, dx_out, p['mix_norm_post'], 1.0, tag + "_post_b", after)
    dy = _mm(dm, w_out, 'nt', F32, tag + "_out_bx")
    gr['w_out'] = _mm(y, dm, 'tn', BF16, tag + "_out_bw")
    dy_a, dy_b = dy[:, :half], dy[:, half:]
    do, dz, gr['dn_norm_w'] = _outgate_bwd(o, z, p['dn_norm_w'], dy_b, tag + "_gate_b")
    dq, dk, dv, dgcol, dgrow, dbcol = _delta_bwd(qkv, gcol, grow, bcol, states, do, tag + "_delta_b")
    dgcum = dgcol[:, :, 0] + dgrow.reshape(nh, t)
    db_raw, da_raw, gr['a_log'], gr['dt_bias'] = _gates_bwd(b_raw, a_raw, p['a_log'].T, p['dt_bias'].T,
                                                             dbcol[:, :, 0], dgcum, tag + "_gates_b")
    db_raw, da_raw = db_raw.T, da_raw.T
    thirds = [_conv_bwd(qkv_raw, conv_w, dpart, i * half, tag + "_conv_b") for i, dpart in enumerate((dq, dk, dv))]
    gr['conv_w'] = jnp.concatenate([dw for _, dw in thirds], axis=1)
    du_a, dv_a, gr['sm_w'], gr['sm_b'], gr['sm_ln_g'], gr['sm_ln_b'] = _sgu_bwd(
        u_a, v_a, p['sm_w'], p['sm_b'], p['sm_ln_g'], p['sm_ln_b'], dy_a, tag + "_sgu_b")
    dpp = _join_cols([du_a, dv_a, *[dx for dx, _ in thirds], dz, db_raw, da_raw], cs, cp).astype(BF16)
    dh = _mm(dpp, w_in, 'nt', F32, tag + "_in_bx")
    gr['w_in'] = _mm(h, dpp, 'tn', BF16, tag + "_in_bw")
    dx, gr['mix_norm_pre'] = _norm_bwd(x, dh, dx_out, p['mix_norm_pre'], tag + "_pre_b")
    return dx, gr


def _xa_fwd(x, mem, p, w_xq, w_xkv, w_xo, tag, after=None):
    h = _norm_fwd(x, p['xa_norm_pre'], tag + "_pre", after)
    mh = _norm_fwd(mem, p['mem_norm'], tag + "_mem")
    q = _mm(h, w_xq, 'nn', BF16, tag + "_q")
    kv = _mm(mh, w_xkv, 'nn', BF16, tag + "_kv")
    o = _attn_fwd(q, kv, tag + "_attn")
    c = _mm(o, w_xo, 'nn', F32, tag + "_o")
    return _resnorm_fwd(x, c, p['xa_norm_post'], 1.0, tag + "_post"), (x, h, mh, q, kv, o, c)


def _xa_bwd(dx_out, saved, mem, p, w_xq, w_xkv, w_xo, tag, after=None):
    x, h, mh, q, kv, o, c = saved
    gr = {}
    dc, gr['xa_norm_post'] = _resnorm_bwd(c, dx_out, p['xa_norm_post'], 1.0, tag + "_post_b", after)
    do = _mm(dc, w_xo, 'nt', F32, tag + "_o_bx")
    gr['w_xo'] = _mm(o, dc, 'tn', BF16, tag + "_o_bw")
    dq, dkv = _attn_bwd(q, kv, do, tag + "_attn_b")
    dkv = dkv.astype(BF16)
    dh = _mm(dq, w_xq, 'nt', F32, tag + "_q_bx")
    gr['w_xq'] = _mm(h, dq, 'tn', BF16, tag + "_q_bw")
    dmh = _mm(dkv, w_xkv, 'nt', F32, tag + "_kv_bx")
    gr['w_xkv'] = _mm(mh, dkv, 'tn', BF16, tag + "_kv_bw")
    gr['mem_norm'] = _norm_bwd_gain(mem, dmh, p['mem_norm'], tag + "_mem_b")
    dx, gr['xa_norm_pre'] = _norm_bwd(x, dh, dx_out, p['xa_norm_pre'], tag + "_pre_b")
    return dx, gr


def _step(inputs):
    x, mem, target = inputs['x'][0], inputs['mem'][0], inputs['loss_target'][0]
    n_layers = inputs['w_in'].shape[0]
    cx, cy, cc = lax.axis_index("x"), lax.axis_index("y"), lax.axis_index("c")
    chip = 2 * cx + cy
    cs = inputs['w_in'].shape[2]
    cp = -(-cs // LANES) * LANES
    kinds = [BIG[nm] for nm in BIG_NAMES]

    conv_local = inputs['conv_w']
    ccols = conv_local.shape[2]

    core_idx, chip_idx = cc.astype(jnp.int32).reshape(1), chip.astype(jnp.int32).reshape(1)
    where_idx = jnp.stack([chip, cc]).astype(jnp.int32)
    order = [(l, gi) for l in range(n_layers) for gi in range(len(GROUPS))]
    n_groups = len(order)

    def small(l):
        p = {nm: inputs[nm][l] for nm in SMALL_NAMES}
        for nm in SMALL_NAMES:
            if p[nm].ndim == 1:
                p[nm] = _row2(p[nm])
        p['sm_b'] = p['sm_b'][:, :, None]
        return p

    def place(k, after=None):
        l, gi = order[k]
        lands = []
        for nm in GROUPS[gi][1]:
            _, r, c_in = inputs[nm].shape
            shape = _full_shape((r, cp if nm == 'w_in' else c_in), BIG[nm])
            lands.append(_place_block(lax.empty(shape, BF16), inputs[nm], l, BIG[nm], chip_idx, "place_block", after))
        return lands

    placed = {}
    conv_group = 1

    def gather_start(k, after):
        names = list(GROUPS[order[k][1]][1])
        kds = [BIG[nm] for nm in names]
        if k not in placed:
            placed[k] = place(k)
        lands = list(placed[k])
        if k == conv_group:
            mine = conv_local.reshape(1, n_layers * CONV_WIDTH, ccols)
            lands.append(lax.dynamic_update_slice(jnp.zeros((N_CHIPS, *mine.shape[1:]), F32), mine, (chip, 0, 0)))
            names.append('conv_w'), kds.append('all')
        plan, wait_plan = _gather_plan(kds)
        send, recv, _, lands, token = _split_copy_start([], lands, plan, after, f"gather_start_{k}")
        return dict(send=send, recv=recv, lands=lands, token=token, kinds=kds, names=names, wait_plan=wait_plan)

    def gather_finish(k, st, after):
        _, lands = _split_copy_wait(st['send'], st['recv'], [], st['lands'], st['wait_plan'], after, f"gather_wait_{k}")
        got = dict(zip(st['names'], lands))
        halves = [nm for nm, kd in zip(st['names'], st['kinds']) if kd != 'all']
        passed = _gather_forward([got[nm] for nm in halves], [BIG[nm] for nm in halves],
                                 "gather_forward_" + GROUPS[order[k][1]][0])
        w = {nm: (g.reshape(-1, g.shape[2]) if BIG[nm] == 'row' else g) for nm, g in zip(halves, passed)}
        if 'conv_w' in got:
            w['conv_w'] = got['conv_w'].reshape(N_CHIPS, n_layers, CONV_WIDTH, ccols).transpose(1, 2, 0, 3).reshape(
                n_layers, CONV_WIDTH, N_CHIPS * ccols)
        return w

    def group_fwd(gi, x_, p, w, l, token):
        if gi == 0:
            return _ffn_fwd(x_, p['ffn1_norm_pre'], w['ffn1_w_gate_up'], w['ffn1_w_down'], p['ffn1_norm_post'], "ffn", token)
        if gi == 1:
            return _mix_fwd(x_, p, w['w_in'], w['w_out'], conv_full[l], cs, cp, "mix", token)
        if gi == 2:
            return _xa_fwd(x_, mem, p, w['w_xq'], w['w_xkv'], w['w_xo'], "xa", token)
        return _ffn_fwd(x_, p['ffn2_norm_pre'], w['ffn2_w_gate_up'], w['ffn2_w_down'], p['ffn2_norm_post'], "ffn", token)

    def group_bwd(gi, dx_, s, p, w, l, token):
        if gi in (0, 3):
            pre = 'ffn1' if gi == 0 else 'ffn2'
            dx_, g_pre, g_gu, g_d, g_post = _ffn_bwd(dx_, s, p[pre + '_norm_pre'], w[pre + '_w_gate_up'],
                                                     w[pre + '_w_down'], p[pre + '_norm_post'], "ffn", token)
            return dx_, {pre + '_norm_pre': g_pre, pre + '_w_gate_up': g_gu, pre + '_w_down': g_d,
                         pre + '_norm_post': g_post}
        if gi == 1:
            return _mix_bwd(dx_, s, p, w['w_in'], w['w_out'], conv_full[l], cs, cp, "mix", token)
        return _xa_bwd(dx_, s, mem, p, w['w_xq'], w['w_xkv'], w['w_xo'], "xa", token)

    started = {0: gather_start(0, None)}
    started[1] = gather_start(1, started[0]['token'])
    weights, saved = [], []
    for k in range(2, n_groups):
        placed[k] = place(k, started[1]['token'])
    head_work = [started[1]['token']] + [land for k in range(2, n_groups) for land in placed[k]]
    conv_full = None
    for k, (l, gi) in enumerate(order):
        w = gather_finish(k, started[k], x if k > 0 else head_work)
        if k == conv_group:
            conv_full = w.pop('conv_w')
        token = None
        if k + 2 < n_groups:
            started[k + 2] = gather_start(k + 2, next(iter(w.values())))
            token = started[k + 2]['token']
        weights.append(w)
        x, s = group_fwd(gi, x, small(l), w, l, token)
        saved.append(s)

    dx, loss_part = _loss_call(x, target, "loss")

    grads = [dict() for _ in range(n_layers)]
    big_grads = {nm: None for nm in BIG_NAMES}
    pair_gathers = {}

    def reduce_finish(pd, after):
        parts, lands = _split_copy_wait(pd['send'], pd['recv'], pd['parts'], pd['lands'], pd['plan'], after,
                                        f"chip_wait_{pd['k']}")
        for nm, kd, part, landed in zip(pd['names'], pd['kinds'], parts, lands):
            if big_grads[nm] is None:
                big_grads[nm] = lax.empty((n_layers, 2 * landed.shape[1], landed.shape[2]), F32)
            big_grads[nm] = _chip_sum(part, landed, kd, where_idx, big_grads[nm], pd['l'], "chip_sum")
        if pd['l'] == 0 and pd['k'] > 0:
            names = pd['names']
            plan, wait_plan = _pair_gather_plan(len(names), n_layers)
            send, recv, _, got, _ = _split_copy_start([], [big_grads[nm] for nm in names], plan, None,
                                                      f"pair_gather_start_{pd['k']}")
            pair_gathers[pd['k']] = dict(send=send, recv=recv, lands=got, wait_plan=wait_plan)

    def by_chip(gr, names, kds):
        return [gr[nm].reshape(N_CHIPS, -1, gr[nm].shape[1]) if kd == 'row' else gr[nm] for nm, kd in zip(names, kds)]

    def chip_start(k, names, kds, gs, theirs):
        parts = [_pair_add(g, t, core_idx, "pair_add") if kd == 'row'
                 else _pair_add(g[None], t[None], core_idx, "pair_add")[0] for g, t, kd in zip(gs, theirs, kds)]
        lands = [lax.empty((3, *_part_dims(part, kd)), BF16) for part, kd in zip(parts, kds)]
        plan = _chip_plan(kds)
        send, recv, parts, lands, token = _split_copy_start(parts, lands, plan, None, f"chip_start_{k}")
        return dict(send=send, recv=recv, parts=parts, lands=lands, token=token, plan=plan, names=names, kinds=kds,
                    k=k, l=order[k][0])

    pair_pending, chip_queue = None, []
    for k in reversed(range(n_groups)):
        l, gi = order[k]
        names = GROUPS[gi][1]
        kds = [BIG[nm] for nm in names]
        p, w = small(l), weights[k]
        tokens = [pd['token'] for pd in ([pair_pending] if pair_pending else []) + chip_queue] or None
        if k > 0:
            dx_next, gr = group_bwd(gi, dx, saved[k], p, w, l, tokens)
        else:
            dgu, g_gu, g_d, g_post = _ffn_bwd_weights(dx, saved[k], w['ffn1_w_down'], p['ffn1_norm_post'], "ffn", tokens)
            gr = {'ffn1_w_gate_up': g_gu, 'ffn1_w_down': g_d, 'ffn1_norm_post': g_post}
            dx_next = g_gu
        for pd in chip_queue:
            reduce_finish(pd, dx_next)
        chip_queue = []
        if pair_pending:
            pd = pair_pending
            gs_prev, theirs = _split_copy_wait(pd['send'], pd['recv'], pd['gs'], pd['lands'], pd['plan'], dx_next,
                                               f"pair_wait_{pd['k']}")
            chip_queue.append(chip_start(pd['k'], pd['names'], pd['kinds'], gs_prev, theirs))
            pair_pending = None
        gs = by_chip(gr, names, kds)
        if k > 1:
            plan = _pair_plan(kds)
            lands = [lax.empty(_half_shape(g, kd), BF16) for g, kd in zip(gs, kds)]
            send, recv, gs, lands, token = _split_copy_start(gs, lands, plan, None, f"pair_start_{k}")
            pair_pending = dict(send=send, recv=recv, gs=gs, lands=lands, token=token, plan=plan, names=names,
                                kinds=kds, k=k)
        else:
            started_now = chip_start(k, names, kds, gs, _pair_exchange(gs, kds, "pair_exchange_" + GROUPS[gi][0]))
            if k > 0:
                chip_queue.append(started_now)
        if k > 0:
            dx = dx_next
        else:
            pending = started_now
            dx, gr['ffn1_norm_pre'] = _ffn_bwd_input(dx, saved[k], dgu, p['ffn1_norm_pre'], w['ffn1_w_gate_up'], "ffn",
                                                     pending['token'])
        grads[l].update({nm: g for nm, g in gr.items() if nm not in BIG})
    for pd in chip_queue:
        reduce_finish(pd, dx)
    out = {}

    def two(a):
        return a.reshape(-1, a.shape[-1])

    def adamw(nm, g):
        res = _adamw_call(two(inputs[nm]), two(g), two(inputs['m_' + nm]), two(inputs['v_' + nm]), "adamw")
        out[nm] = tuple(a.reshape(inputs[nm].shape) for a in res)
        return res[1]

    def finish_weights(gi, after=None):
        names = GROUPS[gi][1]
        if gi in pair_gathers:
            st = pair_gathers[gi]
            _, got = _split_copy_wait(st['send'], st['recv'], [], st['lands'], st['wait_plan'], after,
                                      f"pair_gather_wait_{gi}")
        else:
            got = _pair_gather([big_grads[nm] for nm in names], "pair_gather_" + GROUPS[gi][0], after)
        return [adamw(nm, g) for nm, g in zip(names, got)]

    done = [d for gi in range(len(GROUPS) - 1, 0, -1) for d in finish_weights(gi, pending['token'])]

    small_grads = [jnp.stack([grads[l][nm].reshape(inputs[nm].shape[1:]) if nm != 'conv_w' else grads[l][nm]
                              for l in range(n_layers)]) for nm in SMALL_NAMES]
    summed = _unpack(_all_sum(_pack(small_grads + [loss_part]), "small_sum"), small_grads + [loss_part])
    loss = summed[-1][0, 0]
    small_g = dict(zip(SMALL_NAMES, summed[:-1]))
    small_g['conv_w'] = lax.dynamic_slice(small_g['conv_w'], (0, 0, chip * ccols), (n_layers, CONV_WIDTH, ccols))
    done += [adamw(nm, small_g[nm]) for nm in SMALL_NAMES]

    reduce_finish(pending, done)
    finish_weights(0)
    return (loss, dx[None], *[out[nm][0] for nm in WEIGHTS], *[out[nm][1] for nm in WEIGHTS],
            *[out[nm][2] for nm in WEIGHTS], *[out[nm][3] for nm in WEIGHTS])


def kernel(x, mem, ffn1_norm_pre, ffn1_w_gate_up, ffn1_w_down, ffn1_norm_post, mix_norm_pre, w_in, conv_w, a_log, dt_bias, sm_w, sm_b, sm_ln_g, sm_ln_b, dn_norm_w, w_out, mix_norm_post, xa_norm_pre, mem_norm, w_xq, w_xkv, w_xo, xa_norm_post, ffn2_norm_pre, ffn2_w_gate_up, ffn2_w_down, ffn2_norm_post, loss_target, m_ffn1_norm_pre, m_ffn1_w_gate_up, m_ffn1_w_down, m_ffn1_norm_post, m_mix_norm_pre, m_w_in, m_conv_w, m_a_log, m_dt_bias, m_sm_w, m_sm_b, m_sm_ln_g, m_sm_ln_b, m_dn_norm_w, m_w_out, m_mix_norm_post, m_xa_norm_pre, m_mem_norm, m_w_xq, m_w_xkv, m_w_xo, m_xa_norm_post, m_ffn2_norm_pre, m_ffn2_w_gate_up, m_ffn2_w_down, m_ffn2_norm_post, v_ffn1_norm_pre, v_ffn1_w_gate_up, v_ffn1_w_down, v_ffn1_norm_post, v_mix_norm_pre, v_w_in, v_conv_w, v_a_log, v_dt_bias, v_sm_w, v_sm_b, v_sm_ln_g, v_sm_ln_b, v_dn_norm_w, v_w_out, v_mix_norm_post, v_xa_norm_pre, v_mem_norm, v_w_xq, v_w_xkv, v_w_xo, v_xa_norm_post, v_ffn2_norm_pre, v_ffn2_w_gate_up, v_ffn2_w_down, v_ffn2_norm_post):
    vals = dict(locals())
    return _step(vals)
```

```python
import functools
import math

import jax
import jax.numpy as jnp
from jax import lax
from jax.experimental import pallas as pl
from jax.experimental.pallas import tpu as pltpu

F32, BF16 = jnp.float32, jnp.bfloat16
NORM_EPS = 1e-6
HEAD_DIM = 128
GM_CHUNK = 128
DN_CHUNK = 64
CONV_WIDTH = 4
XA_HEADS = 4
LANES = 128
MXU_WIDTH = 256
N_CHIPS = 4
N_DEV = 8
VMEM_LIMIT = 56 * 1024 * 1024
MESH_IDS = pl.DeviceIdType.MESH
ADAM_LR, ADAM_B1, ADAM_B2, ADAM_EPS, ADAM_WD, ADAM_STEP = 0.001, 0.9, 0.999, 1e-08, 0.01, 10

WEIGHTS = ['ffn1_norm_pre', 'ffn1_w_gate_up', 'ffn1_w_down', 'ffn1_norm_post', 'mix_norm_pre', 'w_in', 'conv_w',
           'a_log', 'dt_bias', 'sm_w', 'sm_b', 'sm_ln_g', 'sm_ln_b', 'dn_norm_w', 'w_out', 'mix_norm_post',
           'xa_norm_pre', 'mem_norm', 'w_xq', 'w_xkv', 'w_xo', 'xa_norm_post', 'ffn2_norm_pre', 'ffn2_w_gate_up',
           'ffn2_w_down', 'ffn2_norm_post']
BIG = {'ffn1_w_gate_up': 'col', 'ffn1_w_down': 'row', 'w_in': 'col', 'w_out': 'row', 'w_xq': 'row',
       'w_xkv': 'col', 'w_xo': 'row', 'ffn2_w_gate_up': 'col', 'ffn2_w_down': 'row'}
BIG_NAMES = list(BIG)
GROUPS = (('ffn1', ('ffn1_w_gate_up', 'ffn1_w_down')), ('mix', ('w_in', 'w_out')),
          ('xa', ('w_xq', 'w_xkv', 'w_xo')), ('ffn2', ('ffn2_w_gate_up', 'ffn2_w_down')))
SMALL_NAMES = [n for n in WEIGHTS if n not in BIG]


def _pcall(body, **kw):
    return pl.pallas_call(body, **kw)


def _params(sem=None):
    return pltpu.CompilerParams(dimension_semantics=sem, vmem_limit_bytes=VMEM_LIMIT)


def _as_list(after):
    if after is None:
        return []
    return list(after) if isinstance(after, (list, tuple)) else [after]


def _divtile(dim, cap, align=LANES):
    if dim <= cap:
        return dim
    t = (cap // align) * align
    while t > align and dim % t:
        t -= align
    assert dim % t == 0, (dim, cap)
    return t


_DN = {'nn': (((1,), (0,)), ((), ())), 'nt': (((1,), (1,)), ((), ())), 'tn': (((0,), (0,)), ((), ()))}


MM_MIN_STEPS = 8
MM_VMEM_BUDGET = 44 * 1024 * 1024


def _mm_tiles(m, n, k, out_itemsize, acc=False):
    def tile(dim, cap):
        return _divtile(dim, cap, MXU_WIDTH) if dim % MXU_WIDTH == 0 else _divtile(dim, cap)

    tm = tile(m, 1024)
    for cap_n, cap_k in ((2048, 2048), (1024, 2048), (2048, 1024), (1024, 1024), (1024, 512), (512, 512)):
        tn, tk = tile(n, cap_n), _divtile(k, cap_k)
        need = 2 * 2 * (tm * tk + tk * tn) + 2 * tm * tn * out_itemsize + (tm * tn * 4 if tk < k or acc else 0)
        if need <= MM_VMEM_BUDGET:
            break
    while (m // tm) * (n // tn) * (k // tk) < MM_MIN_STEPS and tn > 2 * MXU_WIDTH and n % (tn // 2) == 0:
        tn //= 2
    return tm, tn, tk


def _mm(a, b, mode, out_dtype, name, halves=None):
    if halves == 'a':
        assert mode == 'nt'
        a_shape, b_shape = (a.shape[1], 2 * a.shape[2]), b.shape
    elif halves == 'b':
        assert mode == 'tn'
        a_shape, b_shape = a.shape, (b.shape[1], 2 * b.shape[2])
    else:
        a_shape, b_shape = a.shape, b.shape
    if mode == 'nn':
        (m, k), (k2, n) = a_shape, b_shape
    elif mode == 'nt':
        (m, k), (n, k2) = a_shape, b_shape
    else:
        (k, m), (k2, n) = a_shape, b_shape
    assert k == k2, (a.shape, b.shape, mode)
    if halves == 'a':
        tm, tn, tk = _mm_tiles(m, n, k // 2, jnp.dtype(out_dtype).itemsize, acc=True)
    elif halves == 'b':
        tm, tn, tk = _mm_tiles(m, n // 2, k, jnp.dtype(out_dtype).itemsize)
    else:
        tm, tn, tk = _mm_tiles(m, n, k, jnp.dtype(out_dtype).itemsize)
    nk = k // tk
    a_spec = {'nn': pl.BlockSpec((tm, tk), lambda i, j, kk: (i, kk)),
              'nt': pl.BlockSpec((tm, tk), lambda i, j, kk: (i, kk)),
              'tn': pl.BlockSpec((tk, tm), lambda i, j, kk: (kk, i))}[mode]
    b_spec = {'nn': pl.BlockSpec((tk, tn), lambda i, j, kk: (kk, j)),
              'nt': pl.BlockSpec((tn, tk), lambda i, j, kk: (j, kk)),
              'tn': pl.BlockSpec((tk, tn), lambda i, j, kk: (kk, j))}[mode]
    if halves == 'a':
        per = nk // 2
        a_spec = pl.BlockSpec((None, tm, tk), lambda i, j, kk: (kk // per, i, kk % per))
    elif halves == 'b':
        per = (n // tn) // 2
        b_spec = pl.BlockSpec((None, tk, tn), lambda i, j, kk: (j // per, kk, j % per))

    def dot(a_ref, b_ref):
        return lax.dot_general(a_ref[...].astype(BF16), b_ref[...].astype(BF16), _DN[mode], preferred_element_type=F32)

    def body_once(a_ref, b_ref, o_ref):
        o_ref[...] = dot(a_ref, b_ref).astype(o_ref.dtype)

    def body_steps(a_ref, b_ref, o_ref, acc):
        kk = pl.program_id(2)

        @pl.when(kk == 0)
        def _():
            acc[...] = jnp.zeros_like(acc)

        acc[...] += dot(a_ref, b_ref)

        @pl.when(kk == nk - 1)
        def _():
            o_ref[...] = acc[...].astype(o_ref.dtype)

    return _pcall(
        body_once if nk == 1 else body_steps, name=name, grid=(m // tm, n // tn, nk),
        in_specs=[a_spec, b_spec], out_specs=pl.BlockSpec((tm, tn), lambda i, j, kk: (i, j)),
        out_shape=jax.ShapeDtypeStruct((m, n), out_dtype),
        scratch_shapes=[] if nk == 1 else [pltpu.VMEM((tm, tn), F32)],
        compiler_params=_params(("parallel", "parallel", "arbitrary")),
    )(a, b)


def _rowcall(name, f, rows, params=(), row_outs=(), acc_outs=(), br=256, nrows=None, after=None):
    rows = [r if isinstance(r, tuple) else (r, 0) for r in rows]
    t = nrows if nrows is not None else rows[0][0].shape[0]
    br = min(br, t)
    assert t % br == 0, (name, t, br)
    n_in, n_ro = len(rows) + len(params), len(row_outs)
    extra = _as_list(after)

    def row_spec(arr, off):
        assert off % br == 0
        return pl.BlockSpec((br, arr.shape[1]), functools.partial(lambda i, o: (i + o, 0), o=off // br))

    def whole_spec(shape):
        return pl.BlockSpec(shape, functools.partial(lambda i, nd: (0,) * nd, nd=len(shape)))

    def body(*refs):
        res = f(*[r[...] for r in refs[:n_in]])
        outs = refs[n_in + len(extra):]
        for o_ref, val in zip(outs[:n_ro], res[:n_ro]):
            o_ref[...] = val.astype(o_ref.dtype)
        if acc_outs:
            @pl.when(pl.program_id(0) == 0)
            def _():
                for o_ref in outs[n_ro:]:
                    o_ref[...] = jnp.zeros_like(o_ref)

            for o_ref, val in zip(outs[n_ro:], res[n_ro:]):
                o_ref[...] += val.astype(F32)

    out = _pcall(
        body, name=name, grid=(t // br,),
        in_specs=[row_spec(a, o) for a, o in rows] + [whole_spec(p.shape) for p in params]
        + [pl.BlockSpec(memory_space=pl.ANY)] * len(extra),
        out_specs=[pl.BlockSpec((br, c), lambda i: (i, 0)) for c, _ in row_outs] + [whole_spec(s) for s in acc_outs],
        out_shape=[jax.ShapeDtypeStruct((t, c), dt) for c, dt in row_outs]
        + [jax.ShapeDtypeStruct(s, F32) for s in acc_outs],
        compiler_params=_params(("arbitrary",) if acc_outs else ("parallel",)),
    )(*[a for a, _ in rows], *params, *extra)
    return out


_DN_BATCH = {'nn': (((2,), (1,)), ((0,), (0,))), 'nt': (((2,), (2,)), ((0,), (0,))), 'tn': (((1,), (1,)), ((0,), (0,)))}


def _dims(a, dn):
    return _DN_BATCH[dn] if a.ndim == 3 else _DN[dn]


def _bdot_raw(a, b, dn):
    return lax.dot_general(a.astype(BF16), b.astype(BF16), _dims(a, dn), preferred_element_type=F32)


def _hdot_raw(a, b, dn):
    a_hi, b_hi = a.astype(BF16), b.astype(BF16)
    a_lo, b_lo = (a - a_hi.astype(F32)).astype(BF16), (b - b_hi.astype(F32)).astype(BF16)

    def dot(p, q):
        return lax.dot_general(p, q, _dims(a, dn), preferred_element_type=F32)

    return dot(a_hi, b_hi) + (dot(a_hi, b_lo) + dot(a_lo, b_hi))


def _with_vjp(raw):
    @functools.partial(jax.custom_vjp, nondiff_argnums=(2,))
    def dot(a, b, dn):
        return raw(a, b, dn)

    def fwd(a, b, dn):
        return raw(a, b, dn), (a, b)

    def bwd(dn, res, ct):
        a, b = res
        if dn == 'nn':
            da, db = raw(ct, b, 'nt'), raw(a, ct, 'tn')
        elif dn == 'nt':
            da, db = raw(ct, b, 'nn'), raw(ct, a, 'tn')
        else:
            da, db = raw(b, ct, 'nt'), raw(a, ct, 'nn')
        return da.astype(a.dtype), db.astype(b.dtype)

    dot.defvjp(fwd, bwd)
    return dot


_bdot, _hdot = _with_vjp(_bdot_raw), _with_vjp(_hdot_raw)


def _rms(x, g):
    return x * lax.rsqrt(jnp.mean(x * x, axis=-1, keepdims=True) + NORM_EPS) * g


def _sigmoid(x):
    return 1.0 / (1.0 + jnp.exp(-x))


def _silu(x):
    return x * _sigmoid(x)


def _gelu(x):
    return 0.5 * x * (1.0 + lax.erf(x * (2.0 ** -0.5)))


def _norm_fwd(x, g, name, after=None):
    return _rowcall(name, lambda x_, g_: (_rms(x_, g_),), [x], [g], [(x.shape[1], BF16)], after=after)[0]


def _resnorm_fwd(x, f, g, alpha, name):
    return _rowcall(name, lambda x_, f_, g_: (x_ + alpha * _rms(f_, g_),), [x, f], [g], [(x.shape[1], F32)])[0]


def _resnorm_bwd(f, dx, g, alpha, name, after=None):
    def fn(f_, dx_, g_):
        _, vjp = jax.vjp(lambda a, b: alpha * _rms(a, b), f_, g_)
        return vjp(dx_)

    return _rowcall(name, fn, [f, dx], [g], [(f.shape[1], BF16)], [g.shape], after=after)


def _norm_bwd(x, dh, dx_out, g, name, after=None):
    def fn(x_, dh_, dxo_, g_):
        _, vjp = jax.vjp(_rms, x_, g_)
        dx, dg = vjp(dh_)
        return dxo_ + dx, dg

    return _rowcall(name, fn, [x, dh, dx_out], [g], [(x.shape[1], F32)], [g.shape], after=after)


def _norm_bwd_gain(x, dh, g, name):
    def fn(x_, dh_, g_):
        _, vjp = jax.vjp(lambda b: _rms(x_, b), g_)
        return vjp(dh_)

    return _rowcall(name, fn, [x, dh], [g], [], [g.shape])[0]


def _ffn_up(h, w_gu, name):
    t, d = h.shape
    ff = w_gu.shape[1] // 2
    tm, tn = _divtile(t, 1024), _divtile(ff, 512, MXU_WIDTH)
    nj = ff // tn

    def body(h_ref, wg_ref, wu_ref, gu_ref, a_ref):
        hv = h_ref[...]
        gate = jnp.dot(hv, wg_ref[...], preferred_element_type=F32)
        up = jnp.dot(hv, wu_ref[...], preferred_element_type=F32)
        gu_ref[0], gu_ref[1] = gate, up
        a_ref[...] = (_silu(gate) * up).astype(a_ref.dtype)

    return _pcall(body, name=name, grid=(t // tm, nj),
                  in_specs=[pl.BlockSpec((tm, d), lambda i, j: (i, 0)), pl.BlockSpec((d, tn), lambda i, j: (0, j)),
                            pl.BlockSpec((d, tn), lambda i, j: (0, j + nj))],
                  out_specs=[pl.BlockSpec((2, tm, tn), lambda i, j: (0, i, j)), pl.BlockSpec((tm, tn), lambda i, j: (i, j))],
                  out_shape=[jax.ShapeDtypeStruct((2, t, ff), F32), jax.ShapeDtypeStruct((t, ff), BF16)],
                  compiler_params=_params(("parallel", "parallel")))(h, w_gu, w_gu)


def _ffn_down_bwd(df, w_d, gu, name):
    t, d = df.shape
    ff = w_d.shape[0]
    tm, tn = _divtile(t, 1024), _divtile(ff, 512, MXU_WIDTH)

    def body(df_ref, w_ref, gu_ref, o_ref):
        da = lax.dot_general(df_ref[...], w_ref[...], _DN['nt'], preferred_element_type=F32)
        gate, up = gu_ref[0], gu_ref[1]
        s = _sigmoid(gate)
        o_ref[0] = (da * up * (s * (1.0 + gate * (1.0 - s)))).astype(o_ref.dtype)
        o_ref[1] = (da * (gate * s)).astype(o_ref.dtype)

    blk = pl.BlockSpec((2, tm, tn), lambda i, j: (0, i, j))
    return _pcall(body, name=name, grid=(t // tm, ff // tn),
                  in_specs=[pl.BlockSpec((tm, d), lambda i, j: (i, 0)), pl.BlockSpec((tn, d), lambda i, j: (j, 0)), blk],
                  out_specs=blk, out_shape=jax.ShapeDtypeStruct((2, t, ff), BF16),
                  compiler_params=_params(("parallel", "parallel")))(df, w_d, gu)


def _sgu_norm(v, lg, lb):
    vf = _gelu(v)
    mu = jnp.mean(vf, axis=-1, keepdims=True)
    var = jnp.mean(jnp.square(vf - mu), axis=-1, keepdims=True)
    return (vf - mu) * lax.rsqrt(var + NORM_EPS) * lg + lb


def _sgu_head(dot, u_h, vn_h, w_h, b_h):
    r = lax.broadcasted_iota(jnp.int32, w_h.shape, 0)
    c = lax.broadcasted_iota(jnp.int32, w_h.shape, 1)
    mixed = dot(jnp.where(r >= c, w_h, 0.0), vn_h, 'nn') + b_h
    return _gelu(u_h) * mixed


def _heads(width):
    return [slice(h * HEAD_DIM, (h + 1) * HEAD_DIM) for h in range(width // HEAD_DIM)]


def _sgu_fwd(u, v, w, bcol, lg, lb, name):
    def fn(u_, v_, w_, b_, lg_, lb_):
        vn = _sgu_norm(v_, lg_, lb_)
        return (jnp.concatenate([_sgu_head(_bdot_raw, u_[:, s], vn[:, s], w_[h], b_[h])
                                 for h, s in enumerate(_heads(u_.shape[1]))], axis=1),)

    return _rowcall(name, fn, [u, v], [w, bcol, lg, lb], [(u.shape[1], BF16)], br=GM_CHUNK)[0]


def _sgu_bwd(u, v, w, bcol, lg, lb, dy, name):
    def fn(u_, v_, dy_, w_, b_, lg_, lb_):
        vn, vjp_norm = jax.vjp(_sgu_norm, v_, lg_, lb_)
        du, dvn, dw, db = [], [], [], []
        for h, s in enumerate(_heads(u_.shape[1])):
            _, vjp = jax.vjp(functools.partial(_sgu_head, _bdot), u_[:, s], vn[:, s], w_[h], b_[h])
            a, b, c, d = vjp(dy_[:, s])
            du.append(a), dvn.append(b), dw.append(c[None]), db.append(d[None])
        dv, dlg, dlb = vjp_norm(jnp.concatenate(dvn, axis=1))
        return (jnp.concatenate(du, axis=1), dv, jnp.concatenate(dw, axis=0), jnp.concatenate(db, axis=0), dlg, dlb)

    wd = u.shape[1]
    return _rowcall(name, fn, [u, v, dy], [w, bcol, lg, lb], [(wd, BF16), (wd, BF16)],
                    [w.shape, bcol.shape, lg.shape, lb.shape], br=GM_CHUNK)


def _shift_down(x, s):
    if s == 0:
        return x
    rows = lax.broadcasted_iota(jnp.int32, x.shape, 0)
    return jnp.where(rows >= s, pltpu.roll(x, s, 0), 0.0)


def _shift_up(x, s):
    if s == 0:
        return x
    t = x.shape[0]
    rows = lax.broadcasted_iota(jnp.int32, x.shape, 0)
    return jnp.where(rows < t - s, pltpu.roll(x, t - s, 0), 0.0)


def _conv_pre(x, taps):
    acc = None
    for i in range(CONV_WIDTH):
        term = _shift_down(x, CONV_WIDTH - 1 - i) * taps[i]
        acc = term if acc is None else acc + term
    return acc


def _taps(w_ref):
    return [w_ref[i:i + 1, :] for i in range(CONV_WIDTH)]


def _conv_fwd(x, w, name):
    t, ch = x.shape
    cb = _divtile(ch, 512)

    def body(x_ref, w_ref, o_ref):
        o_ref[...] = _silu(_conv_pre(x_ref[...], _taps(w_ref)))

    return _pcall(body, name=name, grid=(ch // cb,),
                  in_specs=[pl.BlockSpec((t, cb), lambda j: (0, j)), pl.BlockSpec((CONV_WIDTH, cb), lambda j: (0, j))],
                  out_specs=pl.BlockSpec((t, cb), lambda j: (0, j)),
                  out_shape=jax.ShapeDtypeStruct((t, ch), F32), compiler_params=_params(("parallel",)))(x, w)


def _conv_bwd(x, w, dout, col_off, name):
    t, ch = dout.shape
    cb = _divtile(ch, 256)
    assert col_off % cb == 0
    off = col_off // cb

    def body(x_ref, w_ref, do_ref, dx_ref, dw_ref):
        xv, taps = x_ref[...], _taps(w_ref)
        pre = _conv_pre(xv, taps)
        s = _sigmoid(pre)
        dpre = do_ref[...] * (s * (1.0 + pre * (1.0 - s)))
        dx = None
        for i in range(CONV_WIDTH):
            sh = CONV_WIDTH - 1 - i
            term = _shift_up(dpre, sh) * taps[i]
            dx = term if dx is None else dx + term
            dw_ref[i:i + 1, :] = jnp.sum(dpre * _shift_down(xv, sh), axis=0, keepdims=True)
        dx_ref[...] = dx.astype(dx_ref.dtype)

    return _pcall(body, name=name, grid=(ch // cb,),
                  in_specs=[pl.BlockSpec((t, cb), lambda j: (0, j + off)),
                            pl.BlockSpec((CONV_WIDTH, cb), lambda j: (0, j + off)),
                            pl.BlockSpec((t, cb), lambda j: (0, j))],
                  out_specs=[pl.BlockSpec((t, cb), lambda j: (0, j)), pl.BlockSpec((CONV_WIDTH, cb), lambda j: (0, j))],
                  out_shape=[jax.ShapeDtypeStruct((t, ch), BF16), jax.ShapeDtypeStruct((CONV_WIDTH, ch), F32)],
                  compiler_params=_params(("parallel",)))(x, w, dout)


GATE_BLOCK = 256


def _gates(dot, b_raw, a_raw, a_log, dt_bias):
    blk = b_raw.shape[1]
    z = a_raw + dt_bias
    softplus = jnp.maximum(z, 0.0) + jnp.log(1.0 + jnp.exp(-jnp.abs(z)))
    g = -jnp.exp(a_log) * softplus
    ri = lax.broadcasted_iota(jnp.int32, (blk, blk), 0)
    ci = lax.broadcasted_iota(jnp.int32, (blk, blk), 1)
    upto = ((ri <= ci) & (ri // DN_CHUNK == ci // DN_CHUNK)).astype(F32)
    return _sigmoid(b_raw), dot(g, upto, 'nn')


def _gate_blocks(t):
    blk = min(GATE_BLOCK, t)
    return [slice(i, i + blk) for i in range(0, t, blk)]


def _whole_call(name, f, ins, out_shapes):
    n_in = len(ins)

    def body(*refs):
        for o_ref, val in zip(refs[n_in:], f(*[r[...] for r in refs[:n_in]])):
            o_ref[...] = val

    vmem = pl.BlockSpec(memory_space=pltpu.VMEM)
    return _pcall(body, name=name, in_specs=[vmem] * n_in, out_specs=[vmem] * len(out_shapes),
                  out_shape=[jax.ShapeDtypeStruct(s, F32) for s in out_shapes],
                  compiler_params=pltpu.CompilerParams(vmem_limit_bytes=VMEM_LIMIT))(*ins)


def _gates_fwd(b_raw, a_raw, a_log, dt_bias, name):
    def fn(b_, a_, al_, dt_):
        parts = [_gates(_hdot_raw, b_[:, s], a_[:, s], al_, dt_) for s in _gate_blocks(b_.shape[1])]
        return [jnp.concatenate(p, axis=1) for p in zip(*parts)]

    return _whole_call(name, fn, [b_raw, a_raw, a_log, dt_bias], [b_raw.shape, b_raw.shape])


def _gates_bwd(b_raw, a_raw, a_log, dt_bias, dbeta, dgcum, name):
    def fn(b_, a_, al_, dt_, dbeta_, dgcum_):
        db, da, dal, ddt = [], [], None, None
        for s in _gate_blocks(b_.shape[1]):
            _, vjp = jax.vjp(functools.partial(_gates, _hdot), b_[:, s], a_[:, s], al_, dt_)
            p, q, r, u = vjp((dbeta_[:, s], dgcum_[:, s]))
            db.append(p), da.append(q)
            dal, ddt = (r, u) if dal is None else (dal + r, ddt + u)
        return jnp.concatenate(db, axis=1), jnp.concatenate(da, axis=1), dal, ddt

    return _whole_call(name, fn, [b_raw, a_raw, a_log, dt_bias, dbeta, dgcum],
                       [b_raw.shape, a_raw.shape, a_log.shape, dt_bias.shape])


def _unit_lower_inverse_raw(a):
    c = a.shape[-1]
    eye = (lax.broadcasted_iota(jnp.int32, (c, c), 0) == lax.broadcasted_iota(jnp.int32, (c, c), 1)).astype(F32)
    inv, p = eye - a, _hdot_raw(a, a, 'nn')
    n_fac = int(math.log2(c)) - 1
    for it in range(n_fac):
        inv = inv + _hdot_raw(inv, p, 'nn')
        if it < n_fac - 1:
            p = _hdot_raw(p, p, 'nn')
    return inv


@jax.custom_vjp
def _unit_lower_inverse(a):
    return _unit_lower_inverse_raw(a)


def _unit_lower_inverse_fwd(a):
    inv = _unit_lower_inverse_raw(a)
    return inv, inv


def _unit_lower_inverse_bwd(inv, ct):
    return (-_hdot_raw(_hdot_raw(inv, ct, 'tn'), inv, 'nt'),)


_unit_lower_inverse.defvjp(_unit_lower_inverse_fwd, _unit_lower_inverse_bwd)


def _halves_raw(x):
    return x[..., :x.shape[-1] // 2], x[..., x.shape[-1] // 2:]


@jax.custom_vjp
def _halves(x):
    return _halves_raw(x)


_halves.defvjp(lambda x: (_halves_raw(x), None), lambda _, ct: (jnp.concatenate(ct, axis=-1),))

_DELTA_OPS = (_bdot_raw, _hdot_raw, _unit_lower_inverse_raw, _halves_raw)
_DELTA_OPS_VJP = (_bdot, _hdot, _unit_lower_inverse, _halves)


def _delta_chunk(ops, state, q, k, v, gc, gr, bc):
    bd, hd, inverse, halves = ops
    c, d = q.shape[-2:]
    ri = lax.broadcasted_iota(jnp.int32, (c, c), 0)
    ci = lax.broadcasted_iota(jnp.int32, (c, c), 1)
    causal, strict = ri >= ci, ri > ci
    qn = q * lax.rsqrt(jnp.sum(q * q, axis=-1, keepdims=True) + NORM_EPS) * (d ** -0.5)
    kn = k * lax.rsqrt(jnp.sum(k * k, axis=-1, keepdims=True) + NORM_EPS)
    gcum = jnp.broadcast_to(gc, q.shape)
    decay = jnp.where(causal, jnp.exp(jnp.where(causal, gc - gr, 0.0)), 0.0)
    bb = jnp.broadcast_to(bc, q.shape)
    k_beta = kn * bb
    inv = inverse(jnp.where(strict, bd(k_beta, kn, 'nt') * decay, 0.0))
    uw = hd(inv, jnp.concatenate([v * bb, k_beta * jnp.exp(gcum)], axis=-1), 'nn')
    u, w = halves(uw)
    qk = jnp.where(causal, bd(qn, kn, 'nt') * decay, 0.0)
    v_new = u - bd(w, state, 'nn')
    o = bd(qn * jnp.exp(gcum), state, 'nn') + bd(qk, v_new, 'nn')
    last = lax.broadcasted_iota(jnp.int32, (c, d), 0) == c - 1
    g_last = jnp.sum(jnp.where(last, gcum, 0.0), axis=-2, keepdims=True)
    k_dec = kn * jnp.exp(g_last - gcum)
    return state * jnp.exp(g_last) + bd(k_dec, v_new, 'tn'), o


DN_HEADS_PER_STEP = 8


def _by_head(ref):
    return jnp.stack([ref[:, s] for s in _heads(ref.shape[1])])


def _delta_specs(nh, nc, rev):
    c, d = DN_CHUNK, HEAD_DIM
    hb = min(DN_HEADS_PER_STEP, nh)
    ng = nh // hb
    ch = (lambda n: nc - 1 - n) if rev else (lambda n: n)
    qkv = [pl.BlockSpec((c, hb * d), functools.partial(lambda h, n, o: (ch(n), o * ng + h), o=o)) for o in range(3)]
    col = pl.BlockSpec((hb, c, 1), lambda h, n: (h, ch(n), 0))
    row = pl.BlockSpec((hb, None, 1, c), lambda h, n: (h, ch(n), 0, 0))
    st = pl.BlockSpec((hb, None, d, d), lambda h, n: (h, ch(n), 0, 0))
    tok = pl.BlockSpec((c, hb * d), lambda h, n: (ch(n), h))
    return hb, ng, qkv, col, row, st, tok


def _delta_fwd(qkv, gcol, grow, bcol, name):
    t = qkv.shape[0]
    nh, nc, d = gcol.shape[0], t // DN_CHUNK, HEAD_DIM
    hb, ng, qkv_s, col, row, st, tok = _delta_specs(nh, nc, False)

    def body(q_ref, k_ref, v_ref, gc_ref, gr_ref, bc_ref, o_ref, st_ref, state):
        @pl.when(pl.program_id(1) == 0)
        def _():
            state[...] = jnp.zeros_like(state)

        s0 = state[...]
        st_ref[...] = s0
        s1, o = _delta_chunk(_DELTA_OPS, s0, _by_head(q_ref), _by_head(k_ref), _by_head(v_ref), gc_ref[...],
                             gr_ref[...], bc_ref[...])
        for j, s in enumerate(_heads(hb * d)):
            o_ref[:, s] = o[j]
        state[...] = s1

    return _pcall(body, name=name, grid=(ng, nc), in_specs=qkv_s + [col, row, col], out_specs=[tok, st],
                  out_shape=[jax.ShapeDtypeStruct((t, nh * d), F32), jax.ShapeDtypeStruct((nh, nc, d, d), F32)],
                  scratch_shapes=[pltpu.VMEM((hb, d, d), F32)],
                  compiler_params=_params(("parallel", "arbitrary")))(qkv, qkv, qkv, gcol, grow, bcol)


def _delta_bwd(qkv, gcol, grow, bcol, states, do, name):
    t = qkv.shape[0]
    nh, nc, d = gcol.shape[0], t // DN_CHUNK, HEAD_DIM
    hb, ng, qkv_s, col, row, st, tok = _delta_specs(nh, nc, True)

    def body(q_ref, k_ref, v_ref, gc_ref, gr_ref, bc_ref, st_ref, do_ref,
             dq_ref, dk_ref, dv_ref, dgc_ref, dgr_ref, dbc_ref, dstate):
        @pl.when(pl.program_id(1) == 0)
        def _():
            dstate[...] = jnp.zeros_like(dstate)

        _, vjp = jax.vjp(functools.partial(_delta_chunk, _DELTA_OPS_VJP), st_ref[...], _by_head(q_ref), _by_head(k_ref),
                         _by_head(v_ref), gc_ref[...], gr_ref[...], bc_ref[...])
        ds, dq, dk, dv, dgc, dgr, dbc = vjp((dstate[...], _by_head(do_ref)))
        for j, s in enumerate(_heads(hb * d)):
            dq_ref[:, s], dk_ref[:, s], dv_ref[:, s] = dq[j], dk[j], dv[j]
        dgc_ref[...], dgr_ref[...], dbc_ref[...] = dgc, dgr, dbc
        dstate[...] = ds

    tok_out = jax.ShapeDtypeStruct((t, nh * d), F32)
    return _pcall(body, name=name, grid=(ng, nc), in_specs=qkv_s + [col, row, col, st, tok],
                  out_specs=[tok, tok, tok, col, row, col],
                  out_shape=[tok_out, tok_out, tok_out, jax.ShapeDtypeStruct(gcol.shape, F32),
                             jax.ShapeDtypeStruct(grow.shape, F32), jax.ShapeDtypeStruct(bcol.shape, F32)],
                  scratch_shapes=[pltpu.VMEM((hb, d, d), F32)],
                  compiler_params=_params(("parallel", "arbitrary")))(qkv, qkv, qkv, gcol, grow, bcol, states, do)


def _outgate_head(o_h, z_h, w):
    return _rms(o_h, w) * _silu(z_h)


def _outgate_fwd(o, z, w, name):
    def fn(o_, z_, w_):
        return (jnp.concatenate([_outgate_head(o_[:, s], z_[:, s], w_) for s in _heads(o_.shape[1])], axis=1),)

    return _rowcall(name, fn, [o, z], [w], [(o.shape[1], BF16)])[0]


def _outgate_bwd(o, z, w, dy, name):
    def fn(o_, z_, dy_, w_):
        do, dz, dw = [], [], None
        for s in _heads(o_.shape[1]):
            _, vjp = jax.vjp(_outgate_head, o_[:, s], z_[:, s], w_)
            a, b, c = vjp(dy_[:, s])
            do.append(a), dz.append(b)
            dw = c if dw is None else dw + c
        return jnp.concatenate(do, axis=1), jnp.concatenate(dz, axis=1), dw

    wd = o.shape[1]
    return _rowcall(name, fn, [o, z, dy], [w], [(wd, F32), (wd, BF16)], [w.shape])


def _attn_head(dot, q_h, k_h, v_h):
    s = dot(q_h, k_h, 'nt') * (q_h.shape[1] ** -0.5)
    e = jnp.exp(s - lax.stop_gradient(jnp.max(s, axis=-1, keepdims=True)))
    p = e / jnp.sum(e, axis=-1, keepdims=True)
    return dot(p, v_h, 'nn')


def _xa_slices(width):
    hd = width // XA_HEADS
    return [slice(h * hd, (h + 1) * hd) for h in range(XA_HEADS)]


def _attn_fwd(q, kv, name):
    wd = q.shape[1]

    def fn(q_, kv_):
        k_, v_ = kv_[:, :wd], kv_[:, wd:]
        return (jnp.concatenate([_attn_head(_bdot_raw, q_[:, s], k_[:, s], v_[:, s]) for s in _xa_slices(wd)],
                                axis=1),)

    return _rowcall(name, fn, [q], [kv], [(wd, BF16)])[0]


def _attn_bwd(q, kv, do, name):
    wd = q.shape[1]

    def fn(q_, do_, kv_):
        k_, v_ = kv_[:, :wd].astype(F32), kv_[:, wd:].astype(F32)
        dq, dk, dv = [], [], []
        for s in _xa_slices(wd):
            _, vjp = jax.vjp(functools.partial(_attn_head, _bdot), q_[:, s].astype(F32), k_[:, s], v_[:, s])
            a, b, c = vjp(do_[:, s])
            dq.append(a), dk.append(b), dv.append(c)
        return jnp.concatenate(dq, axis=1), jnp.concatenate(dk + dv, axis=1)

    return _rowcall(name, fn, [q, do], [kv], [(wd, BF16)], [kv.shape])


def _loss_call(y, target, name):
    d = y.shape[1]

    def fn(y_, t_):
        err = y_ - t_
        part = 0.5 * jnp.sum(jnp.sum(err * err, axis=1, keepdims=True), axis=0, keepdims=True) / d
        return err / d, jnp.broadcast_to(part, (1, LANES))

    return _rowcall(name, fn, [y, target], [], [(d, F32)], [(1, LANES)])


def _adamw_call(w, g, m, v, name):
    c = w.shape[1]

    def fn(w_, g_, m_, v_):
        g_ = g_[:, :c]
        m1 = ADAM_B1 * m_ + (1.0 - ADAM_B1) * g_
        v1 = ADAM_B2 * v_ + (1.0 - ADAM_B2) * jnp.square(g_)
        m_hat = m1 / (1.0 - ADAM_B1 ** ADAM_STEP)
        v_hat = v1 / (1.0 - ADAM_B2 ** ADAM_STEP)
        return g_, -ADAM_LR * (m_hat / (jnp.sqrt(v_hat) + ADAM_EPS) + ADAM_WD * w_), m1, v1

    return _rowcall(name, fn, [w, g, m, v], [], [(c, F32)] * 4, br=_divtile(w.shape[0], 128, 8))


STREAM_BLOCK_BYTES = 4 * 1024 * 1024


def _rows_for(cols, itemsize):
    return max(16, STREAM_BLOCK_BYTES // (cols * itemsize) // 16 * 16)


def _pair_add(grad, theirs, core, name):
    s, hr, cols = theirs.shape
    br = _divtile(hr, _rows_for(cols, 2), 16)
    nb = hr // br

    def body(c_ref, g_ref, t_ref, o_ref):
        o_ref[...] = (g_ref[...].astype(F32) + t_ref[...].astype(F32)).astype(o_ref.dtype)

    spec = pl.BlockSpec((None, br, cols), lambda j, i, c_ref: (j, i, 0))
    return _pcall(body, name=name, out_shape=jax.ShapeDtypeStruct(theirs.shape, BF16),
                  grid_spec=pltpu.PrefetchScalarGridSpec(
                      num_scalar_prefetch=1, grid=(s, nb),
                      in_specs=[pl.BlockSpec((None, br, cols), lambda j, i, c_ref: (j, c_ref[0] * nb + i, 0)), spec],
                      out_specs=spec),
                  compiler_params=_params(("parallel", "parallel")))(core, grad, theirs)


def _chip_sum(part, landed, kind, where, grad, layer, name):
    _, hr, cw = landed.shape
    br = _divtile(hr, _rows_for(cw, 4), 16)
    nb = hr // br

    def body(w_ref, p_ref, a_ref, b_ref, d_ref, g_ref, o_ref):
        o_ref[...] = ((p_ref[...].astype(F32) + a_ref[...].astype(F32)) + b_ref[...].astype(F32)) + d_ref[...].astype(F32)

    if kind == 'row':
        own = pl.BlockSpec((None, br, cw), lambda i, w_ref: (w_ref[0], i, 0))
    else:
        own = pl.BlockSpec((br, cw), lambda i, w_ref: (i, w_ref[0]))
    got = [pl.BlockSpec((None, br, cw), functools.partial(lambda i, w_ref, k: (k, i, 0), k=k)) for k in range(3)]
    return _pcall(body, name=name, out_shape=jax.ShapeDtypeStruct(grad.shape, F32),
                  grid_spec=pltpu.PrefetchScalarGridSpec(
                      num_scalar_prefetch=1, grid=(nb,), in_specs=[own] + got + [ANY_SPEC],
                      out_specs=pl.BlockSpec((None, br, cw), lambda i, w_ref: (layer, w_ref[1] * nb + i, 0))),
                  input_output_aliases={5: 0},
                  compiler_params=_params(("parallel",)))(where, part, landed, landed, landed, grad)


def _position():
    x, y, c = lax.axis_index("x"), lax.axis_index("y"), lax.axis_index("c")
    others = [(1 - x, y), (x, 1 - y), (1 - x, 1 - y)]
    return x, y, c, others


def _any_specs(n):
    return [pl.BlockSpec(memory_space=pl.ANY)] * n


def _ds(start, size):
    return pl.ds(pl.multiple_of(start, size), size)


HBM_SPEC = pl.BlockSpec(memory_space=pltpu.HBM)
SEM_SPEC = pl.BlockSpec(memory_space=pltpu.SEMAPHORE)
ANY_SPEC = pl.BlockSpec(memory_space=pl.ANY)
DATAFLOW = pltpu.SideEffectType.DATAFLOW_SIDE_EFFECTING


def _hbm(a):
    return pltpu.with_memory_space_constraint(a, pltpu.HBM)


def _full_shape(shard_shape, kind):
    r, cw = shard_shape
    return (N_CHIPS, r, cw) if kind == 'row' else (r, N_CHIPS * cw)


def _block_dims(full, kind):
    return (full.shape[1], full.shape[2]) if kind == 'row' else (full.shape[0], full.shape[1] // N_CHIPS)


def _region(full, kind, chip, half):
    if kind == 'all':
        return full.at[chip]
    r, cw = _block_dims(full, kind)
    if kind == 'row':
        return full.at[chip, _ds(half * (r // 2), r // 2), :]
    return full.at[_ds(half * (r // 2), r // 2), _ds(chip * cw, cw)]


def _place_block(full, shards, layer, kind, chip, name, after=None):
    _, r, cs = shards.shape
    cw = _block_dims(full, kind)[1]
    br = _divtile(r, 256, 16)

    def body(c_ref, s_ref, *rest):
        o_ref = rest[-1]
        if cs == cw:
            o_ref[...] = s_ref[...].astype(o_ref.dtype)
        else:
            o_ref[:, :cs] = s_ref[...].astype(o_ref.dtype)
            o_ref[:, cs:] = jnp.zeros((br, cw - cs), o_ref.dtype)

    if kind == 'row':
        out_spec = pl.BlockSpec((None, br, cw), lambda i, c_ref: (c_ref[0], i, 0))
    else:
        out_spec = pl.BlockSpec((br, cw), lambda i, c_ref: (i, c_ref[0]))
    extra = _as_list(after)
    return _pcall(body, name=name, out_shape=jax.ShapeDtypeStruct(full.shape, full.dtype),
                  grid_spec=pltpu.PrefetchScalarGridSpec(
                      num_scalar_prefetch=1, grid=(r // br,),
                      in_specs=[pl.BlockSpec((None, br, cs), lambda i, c_ref: (layer, i, 0)), ANY_SPEC]
                      + [ANY_SPEC] * len(extra), out_specs=out_spec),
                  input_output_aliases={2: 0}, compiler_params=_params(("parallel",)))(chip, shards, full, *extra)


def _split_copy_start(bufs, lands, plan, after, name):
    nb, nl = len(bufs), len(lands)
    extra = _as_list(after)

    def body(*refs):
        ins = refs[:nb + nl]
        send, recv = refs[nb + nl + len(extra)], refs[nb + nl + len(extra) + 1]
        for cp in plan(ins[:nb], ins[nb:], send, recv):
            cp.start()
        refs[-1][...] = jnp.zeros_like(refs[-1])

    n_copies = plan.n_copies
    out = _pcall(
        body, name=name,
        out_shape=(pltpu.SemaphoreType.DMA((n_copies,)), pltpu.SemaphoreType.DMA((n_copies,)),
                   *[pltpu.HBM(a.shape, a.dtype) for a in (*bufs, *lands)], jax.ShapeDtypeStruct((8, LANES), F32)),
        in_specs=[HBM_SPEC] * (nb + nl) + [ANY_SPEC] * len(extra),
        out_specs=(SEM_SPEC, SEM_SPEC, *[HBM_SPEC] * (nb + nl), pl.BlockSpec(memory_space=pltpu.VMEM)),
        input_output_aliases={i: 2 + i for i in range(nb + nl)},
        compiler_params=pltpu.CompilerParams(has_side_effects=DATAFLOW),
    )(*[_hbm(a) for a in (*bufs, *lands)], *extra)
    return out[0], out[1], list(out[2:2 + nb]), list(out[2 + nb:2 + nb + nl]), out[-1]


def _split_copy_wait(send, recv, bufs, lands, plan, after, name):
    nb, nl = len(bufs), len(lands)
    extra = _as_list(after)

    def body(*refs):
        ins = refs[:nb + nl]
        send_ref, recv_ref = refs[nb + nl], refs[nb + nl + 1]
        for cp in plan(ins[:nb], ins[nb:], send_ref, recv_ref):
            cp.wait_send()
            cp.wait_recv()

    out = _pcall(
        body, name=name, out_shape=tuple(pltpu.HBM(a.shape, a.dtype) for a in (*bufs, *lands)),
        in_specs=[HBM_SPEC] * (nb + nl) + [SEM_SPEC, SEM_SPEC] + [ANY_SPEC] * len(extra),
        out_specs=tuple([HBM_SPEC] * (nb + nl)), input_output_aliases={i: i for i in range(nb + nl)},
        compiler_params=pltpu.CompilerParams(has_side_effects=DATAFLOW),
    )(*bufs, *lands, send, recv, *extra)
    return list(out[:nb]), list(out[nb:])


def _gather_plan(kinds):
    def plan(bufs, lands, send, recv):
        x, y, c, others = _position()
        me = 2 * x + y
        return [pltpu.make_async_remote_copy(
            src_ref=_region(lands[w], kinds[w], me, c), dst_ref=_region(lands[w], kinds[w], me, c),
            send_sem=send.at[3 * w + k], recv_sem=recv.at[3 * w + k], device_id=(px, py, c), device_id_type=MESH_IDS)
            for w in range(len(lands)) for k, (px, py) in enumerate(others)]

    def wait_plan(bufs, lands, send, recv):
        x, y, c, others = _position()
        me = 2 * x + y
        return [pltpu.make_async_remote_copy(
            src_ref=_region(lands[w], kinds[w], me, c), dst_ref=_region(lands[w], kinds[w], 2 * px + py, c),
            send_sem=send.at[3 * w + k], recv_sem=recv.at[3 * w + k], device_id=(px, py, c), device_id_type=MESH_IDS)
            for w in range(len(lands)) for k, (px, py) in enumerate(others)]

    plan.n_copies = wait_plan.n_copies = 3 * len(kinds)
    return plan, wait_plan


def _gather_forward(fulls, kinds, name):
    n = len(fulls)

    def body(*refs):
        dst = refs[n:2 * n]
        send, recv = refs[2 * n:]
        x, y, c, others = _position()
        sib = (x, y, 1 - c)

        def copy(w, k, chip, half):
            reg = _region(dst[w], kinds[w], chip, half)
            return pltpu.make_async_remote_copy(src_ref=reg, dst_ref=reg, send_sem=send.at[3 * w + k],
                                                recv_sem=recv.at[3 * w + k], device_id=sib, device_id_type=MESH_IDS)

        give = [copy(w, k, 2 * px + py, c) for w in range(n) for k, (px, py) in enumerate(others)]
        for cp in give:
            cp.start()
        for w in range(n):
            for k, (px, py) in enumerate(others):
                copy(w, k, 2 * px + py, 1 - c).wait_recv()
        for cp in give:
            cp.wait_send()

    return _pcall(body, name=name, in_specs=_any_specs(n), out_specs=_any_specs(n),
                  out_shape=[jax.ShapeDtypeStruct(f.shape, f.dtype) for f in fulls],
                  input_output_aliases={i: i for i in range(n)},
                  scratch_shapes=[pltpu.SemaphoreType.DMA((3 * n,)), pltpu.SemaphoreType.DMA((3 * n,))])(*fulls)


def _grad_half(ref, kind, half):
    if kind == 'row':
        hr = ref.shape[1] // 2
        return ref.at[:, _ds(half * hr, hr), :]
    hr = ref.shape[0] // 2
    return ref.at[_ds(half * hr, hr), :]


def _half_shape(g, kind):
    return (g.shape[0], g.shape[1] // 2, g.shape[2]) if kind == 'row' else (g.shape[0] // 2, g.shape[1])


def _pair_plan(kinds):
    def plan(bufs, lands, send, recv):
        x, y, c, _ = _position()
        return [pltpu.make_async_remote_copy(src_ref=_grad_half(bufs[w], kinds[w], 1 - c), dst_ref=lands[w],
                                             send_sem=send.at[w], recv_sem=recv.at[w], device_id=(x, y, 1 - c),
                                             device_id_type=MESH_IDS) for w in range(len(bufs))]

    plan.n_copies = len(kinds)
    return plan


def _pair_exchange(grads, kinds, name):
    n = len(grads)

    def body(*refs):
        src, theirs_ref = refs[:n], refs[n:2 * n]
        send, recv = refs[2 * n:]
        x, y, c, _ = _position()
        sib = (x, y, 1 - c)

        def half(w, h):
            if kinds[w] == 'row':
                hr = src[w].shape[1] // 2
                return src[w].at[:, _ds(h * hr, hr), :]
            hr = src[w].shape[0] // 2
            return src[w].at[_ds(h * hr, hr), :]

        give = [pltpu.make_async_remote_copy(src_ref=half(w, 1 - c), dst_ref=theirs_ref[w], send_sem=send.at[w],
                                             recv_sem=recv.at[w], device_id=sib, device_id_type=MESH_IDS)
                for w in range(n)]
        for cp in give:
            cp.start()
        for cp in give:
            cp.wait()

    def half_shape(g, kind):
        return (g.shape[0], g.shape[1] // 2, g.shape[2]) if kind == 'row' else (g.shape[0] // 2, g.shape[1])

    return _pcall(body, name=name, in_specs=_any_specs(n), out_specs=_any_specs(n),
                  out_shape=[jax.ShapeDtypeStruct(half_shape(g, kd), g.dtype) for g, kd in zip(grads, kinds)],
                  scratch_shapes=[pltpu.SemaphoreType.DMA((n,)), pltpu.SemaphoreType.DMA((n,))])(*grads)


def _part_dims(part, kind):
    return (part.shape[1], part.shape[2]) if kind == 'row' else (part.shape[0], part.shape[1] // N_CHIPS)


def _chip_plan(kinds):
    def plan(bufs, lands, send, recv):
        x, y, c, others = _position()

        def slot(w, chip):
            if kinds[w] == 'row':
                return bufs[w].at[chip]
            cw = _part_dims(bufs[w], kinds[w])[1]
            return bufs[w].at[:, _ds(chip * cw, cw)]

        return [pltpu.make_async_remote_copy(src_ref=slot(w, 2 * px + py), dst_ref=lands[w].at[k],
                                             send_sem=send.at[3 * w + k], recv_sem=recv.at[3 * w + k],
                                             device_id=(px, py, c), device_id_type=MESH_IDS)
                for w in range(len(bufs)) for k, (px, py) in enumerate(others)]

    plan.n_copies = 3 * len(kinds)
    return plan


def _pair_gather_plan(n, n_layers):
    def copies(lands, send, recv, take):
        x, y, c, _ = _position()
        out = []
        for w in range(n):
            hr = lands[w].shape[1] // 2
            for l in range(n_layers):
                mine = lands[w].at[l, _ds(c * hr, hr), :]
                theirs = lands[w].at[l, _ds((1 - c) * hr, hr), :]
                out.append(pltpu.make_async_remote_copy(
                    src_ref=mine, dst_ref=theirs if take else mine, send_sem=send.at[w * n_layers + l],
                    recv_sem=recv.at[w * n_layers + l], device_id=(x, y, 1 - c), device_id_type=MESH_IDS))
        return out

    def plan(bufs, lands, send, recv):
        return copies(lands, send, recv, False)

    def wait_plan(bufs, lands, send, recv):
        return copies(lands, send, recv, True)

    plan.n_copies = wait_plan.n_copies = n * n_layers
    return plan, wait_plan


def _pair_gather(grads, name, after=None):
    n = len(grads)
    n_layers = grads[0].shape[0]
    extra = _as_list(after)

    def body(*refs):
        dst = refs[n + len(extra):2 * n + len(extra)]
        send, recv = refs[2 * n + len(extra):]
        x, y, c, _ = _position()
        sib = (x, y, 1 - c)

        def copy(w, l, half):
            hr = dst[w].shape[1] // 2
            rows = dst[w].at[l, _ds(half * hr, hr), :]
            return pltpu.make_async_remote_copy(src_ref=rows, dst_ref=rows, send_sem=send.at[w * n_layers + l],
                                                recv_sem=recv.at[w * n_layers + l], device_id=sib,
                                                device_id_type=MESH_IDS)

        give = [copy(w, l, c) for w in range(n) for l in range(n_layers)]
        for cp in give:
            cp.start()
        for w in range(n):
            for l in range(n_layers):
                copy(w, l, 1 - c).wait_recv()
        for cp in give:
            cp.wait_send()

    m = n * n_layers
    return _pcall(body, name=name, in_specs=_any_specs(n + len(extra)), out_specs=_any_specs(n),
                  out_shape=[jax.ShapeDtypeStruct(g.shape, g.dtype) for g in grads],
                  input_output_aliases={i: i for i in range(n)},
                  scratch_shapes=[pltpu.SemaphoreType.DMA((m,)), pltpu.SemaphoreType.DMA((m,))])(*grads, *extra)


def _all_gather_plan():
    flips = [(fx, fy, fc) for fx in (0, 1) for fy in (0, 1) for fc in (0, 1)][1:]

    def copies(bufs, lands, send, recv, take):
        x, y, c, _ = _position()
        out = []
        for k, f in enumerate(flips):
            px, py, pc = (1 - x if f[0] else x, 1 - y if f[1] else y, 1 - c if f[2] else c)
            slot = 4 * px + 2 * py + pc if take else 4 * x + 2 * y + c
            out.append(pltpu.make_async_remote_copy(src_ref=bufs[0], dst_ref=lands[0].at[slot], send_sem=send.at[k],
                                                    recv_sem=recv.at[k], device_id=(px, py, pc),
                                                    device_id_type=MESH_IDS))
        return out

    def plan(bufs, lands, send, recv):
        return copies(bufs, lands, send, recv, False)

    def wait_plan(bufs, lands, send, recv):
        return copies(bufs, lands, send, recv, True)

    plan.n_copies = wait_plan.n_copies = N_DEV - 1
    return plan, wait_plan


def _sum_slots(land, name):
    n, rows, cols = land.shape

    def fn(*slots):
        total = slots[0]
        for s in slots[1:]:
            total = total + s
        return (total,)

    flat = land.reshape(n * rows, cols)
    return _rowcall(name, fn, [(flat, d * rows) for d in range(n)], [], [(cols, F32)], br=_divtile(rows, 256, 8),
                    nrows=rows)[0]


def _pack(arrays):
    flat = jnp.concatenate([a.reshape(-1).astype(F32) for a in arrays])
    pad = (-flat.shape[0]) % (8 * LANES)
    return jnp.pad(flat, (0, pad)).reshape(-1, LANES)


def _unpack(buf, like):
    flat, out, off = buf.reshape(-1), [], 0
    for a in like:
        out.append(flat[off:off + a.size].reshape(a.shape))
        off += a.size
    return out


def _row2(v):
    return v.reshape(1, -1)


def _ffn_fwd(x, g_pre, w_gu, w_d, g_post, tag, after=None):
    h = _norm_fwd(x, g_pre, tag + "_pre", after)
    gu, a = _ffn_up(h, w_gu, tag + "_gu")
    f = _mm(a, w_d, 'nn', F32, tag + "_down")
    return _resnorm_fwd(x, f, g_post, 0.5, tag + "_post"), (x, h, gu, a, f)


def _ffn_bwd_weights(dx_out, saved, w_d, g_post, tag, after=None):
    x, h, gu, a, f = saved
    df, dg_post = _resnorm_bwd(f, dx_out, g_post, 0.5, tag + "_post_b", after)
    dw_d = _mm(a, df, 'tn', BF16, tag + "_down_bw")
    dgu = _ffn_down_bwd(df, w_d, gu, tag + "_down_bx")
    dw_gu = _mm(h, dgu, 'tn', BF16, tag + "_gu_bw", halves='b')
    return dgu, dw_gu, dw_d, dg_post


def _ffn_bwd_input(dx_out, saved, dgu, g_pre, w_gu, tag, after=None):
    x = saved[0]
    dh = _mm(dgu, w_gu, 'nt', F32, tag + "_gu_bx", halves='a')
    return _norm_bwd(x, dh, dx_out, g_pre, tag + "_pre_b", after)


def _ffn_bwd(dx_out, saved, g_pre, w_gu, w_d, g_post, tag, after=None):
    dgu, dw_gu, dw_d, dg_post = _ffn_bwd_weights(dx_out, saved, w_d, g_post, tag, after)
    dx, dg_pre = _ffn_bwd_input(dx_out, saved, dgu, g_pre, w_gu, tag)
    return dx, dg_pre, dw_gu, dw_d, dg_post


def _split_cols(pp, cs, cp, widths):
    proj = jnp.concatenate([pp[:, j * cp:j * cp + cs] for j in range(N_CHIPS)], axis=1)
    out, off = [], 0
    for wd in widths:
        out.append(proj[:, off:off + wd])
        off += wd
    return out


def _join_cols(pieces, cs, cp):
    proj = jnp.concatenate([pc.astype(BF16) for pc in pieces], axis=1)
    t = proj.shape[0]
    cols = []
    for j in range(N_CHIPS):
        cols += [proj[:, j * cs:(j + 1) * cs], jnp.zeros((t, cp - cs), proj.dtype)]
    return jnp.concatenate(cols, axis=1)


def _mix_fwd(x, p, w_in, w_out, conv_w, cs, cp, tag, after=None):
    t, d = x.shape
    half = d // 2
    nh = half // HEAD_DIM
    h = _norm_fwd(x, p['mix_norm_pre'], tag + "_pre", after)
    pp = _mm(h, w_in, 'nn', F32, tag + "_in")
    u_a, v_a, qkv_raw, z, b_raw, a_raw = _split_cols(pp, cs, cp, [half, half, 3 * half, half, nh, nh])
    y_a = _sgu_fwd(u_a, v_a, p['sm_w'], p['sm_b'], p['sm_ln_g'], p['sm_ln_b'], tag + "_sgu")
    qkv = _conv_fwd(qkv_raw, conv_w, tag + "_conv")
    b_raw, a_raw = b_raw.T, a_raw.T
    beta, gcum = _gates_fwd(b_raw, a_raw, p['a_log'].T, p['dt_bias'].T, tag + "_gates")
    gcol, bcol = gcum[:, :, None], beta[:, :, None]
    grow = gcum.reshape(nh, t // DN_CHUNK, 1, DN_CHUNK)
    o, states = _delta_fwd(qkv, gcol, grow, bcol, tag + "_delta")
    y_b = _outgate_fwd(o, z, p['dn_norm_w'], tag + "_gate")
    y = jnp.concatenate([y_a, y_b], axis=1)
    m = _mm(y, w_out, 'nn', F32, tag + "_out")
    x_out = _resnorm_fwd(x, m, p['mix_norm_post'], 1.0, tag + "_post")
    return x_out, (x, h, u_a, v_a, qkv_raw, z, b_raw, a_raw, qkv, gcol, grow, bcol, states, o, y, m)


def _mix_bwd(dx_out, saved, p, w_in, w_out, conv_w, cs, cp, tag, after=None):
    x, h, u_a, v_a, qkv_raw, z, b_raw, a_raw, qkv, gcol, grow, bcol, states, o, y, m = saved
    t, d = x.shape
    half = d // 2
    nh = half // HEAD_DIM
    gr = {}
    dm, gr['mix_norm_post'] = _resnorm_bwd(m, dx_out, p['mix_norm_post'], 1.0, tag + "_post_b", after)
    dy = _mm(dm, w_out, 'nt', F32, tag + "_out_bx")
    gr['w_out'] = _mm(y, dm, 'tn', BF16, tag + "_out_bw")
    dy_a, dy_b = dy[:, :half], dy[:, half:]
    do, dz, gr['dn_norm_w'] = _outgate_bwd(o, z, p['dn_norm_w'], dy_b, tag + "_gate_b")
    dq, dk, dv, dgcol, dgrow, dbcol = _delta_bwd(qkv, gcol, grow, bcol, states, do, tag + "_delta_b")
    dgcum = dgcol[:, :, 0] + dgrow.reshape(nh, t)
    db_raw, da_raw, gr['a_log'], gr['dt_bias'] = _gates_bwd(b_raw, a_raw, p['a_log'].T, p['dt_bias'].T,
                                                             dbcol[:, :, 0], dgcum, tag + "_gates_b")
    db_raw, da_raw = db_raw.T, da_raw.T
    thirds = [_conv_bwd(qkv_raw, conv_w, dpart, i * half, tag + "_conv_b") for i, dpart in enumerate((dq, dk, dv))]
    gr['conv_w'] = jnp.concatenate([dw for _, dw in thirds], axis=1)
    du_a, dv_a, gr['sm_w'], gr['sm_b'], gr['sm_ln_g'], gr['sm_ln_b'] = _sgu_bwd(
        u_a, v_a, p['sm_w'], p['sm_b'], p['sm_ln_g'], p['sm_ln_b'], dy_a, tag + "_sgu_b")
    dpp = _join_cols([du_a, dv_a, *[dx for dx, _ in thirds], dz, db_raw, da_raw], cs, cp).astype(BF16)
    dh = _mm(dpp, w_in, 'nt', F32, tag + "_in_bx")
    gr['w_in'] = _mm(h, dpp, 'tn', BF16, tag + "_in_bw")
    dx, gr['mix_norm_pre'] = _norm_bwd(x, dh, dx_out, p['mix_norm_pre'], tag + "_pre_b")
    return dx, gr


def _xa_fwd(x, mem, p, w_xq, w_xkv, w_xo, tag, after=None):
    h = _norm_fwd(x, p['xa_norm_pre'], tag + "_pre", after)
    mh = _norm_fwd(mem, p['mem_norm'], tag + "_mem")
    q = _mm(h, w_xq, 'nn', BF16, tag + "_q")
    kv = _mm(mh, w_xkv, 'nn', BF16, tag + "_kv")
    o = _attn_fwd(q, kv, tag + "_attn")
    c = _mm(o, w_xo, 'nn', F32, tag + "_o")
    return _resnorm_fwd(x, c, p['xa_norm_post'], 1.0, tag + "_post"), (x, h, mh, q, kv, o, c)


def _xa_bwd(dx_out, saved, mem, p, w_xq, w_xkv, w_xo, tag, after=None):
    x, h, mh, q, kv, o, c = saved
    gr = {}
    dc, gr['xa_norm_post'] = _resnorm_bwd(c, dx_out, p['xa_norm_post'], 1.0, tag + "_post_b", after)
    do = _mm(dc, w_xo, 'nt', F32, tag + "_o_bx")
    gr['w_xo'] = _mm(o, dc, 'tn', BF16, tag + "_o_bw")
    dq, dkv = _attn_bwd(q, kv, do, tag + "_attn_b")
    dkv = dkv.astype(BF16)
    dh = _mm(dq, w_xq, 'nt', F32, tag + "_q_bx")
    gr['w_xq'] = _mm(h, dq, 'tn', BF16, tag + "_q_bw")
    dmh = _mm(dkv, w_xkv, 'nt', F32, tag + "_kv_bx")
    gr['w_xkv'] = _mm(mh, dkv, 'tn', BF16, tag + "_kv_bw")
    gr['mem_norm'] = _norm_bwd_gain(mem, dmh, p['mem_norm'], tag + "_mem_b")
    dx, gr['xa_norm_pre'] = _norm_bwd(x, dh, dx_out, p['xa_norm_pre'], tag + "_pre_b")
    return dx, gr


def _step(inputs):
    x, mem, target = inputs['x'][0], inputs['mem'][0], inputs['loss_target'][0]
    n_layers = inputs['w_in'].shape[0]
    cx, cy, cc = lax.axis_index("x"), lax.axis_index("y"), lax.axis_index("c")
    chip = 2 * cx + cy
    cs = inputs['w_in'].shape[2]
    cp = -(-cs // LANES) * LANES
    kinds = [BIG[nm] for nm in BIG_NAMES]

    conv_local = inputs['conv_w']
    ccols = conv_local.shape[2]

    core_idx, chip_idx = cc.astype(jnp.int32).reshape(1), chip.astype(jnp.int32).reshape(1)
    where_idx = jnp.stack([chip, cc]).astype(jnp.int32)
    order = [(l, gi) for l in range(n_layers) for gi in range(len(GROUPS))]
    n_groups = len(order)

    def small(l):
        p = {nm: inputs[nm][l] for nm in SMALL_NAMES}
        for nm in SMALL_NAMES:
            if p[nm].ndim == 1:
                p[nm] = _row2(p[nm])
        p['sm_b'] = p['sm_b'][:, :, None]
        return p

    def place(k, after=None):
        l, gi = order[k]
        lands = []
        for nm in GROUPS[gi][1]:
            _, r, c_in = inputs[nm].shape
            shape = _full_shape((r, cp if nm == 'w_in' else c_in), BIG[nm])
            lands.append(_place_block(lax.empty(shape, BF16), inputs[nm], l, BIG[nm], chip_idx, "place_block", after))
        return lands

    placed = {}
    conv_group = 1

    def gather_start(k, after):
        names = list(GROUPS[order[k][1]][1])
        kds = [BIG[nm] for nm in names]
        if k not in placed:
            placed[k] = place(k)
        lands = list(placed[k])
        if k == conv_group:
            mine = conv_local.reshape(1, n_layers * CONV_WIDTH, ccols)
            lands.append(lax.dynamic_update_slice(jnp.zeros((N_CHIPS, *mine.shape[1:]), F32), mine, (chip, 0, 0)))
            names.append('conv_w'), kds.append('all')
        plan, wait_plan = _gather_plan(kds)
        send, recv, _, lands, token = _split_copy_start([], lands, plan, after, f"gather_start_{k}")
        return dict(send=send, recv=recv, lands=lands, token=token, kinds=kds, names=names, wait_plan=wait_plan)

    def gather_finish(k, st, after):
        _, lands = _split_copy_wait(st['send'], st['recv'], [], st['lands'], st['wait_plan'], after, f"gather_wait_{k}")
        got = dict(zip(st['names'], lands))
        halves = [nm for nm, kd in zip(st['names'], st['kinds']) if kd != 'all']
        passed = _gather_forward([got[nm] for nm in halves], [BIG[nm] for nm in halves],
                                 "gather_forward_" + GROUPS[order[k][1]][0])
        w = {nm: (g.reshape(-1, g.shape[2]) if BIG[nm] == 'row' else g) for nm, g in zip(halves, passed)}
        if 'conv_w' in got:
            w['conv_w'] = got['conv_w'].reshape(N_CHIPS, n_layers, CONV_WIDTH, ccols).transpose(1, 2, 0, 3).reshape(
                n_layers, CONV_WIDTH, N_CHIPS * ccols)
        return w

    def group_fwd(gi, x_, p, w, l, token):
        if gi == 0:
            return _ffn_fwd(x_, p['ffn1_norm_pre'], w['ffn1_w_gate_up'], w['ffn1_w_down'], p['ffn1_norm_post'], "ffn", token)
        if gi == 1:
            return _mix_fwd(x_, p, w['w_in'], w['w_out'], conv_full[l], cs, cp, "mix", token)
        if gi == 2:
            return _xa_fwd(x_, mem, p, w['w_xq'], w['w_xkv'], w['w_xo'], "xa", token)
        return _ffn_fwd(x_, p['ffn2_norm_pre'], w['ffn2_w_gate_up'], w['ffn2_w_down'], p['ffn2_norm_post'], "ffn", token)

    def group_bwd(gi, dx_, s, p, w, l, token):
        if gi in (0, 3):
            pre = 'ffn1' if gi == 0 else 'ffn2'
            dx_, g_pre, g_gu, g_d, g_post = _ffn_bwd(dx_, s, p[pre + '_norm_pre'], w[pre + '_w_gate_up'],
                                                     w[pre + '_w_down'], p[pre + '_norm_post'], "ffn", token)
            return dx_, {pre + '_norm_pre': g_pre, pre + '_w_gate_up': g_gu, pre + '_w_down': g_d,
                         pre + '_norm_post': g_post}
        if gi == 1:
            return _mix_bwd(dx_, s, p, w['w_in'], w['w_out'], conv_full[l], cs, cp, "mix", token)
        return _xa_bwd(dx_, s, mem, p, w['w_xq'], w['w_xkv'], w['w_xo'], "xa", token)

    started = {0: gather_start(0, None)}
    started[1] = gather_start(1, started[0]['token'])
    weights, saved = [], []
    for k in range(2, n_groups):
        placed[k] = place(k, started[1]['token'])
    head_work = [started[1]['token']] + [land for k in range(2, n_groups) for land in placed[k]]
    conv_full = None
    for k, (l, gi) in enumerate(order):
        w = gather_finish(k, started[k], x if k > 0 else head_work)
        if k == conv_group:
            conv_full = w.pop('conv_w')
        token = None
        if k + 2 < n_groups:
            started[k + 2] = gather_start(k + 2, next(iter(w.values())))
            token = started[k + 2]['token']
        weights.append(w)
        x, s = group_fwd(gi, x, small(l), w, l, token)
        saved.append(s)

    dx, loss_part = _loss_call(x, target, "loss")

    grads = [dict() for _ in range(n_layers)]
    big_grads = {nm: None for nm in BIG_NAMES}
    pair_gathers = {}

    def reduce_finish(pd, after):
        parts, lands = _split_copy_wait(pd['send'], pd['recv'], pd['parts'], pd['lands'], pd['plan'], after,
                                        f"chip_wait_{pd['k']}")
        for nm, kd, part, landed in zip(pd['names'], pd['kinds'], parts, lands):
            if big_grads[nm] is None:
                big_grads[nm] = lax.empty((n_layers, 2 * landed.shape[1], landed.shape[2]), F32)
            big_grads[nm] = _chip_sum(part, landed, kd, where_idx, big_grads[nm], pd['l'], "chip_sum")
        if pd['l'] == 0 and pd['k'] > 0:
            names = pd['names']
            plan, wait_plan = _pair_gather_plan(len(names), n_layers)
            send, recv, _, got, _ = _split_copy_start([], [big_grads[nm] for nm in names], plan, None,
                                                      f"pair_gather_start_{pd['k']}")
            pair_gathers[pd['k']] = dict(send=send, recv=recv, lands=got, wait_plan=wait_plan)

    def by_chip(gr, names, kds):
        return [gr[nm].reshape(N_CHIPS, -1, gr[nm].shape[1]) if kd == 'row' else gr[nm] for nm, kd in zip(names, kds)]

    def chip_start(k, names, kds, gs, theirs):
        parts = [_pair_add(g, t, core_idx, "pair_add") if kd == 'row'
                 else _pair_add(g[None], t[None], core_idx, "pair_add")[0] for g, t, kd in zip(gs, theirs, kds)]
        lands = [lax.empty((3, *_part_dims(part, kd)), BF16) for part, kd in zip(parts, kds)]
        plan = _chip_plan(kds)
        send, recv, parts, lands, token = _split_copy_start(parts, lands, plan, None, f"chip_start_{k}")
        return dict(send=send, recv=recv, parts=parts, lands=lands, token=token, plan=plan, names=names, kinds=kds,
                    k=k, l=order[k][0])

    pair_pending, chip_queue = None, []
    for k in reversed(range(n_groups)):
        l, gi = order[k]
        names = GROUPS[gi][1]
        kds = [BIG[nm] for nm in names]
        p, w = small(l), weights[k]
        tokens = [pd['token'] for pd in ([pair_pending] if pair_pending else []) + chip_queue] or None
        if k > 0:
            dx_next, gr = group_bwd(gi, dx, saved[k], p, w, l, tokens)
        else:
            dgu, g_gu, g_d, g_post = _ffn_bwd_weights(dx, saved[k], w['ffn1_w_down'], p['ffn1_norm_post'], "ffn", tokens)
            gr = {'ffn1_w_gate_up': g_gu, 'ffn1_w_down': g_d, 'ffn1_norm_post': g_post}
            dx_next = g_gu
        for pd in chip_queue:
            reduce_finish(pd, dx_next)
        chip_queue = []
        if pair_pending:
            pd = pair_pending
            gs_prev, theirs = _split_copy_wait(pd['send'], pd['recv'], pd['gs'], pd['lands'], pd['plan'], dx_next,
                                               f"pair_wait_{pd['k']}")
            chip_queue.append(chip_start(pd['k'], pd['names'], pd['kinds'], gs_prev, theirs))
            pair_pending = None
        gs = by_chip(gr, names, kds)
        if k > 1:
            plan = _pair_plan(kds)
            lands = [lax.empty(_half_shape(g, kd), BF16) for g, kd in zip(gs, kds)]
            send, recv, gs, lands, token = _split_copy_start(gs, lands, plan, None, f"pair_start_{k}")
            pair_pending = dict(send=send, recv=recv, gs=gs, lands=lands, token=token, plan=plan, names=names,
                                kinds=kds, k=k)
        else:
            started_now = chip_start(k, names, kds, gs, _pair_exchange(gs, kds, "pair_exchange_" + GROUPS[gi][0]))
            if k > 0:
                chip_queue.append(started_now)
        if k > 0:
            dx = dx_next
        else:
            pending = started_now
            dx, gr['ffn1_norm_pre'] = _ffn_bwd_input(dx, saved[k], dgu, p['ffn1_norm_pre'], w['ffn1_w_gate_up'], "ffn",
                                                     pending['token'])
        grads[l].update({nm: g for nm, g in gr.items() if nm not in BIG})
    for pd in chip_queue:
        reduce_finish(pd, dx)
    out = {}

    def two(a):
        return a.reshape(-1, a.shape[-1])

    def adamw(nm, g):
        res = _adamw_call(two(inputs[nm]), two(g), two(inputs['m_' + nm]), two(inputs['v_' + nm]), "adamw")
        out[nm] = tuple(a.reshape(inputs[nm].shape) for a in res)
        return res[1]

    def finish_weights(gi, after=None):
        names = GROUPS[gi][1]
        if gi in pair_gathers:
            st = pair_gathers[gi]
            _, got = _split_copy_wait(st['send'], st['recv'], [], st['lands'], st['wait_plan'], after,
                                      f"pair_gather_wait_{gi}")
        else:
            got = _pair_gather([big_grads[nm] for nm in names], "pair_gather_" + GROUPS[gi][0], after)
        return [adamw(nm, g) for nm, g in zip(names, got)]

    small_grads = [jnp.stack([grads[l][nm].reshape(inputs[nm].shape[1:]) if nm != 'conv_w' else grads[l][nm]
                              for l in range(n_layers)]) for nm in SMALL_NAMES]
    packed = _pack(small_grads + [loss_part])
    me = 4 * cx + 2 * cy + cc
    land = lax.dynamic_update_slice(lax.empty((N_DEV, *packed.shape), F32), packed[None], (me, 0, 0))
    plan, wait_plan = _all_gather_plan()
    send, recv, bufs, lands, _ = _split_copy_start([packed], [land], plan, pending['token'], "small_start")

    done = [d for gi in range(len(GROUPS) - 1, 0, -1) for d in finish_weights(gi, pending['token'])]
    reduce_finish(pending, done)
    done = finish_weights(0)

    _, lands = _split_copy_wait(send, recv, bufs, lands, wait_plan, done, "small_wait")
    summed = _unpack(_sum_slots(lands[0], "small_sum"), small_grads + [loss_part])
    loss = summed[-1][0, 0]
    small_g = dict(zip(SMALL_NAMES, summed[:-1]))
    small_g['conv_w'] = lax.dynamic_slice(small_g['conv_w'], (0, 0, chip * ccols), (n_layers, CONV_WIDTH, ccols))
    for nm in SMALL_NAMES:
        adamw(nm, small_g[nm])
    return (loss, dx[None], *[out[nm][0] for nm in WEIGHTS], *[out[nm][1] for nm in WEIGHTS],
            *[out[nm][2] for nm in WEIGHTS], *[out[nm][3] for nm in WEIGHTS])


def kernel(x, mem, ffn1_norm_pre, ffn1_w_gate_up, ffn1_w_down, ffn1_norm_post, mix_norm_pre, w_in, conv_w, a_log, dt_bias, sm_w, sm_b, sm_ln_g, sm_ln_b, dn_norm_w, w_out, mix_norm_post, xa_norm_pre, mem_norm, w_xq, w_xkv, w_xo, xa_norm_post, ffn2_norm_pre, ffn2_w_gate_up, ffn2_w_down, ffn2_norm_post, loss_target, m_ffn1_norm_pre, m_ffn1_w_gate_up, m_ffn1_w_down, m_ffn1_norm_post, m_mix_norm_pre, m_w_in, m_conv_w, m_a_log, m_dt_bias, m_sm_w, m_sm_b, m_sm_ln_g, m_sm_ln_b, m_dn_norm_w, m_w_out, m_mix_norm_post, m_xa_norm_pre, m_mem_norm, m_w_xq, m_w_xkv, m_w_xo, m_xa_norm_post, m_ffn2_norm_pre, m_ffn2_w_gate_up, m_ffn2_w_down, m_ffn2_norm_post, v_ffn1_norm_pre, v_ffn1_w_gate_up, v_ffn1_w_down, v_ffn1_norm_post, v_mix_norm_pre, v_w_in, v_conv_w, v_a_log, v_dt_bias, v_sm_w, v_sm_b, v_sm_ln_g, v_sm_ln_b, v_dn_norm_w, v_w_out, v_mix_norm_post, v_xa_norm_pre, v_mem_norm, v_w_xq, v_w_xkv, v_w_xo, v_xa_norm_post, v_ffn2_norm_pre, v_ffn2_w_gate_up, v_ffn2_w_down, v_ffn2_norm_post):
    vals = dict(locals())
    return _step(vals)
```

```python
import functools
import math

import jax
import jax.numpy as jnp
from jax import lax
from jax.experimental import pallas as pl
from jax.experimental.pallas import tpu as pltpu

F32, BF16 = jnp.float32, jnp.bfloat16
NORM_EPS = 1e-6
HEAD_DIM = 128
GM_CHUNK = 128
DN_CHUNK = 64
CONV_WIDTH = 4
XA_HEADS = 4
LANES = 128
MXU_WIDTH = 256
N_CHIPS = 4
N_DEV = 8
VMEM_LIMIT = 56 * 1024 * 1024
MESH_IDS = pl.DeviceIdType.MESH
ADAM_LR, ADAM_B1, ADAM_B2, ADAM_EPS, ADAM_WD, ADAM_STEP = 0.001, 0.9, 0.999, 1e-08, 0.01, 10

WEIGHTS = ['ffn1_norm_pre', 'ffn1_w_gate_up', 'ffn1_w_down', 'ffn1_norm_post', 'mix_norm_pre', 'w_in', 'conv_w',
           'a_log', 'dt_bias', 'sm_w', 'sm_b', 'sm_ln_g', 'sm_ln_b', 'dn_norm_w', 'w_out', 'mix_norm_post',
           'xa_norm_pre', 'mem_norm', 'w_xq', 'w_xkv', 'w_xo', 'xa_norm_post', 'ffn2_norm_pre', 'ffn2_w_gate_up',
           'ffn2_w_down', 'ffn2_norm_post']
BIG = {'ffn1_w_gate_up': 'col', 'ffn1_w_down': 'row', 'w_in': 'col', 'w_out': 'row', 'w_xq': 'row',
       'w_xkv': 'col', 'w_xo': 'row', 'ffn2_w_gate_up': 'col', 'ffn2_w_down': 'row'}
BIG_NAMES = list(BIG)
GROUPS = (('ffn1', ('ffn1_w_gate_up', 'ffn1_w_down')), ('mix', ('w_in', 'w_out')),
          ('xa', ('w_xq', 'w_xkv', 'w_xo')), ('ffn2', ('ffn2_w_gate_up', 'ffn2_w_down')))
SMALL_NAMES = [n for n in WEIGHTS if n not in BIG]


def _pcall(body, **kw):
    return pl.pallas_call(body, **kw)


def _params(sem=None):
    return pltpu.CompilerParams(dimension_semantics=sem, vmem_limit_bytes=VMEM_LIMIT)


def _as_list(after):
    if after is None:
        return []
    return list(after) if isinstance(after, (list, tuple)) else [after]


def _divtile(dim, cap, align=LANES):
    if dim <= cap:
        return dim
    t = (cap // align) * align
    while t > align and dim % t:
        t -= align
    assert dim % t == 0, (dim, cap)
    return t


_DN = {'nn': (((1,), (0,)), ((), ())), 'nt': (((1,), (1,)), ((), ())), 'tn': (((0,), (0,)), ((), ()))}


MM_MIN_STEPS = 8
MM_VMEM_BUDGET = 44 * 1024 * 1024


def _mm_tiles(m, n, k, out_itemsize, acc=False):
    def tile(dim, cap):
        return _divtile(dim, cap, MXU_WIDTH) if dim % MXU_WIDTH == 0 else _divtile(dim, cap)

    tm = tile(m, 1024)
    for cap_n, cap_k in ((2048, 2048), (1024, 2048), (2048, 1024), (1024, 1024), (1024, 512), (512, 512)):
        tn, tk = tile(n, cap_n), _divtile(k, cap_k)
        need = 2 * 2 * (tm * tk + tk * tn) + 2 * tm * tn * out_itemsize + (tm * tn * 4 if tk < k or acc else 0)
        if need <= MM_VMEM_BUDGET:
            break
    while (m // tm) * (n // tn) * (k // tk) < MM_MIN_STEPS and tn > 2 * MXU_WIDTH and n % (tn // 2) == 0:
        tn //= 2
    return tm, tn, tk


def _mm(a, b, mode, out_dtype, name, halves=None):
    if halves == 'a':
        assert mode == 'nt'
        a_shape, b_shape = (a.shape[1], 2 * a.shape[2]), b.shape
    elif halves == 'b':
        assert mode == 'tn'
        a_shape, b_shape = a.shape, (b.shape[1], 2 * b.shape[2])
    else:
        a_shape, b_shape = a.shape, b.shape
    if mode == 'nn':
        (m, k), (k2, n) = a_shape, b_shape
    elif mode == 'nt':
        (m, k), (n, k2) = a_shape, b_shape
    else:
        (k, m), (k2, n) = a_shape, b_shape
    assert k == k2, (a.shape, b.shape, mode)
    if halves == 'a':
        tm, tn, tk = _mm_tiles(m, n, k // 2, jnp.dtype(out_dtype).itemsize, acc=True)
    elif halves == 'b':
        tm, tn, tk = _mm_tiles(m, n // 2, k, jnp.dtype(out_dtype).itemsize)
    else:
        tm, tn, tk = _mm_tiles(m, n, k, jnp.dtype(out_dtype).itemsize)
    nk = k // tk
    a_spec = {'nn': pl.BlockSpec((tm, tk), lambda i, j, kk: (i, kk)),
              'nt': pl.BlockSpec((tm, tk), lambda i, j, kk: (i, kk)),
              'tn': pl.BlockSpec((tk, tm), lambda i, j, kk: (kk, i))}[mode]
    b_spec = {'nn': pl.BlockSpec((tk, tn), lambda i, j, kk: (kk, j)),
              'nt': pl.BlockSpec((tn, tk), lambda i, j, kk: (j, kk)),
              'tn': pl.BlockSpec((tk, tn), lambda i, j, kk: (kk, j))}[mode]
    if halves == 'a':
        per = nk // 2
        a_spec = pl.BlockSpec((None, tm, tk), lambda i, j, kk: (kk // per, i, kk % per))
    elif halves == 'b':
        per = (n // tn) // 2
        b_spec = pl.BlockSpec((None, tk, tn), lambda i, j, kk: (j // per, kk, j % per))

    def dot(a_ref, b_ref):
        return lax.dot_general(a_ref[...].astype(BF16), b_ref[...].astype(BF16), _DN[mode], preferred_element_type=F32)

    def body_once(a_ref, b_ref, o_ref):
        o_ref[...] = dot(a_ref, b_ref).astype(o_ref.dtype)

    def body_steps(a_ref, b_ref, o_ref, acc):
        kk = pl.program_id(2)

        @pl.when(kk == 0)
        def _():
            acc[...] = jnp.zeros_like(acc)

        acc[...] += dot(a_ref, b_ref)

        @pl.when(kk == nk - 1)
        def _():
            o_ref[...] = acc[...].astype(o_ref.dtype)

    return _pcall(
        body_once if nk == 1 else body_steps, name=name, grid=(m // tm, n // tn, nk),
        in_specs=[a_spec, b_spec], out_specs=pl.BlockSpec((tm, tn), lambda i, j, kk: (i, j)),
        out_shape=jax.ShapeDtypeStruct((m, n), out_dtype),
        scratch_shapes=[] if nk == 1 else [pltpu.VMEM((tm, tn), F32)],
        compiler_params=_params(("parallel", "parallel", "arbitrary")),
    )(a, b)


def _rowcall(name, f, rows, params=(), row_outs=(), acc_outs=(), br=256, nrows=None, after=None):
    rows = [r if isinstance(r, tuple) else (r, 0) for r in rows]
    t = nrows if nrows is not None else rows[0][0].shape[0]
    br = min(br, t)
    assert t % br == 0, (name, t, br)
    n_in, n_ro = len(rows) + len(params), len(row_outs)
    extra = _as_list(after)

    def row_spec(arr, off):
        assert off % br == 0
        return pl.BlockSpec((br, arr.shape[1]), functools.partial(lambda i, o: (i + o, 0), o=off // br))

    def whole_spec(shape):
        return pl.BlockSpec(shape, functools.partial(lambda i, nd: (0,) * nd, nd=len(shape)))

    def body(*refs):
        res = f(*[r[...] for r in refs[:n_in]])
        outs = refs[n_in + len(extra):]
        for o_ref, val in zip(outs[:n_ro], res[:n_ro]):
            o_ref[...] = val.astype(o_ref.dtype)
        if acc_outs:
            @pl.when(pl.program_id(0) == 0)
            def _():
                for o_ref in outs[n_ro:]:
                    o_ref[...] = jnp.zeros_like(o_ref)

            for o_ref, val in zip(outs[n_ro:], res[n_ro:]):
                o_ref[...] += val.astype(F32)

    out = _pcall(
        body, name=name, grid=(t // br,),
        in_specs=[row_spec(a, o) for a, o in rows] + [whole_spec(p.shape) for p in params]
        + [pl.BlockSpec(memory_space=pl.ANY)] * len(extra),
        out_specs=[pl.BlockSpec((br, c), lambda i: (i, 0)) for c, _ in row_outs] + [whole_spec(s) for s in acc_outs],
        out_shape=[jax.ShapeDtypeStruct((t, c), dt) for c, dt in row_outs]
        + [jax.ShapeDtypeStruct(s, F32) for s in acc_outs],
        compiler_params=_params(("arbitrary",) if acc_outs else ("parallel",)),
    )(*[a for a, _ in rows], *params, *extra)
    return out


_DN_BATCH = {'nn': (((2,), (1,)), ((0,), (0,))), 'nt': (((2,), (2,)), ((0,), (0,))), 'tn': (((1,), (1,)), ((0,), (0,)))}


def _dims(a, dn):
    return _DN_BATCH[dn] if a.ndim == 3 else _DN[dn]


def _bdot_raw(a, b, dn):
    return lax.dot_general(a.astype(BF16), b.astype(BF16), _dims(a, dn), preferred_element_type=F32)


def _hdot_raw(a, b, dn):
    a_hi, b_hi = a.astype(BF16), b.astype(BF16)
    a_lo, b_lo = (a - a_hi.astype(F32)).astype(BF16), (b - b_hi.astype(F32)).astype(BF16)

    def dot(p, q):
        return lax.dot_general(p, q, _dims(a, dn), preferred_element_type=F32)

    return dot(a_hi, b_hi) + (dot(a_hi, b_lo) + dot(a_lo, b_hi))


def _with_vjp(raw):
    @functools.partial(jax.custom_vjp, nondiff_argnums=(2,))
    def dot(a, b, dn):
        return raw(a, b, dn)

    def fwd(a, b, dn):
        return raw(a, b, dn), (a, b)

    def bwd(dn, res, ct):
        a, b = res
        if dn == 'nn':
            da, db = raw(ct, b, 'nt'), raw(a, ct, 'tn')
        elif dn == 'nt':
            da, db = raw(ct, b, 'nn'), raw(ct, a, 'tn')
        else:
            da, db = raw(b, ct, 'nt'), raw(a, ct, 'nn')
        return da.astype(a.dtype), db.astype(b.dtype)

    dot.defvjp(fwd, bwd)
    return dot


_bdot, _hdot = _with_vjp(_bdot_raw), _with_vjp(_hdot_raw)


def _rms(x, g):
    return x * lax.rsqrt(jnp.mean(x * x, axis=-1, keepdims=True) + NORM_EPS) * g


def _sigmoid(x):
    return 1.0 / (1.0 + jnp.exp(-x))


def _silu(x):
    return x * _sigmoid(x)


def _gelu(x):
    return 0.5 * x * (1.0 + lax.erf(x * (2.0 ** -0.5)))


def _norm_fwd(x, g, name, after=None):
    return _rowcall(name, lambda x_, g_: (_rms(x_, g_),), [x], [g], [(x.shape[1], BF16)], after=after)[0]


def _resnorm_fwd(x, f, g, alpha, name):
    return _rowcall(name, lambda x_, f_, g_: (x_ + alpha * _rms(f_, g_),), [x, f], [g], [(x.shape[1], F32)])[0]


def _resnorm_bwd(f, dx, g, alpha, name, after=None):
    def fn(f_, dx_, g_):
        _, vjp = jax.vjp(lambda a, b: alpha * _rms(a, b), f_, g_)
        return vjp(dx_)

    return _rowcall(name, fn, [f, dx], [g], [(f.shape[1], BF16)], [g.shape], after=after)


def _norm_bwd(x, dh, dx_out, g, name, after=None):
    def fn(x_, dh_, dxo_, g_):
        _, vjp = jax.vjp(_rms, x_, g_)
        dx, dg = vjp(dh_)
        return dxo_ + dx, dg

    return _rowcall(name, fn, [x, dh, dx_out], [g], [(x.shape[1], F32)], [g.shape], after=after)


def _norm_bwd_gain(x, dh, g, name):
    def fn(x_, dh_, g_):
        _, vjp = jax.vjp(lambda b: _rms(x_, b), g_)
        return vjp(dh_)

    return _rowcall(name, fn, [x, dh], [g], [], [g.shape])[0]


def _ffn_up(h, w_gu, name):
    t, d = h.shape
    ff = w_gu.shape[1] // 2
    tm, tn = _divtile(t, 1024), _divtile(ff, 512, MXU_WIDTH)
    nj = ff // tn

    def body(h_ref, wg_ref, wu_ref, gu_ref, a_ref):
        hv = h_ref[...]
        gate = jnp.dot(hv, wg_ref[...], preferred_element_type=F32)
        up = jnp.dot(hv, wu_ref[...], preferred_element_type=F32)
        gu_ref[0], gu_ref[1] = gate, up
        a_ref[...] = (_silu(gate) * up).astype(a_ref.dtype)

    return _pcall(body, name=name, grid=(t // tm, nj),
                  in_specs=[pl.BlockSpec((tm, d), lambda i, j: (i, 0)), pl.BlockSpec((d, tn), lambda i, j: (0, j)),
                            pl.BlockSpec((d, tn), lambda i, j: (0, j + nj))],
                  out_specs=[pl.BlockSpec((2, tm, tn), lambda i, j: (0, i, j)), pl.BlockSpec((tm, tn), lambda i, j: (i, j))],
                  out_shape=[jax.ShapeDtypeStruct((2, t, ff), F32), jax.ShapeDtypeStruct((t, ff), BF16)],
                  compiler_params=_params(("parallel", "parallel")))(h, w_gu, w_gu)


def _ffn_down_bwd(df, w_d, gu, name):
    t, d = df.shape
    ff = w_d.shape[0]
    tm, tn = _divtile(t, 1024), _divtile(ff, 512, MXU_WIDTH)

    def body(df_ref, w_ref, gu_ref, o_ref):
        da = lax.dot_general(df_ref[...], w_ref[...], _DN['nt'], preferred_element_type=F32)
        gate, up = gu_ref[0], gu_ref[1]
        s = _sigmoid(gate)
        o_ref[0] = (da * up * (s * (1.0 + gate * (1.0 - s)))).astype(o_ref.dtype)
        o_ref[1] = (da * (gate * s)).astype(o_ref.dtype)

    blk = pl.BlockSpec((2, tm, tn), lambda i, j: (0, i, j))
    return _pcall(body, name=name, grid=(t // tm, ff // tn),
                  in_specs=[pl.BlockSpec((tm, d), lambda i, j: (i, 0)), pl.BlockSpec((tn, d), lambda i, j: (j, 0)), blk],
                  out_specs=blk, out_shape=jax.ShapeDtypeStruct((2, t, ff), BF16),
                  compiler_params=_params(("parallel", "parallel")))(df, w_d, gu)


def _sgu_norm(v, lg, lb):
    vf = _gelu(v)
    mu = jnp.mean(vf, axis=-1, keepdims=True)
    var = jnp.mean(jnp.square(vf - mu), axis=-1, keepdims=True)
    return (vf - mu) * lax.rsqrt(var + NORM_EPS) * lg + lb


def _sgu_head(dot, u_h, vn_h, w_h, b_h):
    r = lax.broadcasted_iota(jnp.int32, w_h.shape, 0)
    c = lax.broadcasted_iota(jnp.int32, w_h.shape, 1)
    mixed = dot(jnp.where(r >= c, w_h, 0.0), vn_h, 'nn') + b_h
    return _gelu(u_h) * mixed


def _heads(width):
    return [slice(h * HEAD_DIM, (h + 1) * HEAD_DIM) for h in range(width // HEAD_DIM)]


def _sgu_fwd(u, v, w, bcol, lg, lb, name):
    def fn(u_, v_, w_, b_, lg_, lb_):
        vn = _sgu_norm(v_, lg_, lb_)
        return (jnp.concatenate([_sgu_head(_bdot_raw, u_[:, s], vn[:, s], w_[h], b_[h])
                                 for h, s in enumerate(_heads(u_.shape[1]))], axis=1),)

    return _rowcall(name, fn, [u, v], [w, bcol, lg, lb], [(u.shape[1], BF16)], br=GM_CHUNK)[0]


def _sgu_bwd(u, v, w, bcol, lg, lb, dy, name):
    def fn(u_, v_, dy_, w_, b_, lg_, lb_):
        vn, vjp_norm = jax.vjp(_sgu_norm, v_, lg_, lb_)
        du, dvn, dw, db = [], [], [], []
        for h, s in enumerate(_heads(u_.shape[1])):
            _, vjp = jax.vjp(functools.partial(_sgu_head, _bdot), u_[:, s], vn[:, s], w_[h], b_[h])
            a, b, c, d = vjp(dy_[:, s])
            du.append(a), dvn.append(b), dw.append(c[None]), db.append(d[None])
        dv, dlg, dlb = vjp_norm(jnp.concatenate(dvn, axis=1))
        return (jnp.concatenate(du, axis=1), dv, jnp.concatenate(dw, axis=0), jnp.concatenate(db, axis=0), dlg, dlb)

    wd = u.shape[1]
    return _rowcall(name, fn, [u, v, dy], [w, bcol, lg, lb], [(wd, BF16), (wd, BF16)],
                    [w.shape, bcol.shape, lg.shape, lb.shape], br=GM_CHUNK)


def _shift_down(x, s):
    if s == 0:
        return x
    rows = lax.broadcasted_iota(jnp.int32, x.shape, 0)
    return jnp.where(rows >= s, pltpu.roll(x, s, 0), 0.0)


def _shift_up(x, s):
    if s == 0:
        return x
    t = x.shape[0]
    rows = lax.broadcasted_iota(jnp.int32, x.shape, 0)
    return jnp.where(rows < t - s, pltpu.roll(x, t - s, 0), 0.0)


def _conv_pre(x, taps):
    acc = None
    for i in range(CONV_WIDTH):
        term = _shift_down(x, CONV_WIDTH - 1 - i) * taps[i]
        acc = term if acc is None else acc + term
    return acc


def _taps(w_ref):
    return [w_ref[i:i + 1, :] for i in range(CONV_WIDTH)]


def _conv_fwd(x, w, name):
    t, ch = x.shape
    cb = _divtile(ch, 512)

    def body(x_ref, w_ref, o_ref):
        o_ref[...] = _silu(_conv_pre(x_ref[...], _taps(w_ref)))

    return _pcall(body, name=name, grid=(ch // cb,),
                  in_specs=[pl.BlockSpec((t, cb), lambda j: (0, j)), pl.BlockSpec((CONV_WIDTH, cb), lambda j: (0, j))],
                  out_specs=pl.BlockSpec((t, cb), lambda j: (0, j)),
                  out_shape=jax.ShapeDtypeStruct((t, ch), F32), compiler_params=_params(("parallel",)))(x, w)


def _conv_bwd(x, w, dout, col_off, name):
    t, ch = dout.shape
    cb = _divtile(ch, 256)
    assert col_off % cb == 0
    off = col_off // cb

    def body(x_ref, w_ref, do_ref, dx_ref, dw_ref):
        xv, taps = x_ref[...], _taps(w_ref)
        pre = _conv_pre(xv, taps)
        s = _sigmoid(pre)
        dpre = do_ref[...] * (s * (1.0 + pre * (1.0 - s)))
        dx = None
        for i in range(CONV_WIDTH):
            sh = CONV_WIDTH - 1 - i
            term = _shift_up(dpre, sh) * taps[i]
            dx = term if dx is None else dx + term
            dw_ref[i:i + 1, :] = jnp.sum(dpre * _shift_down(xv, sh), axis=0, keepdims=True)
        dx_ref[...] = dx.astype(dx_ref.dtype)

    return _pcall(body, name=name, grid=(ch // cb,),
                  in_specs=[pl.BlockSpec((t, cb), lambda j: (0, j + off)),
                            pl.BlockSpec((CONV_WIDTH, cb), lambda j: (0, j + off)),
                            pl.BlockSpec((t, cb), lambda j: (0, j))],
                  out_specs=[pl.BlockSpec((t, cb), lambda j: (0, j)), pl.BlockSpec((CONV_WIDTH, cb), lambda j: (0, j))],
                  out_shape=[jax.ShapeDtypeStruct((t, ch), BF16), jax.ShapeDtypeStruct((CONV_WIDTH, ch), F32)],
                  compiler_params=_params(("parallel",)))(x, w, dout)


GATE_BLOCK = 256


def _gates(dot, b_raw, a_raw, a_log, dt_bias):
    blk = b_raw.shape[1]
    z = a_raw + dt_bias
    softplus = jnp.maximum(z, 0.0) + jnp.log(1.0 + jnp.exp(-jnp.abs(z)))
    g = -jnp.exp(a_log) * softplus
    ri = lax.broadcasted_iota(jnp.int32, (blk, blk), 0)
    ci = lax.broadcasted_iota(jnp.int32, (blk, blk), 1)
    upto = ((ri <= ci) & (ri // DN_CHUNK == ci // DN_CHUNK)).astype(F32)
    return _sigmoid(b_raw), dot(g, upto, 'nn')


def _gate_blocks(t):
    blk = min(GATE_BLOCK, t)
    return [slice(i, i + blk) for i in range(0, t, blk)]


def _whole_call(name, f, ins, out_shapes):
    n_in = len(ins)

    def body(*refs):
        for o_ref, val in zip(refs[n_in:], f(*[r[...] for r in refs[:n_in]])):
            o_ref[...] = val

    vmem = pl.BlockSpec(memory_space=pltpu.VMEM)
    return _pcall(body, name=name, in_specs=[vmem] * n_in, out_specs=[vmem] * len(out_shapes),
                  out_shape=[jax.ShapeDtypeStruct(s, F32) for s in out_shapes],
                  compiler_params=pltpu.CompilerParams(vmem_limit_bytes=VMEM_LIMIT))(*ins)


def _gates_fwd(b_raw, a_raw, a_log, dt_bias, name):
    def fn(b_, a_, al_, dt_):
        parts = [_gates(_hdot_raw, b_[:, s], a_[:, s], al_, dt_) for s in _gate_blocks(b_.shape[1])]
        return [jnp.concatenate(p, axis=1) for p in zip(*parts)]

    return _whole_call(name, fn, [b_raw, a_raw, a_log, dt_bias], [b_raw.shape, b_raw.shape])


def _gates_bwd(b_raw, a_raw, a_log, dt_bias, dbeta, dgcum, name):
    def fn(b_, a_, al_, dt_, dbeta_, dgcum_):
        db, da, dal, ddt = [], [], None, None
        for s in _gate_blocks(b_.shape[1]):
            _, vjp = jax.vjp(functools.partial(_gates, _hdot), b_[:, s], a_[:, s], al_, dt_)
            p, q, r, u = vjp((dbeta_[:, s], dgcum_[:, s]))
            db.append(p), da.append(q)
            dal, ddt = (r, u) if dal is None else (dal + r, ddt + u)
        return jnp.concatenate(db, axis=1), jnp.concatenate(da, axis=1), dal, ddt

    return _whole_call(name, fn, [b_raw, a_raw, a_log, dt_bias, dbeta, dgcum],
                       [b_raw.shape, a_raw.shape, a_log.shape, dt_bias.shape])


def _unit_lower_inverse_raw(a):
    c = a.shape[-1]
    eye = (lax.broadcasted_iota(jnp.int32, (c, c), 0) == lax.broadcasted_iota(jnp.int32, (c, c), 1)).astype(F32)
    inv, p = eye - a, _hdot_raw(a, a, 'nn')
    n_fac = int(math.log2(c)) - 1
    for it in range(n_fac):
        inv = inv + _hdot_raw(inv, p, 'nn')
        if it < n_fac - 1:
            p = _hdot_raw(p, p, 'nn')
    return inv


@jax.custom_vjp
def _unit_lower_inverse(a):
    return _unit_lower_inverse_raw(a)


def _unit_lower_inverse_fwd(a):
    inv = _unit_lower_inverse_raw(a)
    return inv, inv


def _unit_lower_inverse_bwd(inv, ct):
    return (-_hdot_raw(_hdot_raw(inv, ct, 'tn'), inv, 'nt'),)


_unit_lower_inverse.defvjp(_unit_lower_inverse_fwd, _unit_lower_inverse_bwd)


def _halves_raw(x):
    return x[..., :x.shape[-1] // 2], x[..., x.shape[-1] // 2:]


@jax.custom_vjp
def _halves(x):
    return _halves_raw(x)


_halves.defvjp(lambda x: (_halves_raw(x), None), lambda _, ct: (jnp.concatenate(ct, axis=-1),))

_DELTA_OPS = (_bdot_raw, _hdot_raw, _unit_lower_inverse_raw, _halves_raw)
_DELTA_OPS_VJP = (_bdot, _hdot, _unit_lower_inverse, _halves)


def _delta_chunk(ops, state, q, k, v, gc, gr, bc):
    bd, hd, inverse, halves = ops
    c, d = q.shape[-2:]
    ri = lax.broadcasted_iota(jnp.int32, (c, c), 0)
    ci = lax.broadcasted_iota(jnp.int32, (c, c), 1)
    causal, strict = ri >= ci, ri > ci
    qn = q * lax.rsqrt(jnp.sum(q * q, axis=-1, keepdims=True) + NORM_EPS) * (d ** -0.5)
    kn = k * lax.rsqrt(jnp.sum(k * k, axis=-1, keepdims=True) + NORM_EPS)
    gcum = jnp.broadcast_to(gc, q.shape)
    decay = jnp.where(causal, jnp.exp(jnp.where(causal, gc - gr, 0.0)), 0.0)
    bb = jnp.broadcast_to(bc, q.shape)
    k_beta = kn * bb
    inv = inverse(jnp.where(strict, bd(k_beta, kn, 'nt') * decay, 0.0))
    uw = hd(inv, jnp.concatenate([v * bb, k_beta * jnp.exp(gcum)], axis=-1), 'nn')
    u, w = halves(uw)
    qk = jnp.where(causal, bd(qn, kn, 'nt') * decay, 0.0)
    v_new = u - bd(w, state, 'nn')
    o = bd(qn * jnp.exp(gcum), state, 'nn') + bd(qk, v_new, 'nn')
    last = lax.broadcasted_iota(jnp.int32, (c, d), 0) == c - 1
    g_last = jnp.sum(jnp.where(last, gcum, 0.0), axis=-2, keepdims=True)
    k_dec = kn * jnp.exp(g_last - gcum)
    return state * jnp.exp(g_last) + bd(k_dec, v_new, 'tn'), o


DN_HEADS_PER_STEP = 8


def _by_head(ref):
    return jnp.stack([ref[:, s] for s in _heads(ref.shape[1])])


def _delta_specs(nh, nc, rev):
    c, d = DN_CHUNK, HEAD_DIM
    hb = min(DN_HEADS_PER_STEP, nh)
    ng = nh // hb
    ch = (lambda n: nc - 1 - n) if rev else (lambda n: n)
    qkv = [pl.BlockSpec((c, hb * d), functools.partial(lambda h, n, o: (ch(n), o * ng + h), o=o)) for o in range(3)]
    col = pl.BlockSpec((hb, c, 1), lambda h, n: (h, ch(n), 0))
    row = pl.BlockSpec((hb, None, 1, c), lambda h, n: (h, ch(n), 0, 0))
    st = pl.BlockSpec((hb, None, d, d), lambda h, n: (h, ch(n), 0, 0))
    tok = pl.BlockSpec((c, hb * d), lambda h, n: (ch(n), h))
    return hb, ng, qkv, col, row, st, tok


def _delta_fwd(qkv, gcol, grow, bcol, name):
    t = qkv.shape[0]
    nh, nc, d = gcol.shape[0], t // DN_CHUNK, HEAD_DIM
    hb, ng, qkv_s, col, row, st, tok = _delta_specs(nh, nc, False)

    def body(q_ref, k_ref, v_ref, gc_ref, gr_ref, bc_ref, o_ref, st_ref, state):
        @pl.when(pl.program_id(1) == 0)
        def _():
            state[...] = jnp.zeros_like(state)

        s0 = state[...]
        st_ref[...] = s0
        s1, o = _delta_chunk(_DELTA_OPS, s0, _by_head(q_ref), _by_head(k_ref), _by_head(v_ref), gc_ref[...],
                             gr_ref[...], bc_ref[...])
        for j, s in enumerate(_heads(hb * d)):
            o_ref[:, s] = o[j]
        state[...] = s1

    return _pcall(body, name=name, grid=(ng, nc), in_specs=qkv_s + [col, row, col], out_specs=[tok, st],
                  out_shape=[jax.ShapeDtypeStruct((t, nh * d), F32), jax.ShapeDtypeStruct((nh, nc, d, d), F32)],
                  scratch_shapes=[pltpu.VMEM((hb, d, d), F32)],
                  compiler_params=_params(("parallel", "arbitrary")))(qkv, qkv, qkv, gcol, grow, bcol)


def _delta_bwd(qkv, gcol, grow, bcol, states, do, name):
    t = qkv.shape[0]
    nh, nc, d = gcol.shape[0], t // DN_CHUNK, HEAD_DIM
    hb, ng, qkv_s, col, row, st, tok = _delta_specs(nh, nc, True)

    def body(q_ref, k_ref, v_ref, gc_ref, gr_ref, bc_ref, st_ref, do_ref,
             dq_ref, dk_ref, dv_ref, dgc_ref, dgr_ref, dbc_ref, dstate):
        @pl.when(pl.program_id(1) == 0)
        def _():
            dstate[...] = jnp.zeros_like(dstate)

        _, vjp = jax.vjp(functools.partial(_delta_chunk, _DELTA_OPS_VJP), st_ref[...], _by_head(q_ref), _by_head(k_ref),
                         _by_head(v_ref), gc_ref[...], gr_ref[...], bc_ref[...])
        ds, dq, dk, dv, dgc, dgr, dbc = vjp((dstate[...], _by_head(do_ref)))
        for j, s in enumerate(_heads(hb * d)):
            dq_ref[:, s], dk_ref[:, s], dv_ref[:, s] = dq[j], dk[j], dv[j]
        dgc_ref[...], dgr_ref[...], dbc_ref[...] = dgc, dgr, dbc
        dstate[...] = ds

    tok_out = jax.ShapeDtypeStruct((t, nh * d), F32)
    return _pcall(body, name=name, grid=(ng, nc), in_specs=qkv_s + [col, row, col, st, tok],
                  out_specs=[tok, tok, tok, col, row, col],
                  out_shape=[tok_out, tok_out, tok_out, jax.ShapeDtypeStruct(gcol.shape, F32),
                             jax.ShapeDtypeStruct(grow.shape, F32), jax.ShapeDtypeStruct(bcol.shape, F32)],
                  scratch_shapes=[pltpu.VMEM((hb, d, d), F32)],
                  compiler_params=_params(("parallel", "arbitrary")))(qkv, qkv, qkv, gcol, grow, bcol, states, do)


def _outgate_head(o_h, z_h, w):
    return _rms(o_h, w) * _silu(z_h)


def _outgate_fwd(o, z, w, name):
    def fn(o_, z_, w_):
        return (jnp.concatenate([_outgate_head(o_[:, s], z_[:, s], w_) for s in _heads(o_.shape[1])], axis=1),)

    return _rowcall(name, fn, [o, z], [w], [(o.shape[1], BF16)])[0]


def _outgate_bwd(o, z, w, dy, name):
    def fn(o_, z_, dy_, w_):
        do, dz, dw = [], [], None
        for s in _heads(o_.shape[1]):
            _, vjp = jax.vjp(_outgate_head, o_[:, s], z_[:, s], w_)
            a, b, c = vjp(dy_[:, s])
            do.append(a), dz.append(b)
            dw = c if dw is None else dw + c
        return jnp.concatenate(do, axis=1), jnp.concatenate(dz, axis=1), dw

    wd = o.shape[1]
    return _rowcall(name, fn, [o, z, dy], [w], [(wd, F32), (wd, BF16)], [w.shape])


def _attn_head(dot, q_h, k_h, v_h):
    s = dot(q_h, k_h, 'nt') * (q_h.shape[1] ** -0.5)
    e = jnp.exp(s - lax.stop_gradient(jnp.max(s, axis=-1, keepdims=True)))
    p = e / jnp.sum(e, axis=-1, keepdims=True)
    return dot(p, v_h, 'nn')


def _xa_slices(width):
    hd = width // XA_HEADS
    return [slice(h * hd, (h + 1) * hd) for h in range(XA_HEADS)]


def _attn_fwd(q, kv, name):
    wd = q.shape[1]

    def fn(q_, kv_):
        k_, v_ = kv_[:, :wd], kv_[:, wd:]
        return (jnp.concatenate([_attn_head(_bdot_raw, q_[:, s], k_[:, s], v_[:, s]) for s in _xa_slices(wd)],
                                axis=1),)

    return _rowcall(name, fn, [q], [kv], [(wd, BF16)])[0]


def _attn_bwd(q, kv, do, name):
    wd = q.shape[1]

    def fn(q_, do_, kv_):
        k_, v_ = kv_[:, :wd].astype(F32), kv_[:, wd:].astype(F32)
        dq, dk, dv = [], [], []
        for s in _xa_slices(wd):
            _, vjp = jax.vjp(functools.partial(_attn_head, _bdot), q_[:, s].astype(F32), k_[:, s], v_[:, s])
            a, b, c = vjp(do_[:, s])
            dq.append(a), dk.append(b), dv.append(c)
        return jnp.concatenate(dq, axis=1), jnp.concatenate(dk + dv, axis=1)

    return _rowcall(name, fn, [q, do], [kv], [(wd, BF16)], [kv.shape])


def _loss_call(y, target, name):
    d = y.shape[1]

    def fn(y_, t_):
        err = y_ - t_
        part = 0.5 * jnp.sum(jnp.sum(err * err, axis=1, keepdims=True), axis=0, keepdims=True) / d
        return err / d, jnp.broadcast_to(part, (1, LANES))

    return _rowcall(name, fn, [y, target], [], [(d, F32)], [(1, LANES)])


def _adamw_call(w, g, m, v, name):
    c = w.shape[1]

    def fn(w_, g_, m_, v_):
        g_ = g_[:, :c]
        m1 = ADAM_B1 * m_ + (1.0 - ADAM_B1) * g_
        v1 = ADAM_B2 * v_ + (1.0 - ADAM_B2) * jnp.square(g_)
        m_hat = m1 / (1.0 - ADAM_B1 ** ADAM_STEP)
        v_hat = v1 / (1.0 - ADAM_B2 ** ADAM_STEP)
        return g_, -ADAM_LR * (m_hat / (jnp.sqrt(v_hat) + ADAM_EPS) + ADAM_WD * w_), m1, v1

    return _rowcall(name, fn, [w, g, m, v], [], [(c, F32)] * 4, br=_divtile(w.shape[0], 128, 8))


STREAM_BLOCK_BYTES = 4 * 1024 * 1024


def _rows_for(cols, itemsize):
    return max(16, STREAM_BLOCK_BYTES // (cols * itemsize) // 16 * 16)


def _pair_add(grad, theirs, core, name):
    s, hr, cols = theirs.shape
    br = _divtile(hr, _rows_for(cols, 2), 16)
    nb = hr // br

    def body(c_ref, g_ref, t_ref, o_ref):
        o_ref[...] = (g_ref[...].astype(F32) + t_ref[...].astype(F32)).astype(o_ref.dtype)

    spec = pl.BlockSpec((None, br, cols), lambda j, i, c_ref: (j, i, 0))
    return _pcall(body, name=name, out_shape=jax.ShapeDtypeStruct(theirs.shape, BF16),
                  grid_spec=pltpu.PrefetchScalarGridSpec(
                      num_scalar_prefetch=1, grid=(s, nb),
                      in_specs=[pl.BlockSpec((None, br, cols), lambda j, i, c_ref: (j, c_ref[0] * nb + i, 0)), spec],
                      out_specs=spec),
                  compiler_params=_params(("parallel", "parallel")))(core, grad, theirs)


def _chip_sum(part, landed, kind, where, grad, layer, name):
    _, hr, cw = landed.shape
    br = _divtile(hr, _rows_for(cw, 4), 16)
    nb = hr // br

    def body(w_ref, p_ref, a_ref, b_ref, d_ref, g_ref, o_ref):
        o_ref[...] = ((p_ref[...].astype(F32) + a_ref[...].astype(F32)) + b_ref[...].astype(F32)) + d_ref[...].astype(F32)

    if kind == 'row':
        own = pl.BlockSpec((None, br, cw), lambda i, w_ref: (w_ref[0], i, 0))
    else:
        own = pl.BlockSpec((br, cw), lambda i, w_ref: (i, w_ref[0]))
    got = [pl.BlockSpec((None, br, cw), functools.partial(lambda i, w_ref, k: (k, i, 0), k=k)) for k in range(3)]
    return _pcall(body, name=name, out_shape=jax.ShapeDtypeStruct(grad.shape, F32),
                  grid_spec=pltpu.PrefetchScalarGridSpec(
                      num_scalar_prefetch=1, grid=(nb,), in_specs=[own] + got + [ANY_SPEC],
                      out_specs=pl.BlockSpec((None, br, cw), lambda i, w_ref: (layer, w_ref[1] * nb + i, 0))),
                  input_output_aliases={5: 0},
                  compiler_params=_params(("parallel",)))(where, part, landed, landed, landed, grad)


def _position():
    x, y, c = lax.axis_index("x"), lax.axis_index("y"), lax.axis_index("c")
    others = [(1 - x, y), (x, 1 - y), (1 - x, 1 - y)]
    return x, y, c, others


def _any_specs(n):
    return [pl.BlockSpec(memory_space=pl.ANY)] * n


def _ds(start, size):
    return pl.ds(pl.multiple_of(start, size), size)


HBM_SPEC = pl.BlockSpec(memory_space=pltpu.HBM)
SEM_SPEC = pl.BlockSpec(memory_space=pltpu.SEMAPHORE)
ANY_SPEC = pl.BlockSpec(memory_space=pl.ANY)
DATAFLOW = pltpu.SideEffectType.DATAFLOW_SIDE_EFFECTING


def _hbm(a):
    return pltpu.with_memory_space_constraint(a, pltpu.HBM)


def _full_shape(shard_shape, kind):
    r, cw = shard_shape
    return (N_CHIPS, r, cw) if kind == 'row' else (r, N_CHIPS * cw)


def _block_dims(full, kind):
    return (full.shape[1], full.shape[2]) if kind == 'row' else (full.shape[0], full.shape[1] // N_CHIPS)


def _region(full, kind, chip, half):
    if kind == 'all':
        return full.at[chip]
    r, cw = _block_dims(full, kind)
    if kind == 'row':
        return full.at[chip, _ds(half * (r // 2), r // 2), :]
    return full.at[_ds(half * (r // 2), r // 2), _ds(chip * cw, cw)]


def _place_block(full, shards, layer, kind, chip, name, after=None):
    _, r, cs = shards.shape
    cw = _block_dims(full, kind)[1]
    br = _divtile(r, 256, 16)

    def body(c_ref, s_ref, *rest):
        o_ref = rest[-1]
        if cs == cw:
            o_ref[...] = s_ref[...].astype(o_ref.dtype)
        else:
            o_ref[:, :cs] = s_ref[...].astype(o_ref.dtype)
            o_ref[:, cs:] = jnp.zeros((br, cw - cs), o_ref.dtype)

    if kind == 'row':
        out_spec = pl.BlockSpec((None, br, cw), lambda i, c_ref: (c_ref[0], i, 0))
    else:
        out_spec = pl.BlockSpec((br, cw), lambda i, c_ref: (i, c_ref[0]))
    extra = _as_list(after)
    return _pcall(body, name=name, out_shape=jax.ShapeDtypeStruct(full.shape, full.dtype),
                  grid_spec=pltpu.PrefetchScalarGridSpec(
                      num_scalar_prefetch=1, grid=(r // br,),
                      in_specs=[pl.BlockSpec((None, br, cs), lambda i, c_ref: (layer, i, 0)), ANY_SPEC]
                      + [ANY_SPEC] * len(extra), out_specs=out_spec),
                  input_output_aliases={2: 0}, compiler_params=_params(("parallel",)))(chip, shards, full, *extra)


def _split_copy_start(bufs, lands, plan, after, name):
    nb, nl = len(bufs), len(lands)
    extra = _as_list(after)

    def body(*refs):
        ins = refs[:nb + nl]
        send, recv = refs[nb + nl + len(extra)], refs[nb + nl + len(extra) + 1]
        for cp in plan(ins[:nb], ins[nb:], send, recv):
            cp.start()
        refs[-1][...] = jnp.zeros_like(refs[-1])

    n_copies = plan.n_copies
    out = _pcall(
        body, name=name,
        out_shape=(pltpu.SemaphoreType.DMA((n_copies,)), pltpu.SemaphoreType.DMA((n_copies,)),
                   *[pltpu.HBM(a.shape, a.dtype) for a in (*bufs, *lands)], jax.ShapeDtypeStruct((8, LANES), F32)),
        in_specs=[HBM_SPEC] * (nb + nl) + [ANY_SPEC] * len(extra),
        out_specs=(SEM_SPEC, SEM_SPEC, *[HBM_SPEC] * (nb + nl), pl.BlockSpec(memory_space=pltpu.VMEM)),
        input_output_aliases={i: 2 + i for i in range(nb + nl)},
        compiler_params=pltpu.CompilerParams(has_side_effects=DATAFLOW),
    )(*[_hbm(a) for a in (*bufs, *lands)], *extra)
    return out[0], out[1], list(out[2:2 + nb]), list(out[2 + nb:2 + nb + nl]), out[-1]


def _split_copy_wait(send, recv, bufs, lands, plan, after, name):
    nb, nl = len(bufs), len(lands)
    extra = _as_list(after)

    def body(*refs):
        ins = refs[:nb + nl]
        send_ref, recv_ref = refs[nb + nl], refs[nb + nl + 1]
        for cp in plan(ins[:nb], ins[nb:], send_ref, recv_ref):
            cp.wait_send()
            cp.wait_recv()

    out = _pcall(
        body, name=name, out_shape=tuple(pltpu.HBM(a.shape, a.dtype) for a in (*bufs, *lands)),
        in_specs=[HBM_SPEC] * (nb + nl) + [SEM_SPEC, SEM_SPEC] + [ANY_SPEC] * len(extra),
        out_specs=tuple([HBM_SPEC] * (nb + nl)), input_output_aliases={i: i for i in range(nb + nl)},
        compiler_params=pltpu.CompilerParams(has_side_effects=DATAFLOW),
    )(*bufs, *lands, send, recv, *extra)
    return list(out[:nb]), list(out[nb:])


def _gather_plan(kinds):
    def plan(bufs, lands, send, recv):
        x, y, c, others = _position()
        me = 2 * x + y
        return [pltpu.make_async_remote_copy(
            src_ref=_region(lands[w], kinds[w], me, c), dst_ref=_region(lands[w], kinds[w], me, c),
            send_sem=send.at[3 * w + k], recv_sem=recv.at[3 * w + k], device_id=(px, py, c), device_id_type=MESH_IDS)
            for w in range(len(lands)) for k, (px, py) in enumerate(others)]

    def wait_plan(bufs, lands, send, recv):
        x, y, c, others = _position()
        me = 2 * x + y
        return [pltpu.make_async_remote_copy(
            src_ref=_region(lands[w], kinds[w], me, c), dst_ref=_region(lands[w], kinds[w], 2 * px + py, c),
            send_sem=send.at[3 * w + k], recv_sem=recv.at[3 * w + k], device_id=(px, py, c), device_id_type=MESH_IDS)
            for w in range(len(lands)) for k, (px, py) in enumerate(others)]

    plan.n_copies = wait_plan.n_copies = 3 * len(kinds)
    return plan, wait_plan


def _gather_forward(fulls, kinds, name):
    n = len(fulls)

    def body(*refs):
        dst = refs[n:2 * n]
        send, recv = refs[2 * n:]
        x, y, c, others = _position()
        sib = (x, y, 1 - c)

        def copy(w, k, chip, half):
            reg = _region(dst[w], kinds[w], chip, half)
            return pltpu.make_async_remote_copy(src_ref=reg, dst_ref=reg, send_sem=send.at[3 * w + k],
                                                recv_sem=recv.at[3 * w + k], device_id=sib, device_id_type=MESH_IDS)

        give = [copy(w, k, 2 * px + py, c) for w in range(n) for k, (px, py) in enumerate(others)]
        for cp in give:
            cp.start()
        for w in range(n):
            for k, (px, py) in enumerate(others):
                copy(w, k, 2 * px + py, 1 - c).wait_recv()
        for cp in give:
            cp.wait_send()

    return _pcall(body, name=name, in_specs=_any_specs(n), out_specs=_any_specs(n),
                  out_shape=[jax.ShapeDtypeStruct(f.shape, f.dtype) for f in fulls],
                  input_output_aliases={i: i for i in range(n)},
                  scratch_shapes=[pltpu.SemaphoreType.DMA((3 * n,)), pltpu.SemaphoreType.DMA((3 * n,))])(*fulls)


def _grad_half(ref, kind, half):
    if kind == 'row':
        hr = ref.shape[1] // 2
        return ref.at[:, _ds(half * hr, hr), :]
    hr = ref.shape[0] // 2
    return ref.at[_ds(half * hr, hr), :]


def _half_shape(g, kind):
    return (g.shape[0], g.shape[1] // 2, g.shape[2]) if kind == 'row' else (g.shape[0] // 2, g.shape[1])


def _pair_plan(kinds):
    def plan(bufs, lands, send, recv):
        x, y, c, _ = _position()
        return [pltpu.make_async_remote_copy(src_ref=_grad_half(bufs[w], kinds[w], 1 - c), dst_ref=lands[w],
                                             send_sem=send.at[w], recv_sem=recv.at[w], device_id=(x, y, 1 - c),
                                             device_id_type=MESH_IDS) for w in range(len(bufs))]

    plan.n_copies = len(kinds)
    return plan


def _pair_exchange(grads, kinds, name):
    n = len(grads)

    def body(*refs):
        src, theirs_ref = refs[:n], refs[n:2 * n]
        send, recv = refs[2 * n:]
        x, y, c, _ = _position()
        sib = (x, y, 1 - c)

        def half(w, h):
            if kinds[w] == 'row':
                hr = src[w].shape[1] // 2
                return src[w].at[:, _ds(h * hr, hr), :]
            hr = src[w].shape[0] // 2
            return src[w].at[_ds(h * hr, hr), :]

        give = [pltpu.make_async_remote_copy(src_ref=half(w, 1 - c), dst_ref=theirs_ref[w], send_sem=send.at[w],
                                             recv_sem=recv.at[w], device_id=sib, device_id_type=MESH_IDS)
                for w in range(n)]
        for cp in give:
            cp.start()
        for cp in give:
            cp.wait()

    def half_shape(g, kind):
        return (g.shape[0], g.shape[1] // 2, g.shape[2]) if kind == 'row' else (g.shape[0] // 2, g.shape[1])

    return _pcall(body, name=name, in_specs=_any_specs(n), out_specs=_any_specs(n),
                  out_shape=[jax.ShapeDtypeStruct(half_shape(g, kd), g.dtype) for g, kd in zip(grads, kinds)],
                  scratch_shapes=[pltpu.SemaphoreType.DMA((n,)), pltpu.SemaphoreType.DMA((n,))])(*grads)


def _part_dims(part, kind):
    return (part.shape[1], part.shape[2]) if kind == 'row' else (part.shape[0], part.shape[1] // N_CHIPS)


def _chip_plan(kinds):
    def plan(bufs, lands, send, recv):
        x, y, c, others = _position()

        def slot(w, chip):
            if kinds[w] == 'row':
                return bufs[w].at[chip]
            cw = _part_dims(bufs[w], kinds[w])[1]
            return bufs[w].at[:, _ds(chip * cw, cw)]

        return [pltpu.make_async_remote_copy(src_ref=slot(w, 2 * px + py), dst_ref=lands[w].at[k],
                                             send_sem=send.at[3 * w + k], recv_sem=recv.at[3 * w + k],
                                             device_id=(px, py, c), device_id_type=MESH_IDS)
                for w in range(len(bufs)) for k, (px, py) in enumerate(others)]

    plan.n_copies = 3 * len(kinds)
    return plan


def _pair_gather_plan(n, n_layers):
    def copies(lands, send, recv, take):
        x, y, c, _ = _position()
        out = []
        for w in range(n):
            hr = lands[w].shape[1] // 2
            for l in range(n_layers):
                mine = lands[w].at[l, _ds(c * hr, hr), :]
                theirs = lands[w].at[l, _ds((1 - c) * hr, hr), :]
                out.append(pltpu.make_async_remote_copy(
                    src_ref=mine, dst_ref=theirs if take else mine, send_sem=send.at[w * n_layers + l],
                    recv_sem=recv.at[w * n_layers + l], device_id=(x, y, 1 - c), device_id_type=MESH_IDS))
        return out

    def plan(bufs, lands, send, recv):
        return copies(lands, send, recv, False)

    def wait_plan(bufs, lands, send, recv):
        return copies(lands, send, recv, True)

    plan.n_copies = wait_plan.n_copies = n * n_layers
    return plan, wait_plan


def _pair_gather(grads, name, after=None):
    n = len(grads)
    n_layers = grads[0].shape[0]
    extra = _as_list(after)

    def body(*refs):
        dst = refs[n + len(extra):2 * n + len(extra)]
        send, recv = refs[2 * n + len(extra):]
        x, y, c, _ = _position()
        sib = (x, y, 1 - c)

        def copy(w, l, half):
            hr = dst[w].shape[1] // 2
            rows = dst[w].at[l, _ds(half * hr, hr), :]
            return pltpu.make_async_remote_copy(src_ref=rows, dst_ref=rows, send_sem=send.at[w * n_layers + l],
                                                recv_sem=recv.at[w * n_layers + l], device_id=sib,
                                                device_id_type=MESH_IDS)

        give = [copy(w, l, c) for w in range(n) for l in range(n_layers)]
        for cp in give:
            cp.start()
        for w in range(n):
            for l in range(n_layers):
                copy(w, l, 1 - c).wait_recv()
        for cp in give:
            cp.wait_send()

    m = n * n_layers
    return _pcall(body, name=name, in_specs=_any_specs(n + len(extra)), out_specs=_any_specs(n),
                  out_shape=[jax.ShapeDtypeStruct(g.shape, g.dtype) for g in grads],
                  input_output_aliases={i: i for i in range(n)},
                  scratch_shapes=[pltpu.SemaphoreType.DMA((m,)), pltpu.SemaphoreType.DMA((m,))])(*grads, *extra)


def _all_gather_plan():
    flips = [(fx, fy, fc) for fx in (0, 1) for fy in (0, 1) for fc in (0, 1)][1:]

    def copies(bufs, lands, send, recv, take):
        x, y, c, _ = _position()
        out = []
        for k, f in enumerate(flips):
            px, py, pc = (1 - x if f[0] else x, 1 - y if f[1] else y, 1 - c if f[2] else c)
            slot = 4 * px + 2 * py + pc if take else 4 * x + 2 * y + c
            out.append(pltpu.make_async_remote_copy(src_ref=bufs[0], dst_ref=lands[0].at[slot], send_sem=send.at[k],
                                                    recv_sem=recv.at[k], device_id=(px, py, pc),
                                                    device_id_type=MESH_IDS))
        return out

    def plan(bufs, lands, send, recv):
        return copies(bufs, lands, send, recv, False)

    def wait_plan(bufs, lands, send, recv):
        return copies(bufs, lands, send, recv, True)

    plan.n_copies = wait_plan.n_copies = N_DEV - 1
    return plan, wait_plan


def _sum_slots(land, name):
    n, rows, cols = land.shape

    def fn(*slots):
        total = slots[0]
        for s in slots[1:]:
            total = total + s
        return (total,)

    flat = land.reshape(n * rows, cols)
    return _rowcall(name, fn, [(flat, d * rows) for d in range(n)], [], [(cols, F32)], br=_divtile(rows, 256, 8),
                    nrows=rows)[0]


def _pack(arrays):
    flat = jnp.concatenate([a.reshape(-1).astype(F32) for a in arrays])
    pad = (-flat.shape[0]) % (8 * LANES)
    return jnp.pad(flat, (0, pad)).reshape(-1, LANES)


def _unpack(buf, like):
    flat, out, off = buf.reshape(-1), [], 0
    for a in like:
        out.append(flat[off:off + a.size].reshape(a.shape))
        off += a.size
    return out


def _row2(v):
    return v.reshape(1, -1)


def _ffn_fwd(x, g_pre, w_gu, w_d, g_post, tag, after=None):
    h = _norm_fwd(x, g_pre, tag + "_pre", after)
    gu, a = _ffn_up(h, w_gu, tag + "_gu")
    f = _mm(a, w_d, 'nn', F32, tag + "_down")
    return _resnorm_fwd(x, f, g_post, 0.5, tag + "_post"), (x, h, gu, a, f)


def _ffn_bwd_weights(dx_out, saved, w_d, g_post, tag, after=None):
    x, h, gu, a, f = saved
    df, dg_post = _resnorm_bwd(f, dx_out, g_post, 0.5, tag + "_post_b", after)
    dw_d = _mm(a, df, 'tn', BF16, tag + "_down_bw")
    dgu = _ffn_down_bwd(df, w_d, gu, tag + "_down_bx")
    dw_gu = _mm(h, dgu, 'tn', BF16, tag + "_gu_bw", halves='b')
    return dgu, dw_gu, dw_d, dg_post


def _ffn_bwd_input(dx_out, saved, dgu, g_pre, w_gu, tag, after=None):
    x = saved[0]
    dh = _mm(dgu, w_gu, 'nt', F32, tag + "_gu_bx", halves='a')
    return _norm_bwd(x, dh, dx_out, g_pre, tag + "_pre_b", after)


def _ffn_bwd(dx_out, saved, g_pre, w_gu, w_d, g_post, tag, after=None):
    dgu, dw_gu, dw_d, dg_post = _ffn_bwd_weights(dx_out, saved, w_d, g_post, tag, after)
    dx, dg_pre = _ffn_bwd_input(dx_out, saved, dgu, g_pre, w_gu, tag)
    return dx, dg_pre, dw_gu, dw_d, dg_post


def _split_cols(pp, cs, cp, widths):
    proj = jnp.concatenate([pp[:, j * cp:j * cp + cs] for j in range(N_CHIPS)], axis=1)
    out, off = [], 0
    for wd in widths:
        out.append(proj[:, off:off + wd])
        off += wd
    return out


def _join_cols(pieces, cs, cp):
    proj = jnp.concatenate([pc.astype(BF16) for pc in pieces], axis=1)
    t = proj.shape[0]
    cols = []
    for j in range(N_CHIPS):
        cols += [proj[:, j * cs:(j + 1) * cs], jnp.zeros((t, cp - cs), proj.dtype)]
    return jnp.concatenate(cols, axis=1)


def _mix_fwd(x, p, w_in, w_out, conv_w, cs, cp, tag, after=None):
    t, d = x.shape
    half = d // 2
    nh = half // HEAD_DIM
    h = _norm_fwd(x, p['mix_norm_pre'], tag + "_pre", after)
    pp = _mm(h, w_in, 'nn', F32, tag + "_in")
    u_a, v_a, qkv_raw, z, b_raw, a_raw = _split_cols(pp, cs, cp, [half, half, 3 * half, half, nh, nh])
    y_a = _sgu_fwd(u_a, v_a, p['sm_w'], p['sm_b'], p['sm_ln_g'], p['sm_ln_b'], tag + "_sgu")
    qkv = _conv_fwd(qkv_raw, conv_w, tag + "_conv")
    b_raw, a_raw = b_raw.T, a_raw.T
    beta, gcum = _gates_fwd(b_raw, a_raw, p['a_log'].T, p['dt_bias'].T, tag + "_gates")
    gcol, bcol = gcum[:, :, None], beta[:, :, None]
    grow = gcum.reshape(nh, t // DN_CHUNK, 1, DN_CHUNK)
    o, states = _delta_fwd(qkv, gcol, grow, bcol, tag + "_delta")
    y_b = _outgate_fwd(o, z, p['dn_norm_w'], tag + "_gate")
    y = jnp.concatenate([y_a, y_b], axis=1)
    m = _mm(y, w_out, 'nn', F32, tag + "_out")
    x_out = _resnorm_fwd(x, m, p['mix_norm_post'], 1.0, tag + "_post")
    return x_out, (x, h, u_a, v_a, qkv_raw, z, b_raw, a_raw, qkv, gcol, grow, bcol, states, o, y, m)


def _mix_bwd(dx_out, saved, p, w_in, w_out, conv_w, cs, cp, tag, after=None):
    x, h, u_a, v_a, qkv_raw, z, b_raw, a_raw, qkv, gcol, grow, bcol, states, o, y, m = saved
    t, d = x.shape
    half = d // 2
    nh = half // HEAD_DIM
    gr = {}
    dm, gr['mix_norm_post'] = _resnorm_bwd(m, dx_out, p['mix_norm_post'], 1.0, tag + "_post_b", after)
    dy = _mm(dm, w_out, 'nt', F32, tag + "_out_bx")
    gr['w_out'] = _mm(y, dm, 'tn', BF16, tag + "_out_bw")
    dy_a, dy_b = dy[:, :half], dy[:, half:]
    do, dz, gr['dn_norm_w'] = _outgate_bwd(o, z, p['dn_norm_w'], dy_b, tag + "_gate_b")
    dq, dk, dv, dgcol, dgrow, dbcol = _delta_bwd(qkv, gcol, grow, bcol, states, do, tag + "_delta_b")
    dgcum = dgcol[:, :, 0] + dgrow.reshape(nh, t)
    db_raw, da_raw, gr['a_log'], gr['dt_bias'] = _gates_bwd(b_raw, a_raw, p['a_log'].T, p['dt_bias'].T,
                                                             dbcol[:, :, 0], dgcum, tag + "_gates_b")
    db_raw, da_raw = db_raw.T, da_raw.T
    thirds = [_conv_bwd(qkv_raw, conv_w, dpart, i * half, tag + "_conv_b") for i, dpart in enumerate((dq, dk, dv))]
    gr['conv_w'] = jnp.concatenate([dw for _, dw in thirds], axis=1)
    du_a, dv_a, gr['sm_w'], gr['sm_b'], gr['sm_ln_g'], gr['sm_ln_b'] = _sgu_bwd(
        u_a, v_a, p['sm_w'], p['sm_b'], p['sm_ln_g'], p['sm_ln_b'], dy_a, tag + "_sgu_b")
    dpp = _join_cols([du_a, dv_a, *[dx for dx, _ in thirds], dz, db_raw, da_raw], cs, cp).astype(BF16)
    dh = _mm(dpp, w_in, 'nt', F32, tag + "_in_bx")
    gr['w_in'] = _mm(h, dpp, 'tn', BF16, tag + "_in_bw")
    dx, gr['mix_norm_pre'] = _norm_bwd(x, dh, dx_out, p['mix_norm_pre'], tag + "_pre_b")
    return dx, gr


def _xa_fwd(x, mem, p, w_xq, w_xkv, w_xo, tag, after=None):
    h = _norm_fwd(x, p['xa_norm_pre'], tag + "_pre", after)
    mh = _norm_fwd(mem, p['mem_norm'], tag + "_mem")
    q = _mm(h, w_xq, 'nn', BF16, tag + "_q")
    kv = _mm(mh, w_xkv, 'nn', BF16, tag + "_kv")
    o = _attn_fwd(q, kv, tag + "_attn")
    c = _mm(o, w_xo, 'nn', F32, tag + "_o")
    return _resnorm_fwd(x, c, p['xa_norm_post'], 1.0, tag + "_post"), (x, h, mh, q, kv, o, c)


def _xa_bwd(dx_out, saved, mem, p, w_xq, w_xkv, w_xo, tag, after=None):
    x, h, mh, q, kv, o, c = saved
    gr = {}
    dc, gr['xa_norm_post'] = _resnorm_bwd(c, dx_out, p['xa_norm_post'], 1.0, tag + "_post_b", after)
    do = _mm(dc, w_xo, 'nt', F32, tag + "_o_bx")
    gr['w_xo'] = _mm(o, dc, 'tn', BF16, tag + "_o_bw")
    dq, dkv = _attn_bwd(q, kv, do, tag + "_attn_b")
    dkv = dkv.astype(BF16)
    dh = _mm(dq, w_xq, 'nt', F32, tag + "_q_bx")
    gr['w_xq'] = _mm(h, dq, 'tn', BF16, tag + "_q_bw")
    dmh = _mm(dkv, w_xkv, 'nt', F32, tag + "_kv_bx")
    gr['w_xkv'] = _mm(mh, dkv, 'tn', BF16, tag + "_kv_bw")
    gr['mem_norm'] = _norm_bwd_gain(mem, dmh, p['mem_norm'], tag + "_mem_b")
    dx, gr['xa_norm_pre'] = _norm_bwd(x, dh, dx_out, p['xa_norm_pre'], tag + "_pre_b")
    return dx, gr


def _step(inputs):
    x, mem, target = inputs['x'][0], inputs['mem'][0], inputs['loss_target'][0]
    n_layers = inputs['w_in'].shape[0]
    cx, cy, cc = lax.axis_index("x"), lax.axis_index("y"), lax.axis_index("c")
    chip = 2 * cx + cy
    cs = inputs['w_in'].shape[2]
    cp = -(-cs // LANES) * LANES
    kinds = [BIG[nm] for nm in BIG_NAMES]

    conv_local = inputs['conv_w']
    ccols = conv_local.shape[2]

    core_idx, chip_idx = cc.astype(jnp.int32).reshape(1), chip.astype(jnp.int32).reshape(1)
    where_idx = jnp.stack([chip, cc]).astype(jnp.int32)
    order = [(l, gi) for l in range(n_layers) for gi in range(len(GROUPS))]
    n_groups = len(order)

    def small(l):
        p = {nm: inputs[nm][l] for nm in SMALL_NAMES}
        for nm in SMALL_NAMES:
            if p[nm].ndim == 1:
                p[nm] = _row2(p[nm])
        p['sm_b'] = p['sm_b'][:, :, None]
        return p

    def place(k, after=None):
        l, gi = order[k]
        lands = []
        for nm in GROUPS[gi][1]:
            _, r, c_in = inputs[nm].shape
            shape = _full_shape((r, cp if nm == 'w_in' else c_in), BIG[nm])
            lands.append(_place_block(lax.empty(shape, BF16), w_src[nm], l, BIG[nm], chip_idx, "place_block", after))
        return lands

    w_src = {nm: inputs[nm].astype(BF16) if nm == 'w_in' else inputs[nm] for nm in BIG_NAMES}

    placed = {}
    conv_group = 1

    def gather_start(k, after):
        names = list(GROUPS[order[k][1]][1])
        kds = [BIG[nm] for nm in names]
        if k not in placed:
            placed[k] = place(k)
        lands = list(placed[k])
        if k == conv_group:
            mine = conv_local.reshape(1, n_layers * CONV_WIDTH, ccols)
            lands.append(lax.dynamic_update_slice(jnp.zeros((N_CHIPS, *mine.shape[1:]), F32), mine, (chip, 0, 0)))
            names.append('conv_w'), kds.append('all')
        plan, wait_plan = _gather_plan(kds)
        send, recv, _, lands, token = _split_copy_start([], lands, plan, after, f"gather_start_{k}")
        return dict(send=send, recv=recv, lands=lands, token=token, kinds=kds, names=names, wait_plan=wait_plan)

    def gather_finish(k, st, after):
        _, lands = _split_copy_wait(st['send'], st['recv'], [], st['lands'], st['wait_plan'], after, f"gather_wait_{k}")
        got = dict(zip(st['names'], lands))
        halves = [nm for nm, kd in zip(st['names'], st['kinds']) if kd != 'all']
        passed = _gather_forward([got[nm] for nm in halves], [BIG[nm] for nm in halves],
                                 "gather_forward_" + GROUPS[order[k][1]][0])
        w = {nm: (g.reshape(-1, g.shape[2]) if BIG[nm] == 'row' else g) for nm, g in zip(halves, passed)}
        if 'conv_w' in got:
            w['conv_w'] = got['conv_w'].reshape(N_CHIPS, n_layers, CONV_WIDTH, ccols).transpose(1, 2, 0, 3).reshape(
                n_layers, CONV_WIDTH, N_CHIPS * ccols)
        return w

    def group_fwd(gi, x_, p, w, l, token):
        if gi == 0:
            return _ffn_fwd(x_, p['ffn1_norm_pre'], w['ffn1_w_gate_up'], w['ffn1_w_down'], p['ffn1_norm_post'], "ffn", token)
        if gi == 1:
            return _mix_fwd(x_, p, w['w_in'], w['w_out'], conv_full[l], cs, cp, "mix", token)
        if gi == 2:
            return _xa_fwd(x_, mem, p, w['w_xq'], w['w_xkv'], w['w_xo'], "xa", token)
        return _ffn_fwd(x_, p['ffn2_norm_pre'], w['ffn2_w_gate_up'], w['ffn2_w_down'], p['ffn2_norm_post'], "ffn", token)

    def group_bwd(gi, dx_, s, p, w, l, token):
        if gi in (0, 3):
            pre = 'ffn1' if gi == 0 else 'ffn2'
            dx_, g_pre, g_gu, g_d, g_post = _ffn_bwd(dx_, s, p[pre + '_norm_pre'], w[pre + '_w_gate_up'],
                                                     w[pre + '_w_down'], p[pre + '_norm_post'], "ffn", token)
            return dx_, {pre + '_norm_pre': g_pre, pre + '_w_gate_up': g_gu, pre + '_w_down': g_d,
                         pre + '_norm_post': g_post}
        if gi == 1:
            return _mix_bwd(dx_, s, p, w['w_in'], w['w_out'], conv_full[l], cs, cp, "mix", token)
        return _xa_bwd(dx_, s, mem, p, w['w_xq'], w['w_xkv'], w['w_xo'], "xa", token)

    started = {0: gather_start(0, None)}
    started[1] = gather_start(1, started[0]['token'])
    weights, saved = [], []
    for k in range(2, n_groups):
        placed[k] = place(k, started[1]['token'])
    head_work = [started[1]['token']] + [land for k in range(2, n_groups) for land in placed[k]]
    head_work += [inputs['m_w_in'].reshape(-1, cs), inputs['v_w_in'].reshape(-1, cs)]
    conv_full = None
    for k, (l, gi) in enumerate(order):
        w = gather_finish(k, started[k], x if k > 0 else head_work)
        if k == conv_group:
            conv_full = w.pop('conv_w')
        token = None
        if k + 2 < n_groups:
            started[k + 2] = gather_start(k + 2, next(iter(w.values())))
            token = started[k + 2]['token']
        weights.append(w)
        x, s = group_fwd(gi, x, small(l), w, l, token)
        saved.append(s)

    dx, loss_part = _loss_call(x, target, "loss")

    grads = [dict() for _ in range(n_layers)]
    big_grads = {nm: None for nm in BIG_NAMES}
    pair_gathers = {}

    def reduce_finish(pd, after):
        parts, lands = _split_copy_wait(pd['send'], pd['recv'], pd['parts'], pd['lands'], pd['plan'], after,
                                        f"chip_wait_{pd['k']}")
        for nm, kd, part, landed in zip(pd['names'], pd['kinds'], parts, lands):
            if big_grads[nm] is None:
                big_grads[nm] = lax.empty((n_layers, 2 * landed.shape[1], landed.shape[2]), F32)
            big_grads[nm] = _chip_sum(part, landed, kd, where_idx, big_grads[nm], pd['l'], "chip_sum")
        if pd['l'] == 0 and pd['k'] > 0:
            names = pd['names']
            plan, wait_plan = _pair_gather_plan(len(names), n_layers)
            send, recv, _, got, _ = _split_copy_start([], [big_grads[nm] for nm in names], plan, None,
                                                      f"pair_gather_start_{pd['k']}")
            pair_gathers[pd['k']] = dict(send=send, recv=recv, lands=got, wait_plan=wait_plan)

    def by_chip(gr, names, kds):
        return [gr[nm].reshape(N_CHIPS, -1, gr[nm].shape[1]) if kd == 'row' else gr[nm] for nm, kd in zip(names, kds)]

    def chip_start(k, names, kds, gs, theirs):
        parts = [_pair_add(g, t, core_idx, "pair_add") if kd == 'row'
                 else _pair_add(g[None], t[None], core_idx, "pair_add")[0] for g, t, kd in zip(gs, theirs, kds)]
        lands = [lax.empty((3, *_part_dims(part, kd)), BF16) for part, kd in zip(parts, kds)]
        plan = _chip_plan(kds)
        send, recv, parts, lands, token = _split_copy_start(parts, lands, plan, None, f"chip_start_{k}")
        return dict(send=send, recv=recv, parts=parts, lands=lands, token=token, plan=plan, names=names, kinds=kds,
                    k=k, l=order[k][0])

    pair_pending, chip_queue = None, []
    for k in reversed(range(n_groups)):
        l, gi = order[k]
        names = GROUPS[gi][1]
        kds = [BIG[nm] for nm in names]
        p, w = small(l), weights[k]
        tokens = [pd['token'] for pd in ([pair_pending] if pair_pending else []) + chip_queue] or None
        if k > 0:
            dx_next, gr = group_bwd(gi, dx, saved[k], p, w, l, tokens)
        else:
            dgu, g_gu, g_d, g_post = _ffn_bwd_weights(dx, saved[k], w['ffn1_w_down'], p['ffn1_norm_post'], "ffn", tokens)
            gr = {'ffn1_w_gate_up': g_gu, 'ffn1_w_down': g_d, 'ffn1_norm_post': g_post}
            dx_next = g_gu
        for pd in chip_queue:
            reduce_finish(pd, dx_next)
        chip_queue = []
        if pair_pending:
            pd = pair_pending
            gs_prev, theirs = _split_copy_wait(pd['send'], pd['recv'], pd['gs'], pd['lands'], pd['plan'], dx_next,
                                               f"pair_wait_{pd['k']}")
            chip_queue.append(chip_start(pd['k'], pd['names'], pd['kinds'], gs_prev, theirs))
            pair_pending = None
        gs = by_chip(gr, names, kds)
        if k > 1:
            plan = _pair_plan(kds)
            lands = [lax.empty(_half_shape(g, kd), BF16) for g, kd in zip(gs, kds)]
            send, recv, gs, lands, token = _split_copy_start(gs, lands, plan, None, f"pair_start_{k}")
            pair_pending = dict(send=send, recv=recv, gs=gs, lands=lands, token=token, plan=plan, names=names,
                                kinds=kds, k=k)
        else:
            started_now = chip_start(k, names, kds, gs, _pair_exchange(gs, kds, "pair_exchange_" + GROUPS[gi][0]))
            if k > 0:
                chip_queue.append(started_now)
        if k > 0:
            dx = dx_next
        else:
            pending = started_now
            dx, gr['ffn1_norm_pre'] = _ffn_bwd_input(dx, saved[k], dgu, p['ffn1_norm_pre'], w['ffn1_w_gate_up'], "ffn",
                                                     pending['token'])
        grads[l].update({nm: g for nm, g in gr.items() if nm not in BIG})
    for pd in chip_queue:
        reduce_finish(pd, dx)
    out = {}

    def two(a):
        return a.reshape(-1, a.shape[-1])

    def adamw(nm, g):
        res = _adamw_call(two(inputs[nm]), two(g), two(inputs['m_' + nm]), two(inputs['v_' + nm]), "adamw")
        out[nm] = tuple(a.reshape(inputs[nm].shape) for a in res)
        return res[1]

    def finish_weights(gi, after=None):
        names = GROUPS[gi][1]
        if gi in pair_gathers:
            st = pair_gathers[gi]
            _, got = _split_copy_wait(st['send'], st['recv'], [], st['lands'], st['wait_plan'], after,
                                      f"pair_gather_wait_{gi}")
        else:
            got = _pair_gather([big_grads[nm] for nm in names], "pair_gather_" + GROUPS[gi][0], after)
        return [adamw(nm, g) for nm, g in zip(names, got)]

    small_grads = [jnp.stack([grads[l][nm].reshape(inputs[nm].shape[1:]) if nm != 'conv_w' else grads[l][nm]
                              for l in range(n_layers)]) for nm in SMALL_NAMES]
    packed = _pack(small_grads + [loss_part])
    me = 4 * cx + 2 * cy + cc
    land = lax.dynamic_update_slice(lax.empty((N_DEV, *packed.shape), F32), packed[None], (me, 0, 0))
    plan, wait_plan = _all_gather_plan()
    send, recv, bufs, lands, _ = _split_copy_start([packed], [land], plan, pending['token'], "small_start")

    done = [d for gi in range(len(GROUPS) - 1, 0, -1) for d in finish_weights(gi, pending['token'])]
    reduce_finish(pending, done)
    done = finish_weights(0)

    _, lands = _split_copy_wait(send, recv, bufs, lands, wait_plan, done, "small_wait")
    summed = _unpack(_sum_slots(lands[0], "small_sum"), small_grads + [loss_part])
    loss = summed[-1][0, 0]
    small_g = dict(zip(SMALL_NAMES, summed[:-1]))
    small_g['conv_w'] = lax.dynamic_slice(small_g['conv_w'], (0, 0, chip * ccols), (n_layers, CONV_WIDTH, ccols))
    for nm in SMALL_NAMES:
        adamw(nm, small_g[nm])
    return (loss, dx[None], *[out[nm][0] for nm in WEIGHTS], *[out[nm][1] for nm in WEIGHTS],
            *[out[nm][2] for nm in WEIGHTS], *[out[nm][3] for nm in WEIGHTS])


def kernel(x, mem, ffn1_norm_pre, ffn1_w_gate_up, ffn1_w_down, ffn1_norm_post, mix_norm_pre, w_in, conv_w, a_log, dt_bias, sm_w, sm_b, sm_ln_g, sm_ln_b, dn_norm_w, w_out, mix_norm_post, xa_norm_pre, mem_norm, w_xq, w_xkv, w_xo, xa_norm_post, ffn2_norm_pre, ffn2_w_gate_up, ffn2_w_down, ffn2_norm_post, loss_target, m_ffn1_norm_pre, m_ffn1_w_gate_up, m_ffn1_w_down, m_ffn1_norm_post, m_mix_norm_pre, m_w_in, m_conv_w, m_a_log, m_dt_bias, m_sm_w, m_sm_b, m_sm_ln_g, m_sm_ln_b, m_dn_norm_w, m_w_out, m_mix_norm_post, m_xa_norm_pre, m_mem_norm, m_w_xq, m_w_xkv, m_w_xo, m_xa_norm_post, m_ffn2_norm_pre, m_ffn2_w_gate_up, m_ffn2_w_down, m_ffn2_norm_post, v_ffn1_norm_pre, v_ffn1_w_gate_up, v_ffn1_w_down, v_ffn1_norm_post, v_mix_norm_pre, v_w_in, v_conv_w, v_a_log, v_dt_bias, v_sm_w, v_sm_b, v_sm_ln_g, v_sm_ln_b, v_dn_norm_w, v_w_out, v_mix_norm_post, v_xa_norm_pre, v_mem_norm, v_w_xq, v_w_xkv, v_w_xo, v_xa_norm_post, v_ffn2_norm_pre, v_ffn2_w_gate_up, v_ffn2_w_down, v_ffn2_norm_post):
    vals = dict(locals())
    return _step(vals)
```

```python
import functools
import math

import jax
import jax.numpy as jnp
from jax import lax
from jax.experimental import pallas as pl
from jax.experimental.pallas import tpu as pltpu

F32, BF16 = jnp.float32, jnp.bfloat16
NORM_EPS = 1e-6
HEAD_DIM = 128
GM_CHUNK = 128
DN_CHUNK = 64
CONV_WIDTH = 4
XA_HEADS = 4
LANES = 128
MXU_WIDTH = 256
N_CHIPS = 4
N_DEV = 8
VMEM_LIMIT = 56 * 1024 * 1024
MESH_IDS = pl.DeviceIdType.MESH
ADAM_LR, ADAM_B1, ADAM_B2, ADAM_EPS, ADAM_WD, ADAM_STEP = 0.001, 0.9, 0.999, 1e-08, 0.01, 10

WEIGHTS = ['ffn1_norm_pre', 'ffn1_w_gate_up', 'ffn1_w_down', 'ffn1_norm_post', 'mix_norm_pre', 'w_in', 'conv_w',
           'a_log', 'dt_bias', 'sm_w', 'sm_b', 'sm_ln_g', 'sm_ln_b', 'dn_norm_w', 'w_out', 'mix_norm_post',
           'xa_norm_pre', 'mem_norm', 'w_xq', 'w_xkv', 'w_xo', 'xa_norm_post', 'ffn2_norm_pre', 'ffn2_w_gate_up',
           'ffn2_w_down', 'ffn2_norm_post']
BIG = {'ffn1_w_gate_up': 'col', 'ffn1_w_down': 'row', 'w_in': 'col', 'w_out': 'row', 'w_xq': 'row',
       'w_xkv': 'col', 'w_xo': 'row', 'ffn2_w_gate_up': 'col', 'ffn2_w_down': 'row'}
BIG_NAMES = list(BIG)
GROUPS = (('ffn1', ('ffn1_w_gate_up', 'ffn1_w_down')), ('mix', ('w_in', 'w_out')),
          ('xa', ('w_xq', 'w_xkv', 'w_xo')), ('ffn2', ('ffn2_w_gate_up', 'ffn2_w_down')))
SMALL_NAMES = [n for n in WEIGHTS if n not in BIG]


def _pcall(body, **kw):
    return pl.pallas_call(body, **kw)


def _params(sem=None):
    return pltpu.CompilerParams(dimension_semantics=sem, vmem_limit_bytes=VMEM_LIMIT)


def _as_list(after):
    if after is None:
        return []
    return list(after) if isinstance(after, (list, tuple)) else [after]


def _divtile(dim, cap, align=LANES):
    if dim <= cap:
        return dim
    t = (cap // align) * align
    while t > align and dim % t:
        t -= align
    assert dim % t == 0, (dim, cap)
    return t


_DN = {'nn': (((1,), (0,)), ((), ())), 'nt': (((1,), (1,)), ((), ())), 'tn': (((0,), (0,)), ((), ()))}


MM_MIN_STEPS = 8
MM_VMEM_BUDGET = 44 * 1024 * 1024


def _mm_tiles(m, n, k, out_itemsize, acc=False):
    def tile(dim, cap):
        return _divtile(dim, cap, MXU_WIDTH) if dim % MXU_WIDTH == 0 else _divtile(dim, cap)

    tm = tile(m, 1024)
    for cap_n, cap_k in ((2048, 2048), (1024, 2048), (2048, 1024), (1024, 1024), (1024, 512), (512, 512)):
        tn, tk = tile(n, cap_n), _divtile(k, cap_k)
        need = 2 * 2 * (tm * tk + tk * tn) + 2 * tm * tn * out_itemsize + (tm * tn * 4 if tk < k or acc else 0)
        if need <= MM_VMEM_BUDGET:
            break
    while (m // tm) * (n // tn) * (k // tk) < MM_MIN_STEPS and tn > 2 * MXU_WIDTH and n % (tn // 2) == 0:
        tn //= 2
    return tm, tn, tk


def _mm(a, b, mode, out_dtype, name, halves=None):
    if halves == 'a':
        assert mode == 'nt'
        a_shape, b_shape = (a.shape[1], 2 * a.shape[2]), b.shape
    elif halves == 'b':
        assert mode == 'tn'
        a_shape, b_shape = a.shape, (b.shape[1], 2 * b.shape[2])
    else:
        a_shape, b_shape = a.shape, b.shape
    if mode == 'nn':
        (m, k), (k2, n) = a_shape, b_shape
    elif mode == 'nt':
        (m, k), (n, k2) = a_shape, b_shape
    else:
        (k, m), (k2, n) = a_shape, b_shape
    assert k == k2, (a.shape, b.shape, mode)
    if halves == 'a':
        tm, tn, tk = _mm_tiles(m, n, k // 2, jnp.dtype(out_dtype).itemsize, acc=True)
    elif halves == 'b':
        tm, tn, tk = _mm_tiles(m, n // 2, k, jnp.dtype(out_dtype).itemsize)
    else:
        tm, tn, tk = _mm_tiles(m, n, k, jnp.dtype(out_dtype).itemsize)
    nk = k // tk
    a_spec = {'nn': pl.BlockSpec((tm, tk), lambda i, j, kk: (i, kk)),
              'nt': pl.BlockSpec((tm, tk), lambda i, j, kk: (i, kk)),
              'tn': pl.BlockSpec((tk, tm), lambda i, j, kk: (kk, i))}[mode]
    b_spec = {'nn': pl.BlockSpec((tk, tn), lambda i, j, kk: (kk, j)),
              'nt': pl.BlockSpec((tn, tk), lambda i, j, kk: (j, kk)),
              'tn': pl.BlockSpec((tk, tn), lambda i, j, kk: (kk, j))}[mode]
    if halves == 'a':
        per = nk // 2
        a_spec = pl.BlockSpec((None, tm, tk), lambda i, j, kk: (kk // per, i, kk % per))
    elif halves == 'b':
        per = (n // tn) // 2
        b_spec = pl.BlockSpec((None, tk, tn), lambda i, j, kk: (j // per, kk, j % per))

    def dot(a_ref, b_ref):
        return lax.dot_general(a_ref[...].astype(BF16), b_ref[...].astype(BF16), _DN[mode], preferred_element_type=F32)

    def body_once(a_ref, b_ref, o_ref):
        o_ref[...] = dot(a_ref, b_ref).astype(o_ref.dtype)

    def body_steps(a_ref, b_ref, o_ref, acc):
        kk = pl.program_id(2)

        @pl.when(kk == 0)
        def _():
            acc[...] = jnp.zeros_like(acc)

        acc[...] += dot(a_ref, b_ref)

        @pl.when(kk == nk - 1)
        def _():
            o_ref[...] = acc[...].astype(o_ref.dtype)

    return _pcall(
        body_once if nk == 1 else body_steps, name=name, grid=(m // tm, n // tn, nk),
        in_specs=[a_spec, b_spec], out_specs=pl.BlockSpec((tm, tn), lambda i, j, kk: (i, j)),
        out_shape=jax.ShapeDtypeStruct((m, n), out_dtype),
        scratch_shapes=[] if nk == 1 else [pltpu.VMEM((tm, tn), F32)],
        compiler_params=_params(("parallel", "parallel", "arbitrary")),
    )(a, b)


def _rowcall(name, f, rows, params=(), row_outs=(), acc_outs=(), br=256, nrows=None, after=None):
    rows = [r if isinstance(r, tuple) else (r, 0) for r in rows]
    t = nrows if nrows is not None else rows[0][0].shape[0]
    br = min(br, t)
    assert t % br == 0, (name, t, br)
    n_in, n_ro = len(rows) + len(params), len(row_outs)
    extra = _as_list(after)

    def row_spec(arr, off):
        assert off % br == 0
        return pl.BlockSpec((br, arr.shape[1]), functools.partial(lambda i, o: (i + o, 0), o=off // br))

    def whole_spec(shape):
        return pl.BlockSpec(shape, functools.partial(lambda i, nd: (0,) * nd, nd=len(shape)))

    def body(*refs):
        res = f(*[r[...] for r in refs[:n_in]])
        outs = refs[n_in + len(extra):]
        for o_ref, val in zip(outs[:n_ro], res[:n_ro]):
            o_ref[...] = val.astype(o_ref.dtype)
        if acc_outs:
            @pl.when(pl.program_id(0) == 0)
            def _():
                for o_ref in outs[n_ro:]:
                    o_ref[...] = jnp.zeros_like(o_ref)

            for o_ref, val in zip(outs[n_ro:], res[n_ro:]):
                o_ref[...] += val.astype(F32)

    out = _pcall(
        body, name=name, grid=(t // br,),
        in_specs=[row_spec(a, o) for a, o in rows] + [whole_spec(p.shape) for p in params]
        + [pl.BlockSpec(memory_space=pl.ANY)] * len(extra),
        out_specs=[pl.BlockSpec((br, c), lambda i: (i, 0)) for c, _ in row_outs] + [whole_spec(s) for s in acc_outs],
        out_shape=[jax.ShapeDtypeStruct((t, c), dt) for c, dt in row_outs]
        + [jax.ShapeDtypeStruct(s, F32) for s in acc_outs],
        compiler_params=_params(("arbitrary",) if acc_outs else ("parallel",)),
    )(*[a for a, _ in rows], *params, *extra)
    return out


_DN_BATCH = {'nn': (((2,), (1,)), ((0,), (0,))), 'nt': (((2,), (2,)), ((0,), (0,))), 'tn': (((1,), (1,)), ((0,), (0,)))}


def _dims(a, dn):
    return _DN_BATCH[dn] if a.ndim == 3 else _DN[dn]


def _bdot_raw(a, b, dn):
    return lax.dot_general(a.astype(BF16), b.astype(BF16), _dims(a, dn), preferred_element_type=F32)


def _hdot_raw(a, b, dn):
    a_hi, b_hi = a.astype(BF16), b.astype(BF16)
    a_lo, b_lo = (a - a_hi.astype(F32)).astype(BF16), (b - b_hi.astype(F32)).astype(BF16)

    def dot(p, q):
        return lax.dot_general(p, q, _dims(a, dn), preferred_element_type=F32)

    return dot(a_hi, b_hi) + (dot(a_hi, b_lo) + dot(a_lo, b_hi))


def _with_vjp(raw):
    @functools.partial(jax.custom_vjp, nondiff_argnums=(2,))
    def dot(a, b, dn):
        return raw(a, b, dn)

    def fwd(a, b, dn):
        return raw(a, b, dn), (a, b)

    def bwd(dn, res, ct):
        a, b = res
        if dn == 'nn':
            da, db = raw(ct, b, 'nt'), raw(a, ct, 'tn')
        elif dn == 'nt':
            da, db = raw(ct, b, 'nn'), raw(ct, a, 'tn')
        else:
            da, db = raw(b, ct, 'nt'), raw(a, ct, 'nn')
        return da.astype(a.dtype), db.astype(b.dtype)

    dot.defvjp(fwd, bwd)
    return dot


_bdot, _hdot = _with_vjp(_bdot_raw), _with_vjp(_hdot_raw)


def _rms(x, g):
    return x * lax.rsqrt(jnp.mean(x * x, axis=-1, keepdims=True) + NORM_EPS) * g


def _sigmoid(x):
    return 1.0 / (1.0 + jnp.exp(-x))


def _silu(x):
    return x * _sigmoid(x)


def _gelu(x):
    return 0.5 * x * (1.0 + lax.erf(x * (2.0 ** -0.5)))


def _norm_fwd(x, g, name, after=None):
    return _rowcall(name, lambda x_, g_: (_rms(x_, g_),), [x], [g], [(x.shape[1], BF16)], after=after)[0]


def _resnorm_fwd(x, f, g, alpha, name):
    return _rowcall(name, lambda x_, f_, g_: (x_ + alpha * _rms(f_, g_),), [x, f], [g], [(x.shape[1], F32)])[0]


def _resnorm_bwd(f, dx, g, alpha, name, after=None):
    def fn(f_, dx_, g_):
        _, vjp = jax.vjp(lambda a, b: alpha * _rms(a, b), f_, g_)
        return vjp(dx_)

    return _rowcall(name, fn, [f, dx], [g], [(f.shape[1], BF16)], [g.shape], after=after)


def _norm_bwd(x, dh, dx_out, g, name, after=None):
    def fn(x_, dh_, dxo_, g_):
        _, vjp = jax.vjp(_rms, x_, g_)
        dx, dg = vjp(dh_)
        return dxo_ + dx, dg

    return _rowcall(name, fn, [x, dh, dx_out], [g], [(x.shape[1], F32)], [g.shape], after=after)


def _norm_bwd_gain(x, dh, g, name):
    def fn(x_, dh_, g_):
        _, vjp = jax.vjp(lambda b: _rms(x_, b), g_)
        return vjp(dh_)

    return _rowcall(name, fn, [x, dh], [g], [], [g.shape])[0]


def _ffn_up(h, w_gu, name):
    t, d = h.shape
    ff = w_gu.shape[1] // 2
    tm, tn = _divtile(t, 1024), _divtile(ff, 512, MXU_WIDTH)
    nj = ff // tn

    def body(h_ref, wg_ref, wu_ref, gu_ref, a_ref):
        hv = h_ref[...]
        gate = jnp.dot(hv, wg_ref[...], preferred_element_type=F32)
        up = jnp.dot(hv, wu_ref[...], preferred_element_type=F32)
        gu_ref[0], gu_ref[1] = gate, up
        a_ref[...] = (_silu(gate) * up).astype(a_ref.dtype)

    return _pcall(body, name=name, grid=(t // tm, nj),
                  in_specs=[pl.BlockSpec((tm, d), lambda i, j: (i, 0)), pl.BlockSpec((d, tn), lambda i, j: (0, j)),
                            pl.BlockSpec((d, tn), lambda i, j: (0, j + nj))],
                  out_specs=[pl.BlockSpec((2, tm, tn), lambda i, j: (0, i, j)), pl.BlockSpec((tm, tn), lambda i, j: (i, j))],
                  out_shape=[jax.ShapeDtypeStruct((2, t, ff), F32), jax.ShapeDtypeStruct((t, ff), BF16)],
                  compiler_params=_params(("parallel", "parallel")))(h, w_gu, w_gu)


def _ffn_down_bwd(df, w_d, gu, name):
    t, d = df.shape
    ff = w_d.shape[0]
    tm, tn = _divtile(t, 1024), _divtile(ff, 512, MXU_WIDTH)

    def body(df_ref, w_ref, gu_ref, o_ref):
        da = lax.dot_general(df_ref[...], w_ref[...], _DN['nt'], preferred_element_type=F32)
        gate, up = gu_ref[0], gu_ref[1]
        s = _sigmoid(gate)
        o_ref[0] = (da * up * (s * (1.0 + gate * (1.0 - s)))).astype(o_ref.dtype)
        o_ref[1] = (da * (gate * s)).astype(o_ref.dtype)

    blk = pl.BlockSpec((2, tm, tn), lambda i, j: (0, i, j))
    return _pcall(body, name=name, grid=(t // tm, ff // tn),
                  in_specs=[pl.BlockSpec((tm, d), lambda i, j: (i, 0)), pl.BlockSpec((tn, d), lambda i, j: (j, 0)), blk],
                  out_specs=blk, out_shape=jax.ShapeDtypeStruct((2, t, ff), BF16),
                  compiler_params=_params(("parallel", "parallel")))(df, w_d, gu)


def _sgu_norm(v, lg, lb):
    vf = _gelu(v)
    mu = jnp.mean(vf, axis=-1, keepdims=True)
    var = jnp.mean(jnp.square(vf - mu), axis=-1, keepdims=True)
    return (vf - mu) * lax.rsqrt(var + NORM_EPS) * lg + lb


def _sgu_head(dot, u_h, vn_h, w_h, b_h):
    r = lax.broadcasted_iota(jnp.int32, w_h.shape, 0)
    c = lax.broadcasted_iota(jnp.int32, w_h.shape, 1)
    mixed = dot(jnp.where(r >= c, w_h, 0.0), vn_h, 'nn') + b_h
    return _gelu(u_h) * mixed


def _heads(width):
    return [slice(h * HEAD_DIM, (h + 1) * HEAD_DIM) for h in range(width // HEAD_DIM)]


def _sgu_fwd(u, v, w, bcol, lg, lb, name):
    def fn(u_, v_, w_, b_, lg_, lb_):
        vn = _sgu_norm(v_, lg_, lb_)
        return (jnp.concatenate([_sgu_head(_bdot_raw, u_[:, s], vn[:, s], w_[h], b_[h])
                                 for h, s in enumerate(_heads(u_.shape[1]))], axis=1),)

    return _rowcall(name, fn, [u, v], [w, bcol, lg, lb], [(u.shape[1], BF16)], br=GM_CHUNK)[0]


def _sgu_bwd(u, v, w, bcol, lg, lb, dy, name):
    def fn(u_, v_, dy_, w_, b_, lg_, lb_):
        vn, vjp_norm = jax.vjp(_sgu_norm, v_, lg_, lb_)
        du, dvn, dw, db = [], [], [], []
        for h, s in enumerate(_heads(u_.shape[1])):
            _, vjp = jax.vjp(functools.partial(_sgu_head, _bdot), u_[:, s], vn[:, s], w_[h], b_[h])
            a, b, c, d = vjp(dy_[:, s])
            du.append(a), dvn.append(b), dw.append(c[None]), db.append(d[None])
        dv, dlg, dlb = vjp_norm(jnp.concatenate(dvn, axis=1))
        return (jnp.concatenate(du, axis=1), dv, jnp.concatenate(dw, axis=0), jnp.concatenate(db, axis=0), dlg, dlb)

    wd = u.shape[1]
    return _rowcall(name, fn, [u, v, dy], [w, bcol, lg, lb], [(wd, BF16), (wd, BF16)],
                    [w.shape, bcol.shape, lg.shape, lb.shape], br=GM_CHUNK)


def _shift_down(x, s):
    if s == 0:
        return x
    rows = lax.broadcasted_iota(jnp.int32, x.shape, 0)
    return jnp.where(rows >= s, pltpu.roll(x, s, 0), 0.0)


def _shift_up(x, s):
    if s == 0:
        return x
    t = x.shape[0]
    rows = lax.broadcasted_iota(jnp.int32, x.shape, 0)
    return jnp.where(rows < t - s, pltpu.roll(x, t - s, 0), 0.0)


def _conv_pre(x, taps):
    acc = None
    for i in range(CONV_WIDTH):
        term = _shift_down(x, CONV_WIDTH - 1 - i) * taps[i]
        acc = term if acc is None else acc + term
    return acc


def _taps(w_ref):
    return [w_ref[i:i + 1, :] for i in range(CONV_WIDTH)]


def _conv_fwd(x, w, name):
    t, ch = x.shape
    cb = _divtile(ch, 512)

    def body(x_ref, w_ref, o_ref):
        o_ref[...] = _silu(_conv_pre(x_ref[...], _taps(w_ref)))

    return _pcall(body, name=name, grid=(ch // cb,),
                  in_specs=[pl.BlockSpec((t, cb), lambda j: (0, j)), pl.BlockSpec((CONV_WIDTH, cb), lambda j: (0, j))],
                  out_specs=pl.BlockSpec((t, cb), lambda j: (0, j)),
                  out_shape=jax.ShapeDtypeStruct((t, ch), F32), compiler_params=_params(("parallel",)))(x, w)


def _conv_bwd(x, w, dout, col_off, name):
    t, ch = dout.shape
    cb = _divtile(ch, 256)
    assert col_off % cb == 0
    off = col_off // cb

    def body(x_ref, w_ref, do_ref, dx_ref, dw_ref):
        xv, taps = x_ref[...], _taps(w_ref)
        pre = _conv_pre(xv, taps)
        s = _sigmoid(pre)
        dpre = do_ref[...] * (s * (1.0 + pre * (1.0 - s)))
        dx = None
        for i in range(CONV_WIDTH):
            sh = CONV_WIDTH - 1 - i
            term = _shift_up(dpre, sh) * taps[i]
            dx = term if dx is None else dx + term
            dw_ref[i:i + 1, :] = jnp.sum(dpre * _shift_down(xv, sh), axis=0, keepdims=True)
        dx_ref[...] = dx.astype(dx_ref.dtype)

    return _pcall(body, name=name, grid=(ch // cb,),
                  in_specs=[pl.BlockSpec((t, cb), lambda j: (0, j + off)),
                            pl.BlockSpec((CONV_WIDTH, cb), lambda j: (0, j + off)),
                            pl.BlockSpec((t, cb), lambda j: (0, j))],
                  out_specs=[pl.BlockSpec((t, cb), lambda j: (0, j)), pl.BlockSpec((CONV_WIDTH, cb), lambda j: (0, j))],
                  out_shape=[jax.ShapeDtypeStruct((t, ch), BF16), jax.ShapeDtypeStruct((CONV_WIDTH, ch), F32)],
                  compiler_params=_params(("parallel",)))(x, w, dout)


GATE_BLOCK = 256


def _gates(dot, b_raw, a_raw, a_log, dt_bias):
    blk = b_raw.shape[1]
    z = a_raw + dt_bias
    softplus = jnp.maximum(z, 0.0) + jnp.log(1.0 + jnp.exp(-jnp.abs(z)))
    g = -jnp.exp(a_log) * softplus
    ri = lax.broadcasted_iota(jnp.int32, (blk, blk), 0)
    ci = lax.broadcasted_iota(jnp.int32, (blk, blk), 1)
    upto = ((ri <= ci) & (ri // DN_CHUNK == ci // DN_CHUNK)).astype(F32)
    return _sigmoid(b_raw), dot(g, upto, 'nn')


def _gate_blocks(t):
    blk = min(GATE_BLOCK, t)
    return [slice(i, i + blk) for i in range(0, t, blk)]


def _whole_call(name, f, ins, out_shapes):
    n_in = len(ins)

    def body(*refs):
        for o_ref, val in zip(refs[n_in:], f(*[r[...] for r in refs[:n_in]])):
            o_ref[...] = val

    vmem = pl.BlockSpec(memory_space=pltpu.VMEM)
    return _pcall(body, name=name, in_specs=[vmem] * n_in, out_specs=[vmem] * len(out_shapes),
                  out_shape=[jax.ShapeDtypeStruct(s, F32) for s in out_shapes],
                  compiler_params=pltpu.CompilerParams(vmem_limit_bytes=VMEM_LIMIT))(*ins)


def _gates_fwd(b_raw, a_raw, a_log, dt_bias, name):
    def fn(b_, a_, al_, dt_):
        parts = [_gates(_hdot_raw, b_[:, s], a_[:, s], al_, dt_) for s in _gate_blocks(b_.shape[1])]
        return [jnp.concatenate(p, axis=1) for p in zip(*parts)]

    return _whole_call(name, fn, [b_raw, a_raw, a_log, dt_bias], [b_raw.shape, b_raw.shape])


def _gates_bwd(b_raw, a_raw, a_log, dt_bias, dbeta, dgcum, name):
    def fn(b_, a_, al_, dt_, dbeta_, dgcum_):
        db, da, dal, ddt = [], [], None, None
        for s in _gate_blocks(b_.shape[1]):
            _, vjp = jax.vjp(functools.partial(_gates, _hdot), b_[:, s], a_[:, s], al_, dt_)
            p, q, r, u = vjp((dbeta_[:, s], dgcum_[:, s]))
            db.append(p), da.append(q)
            dal, ddt = (r, u) if dal is None else (dal + r, ddt + u)
        return jnp.concatenate(db, axis=1), jnp.concatenate(da, axis=1), dal, ddt

    return _whole_call(name, fn, [b_raw, a_raw, a_log, dt_bias, dbeta, dgcum],
                       [b_raw.shape, a_raw.shape, a_log.shape, dt_bias.shape])


def _unit_lower_inverse_raw(a):
    c = a.shape[-1]
    eye = (lax.broadcasted_iota(jnp.int32, (c, c), 0) == lax.broadcasted_iota(jnp.int32, (c, c), 1)).astype(F32)
    inv, p = eye - a, _hdot_raw(a, a, 'nn')
    n_fac = int(math.log2(c)) - 1
    for it in range(n_fac):
        inv = inv + _hdot_raw(inv, p, 'nn')
        if it < n_fac - 1:
            p = _hdot_raw(p, p, 'nn')
    return inv


@jax.custom_vjp
def _unit_lower_inverse(a):
    return _unit_lower_inverse_raw(a)


def _unit_lower_inverse_fwd(a):
    inv = _unit_lower_inverse_raw(a)
    return inv, inv


def _unit_lower_inverse_bwd(inv, ct):
    return (-_hdot_raw(_hdot_raw(inv, ct, 'tn'), inv, 'nt'),)


_unit_lower_inverse.defvjp(_unit_lower_inverse_fwd, _unit_lower_inverse_bwd)


def _halves_raw(x):
    return x[..., :x.shape[-1] // 2], x[..., x.shape[-1] // 2:]


@jax.custom_vjp
def _halves(x):
    return _halves_raw(x)


_halves.defvjp(lambda x: (_halves_raw(x), None), lambda _, ct: (jnp.concatenate(ct, axis=-1),))

_DELTA_OPS = (_bdot_raw, _hdot_raw, _unit_lower_inverse_raw, _halves_raw)
_DELTA_OPS_VJP = (_bdot, _hdot, _unit_lower_inverse, _halves)


def _delta_chunk(ops, state, q, k, v, gc, gr, bc):
    bd, hd, inverse, halves = ops
    c, d = q.shape[-2:]
    ri = lax.broadcasted_iota(jnp.int32, (c, c), 0)
    ci = lax.broadcasted_iota(jnp.int32, (c, c), 1)
    causal, strict = ri >= ci, ri > ci
    qn = q * lax.rsqrt(jnp.sum(q * q, axis=-1, keepdims=True) + NORM_EPS) * (d ** -0.5)
    kn = k * lax.rsqrt(jnp.sum(k * k, axis=-1, keepdims=True) + NORM_EPS)
    gcum = jnp.broadcast_to(gc, q.shape)
    decay = jnp.where(causal, jnp.exp(jnp.where(causal, gc - gr, 0.0)), 0.0)
    bb = jnp.broadcast_to(bc, q.shape)
    k_beta = kn * bb
    inv = inverse(jnp.where(strict, bd(k_beta, kn, 'nt') * decay, 0.0))
    uw = hd(inv, jnp.concatenate([v * bb, k_beta * jnp.exp(gcum)], axis=-1), 'nn')
    u, w = halves(uw)
    qk = jnp.where(causal, bd(qn, kn, 'nt') * decay, 0.0)
    v_new = u - bd(w, state, 'nn')
    o = bd(qn * jnp.exp(gcum), state, 'nn') + bd(qk, v_new, 'nn')
    last = lax.broadcasted_iota(jnp.int32, (c, d), 0) == c - 1
    g_last = jnp.sum(jnp.where(last, gcum, 0.0), axis=-2, keepdims=True)
    k_dec = kn * jnp.exp(g_last - gcum)
    return state * jnp.exp(g_last) + bd(k_dec, v_new, 'tn'), o


DN_HEADS_PER_STEP = 8


def _by_head(ref):
    return jnp.stack([ref[:, s] for s in _heads(ref.shape[1])])


def _delta_specs(nh, nc, rev):
    c, d = DN_CHUNK, HEAD_DIM
    hb = min(DN_HEADS_PER_STEP, nh)
    ng = nh // hb
    ch = (lambda n: nc - 1 - n) if rev else (lambda n: n)
    qkv = [pl.BlockSpec((c, hb * d), functools.partial(lambda h, n, o: (ch(n), o * ng + h), o=o)) for o in range(3)]
    col = pl.BlockSpec((hb, c, 1), lambda h, n: (h, ch(n), 0))
    row = pl.BlockSpec((hb, None, 1, c), lambda h, n: (h, ch(n), 0, 0))
    st = pl.BlockSpec((hb, None, d, d), lambda h, n: (h, ch(n), 0, 0))
    tok = pl.BlockSpec((c, hb * d), lambda h, n: (ch(n), h))
    return hb, ng, qkv, col, row, st, tok


def _delta_fwd(qkv, gcol, grow, bcol, name):
    t = qkv.shape[0]
    nh, nc, d = gcol.shape[0], t // DN_CHUNK, HEAD_DIM
    hb, ng, qkv_s, col, row, st, tok = _delta_specs(nh, nc, False)

    def body(q_ref, k_ref, v_ref, gc_ref, gr_ref, bc_ref, o_ref, st_ref, state):
        @pl.when(pl.program_id(1) == 0)
        def _():
            state[...] = jnp.zeros_like(state)

        s0 = state[...]
        st_ref[...] = s0
        s1, o = _delta_chunk(_DELTA_OPS, s0, _by_head(q_ref), _by_head(k_ref), _by_head(v_ref), gc_ref[...],
                             gr_ref[...], bc_ref[...])
        for j, s in enumerate(_heads(hb * d)):
            o_ref[:, s] = o[j]
        state[...] = s1

    return _pcall(body, name=name, grid=(ng, nc), in_specs=qkv_s + [col, row, col], out_specs=[tok, st],
                  out_shape=[jax.ShapeDtypeStruct((t, nh * d), F32), jax.ShapeDtypeStruct((nh, nc, d, d), F32)],
                  scratch_shapes=[pltpu.VMEM((hb, d, d), F32)],
                  compiler_params=_params(("parallel", "arbitrary")))(qkv, qkv, qkv, gcol, grow, bcol)


def _delta_bwd(qkv, gcol, grow, bcol, states, do, name):
    t = qkv.shape[0]
    nh, nc, d = gcol.shape[0], t // DN_CHUNK, HEAD_DIM
    hb, ng, qkv_s, col, row, st, tok = _delta_specs(nh, nc, True)

    def body(q_ref, k_ref, v_ref, gc_ref, gr_ref, bc_ref, st_ref, do_ref,
             dq_ref, dk_ref, dv_ref, dgc_ref, dgr_ref, dbc_ref, dstate):
        @pl.when(pl.program_id(1) == 0)
        def _():
            dstate[...] = jnp.zeros_like(dstate)

        _, vjp = jax.vjp(functools.partial(_delta_chunk, _DELTA_OPS_VJP), st_ref[...], _by_head(q_ref), _by_head(k_ref),
                         _by_head(v_ref), gc_ref[...], gr_ref[...], bc_ref[...])
        ds, dq, dk, dv, dgc, dgr, dbc = vjp((dstate[...], _by_head(do_ref)))
        for j, s in enumerate(_heads(hb * d)):
            dq_ref[:, s], dk_ref[:, s], dv_ref[:, s] = dq[j], dk[j], dv[j]
        dgc_ref[...], dgr_ref[...], dbc_ref[...] = dgc, dgr, dbc
        dstate[...] = ds

    tok_out = jax.ShapeDtypeStruct((t, nh * d), F32)
    return _pcall(body, name=name, grid=(ng, nc), in_specs=qkv_s + [col, row, col, st, tok],
                  out_specs=[tok, tok, tok, col, row, col],
                  out_shape=[tok_out, tok_out, tok_out, jax.ShapeDtypeStruct(gcol.shape, F32),
                             jax.ShapeDtypeStruct(grow.shape, F32), jax.ShapeDtypeStruct(bcol.shape, F32)],
                  scratch_shapes=[pltpu.VMEM((hb, d, d), F32)],
                  compiler_params=_params(("parallel", "arbitrary")))(qkv, qkv, qkv, gcol, grow, bcol, states, do)


def _outgate_head(o_h, z_h, w):
    return _rms(o_h, w) * _silu(z_h)


def _outgate_fwd(o, z, w, name):
    def fn(o_, z_, w_):
        return (jnp.concatenate([_outgate_head(o_[:, s], z_[:, s], w_) for s in _heads(o_.shape[1])], axis=1),)

    return _rowcall(name, fn, [o, z], [w], [(o.shape[1], BF16)])[0]


def _outgate_bwd(o, z, w, dy, name):
    def fn(o_, z_, dy_, w_):
        do, dz, dw = [], [], None
        for s in _heads(o_.shape[1]):
            _, vjp = jax.vjp(_outgate_head, o_[:, s], z_[:, s], w_)
            a, b, c = vjp(dy_[:, s])
            do.append(a), dz.append(b)
            dw = c if dw is None else dw + c
        return jnp.concatenate(do, axis=1), jnp.concatenate(dz, axis=1), dw

    wd = o.shape[1]
    return _rowcall(name, fn, [o, z, dy], [w], [(wd, F32), (wd, BF16)], [w.shape])


def _attn_head(dot, q_h, k_h, v_h):
    s = dot(q_h, k_h, 'nt') * (q_h.shape[1] ** -0.5)
    e = jnp.exp(s - lax.stop_gradient(jnp.max(s, axis=-1, keepdims=True)))
    p = e / jnp.sum(e, axis=-1, keepdims=True)
    return dot(p, v_h, 'nn')


def _xa_slices(width):
    hd = width // XA_HEADS
    return [slice(h * hd, (h + 1) * hd) for h in range(XA_HEADS)]


def _attn_fwd(q, kv, name):
    wd = q.shape[1]

    def fn(q_, kv_):
        k_, v_ = kv_[:, :wd], kv_[:, wd:]
        return (jnp.concatenate([_attn_head(_bdot_raw, q_[:, s], k_[:, s], v_[:, s]) for s in _xa_slices(wd)],
                                axis=1),)

    return _rowcall(name, fn, [q], [kv], [(wd, BF16)])[0]


def _attn_bwd(q, kv, do, name):
    wd = q.shape[1]

    def fn(q_, do_, kv_):
        k_, v_ = kv_[:, :wd].astype(F32), kv_[:, wd:].astype(F32)
        dq, dk, dv = [], [], []
        for s in _xa_slices(wd):
            _, vjp = jax.vjp(functools.partial(_attn_head, _bdot), q_[:, s].astype(F32), k_[:, s], v_[:, s])
            a, b, c = vjp(do_[:, s])
            dq.append(a), dk.append(b), dv.append(c)
        return jnp.concatenate(dq, axis=1), jnp.concatenate(dk + dv, axis=1)

    return _rowcall(name, fn, [q, do], [kv], [(wd, BF16)], [kv.shape])


def _loss_call(y, target, name):
    d = y.shape[1]

    def fn(y_, t_):
        err = y_ - t_
        part = 0.5 * jnp.sum(jnp.sum(err * err, axis=1, keepdims=True), axis=0, keepdims=True) / d
        return err / d, jnp.broadcast_to(part, (1, LANES))

    return _rowcall(name, fn, [y, target], [], [(d, F32)], [(1, LANES)])


def _adamw_call(w, g, m, v, name):
    c = w.shape[1]

    def fn(w_, g_, m_, v_):
        g_ = g_[:, :c]
        m1 = ADAM_B1 * m_ + (1.0 - ADAM_B1) * g_
        v1 = ADAM_B2 * v_ + (1.0 - ADAM_B2) * jnp.square(g_)
        m_hat = m1 / (1.0 - ADAM_B1 ** ADAM_STEP)
        v_hat = v1 / (1.0 - ADAM_B2 ** ADAM_STEP)
        return g_, -ADAM_LR * (m_hat / (jnp.sqrt(v_hat) + ADAM_EPS) + ADAM_WD * w_), m1, v1

    return _rowcall(name, fn, [w, g, m, v], [], [(c, F32)] * 4, br=_divtile(w.shape[0], 128, 8))


STREAM_BLOCK_BYTES = 4 * 1024 * 1024


def _rows_for(cols, itemsize):
    return max(16, STREAM_BLOCK_BYTES // (cols * itemsize) // 16 * 16)


def _pair_add(grad, theirs, core, name):
    s, hr, cols = theirs.shape
    br = _divtile(hr, _rows_for(cols, 2), 16)
    nb = hr // br

    def body(c_ref, g_ref, t_ref, o_ref):
        o_ref[...] = (g_ref[...].astype(F32) + t_ref[...].astype(F32)).astype(o_ref.dtype)

    spec = pl.BlockSpec((None, br, cols), lambda j, i, c_ref: (j, i, 0))
    return _pcall(body, name=name, out_shape=jax.ShapeDtypeStruct(theirs.shape, BF16),
                  grid_spec=pltpu.PrefetchScalarGridSpec(
                      num_scalar_prefetch=1, grid=(s, nb),
                      in_specs=[pl.BlockSpec((None, br, cols), lambda j, i, c_ref: (j, c_ref[0] * nb + i, 0)), spec],
                      out_specs=spec),
                  compiler_params=_params(("parallel", "parallel")))(core, grad, theirs)


def _chip_sum(part, landed, kind, where, grad, layer, name):
    _, hr, cw = landed.shape
    br = _divtile(hr, _rows_for(cw, 4), 16)
    nb = hr // br

    def body(w_ref, p_ref, a_ref, b_ref, d_ref, g_ref, o_ref):
        o_ref[...] = ((p_ref[...].astype(F32) + a_ref[...].astype(F32)) + b_ref[...].astype(F32)) + d_ref[...].astype(F32)

    if kind == 'row':
        own = pl.BlockSpec((None, br, cw), lambda i, w_ref: (w_ref[0], i, 0))
    else:
        own = pl.BlockSpec((br, cw), lambda i, w_ref: (i, w_ref[0]))
    got = [pl.BlockSpec((None, br, cw), functools.partial(lambda i, w_ref, k: (k, i, 0), k=k)) for k in range(3)]
    return _pcall(body, name=name, out_shape=jax.ShapeDtypeStruct(grad.shape, F32),
                  grid_spec=pltpu.PrefetchScalarGridSpec(
                      num_scalar_prefetch=1, grid=(nb,), in_specs=[own] + got + [ANY_SPEC],
                      out_specs=pl.BlockSpec((None, br, cw), lambda i, w_ref: (layer, w_ref[1] * nb + i, 0))),
                  input_output_aliases={5: 0},
                  compiler_params=_params(("parallel",)))(where, part, landed, landed, landed, grad)


def _position():
    x, y, c = lax.axis_index("x"), lax.axis_index("y"), lax.axis_index("c")
    others = [(1 - x, y), (x, 1 - y), (1 - x, 1 - y)]
    return x, y, c, others


def _any_specs(n):
    return [pl.BlockSpec(memory_space=pl.ANY)] * n


def _ds(start, size):
    return pl.ds(pl.multiple_of(start, size), size)


HBM_SPEC = pl.BlockSpec(memory_space=pltpu.HBM)
SEM_SPEC = pl.BlockSpec(memory_space=pltpu.SEMAPHORE)
ANY_SPEC = pl.BlockSpec(memory_space=pl.ANY)
DATAFLOW = pltpu.SideEffectType.DATAFLOW_SIDE_EFFECTING


def _hbm(a):
    return pltpu.with_memory_space_constraint(a, pltpu.HBM)


def _full_shape(shard_shape, kind):
    r, cw = shard_shape
    return (N_CHIPS, r, cw) if kind == 'row' else (r, N_CHIPS * cw)


def _block_dims(full, kind):
    return (full.shape[1], full.shape[2]) if kind == 'row' else (full.shape[0], full.shape[1] // N_CHIPS)


def _region(full, kind, chip, half):
    if kind == 'all':
        return full.at[chip]
    r, cw = _block_dims(full, kind)
    if kind == 'row':
        return full.at[chip, _ds(half * (r // 2), r // 2), :]
    return full.at[_ds(half * (r // 2), r // 2), _ds(chip * cw, cw)]


def _place_block(full, shards, layer, kind, chip, name, after=None):
    _, r, cs = shards.shape
    cw = _block_dims(full, kind)[1]
    br = _divtile(r, 256, 16)

    def body(c_ref, s_ref, *rest):
        o_ref = rest[-1]
        if cs == cw:
            o_ref[...] = s_ref[...].astype(o_ref.dtype)
        else:
            o_ref[:, :cs] = s_ref[...].astype(o_ref.dtype)
            o_ref[:, cs:] = jnp.zeros((br, cw - cs), o_ref.dtype)

    if kind == 'row':
        out_spec = pl.BlockSpec((None, br, cw), lambda i, c_ref: (c_ref[0], i, 0))
    else:
        out_spec = pl.BlockSpec((br, cw), lambda i, c_ref: (i, c_ref[0]))
    extra = _as_list(after)
    return _pcall(body, name=name, out_shape=jax.ShapeDtypeStruct(full.shape, full.dtype),
                  grid_spec=pltpu.PrefetchScalarGridSpec(
                      num_scalar_prefetch=1, grid=(r // br,),
                      in_specs=[pl.BlockSpec((None, br, cs), lambda i, c_ref: (layer, i, 0)), ANY_SPEC]
                      + [ANY_SPEC] * len(extra), out_specs=out_spec),
                  input_output_aliases={2: 0}, compiler_params=_params(("parallel",)))(chip, shards, full, *extra)


def _split_copy_start(bufs, lands, plan, after, name):
    nb, nl = len(bufs), len(lands)
    extra = _as_list(after)

    def body(*refs):
        ins = refs[:nb + nl]
        send, recv = refs[nb + nl + len(extra)], refs[nb + nl + len(extra) + 1]
        for cp in plan(ins[:nb], ins[nb:], send, recv):
            cp.start()
        refs[-1][...] = jnp.zeros_like(refs[-1])

    n_copies = plan.n_copies
    out = _pcall(
        body, name=name,
        out_shape=(pltpu.SemaphoreType.DMA((n_copies,)), pltpu.SemaphoreType.DMA((n_copies,)),
                   *[pltpu.HBM(a.shape, a.dtype) for a in (*bufs, *lands)], jax.ShapeDtypeStruct((8, LANES), F32)),
        in_specs=[HBM_SPEC] * (nb + nl) + [ANY_SPEC] * len(extra),
        out_specs=(SEM_SPEC, SEM_SPEC, *[HBM_SPEC] * (nb + nl), pl.BlockSpec(memory_space=pltpu.VMEM)),
        input_output_aliases={i: 2 + i for i in range(nb + nl)},
        compiler_params=pltpu.CompilerParams(has_side_effects=DATAFLOW),
    )(*[_hbm(a) for a in (*bufs, *lands)], *extra)
    return out[0], out[1], list(out[2:2 + nb]), list(out[2 + nb:2 + nb + nl]), out[-1]


def _split_copy_wait(send, recv, bufs, lands, plan, after, name):
    nb, nl = len(bufs), len(lands)
    extra = _as_list(after)

    def body(*refs):
        ins = refs[:nb + nl]
        send_ref, recv_ref = refs[nb + nl], refs[nb + nl + 1]
        for cp in plan(ins[:nb], ins[nb:], send_ref, recv_ref):
            cp.wait_send()
            cp.wait_recv()

    out = _pcall(
        body, name=name, out_shape=tuple(pltpu.HBM(a.shape, a.dtype) for a in (*bufs, *lands)),
        in_specs=[HBM_SPEC] * (nb + nl) + [SEM_SPEC, SEM_SPEC] + [ANY_SPEC] * len(extra),
        out_specs=tuple([HBM_SPEC] * (nb + nl)), input_output_aliases={i: i for i in range(nb + nl)},
        compiler_params=pltpu.CompilerParams(has_side_effects=DATAFLOW),
    )(*bufs, *lands, send, recv, *extra)
    return list(out[:nb]), list(out[nb:])


def _gather_plan(kinds):
    def plan(bufs, lands, send, recv):
        x, y, c, others = _position()
        me = 2 * x + y
        return [pltpu.make_async_remote_copy(
            src_ref=_region(lands[w], kinds[w], me, c), dst_ref=_region(lands[w], kinds[w], me, c),
            send_sem=send.at[3 * w + k], recv_sem=recv.at[3 * w + k], device_id=(px, py, c), device_id_type=MESH_IDS)
            for w in range(len(lands)) for k, (px, py) in enumerate(others)]

    def wait_plan(bufs, lands, send, recv):
        x, y, c, others = _position()
        me = 2 * x + y
        return [pltpu.make_async_remote_copy(
            src_ref=_region(lands[w], kinds[w], me, c), dst_ref=_region(lands[w], kinds[w], 2 * px + py, c),
            send_sem=send.at[3 * w + k], recv_sem=recv.at[3 * w + k], device_id=(px, py, c), device_id_type=MESH_IDS)
            for w in range(len(lands)) for k, (px, py) in enumerate(others)]

    plan.n_copies = wait_plan.n_copies = 3 * len(kinds)
    return plan, wait_plan


def _gather_forward(fulls, kinds, name):
    n = len(fulls)

    def body(*refs):
        dst = refs[n:2 * n]
        send, recv = refs[2 * n:]
        x, y, c, others = _position()
        sib = (x, y, 1 - c)

        def copy(w, k, chip, half):
            reg = _region(dst[w], kinds[w], chip, half)
            return pltpu.make_async_remote_copy(src_ref=reg, dst_ref=reg, send_sem=send.at[3 * w + k],
                                                recv_sem=recv.at[3 * w + k], device_id=sib, device_id_type=MESH_IDS)

        give = [copy(w, k, 2 * px + py, c) for w in range(n) for k, (px, py) in enumerate(others)]
        for cp in give:
            cp.start()
        for w in range(n):
            for k, (px, py) in enumerate(others):
                copy(w, k, 2 * px + py, 1 - c).wait_recv()
        for cp in give:
            cp.wait_send()

    return _pcall(body, name=name, in_specs=_any_specs(n), out_specs=_any_specs(n),
                  out_shape=[jax.ShapeDtypeStruct(f.shape, f.dtype) for f in fulls],
                  input_output_aliases={i: i for i in range(n)},
                  scratch_shapes=[pltpu.SemaphoreType.DMA((3 * n,)), pltpu.SemaphoreType.DMA((3 * n,))])(*fulls)


def _grad_half(ref, kind, half):
    if kind == 'row':
        hr = ref.shape[1] // 2
        return ref.at[:, _ds(half * hr, hr), :]
    hr = ref.shape[0] // 2
    return ref.at[_ds(half * hr, hr), :]


def _half_shape(g, kind):
    return (g.shape[0], g.shape[1] // 2, g.shape[2]) if kind == 'row' else (g.shape[0] // 2, g.shape[1])


def _pair_plan(kinds):
    def plan(bufs, lands, send, recv):
        x, y, c, _ = _position()
        return [pltpu.make_async_remote_copy(src_ref=_grad_half(bufs[w], kinds[w], 1 - c), dst_ref=lands[w],
                                             send_sem=send.at[w], recv_sem=recv.at[w], device_id=(x, y, 1 - c),
                                             device_id_type=MESH_IDS) for w in range(len(bufs))]

    plan.n_copies = len(kinds)
    return plan


def _pair_exchange(grads, kinds, name):
    n = len(grads)

    def body(*refs):
        src, theirs_ref = refs[:n], refs[n:2 * n]
        send, recv = refs[2 * n:]
        x, y, c, _ = _position()
        sib = (x, y, 1 - c)

        def half(w, h):
            if kinds[w] == 'row':
                hr = src[w].shape[1] // 2
                return src[w].at[:, _ds(h * hr, hr), :]
            hr = src[w].shape[0] // 2
            return src[w].at[_ds(h * hr, hr), :]

        give = [pltpu.make_async_remote_copy(src_ref=half(w, 1 - c), dst_ref=theirs_ref[w], send_sem=send.at[w],
                                             recv_sem=recv.at[w], device_id=sib, device_id_type=MESH_IDS)
                for w in range(n)]
        for cp in give:
            cp.start()
        for cp in give:
            cp.wait()

    def half_shape(g, kind):
        return (g.shape[0], g.shape[1] // 2, g.shape[2]) if kind == 'row' else (g.shape[0] // 2, g.shape[1])

    return _pcall(body, name=name, in_specs=_any_specs(n), out_specs=_any_specs(n),
                  out_shape=[jax.ShapeDtypeStruct(half_shape(g, kd), g.dtype) for g, kd in zip(grads, kinds)],
                  scratch_shapes=[pltpu.SemaphoreType.DMA((n,)), pltpu.SemaphoreType.DMA((n,))])(*grads)


def _part_dims(part, kind):
    return (part.shape[1], part.shape[2]) if kind == 'row' else (part.shape[0], part.shape[1] // N_CHIPS)


def _chip_plan(kinds):
    def plan(bufs, lands, send, recv):
        x, y, c, others = _position()

        def slot(w, chip):
            if kinds[w] == 'row':
                return bufs[w].at[chip]
            cw = _part_dims(bufs[w], kinds[w])[1]
            return bufs[w].at[:, _ds(chip * cw, cw)]

        return [pltpu.make_async_remote_copy(src_ref=slot(w, 2 * px + py), dst_ref=lands[w].at[k],
                                             send_sem=send.at[3 * w + k], recv_sem=recv.at[3 * w + k],
                                             device_id=(px, py, c), device_id_type=MESH_IDS)
                for w in range(len(bufs)) for k, (px, py) in enumerate(others)]

    plan.n_copies = 3 * len(kinds)
    return plan


def _pair_gather_plan(n, n_layers):
    def copies(lands, send, recv, take):
        x, y, c, _ = _position()
        out = []
        for w in range(n):
            hr = lands[w].shape[1] // 2
            for l in range(n_layers):
                mine = lands[w].at[l, _ds(c * hr, hr), :]
                theirs = lands[w].at[l, _ds((1 - c) * hr, hr), :]
                out.append(pltpu.make_async_remote_copy(
                    src_ref=mine, dst_ref=theirs if take else mine, send_sem=send.at[w * n_layers + l],
                    recv_sem=recv.at[w * n_layers + l], device_id=(x, y, 1 - c), device_id_type=MESH_IDS))
        return out

    def plan(bufs, lands, send, recv):
        return copies(lands, send, recv, False)

    def wait_plan(bufs, lands, send, recv):
        return copies(lands, send, recv, True)

    plan.n_copies = wait_plan.n_copies = n * n_layers
    return plan, wait_plan


def _pair_gather(grads, name, after=None):
    n = len(grads)
    n_layers = grads[0].shape[0]
    extra = _as_list(after)

    def body(*refs):
        dst = refs[n + len(extra):2 * n + len(extra)]
        send, recv = refs[2 * n + len(extra):]
        x, y, c, _ = _position()
        sib = (x, y, 1 - c)

        def copy(w, l, half):
            hr = dst[w].shape[1] // 2
            rows = dst[w].at[l, _ds(half * hr, hr), :]
            return pltpu.make_async_remote_copy(src_ref=rows, dst_ref=rows, send_sem=send.at[w * n_layers + l],
                                                recv_sem=recv.at[w * n_layers + l], device_id=sib,
                                                device_id_type=MESH_IDS)

        give = [copy(w, l, c) for w in range(n) for l in range(n_layers)]
        for cp in give:
            cp.start()
        for w in range(n):
            for l in range(n_layers):
                copy(w, l, 1 - c).wait_recv()
        for cp in give:
            cp.wait_send()

    m = n * n_layers
    return _pcall(body, name=name, in_specs=_any_specs(n + len(extra)), out_specs=_any_specs(n),
                  out_shape=[jax.ShapeDtypeStruct(g.shape, g.dtype) for g in grads],
                  input_output_aliases={i: i for i in range(n)},
                  scratch_shapes=[pltpu.SemaphoreType.DMA((m,)), pltpu.SemaphoreType.DMA((m,))])(*grads, *extra)


def _all_gather_plan():
    flips = [(fx, fy, fc) for fx in (0, 1) for fy in (0, 1) for fc in (0, 1)][1:]

    def copies(bufs, lands, send, recv, take):
        x, y, c, _ = _position()
        out = []
        for k, f in enumerate(flips):
            px, py, pc = (1 - x if f[0] else x, 1 - y if f[1] else y, 1 - c if f[2] else c)
            slot = 4 * px + 2 * py + pc if take else 4 * x + 2 * y + c
            out.append(pltpu.make_async_remote_copy(src_ref=bufs[0], dst_ref=lands[0].at[slot], send_sem=send.at[k],
                                                    recv_sem=recv.at[k], device_id=(px, py, pc),
                                                    device_id_type=MESH_IDS))
        return out

    def plan(bufs, lands, send, recv):
        return copies(bufs, lands, send, recv, False)

    def wait_plan(bufs, lands, send, recv):
        return copies(bufs, lands, send, recv, True)

    plan.n_copies = wait_plan.n_copies = N_DEV - 1
    return plan, wait_plan


def _sum_slots(land, name):
    n, rows, cols = land.shape

    def fn(*slots):
        total = slots[0]
        for s in slots[1:]:
            total = total + s
        return (total,)

    flat = land.reshape(n * rows, cols)
    return _rowcall(name, fn, [(flat, d * rows) for d in range(n)], [], [(cols, F32)], br=_divtile(rows, 256, 8),
                    nrows=rows)[0]


def _pack(arrays):
    flat = jnp.concatenate([a.reshape(-1).astype(F32) for a in arrays])
    pad = (-flat.shape[0]) % (8 * LANES)
    return jnp.pad(flat, (0, pad)).reshape(-1, LANES)


def _unpack(buf, like):
    flat, out, off = buf.reshape(-1), [], 0
    for a in like:
        out.append(flat[off:off + a.size].reshape(a.shape))
        off += a.size
    return out


def _row2(v):
    return v.reshape(1, -1)


def _ffn_fwd(x, g_pre, w_gu, w_d, g_post, tag, after=None):
    h = _norm_fwd(x, g_pre, tag + "_pre", after)
    gu, a = _ffn_up(h, w_gu, tag + "_gu")
    f = _mm(a, w_d, 'nn', F32, tag + "_down")
    return _resnorm_fwd(x, f, g_post, 0.5, tag + "_post"), (x, h, gu, a, f)


def _ffn_bwd_weights(dx_out, saved, w_d, g_post, tag, after=None):
    x, h, gu, a, f = saved
    df, dg_post = _resnorm_bwd(f, dx_out, g_post, 0.5, tag + "_post_b", after)
    dw_d = _mm(a, df, 'tn', BF16, tag + "_down_bw")
    dgu = _ffn_down_bwd(df, w_d, gu, tag + "_down_bx")
    dw_gu = _mm(h, dgu, 'tn', BF16, tag + "_gu_bw", halves='b')
    return dgu, dw_gu, dw_d, dg_post


def _ffn_bwd_input(dx_out, saved, dgu, g_pre, w_gu, tag, after=None):
    x = saved[0]
    dh = _mm(dgu, w_gu, 'nt', F32, tag + "_gu_bx", halves='a')
    return _norm_bwd(x, dh, dx_out, g_pre, tag + "_pre_b", after)


def _ffn_bwd(dx_out, saved, g_pre, w_gu, w_d, g_post, tag, after=None):
    dgu, dw_gu, dw_d, dg_post = _ffn_bwd_weights(dx_out, saved, w_d, g_post, tag, after)
    dx, dg_pre = _ffn_bwd_input(dx_out, saved, dgu, g_pre, w_gu, tag)
    return dx, dg_pre, dw_gu, dw_d, dg_post


def _split_cols(pp, cs, cp, widths):
    proj = jnp.concatenate([pp[:, j * cp:j * cp + cs] for j in range(N_CHIPS)], axis=1)
    out, off = [], 0
    for wd in widths:
        out.append(proj[:, off:off + wd])
        off += wd
    return out


def _join_cols(pieces, cs, cp):
    proj = jnp.concatenate([pc.astype(BF16) for pc in pieces], axis=1)
    t = proj.shape[0]
    cols = []
    for j in range(N_CHIPS):
        cols += [proj[:, j * cs:(j + 1) * cs], jnp.zeros((t, cp - cs), proj.dtype)]
    return jnp.concatenate(cols, axis=1)


def _mix_fwd(x, p, w_in, w_out, conv_w, cs, cp, tag, after=None):
    t, d = x.shape
    half = d // 2
    nh = half // HEAD_DIM
    h = _norm_fwd(x, p['mix_norm_pre'], tag + "_pre", after)
    pp = _mm(h, w_in, 'nn', F32, tag + "_in")
    u_a, v_a, qkv_raw, z, b_raw, a_raw = _split_cols(pp, cs, cp, [half, half, 3 * half, half, nh, nh])
    y_a = _sgu_fwd(u_a, v_a, p['sm_w'], p['sm_b'], p['sm_ln_g'], p['sm_ln_b'], tag + "_sgu")
    qkv = _conv_fwd(qkv_raw, conv_w, tag + "_conv")
    b_raw, a_raw = b_raw.T, a_raw.T
    beta, gcum = _gates_fwd(b_raw, a_raw, p['a_log'].T, p['dt_bias'].T, tag + "_gates")
    gcol, bcol = gcum[:, :, None], beta[:, :, None]
    grow = gcum.reshape(nh, t // DN_CHUNK, 1, DN_CHUNK)
    o, states = _delta_fwd(qkv, gcol, grow, bcol, tag + "_delta")
    y_b = _outgate_fwd(o, z, p['dn_norm_w'], tag + "_gate")
    y = jnp.concatenate([y_a, y_b], axis=1)
    m = _mm(y, w_out, 'nn', F32, tag + "_out")
    x_out = _resnorm_fwd(x, m, p['mix_norm_post'], 1.0, tag + "_post")
    return x_out, (x, h, u_a, v_a, qkv_raw, z, b_raw, a_raw, qkv, gcol, grow, bcol, states, o, y, m)


def _mix_bwd(dx_out, saved, p, w_in, w_out, conv_w, cs, cp, tag, after=None):
    x, h, u_a, v_a, qkv_raw, z, b_raw, a_raw, qkv, gcol, grow, bcol, states, o, y, m = saved
    t, d = x.shape
    half = d // 2
    nh = half // HEAD_DIM
    gr = {}
    dm, gr['mix_norm_post'] = _resnorm_bwd(m, dx_out, p['mix_norm_post'], 1.0, tag + "_post_b", after)
    dy = _mm(dm, w_out, 'nt', F32, tag + "_out_bx")
    gr['w_out'] = _mm(y, dm, 'tn', BF16, tag + "_out_bw")
    dy_a, dy_b = dy[:, :half], dy[:, half:]
    do, dz, gr['dn_norm_w'] = _outgate_bwd(o, z, p['dn_norm_w'], dy_b, tag + "_gate_b")
    dq, dk, dv, dgcol, dgrow, dbcol = _delta_bwd(qkv, gcol, grow, bcol, states, do, tag + "_delta_b")
    dgcum = dgcol[:, :, 0] + dgrow.reshape(nh, t)
    db_raw, da_raw, gr['a_log'], gr['dt_bias'] = _gates_bwd(b_raw, a_raw, p['a_log'].T, p['dt_bias'].T,
                                                             dbcol[:, :, 0], dgcum, tag + "_gates_b")
    db_raw, da_raw = db_raw.T, da_raw.T
    thirds = [_conv_bwd(qkv_raw, conv_w, dpart, i * half, tag + "_conv_b") for i, dpart in enumerate((dq, dk, dv))]
    gr['conv_w'] = jnp.concatenate([dw for _, dw in thirds], axis=1)
    du_a, dv_a, gr['sm_w'], gr['sm_b'], gr['sm_ln_g'], gr['sm_ln_b'] = _sgu_bwd(
        u_a, v_a, p['sm_w'], p['sm_b'], p['sm_ln_g'], p['sm_ln_b'], dy_a, tag + "_sgu_b")
    dpp = _join_cols([du_a, dv_a, *[dx for dx, _ in thirds], dz, db_raw, da_raw], cs, cp).astype(BF16)
    dh = _mm(dpp, w_in, 'nt', F32, tag + "_in_bx")
    gr['w_in'] = _mm(h, dpp, 'tn', BF16, tag + "_in_bw")
    dx, gr['mix_norm_pre'] = _norm_bwd(x, dh, dx_out, p['mix_norm_pre'], tag + "_pre_b")
    return dx, gr


def _xa_fwd(x, mem, p, w_xq, w_xkv, w_xo, tag, after=None):
    h = _norm_fwd(x, p['xa_norm_pre'], tag + "_pre", after)
    mh = _norm_fwd(mem, p['mem_norm'], tag + "_mem")
    q = _mm(h, w_xq, 'nn', BF16, tag + "_q")
    kv = _mm(mh, w_xkv, 'nn', BF16, tag + "_kv")
    o = _attn_fwd(q, kv, tag + "_attn")
    c = _mm(o, w_xo, 'nn', F32, tag + "_o")
    return _resnorm_fwd(x, c, p['xa_norm_post'], 1.0, tag + "_post"), (x, h, mh, q, kv, o, c)


def _xa_bwd(dx_out, saved, mem, p, w_xq, w_xkv, w_xo, tag, after=None):
    x, h, mh, q, kv, o, c = saved
    gr = {}
    dc, gr['xa_norm_post'] = _resnorm_bwd(c, dx_out, p['xa_norm_post'], 1.0, tag + "_post_b", after)
    do = _mm(dc, w_xo, 'nt', F32, tag + "_o_bx")
    gr['w_xo'] = _mm(o, dc, 'tn', BF16, tag + "_o_bw")
    dq, dkv = _attn_bwd(q, kv, do, tag + "_attn_b")
    dkv = dkv.astype(BF16)
    dh = _mm(dq, w_xq, 'nt', F32, tag + "_q_bx")
    gr['w_xq'] = _mm(h, dq, 'tn', BF16, tag + "_q_bw")
    dmh = _mm(dkv, w_xkv, 'nt', F32, tag + "_kv_bx")
    gr['w_xkv'] = _mm(mh, dkv, 'tn', BF16, tag + "_kv_bw")
    gr['mem_norm'] = _norm_bwd_gain(mem, dmh, p['mem_norm'], tag + "_mem_b")
    dx, gr['xa_norm_pre'] = _norm_bwd(x, dh, dx_out, p['xa_norm_pre'], tag + "_pre_b")
    return dx, gr


def _step(inputs):
    x, mem, target = inputs['x'][0], inputs['mem'][0], inputs['loss_target'][0]
    n_layers = inputs['w_in'].shape[0]
    cx, cy, cc = lax.axis_index("x"), lax.axis_index("y"), lax.axis_index("c")
    chip = 2 * cx + cy
    cs = inputs['w_in'].shape[2]
    cp = -(-cs // LANES) * LANES
    kinds = [BIG[nm] for nm in BIG_NAMES]

    conv_local = inputs['conv_w']
    ccols = conv_local.shape[2]

    core_idx, chip_idx = cc.astype(jnp.int32).reshape(1), chip.astype(jnp.int32).reshape(1)
    where_idx = jnp.stack([chip, cc]).astype(jnp.int32)
    order = [(l, gi) for l in range(n_layers) for gi in range(len(GROUPS))]
    n_groups = len(order)

    def small(l):
        p = {nm: inputs[nm][l] for nm in SMALL_NAMES}
        for nm in SMALL_NAMES:
            if p[nm].ndim == 1:
                p[nm] = _row2(p[nm])
        p['sm_b'] = p['sm_b'][:, :, None]
        return p

    def place(k, after=None):
        l, gi = order[k]
        lands = []
        for nm in GROUPS[gi][1]:
            _, r, c_in = inputs[nm].shape
            shape = _full_shape((r, cp if nm == 'w_in' else c_in), BIG[nm])
            lands.append(_place_block(lax.empty(shape, BF16), w_src[nm], l, BIG[nm], chip_idx, "place_block", after))
        return lands

    w_src = {nm: inputs[nm] for nm in BIG_NAMES}

    placed = {}
    conv_group = 1

    def gather_start(k, after):
        names = list(GROUPS[order[k][1]][1])
        kds = [BIG[nm] for nm in names]
        if k not in placed:
            placed[k] = place(k)
        lands = list(placed[k])
        if k == conv_group:
            mine = conv_local.reshape(1, n_layers * CONV_WIDTH, ccols)
            lands.append(lax.dynamic_update_slice(jnp.zeros((N_CHIPS, *mine.shape[1:]), F32), mine, (chip, 0, 0)))
            names.append('conv_w'), kds.append('all')
        plan, wait_plan = _gather_plan(kds)
        send, recv, _, lands, token = _split_copy_start([], lands, plan, after, f"gather_start_{k}")
        return dict(send=send, recv=recv, lands=lands, token=token, kinds=kds, names=names, wait_plan=wait_plan)

    def gather_finish(k, st, after):
        _, lands = _split_copy_wait(st['send'], st['recv'], [], st['lands'], st['wait_plan'], after, f"gather_wait_{k}")
        got = dict(zip(st['names'], lands))
        halves = [nm for nm, kd in zip(st['names'], st['kinds']) if kd != 'all']
        passed = _gather_forward([got[nm] for nm in halves], [BIG[nm] for nm in halves],
                                 "gather_forward_" + GROUPS[order[k][1]][0])
        w = {nm: (g.reshape(-1, g.shape[2]) if BIG[nm] == 'row' else g) for nm, g in zip(halves, passed)}
        if 'conv_w' in got:
            w['conv_w'] = got['conv_w'].reshape(N_CHIPS, n_layers, CONV_WIDTH, ccols).transpose(1, 2, 0, 3).reshape(
                n_layers, CONV_WIDTH, N_CHIPS * ccols)
        return w

    def group_fwd(gi, x_, p, w, l, token):
        if gi == 0:
            return _ffn_fwd(x_, p['ffn1_norm_pre'], w['ffn1_w_gate_up'], w['ffn1_w_down'], p['ffn1_norm_post'], "ffn", token)
        if gi == 1:
            return _mix_fwd(x_, p, w['w_in'], w['w_out'], conv_full[l], cs, cp, "mix", token)
        if gi == 2:
            return _xa_fwd(x_, mem, p, w['w_xq'], w['w_xkv'], w['w_xo'], "xa", token)
        return _ffn_fwd(x_, p['ffn2_norm_pre'], w['ffn2_w_gate_up'], w['ffn2_w_down'], p['ffn2_norm_post'], "ffn", token)

    def group_bwd(gi, dx_, s, p, w, l, token):
        if gi in (0, 3):
            pre = 'ffn1' if gi == 0 else 'ffn2'
            dx_, g_pre, g_gu, g_d, g_post = _ffn_bwd(dx_, s, p[pre + '_norm_pre'], w[pre + '_w_gate_up'],
                                                     w[pre + '_w_down'], p[pre + '_norm_post'], "ffn", token)
            return dx_, {pre + '_norm_pre': g_pre, pre + '_w_gate_up': g_gu, pre + '_w_down': g_d,
                         pre + '_norm_post': g_post}
        if gi == 1:
            return _mix_bwd(dx_, s, p, w['w_in'], w['w_out'], conv_full[l], cs, cp, "mix", token)
        return _xa_bwd(dx_, s, mem, p, w['w_xq'], w['w_xkv'], w['w_xo'], "xa", token)

    started = {0: gather_start(0, None)}
    w_src['w_in'] = (inputs['w_in'] + started[0]['token'][0, 0]).astype(BF16)
    started[1] = gather_start(1, started[0]['token'])
    weights, saved = [], []
    for k in range(2, n_groups):
        placed[k] = place(k, started[1]['token'])
    head_work = [started[1]['token']] + [land for k in range(2, n_groups) for land in placed[k]]
    head_work += [inputs[pre + 'w_in'].reshape(-1, cs) for pre in ('', 'm_', 'v_')]
    conv_full = None
    for k, (l, gi) in enumerate(order):
        w = gather_finish(k, started[k], x if k > 0 else head_work)
        if k == conv_group:
            conv_full = w.pop('conv_w')
        token = None
        if k + 2 < n_groups:
            started[k + 2] = gather_start(k + 2, next(iter(w.values())))
            token = started[k + 2]['token']
        weights.append(w)
        x, s = group_fwd(gi, x, small(l), w, l, token)
        saved.append(s)

    dx, loss_part = _loss_call(x, target, "loss")

    grads = [dict() for _ in range(n_layers)]
    big_grads = {nm: None for nm in BIG_NAMES}
    pair_gathers = {}

    def reduce_finish(pd, after):
        parts, lands = _split_copy_wait(pd['send'], pd['recv'], pd['parts'], pd['lands'], pd['plan'], after,
                                        f"chip_wait_{pd['k']}")
        for nm, kd, part, landed in zip(pd['names'], pd['kinds'], parts, lands):
            if big_grads[nm] is None:
                big_grads[nm] = lax.empty((n_layers, 2 * landed.shape[1], landed.shape[2]), F32)
            big_grads[nm] = _chip_sum(part, landed, kd, where_idx, big_grads[nm], pd['l'], "chip_sum")
        if pd['l'] == 0 and pd['k'] > 0:
            names = pd['names']
            plan, wait_plan = _pair_gather_plan(len(names), n_layers)
            send, recv, _, got, _ = _split_copy_start([], [big_grads[nm] for nm in names], plan, None,
                                                      f"pair_gather_start_{pd['k']}")
            pair_gathers[pd['k']] = dict(send=send, recv=recv, lands=got, wait_plan=wait_plan)

    def by_chip(gr, names, kds):
        return [gr[nm].reshape(N_CHIPS, -1, gr[nm].shape[1]) if kd == 'row' else gr[nm] for nm, kd in zip(names, kds)]

    def chip_start(k, names, kds, gs, theirs):
        parts = [_pair_add(g, t, core_idx, "pair_add") if kd == 'row'
                 else _pair_add(g[None], t[None], core_idx, "pair_add")[0] for g, t, kd in zip(gs, theirs, kds)]
        lands = [lax.empty((3, *_part_dims(part, kd)), BF16) for part, kd in zip(parts, kds)]
        plan = _chip_plan(kds)
        send, recv, parts, lands, token = _split_copy_start(parts, lands, plan, None, f"chip_start_{k}")
        return dict(send=send, recv=recv, parts=parts, lands=lands, token=token, plan=plan, names=names, kinds=kds,
                    k=k, l=order[k][0])

    pair_pending, chip_queue = None, []
    for k in reversed(range(n_groups)):
        l, gi = order[k]
        names = GROUPS[gi][1]
        kds = [BIG[nm] for nm in names]
        p, w = small(l), weights[k]
        tokens = [pd['token'] for pd in ([pair_pending] if pair_pending else []) + chip_queue] or None
        if k > 0:
            dx_next, gr = group_bwd(gi, dx, saved[k], p, w, l, tokens)
        else:
            dgu, g_gu, g_d, g_post = _ffn_bwd_weights(dx, saved[k], w['ffn1_w_down'], p['ffn1_norm_post'], "ffn", tokens)
            gr = {'ffn1_w_gate_up': g_gu, 'ffn1_w_down': g_d, 'ffn1_norm_post': g_post}
            dx_next = g_gu
        for pd in chip_queue:
            reduce_finish(pd, dx_next)
        chip_queue = []
        if pair_pending:
            pd = pair_pending
            gs_prev, theirs = _split_copy_wait(pd['send'], pd['recv'], pd['gs'], pd['lands'], pd['plan'], dx_next,
                                               f"pair_wait_{pd['k']}")
            chip_queue.append(chip_start(pd['k'], pd['names'], pd['kinds'], gs_prev, theirs))
            pair_pending = None
        gs = by_chip(gr, names, kds)
        if k > 1:
            plan = _pair_plan(kds)
            lands = [lax.empty(_half_shape(g, kd), BF16) for g, kd in zip(gs, kds)]
            send, recv, gs, lands, token = _split_copy_start(gs, lands, plan, None, f"pair_start_{k}")
            pair_pending = dict(send=send, recv=recv, gs=gs, lands=lands, token=token, plan=plan, names=names,
                                kinds=kds, k=k)
        else:
            started_now = chip_start(k, names, kds, gs, _pair_exchange(gs, kds, "pair_exchange_" + GROUPS[gi][0]))
            if k > 0:
                chip_queue.append(started_now)
        if k > 0:
            dx = dx_next
        else:
            pending = started_now
            dx, gr['ffn1_norm_pre'] = _ffn_bwd_input(dx, saved[k], dgu, p['ffn1_norm_pre'], w['ffn1_w_gate_up'], "ffn",
                                                     pending['token'])
        grads[l].update({nm: g for nm, g in gr.items() if nm not in BIG})
    for pd in chip_queue:
        reduce_finish(pd, dx)
    out = {}

    def two(a):
        return a.reshape(-1, a.shape[-1])

    def adamw(nm, g):
        res = _adamw_call(two(inputs[nm]), two(g), two(inputs['m_' + nm]), two(inputs['v_' + nm]), "adamw")
        out[nm] = tuple(a.reshape(inputs[nm].shape) for a in res)
        return res[1]

    def finish_weights(gi, after=None):
        names = GROUPS[gi][1]
        if gi in pair_gathers:
            st = pair_gathers[gi]
            _, got = _split_copy_wait(st['send'], st['recv'], [], st['lands'], st['wait_plan'], after,
                                      f"pair_gather_wait_{gi}")
        else:
            got = _pair_gather([big_grads[nm] for nm in names], "pair_gather_" + GROUPS[gi][0], after)
        return [adamw(nm, g) for nm, g in zip(names, got)]

    small_grads = [jnp.stack([grads[l][nm].reshape(inputs[nm].shape[1:]) if nm != 'conv_w' else grads[l][nm]
                              for l in range(n_layers)]) for nm in SMALL_NAMES]
    packed = _pack(small_grads + [loss_part])
    me = 4 * cx + 2 * cy + cc
    land = lax.dynamic_update_slice(lax.empty((N_DEV, *packed.shape), F32), packed[None], (me, 0, 0))
    plan, wait_plan = _all_gather_plan()
    send, recv, bufs, lands, _ = _split_copy_start([packed], [land], plan, pending['token'], "small_start")

    done = [d for gi in range(len(GROUPS) - 1, 0, -1) for d in finish_weights(gi, pending['token'])]
    reduce_finish(pending, done)
    done = finish_weights(0)

    _, lands = _split_copy_wait(send, recv, bufs, lands, wait_plan, done, "small_wait")
    summed = _unpack(_sum_slots(lands[0], "small_sum"), small_grads + [loss_part])
    loss = summed[-1][0, 0]
    small_g = dict(zip(SMALL_NAMES, summed[:-1]))
    small_g['conv_w'] = lax.dynamic_slice(small_g['conv_w'], (0, 0, chip * ccols), (n_layers, CONV_WIDTH, ccols))
    for nm in SMALL_NAMES:
        adamw(nm, small_g[nm])
    return (loss, dx[None], *[out[nm][0] for nm in WEIGHTS], *[out[nm][1] for nm in WEIGHTS],
            *[out[nm][2] for nm in WEIGHTS], *[out[nm][3] for nm in WEIGHTS])


def kernel(x, mem, ffn1_norm_pre, ffn1_w_gate_up, ffn1_w_down, ffn1_norm_post, mix_norm_pre, w_in, conv_w, a_log, dt_bias, sm_w, sm_b, sm_ln_g, sm_ln_b, dn_norm_w, w_out, mix_norm_post, xa_norm_pre, mem_norm, w_xq, w_xkv, w_xo, xa_norm_post, ffn2_norm_pre, ffn2_w_gate_up, ffn2_w_down, ffn2_norm_post, loss_target, m_ffn1_norm_pre, m_ffn1_w_gate_up, m_ffn1_w_down, m_ffn1_norm_post, m_mix_norm_pre, m_w_in, m_conv_w, m_a_log, m_dt_bias, m_sm_w, m_sm_b, m_sm_ln_g, m_sm_ln_b, m_dn_norm_w, m_w_out, m_mix_norm_post, m_xa_norm_pre, m_mem_norm, m_w_xq, m_w_xkv, m_w_xo, m_xa_norm_post, m_ffn2_norm_pre, m_ffn2_w_gate_up, m_ffn2_w_down, m_ffn2_norm_post, v_ffn1_norm_pre, v_ffn1_w_gate_up, v_ffn1_w_down, v_ffn1_norm_post, v_mix_norm_pre, v_w_in, v_conv_w, v_a_log, v_dt_bias, v_sm_w, v_sm_b, v_sm_ln_g, v_sm_ln_b, v_dn_norm_w, v_w_out, v_mix_norm_post, v_xa_norm_pre, v_mem_norm, v_w_xq, v_w_xkv, v_w_xo, v_xa_norm_post, v_ffn2_norm_pre, v_ffn2_w_gate_up, v_ffn2_w_down, v_ffn2_norm_post):
    vals = dict(locals())
    return _step(vals)
```

```python
import functools
import math

import jax
import jax.numpy as jnp
from jax import lax
from jax.experimental import pallas as pl
from jax.experimental.pallas import tpu as pltpu

F32, BF16 = jnp.float32, jnp.bfloat16
NORM_EPS = 1e-6
HEAD_DIM = 128
GM_CHUNK = 128
DN_CHUNK = 64
CONV_WIDTH = 4
XA_HEADS = 4
LANES = 128
MXU_WIDTH = 256
N_CHIPS = 4
N_DEV = 8
VMEM_LIMIT = 56 * 1024 * 1024
MESH_IDS = pl.DeviceIdType.MESH
ADAM_LR, ADAM_B1, ADAM_B2, ADAM_EPS, ADAM_WD, ADAM_STEP = 0.001, 0.9, 0.999, 1e-08, 0.01, 10

WEIGHTS = ['ffn1_norm_pre', 'ffn1_w_gate_up', 'ffn1_w_down', 'ffn1_norm_post', 'mix_norm_pre', 'w_in', 'conv_w',
           'a_log', 'dt_bias', 'sm_w', 'sm_b', 'sm_ln_g', 'sm_ln_b', 'dn_norm_w', 'w_out', 'mix_norm_post',
           'xa_norm_pre', 'mem_norm', 'w_xq', 'w_xkv', 'w_xo', 'xa_norm_post', 'ffn2_norm_pre', 'ffn2_w_gate_up',
           'ffn2_w_down', 'ffn2_norm_post']
BIG = {'ffn1_w_gate_up': 'col', 'ffn1_w_down': 'row', 'w_in': 'col', 'w_out': 'row', 'w_xq': 'row',
       'w_xkv': 'col', 'w_xo': 'row', 'ffn2_w_gate_up': 'col', 'ffn2_w_down': 'row'}
BIG_NAMES = list(BIG)
GROUPS = (('ffn1', ('ffn1_w_gate_up', 'ffn1_w_down')), ('mix', ('w_in', 'w_out')),
          ('xa', ('w_xq', 'w_xkv', 'w_xo')), ('ffn2', ('ffn2_w_gate_up', 'ffn2_w_down')))
SMALL_NAMES = [n for n in WEIGHTS if n not in BIG]


def _pcall(body, **kw):
    return pl.pallas_call(body, **kw)


def _params(sem=None):
    return pltpu.CompilerParams(dimension_semantics=sem, vmem_limit_bytes=VMEM_LIMIT)


def _as_list(after):
    if after is None:
        return []
    return list(after) if isinstance(after, (list, tuple)) else [after]


def _divtile(dim, cap, align=LANES):
    if dim <= cap:
        return dim
    t = (cap // align) * align
    while t > align and dim % t:
        t -= align
    assert dim % t == 0, (dim, cap)
    return t


_DN = {'nn': (((1,), (0,)), ((), ())), 'nt': (((1,), (1,)), ((), ())), 'tn': (((0,), (0,)), ((), ()))}


MM_MIN_STEPS = 8
MM_VMEM_BUDGET = 44 * 1024 * 1024


def _mm_tiles(m, n, k, out_itemsize, acc=False):
    def tile(dim, cap):
        return _divtile(dim, cap, MXU_WIDTH) if dim % MXU_WIDTH == 0 else _divtile(dim, cap)

    tm = tile(m, 1024)
    for cap_n, cap_k in ((2048, 2048), (1024, 2048), (2048, 1024), (1024, 1024), (1024, 512), (512, 512)):
        tn, tk = tile(n, cap_n), _divtile(k, cap_k)
        need = 2 * 2 * (tm * tk + tk * tn) + 2 * tm * tn * out_itemsize + (tm * tn * 4 if tk < k or acc else 0)
        if need <= MM_VMEM_BUDGET:
            break
    while (m // tm) * (n // tn) * (k // tk) < MM_MIN_STEPS and tn > 2 * MXU_WIDTH and n % (tn // 2) == 0:
        tn //= 2
    return tm, tn, tk


def _mm(a, b, mode, out_dtype, name, halves=None):
    if halves == 'a':
        assert mode == 'nt'
        a_shape, b_shape = (a.shape[1], 2 * a.shape[2]), b.shape
    elif halves == 'b':
        assert mode == 'tn'
        a_shape, b_shape = a.shape, (b.shape[1], 2 * b.shape[2])
    else:
        a_shape, b_shape = a.shape, b.shape
    if mode == 'nn':
        (m, k), (k2, n) = a_shape, b_shape
    elif mode == 'nt':
        (m, k), (n, k2) = a_shape, b_shape
    else:
        (k, m), (k2, n) = a_shape, b_shape
    assert k == k2, (a.shape, b.shape, mode)
    if halves == 'a':
        tm, tn, tk = _mm_tiles(m, n, k // 2, jnp.dtype(out_dtype).itemsize, acc=True)
    elif halves == 'b':
        tm, tn, tk = _mm_tiles(m, n // 2, k, jnp.dtype(out_dtype).itemsize)
    else:
        tm, tn, tk = _mm_tiles(m, n, k, jnp.dtype(out_dtype).itemsize)
    nk = k // tk
    a_spec = {'nn': pl.BlockSpec((tm, tk), lambda i, j, kk: (i, kk)),
              'nt': pl.BlockSpec((tm, tk), lambda i, j, kk: (i, kk)),
              'tn': pl.BlockSpec((tk, tm), lambda i, j, kk: (kk, i))}[mode]
    b_spec = {'nn': pl.BlockSpec((tk, tn), lambda i, j, kk: (kk, j)),
              'nt': pl.BlockSpec((tn, tk), lambda i, j, kk: (j, kk)),
              'tn': pl.BlockSpec((tk, tn), lambda i, j, kk: (kk, j))}[mode]
    if halves == 'a':
        per = nk // 2
        a_spec = pl.BlockSpec((None, tm, tk), lambda i, j, kk: (kk // per, i, kk % per))
    elif halves == 'b':
        per = (n // tn) // 2
        b_spec = pl.BlockSpec((None, tk, tn), lambda i, j, kk: (j // per, kk, j % per))

    def dot(a_ref, b_ref):
        return lax.dot_general(a_ref[...].astype(BF16), b_ref[...].astype(BF16), _DN[mode], preferred_element_type=F32)

    def body_once(a_ref, b_ref, o_ref):
        o_ref[...] = dot(a_ref, b_ref).astype(o_ref.dtype)

    def body_steps(a_ref, b_ref, o_ref, acc):
        kk = pl.program_id(2)

        @pl.when(kk == 0)
        def _():
            acc[...] = jnp.zeros_like(acc)

        acc[...] += dot(a_ref, b_ref)

        @pl.when(kk == nk - 1)
        def _():
            o_ref[...] = acc[...].astype(o_ref.dtype)

    return _pcall(
        body_once if nk == 1 else body_steps, name=name, grid=(m // tm, n // tn, nk),
        in_specs=[a_spec, b_spec], out_specs=pl.BlockSpec((tm, tn), lambda i, j, kk: (i, j)),
        out_shape=jax.ShapeDtypeStruct((m, n), out_dtype),
        scratch_shapes=[] if nk == 1 else [pltpu.VMEM((tm, tn), F32)],
        compiler_params=_params(("parallel", "parallel", "arbitrary")),
    )(a, b)


def _rowcall(name, f, rows, params=(), row_outs=(), acc_outs=(), br=256, nrows=None, after=None):
    rows = [r if isinstance(r, tuple) else (r, 0) for r in rows]
    t = nrows if nrows is not None else rows[0][0].shape[0]
    br = min(br, t)
    assert t % br == 0, (name, t, br)
    n_in, n_ro = len(rows) + len(params), len(row_outs)
    extra = _as_list(after)

    def row_spec(arr, off):
        assert off % br == 0
        return pl.BlockSpec((br, arr.shape[1]), functools.partial(lambda i, o: (i + o, 0), o=off // br))

    def whole_spec(shape):
        return pl.BlockSpec(shape, functools.partial(lambda i, nd: (0,) * nd, nd=len(shape)))

    def body(*refs):
        res = f(*[r[...] for r in refs[:n_in]])
        outs = refs[n_in + len(extra):]
        for o_ref, val in zip(outs[:n_ro], res[:n_ro]):
            o_ref[...] = val.astype(o_ref.dtype)
        if acc_outs:
            @pl.when(pl.program_id(0) == 0)
            def _():
                for o_ref in outs[n_ro:]:
                    o_ref[...] = jnp.zeros_like(o_ref)

            for o_ref, val in zip(outs[n_ro:], res[n_ro:]):
                o_ref[...] += val.astype(F32)

    out = _pcall(
        body, name=name, grid=(t // br,),
        in_specs=[row_spec(a, o) for a, o in rows] + [whole_spec(p.shape) for p in params]
        + [pl.BlockSpec(memory_space=pl.ANY)] * len(extra),
        out_specs=[pl.BlockSpec((br, c), lambda i: (i, 0)) for c, _ in row_outs] + [whole_spec(s) for s in acc_outs],
        out_shape=[jax.ShapeDtypeStruct((t, c), dt) for c, dt in row_outs]
        + [jax.ShapeDtypeStruct(s, F32) for s in acc_outs],
        compiler_params=_params(("arbitrary",) if acc_outs else ("parallel",)),
    )(*[a for a, _ in rows], *params, *extra)
    return out


_DN_BATCH = {'nn': (((2,), (1,)), ((0,), (0,))), 'nt': (((2,), (2,)), ((0,), (0,))), 'tn': (((1,), (1,)), ((0,), (0,)))}


def _dims(a, dn):
    return _DN_BATCH[dn] if a.ndim == 3 else _DN[dn]


def _bdot_raw(a, b, dn):
    return lax.dot_general(a.astype(BF16), b.astype(BF16), _dims(a, dn), preferred_element_type=F32)


def _hdot_raw(a, b, dn):
    a_hi, b_hi = a.astype(BF16), b.astype(BF16)
    a_lo, b_lo = (a - a_hi.astype(F32)).astype(BF16), (b - b_hi.astype(F32)).astype(BF16)

    def dot(p, q):
        return lax.dot_general(p, q, _dims(a, dn), preferred_element_type=F32)

    return dot(a_hi, b_hi) + (dot(a_hi, b_lo) + dot(a_lo, b_hi))


def _with_vjp(raw):
    @functools.partial(jax.custom_vjp, nondiff_argnums=(2,))
    def dot(a, b, dn):
        return raw(a, b, dn)

    def fwd(a, b, dn):
        return raw(a, b, dn), (a, b)

    def bwd(dn, res, ct):
        a, b = res
        if dn == 'nn':
            da, db = raw(ct, b, 'nt'), raw(a, ct, 'tn')
        elif dn == 'nt':
            da, db = raw(ct, b, 'nn'), raw(ct, a, 'tn')
        else:
            da, db = raw(b, ct, 'nt'), raw(a, ct, 'nn')
        return da.astype(a.dtype), db.astype(b.dtype)

    dot.defvjp(fwd, bwd)
    return dot


_bdot, _hdot = _with_vjp(_bdot_raw), _with_vjp(_hdot_raw)


def _rms(x, g):
    return x * lax.rsqrt(jnp.mean(x * x, axis=-1, keepdims=True) + NORM_EPS) * g


def _sigmoid(x):
    return 1.0 / (1.0 + jnp.exp(-x))


def _silu(x):
    return x * _sigmoid(x)


def _gelu(x):
    return 0.5 * x * (1.0 + lax.erf(x * (2.0 ** -0.5)))


def _norm_fwd(x, g, name, after=None):
    return _rowcall(name, lambda x_, g_: (_rms(x_, g_),), [x], [g], [(x.shape[1], BF16)], after=after)[0]


def _resnorm_fwd(x, f, g, alpha, name):
    return _rowcall(name, lambda x_, f_, g_: (x_ + alpha * _rms(f_, g_),), [x, f], [g], [(x.shape[1], F32)])[0]


def _resnorm_bwd(f, dx, g, alpha, name, after=None):
    def fn(f_, dx_, g_):
        _, vjp = jax.vjp(lambda a, b: alpha * _rms(a, b), f_, g_)
        return vjp(dx_)

    return _rowcall(name, fn, [f, dx], [g], [(f.shape[1], BF16)], [g.shape], after=after)


def _norm_bwd(x, dh, dx_out, g, name, after=None):
    def fn(x_, dh_, dxo_, g_):
        _, vjp = jax.vjp(_rms, x_, g_)
        dx, dg = vjp(dh_)
        return dxo_ + dx, dg

    return _rowcall(name, fn, [x, dh, dx_out], [g], [(x.shape[1], F32)], [g.shape], after=after)


def _norm_bwd_gain(x, dh, g, name):
    def fn(x_, dh_, g_):
        _, vjp = jax.vjp(lambda b: _rms(x_, b), g_)
        return vjp(dh_)

    return _rowcall(name, fn, [x, dh], [g], [], [g.shape])[0]


def _ffn_up(h, w_gu, name):
    t, d = h.shape
    ff = w_gu.shape[1] // 2
    tm, tn = _divtile(t, 1024), _divtile(ff, 512, MXU_WIDTH)
    nj = ff // tn

    def body(h_ref, wg_ref, wu_ref, gu_ref, a_ref):
        hv = h_ref[...]
        gate = jnp.dot(hv, wg_ref[...], preferred_element_type=F32)
        up = jnp.dot(hv, wu_ref[...], preferred_element_type=F32)
        gu_ref[0], gu_ref[1] = gate, up
        a_ref[...] = (_silu(gate) * up).astype(a_ref.dtype)

    return _pcall(body, name=name, grid=(t // tm, nj),
                  in_specs=[pl.BlockSpec((tm, d), lambda i, j: (i, 0)), pl.BlockSpec((d, tn), lambda i, j: (0, j)),
                            pl.BlockSpec((d, tn), lambda i, j: (0, j + nj))],
                  out_specs=[pl.BlockSpec((2, tm, tn), lambda i, j: (0, i, j)), pl.BlockSpec((tm, tn), lambda i, j: (i, j))],
                  out_shape=[jax.ShapeDtypeStruct((2, t, ff), F32), jax.ShapeDtypeStruct((t, ff), BF16)],
                  compiler_params=_params(("parallel", "parallel")))(h, w_gu, w_gu)


def _ffn_down_bwd(df, w_d, gu, name):
    t, d = df.shape
    ff = w_d.shape[0]
    tm, tn = _divtile(t, 1024), _divtile(ff, 512, MXU_WIDTH)

    def body(df_ref, w_ref, gu_ref, o_ref):
        da = lax.dot_general(df_ref[...], w_ref[...], _DN['nt'], preferred_element_type=F32)
        gate, up = gu_ref[0], gu_ref[1]
        s = _sigmoid(gate)
        o_ref[0] = (da * up * (s * (1.0 + gate * (1.0 - s)))).astype(o_ref.dtype)
        o_ref[1] = (da * (gate * s)).astype(o_ref.dtype)

    blk = pl.BlockSpec((2, tm, tn), lambda i, j: (0, i, j))
    return _pcall(body, name=name, grid=(t // tm, ff // tn),
                  in_specs=[pl.BlockSpec((tm, d), lambda i, j: (i, 0)), pl.BlockSpec((tn, d), lambda i, j: (j, 0)), blk],
                  out_specs=blk, out_shape=jax.ShapeDtypeStruct((2, t, ff), BF16),
                  compiler_params=_params(("parallel", "parallel")))(df, w_d, gu)


def _sgu_norm(v, lg, lb):
    vf = _gelu(v)
    mu = jnp.mean(vf, axis=-1, keepdims=True)
    var = jnp.mean(jnp.square(vf - mu), axis=-1, keepdims=True)
    return (vf - mu) * lax.rsqrt(var + NORM_EPS) * lg + lb


def _sgu_head(dot, u_h, vn_h, w_h, b_h):
    r = lax.broadcasted_iota(jnp.int32, w_h.shape, 0)
    c = lax.broadcasted_iota(jnp.int32, w_h.shape, 1)
    mixed = dot(jnp.where(r >= c, w_h, 0.0), vn_h, 'nn') + b_h
    return _gelu(u_h) * mixed


def _heads(width):
    return [slice(h * HEAD_DIM, (h + 1) * HEAD_DIM) for h in range(width // HEAD_DIM)]


def _sgu_fwd(u, v, w, bcol, lg, lb, name):
    def fn(u_, v_, w_, b_, lg_, lb_):
        vn = _sgu_norm(v_, lg_, lb_)
        return (jnp.concatenate([_sgu_head(_bdot_raw, u_[:, s], vn[:, s], w_[h], b_[h])
                                 for h, s in enumerate(_heads(u_.shape[1]))], axis=1),)

    return _rowcall(name, fn, [u, v], [w, bcol, lg, lb], [(u.shape[1], BF16)], br=GM_CHUNK)[0]


def _sgu_bwd(u, v, w, bcol, lg, lb, dy, name):
    def fn(u_, v_, dy_, w_, b_, lg_, lb_):
        vn, vjp_norm = jax.vjp(_sgu_norm, v_, lg_, lb_)
        du, dvn, dw, db = [], [], [], []
        for h, s in enumerate(_heads(u_.shape[1])):
            _, vjp = jax.vjp(functools.partial(_sgu_head, _bdot), u_[:, s], vn[:, s], w_[h], b_[h])
            a, b, c, d = vjp(dy_[:, s])
            du.append(a), dvn.append(b), dw.append(c[None]), db.append(d[None])
        dv, dlg, dlb = vjp_norm(jnp.concatenate(dvn, axis=1))
        return (jnp.concatenate(du, axis=1), dv, jnp.concatenate(dw, axis=0), jnp.concatenate(db, axis=0), dlg, dlb)

    wd = u.shape[1]
    return _rowcall(name, fn, [u, v, dy], [w, bcol, lg, lb], [(wd, BF16), (wd, BF16)],
                    [w.shape, bcol.shape, lg.shape, lb.shape], br=GM_CHUNK)


def _shift_down(x, s):
    if s == 0:
        return x
    rows = lax.broadcasted_iota(jnp.int32, x.shape, 0)
    return jnp.where(rows >= s, pltpu.roll(x, s, 0), 0.0)


def _shift_up(x, s):
    if s == 0:
        return x
    t = x.shape[0]
    rows = lax.broadcasted_iota(jnp.int32, x.shape, 0)
    return jnp.where(rows < t - s, pltpu.roll(x, t - s, 0), 0.0)


def _conv_pre(x, taps):
    acc = None
    for i in range(CONV_WIDTH):
        term = _shift_down(x, CONV_WIDTH - 1 - i) * taps[i]
        acc = term if acc is None else acc + term
    return acc


def _taps(w_ref):
    return [w_ref[i:i + 1, :] for i in range(CONV_WIDTH)]


def _conv_fwd(x, w, name):
    t, ch = x.shape
    cb = _divtile(ch, 512)

    def body(x_ref, w_ref, o_ref):
        o_ref[...] = _silu(_conv_pre(x_ref[...], _taps(w_ref)))

    return _pcall(body, name=name, grid=(ch // cb,),
                  in_specs=[pl.BlockSpec((t, cb), lambda j: (0, j)), pl.BlockSpec((CONV_WIDTH, cb), lambda j: (0, j))],
                  out_specs=pl.BlockSpec((t, cb), lambda j: (0, j)),
                  out_shape=jax.ShapeDtypeStruct((t, ch), F32), compiler_params=_params(("parallel",)))(x, w)


def _conv_bwd(x, w, dout, col_off, name):
    t, ch = dout.shape
    cb = _divtile(ch, 256)
    assert col_off % cb == 0
    off = col_off // cb

    def body(x_ref, w_ref, do_ref, dx_ref, dw_ref):
        xv, taps = x_ref[...], _taps(w_ref)
        pre = _conv_pre(xv, taps)
        s = _sigmoid(pre)
        dpre = do_ref[...] * (s * (1.0 + pre * (1.0 - s)))
        dx = None
        for i in range(CONV_WIDTH):
            sh = CONV_WIDTH - 1 - i
            term = _shift_up(dpre, sh) * taps[i]
            dx = term if dx is None else dx + term
            dw_ref[i:i + 1, :] = jnp.sum(dpre * _shift_down(xv, sh), axis=0, keepdims=True)
        dx_ref[...] = dx.astype(dx_ref.dtype)

    return _pcall(body, name=name, grid=(ch // cb,),
                  in_specs=[pl.BlockSpec((t, cb), lambda j: (0, j + off)),
                            pl.BlockSpec((CONV_WIDTH, cb), lambda j: (0, j + off)),
                            pl.BlockSpec((t, cb), lambda j: (0, j))],
                  out_specs=[pl.BlockSpec((t, cb), lambda j: (0, j)), pl.BlockSpec((CONV_WIDTH, cb), lambda j: (0, j))],
                  out_shape=[jax.ShapeDtypeStruct((t, ch), BF16), jax.ShapeDtypeStruct((CONV_WIDTH, ch), F32)],
                  compiler_params=_params(("parallel",)))(x, w, dout)


GATE_BLOCK = 256


def _gates(dot, b_raw, a_raw, a_log, dt_bias):
    blk = b_raw.shape[1]
    z = a_raw + dt_bias
    softplus = jnp.maximum(z, 0.0) + jnp.log(1.0 + jnp.exp(-jnp.abs(z)))
    g = -jnp.exp(a_log) * softplus
    ri = lax.broadcasted_iota(jnp.int32, (blk, blk), 0)
    ci = lax.broadcasted_iota(jnp.int32, (blk, blk), 1)
    upto = ((ri <= ci) & (ri // DN_CHUNK == ci // DN_CHUNK)).astype(F32)
    return _sigmoid(b_raw), dot(g, upto, 'nn')


def _gate_blocks(t):
    blk = min(GATE_BLOCK, t)
    return [slice(i, i + blk) for i in range(0, t, blk)]


def _whole_call(name, f, ins, out_shapes):
    n_in = len(ins)

    def body(*refs):
        for o_ref, val in zip(refs[n_in:], f(*[r[...] for r in refs[:n_in]])):
            o_ref[...] = val

    vmem = pl.BlockSpec(memory_space=pltpu.VMEM)
    return _pcall(body, name=name, in_specs=[vmem] * n_in, out_specs=[vmem] * len(out_shapes),
                  out_shape=[jax.ShapeDtypeStruct(s, F32) for s in out_shapes],
                  compiler_params=pltpu.CompilerParams(vmem_limit_bytes=VMEM_LIMIT))(*ins)


def _gates_fwd(b_raw, a_raw, a_log, dt_bias, name):
    def fn(b_, a_, al_, dt_):
        parts = [_gates(_hdot_raw, b_[:, s], a_[:, s], al_, dt_) for s in _gate_blocks(b_.shape[1])]
        return [jnp.concatenate(p, axis=1) for p in zip(*parts)]

    return _whole_call(name, fn, [b_raw, a_raw, a_log, dt_bias], [b_raw.shape, b_raw.shape])


def _gates_bwd(b_raw, a_raw, a_log, dt_bias, dbeta, dgcum, name):
    def fn(b_, a_, al_, dt_, dbeta_, dgcum_):
        db, da, dal, ddt = [], [], None, None
        for s in _gate_blocks(b_.shape[1]):
            _, vjp = jax.vjp(functools.partial(_gates, _hdot), b_[:, s], a_[:, s], al_, dt_)
            p, q, r, u = vjp((dbeta_[:, s], dgcum_[:, s]))
            db.append(p), da.append(q)
            dal, ddt = (r, u) if dal is None else (dal + r, ddt + u)
        return jnp.concatenate(db, axis=1), jnp.concatenate(da, axis=1), dal, ddt

    return _whole_call(name, fn, [b_raw, a_raw, a_log, dt_bias, dbeta, dgcum],
                       [b_raw.shape, a_raw.shape, a_log.shape, dt_bias.shape])


def _unit_lower_inverse_raw(a):
    c = a.shape[-1]
    eye = (lax.broadcasted_iota(jnp.int32, (c, c), 0) == lax.broadcasted_iota(jnp.int32, (c, c), 1)).astype(F32)
    inv, p = eye - a, _hdot_raw(a, a, 'nn')
    n_fac = int(math.log2(c)) - 1
    for it in range(n_fac):
        inv = inv + _hdot_raw(inv, p, 'nn')
        if it < n_fac - 1:
            p = _hdot_raw(p, p, 'nn')
    return inv


@jax.custom_vjp
def _unit_lower_inverse(a):
    return _unit_lower_inverse_raw(a)


def _unit_lower_inverse_fwd(a):
    inv = _unit_lower_inverse_raw(a)
    return inv, inv


def _unit_lower_inverse_bwd(inv, ct):
    return (-_hdot_raw(_hdot_raw(inv, ct, 'tn'), inv, 'nt'),)


_unit_lower_inverse.defvjp(_unit_lower_inverse_fwd, _unit_lower_inverse_bwd)


def _halves_raw(x):
    return x[..., :x.shape[-1] // 2], x[..., x.shape[-1] // 2:]


@jax.custom_vjp
def _halves(x):
    return _halves_raw(x)


_halves.defvjp(lambda x: (_halves_raw(x), None), lambda _, ct: (jnp.concatenate(ct, axis=-1),))

_DELTA_OPS = (_bdot_raw, _hdot_raw, _unit_lower_inverse_raw, _halves_raw)
_DELTA_OPS_VJP = (_bdot, _hdot, _unit_lower_inverse, _halves)


def _delta_chunk(ops, state, q, k, v, gc, gr, bc):
    bd, hd, inverse, halves = ops
    c, d = q.shape[-2:]
    ri = lax.broadcasted_iota(jnp.int32, (c, c), 0)
    ci = lax.broadcasted_iota(jnp.int32, (c, c), 1)
    causal, strict = ri >= ci, ri > ci
    qn = q * lax.rsqrt(jnp.sum(q * q, axis=-1, keepdims=True) + NORM_EPS) * (d ** -0.5)
    kn = k * lax.rsqrt(jnp.sum(k * k, axis=-1, keepdims=True) + NORM_EPS)
    gcum = jnp.broadcast_to(gc, q.shape)
    decay = jnp.where(causal, jnp.exp(jnp.where(causal, gc - gr, 0.0)), 0.0)
    bb = jnp.broadcast_to(bc, q.shape)
    k_beta = kn * bb
    inv = inverse(jnp.where(strict, bd(k_beta, kn, 'nt') * decay, 0.0))
    uw = hd(inv, jnp.concatenate([v * bb, k_beta * jnp.exp(gcum)], axis=-1), 'nn')
    u, w = halves(uw)
    qk = jnp.where(causal, bd(qn, kn, 'nt') * decay, 0.0)
    v_new = u - bd(w, state, 'nn')
    o = bd(qn * jnp.exp(gcum), state, 'nn') + bd(qk, v_new, 'nn')
    last = lax.broadcasted_iota(jnp.int32, (c, d), 0) == c - 1
    g_last = jnp.sum(jnp.where(last, gcum, 0.0), axis=-2, keepdims=True)
    k_dec = kn * jnp.exp(g_last - gcum)
    return state * jnp.exp(g_last) + bd(k_dec, v_new, 'tn'), o


DN_HEADS_PER_STEP = 8


def _by_head(ref):
    return jnp.stack([ref[:, s] for s in _heads(ref.shape[1])])


def _delta_specs(nh, nc, rev):
    c, d = DN_CHUNK, HEAD_DIM
    hb = min(DN_HEADS_PER_STEP, nh)
    ng = nh // hb
    ch = (lambda n: nc - 1 - n) if rev else (lambda n: n)
    qkv = [pl.BlockSpec((c, hb * d), functools.partial(lambda h, n, o: (ch(n), o * ng + h), o=o)) for o in range(3)]
    col = pl.BlockSpec((hb, c, 1), lambda h, n: (h, ch(n), 0))
    row = pl.BlockSpec((hb, None, 1, c), lambda h, n: (h, ch(n), 0, 0))
    st = pl.BlockSpec((hb, None, d, d), lambda h, n: (h, ch(n), 0, 0))
    tok = pl.BlockSpec((c, hb * d), lambda h, n: (ch(n), h))
    return hb, ng, qkv, col, row, st, tok


def _delta_fwd(qkv, gcol, grow, bcol, name):
    t = qkv.shape[0]
    nh, nc, d = gcol.shape[0], t // DN_CHUNK, HEAD_DIM
    hb, ng, qkv_s, col, row, st, tok = _delta_specs(nh, nc, False)

    def body(q_ref, k_ref, v_ref, gc_ref, gr_ref, bc_ref, o_ref, st_ref, state):
        @pl.when(pl.program_id(1) == 0)
        def _():
            state[...] = jnp.zeros_like(state)

        s0 = state[...]
        st_ref[...] = s0
        s1, o = _delta_chunk(_DELTA_OPS, s0, _by_head(q_ref), _by_head(k_ref), _by_head(v_ref), gc_ref[...],
                             gr_ref[...], bc_ref[...])
        for j, s in enumerate(_heads(hb * d)):
            o_ref[:, s] = o[j]
        state[...] = s1

    return _pcall(body, name=name, grid=(ng, nc), in_specs=qkv_s + [col, row, col], out_specs=[tok, st],
                  out_shape=[jax.ShapeDtypeStruct((t, nh * d), F32), jax.ShapeDtypeStruct((nh, nc, d, d), F32)],
                  scratch_shapes=[pltpu.VMEM((hb, d, d), F32)],
                  compiler_params=_params(("parallel", "arbitrary")))(qkv, qkv, qkv, gcol, grow, bcol)


def _delta_bwd(qkv, gcol, grow, bcol, states, do, name):
    t = qkv.shape[0]
    nh, nc, d = gcol.shape[0], t // DN_CHUNK, HEAD_DIM
    hb, ng, qkv_s, col, row, st, tok = _delta_specs(nh, nc, True)

    def body(q_ref, k_ref, v_ref, gc_ref, gr_ref, bc_ref, st_ref, do_ref,
             dq_ref, dk_ref, dv_ref, dgc_ref, dgr_ref, dbc_ref, dstate):
        @pl.when(pl.program_id(1) == 0)
        def _():
            dstate[...] = jnp.zeros_like(dstate)

        _, vjp = jax.vjp(functools.partial(_delta_chunk, _DELTA_OPS_VJP), st_ref[...], _by_head(q_ref), _by_head(k_ref),
                         _by_head(v_ref), gc_ref[...], gr_ref[...], bc_ref[...])
        ds, dq, dk, dv, dgc, dgr, dbc = vjp((dstate[...], _by_head(do_ref)))
        for j, s in enumerate(_heads(hb * d)):
            dq_ref[:, s], dk_ref[:, s], dv_ref[:, s] = dq[j], dk[j], dv[j]
        dgc_ref[...], dgr_ref[...], dbc_ref[...] = dgc, dgr, dbc
        dstate[...] = ds

    tok_out = jax.ShapeDtypeStruct((t, nh * d), F32)
    return _pcall(body, name=name, grid=(ng, nc), in_specs=qkv_s + [col, row, col, st, tok],
                  out_specs=[tok, tok, tok, col, row, col],
                  out_shape=[tok_out, tok_out, tok_out, jax.ShapeDtypeStruct(gcol.shape, F32),
                             jax.ShapeDtypeStruct(grow.shape, F32), jax.ShapeDtypeStruct(bcol.shape, F32)],
                  scratch_shapes=[pltpu.VMEM((hb, d, d), F32)],
                  compiler_params=_params(("parallel", "arbitrary")))(qkv, qkv, qkv, gcol, grow, bcol, states, do)


def _outgate_head(o_h, z_h, w):
    return _rms(o_h, w) * _silu(z_h)


def _outgate_fwd(o, z, w, name):
    def fn(o_, z_, w_):
        return (jnp.concatenate([_outgate_head(o_[:, s], z_[:, s], w_) for s in _heads(o_.shape[1])], axis=1),)

    return _rowcall(name, fn, [o, z], [w], [(o.shape[1], BF16)])[0]


def _outgate_bwd(o, z, w, dy, name):
    def fn(o_, z_, dy_, w_):
        do, dz, dw = [], [], None
        for s in _heads(o_.shape[1]):
            _, vjp = jax.vjp(_outgate_head, o_[:, s], z_[:, s], w_)
            a, b, c = vjp(dy_[:, s])
            do.append(a), dz.append(b)
            dw = c if dw is None else dw + c
        return jnp.concatenate(do, axis=1), jnp.concatenate(dz, axis=1), dw

    wd = o.shape[1]
    return _rowcall(name, fn, [o, z, dy], [w], [(wd, F32), (wd, BF16)], [w.shape])


def _attn_head(dot, q_h, k_h, v_h):
    s = dot(q_h, k_h, 'nt') * (q_h.shape[1] ** -0.5)
    e = jnp.exp(s - lax.stop_gradient(jnp.max(s, axis=-1, keepdims=True)))
    p = e / jnp.sum(e, axis=-1, keepdims=True)
    return dot(p, v_h, 'nn')


def _xa_slices(width):
    hd = width // XA_HEADS
    return [slice(h * hd, (h + 1) * hd) for h in range(XA_HEADS)]


def _attn_fwd(q, kv, name):
    wd = q.shape[1]

    def fn(q_, kv_):
        k_, v_ = kv_[:, :wd], kv_[:, wd:]
        return (jnp.concatenate([_attn_head(_bdot_raw, q_[:, s], k_[:, s], v_[:, s]) for s in _xa_slices(wd)],
                                axis=1),)

    return _rowcall(name, fn, [q], [kv], [(wd, BF16)])[0]


def _attn_bwd(q, kv, do, name):
    wd = q.shape[1]

    def fn(q_, do_, kv_):
        k_, v_ = kv_[:, :wd].astype(F32), kv_[:, wd:].astype(F32)
        dq, dk, dv = [], [], []
        for s in _xa_slices(wd):
            _, vjp = jax.vjp(functools.partial(_attn_head, _bdot), q_[:, s].astype(F32), k_[:, s], v_[:, s])
            a, b, c = vjp(do_[:, s])
            dq.append(a), dk.append(b), dv.append(c)
        return jnp.concatenate(dq, axis=1), jnp.concatenate(dk + dv, axis=1)

    return _rowcall(name, fn, [q, do], [kv], [(wd, BF16)], [kv.shape])


def _loss_call(y, target, name):
    d = y.shape[1]

    def fn(y_, t_):
        err = y_ - t_
        part = 0.5 * jnp.sum(jnp.sum(err * err, axis=1, keepdims=True), axis=0, keepdims=True) / d
        return err / d, jnp.broadcast_to(part, (1, LANES))

    return _rowcall(name, fn, [y, target], [], [(d, F32)], [(1, LANES)])


def _adamw_call(w, g, m, v, name):
    c = w.shape[1]

    def fn(w_, g_, m_, v_):
        g_ = g_[:, :c]
        m1 = ADAM_B1 * m_ + (1.0 - ADAM_B1) * g_
        v1 = ADAM_B2 * v_ + (1.0 - ADAM_B2) * jnp.square(g_)
        m_hat = m1 / (1.0 - ADAM_B1 ** ADAM_STEP)
        v_hat = v1 / (1.0 - ADAM_B2 ** ADAM_STEP)
        return g_, -ADAM_LR * (m_hat / (jnp.sqrt(v_hat) + ADAM_EPS) + ADAM_WD * w_), m1, v1

    return _rowcall(name, fn, [w, g, m, v], [], [(c, F32)] * 4, br=_divtile(w.shape[0], 128, 8))


STREAM_BLOCK_BYTES = 4 * 1024 * 1024


def _rows_for(cols, itemsize):
    return max(16, STREAM_BLOCK_BYTES // (cols * itemsize) // 16 * 16)


def _pair_add(grad, theirs, core, name):
    s, hr, cols = theirs.shape
    br = _divtile(hr, _rows_for(cols, 2), 16)
    nb = hr // br

    def body(c_ref, g_ref, t_ref, o_ref):
        o_ref[...] = (g_ref[...].astype(F32) + t_ref[...].astype(F32)).astype(o_ref.dtype)

    spec = pl.BlockSpec((None, br, cols), lambda j, i, c_ref: (j, i, 0))
    return _pcall(body, name=name, out_shape=jax.ShapeDtypeStruct(theirs.shape, BF16),
                  grid_spec=pltpu.PrefetchScalarGridSpec(
                      num_scalar_prefetch=1, grid=(s, nb),
                      in_specs=[pl.BlockSpec((None, br, cols), lambda j, i, c_ref: (j, c_ref[0] * nb + i, 0)), spec],
                      out_specs=spec),
                  compiler_params=_params(("parallel", "parallel")))(core, grad, theirs)


def _chip_sum(part, landed, kind, where, grad, layer, name):
    _, hr, cw = landed.shape
    br = _divtile(hr, _rows_for(cw, 4), 16)
    nb = hr // br

    def body(w_ref, p_ref, a_ref, b_ref, d_ref, g_ref, o_ref):
        o_ref[...] = ((p_ref[...].astype(F32) + a_ref[...].astype(F32)) + b_ref[...].astype(F32)) + d_ref[...].astype(F32)

    if kind == 'row':
        own = pl.BlockSpec((None, br, cw), lambda i, w_ref: (w_ref[0], i, 0))
    else:
        own = pl.BlockSpec((br, cw), lambda i, w_ref: (i, w_ref[0]))
    got = [pl.BlockSpec((None, br, cw), functools.partial(lambda i, w_ref, k: (k, i, 0), k=k)) for k in range(3)]
    return _pcall(body, name=name, out_shape=jax.ShapeDtypeStruct(grad.shape, F32),
                  grid_spec=pltpu.PrefetchScalarGridSpec(
                      num_scalar_prefetch=1, grid=(nb,), in_specs=[own] + got + [ANY_SPEC],
                      out_specs=pl.BlockSpec((None, br, cw), lambda i, w_ref: (layer, w_ref[1] * nb + i, 0))),
                  input_output_aliases={5: 0},
                  compiler_params=_params(("parallel",)))(where, part, landed, landed, landed, grad)


def _position():
    x, y, c = lax.axis_index("x"), lax.axis_index("y"), lax.axis_index("c")
    others = [(1 - x, y), (x, 1 - y), (1 - x, 1 - y)]
    return x, y, c, others


def _any_specs(n):
    return [pl.BlockSpec(memory_space=pl.ANY)] * n


def _ds(start, size):
    return pl.ds(pl.multiple_of(start, size), size)


HBM_SPEC = pl.BlockSpec(memory_space=pltpu.HBM)
SEM_SPEC = pl.BlockSpec(memory_space=pltpu.SEMAPHORE)
ANY_SPEC = pl.BlockSpec(memory_space=pl.ANY)
DATAFLOW = pltpu.SideEffectType.DATAFLOW_SIDE_EFFECTING


def _hbm(a):
    return pltpu.with_memory_space_constraint(a, pltpu.HBM)


def _full_shape(shard_shape, kind):
    r, cw = shard_shape
    return (N_CHIPS, r, cw) if kind == 'row' else (r, N_CHIPS * cw)


def _block_dims(full, kind):
    return (full.shape[1], full.shape[2]) if kind == 'row' else (full.shape[0], full.shape[1] // N_CHIPS)


def _region(full, kind, chip, half):
    if kind == 'all':
        return full.at[chip]
    r, cw = _block_dims(full, kind)
    if kind == 'row':
        return full.at[chip, _ds(half * (r // 2), r // 2), :]
    return full.at[_ds(half * (r // 2), r // 2), _ds(chip * cw, cw)]


def _place_block(full, shards, layer, kind, chip, name, after=None):
    _, r, cs = shards.shape
    cw = _block_dims(full, kind)[1]
    br = _divtile(r, 256, 16)

    def body(c_ref, s_ref, *rest):
        o_ref = rest[-1]
        if cs == cw:
            o_ref[...] = s_ref[...].astype(o_ref.dtype)
        else:
            o_ref[:, :cs] = s_ref[...].astype(o_ref.dtype)
            o_ref[:, cs:] = jnp.zeros((br, cw - cs), o_ref.dtype)

    if kind == 'row':
        out_spec = pl.BlockSpec((None, br, cw), lambda i, c_ref: (c_ref[0], i, 0))
    else:
        out_spec = pl.BlockSpec((br, cw), lambda i, c_ref: (i, c_ref[0]))
    extra = _as_list(after)
    return _pcall(body, name=name, out_shape=jax.ShapeDtypeStruct(full.shape, full.dtype),
                  grid_spec=pltpu.PrefetchScalarGridSpec(
                      num_scalar_prefetch=1, grid=(r // br,),
                      in_specs=[pl.BlockSpec((None, br, cs), lambda i, c_ref: (layer, i, 0)), ANY_SPEC]
                      + [ANY_SPEC] * len(extra), out_specs=out_spec),
                  input_output_aliases={2: 0}, compiler_params=_params(("parallel",)))(chip, shards, full, *extra)


def _split_copy_start(bufs, lands, plan, after, name):
    nb, nl = len(bufs), len(lands)
    extra = _as_list(after)

    def body(*refs):
        ins = refs[:nb + nl]
        send, recv = refs[nb + nl + len(extra)], refs[nb + nl + len(extra) + 1]
        for cp in plan(ins[:nb], ins[nb:], send, recv):
            cp.start()
        refs[-1][...] = jnp.zeros_like(refs[-1])

    n_copies = plan.n_copies
    out = _pcall(
        body, name=name,
        out_shape=(pltpu.SemaphoreType.DMA((n_copies,)), pltpu.SemaphoreType.DMA((n_copies,)),
                   *[pltpu.HBM(a.shape, a.dtype) for a in (*bufs, *lands)], jax.ShapeDtypeStruct((8, LANES), F32)),
        in_specs=[HBM_SPEC] * (nb + nl) + [ANY_SPEC] * len(extra),
        out_specs=(SEM_SPEC, SEM_SPEC, *[HBM_SPEC] * (nb + nl), pl.BlockSpec(memory_space=pltpu.VMEM)),
        input_output_aliases={i: 2 + i for i in range(nb + nl)},
        compiler_params=pltpu.CompilerParams(has_side_effects=DATAFLOW),
    )(*[_hbm(a) for a in (*bufs, *lands)], *extra)
    return out[0], out[1], list(out[2:2 + nb]), list(out[2 + nb:2 + nb + nl]), out[-1]


def _split_copy_wait(send, recv, bufs, lands, plan, after, name):
    nb, nl = len(bufs), len(lands)
    extra = _as_list(after)

    def body(*refs):
        ins = refs[:nb + nl]
        send_ref, recv_ref = refs[nb + nl], refs[nb + nl + 1]
        for cp in plan(ins[:nb], ins[nb:], send_ref, recv_ref):
            cp.wait_send()
            cp.wait_recv()

    out = _pcall(
        body, name=name, out_shape=tuple(pltpu.HBM(a.shape, a.dtype) for a in (*bufs, *lands)),
        in_specs=[HBM_SPEC] * (nb + nl) + [SEM_SPEC, SEM_SPEC] + [ANY_SPEC] * len(extra),
        out_specs=tuple([HBM_SPEC] * (nb + nl)), input_output_aliases={i: i for i in range(nb + nl)},
        compiler_params=pltpu.CompilerParams(has_side_effects=DATAFLOW),
    )(*bufs, *lands, send, recv, *extra)
    return list(out[:nb]), list(out[nb:])


def _gather_plan(kinds):
    def plan(bufs, lands, send, recv):
        x, y, c, others = _position()
        me = 2 * x + y
        return [pltpu.make_async_remote_copy(
            src_ref=_region(lands[w], kinds[w], me, c), dst_ref=_region(lands[w], kinds[w], me, c),
            send_sem=send.at[3 * w + k], recv_sem=recv.at[3 * w + k], device_id=(px, py, c), device_id_type=MESH_IDS)
            for w in range(len(lands)) for k, (px, py) in enumerate(others)]

    def wait_plan(bufs, lands, send, recv):
        x, y, c, others = _position()
        me = 2 * x + y
        return [pltpu.make_async_remote_copy(
            src_ref=_region(lands[w], kinds[w], me, c), dst_ref=_region(lands[w], kinds[w], 2 * px + py, c),
            send_sem=send.at[3 * w + k], recv_sem=recv.at[3 * w + k], device_id=(px, py, c), device_id_type=MESH_IDS)
            for w in range(len(lands)) for k, (px, py) in enumerate(others)]

    plan.n_copies = wait_plan.n_copies = 3 * len(kinds)
    return plan, wait_plan


def _gather_forward(fulls, kinds, name):
    n = len(fulls)

    def body(*refs):
        dst = refs[n:2 * n]
        send, recv = refs[2 * n:]
        x, y, c, others = _position()
        sib = (x, y, 1 - c)

        def copy(w, k, chip, half):
            reg = _region(dst[w], kinds[w], chip, half)
            return pltpu.make_async_remote_copy(src_ref=reg, dst_ref=reg, send_sem=send.at[3 * w + k],
                                                recv_sem=recv.at[3 * w + k], device_id=sib, device_id_type=MESH_IDS)

        give = [copy(w, k, 2 * px + py, c) for w in range(n) for k, (px, py) in enumerate(others)]
        for cp in give:
            cp.start()
        for w in range(n):
            for k, (px, py) in enumerate(others):
                copy(w, k, 2 * px + py, 1 - c).wait_recv()
        for cp in give:
            cp.wait_send()

    return _pcall(body, name=name, in_specs=_any_specs(n), out_specs=_any_specs(n),
                  out_shape=[jax.ShapeDtypeStruct(f.shape, f.dtype) for f in fulls],
                  input_output_aliases={i: i for i in range(n)},
                  scratch_shapes=[pltpu.SemaphoreType.DMA((3 * n,)), pltpu.SemaphoreType.DMA((3 * n,))])(*fulls)


def _grad_half(ref, kind, half):
    if kind == 'row':
        hr = ref.shape[1] // 2
        return ref.at[:, _ds(half * hr, hr), :]
    hr = ref.shape[0] // 2
    return ref.at[_ds(half * hr, hr), :]


def _half_shape(g, kind):
    return (g.shape[0], g.shape[1] // 2, g.shape[2]) if kind == 'row' else (g.shape[0] // 2, g.shape[1])


def _pair_plan(kinds):
    def plan(bufs, lands, send, recv):
        x, y, c, _ = _position()
        return [pltpu.make_async_remote_copy(src_ref=_grad_half(bufs[w], kinds[w], 1 - c), dst_ref=lands[w],
                                             send_sem=send.at[w], recv_sem=recv.at[w], device_id=(x, y, 1 - c),
                                             device_id_type=MESH_IDS) for w in range(len(bufs))]

    plan.n_copies = len(kinds)
    return plan


def _pair_exchange(grads, kinds, name):
    n = len(grads)

    def body(*refs):
        src, theirs_ref = refs[:n], refs[n:2 * n]
        send, recv = refs[2 * n:]
        x, y, c, _ = _position()
        sib = (x, y, 1 - c)

        def half(w, h):
            if kinds[w] == 'row':
                hr = src[w].shape[1] // 2
                return src[w].at[:, _ds(h * hr, hr), :]
            hr = src[w].shape[0] // 2
            return src[w].at[_ds(h * hr, hr), :]

        give = [pltpu.make_async_remote_copy(src_ref=half(w, 1 - c), dst_ref=theirs_ref[w], send_sem=send.at[w],
                                             recv_sem=recv.at[w], device_id=sib, device_id_type=MESH_IDS)
                for w in range(n)]
        for cp in give:
            cp.start()
        for cp in give:
            cp.wait()

    def half_shape(g, kind):
        return (g.shape[0], g.shape[1] // 2, g.shape[2]) if kind == 'row' else (g.shape[0] // 2, g.shape[1])

    return _pcall(body, name=name, in_specs=_any_specs(n), out_specs=_any_specs(n),
                  out_shape=[jax.ShapeDtypeStruct(half_shape(g, kd), g.dtype) for g, kd in zip(grads, kinds)],
                  scratch_shapes=[pltpu.SemaphoreType.DMA((n,)), pltpu.SemaphoreType.DMA((n,))])(*grads)


def _part_dims(part, kind):
    return (part.shape[1], part.shape[2]) if kind == 'row' else (part.shape[0], part.shape[1] // N_CHIPS)


def _chip_plan(kinds):
    def plan(bufs, lands, send, recv):
        x, y, c, others = _position()

        def slot(w, chip):
            if kinds[w] == 'row':
                return bufs[w].at[chip]
            cw = _part_dims(bufs[w], kinds[w])[1]
            return bufs[w].at[:, _ds(chip * cw, cw)]

        return [pltpu.make_async_remote_copy(src_ref=slot(w, 2 * px + py), dst_ref=lands[w].at[k],
                                             send_sem=send.at[3 * w + k], recv_sem=recv.at[3 * w + k],
                                             device_id=(px, py, c), device_id_type=MESH_IDS)
                for w in range(len(bufs)) for k, (px, py) in enumerate(others)]

    plan.n_copies = 3 * len(kinds)
    return plan


def _pair_gather_plan(n, n_layers):
    def copies(lands, send, recv, take):
        x, y, c, _ = _position()
        out = []
        for w in range(n):
            hr = lands[w].shape[1] // 2
            for l in range(n_layers):
                mine = lands[w].at[l, _ds(c * hr, hr), :]
                theirs = lands[w].at[l, _ds((1 - c) * hr, hr), :]
                out.append(pltpu.make_async_remote_copy(
                    src_ref=mine, dst_ref=theirs if take else mine, send_sem=send.at[w * n_layers + l],
                    recv_sem=recv.at[w * n_layers + l], device_id=(x, y, 1 - c), device_id_type=MESH_IDS))
        return out

    def plan(bufs, lands, send, recv):
        return copies(lands, send, recv, False)

    def wait_plan(bufs, lands, send, recv):
        return copies(lands, send, recv, True)

    plan.n_copies = wait_plan.n_copies = n * n_layers
    return plan, wait_plan


def _pair_gather(grads, name, after=None):
    n = len(grads)
    n_layers = grads[0].shape[0]
    extra = _as_list(after)

    def body(*refs):
        dst = refs[n + len(extra):2 * n + len(extra)]
        send, recv = refs[2 * n + len(extra):]
        x, y, c, _ = _position()
        sib = (x, y, 1 - c)

        def copy(w, l, half):
            hr = dst[w].shape[1] // 2
            rows = dst[w].at[l, _ds(half * hr, hr), :]
            return pltpu.make_async_remote_copy(src_ref=rows, dst_ref=rows, send_sem=send.at[w * n_layers + l],
                                                recv_sem=recv.at[w * n_layers + l], device_id=sib,
                                                device_id_type=MESH_IDS)

        give = [copy(w, l, c) for w in range(n) for l in range(n_layers)]
        for cp in give:
            cp.start()
        for w in range(n):
            for l in range(n_layers):
                copy(w, l, 1 - c).wait_recv()
        for cp in give:
            cp.wait_send()

    m = n * n_layers
    return _pcall(body, name=name, in_specs=_any_specs(n + len(extra)), out_specs=_any_specs(n),
                  out_shape=[jax.ShapeDtypeStruct(g.shape, g.dtype) for g in grads],
                  input_output_aliases={i: i for i in range(n)},
                  scratch_shapes=[pltpu.SemaphoreType.DMA((m,)), pltpu.SemaphoreType.DMA((m,))])(*grads, *extra)


def _all_gather_plan():
    flips = [(fx, fy, fc) for fx in (0, 1) for fy in (0, 1) for fc in (0, 1)][1:]

    def copies(bufs, lands, send, recv, take):
        x, y, c, _ = _position()
        out = []
        for k, f in enumerate(flips):
            px, py, pc = (1 - x if f[0] else x, 1 - y if f[1] else y, 1 - c if f[2] else c)
            slot = 4 * px + 2 * py + pc if take else 4 * x + 2 * y + c
            out.append(pltpu.make_async_remote_copy(src_ref=bufs[0], dst_ref=lands[0].at[slot], send_sem=send.at[k],
                                                    recv_sem=recv.at[k], device_id=(px, py, pc),
                                                    device_id_type=MESH_IDS))
        return out

    def plan(bufs, lands, send, recv):
        return copies(bufs, lands, send, recv, False)

    def wait_plan(bufs, lands, send, recv):
        return copies(bufs, lands, send, recv, True)

    plan.n_copies = wait_plan.n_copies = N_DEV - 1
    return plan, wait_plan


def _sum_slots(land, name):
    n, rows, cols = land.shape

    def fn(*slots):
        total = slots[0]
        for s in slots[1:]:
            total = total + s
        return (total,)

    flat = land.reshape(n * rows, cols)
    return _rowcall(name, fn, [(flat, d * rows) for d in range(n)], [], [(cols, F32)], br=_divtile(rows, 256, 8),
                    nrows=rows)[0]


def _pack(arrays):
    flat = jnp.concatenate([a.reshape(-1).astype(F32) for a in arrays])
    pad = (-flat.shape[0]) % (8 * LANES)
    return jnp.pad(flat, (0, pad)).reshape(-1, LANES)


def _unpack(buf, like):
    flat, out, off = buf.reshape(-1), [], 0
    for a in like:
        out.append(flat[off:off + a.size].reshape(a.shape))
        off += a.size
    return out


def _row2(v):
    return v.reshape(1, -1)


def _ffn_fwd(x, g_pre, w_gu, w_d, g_post, tag, after=None):
    h = _norm_fwd(x, g_pre, tag + "_pre", after)
    gu, a = _ffn_up(h, w_gu, tag + "_gu")
    f = _mm(a, w_d, 'nn', F32, tag + "_down")
    return _resnorm_fwd(x, f, g_post, 0.5, tag + "_post"), (x, h, gu, a, f)


def _ffn_bwd_weights(dx_out, saved, w_d, g_post, tag, after=None):
    x, h, gu, a, f = saved
    df, dg_post = _resnorm_bwd(f, dx_out, g_post, 0.5, tag + "_post_b", after)
    dw_d = _mm(a, df, 'tn', BF16, tag + "_down_bw")
    dgu = _ffn_down_bwd(df, w_d, gu, tag + "_down_bx")
    dw_gu = _mm(h, dgu, 'tn', BF16, tag + "_gu_bw", halves='b')
    return dgu, dw_gu, dw_d, dg_post


def _ffn_bwd_input(dx_out, saved, dgu, g_pre, w_gu, tag, after=None):
    x = saved[0]
    dh = _mm(dgu, w_gu, 'nt', F32, tag + "_gu_bx", halves='a')
    return _norm_bwd(x, dh, dx_out, g_pre, tag + "_pre_b", after)


def _ffn_bwd(dx_out, saved, g_pre, w_gu, w_d, g_post, tag, after=None):
    dgu, dw_gu, dw_d, dg_post = _ffn_bwd_weights(dx_out, saved, w_d, g_post, tag, after)
    dx, dg_pre = _ffn_bwd_input(dx_out, saved, dgu, g_pre, w_gu, tag)
    return dx, dg_pre, dw_gu, dw_d, dg_post


def _split_cols(pp, cs, cp, widths):
    proj = jnp.concatenate([pp[:, j * cp:j * cp + cs] for j in range(N_CHIPS)], axis=1)
    out, off = [], 0
    for wd in widths:
        out.append(proj[:, off:off + wd])
        off += wd
    return out


def _join_cols(pieces, cs, cp):
    proj = jnp.concatenate([pc.astype(BF16) for pc in pieces], axis=1)
    t = proj.shape[0]
    cols = []
    for j in range(N_CHIPS):
        cols += [proj[:, j * cs:(j + 1) * cs], jnp.zeros((t, cp - cs), proj.dtype)]
    return jnp.concatenate(cols, axis=1)


def _mix_fwd(x, p, w_in, w_out, conv_w, cs, cp, tag, after=None):
    t, d = x.shape
    half = d // 2
    nh = half // HEAD_DIM
    h = _norm_fwd(x, p['mix_norm_pre'], tag + "_pre", after)
    pp = _mm(h, w_in, 'nn', F32, tag + "_in")
    u_a, v_a, qkv_raw, z, b_raw, a_raw = _split_cols(pp, cs, cp, [half, half, 3 * half, half, nh, nh])
    y_a = _sgu_fwd(u_a, v_a, p['sm_w'], p['sm_b'], p['sm_ln_g'], p['sm_ln_b'], tag + "_sgu")
    qkv = _conv_fwd(qkv_raw, conv_w, tag + "_conv")
    b_raw, a_raw = b_raw.T, a_raw.T
    beta, gcum = _gates_fwd(b_raw, a_raw, p['a_log'].T, p['dt_bias'].T, tag + "_gates")
    gcol, bcol = gcum[:, :, None], beta[:, :, None]
    grow = gcum.reshape(nh, t // DN_CHUNK, 1, DN_CHUNK)
    o, states = _delta_fwd(qkv, gcol, grow, bcol, tag + "_delta")
    y_b = _outgate_fwd(o, z, p['dn_norm_w'], tag + "_gate")
    y = jnp.concatenate([y_a, y_b], axis=1)
    m = _mm(y, w_out, 'nn', F32, tag + "_out")
    x_out = _resnorm_fwd(x, m, p['mix_norm_post'], 1.0, tag + "_post")
    return x_out, (x, h, u_a, v_a, qkv_raw, z, b_raw, a_raw, qkv, gcol, grow, bcol, states, o, y, m)


def _mix_bwd(dx_out, saved, p, w_in, w_out, conv_w, cs, cp, tag, after=None):
    x, h, u_a, v_a, qkv_raw, z, b_raw, a_raw, qkv, gcol, grow, bcol, states, o, y, m = saved
    t, d = x.shape
    half = d // 2
    nh = half // HEAD_DIM
    gr = {}
    dm, gr['mix_norm_post'] = _resnorm_bwd(m, dx_out, p['mix_norm_post'], 1.0, tag + "_post_b", after)
    dy = _mm(dm, w_out, 'nt', F32, tag + "_out_bx")
    gr['w_out'] = _mm(y, dm, 'tn', BF16, tag + "_out_bw")
    dy_a, dy_b = dy[:, :half], dy[:, half:]
    do, dz, gr['dn_norm_w'] = _outgate_bwd(o, z, p['dn_norm_w'], dy_b, tag + "_gate_b")
    dq, dk, dv, dgcol, dgrow, dbcol = _delta_bwd(qkv, gcol, grow, bcol, states, do, tag + "_delta_b")
    dgcum = dgcol[:, :, 0] + dgrow.reshape(nh, t)
    db_raw, da_raw, gr['a_log'], gr['dt_bias'] = _gates_bwd(b_raw, a_raw, p['a_log'].T, p['dt_bias'].T,
                                                             dbcol[:, :, 0], dgcum, tag + "_gates_b")
    db_raw, da_raw = db_raw.T, da_raw.T
    thirds = [_conv_bwd(qkv_raw, conv_w, dpart, i * half, tag + "_conv_b") for i, dpart in enumerate((dq, dk, dv))]
    gr['conv_w'] = jnp.concatenate([dw for _, dw in thirds], axis=1)
    du_a, dv_a, gr['sm_w'], gr['sm_b'], gr['sm_ln_g'], gr['sm_ln_b'] = _sgu_bwd(
        u_a, v_a, p['sm_w'], p['sm_b'], p['sm_ln_g'], p['sm_ln_b'], dy_a, tag + "_sgu_b")
    dpp = _join_cols([du_a, dv_a, *[dx for dx, _ in thirds], dz, db_raw, da_raw], cs, cp).astype(BF16)
    dh = _mm(dpp, w_in, 'nt', F32, tag + "_in_bx")
    gr['w_in'] = _mm(h, dpp, 'tn', BF16, tag + "_in_bw")
    dx, gr['mix_norm_pre'] = _norm_bwd(x, dh, dx_out, p['mix_norm_pre'], tag + "_pre_b")
    return dx, gr


def _xa_fwd(x, mem, p, w_xq, w_xkv, w_xo, tag, after=None):
    h = _norm_fwd(x, p['xa_norm_pre'], tag + "_pre", after)
    mh = _norm_fwd(mem, p['mem_norm'], tag + "_mem")
    q = _mm(h, w_xq, 'nn', BF16, tag + "_q")
    kv = _mm(mh, w_xkv, 'nn', BF16, tag + "_kv")
    o = _attn_fwd(q, kv, tag + "_attn")
    c = _mm(o, w_xo, 'nn', F32, tag + "_o")
    return _resnorm_fwd(x, c, p['xa_norm_post'], 1.0, tag + "_post"), (x, h, mh, q, kv, o, c)


def _xa_bwd(dx_out, saved, mem, p, w_xq, w_xkv, w_xo, tag, after=None):
    x, h, mh, q, kv, o, c = saved
    gr = {}
    dc, gr['xa_norm_post'] = _resnorm_bwd(c, dx_out, p['xa_norm_post'], 1.0, tag + "_post_b", after)
    do = _mm(dc, w_xo, 'nt', F32, tag + "_o_bx")
    gr['w_xo'] = _mm(o, dc, 'tn', BF16, tag + "_o_bw")
    dq, dkv = _attn_bwd(q, kv, do, tag + "_attn_b")
    dkv = dkv.astype(BF16)
    dh = _mm(dq, w_xq, 'nt', F32, tag + "_q_bx")
    gr['w_xq'] = _mm(h, dq, 'tn', BF16, tag + "_q_bw")
    dmh = _mm(dkv, w_xkv, 'nt', F32, tag + "_kv_bx")
    gr['w_xkv'] = _mm(mh, dkv, 'tn', BF16, tag + "_kv_bw")
    gr['mem_norm'] = _norm_bwd_gain(mem, dmh, p['mem_norm'], tag + "_mem_b")
    dx, gr['xa_norm_pre'] = _norm_bwd(x, dh, dx_out, p['xa_norm_pre'], tag + "_pre_b")
    return dx, gr


def _step(inputs):
    x, mem, target = inputs['x'][0], inputs['mem'][0], inputs['loss_target'][0]
    n_layers = inputs['w_in'].shape[0]
    cx, cy, cc = lax.axis_index("x"), lax.axis_index("y"), lax.axis_index("c")
    chip = 2 * cx + cy
    cs = inputs['w_in'].shape[2]
    cp = -(-cs // LANES) * LANES
    kinds = [BIG[nm] for nm in BIG_NAMES]

    conv_local = inputs['conv_w']
    ccols = conv_local.shape[2]

    core_idx, chip_idx = cc.astype(jnp.int32).reshape(1), chip.astype(jnp.int32).reshape(1)
    where_idx = jnp.stack([chip, cc]).astype(jnp.int32)
    order = [(l, gi) for l in range(n_layers) for gi in range(len(GROUPS))]
    n_groups = len(order)

    def small(l):
        p = {nm: inputs[nm][l] for nm in SMALL_NAMES}
        for nm in SMALL_NAMES:
            if p[nm].ndim == 1:
                p[nm] = _row2(p[nm])
        p['sm_b'] = p['sm_b'][:, :, None]
        return p

    def place(k, after=None):
        l, gi = order[k]
        lands = []
        for nm in GROUPS[gi][1]:
            _, r, c_in = inputs[nm].shape
            shape = _full_shape((r, cp if nm == 'w_in' else c_in), BIG[nm])
            lands.append(_place_block(lax.empty(shape, BF16), w_src[nm], l, BIG[nm], chip_idx, "place_block", after))
        return lands

    w_src = {nm: inputs[nm] for nm in BIG_NAMES}

    placed = {}
    conv_group = 1

    def gather_start(k, after):
        names = list(GROUPS[order[k][1]][1])
        kds = [BIG[nm] for nm in names]
        if k not in placed:
            placed[k] = place(k)
        lands = list(placed[k])
        if k == conv_group:
            mine = conv_local.reshape(1, n_layers * CONV_WIDTH, ccols)
            lands.append(lax.dynamic_update_slice(jnp.zeros((N_CHIPS, *mine.shape[1:]), F32), mine, (chip, 0, 0)))
            names.append('conv_w'), kds.append('all')
        plan, wait_plan = _gather_plan(kds)
        send, recv, _, lands, token = _split_copy_start([], lands, plan, after, f"gather_start_{k}")
        return dict(send=send, recv=recv, lands=lands, token=token, kinds=kds, names=names, wait_plan=wait_plan)

    def gather_finish(k, st, after):
        _, lands = _split_copy_wait(st['send'], st['recv'], [], st['lands'], st['wait_plan'], after, f"gather_wait_{k}")
        got = dict(zip(st['names'], lands))
        halves = [nm for nm, kd in zip(st['names'], st['kinds']) if kd != 'all']
        passed = _gather_forward([got[nm] for nm in halves], [BIG[nm] for nm in halves],
                                 "gather_forward_" + GROUPS[order[k][1]][0])
        w = {nm: (g.reshape(-1, g.shape[2]) if BIG[nm] == 'row' else g) for nm, g in zip(halves, passed)}
        if 'conv_w' in got:
            w['conv_w'] = got['conv_w'].reshape(N_CHIPS, n_layers, CONV_WIDTH, ccols).transpose(1, 2, 0, 3).reshape(
                n_layers, CONV_WIDTH, N_CHIPS * ccols)
        return w

    def group_fwd(gi, x_, p, w, l, token):
        if gi == 0:
            return _ffn_fwd(x_, p['ffn1_norm_pre'], w['ffn1_w_gate_up'], w['ffn1_w_down'], p['ffn1_norm_post'], "ffn", token)
        if gi == 1:
            return _mix_fwd(x_, p, w['w_in'], w['w_out'], conv_full[l], cs, cp, "mix", token)
        if gi == 2:
            return _xa_fwd(x_, mem, p, w['w_xq'], w['w_xkv'], w['w_xo'], "xa", token)
        return _ffn_fwd(x_, p['ffn2_norm_pre'], w['ffn2_w_gate_up'], w['ffn2_w_down'], p['ffn2_norm_post'], "ffn", token)

    def group_bwd(gi, dx_, s, p, w, l, token):
        if gi in (0, 3):
            pre = 'ffn1' if gi == 0 else 'ffn2'
            dx_, g_pre, g_gu, g_d, g_post = _ffn_bwd(dx_, s, p[pre + '_norm_pre'], w[pre + '_w_gate_up'],
                                                     w[pre + '_w_down'], p[pre + '_norm_post'], "ffn", token)
            return dx_, {pre + '_norm_pre': g_pre, pre + '_w_gate_up': g_gu, pre + '_w_down': g_d,
                         pre + '_norm_post': g_post}
        if gi == 1:
            return _mix_bwd(dx_, s, p, w['w_in'], w['w_out'], conv_full[l], cs, cp, "mix", token)
        return _xa_bwd(dx_, s, mem, p, w['w_xq'], w['w_xkv'], w['w_xo'], "xa", token)

    started = {0: gather_start(0, None)}
    w_in_f32 = inputs['w_in'] + started[0]['token'][0, 0]
    w_src['w_in'] = w_in_f32.astype(BF16)
    started[1] = gather_start(1, started[0]['token'])
    weights, saved = [], []
    for k in range(2, n_groups):
        placed[k] = place(k, started[1]['token'])
    head_work = [started[1]['token']] + [land for k in range(2, n_groups) for land in placed[k]]
    head_work += [w_in_f32] + [inputs[pre + 'w_in'].reshape(-1, cs) for pre in ('m_', 'v_')]
    conv_full = None
    for k, (l, gi) in enumerate(order):
        w = gather_finish(k, started[k], x if k > 0 else head_work)
        if k == conv_group:
            conv_full = w.pop('conv_w')
        token = None
        if k + 2 < n_groups:
            started[k + 2] = gather_start(k + 2, next(iter(w.values())))
            token = started[k + 2]['token']
        weights.append(w)
        x, s = group_fwd(gi, x, small(l), w, l, token)
        saved.append(s)

    dx, loss_part = _loss_call(x, target, "loss")

    grads = [dict() for _ in range(n_layers)]
    big_grads = {nm: None for nm in BIG_NAMES}
    pair_gathers = {}

    def reduce_finish(pd, after):
        parts, lands = _split_copy_wait(pd['send'], pd['recv'], pd['parts'], pd['lands'], pd['plan'], after,
                                        f"chip_wait_{pd['k']}")
        for nm, kd, part, landed in zip(pd['names'], pd['kinds'], parts, lands):
            if big_grads[nm] is None:
                big_grads[nm] = lax.empty((n_layers, 2 * landed.shape[1], landed.shape[2]), F32)
            big_grads[nm] = _chip_sum(part, landed, kd, where_idx, big_grads[nm], pd['l'], "chip_sum")
        if pd['l'] == 0 and pd['k'] > 0:
            names = pd['names']
            plan, wait_plan = _pair_gather_plan(len(names), n_layers)
            send, recv, _, got, _ = _split_copy_start([], [big_grads[nm] for nm in names], plan, None,
                                                      f"pair_gather_start_{pd['k']}")
            pair_gathers[pd['k']] = dict(send=send, recv=recv, lands=got, wait_plan=wait_plan)

    def by_chip(gr, names, kds):
        return [gr[nm].reshape(N_CHIPS, -1, gr[nm].shape[1]) if kd == 'row' else gr[nm] for nm, kd in zip(names, kds)]

    def chip_start(k, names, kds, gs, theirs):
        parts = [_pair_add(g, t, core_idx, "pair_add") if kd == 'row'
                 else _pair_add(g[None], t[None], core_idx, "pair_add")[0] for g, t, kd in zip(gs, theirs, kds)]
        lands = [lax.empty((3, *_part_dims(part, kd)), BF16) for part, kd in zip(parts, kds)]
        plan = _chip_plan(kds)
        send, recv, parts, lands, token = _split_copy_start(parts, lands, plan, None, f"chip_start_{k}")
        return dict(send=send, recv=recv, parts=parts, lands=lands, token=token, plan=plan, names=names, kinds=kds,
                    k=k, l=order[k][0])

    pair_pending, chip_queue = None, []
    for k in reversed(range(n_groups)):
        l, gi = order[k]
        names = GROUPS[gi][1]
        kds = [BIG[nm] for nm in names]
        p, w = small(l), weights[k]
        tokens = [pd['token'] for pd in ([pair_pending] if pair_pending else []) + chip_queue] or None
        if k > 0:
            dx_next, gr = group_bwd(gi, dx, saved[k], p, w, l, tokens)
        else:
            dgu, g_gu, g_d, g_post = _ffn_bwd_weights(dx, saved[k], w['ffn1_w_down'], p['ffn1_norm_post'], "ffn", tokens)
            gr = {'ffn1_w_gate_up': g_gu, 'ffn1_w_down': g_d, 'ffn1_norm_post': g_post}
            dx_next = g_gu
        for pd in chip_queue:
            reduce_finish(pd, dx_next)
        chip_queue = []
        if pair_pending:
            pd = pair_pending
            gs_prev, theirs = _split_copy_wait(pd['send'], pd['recv'], pd['gs'], pd['lands'], pd['plan'], dx_next,
                                               f"pair_wait_{pd['k']}")
            chip_queue.append(chip_start(pd['k'], pd['names'], pd['kinds'], gs_prev, theirs))
            pair_pending = None
        gs = by_chip(gr, names, kds)
        if k > 1:
            plan = _pair_plan(kds)
            lands = [lax.empty(_half_shape(g, kd), BF16) for g, kd in zip(gs, kds)]
            send, recv, gs, lands, token = _split_copy_start(gs, lands, plan, None, f"pair_start_{k}")
            pair_pending = dict(send=send, recv=recv, gs=gs, lands=lands, token=token, plan=plan, names=names,
                                kinds=kds, k=k)
        else:
            started_now = chip_start(k, names, kds, gs, _pair_exchange(gs, kds, "pair_exchange_" + GROUPS[gi][0]))
            if k > 0:
                chip_queue.append(started_now)
        if k > 0:
            dx = dx_next
        else:
            pending = started_now
            dx, gr['ffn1_norm_pre'] = _ffn_bwd_input(dx, saved[k], dgu, p['ffn1_norm_pre'], w['ffn1_w_gate_up'], "ffn",
                                                     pending['token'])
        grads[l].update({nm: g for nm, g in gr.items() if nm not in BIG})
    for pd in chip_queue:
        reduce_finish(pd, dx)
    out = {}

    def two(a):
        return a.reshape(-1, a.shape[-1])

    def adamw(nm, g):
        w = w_in_f32 if nm == 'w_in' else inputs[nm]
        res = _adamw_call(two(w), two(g), two(inputs['m_' + nm]), two(inputs['v_' + nm]), "adamw")
        out[nm] = tuple(a.reshape(inputs[nm].shape) for a in res)
        return res[1]

    def finish_weights(gi, after=None):
        names = GROUPS[gi][1]
        if gi in pair_gathers:
            st = pair_gathers[gi]
            _, got = _split_copy_wait(st['send'], st['recv'], [], st['lands'], st['wait_plan'], after,
                                      f"pair_gather_wait_{gi}")
        else:
            got = _pair_gather([big_grads[nm] for nm in names], "pair_gather_" + GROUPS[gi][0], after)
        return [adamw(nm, g) for nm, g in zip(names, got)]

    small_grads = [jnp.stack([grads[l][nm].reshape(inputs[nm].shape[1:]) if nm != 'conv_w' else grads[l][nm]
                              for l in range(n_layers)]) for nm in SMALL_NAMES]
    packed = _pack(small_grads + [loss_part])
    me = 4 * cx + 2 * cy + cc
    land = lax.dynamic_update_slice(lax.empty((N_DEV, *packed.shape), F32), packed[None], (me, 0, 0))
    plan, wait_plan = _all_gather_plan()
    send, recv, bufs, lands, _ = _split_copy_start([packed], [land], plan, pending['token'], "small_start")

    done = [d for gi in range(len(GROUPS) - 1, 0, -1) for d in finish_weights(gi, pending['token'])]
    reduce_finish(pending, done)
    done = finish_weights(0)

    _, lands = _split_copy_wait(send, recv, bufs, lands, wait_plan, done, "small_wait")
    summed = _unpack(_sum_slots(lands[0], "small_sum"), small_grads + [loss_part])
    loss = summed[-1][0, 0]
    small_g = dict(zip(SMALL_NAMES, summed[:-1]))
    small_g['conv_w'] = lax.dynamic_slice(small_g['conv_w'], (0, 0, chip * ccols), (n_layers, CONV_WIDTH, ccols))
    for nm in SMALL_NAMES:
        adamw(nm, small_g[nm])
    return (loss, dx[None], *[out[nm][0] for nm in WEIGHTS], *[out[nm][1] for nm in WEIGHTS],
            *[out[nm][2] for nm in WEIGHTS], *[out[nm][3] for nm in WEIGHTS])


def kernel(x, mem, ffn1_norm_pre, ffn1_w_gate_up, ffn1_w_down, ffn1_norm_post, mix_norm_pre, w_in, conv_w, a_log, dt_bias, sm_w, sm_b, sm_ln_g, sm_ln_b, dn_norm_w, w_out, mix_norm_post, xa_norm_pre, mem_norm, w_xq, w_xkv, w_xo, xa_norm_post, ffn2_norm_pre, ffn2_w_gate_up, ffn2_w_down, ffn2_norm_post, loss_target, m_ffn1_norm_pre, m_ffn1_w_gate_up, m_ffn1_w_down, m_ffn1_norm_post, m_mix_norm_pre, m_w_in, m_conv_w, m_a_log, m_dt_bias, m_sm_w, m_sm_b, m_sm_ln_g, m_sm_ln_b, m_dn_norm_w, m_w_out, m_mix_norm_post, m_xa_norm_pre, m_mem_norm, m_w_xq, m_w_xkv, m_w_xo, m_xa_norm_post, m_ffn2_norm_pre, m_ffn2_w_gate_up, m_ffn2_w_down, m_ffn2_norm_post, v_ffn1_norm_pre, v_ffn1_w_gate_up, v_ffn1_w_down, v_ffn1_norm_post, v_mix_norm_pre, v_w_in, v_conv_w, v_a_log, v_dt_bias, v_sm_w, v_sm_b, v_sm_ln_g, v_sm_ln_b, v_dn_norm_w, v_w_out, v_mix_norm_post, v_xa_norm_pre, v_mem_norm, v_w_xq, v_w_xkv, v_w_xo, v_xa_norm_post, v_ffn2_norm_pre, v_ffn2_w_gate_up, v_ffn2_w_down, v_ffn2_norm_post):
    vals = dict(locals())
    return _step(vals)
```

```python
import functools
import math

import jax
import jax.numpy as jnp
from jax import lax
from jax.experimental import pallas as pl
from jax.experimental.pallas import tpu as pltpu

F32, BF16 = jnp.float32, jnp.bfloat16
NORM_EPS = 1e-6
HEAD_DIM = 128
GM_CHUNK = 128
DN_CHUNK = 64
CONV_WIDTH = 4
XA_HEADS = 4
LANES = 128
MXU_WIDTH = 256
N_CHIPS = 4
N_DEV = 8
VMEM_LIMIT = 56 * 1024 * 1024
MESH_IDS = pl.DeviceIdType.MESH
ADAM_LR, ADAM_B1, ADAM_B2, ADAM_EPS, ADAM_WD, ADAM_STEP = 0.001, 0.9, 0.999, 1e-08, 0.01, 10

WEIGHTS = ['ffn1_norm_pre', 'ffn1_w_gate_up', 'ffn1_w_down', 'ffn1_norm_post', 'mix_norm_pre', 'w_in', 'conv_w',
           'a_log', 'dt_bias', 'sm_w', 'sm_b', 'sm_ln_g', 'sm_ln_b', 'dn_norm_w', 'w_out', 'mix_norm_post',
           'xa_norm_pre', 'mem_norm', 'w_xq', 'w_xkv', 'w_xo', 'xa_norm_post', 'ffn2_norm_pre', 'ffn2_w_gate_up',
           'ffn2_w_down', 'ffn2_norm_post']
BIG = {'ffn1_w_gate_up': 'col', 'ffn1_w_down': 'row', 'w_in': 'col', 'w_out': 'row', 'w_xq': 'row',
       'w_xkv': 'col', 'w_xo': 'row', 'ffn2_w_gate_up': 'col', 'ffn2_w_down': 'row'}
BIG_NAMES = list(BIG)
GROUPS = (('ffn1', ('ffn1_w_gate_up', 'ffn1_w_down')), ('mix', ('w_in', 'w_out')),
          ('xa', ('w_xq', 'w_xkv', 'w_xo')), ('ffn2', ('ffn2_w_gate_up', 'ffn2_w_down')))
SMALL_NAMES = [n for n in WEIGHTS if n not in BIG]


def _pcall(body, **kw):
    return pl.pallas_call(body, **kw)


def _params(sem=None):
    return pltpu.CompilerParams(dimension_semantics=sem, vmem_limit_bytes=VMEM_LIMIT)


def _as_list(after):
    if after is None:
        return []
    return list(after) if isinstance(after, (list, tuple)) else [after]


def _divtile(dim, cap, align=LANES):
    if dim <= cap:
        return dim
    t = (cap // align) * align
    while t > align and dim % t:
        t -= align
    assert dim % t == 0, (dim, cap)
    return t


_DN = {'nn': (((1,), (0,)), ((), ())), 'nt': (((1,), (1,)), ((), ())), 'tn': (((0,), (0,)), ((), ()))}


MM_MIN_STEPS = 8
MM_VMEM_BUDGET = 44 * 1024 * 1024


def _mm_tiles(m, n, k, out_itemsize, acc=False):
    def tile(dim, cap):
        return _divtile(dim, cap, MXU_WIDTH) if dim % MXU_WIDTH == 0 else _divtile(dim, cap)

    tm = tile(m, 1024)
    deep = ((1024, 2816),) if _divtile(k, 2816) > _divtile(k, 2048) else ()
    for cap_n, cap_k in deep + ((2048, 2048), (1024, 2048), (2048, 1024), (1024, 1024), (1024, 512), (512, 512)):
        tn, tk = tile(n, cap_n), _divtile(k, cap_k)
        need = 2 * 2 * (tm * tk + tk * tn) + 2 * tm * tn * out_itemsize + (tm * tn * 4 if tk < k or acc else 0)
        if need <= MM_VMEM_BUDGET:
            break
    while (m // tm) * (n // tn) * (k // tk) < MM_MIN_STEPS and tn > 2 * MXU_WIDTH and n % (tn // 2) == 0:
        tn //= 2
    return tm, tn, tk


def _mm(a, b, mode, out_dtype, name, halves=None):
    if halves == 'a':
        assert mode == 'nt'
        a_shape, b_shape = (a.shape[1], 2 * a.shape[2]), b.shape
    elif halves == 'b':
        assert mode == 'tn'
        a_shape, b_shape = a.shape, (b.shape[1], 2 * b.shape[2])
    else:
        a_shape, b_shape = a.shape, b.shape
    if mode == 'nn':
        (m, k), (k2, n) = a_shape, b_shape
    elif mode == 'nt':
        (m, k), (n, k2) = a_shape, b_shape
    else:
        (k, m), (k2, n) = a_shape, b_shape
    assert k == k2, (a.shape, b.shape, mode)
    if halves == 'a':
        tm, tn, tk = _mm_tiles(m, n, k // 2, jnp.dtype(out_dtype).itemsize, acc=True)
    elif halves == 'b':
        tm, tn, tk = _mm_tiles(m, n // 2, k, jnp.dtype(out_dtype).itemsize)
    else:
        tm, tn, tk = _mm_tiles(m, n, k, jnp.dtype(out_dtype).itemsize)
    nk = k // tk
    a_spec = {'nn': pl.BlockSpec((tm, tk), lambda i, j, kk: (i, kk)),
              'nt': pl.BlockSpec((tm, tk), lambda i, j, kk: (i, kk)),
              'tn': pl.BlockSpec((tk, tm), lambda i, j, kk: (kk, i))}[mode]
    b_spec = {'nn': pl.BlockSpec((tk, tn), lambda i, j, kk: (kk, j)),
              'nt': pl.BlockSpec((tn, tk), lambda i, j, kk: (j, kk)),
              'tn': pl.BlockSpec((tk, tn), lambda i, j, kk: (kk, j))}[mode]
    if halves == 'a':
        per = nk // 2
        a_spec = pl.BlockSpec((None, tm, tk), lambda i, j, kk: (kk // per, i, kk % per))
    elif halves == 'b':
        per = (n // tn) // 2
        b_spec = pl.BlockSpec((None, tk, tn), lambda i, j, kk: (j // per, kk, j % per))

    def dot(a_ref, b_ref):
        return lax.dot_general(a_ref[...].astype(BF16), b_ref[...].astype(BF16), _DN[mode], preferred_element_type=F32)

    def body_once(a_ref, b_ref, o_ref):
        o_ref[...] = dot(a_ref, b_ref).astype(o_ref.dtype)

    def body_steps(a_ref, b_ref, o_ref, acc):
        kk = pl.program_id(2)

        @pl.when(kk == 0)
        def _():
            acc[...] = jnp.zeros_like(acc)

        acc[...] += dot(a_ref, b_ref)

        @pl.when(kk == nk - 1)
        def _():
            o_ref[...] = acc[...].astype(o_ref.dtype)

    return _pcall(
        body_once if nk == 1 else body_steps, name=name, grid=(m // tm, n // tn, nk),
        in_specs=[a_spec, b_spec], out_specs=pl.BlockSpec((tm, tn), lambda i, j, kk: (i, j)),
        out_shape=jax.ShapeDtypeStruct((m, n), out_dtype),
        scratch_shapes=[] if nk == 1 else [pltpu.VMEM((tm, tn), F32)],
        compiler_params=_params(("parallel", "parallel", "arbitrary")),
    )(a, b)


def _rowcall(name, f, rows, params=(), row_outs=(), acc_outs=(), br=256, nrows=None, after=None):
    rows = [r if isinstance(r, tuple) else (r, 0) for r in rows]
    t = nrows if nrows is not None else rows[0][0].shape[0]
    br = min(br, t)
    assert t % br == 0, (name, t, br)
    n_in, n_ro = len(rows) + len(params), len(row_outs)
    extra = _as_list(after)

    def row_spec(arr, off):
        assert off % br == 0
        return pl.BlockSpec((br, arr.shape[1]), functools.partial(lambda i, o: (i + o, 0), o=off // br))

    def whole_spec(shape):
        return pl.BlockSpec(shape, functools.partial(lambda i, nd: (0,) * nd, nd=len(shape)))

    def body(*refs):
        res = f(*[r[...] for r in refs[:n_in]])
        outs = refs[n_in + len(extra):]
        for o_ref, val in zip(outs[:n_ro], res[:n_ro]):
            o_ref[...] = val.astype(o_ref.dtype)
        if acc_outs:
            @pl.when(pl.program_id(0) == 0)
            def _():
                for o_ref in outs[n_ro:]:
                    o_ref[...] = jnp.zeros_like(o_ref)

            for o_ref, val in zip(outs[n_ro:], res[n_ro:]):
                o_ref[...] += val.astype(F32)

    out = _pcall(
        body, name=name, grid=(t // br,),
        in_specs=[row_spec(a, o) for a, o in rows] + [whole_spec(p.shape) for p in params]
        + [pl.BlockSpec(memory_space=pl.ANY)] * len(extra),
        out_specs=[pl.BlockSpec((br, c), lambda i: (i, 0)) for c, _ in row_outs] + [whole_spec(s) for s in acc_outs],
        out_shape=[jax.ShapeDtypeStruct((t, c), dt) for c, dt in row_outs]
        + [jax.ShapeDtypeStruct(s, F32) for s in acc_outs],
        compiler_params=_params(("arbitrary",) if acc_outs else ("parallel",)),
    )(*[a for a, _ in rows], *params, *extra)
    return out


_DN_BATCH = {'nn': (((2,), (1,)), ((0,), (0,))), 'nt': (((2,), (2,)), ((0,), (0,))), 'tn': (((1,), (1,)), ((0,), (0,)))}


def _dims(a, dn):
    return _DN_BATCH[dn] if a.ndim == 3 else _DN[dn]


def _bdot_raw(a, b, dn):
    return lax.dot_general(a.astype(BF16), b.astype(BF16), _dims(a, dn), preferred_element_type=F32)


def _hdot_raw(a, b, dn):
    a_hi, b_hi = a.astype(BF16), b.astype(BF16)
    a_lo, b_lo = (a - a_hi.astype(F32)).astype(BF16), (b - b_hi.astype(F32)).astype(BF16)

    def dot(p, q):
        return lax.dot_general(p, q, _dims(a, dn), preferred_element_type=F32)

    return dot(a_hi, b_hi) + (dot(a_hi, b_lo) + dot(a_lo, b_hi))


def _with_vjp(raw):
    @functools.partial(jax.custom_vjp, nondiff_argnums=(2,))
    def dot(a, b, dn):
        return raw(a, b, dn)

    def fwd(a, b, dn):
        return raw(a, b, dn), (a, b)

    def bwd(dn, res, ct):
        a, b = res
        if dn == 'nn':
            da, db = raw(ct, b, 'nt'), raw(a, ct, 'tn')
        elif dn == 'nt':
            da, db = raw(ct, b, 'nn'), raw(ct, a, 'tn')
        else:
            da, db = raw(b, ct, 'nt'), raw(a, ct, 'nn')
        return da.astype(a.dtype), db.astype(b.dtype)

    dot.defvjp(fwd, bwd)
    return dot


_bdot, _hdot = _with_vjp(_bdot_raw), _with_vjp(_hdot_raw)


def _rms(x, g):
    return x * lax.rsqrt(jnp.mean(x * x, axis=-1, keepdims=True) + NORM_EPS) * g


def _sigmoid(x):
    return 1.0 / (1.0 + jnp.exp(-x))


def _silu(x):
    return x * _sigmoid(x)


def _gelu(x):
    return 0.5 * x * (1.0 + lax.erf(x * (2.0 ** -0.5)))


def _norm_fwd(x, g, name, after=None):
    return _rowcall(name, lambda x_, g_: (_rms(x_, g_),), [x], [g], [(x.shape[1], BF16)], after=after)[0]


def _resnorm_fwd(x, f, g, alpha, name):
    return _rowcall(name, lambda x_, f_, g_: (x_ + alpha * _rms(f_, g_),), [x, f], [g], [(x.shape[1], F32)])[0]


def _resnorm_bwd(f, dx, g, alpha, name, after=None):
    def fn(f_, dx_, g_):
        _, vjp = jax.vjp(lambda a, b: alpha * _rms(a, b), f_, g_)
        return vjp(dx_)

    return _rowcall(name, fn, [f, dx], [g], [(f.shape[1], BF16)], [g.shape], after=after)


def _norm_bwd(x, dh, dx_out, g, name, after=None):
    def fn(x_, dh_, dxo_, g_):
        _, vjp = jax.vjp(_rms, x_, g_)
        dx, dg = vjp(dh_)
        return dxo_ + dx, dg

    return _rowcall(name, fn, [x, dh, dx_out], [g], [(x.shape[1], F32)], [g.shape], after=after)


def _norm_bwd_gain(x, dh, g, name):
    def fn(x_, dh_, g_):
        _, vjp = jax.vjp(lambda b: _rms(x_, b), g_)
        return vjp(dh_)

    return _rowcall(name, fn, [x, dh], [g], [], [g.shape])[0]


def _ffn_up(h, w_gu, name):
    t, d = h.shape
    ff = w_gu.shape[1] // 2
    tm, tn = _divtile(t, 1024), _divtile(ff, 512, MXU_WIDTH)
    nj = ff // tn

    def body(h_ref, wg_ref, wu_ref, gu_ref, a_ref):
        hv = h_ref[...]
        gate = jnp.dot(hv, wg_ref[...], preferred_element_type=F32)
        up = jnp.dot(hv, wu_ref[...], preferred_element_type=F32)
        gu_ref[0], gu_ref[1] = gate, up
        a_ref[...] = (_silu(gate) * up).astype(a_ref.dtype)

    return _pcall(body, name=name, grid=(t // tm, nj),
                  in_specs=[pl.BlockSpec((tm, d), lambda i, j: (i, 0)), pl.BlockSpec((d, tn), lambda i, j: (0, j)),
                            pl.BlockSpec((d, tn), lambda i, j: (0, j + nj))],
                  out_specs=[pl.BlockSpec((2, tm, tn), lambda i, j: (0, i, j)), pl.BlockSpec((tm, tn), lambda i, j: (i, j))],
                  out_shape=[jax.ShapeDtypeStruct((2, t, ff), F32), jax.ShapeDtypeStruct((t, ff), BF16)],
                  compiler_params=_params(("parallel", "parallel")))(h, w_gu, w_gu)


def _ffn_down_bwd(df, w_d, gu, name):
    t, d = df.shape
    ff = w_d.shape[0]
    tm, tn = _divtile(t, 1024), _divtile(ff, 512, MXU_WIDTH)

    def body(df_ref, w_ref, gu_ref, o_ref):
        da = lax.dot_general(df_ref[...], w_ref[...], _DN['nt'], preferred_element_type=F32)
        gate, up = gu_ref[0], gu_ref[1]
        s = _sigmoid(gate)
        o_ref[0] = (da * up * (s * (1.0 + gate * (1.0 - s)))).astype(o_ref.dtype)
        o_ref[1] = (da * (gate * s)).astype(o_ref.dtype)

    blk = pl.BlockSpec((2, tm, tn), lambda i, j: (0, i, j))
    return _pcall(body, name=name, grid=(t // tm, ff // tn),
                  in_specs=[pl.BlockSpec((tm, d), lambda i, j: (i, 0)), pl.BlockSpec((tn, d), lambda i, j: (j, 0)), blk],
                  out_specs=blk, out_shape=jax.ShapeDtypeStruct((2, t, ff), BF16),
                  compiler_params=_params(("parallel", "parallel")))(df, w_d, gu)


def _sgu_norm(v, lg, lb):
    vf = _gelu(v)
    mu = jnp.mean(vf, axis=-1, keepdims=True)
    var = jnp.mean(jnp.square(vf - mu), axis=-1, keepdims=True)
    return (vf - mu) * lax.rsqrt(var + NORM_EPS) * lg + lb


def _sgu_head(dot, u_h, vn_h, w_h, b_h):
    r = lax.broadcasted_iota(jnp.int32, w_h.shape, 0)
    c = lax.broadcasted_iota(jnp.int32, w_h.shape, 1)
    mixed = dot(jnp.where(r >= c, w_h, 0.0), vn_h, 'nn') + b_h
    return _gelu(u_h) * mixed


def _heads(width):
    return [slice(h * HEAD_DIM, (h + 1) * HEAD_DIM) for h in range(width // HEAD_DIM)]


def _sgu_fwd(u, v, w, bcol, lg, lb, name):
    def fn(u_, v_, w_, b_, lg_, lb_):
        vn = _sgu_norm(v_, lg_, lb_)
        return (jnp.concatenate([_sgu_head(_bdot_raw, u_[:, s], vn[:, s], w_[h], b_[h])
                                 for h, s in enumerate(_heads(u_.shape[1]))], axis=1),)

    return _rowcall(name, fn, [u, v], [w, bcol, lg, lb], [(u.shape[1], BF16)], br=GM_CHUNK)[0]


def _sgu_bwd(u, v, w, bcol, lg, lb, dy, name):
    def fn(u_, v_, dy_, w_, b_, lg_, lb_):
        vn, vjp_norm = jax.vjp(_sgu_norm, v_, lg_, lb_)
        du, dvn, dw, db = [], [], [], []
        for h, s in enumerate(_heads(u_.shape[1])):
            _, vjp = jax.vjp(functools.partial(_sgu_head, _bdot), u_[:, s], vn[:, s], w_[h], b_[h])
            a, b, c, d = vjp(dy_[:, s])
            du.append(a), dvn.append(b), dw.append(c[None]), db.append(d[None])
        dv, dlg, dlb = vjp_norm(jnp.concatenate(dvn, axis=1))
        return (jnp.concatenate(du, axis=1), dv, jnp.concatenate(dw, axis=0), jnp.concatenate(db, axis=0), dlg, dlb)

    wd = u.shape[1]
    return _rowcall(name, fn, [u, v, dy], [w, bcol, lg, lb], [(wd, BF16), (wd, BF16)],
                    [w.shape, bcol.shape, lg.shape, lb.shape], br=GM_CHUNK)


def _shift_down(x, s):
    if s == 0:
        return x
    rows = lax.broadcasted_iota(jnp.int32, x.shape, 0)
    return jnp.where(rows >= s, pltpu.roll(x, s, 0), 0.0)


def _shift_up(x, s):
    if s == 0:
        return x
    t = x.shape[0]
    rows = lax.broadcasted_iota(jnp.int32, x.shape, 0)
    return jnp.where(rows < t - s, pltpu.roll(x, t - s, 0), 0.0)


def _conv_pre(x, taps):
    acc = None
    for i in range(CONV_WIDTH):
        term = _shift_down(x, CONV_WIDTH - 1 - i) * taps[i]
        acc = term if acc is None else acc + term
    return acc


def _taps(w_ref):
    return [w_ref[i:i + 1, :] for i in range(CONV_WIDTH)]


def _conv_fwd(x, w, name):
    t, ch = x.shape
    cb = _divtile(ch, 512)

    def body(x_ref, w_ref, o_ref):
        o_ref[...] = _silu(_conv_pre(x_ref[...], _taps(w_ref)))

    return _pcall(body, name=name, grid=(ch // cb,),
                  in_specs=[pl.BlockSpec((t, cb), lambda j: (0, j)), pl.BlockSpec((CONV_WIDTH, cb), lambda j: (0, j))],
                  out_specs=pl.BlockSpec((t, cb), lambda j: (0, j)),
                  out_shape=jax.ShapeDtypeStruct((t, ch), F32), compiler_params=_params(("parallel",)))(x, w)


def _conv_bwd(x, w, dout, col_off, name):
    t, ch = dout.shape
    cb = _divtile(ch, 256)
    assert col_off % cb == 0
    off = col_off // cb

    def body(x_ref, w_ref, do_ref, dx_ref, dw_ref):
        xv, taps = x_ref[...], _taps(w_ref)
        pre = _conv_pre(xv, taps)
        s = _sigmoid(pre)
        dpre = do_ref[...] * (s * (1.0 + pre * (1.0 - s)))
        dx = None
        for i in range(CONV_WIDTH):
            sh = CONV_WIDTH - 1 - i
            term = _shift_up(dpre, sh) * taps[i]
            dx = term if dx is None else dx + term
            dw_ref[i:i + 1, :] = jnp.sum(dpre * _shift_down(xv, sh), axis=0, keepdims=True)
        dx_ref[...] = dx.astype(dx_ref.dtype)

    return _pcall(body, name=name, grid=(ch // cb,),
                  in_specs=[pl.BlockSpec((t, cb), lambda j: (0, j + off)),
                            pl.BlockSpec((CONV_WIDTH, cb), lambda j: (0, j + off)),
                            pl.BlockSpec((t, cb), lambda j: (0, j))],
                  out_specs=[pl.BlockSpec((t, cb), lambda j: (0, j)), pl.BlockSpec((CONV_WIDTH, cb), lambda j: (0, j))],
                  out_shape=[jax.ShapeDtypeStruct((t, ch), BF16), jax.ShapeDtypeStruct((CONV_WIDTH, ch), F32)],
                  compiler_params=_params(("parallel",)))(x, w, dout)


GATE_BLOCK = 256


def _gates(dot, b_raw, a_raw, a_log, dt_bias):
    blk = b_raw.shape[1]
    z = a_raw + dt_bias
    softplus = jnp.maximum(z, 0.0) + jnp.log(1.0 + jnp.exp(-jnp.abs(z)))
    g = -jnp.exp(a_log) * softplus
    ri = lax.broadcasted_iota(jnp.int32, (blk, blk), 0)
    ci = lax.broadcasted_iota(jnp.int32, (blk, blk), 1)
    upto = ((ri <= ci) & (ri // DN_CHUNK == ci // DN_CHUNK)).astype(F32)
    return _sigmoid(b_raw), dot(g, upto, 'nn')


def _gate_blocks(t):
    blk = min(GATE_BLOCK, t)
    return [slice(i, i + blk) for i in range(0, t, blk)]


def _whole_call(name, f, ins, out_shapes):
    n_in = len(ins)

    def body(*refs):
        for o_ref, val in zip(refs[n_in:], f(*[r[...] for r in refs[:n_in]])):
            o_ref[...] = val

    vmem = pl.BlockSpec(memory_space=pltpu.VMEM)
    return _pcall(body, name=name, in_specs=[vmem] * n_in, out_specs=[vmem] * len(out_shapes),
                  out_shape=[jax.ShapeDtypeStruct(s, F32) for s in out_shapes],
                  compiler_params=pltpu.CompilerParams(vmem_limit_bytes=VMEM_LIMIT))(*ins)


def _gates_fwd(b_raw, a_raw, a_log, dt_bias, name):
    def fn(b_, a_, al_, dt_):
        parts = [_gates(_hdot_raw, b_[:, s], a_[:, s], al_, dt_) for s in _gate_blocks(b_.shape[1])]
        return [jnp.concatenate(p, axis=1) for p in zip(*parts)]

    return _whole_call(name, fn, [b_raw, a_raw, a_log, dt_bias], [b_raw.shape, b_raw.shape])


def _gates_bwd(b_raw, a_raw, a_log, dt_bias, dbeta, dgcum, name):
    def fn(b_, a_, al_, dt_, dbeta_, dgcum_):
        db, da, dal, ddt = [], [], None, None
        for s in _gate_blocks(b_.shape[1]):
            _, vjp = jax.vjp(functools.partial(_gates, _hdot), b_[:, s], a_[:, s], al_, dt_)
            p, q, r, u = vjp((dbeta_[:, s], dgcum_[:, s]))
            db.append(p), da.append(q)
            dal, ddt = (r, u) if dal is None else (dal + r, ddt + u)
        return jnp.concatenate(db, axis=1), jnp.concatenate(da, axis=1), dal, ddt

    return _whole_call(name, fn, [b_raw, a_raw, a_log, dt_bias, dbeta, dgcum],
                       [b_raw.shape, a_raw.shape, a_log.shape, dt_bias.shape])


def _unit_lower_inverse_raw(a):
    c = a.shape[-1]
    eye = (lax.broadcasted_iota(jnp.int32, (c, c), 0) == lax.broadcasted_iota(jnp.int32, (c, c), 1)).astype(F32)
    inv, p = eye - a, _hdot_raw(a, a, 'nn')
    n_fac = int(math.log2(c)) - 1
    for it in range(n_fac):
        inv = inv + _hdot_raw(inv, p, 'nn')
        if it < n_fac - 1:
            p = _hdot_raw(p, p, 'nn')
    return inv


@jax.custom_vjp
def _unit_lower_inverse(a):
    return _unit_lower_inverse_raw(a)


def _unit_lower_inverse_fwd(a):
    inv = _unit_lower_inverse_raw(a)
    return inv, inv


def _unit_lower_inverse_bwd(inv, ct):
    return (-_hdot_raw(_hdot_raw(inv, ct, 'tn'), inv, 'nt'),)


_unit_lower_inverse.defvjp(_unit_lower_inverse_fwd, _unit_lower_inverse_bwd)


def _halves_raw(x):
    return x[..., :x.shape[-1] // 2], x[..., x.shape[-1] // 2:]


@jax.custom_vjp
def _halves(x):
    return _halves_raw(x)


_halves.defvjp(lambda x: (_halves_raw(x), None), lambda _, ct: (jnp.concatenate(ct, axis=-1),))

_DELTA_OPS = (_bdot_raw, _hdot_raw, _unit_lower_inverse_raw, _halves_raw)
_DELTA_OPS_VJP = (_bdot, _hdot, _unit_lower_inverse, _halves)


def _delta_chunk(ops, state, q, k, v, gc, gr, bc):
    bd, hd, inverse, halves = ops
    c, d = q.shape[-2:]
    ri = lax.broadcasted_iota(jnp.int32, (c, c), 0)
    ci = lax.broadcasted_iota(jnp.int32, (c, c), 1)
    causal, strict = ri >= ci, ri > ci
    qn = q * lax.rsqrt(jnp.sum(q * q, axis=-1, keepdims=True) + NORM_EPS) * (d ** -0.5)
    kn = k * lax.rsqrt(jnp.sum(k * k, axis=-1, keepdims=True) + NORM_EPS)
    gcum = jnp.broadcast_to(gc, q.shape)
    decay = jnp.where(causal, jnp.exp(jnp.where(causal, gc - gr, 0.0)), 0.0)
    bb = jnp.broadcast_to(bc, q.shape)
    k_beta = kn * bb
    inv = inverse(jnp.where(strict, bd(k_beta, kn, 'nt') * decay, 0.0))
    uw = hd(inv, jnp.concatenate([v * bb, k_beta * jnp.exp(gcum)], axis=-1), 'nn')
    u, w = halves(uw)
    qk = jnp.where(causal, bd(qn, kn, 'nt') * decay, 0.0)
    v_new = u - bd(w, state, 'nn')
    o = bd(qn * jnp.exp(gcum), state, 'nn') + bd(qk, v_new, 'nn')
    last = lax.broadcasted_iota(jnp.int32, (c, d), 0) == c - 1
    g_last = jnp.sum(jnp.where(last, gcum, 0.0), axis=-2, keepdims=True)
    k_dec = kn * jnp.exp(g_last - gcum)
    return state * jnp.exp(g_last) + bd(k_dec, v_new, 'tn'), o


DN_HEADS_PER_STEP = 8


def _by_head(ref):
    return jnp.stack([ref[:, s] for s in _heads(ref.shape[1])])


def _delta_specs(nh, nc, rev):
    c, d = DN_CHUNK, HEAD_DIM
    hb = min(DN_HEADS_PER_STEP, nh)
    ng = nh // hb
    ch = (lambda n: nc - 1 - n) if rev else (lambda n: n)
    qkv = [pl.BlockSpec((c, hb * d), functools.partial(lambda h, n, o: (ch(n), o * ng + h), o=o)) for o in range(3)]
    col = pl.BlockSpec((hb, c, 1), lambda h, n: (h, ch(n), 0))
    row = pl.BlockSpec((hb, None, 1, c), lambda h, n: (h, ch(n), 0, 0))
    st = pl.BlockSpec((hb, None, d, d), lambda h, n: (h, ch(n), 0, 0))
    tok = pl.BlockSpec((c, hb * d), lambda h, n: (ch(n), h))
    return hb, ng, qkv, col, row, st, tok


def _delta_fwd(qkv, gcol, grow, bcol, name):
    t = qkv.shape[0]
    nh, nc, d = gcol.shape[0], t // DN_CHUNK, HEAD_DIM
    hb, ng, qkv_s, col, row, st, tok = _delta_specs(nh, nc, False)

    def body(q_ref, k_ref, v_ref, gc_ref, gr_ref, bc_ref, o_ref, st_ref, state):
        @pl.when(pl.program_id(1) == 0)
        def _():
            state[...] = jnp.zeros_like(state)

        s0 = state[...]
        st_ref[...] = s0
        s1, o = _delta_chunk(_DELTA_OPS, s0, _by_head(q_ref), _by_head(k_ref), _by_head(v_ref), gc_ref[...],
                             gr_ref[...], bc_ref[...])
        for j, s in enumerate(_heads(hb * d)):
            o_ref[:, s] = o[j]
        state[...] = s1

    return _pcall(body, name=name, grid=(ng, nc), in_specs=qkv_s + [col, row, col], out_specs=[tok, st],
                  out_shape=[jax.ShapeDtypeStruct((t, nh * d), F32), jax.ShapeDtypeStruct((nh, nc, d, d), F32)],
                  scratch_shapes=[pltpu.VMEM((hb, d, d), F32)],
                  compiler_params=_params(("parallel", "arbitrary")))(qkv, qkv, qkv, gcol, grow, bcol)


def _delta_bwd(qkv, gcol, grow, bcol, states, do, name):
    t = qkv.shape[0]
    nh, nc, d = gcol.shape[0], t // DN_CHUNK, HEAD_DIM
    hb, ng, qkv_s, col, row, st, tok = _delta_specs(nh, nc, True)

    def body(q_ref, k_ref, v_ref, gc_ref, gr_ref, bc_ref, st_ref, do_ref,
             dq_ref, dk_ref, dv_ref, dgc_ref, dgr_ref, dbc_ref, dstate):
        @pl.when(pl.program_id(1) == 0)
        def _():
            dstate[...] = jnp.zeros_like(dstate)

        _, vjp = jax.vjp(functools.partial(_delta_chunk, _DELTA_OPS_VJP), st_ref[...], _by_head(q_ref), _by_head(k_ref),
                         _by_head(v_ref), gc_ref[...], gr_ref[...], bc_ref[...])
        ds, dq, dk, dv, dgc, dgr, dbc = vjp((dstate[...], _by_head(do_ref)))
        for j, s in enumerate(_heads(hb * d)):
            dq_ref[:, s], dk_ref[:, s], dv_ref[:, s] = dq[j], dk[j], dv[j]
        dgc_ref[...], dgr_ref[...], dbc_ref[...] = dgc, dgr, dbc
        dstate[...] = ds

    tok_out = jax.ShapeDtypeStruct((t, nh * d), F32)
    return _pcall(body, name=name, grid=(ng, nc), in_specs=qkv_s + [col, row, col, st, tok],
                  out_specs=[tok, tok, tok, col, row, col],
                  out_shape=[tok_out, tok_out, tok_out, jax.ShapeDtypeStruct(gcol.shape, F32),
                             jax.ShapeDtypeStruct(grow.shape, F32), jax.ShapeDtypeStruct(bcol.shape, F32)],
                  scratch_shapes=[pltpu.VMEM((hb, d, d), F32)],
                  compiler_params=_params(("parallel", "arbitrary")))(qkv, qkv, qkv, gcol, grow, bcol, states, do)


def _outgate_head(o_h, z_h, w):
    return _rms(o_h, w) * _silu(z_h)


def _outgate_fwd(o, z, w, name):
    def fn(o_, z_, w_):
        return (jnp.concatenate([_outgate_head(o_[:, s], z_[:, s], w_) for s in _heads(o_.shape[1])], axis=1),)

    return _rowcall(name, fn, [o, z], [w], [(o.shape[1], BF16)])[0]


def _outgate_bwd(o, z, w, dy, name):
    def fn(o_, z_, dy_, w_):
        do, dz, dw = [], [], None
        for s in _heads(o_.shape[1]):
            _, vjp = jax.vjp(_outgate_head, o_[:, s], z_[:, s], w_)
            a, b, c = vjp(dy_[:, s])
            do.append(a), dz.append(b)
            dw = c if dw is None else dw + c
        return jnp.concatenate(do, axis=1), jnp.concatenate(dz, axis=1), dw

    wd = o.shape[1]
    return _rowcall(name, fn, [o, z, dy], [w], [(wd, F32), (wd, BF16)], [w.shape])


def _attn_head(dot, q_h, k_h, v_h):
    s = dot(q_h, k_h, 'nt') * (q_h.shape[1] ** -0.5)
    e = jnp.exp(s - lax.stop_gradient(jnp.max(s, axis=-1, keepdims=True)))
    p = e / jnp.sum(e, axis=-1, keepdims=True)
    return dot(p, v_h, 'nn')


def _xa_slices(width):
    hd = width // XA_HEADS
    return [slice(h * hd, (h + 1) * hd) for h in range(XA_HEADS)]


def _attn_fwd(q, kv, name):
    wd = q.shape[1]

    def fn(q_, kv_):
        k_, v_ = kv_[:, :wd], kv_[:, wd:]
        return (jnp.concatenate([_attn_head(_bdot_raw, q_[:, s], k_[:, s], v_[:, s]) for s in _xa_slices(wd)],
                                axis=1),)

    return _rowcall(name, fn, [q], [kv], [(wd, BF16)])[0]


def _attn_bwd(q, kv, do, name):
    wd = q.shape[1]

    def fn(q_, do_, kv_):
        k_, v_ = kv_[:, :wd].astype(F32), kv_[:, wd:].astype(F32)
        dq, dk, dv = [], [], []
        for s in _xa_slices(wd):
            _, vjp = jax.vjp(functools.partial(_attn_head, _bdot), q_[:, s].astype(F32), k_[:, s], v_[:, s])
            a, b, c = vjp(do_[:, s])
            dq.append(a), dk.append(b), dv.append(c)
        return jnp.concatenate(dq, axis=1), jnp.concatenate(dk + dv, axis=1)

    return _rowcall(name, fn, [q, do], [kv], [(wd, BF16)], [kv.shape])


def _loss_call(y, target, name):
    d = y.shape[1]

    def fn(y_, t_):
        err = y_ - t_
        part = 0.5 * jnp.sum(jnp.sum(err * err, axis=1, keepdims=True), axis=0, keepdims=True) / d
        return err / d, jnp.broadcast_to(part, (1, LANES))

    return _rowcall(name, fn, [y, target], [], [(d, F32)], [(1, LANES)])


def _adamw_call(w, g, m, v, name):
    c = w.shape[1]

    def fn(w_, g_, m_, v_):
        g_ = g_[:, :c]
        m1 = ADAM_B1 * m_ + (1.0 - ADAM_B1) * g_
        v1 = ADAM_B2 * v_ + (1.0 - ADAM_B2) * jnp.square(g_)
        m_hat = m1 / (1.0 - ADAM_B1 ** ADAM_STEP)
        v_hat = v1 / (1.0 - ADAM_B2 ** ADAM_STEP)
        return g_, -ADAM_LR * (m_hat / (jnp.sqrt(v_hat) + ADAM_EPS) + ADAM_WD * w_), m1, v1

    return _rowcall(name, fn, [w, g, m, v], [], [(c, F32)] * 4, br=_divtile(w.shape[0], 128, 8))


STREAM_BLOCK_BYTES = 4 * 1024 * 1024


def _rows_for(cols, itemsize):
    return max(16, STREAM_BLOCK_BYTES // (cols * itemsize) // 16 * 16)


def _pair_add(grad, theirs, core, name):
    s, hr, cols = theirs.shape
    br = _divtile(hr, _rows_for(cols, 2), 16)
    nb = hr // br

    def body(c_ref, g_ref, t_ref, o_ref):
        o_ref[...] = (g_ref[...].astype(F32) + t_ref[...].astype(F32)).astype(o_ref.dtype)

    spec = pl.BlockSpec((None, br, cols), lambda j, i, c_ref: (j, i, 0))
    return _pcall(body, name=name, out_shape=jax.ShapeDtypeStruct(theirs.shape, BF16),
                  grid_spec=pltpu.PrefetchScalarGridSpec(
                      num_scalar_prefetch=1, grid=(s, nb),
                      in_specs=[pl.BlockSpec((None, br, cols), lambda j, i, c_ref: (j, c_ref[0] * nb + i, 0)), spec],
                      out_specs=spec),
                  compiler_params=_params(("parallel", "parallel")))(core, grad, theirs)


def _chip_sum(part, landed, kind, where, grad, layer, name):
    _, hr, cw = landed.shape
    br = _divtile(hr, _rows_for(cw, 4), 16)
    nb = hr // br

    def body(w_ref, p_ref, a_ref, b_ref, d_ref, g_ref, o_ref):
        o_ref[...] = ((p_ref[...].astype(F32) + a_ref[...].astype(F32)) + b_ref[...].astype(F32)) + d_ref[...].astype(F32)

    if kind == 'row':
        own = pl.BlockSpec((None, br, cw), lambda i, w_ref: (w_ref[0], i, 0))
    else:
        own = pl.BlockSpec((br, cw), lambda i, w_ref: (i, w_ref[0]))
    got = [pl.BlockSpec((None, br, cw), functools.partial(lambda i, w_ref, k: (k, i, 0), k=k)) for k in range(3)]
    return _pcall(body, name=name, out_shape=jax.ShapeDtypeStruct(grad.shape, F32),
                  grid_spec=pltpu.PrefetchScalarGridSpec(
                      num_scalar_prefetch=1, grid=(nb,), in_specs=[own] + got + [ANY_SPEC],
                      out_specs=pl.BlockSpec((None, br, cw), lambda i, w_ref: (layer, w_ref[1] * nb + i, 0))),
                  input_output_aliases={5: 0},
                  compiler_params=_params(("parallel",)))(where, part, landed, landed, landed, grad)


def _position():
    x, y, c = lax.axis_index("x"), lax.axis_index("y"), lax.axis_index("c")
    others = [(1 - x, y), (x, 1 - y), (1 - x, 1 - y)]
    return x, y, c, others


def _any_specs(n):
    return [pl.BlockSpec(memory_space=pl.ANY)] * n


def _ds(start, size):
    return pl.ds(pl.multiple_of(start, size), size)


HBM_SPEC = pl.BlockSpec(memory_space=pltpu.HBM)
SEM_SPEC = pl.BlockSpec(memory_space=pltpu.SEMAPHORE)
ANY_SPEC = pl.BlockSpec(memory_space=pl.ANY)
DATAFLOW = pltpu.SideEffectType.DATAFLOW_SIDE_EFFECTING


def _hbm(a):
    return pltpu.with_memory_space_constraint(a, pltpu.HBM)


def _full_shape(shard_shape, kind):
    r, cw = shard_shape
    return (N_CHIPS, r, cw) if kind == 'row' else (r, N_CHIPS * cw)


def _block_dims(full, kind):
    return (full.shape[1], full.shape[2]) if kind == 'row' else (full.shape[0], full.shape[1] // N_CHIPS)


def _region(full, kind, chip, half):
    if kind == 'all':
        return full.at[chip]
    r, cw = _block_dims(full, kind)
    if kind == 'row':
        return full.at[chip, _ds(half * (r // 2), r // 2), :]
    return full.at[_ds(half * (r // 2), r // 2), _ds(chip * cw, cw)]


def _place_block(full, shards, layer, kind, chip, name, after=None):
    _, r, cs = shards.shape
    cw = _block_dims(full, kind)[1]
    br = _divtile(r, 256, 16)

    def body(c_ref, s_ref, *rest):
        o_ref = rest[-1]
        if cs == cw:
            o_ref[...] = s_ref[...].astype(o_ref.dtype)
        else:
            o_ref[:, :cs] = s_ref[...].astype(o_ref.dtype)
            o_ref[:, cs:] = jnp.zeros((br, cw - cs), o_ref.dtype)

    if kind == 'row':
        out_spec = pl.BlockSpec((None, br, cw), lambda i, c_ref: (c_ref[0], i, 0))
    else:
        out_spec = pl.BlockSpec((br, cw), lambda i, c_ref: (i, c_ref[0]))
    extra = _as_list(after)
    return _pcall(body, name=name, out_shape=jax.ShapeDtypeStruct(full.shape, full.dtype),
                  grid_spec=pltpu.PrefetchScalarGridSpec(
                      num_scalar_prefetch=1, grid=(r // br,),
                      in_specs=[pl.BlockSpec((None, br, cs), lambda i, c_ref: (layer, i, 0)), ANY_SPEC]
                      + [ANY_SPEC] * len(extra), out_specs=out_spec),
                  input_output_aliases={2: 0}, compiler_params=_params(("parallel",)))(chip, shards, full, *extra)


def _split_copy_start(bufs, lands, plan, after, name):
    nb, nl = len(bufs), len(lands)
    extra = _as_list(after)

    def body(*refs):
        ins = refs[:nb + nl]
        send, recv = refs[nb + nl + len(extra)], refs[nb + nl + len(extra) + 1]
        for cp in plan(ins[:nb], ins[nb:], send, recv):
            cp.start()
        refs[-1][...] = jnp.zeros_like(refs[-1])

    n_copies = plan.n_copies
    out = _pcall(
        body, name=name,
        out_shape=(pltpu.SemaphoreType.DMA((n_copies,)), pltpu.SemaphoreType.DMA((n_copies,)),
                   *[pltpu.HBM(a.shape, a.dtype) for a in (*bufs, *lands)], jax.ShapeDtypeStruct((8, LANES), F32)),
        in_specs=[HBM_SPEC] * (nb + nl) + [ANY_SPEC] * len(extra),
        out_specs=(SEM_SPEC, SEM_SPEC, *[HBM_SPEC] * (nb + nl), pl.BlockSpec(memory_space=pltpu.VMEM)),
        input_output_aliases={i: 2 + i for i in range(nb + nl)},
        compiler_params=pltpu.CompilerParams(has_side_effects=DATAFLOW),
    )(*[_hbm(a) for a in (*bufs, *lands)], *extra)
    return out[0], out[1], list(out[2:2 + nb]), list(out[2 + nb:2 + nb + nl]), out[-1]


def _split_copy_wait(send, recv, bufs, lands, plan, after, name):
    nb, nl = len(bufs), len(lands)
    extra = _as_list(after)

    def body(*refs):
        ins = refs[:nb + nl]
        send_ref, recv_ref = refs[nb + nl], refs[nb + nl + 1]
        for cp in plan(ins[:nb], ins[nb:], send_ref, recv_ref):
            cp.wait_send()
            cp.wait_recv()

    out = _pcall(
        body, name=name, out_shape=tuple(pltpu.HBM(a.shape, a.dtype) for a in (*bufs, *lands)),
        in_specs=[HBM_SPEC] * (nb + nl) + [SEM_SPEC, SEM_SPEC] + [ANY_SPEC] * len(extra),
        out_specs=tuple([HBM_SPEC] * (nb + nl)), input_output_aliases={i: i for i in range(nb + nl)},
        compiler_params=pltpu.CompilerParams(has_side_effects=DATAFLOW),
    )(*bufs, *lands, send, recv, *extra)
    return list(out[:nb]), list(out[nb:])


def _gather_plan(kinds):
    def plan(bufs, lands, send, recv):
        x, y, c, others = _position()
        me = 2 * x + y
        return [pltpu.make_async_remote_copy(
            src_ref=_region(lands[w], kinds[w], me, c), dst_ref=_region(lands[w], kinds[w], me, c),
            send_sem=send.at[3 * w + k], recv_sem=recv.at[3 * w + k], device_id=(px, py, c), device_id_type=MESH_IDS)
            for w in range(len(lands)) for k, (px, py) in enumerate(others)]

    def wait_plan(bufs, lands, send, recv):
        x, y, c, others = _position()
        me = 2 * x + y
        return [pltpu.make_async_remote_copy(
            src_ref=_region(lands[w], kinds[w], me, c), dst_ref=_region(lands[w], kinds[w], 2 * px + py, c),
            send_sem=send.at[3 * w + k], recv_sem=recv.at[3 * w + k], device_id=(px, py, c), device_id_type=MESH_IDS)
            for w in range(len(lands)) for k, (px, py) in enumerate(others)]

    plan.n_copies = wait_plan.n_copies = 3 * len(kinds)
    return plan, wait_plan


def _gather_forward(fulls, kinds, name):
    n = len(fulls)

    def body(*refs):
        dst = refs[n:2 * n]
        send, recv = refs[2 * n:]
        x, y, c, others = _position()
        sib = (x, y, 1 - c)

        def copy(w, k, chip, half):
            reg = _region(dst[w], kinds[w], chip, half)
            return pltpu.make_async_remote_copy(src_ref=reg, dst_ref=reg, send_sem=send.at[3 * w + k],
                                                recv_sem=recv.at[3 * w + k], device_id=sib, device_id_type=MESH_IDS)

        give = [copy(w, k, 2 * px + py, c) for w in range(n) for k, (px, py) in enumerate(others)]
        for cp in give:
            cp.start()
        for w in range(n):
            for k, (px, py) in enumerate(others):
                copy(w, k, 2 * px + py, 1 - c).wait_recv()
        for cp in give:
            cp.wait_send()

    return _pcall(body, name=name, in_specs=_any_specs(n), out_specs=_any_specs(n),
                  out_shape=[jax.ShapeDtypeStruct(f.shape, f.dtype) for f in fulls],
                  input_output_aliases={i: i for i in range(n)},
                  scratch_shapes=[pltpu.SemaphoreType.DMA((3 * n,)), pltpu.SemaphoreType.DMA((3 * n,))])(*fulls)


def _grad_half(ref, kind, half):
    if kind == 'row':
        hr = ref.shape[1] // 2
        return ref.at[:, _ds(half * hr, hr), :]
    hr = ref.shape[0] // 2
    return ref.at[_ds(half * hr, hr), :]


def _half_shape(g, kind):
    return (g.shape[0], g.shape[1] // 2, g.shape[2]) if kind == 'row' else (g.shape[0] // 2, g.shape[1])


def _pair_plan(kinds):
    def plan(bufs, lands, send, recv):
        x, y, c, _ = _position()
        return [pltpu.make_async_remote_copy(src_ref=_grad_half(bufs[w], kinds[w], 1 - c), dst_ref=lands[w],
                                             send_sem=send.at[w], recv_sem=recv.at[w], device_id=(x, y, 1 - c),
                                             device_id_type=MESH_IDS) for w in range(len(bufs))]

    plan.n_copies = len(kinds)
    return plan


def _pair_exchange(grads, kinds, name):
    n = len(grads)

    def body(*refs):
        src, theirs_ref = refs[:n], refs[n:2 * n]
        send, recv = refs[2 * n:]
        x, y, c, _ = _position()
        sib = (x, y, 1 - c)

        def half(w, h):
            if kinds[w] == 'row':
                hr = src[w].shape[1] // 2
                return src[w].at[:, _ds(h * hr, hr), :]
            hr = src[w].shape[0] // 2
            return src[w].at[_ds(h * hr, hr), :]

        give = [pltpu.make_async_remote_copy(src_ref=half(w, 1 - c), dst_ref=theirs_ref[w], send_sem=send.at[w],
                                             recv_sem=recv.at[w], device_id=sib, device_id_type=MESH_IDS)
                for w in range(n)]
        for cp in give:
            cp.start()
        for cp in give:
            cp.wait()

    def half_shape(g, kind):
        return (g.shape[0], g.shape[1] // 2, g.shape[2]) if kind == 'row' else (g.shape[0] // 2, g.shape[1])

    return _pcall(body, name=name, in_specs=_any_specs(n), out_specs=_any_specs(n),
                  out_shape=[jax.ShapeDtypeStruct(half_shape(g, kd), g.dtype) for g, kd in zip(grads, kinds)],
                  scratch_shapes=[pltpu.SemaphoreType.DMA((n,)), pltpu.SemaphoreType.DMA((n,))])(*grads)


def _part_dims(part, kind):
    return (part.shape[1], part.shape[2]) if kind == 'row' else (part.shape[0], part.shape[1] // N_CHIPS)


def _chip_plan(kinds):
    def plan(bufs, lands, send, recv):
        x, y, c, others = _position()

        def slot(w, chip):
            if kinds[w] == 'row':
                return bufs[w].at[chip]
            cw = _part_dims(bufs[w], kinds[w])[1]
            return bufs[w].at[:, _ds(chip * cw, cw)]

        return [pltpu.make_async_remote_copy(src_ref=slot(w, 2 * px + py), dst_ref=lands[w].at[k],
                                             send_sem=send.at[3 * w + k], recv_sem=recv.at[3 * w + k],
                                             device_id=(px, py, c), device_id_type=MESH_IDS)
                for w in range(len(bufs)) for k, (px, py) in enumerate(others)]

    plan.n_copies = 3 * len(kinds)
    return plan


def _pair_gather_plan(n, n_layers):
    def copies(lands, send, recv, take):
        x, y, c, _ = _position()
        out = []
        for w in range(n):
            hr = lands[w].shape[1] // 2
            for l in range(n_layers):
                mine = lands[w].at[l, _ds(c * hr, hr), :]
                theirs = lands[w].at[l, _ds((1 - c) * hr, hr), :]
                out.append(pltpu.make_async_remote_copy(
                    src_ref=mine, dst_ref=theirs if take else mine, send_sem=send.at[w * n_layers + l],
                    recv_sem=recv.at[w * n_layers + l], device_id=(x, y, 1 - c), device_id_type=MESH_IDS))
        return out

    def plan(bufs, lands, send, recv):
        return copies(lands, send, recv, False)

    def wait_plan(bufs, lands, send, recv):
        return copies(lands, send, recv, True)

    plan.n_copies = wait_plan.n_copies = n * n_layers
    return plan, wait_plan


def _pair_gather(grads, name, after=None):
    n = len(grads)
    n_layers = grads[0].shape[0]
    extra = _as_list(after)

    def body(*refs):
        dst = refs[n + len(extra):2 * n + len(extra)]
        send, recv = refs[2 * n + len(extra):]
        x, y, c, _ = _position()
        sib = (x, y, 1 - c)

        def copy(w, l, half):
            hr = dst[w].shape[1] // 2
            rows = dst[w].at[l, _ds(half * hr, hr), :]
            return pltpu.make_async_remote_copy(src_ref=rows, dst_ref=rows, send_sem=send.at[w * n_layers + l],
                                                recv_sem=recv.at[w * n_layers + l], device_id=sib,
                                                device_id_type=MESH_IDS)

        give = [copy(w, l, c) for w in range(n) for l in range(n_layers)]
        for cp in give:
            cp.start()
        for w in range(n):
            for l in range(n_layers):
                copy(w, l, 1 - c).wait_recv()
        for cp in give:
            cp.wait_send()

    m = n * n_layers
    return _pcall(body, name=name, in_specs=_any_specs(n + len(extra)), out_specs=_any_specs(n),
                  out_shape=[jax.ShapeDtypeStruct(g.shape, g.dtype) for g in grads],
                  input_output_aliases={i: i for i in range(n)},
                  scratch_shapes=[pltpu.SemaphoreType.DMA((m,)), pltpu.SemaphoreType.DMA((m,))])(*grads, *extra)


def _all_gather_plan():
    flips = [(fx, fy, fc) for fx in (0, 1) for fy in (0, 1) for fc in (0, 1)][1:]

    def copies(bufs, lands, send, recv, take):
        x, y, c, _ = _position()
        out = []
        for k, f in enumerate(flips):
            px, py, pc = (1 - x if f[0] else x, 1 - y if f[1] else y, 1 - c if f[2] else c)
            slot = 4 * px + 2 * py + pc if take else 4 * x + 2 * y + c
            out.append(pltpu.make_async_remote_copy(src_ref=bufs[0], dst_ref=lands[0].at[slot], send_sem=send.at[k],
                                                    recv_sem=recv.at[k], device_id=(px, py, pc),
                                                    device_id_type=MESH_IDS))
        return out

    def plan(bufs, lands, send, recv):
        return copies(bufs, lands, send, recv, False)

    def wait_plan(bufs, lands, send, recv):
        return copies(bufs, lands, send, recv, True)

    plan.n_copies = wait_plan.n_copies = N_DEV - 1
    return plan, wait_plan


def _sum_slots(land, name):
    n, rows, cols = land.shape

    def fn(*slots):
        total = slots[0]
        for s in slots[1:]:
            total = total + s
        return (total,)

    flat = land.reshape(n * rows, cols)
    return _rowcall(name, fn, [(flat, d * rows) for d in range(n)], [], [(cols, F32)], br=_divtile(rows, 256, 8),
                    nrows=rows)[0]


def _pack(arrays):
    flat = jnp.concatenate([a.reshape(-1).astype(F32) for a in arrays])
    pad = (-flat.shape[0]) % (8 * LANES)
    return jnp.pad(flat, (0, pad)).reshape(-1, LANES)


def _unpack(buf, like):
    flat, out, off = buf.reshape(-1), [], 0
    for a in like:
        out.append(flat[off:off + a.size].reshape(a.shape))
        off += a.size
    return out


def _row2(v):
    return v.reshape(1, -1)


def _ffn_fwd(x, g_pre, w_gu, w_d, g_post, tag, after=None):
    h = _norm_fwd(x, g_pre, tag + "_pre", after)
    gu, a = _ffn_up(h, w_gu, tag + "_gu")
    f = _mm(a, w_d, 'nn', F32, tag + "_down")
    return _resnorm_fwd(x, f, g_post, 0.5, tag + "_post"), (x, h, gu, a, f)


def _ffn_bwd_weights(dx_out, saved, w_d, g_post, tag, after=None):
    x, h, gu, a, f = saved
    df, dg_post = _resnorm_bwd(f, dx_out, g_post, 0.5, tag + "_post_b", after)
    dw_d = _mm(a, df, 'tn', BF16, tag + "_down_bw")
    dgu = _ffn_down_bwd(df, w_d, gu, tag + "_down_bx")
    dw_gu = _mm(h, dgu, 'tn', BF16, tag + "_gu_bw", halves='b')
    return dgu, dw_gu, dw_d, dg_post


def _ffn_bwd_input(dx_out, saved, dgu, g_pre, w_gu, tag, after=None):
    x = saved[0]
    dh = _mm(dgu, w_gu, 'nt', F32, tag + "_gu_bx", halves='a')
    return _norm_bwd(x, dh, dx_out, g_pre, tag + "_pre_b", after)


def _ffn_bwd(dx_out, saved, g_pre, w_gu, w_d, g_post, tag, after=None):
    dgu, dw_gu, dw_d, dg_post = _ffn_bwd_weights(dx_out, saved, w_d, g_post, tag, after)
    dx, dg_pre = _ffn_bwd_input(dx_out, saved, dgu, g_pre, w_gu, tag)
    return dx, dg_pre, dw_gu, dw_d, dg_post


def _split_cols(pp, cs, cp, widths):
    proj = jnp.concatenate([pp[:, j * cp:j * cp + cs] for j in range(N_CHIPS)], axis=1)
    out, off = [], 0
    for wd in widths:
        out.append(proj[:, off:off + wd])
        off += wd
    return out


def _join_cols(pieces, cs, cp):
    proj = jnp.concatenate([pc.astype(BF16) for pc in pieces], axis=1)
    t = proj.shape[0]
    cols = []
    for j in range(N_CHIPS):
        cols += [proj[:, j * cs:(j + 1) * cs], jnp.zeros((t, cp - cs), proj.dtype)]
    return jnp.concatenate(cols, axis=1)


def _mix_fwd(x, p, w_in, w_out, conv_w, cs, cp, tag, after=None):
    t, d = x.shape
    half = d // 2
    nh = half // HEAD_DIM
    h = _norm_fwd(x, p['mix_norm_pre'], tag + "_pre", after)
    pp = _mm(h, w_in, 'nn', F32, tag + "_in")
    u_a, v_a, qkv_raw, z, b_raw, a_raw = _split_cols(pp, cs, cp, [half, half, 3 * half, half, nh, nh])
    y_a = _sgu_fwd(u_a, v_a, p['sm_w'], p['sm_b'], p['sm_ln_g'], p['sm_ln_b'], tag + "_sgu")
    qkv = _conv_fwd(qkv_raw, conv_w, tag + "_conv")
    b_raw, a_raw = b_raw.T, a_raw.T
    beta, gcum = _gates_fwd(b_raw, a_raw, p['a_log'].T, p['dt_bias'].T, tag + "_gates")
    gcol, bcol = gcum[:, :, None], beta[:, :, None]
    grow = gcum.reshape(nh, t // DN_CHUNK, 1, DN_CHUNK)
    o, states = _delta_fwd(qkv, gcol, grow, bcol, tag + "_delta")
    y_b = _outgate_fwd(o, z, p['dn_norm_w'], tag + "_gate")
    y = jnp.concatenate([y_a, y_b], axis=1)
    m = _mm(y, w_out, 'nn', F32, tag + "_out")
    x_out = _resnorm_fwd(x, m, p['mix_norm_post'], 1.0, tag + "_post")
    return x_out, (x, h, u_a, v_a, qkv_raw, z, b_raw, a_raw, qkv, gcol, grow, bcol, states, o, y, m)


def _mix_bwd(dx_out, saved, p, w_in, w_out, conv_w, cs, cp, tag, after=None):
    x, h, u_a, v_a, qkv_raw, z, b_raw, a_raw, qkv, gcol, grow, bcol, states, o, y, m = saved
    t, d = x.shape
    half = d // 2
    nh = half // HEAD_DIM
    gr = {}
    dm, gr['mix_norm_post'] = _resnorm_bwd(m, dx_out, p['mix_norm_post'], 1.0, tag + "_post_b", after)
    dy = _mm(dm, w_out, 'nt', F32, tag + "_out_bx")
    gr['w_out'] = _mm(y, dm, 'tn', BF16, tag + "_out_bw")
    dy_a, dy_b = dy[:, :half], dy[:, half:]
    do, dz, gr['dn_norm_w'] = _outgate_bwd(o, z, p['dn_norm_w'], dy_b, tag + "_gate_b")
    dq, dk, dv, dgcol, dgrow, dbcol = _delta_bwd(qkv, gcol, grow, bcol, states, do, tag + "_delta_b")
    dgcum = dgcol[:, :, 0] + dgrow.reshape(nh, t)
    db_raw, da_raw, gr['a_log'], gr['dt_bias'] = _gates_bwd(b_raw, a_raw, p['a_log'].T, p['dt_bias'].T,
                                                             dbcol[:, :, 0], dgcum, tag + "_gates_b")
    db_raw, da_raw = db_raw.T, da_raw.T
    thirds = [_conv_bwd(qkv_raw, conv_w, dpart, i * half, tag + "_conv_b") for i, dpart in enumerate((dq, dk, dv))]
    gr['conv_w'] = jnp.concatenate([dw for _, dw in thirds], axis=1)
    du_a, dv_a, gr['sm_w'], gr['sm_b'], gr['sm_ln_g'], gr['sm_ln_b'] = _sgu_bwd(
        u_a, v_a, p['sm_w'], p['sm_b'], p['sm_ln_g'], p['sm_ln_b'], dy_a, tag + "_sgu_b")
    dpp = _join_cols([du_a, dv_a, *[dx for dx, _ in thirds], dz, db_raw, da_raw], cs, cp).astype(BF16)
    dh = _mm(dpp, w_in, 'nt', F32, tag + "_in_bx")
    gr['w_in'] = _mm(h, dpp, 'tn', BF16, tag + "_in_bw")
    dx, gr['mix_norm_pre'] = _norm_bwd(x, dh, dx_out, p['mix_norm_pre'], tag + "_pre_b")
    return dx, gr


def _xa_fwd(x, mem, p, w_xq, w_xkv, w_xo, tag, after=None):
    h = _norm_fwd(x, p['xa_norm_pre'], tag + "_pre", after)
    mh = _norm_fwd(mem, p['mem_norm'], tag + "_mem")
    q = _mm(h, w_xq, 'nn', BF16, tag + "_q")
    kv = _mm(mh, w_xkv, 'nn', BF16, tag + "_kv")
    o = _attn_fwd(q, kv, tag + "_attn")
    c = _mm(o, w_xo, 'nn', F32, tag + "_o")
    return _resnorm_fwd(x, c, p['xa_norm_post'], 1.0, tag + "_post"), (x, h, mh, q, kv, o, c)


def _xa_bwd(dx_out, saved, mem, p, w_xq, w_xkv, w_xo, tag, after=None):
    x, h, mh, q, kv, o, c = saved
    gr = {}
    dc, gr['xa_norm_post'] = _resnorm_bwd(c, dx_out, p['xa_norm_post'], 1.0, tag + "_post_b", after)
    do = _mm(dc, w_xo, 'nt', F32, tag + "_o_bx")
    gr['w_xo'] = _mm(o, dc, 'tn', BF16, tag + "_o_bw")
    dq, dkv = _attn_bwd(q, kv, do, tag + "_attn_b")
    dkv = dkv.astype(BF16)
    dh = _mm(dq, w_xq, 'nt', F32, tag + "_q_bx")
    gr['w_xq'] = _mm(h, dq, 'tn', BF16, tag + "_q_bw")
    dmh = _mm(dkv, w_xkv, 'nt', F32, tag + "_kv_bx")
    gr['w_xkv'] = _mm(mh, dkv, 'tn', BF16, tag + "_kv_bw")
    gr['mem_norm'] = _norm_bwd_gain(mem, dmh, p['mem_norm'], tag + "_mem_b")
    dx, gr['xa_norm_pre'] = _norm_bwd(x, dh, dx_out, p['xa_norm_pre'], tag + "_pre_b")
    return dx, gr


def _step(inputs):
    x, mem, target = inputs['x'][0], inputs['mem'][0], inputs['loss_target'][0]
    n_layers = inputs['w_in'].shape[0]
    cx, cy, cc = lax.axis_index("x"), lax.axis_index("y"), lax.axis_index("c")
    chip = 2 * cx + cy
    cs = inputs['w_in'].shape[2]
    cp = -(-cs // LANES) * LANES
    kinds = [BIG[nm] for nm in BIG_NAMES]

    conv_local = inputs['conv_w']
    ccols = conv_local.shape[2]

    core_idx, chip_idx = cc.astype(jnp.int32).reshape(1), chip.astype(jnp.int32).reshape(1)
    where_idx = jnp.stack([chip, cc]).astype(jnp.int32)
    order = [(l, gi) for l in range(n_layers) for gi in range(len(GROUPS))]
    n_groups = len(order)

    def small(l):
        p = {nm: inputs[nm][l] for nm in SMALL_NAMES}
        for nm in SMALL_NAMES:
            if p[nm].ndim == 1:
                p[nm] = _row2(p[nm])
        p['sm_b'] = p['sm_b'][:, :, None]
        return p

    def place(k, after=None):
        l, gi = order[k]
        lands = []
        for nm in GROUPS[gi][1]:
            _, r, c_in = inputs[nm].shape
            shape = _full_shape((r, cp if nm == 'w_in' else c_in), BIG[nm])
            lands.append(_place_block(lax.empty(shape, BF16), w_src[nm], l, BIG[nm], chip_idx, "place_block", after))
        return lands

    w_src = {nm: inputs[nm] for nm in BIG_NAMES}

    placed = {}
    conv_group = 1

    def gather_start(k, after):
        names = list(GROUPS[order[k][1]][1])
        kds = [BIG[nm] for nm in names]
        if k not in placed:
            placed[k] = place(k)
        lands = list(placed[k])
        if k == conv_group:
            mine = conv_local.reshape(1, n_layers * CONV_WIDTH, ccols)
            lands.append(lax.dynamic_update_slice(jnp.zeros((N_CHIPS, *mine.shape[1:]), F32), mine, (chip, 0, 0)))
            names.append('conv_w'), kds.append('all')
        plan, wait_plan = _gather_plan(kds)
        send, recv, _, lands, token = _split_copy_start([], lands, plan, after, f"gather_start_{k}")
        return dict(send=send, recv=recv, lands=lands, token=token, kinds=kds, names=names, wait_plan=wait_plan)

    def gather_finish(k, st, after):
        _, lands = _split_copy_wait(st['send'], st['recv'], [], st['lands'], st['wait_plan'], after, f"gather_wait_{k}")
        got = dict(zip(st['names'], lands))
        halves = [nm for nm, kd in zip(st['names'], st['kinds']) if kd != 'all']
        passed = _gather_forward([got[nm] for nm in halves], [BIG[nm] for nm in halves],
                                 "gather_forward_" + GROUPS[order[k][1]][0])
        w = {nm: (g.reshape(-1, g.shape[2]) if BIG[nm] == 'row' else g) for nm, g in zip(halves, passed)}
        if 'conv_w' in got:
            w['conv_w'] = got['conv_w'].reshape(N_CHIPS, n_layers, CONV_WIDTH, ccols).transpose(1, 2, 0, 3).reshape(
                n_layers, CONV_WIDTH, N_CHIPS * ccols)
        return w

    def group_fwd(gi, x_, p, w, l, token):
        if gi == 0:
            return _ffn_fwd(x_, p['ffn1_norm_pre'], w['ffn1_w_gate_up'], w['ffn1_w_down'], p['ffn1_norm_post'], "ffn", token)
        if gi == 1:
            return _mix_fwd(x_, p, w['w_in'], w['w_out'], conv_full[l], cs, cp, "mix", token)
        if gi == 2:
            return _xa_fwd(x_, mem, p, w['w_xq'], w['w_xkv'], w['w_xo'], "xa", token)
        return _ffn_fwd(x_, p['ffn2_norm_pre'], w['ffn2_w_gate_up'], w['ffn2_w_down'], p['ffn2_norm_post'], "ffn", token)

    def group_bwd(gi, dx_, s, p, w, l, token):
        if gi in (0, 3):
            pre = 'ffn1' if gi == 0 else 'ffn2'
            dx_, g_pre, g_gu, g_d, g_post = _ffn_bwd(dx_, s, p[pre + '_norm_pre'], w[pre + '_w_gate_up'],
                                                     w[pre + '_w_down'], p[pre + '_norm_post'], "ffn", token)
            return dx_, {pre + '_norm_pre': g_pre, pre + '_w_gate_up': g_gu, pre + '_w_down': g_d,
                         pre + '_norm_post': g_post}
        if gi == 1:
            return _mix_bwd(dx_, s, p, w['w_in'], w['w_out'], conv_full[l], cs, cp, "mix", token)
        return _xa_bwd(dx_, s, mem, p, w['w_xq'], w['w_xkv'], w['w_xo'], "xa", token)

    started = {0: gather_start(0, None)}
    w_in_f32 = inputs['w_in'] + started[0]['token'][0, 0]
    w_src['w_in'] = w_in_f32.astype(BF16)
    started[1] = gather_start(1, started[0]['token'])
    weights, saved = [], []
    for k in range(2, n_groups):
        placed[k] = place(k, started[1]['token'])
    head_work = [started[1]['token']] + [land for k in range(2, n_groups) for land in placed[k]]
    head_work += [w_in_f32] + [inputs[pre + 'w_in'].reshape(-1, cs) for pre in ('m_', 'v_')]
    conv_full = None
    for k, (l, gi) in enumerate(order):
        w = gather_finish(k, started[k], x if k > 0 else head_work)
        if k == conv_group:
            conv_full = w.pop('conv_w')
        token = None
        if k + 2 < n_groups:
            started[k + 2] = gather_start(k + 2, next(iter(w.values())))
            token = started[k + 2]['token']
        weights.append(w)
        x, s = group_fwd(gi, x, small(l), w, l, token)
        saved.append(s)

    dx, loss_part = _loss_call(x, target, "loss")

    grads = [dict() for _ in range(n_layers)]
    big_grads = {nm: None for nm in BIG_NAMES}
    pair_gathers = {}

    def reduce_finish(pd, after):
        parts, lands = _split_copy_wait(pd['send'], pd['recv'], pd['parts'], pd['lands'], pd['plan'], after,
                                        f"chip_wait_{pd['k']}")
        for nm, kd, part, landed in zip(pd['names'], pd['kinds'], parts, lands):
            if big_grads[nm] is None:
                big_grads[nm] = lax.empty((n_layers, 2 * landed.shape[1], landed.shape[2]), F32)
            big_grads[nm] = _chip_sum(part, landed, kd, where_idx, big_grads[nm], pd['l'], "chip_sum")
        if pd['l'] == 0 and pd['k'] > 0:
            names = pd['names']
            plan, wait_plan = _pair_gather_plan(len(names), n_layers)
            send, recv, _, got, _ = _split_copy_start([], [big_grads[nm] for nm in names], plan, None,
                                                      f"pair_gather_start_{pd['k']}")
            pair_gathers[pd['k']] = dict(send=send, recv=recv, lands=got, wait_plan=wait_plan)

    def by_chip(gr, names, kds):
        return [gr[nm].reshape(N_CHIPS, -1, gr[nm].shape[1]) if kd == 'row' else gr[nm] for nm, kd in zip(names, kds)]

    def chip_start(k, names, kds, gs, theirs):
        parts = [_pair_add(g, t, core_idx, "pair_add") if kd == 'row'
                 else _pair_add(g[None], t[None], core_idx, "pair_add")[0] for g, t, kd in zip(gs, theirs, kds)]
        lands = [lax.empty((3, *_part_dims(part, kd)), BF16) for part, kd in zip(parts, kds)]
        plan = _chip_plan(kds)
        send, recv, parts, lands, token = _split_copy_start(parts, lands, plan, None, f"chip_start_{k}")
        return dict(send=send, recv=recv, parts=parts, lands=lands, token=token, plan=plan, names=names, kinds=kds,
                    k=k, l=order[k][0])

    pair_pending, chip_queue = None, []
    for k in reversed(range(n_groups)):
        l, gi = order[k]
        names = GROUPS[gi][1]
        kds = [BIG[nm] for nm in names]
        p, w = small(l), weights[k]
        tokens = [pd['token'] for pd in ([pair_pending] if pair_pending else []) + chip_queue] or None
        if k > 0:
            dx_next, gr = group_bwd(gi, dx, saved[k], p, w, l, tokens)
        else:
            dgu, g_gu, g_d, g_post = _ffn_bwd_weights(dx, saved[k], w['ffn1_w_down'], p['ffn1_norm_post'], "ffn", tokens)
            gr = {'ffn1_w_gate_up': g_gu, 'ffn1_w_down': g_d, 'ffn1_norm_post': g_post}
            dx_next = g_gu
        for pd in chip_queue:
            reduce_finish(pd, dx_next)
        chip_queue = []
        if pair_pending:
            pd = pair_pending
            gs_prev, theirs = _split_copy_wait(pd['send'], pd['recv'], pd['gs'], pd['lands'], pd['plan'], dx_next,
                                               f"pair_wait_{pd['k']}")
            chip_queue.append(chip_start(pd['k'], pd['names'], pd['kinds'], gs_prev, theirs))
            pair_pending = None
        gs = by_chip(gr, names, kds)
        if k > 1:
            plan = _pair_plan(kds)
            lands = [lax.empty(_half_shape(g, kd), BF16) for g, kd in zip(gs, kds)]
            send, recv, gs, lands, token = _split_copy_start(gs, lands, plan, None, f"pair_start_{k}")
            pair_pending = dict(send=send, recv=recv, gs=gs, lands=lands, token=token, plan=plan, names=names,
                                kinds=kds, k=k)
        else:
            started_now = chip_start(k, names, kds, gs, _pair_exchange(gs, kds, "pair_exchange_" + GROUPS[gi][0]))
            if k > 0:
                chip_queue.append(started_now)
        if k > 0:
            dx = dx_next
        else:
            pending = started_now
            dx, gr['ffn1_norm_pre'] = _ffn_bwd_input(dx, saved[k], dgu, p['ffn1_norm_pre'], w['ffn1_w_gate_up'], "ffn",
                                                     pending['token'])
        grads[l].update({nm: g for nm, g in gr.items() if nm not in BIG})
    for pd in chip_queue:
        reduce_finish(pd, dx)
    out = {}

    def two(a):
        return a.reshape(-1, a.shape[-1])

    def adamw(nm, g):
        w = w_in_f32 if nm == 'w_in' else inputs[nm]
        res = _adamw_call(two(w), two(g), two(inputs['m_' + nm]), two(inputs['v_' + nm]), "adamw")
        out[nm] = tuple(a.reshape(inputs[nm].shape) for a in res)
        return res[1]

    def finish_weights(gi, after=None):
        names = GROUPS[gi][1]
        if gi in pair_gathers:
            st = pair_gathers[gi]
            _, got = _split_copy_wait(st['send'], st['recv'], [], st['lands'], st['wait_plan'], after,
                                      f"pair_gather_wait_{gi}")
        else:
            got = _pair_gather([big_grads[nm] for nm in names], "pair_gather_" + GROUPS[gi][0], after)
        return [adamw(nm, g) for nm, g in zip(names, got)]

    small_grads = [jnp.stack([grads[l][nm].reshape(inputs[nm].shape[1:]) if nm != 'conv_w' else grads[l][nm]
                              for l in range(n_layers)]) for nm in SMALL_NAMES]
    packed = _pack(small_grads + [loss_part])
    me = 4 * cx + 2 * cy + cc
    land = lax.dynamic_update_slice(lax.empty((N_DEV, *packed.shape), F32), packed[None], (me, 0, 0))
    plan, wait_plan = _all_gather_plan()
    send, recv, bufs, lands, _ = _split_copy_start([packed], [land], plan, pending['token'], "small_start")

    done = [d for gi in range(len(GROUPS) - 1, 0, -1) for d in finish_weights(gi, pending['token'])]
    reduce_finish(pending, done)
    done = finish_weights(0)

    _, lands = _split_copy_wait(send, recv, bufs, lands, wait_plan, done, "small_wait")
    summed = _unpack(_sum_slots(lands[0], "small_sum"), small_grads + [loss_part])
    loss = summed[-1][0, 0]
    small_g = dict(zip(SMALL_NAMES, summed[:-1]))
    small_g['conv_w'] = lax.dynamic_slice(small_g['conv_w'], (0, 0, chip * ccols), (n_layers, CONV_WIDTH, ccols))
    for nm in SMALL_NAMES:
        adamw(nm, small_g[nm])
    return (loss, dx[None], *[out[nm][0] for nm in WEIGHTS], *[out[nm][1] for nm in WEIGHTS],
            *[out[nm][2] for nm in WEIGHTS], *[out[nm][3] for nm in WEIGHTS])


def kernel(x, mem, ffn1_norm_pre, ffn1_w_gate_up, ffn1_w_down, ffn1_norm_post, mix_norm_pre, w_in, conv_w, a_log, dt_bias, sm_w, sm_b, sm_ln_g, sm_ln_b, dn_norm_w, w_out, mix_norm_post, xa_norm_pre, mem_norm, w_xq, w_xkv, w_xo, xa_norm_post, ffn2_norm_pre, ffn2_w_gate_up, ffn2_w_down, ffn2_norm_post, loss_target, m_ffn1_norm_pre, m_ffn1_w_gate_up, m_ffn1_w_down, m_ffn1_norm_post, m_mix_norm_pre, m_w_in, m_conv_w, m_a_log, m_dt_bias, m_sm_w, m_sm_b, m_sm_ln_g, m_sm_ln_b, m_dn_norm_w, m_w_out, m_mix_norm_post, m_xa_norm_pre, m_mem_norm, m_w_xq, m_w_xkv, m_w_xo, m_xa_norm_post, m_ffn2_norm_pre, m_ffn2_w_gate_up, m_ffn2_w_down, m_ffn2_norm_post, v_ffn1_norm_pre, v_ffn1_w_gate_up, v_ffn1_w_down, v_ffn1_norm_post, v_mix_norm_pre, v_w_in, v_conv_w, v_a_log, v_dt_bias, v_sm_w, v_sm_b, v_sm_ln_g, v_sm_ln_b, v_dn_norm_w, v_w_out, v_mix_norm_post, v_xa_norm_pre, v_mem_norm, v_w_xq, v_w_xkv, v_w_xo, v_xa_norm_post, v_ffn2_norm_pre, v_ffn2_w_gate_up, v_ffn2_w_down, v_ffn2_norm_post):
    vals = dict(locals())
    return _step(vals)
```
